```python
import math
import jax, jax.numpy as jnp
from jax import lax
import numpy as np

D_MODEL = 1024
BATCH = 8
SEQ = 4096
DEPTH = 1

RET_HEADS = 8
RET_DK = 128
RET_DV = 256
RET_CHUNK = 128
ROPE_BASE = 10000.0
CONV_CH = D_MODEL
CONV_WIDTH = 31
D_FF = 2816
LN_EPS = 1e-5

QK_W = RET_HEADS * RET_DK
V_W = RET_HEADS * RET_DV
SPLIT_POINTS = tuple(int(p) for p in np.cumsum([QK_W, QK_W, V_W, V_W, CONV_CH, CONV_CH, D_MODEL]))
IN_W = 2 * QK_W + 2 * V_W + 2 * CONV_CH + 2 * D_MODEL

kernel_name = "hybrid_retention_conformer_macaron_deepnorm"


def layer_norm(x, g, b):
    xf = x.astype(jnp.float32)
    mu = jnp.mean(xf, axis=-1, keepdims=True)
    var = jnp.mean(jnp.square(xf - mu), axis=-1, keepdims=True)
    y = (xf - mu) * lax.rsqrt(var + LN_EPS) * g.astype(jnp.float32) + b.astype(jnp.float32)
    return y.astype(x.dtype)


def swiglu_ffn(h, w_gate, w_up, w_down):
    return (jax.nn.silu(h @ w_gate) * (h @ w_up)) @ w_down


def rotary(x, cos, sin):
    half = x.shape[-1] // 2
    x1, x2 = x[..., :half], x[..., half:]
    return jnp.concatenate([x1 * cos - x2 * sin, x2 * cos + x1 * sin], axis=-1).astype(x.dtype)


def retention_chunkwise(q, k, v):
    b, s, h, dk = q.shape
    dv = v.shape[-1]
    c = RET_CHUNK
    n = s // c
    log_g = jnp.log(1.0 - jnp.exp2(-5.0 - jnp.arange(h, dtype=jnp.float32)))
    idx = jnp.arange(c, dtype=jnp.float32)
    diff = idx[:, None] - idx[None, :]
    decay_mask = jnp.where(diff[None] >= 0,
                           jnp.exp(jnp.maximum(diff, 0.0)[None] * log_g[:, None, None]), 0.0)
    qc = q.reshape(b, n, c, h, dk)
    kc = k.reshape(b, n, c, h, dk)
    vc = v.reshape(b, n, c, h, dv)
    scores = jnp.einsum('bnihd,bnjhd->bnhij', qc, kc) * decay_mask
    intra = jnp.einsum('bnhij,bnjhe->bnihe', scores, vc)
    xi = jnp.exp((idx[:, None] + 1.0) * log_g[None, :])
    zeta = jnp.exp((c - 1.0 - idx)[:, None] * log_g[None, :])
    chunk_decay = jnp.exp(c * log_g)

    def step(state, xs):
        qn, kn, vn = xs
        cross = jnp.einsum('bihd,bhde->bihe', qn, state) * xi[None, :, :, None]
        new_state = chunk_decay[None, :, None, None] * state + jnp.einsum(
            'bjhd,bjhe->bhde', kn * zeta[None, :, :, None], vn)
        return new_state, cross

    state0 = jnp.zeros((b, h, dk, dv), jnp.float32)
    _, cross = lax.scan(step, state0, (jnp.moveaxis(qc, 1, 0), jnp.moveaxis(kc, 1, 0), jnp.moveaxis(vc, 1, 0)))
    cross = jnp.moveaxis(cross, 0, 1)
    return (intra + cross).reshape(b, s, h, dv)


def hybrid_mixer(h, w_in, b_in, ret_gn_g, conv_k, conv_b, conv_ln_g, conv_ln_b,
                 w_ret_o, w_conv_o, w_out, cos, sin):
    bsz, s, _ = h.shape
    proj = h @ w_in + b_in
    q, k, v, g, glu_a, glu_b, gate_r, gate_c = jnp.split(proj, SPLIT_POINTS, axis=-1)

    q = rotary(q.reshape(bsz, s, RET_HEADS, RET_DK), cos, sin) * (RET_DK ** -0.5)
    k = rotary(k.reshape(bsz, s, RET_HEADS, RET_DK), cos, sin)
    v = v.reshape(bsz, s, RET_HEADS, RET_DV)
    r = retention_chunkwise(q, k, v)
    mu = jnp.mean(r, axis=-1, keepdims=True)
    var = jnp.mean(jnp.square(r - mu), axis=-1, keepdims=True)
    r = ((r - mu) * lax.rsqrt(var + LN_EPS)).reshape(bsz, s, V_W) * ret_gn_g.astype(jnp.float32)
    ret_out = (jax.nn.silu(g) * r.astype(h.dtype)) @ w_ret_o

    u = glu_a * jax.nn.sigmoid(glu_b)
    u = lax.conv_general_dilated(u, conv_k, window_strides=(1,), padding=[(CONV_WIDTH - 1, 0)],
                                 dimension_numbers=('NWC', 'WIO', 'NWC'),
                                 feature_group_count=CONV_CH) + conv_b
    u = jax.nn.silu(layer_norm(u, conv_ln_g, conv_ln_b))
    conv_out = u @ w_conv_o

    merged = jax.nn.sigmoid(gate_r) * ret_out + jax.nn.sigmoid(gate_c) * conv_out
    return merged @ w_out


def _fwd_setup_inputs(seed: int = 0) -> dict:
    key = jax.random.key(seed)
    ks = jax.random.split(key, 24)
    beta = (8.0 * DEPTH) ** -0.25
    L = DEPTH

    def nrm(k, shape, scale):
        return jax.random.normal(k, shape, jnp.float32) * scale

    def gain(k, shape):
        return 1.0 + 0.02 * jax.random.normal(k, shape, jnp.float32)

    return {
        "x": jax.random.normal(ks[0], (BATCH, SEQ, D_MODEL), jnp.float32),
        "ffn1_w_gate": nrm(ks[1], (L, D_MODEL, D_FF), D_MODEL ** -0.5),
        "ffn1_w_up": nrm(ks[2], (L, D_MODEL, D_FF), D_MODEL ** -0.5),
        "ffn1_w_down": nrm(ks[3], (L, D_FF, D_MODEL), beta * D_FF ** -0.5),
        "ln1_g": gain(ks[4], (L, D_MODEL)),
        "ln1_b": nrm(ks[5], (L, D_MODEL), 0.02),
        "w_in": nrm(ks[6], (L, D_MODEL, IN_W), D_MODEL ** -0.5),
        "b_in": nrm(ks[7], (L, IN_W), 0.02),
        "ret_gn_g": gain(ks[8], (L, V_W)),
        "conv_k": nrm(ks[9], (L, CONV_WIDTH, 1, CONV_CH), CONV_WIDTH ** -0.5),
        "conv_b": nrm(ks[10], (L, CONV_CH), 0.02),
        "conv_ln_g": gain(ks[11], (L, CONV_CH)),
        "conv_ln_b": nrm(ks[12], (L, CONV_CH), 0.02),
        "w_ret_o": nrm(ks[13], (L, V_W, D_MODEL), beta * V_W ** -0.5),
        "w_conv_o": nrm(ks[14], (L, CONV_CH, D_MODEL), beta * CONV_CH ** -0.5),
        "w_out": nrm(ks[15], (L, D_MODEL, D_MODEL), beta * D_MODEL ** -0.5),
        "ln2_g": gain(ks[16], (L, D_MODEL)),
        "ln2_b": nrm(ks[17], (L, D_MODEL), 0.02),
        "ffn2_w_gate": nrm(ks[18], (L, D_MODEL, D_FF), D_MODEL ** -0.5),
        "ffn2_w_up": nrm(ks[19], (L, D_MODEL, D_FF), D_MODEL ** -0.5),
        "ffn2_w_down": nrm(ks[20], (L, D_FF, D_MODEL), beta * D_FF ** -0.5),
        "ln3_g": gain(ks[21], (L, D_MODEL)),
        "ln3_b": nrm(ks[22], (L, D_MODEL), 0.02),
    }


def _fwd_reference(x, ffn1_w_gate, ffn1_w_up, ffn1_w_down, ln1_g, ln1_b, w_in, b_in, ret_gn_g,
              conv_k, conv_b, conv_ln_g, conv_ln_b, w_ret_o, w_conv_o, w_out, ln2_g, ln2_b,
              ffn2_w_gate, ffn2_w_up, ffn2_w_down, ln3_g, ln3_b):
    alpha = (2.0 * DEPTH) ** 0.25
    s = x.shape[1]
    half = RET_DK // 2
    freqs = ROPE_BASE ** (-jnp.arange(half, dtype=jnp.float32) / half)
    ang = jnp.arange(s, dtype=jnp.float32)[:, None] * freqs[None, :]
    cos = jnp.cos(ang)[:, None, :]
    sin = jnp.sin(ang)[:, None, :]

    for l in range(DEPTH):
        x = layer_norm(alpha * x + 0.5 * swiglu_ffn(x, ffn1_w_gate[l], ffn1_w_up[l], ffn1_w_down[l]),
                       ln1_g[l], ln1_b[l])
        m = hybrid_mixer(x, w_in[l], b_in[l], ret_gn_g[l], conv_k[l], conv_b[l], conv_ln_g[l],
                         conv_ln_b[l], w_ret_o[l], w_conv_o[l], w_out[l], cos, sin)
        x = layer_norm(alpha * x + m, ln2_g[l], ln2_b[l])
        x = layer_norm(alpha * x + 0.5 * swiglu_ffn(x, ffn2_w_gate[l], ffn2_w_up[l], ffn2_w_down[l]),
                       ln3_g[l], ln3_b[l])
    return x


import jax as _jax
import jax.numpy as _jnp

TWIN_FORMAT = 'train_step'
FWD_PARAMS = ['x', 'ffn1_w_gate', 'ffn1_w_up', 'ffn1_w_down', 'ln1_g', 'ln1_b', 'w_in', 'b_in', 'ret_gn_g', 'conv_k', 'conv_b', 'conv_ln_g', 'conv_ln_b', 'w_ret_o', 'w_conv_o', 'w_out', 'ln2_g', 'ln2_b', 'ffn2_w_gate', 'ffn2_w_up', 'ffn2_w_down', 'ln3_g', 'ln3_b']
TWIN_WEIGHTS = ['ffn1_w_gate', 'ffn1_w_up', 'ffn1_w_down', 'ln1_g', 'ln1_b', 'w_in', 'b_in', 'ret_gn_g', 'conv_k', 'conv_b', 'conv_ln_g', 'conv_ln_b', 'w_ret_o', 'w_conv_o', 'w_out', 'ln2_g', 'ln2_b', 'ffn2_w_gate', 'ffn2_w_up', 'ffn2_w_down', 'ln3_g', 'ln3_b']
TWIN_DIFF_INPUT = 'x'
TWIN_INPUTS = ['x', 'ffn1_w_gate', 'ffn1_w_up', 'ffn1_w_down', 'ln1_g', 'ln1_b', 'w_in', 'b_in', 'ret_gn_g', 'conv_k', 'conv_b', 'conv_ln_g', 'conv_ln_b', 'w_ret_o', 'w_conv_o', 'w_out', 'ln2_g', 'ln2_b', 'ffn2_w_gate', 'ffn2_w_up', 'ffn2_w_down', 'ln3_g', 'ln3_b', 'loss_target', 'm_ffn1_w_gate', 'm_ffn1_w_up', 'm_ffn1_w_down', 'm_ln1_g', 'm_ln1_b', 'm_w_in', 'm_b_in', 'm_ret_gn_g', 'm_conv_k', 'm_conv_b', 'm_conv_ln_g', 'm_conv_ln_b', 'm_w_ret_o', 'm_w_conv_o', 'm_w_out', 'm_ln2_g', 'm_ln2_b', 'm_ffn2_w_gate', 'm_ffn2_w_up', 'm_ffn2_w_down', 'm_ln3_g', 'm_ln3_b', 'v_ffn1_w_gate', 'v_ffn1_w_up', 'v_ffn1_w_down', 'v_ln1_g', 'v_ln1_b', 'v_w_in', 'v_b_in', 'v_ret_gn_g', 'v_conv_k', 'v_conv_b', 'v_conv_ln_g', 'v_conv_ln_b', 'v_w_ret_o', 'v_w_conv_o', 'v_w_out', 'v_ln2_g', 'v_ln2_b', 'v_ffn2_w_gate', 'v_ffn2_w_up', 'v_ffn2_w_down', 'v_ln3_g', 'v_ln3_b']
TWIN_OUTPUTS = ['loss', 'grad_x', 'grad_ffn1_w_gate', 'grad_ffn1_w_up', 'grad_ffn1_w_down', 'grad_ln1_g', 'grad_ln1_b', 'grad_w_in', 'grad_b_in', 'grad_ret_gn_g', 'grad_conv_k', 'grad_conv_b', 'grad_conv_ln_g', 'grad_conv_ln_b', 'grad_w_ret_o', 'grad_w_conv_o', 'grad_w_out', 'grad_ln2_g', 'grad_ln2_b', 'grad_ffn2_w_gate', 'grad_ffn2_w_up', 'grad_ffn2_w_down', 'grad_ln3_g', 'grad_ln3_b', 'delta_ffn1_w_gate', 'delta_ffn1_w_up', 'delta_ffn1_w_down', 'delta_ln1_g', 'delta_ln1_b', 'delta_w_in', 'delta_b_in', 'delta_ret_gn_g', 'delta_conv_k', 'delta_conv_b', 'delta_conv_ln_g', 'delta_conv_ln_b', 'delta_w_ret_o', 'delta_w_conv_o', 'delta_w_out', 'delta_ln2_g', 'delta_ln2_b', 'delta_ffn2_w_gate', 'delta_ffn2_w_up', 'delta_ffn2_w_down', 'delta_ln3_g', 'delta_ln3_b', 'new_m_ffn1_w_gate', 'new_m_ffn1_w_up', 'new_m_ffn1_w_down', 'new_m_ln1_g', 'new_m_ln1_b', 'new_m_w_in', 'new_m_b_in', 'new_m_ret_gn_g', 'new_m_conv_k', 'new_m_conv_b', 'new_m_conv_ln_g', 'new_m_conv_ln_b', 'new_m_w_ret_o', 'new_m_w_conv_o', 'new_m_w_out', 'new_m_ln2_g', 'new_m_ln2_b', 'new_m_ffn2_w_gate', 'new_m_ffn2_w_up', 'new_m_ffn2_w_down', 'new_m_ln3_g', 'new_m_ln3_b', 'new_v_ffn1_w_gate', 'new_v_ffn1_w_up', 'new_v_ffn1_w_down', 'new_v_ln1_g', 'new_v_ln1_b', 'new_v_w_in', 'new_v_b_in', 'new_v_ret_gn_g', 'new_v_conv_k', 'new_v_conv_b', 'new_v_conv_ln_g', 'new_v_conv_ln_b', 'new_v_w_ret_o', 'new_v_w_conv_o', 'new_v_w_out', 'new_v_ln2_g', 'new_v_ln2_b', 'new_v_ffn2_w_gate', 'new_v_ffn2_w_up', 'new_v_ffn2_w_down', 'new_v_ln3_g', 'new_v_ln3_b']
TWIN_LEAF_KINDS = {'loss': 'loss', 'grad_x': 'grad_x', 'grad_ffn1_w_gate': 'grad_w', 'grad_ffn1_w_up': 'grad_w', 'grad_ffn1_w_down': 'grad_w', 'grad_ln1_g': 'grad_w', 'grad_ln1_b': 'grad_w', 'grad_w_in': 'grad_w', 'grad_b_in': 'grad_w', 'grad_ret_gn_g': 'grad_w', 'grad_conv_k': 'grad_w', 'grad_conv_b': 'grad_w', 'grad_conv_ln_g': 'grad_w', 'grad_conv_ln_b': 'grad_w', 'grad_w_ret_o': 'grad_w', 'grad_w_conv_o': 'grad_w', 'grad_w_out': 'grad_w', 'grad_ln2_g': 'grad_w', 'grad_ln2_b': 'grad_w', 'grad_ffn2_w_gate': 'grad_w', 'grad_ffn2_w_up': 'grad_w', 'grad_ffn2_w_down': 'grad_w', 'grad_ln3_g': 'grad_w', 'grad_ln3_b': 'grad_w', 'delta_ffn1_w_gate': 'delta_w', 'delta_ffn1_w_up': 'delta_w', 'delta_ffn1_w_down': 'delta_w', 'delta_ln1_g': 'delta_w', 'delta_ln1_b': 'delta_w', 'delta_w_in': 'delta_w', 'delta_b_in': 'delta_w', 'delta_ret_gn_g': 'delta_w', 'delta_conv_k': 'delta_w', 'delta_conv_b': 'delta_w', 'delta_conv_ln_g': 'delta_w', 'delta_conv_ln_b': 'delta_w', 'delta_w_ret_o': 'delta_w', 'delta_w_conv_o': 'delta_w', 'delta_w_out': 'delta_w', 'delta_ln2_g': 'delta_w', 'delta_ln2_b': 'delta_w', 'delta_ffn2_w_gate': 'delta_w', 'delta_ffn2_w_up': 'delta_w', 'delta_ffn2_w_down': 'delta_w', 'delta_ln3_g': 'delta_w', 'delta_ln3_b': 'delta_w', 'new_m_ffn1_w_gate': 'new_m', 'new_m_ffn1_w_up': 'new_m', 'new_m_ffn1_w_down': 'new_m', 'new_m_ln1_g': 'new_m', 'new_m_ln1_b': 'new_m', 'new_m_w_in': 'new_m', 'new_m_b_in': 'new_m', 'new_m_ret_gn_g': 'new_m', 'new_m_conv_k': 'new_m', 'new_m_conv_b': 'new_m', 'new_m_conv_ln_g': 'new_m', 'new_m_conv_ln_b': 'new_m', 'new_m_w_ret_o': 'new_m', 'new_m_w_conv_o': 'new_m', 'new_m_w_out': 'new_m', 'new_m_ln2_g': 'new_m', 'new_m_ln2_b': 'new_m', 'new_m_ffn2_w_gate': 'new_m', 'new_m_ffn2_w_up': 'new_m', 'new_m_ffn2_w_down': 'new_m', 'new_m_ln3_g': 'new_m', 'new_m_ln3_b': 'new_m', 'new_v_ffn1_w_gate': 'new_v', 'new_v_ffn1_w_up': 'new_v', 'new_v_ffn1_w_down': 'new_v', 'new_v_ln1_g': 'new_v', 'new_v_ln1_b': 'new_v', 'new_v_w_in': 'new_v', 'new_v_b_in': 'new_v', 'new_v_ret_gn_g': 'new_v', 'new_v_conv_k': 'new_v', 'new_v_conv_b': 'new_v', 'new_v_conv_ln_g': 'new_v', 'new_v_conv_ln_b': 'new_v', 'new_v_w_ret_o': 'new_v', 'new_v_w_conv_o': 'new_v', 'new_v_w_out': 'new_v', 'new_v_ln2_g': 'new_v', 'new_v_ln2_b': 'new_v', 'new_v_ffn2_w_gate': 'new_v', 'new_v_ffn2_w_up': 'new_v', 'new_v_ffn2_w_down': 'new_v', 'new_v_ln3_g': 'new_v', 'new_v_ln3_b': 'new_v'}


def _forward(args):
    return _fwd_reference(*[args[k] for k in FWD_PARAMS])


def _output_shape():
    out = _jax.eval_shape(lambda: _forward(_fwd_setup_inputs(0)))
    return out.shape, out.dtype

N_MICROBATCH = 1
ADAM_LR = 0.001
ADAM_B1 = 0.9
ADAM_B2 = 0.999
ADAM_EPS = 1e-08
ADAM_WD = 0.01
ADAM_STEP = 10
PER_EXAMPLE_BATCH_AXIS = {'x': 0, 'loss_target': 0}
SHARED_INPUTS = []
_WEIGHT_DTYPES = {'ffn1_w_gate': _jnp.float32, 'ffn1_w_up': _jnp.float32, 'ffn1_w_down': _jnp.float32, 'ln1_g': _jnp.float32, 'ln1_b': _jnp.float32, 'w_in': _jnp.float32, 'b_in': _jnp.float32, 'ret_gn_g': _jnp.float32, 'conv_k': _jnp.float32, 'conv_b': _jnp.float32, 'conv_ln_g': _jnp.float32, 'conv_ln_b': _jnp.float32, 'w_ret_o': _jnp.float32, 'w_conv_o': _jnp.float32, 'w_out': _jnp.float32, 'ln2_g': _jnp.float32, 'ln2_b': _jnp.float32, 'ffn2_w_gate': _jnp.float32, 'ffn2_w_up': _jnp.float32, 'ffn2_w_down': _jnp.float32, 'ln3_g': _jnp.float32, 'ln3_b': _jnp.float32}
MOMENT_SCALE = {'ffn1_w_gate': 1.670748e-02, 'ffn1_w_up': 1.624057e-02, 'ffn1_w_down': 4.521745e-02, 'ln1_g': 9.984239e-01, 'ln1_b': 4.480118e-01, 'w_in': 1.291926e-02, 'b_in': 1.486827e-02, 'ret_gn_g': 1.207633e-02, 'conv_k': 1.815550e-02, 'conv_b': 5.467086e-02, 'conv_ln_g': 2.792123e-02, 'conv_ln_b': 3.315449e-02, 'w_ret_o': 2.866334e-02, 'w_conv_o': 3.436953e-02, 'w_out': 4.432424e-02, 'ln2_g': 1.042651e+00, 'ln2_b': 4.423141e-01, 'ffn2_w_gate': 1.629361e-02, 'ffn2_w_up': 1.582359e-02, 'ffn2_w_down': 4.414529e-02, 'ln3_g': 3.206986e+01, 'ln3_b': 1.248006e+00}


def _to_microbatches(a, axis):
    t = _jnp.moveaxis(a, axis, 0)
    t = t.reshape((N_MICROBATCH, t.shape[0] // N_MICROBATCH) + t.shape[1:])
    return _jnp.moveaxis(t, 1, axis + 1)


def setup_inputs(seed: int = 0) -> dict:
    inp = _fwd_setup_inputs(seed)
    key = _jax.random.fold_in(_jax.random.key(seed), 7919)
    shape, _ = _output_shape()
    out = dict(inp)
    out["loss_target"] = _jax.random.normal(_jax.random.fold_in(key, 0), shape, _jnp.float32)
    for i, name in enumerate(TWIN_WEIGHTS):
        w = inp[name].astype(_jnp.float32)
        if MOMENT_SCALE is None:
            s = _jnp.sqrt(_jnp.mean(_jnp.square(w)) + 1e-30)
        else:
            s = MOMENT_SCALE[name]
        km, kv = _jax.random.split(_jax.random.fold_in(key, i + 1))
        out[name] = w
        out["m_" + name] = s * _jax.random.normal(km, w.shape, _jnp.float32)
        out["v_" + name] = (s * s) * _jax.random.uniform(kv, w.shape, _jnp.float32, 0.5, 1.5)
    if N_MICROBATCH > 1:
        for name, axis in PER_EXAMPLE_BATCH_AXIS.items():
            out[name] = _to_microbatches(out[name], axis)
    return {'x': out['x'], 'ffn1_w_gate': out['ffn1_w_gate'], 'ffn1_w_up': out['ffn1_w_up'], 'ffn1_w_down': out['ffn1_w_down'], 'ln1_g': out['ln1_g'], 'ln1_b': out['ln1_b'], 'w_in': out['w_in'], 'b_in': out['b_in'], 'ret_gn_g': out['ret_gn_g'], 'conv_k': out['conv_k'], 'conv_b': out['conv_b'], 'conv_ln_g': out['conv_ln_g'], 'conv_ln_b': out['conv_ln_b'], 'w_ret_o': out['w_ret_o'], 'w_conv_o': out['w_conv_o'], 'w_out': out['w_out'], 'ln2_g': out['ln2_g'], 'ln2_b': out['ln2_b'], 'ffn2_w_gate': out['ffn2_w_gate'], 'ffn2_w_up': out['ffn2_w_up'], 'ffn2_w_down': out['ffn2_w_down'], 'ln3_g': out['ln3_g'], 'ln3_b': out['ln3_b'], 'loss_target': out['loss_target'], 'm_ffn1_w_gate': out['m_ffn1_w_gate'], 'm_ffn1_w_up': out['m_ffn1_w_up'], 'm_ffn1_w_down': out['m_ffn1_w_down'], 'm_ln1_g': out['m_ln1_g'], 'm_ln1_b': out['m_ln1_b'], 'm_w_in': out['m_w_in'], 'm_b_in': out['m_b_in'], 'm_ret_gn_g': out['m_ret_gn_g'], 'm_conv_k': out['m_conv_k'], 'm_conv_b': out['m_conv_b'], 'm_conv_ln_g': out['m_conv_ln_g'], 'm_conv_ln_b': out['m_conv_ln_b'], 'm_w_ret_o': out['m_w_ret_o'], 'm_w_conv_o': out['m_w_conv_o'], 'm_w_out': out['m_w_out'], 'm_ln2_g': out['m_ln2_g'], 'm_ln2_b': out['m_ln2_b'], 'm_ffn2_w_gate': out['m_ffn2_w_gate'], 'm_ffn2_w_up': out['m_ffn2_w_up'], 'm_ffn2_w_down': out['m_ffn2_w_down'], 'm_ln3_g': out['m_ln3_g'], 'm_ln3_b': out['m_ln3_b'], 'v_ffn1_w_gate': out['v_ffn1_w_gate'], 'v_ffn1_w_up': out['v_ffn1_w_up'], 'v_ffn1_w_down': out['v_ffn1_w_down'], 'v_ln1_g': out['v_ln1_g'], 'v_ln1_b': out['v_ln1_b'], 'v_w_in': out['v_w_in'], 'v_b_in': out['v_b_in'], 'v_ret_gn_g': out['v_ret_gn_g'], 'v_conv_k': out['v_conv_k'], 'v_conv_b': out['v_conv_b'], 'v_conv_ln_g': out['v_conv_ln_g'], 'v_conv_ln_b': out['v_conv_ln_b'], 'v_w_ret_o': out['v_w_ret_o'], 'v_w_conv_o': out['v_w_conv_o'], 'v_w_out': out['v_w_out'], 'v_ln2_g': out['v_ln2_g'], 'v_ln2_b': out['v_ln2_b'], 'v_ffn2_w_gate': out['v_ffn2_w_gate'], 'v_ffn2_w_up': out['v_ffn2_w_up'], 'v_ffn2_w_down': out['v_ffn2_w_down'], 'v_ln3_g': out['v_ln3_g'], 'v_ln3_b': out['v_ln3_b']}


def _loss(weights, diff, rest, loss_target):
    with _jax.named_scope("forward"):
        args = {**rest, TWIN_DIFF_INPUT: diff, **{k: w.astype(_WEIGHT_DTYPES[k]) for k, w in weights.items()}}
        y = _forward(args)
    with _jax.named_scope("loss_head"):
        err = _jnp.square(y.astype(_jnp.float32) - loss_target)
        return 0.5 * _jnp.sum(_jnp.mean(err, axis=-1)) if err.ndim else 0.5 * err


def _adamw(w, g, m, v):
    m = ADAM_B1 * m + (1.0 - ADAM_B1) * g
    v = ADAM_B2 * v + (1.0 - ADAM_B2) * _jnp.square(g)
    m_hat = m / (1.0 - ADAM_B1 ** ADAM_STEP)
    v_hat = v / (1.0 - ADAM_B2 ** ADAM_STEP)
    delta = -ADAM_LR * (m_hat / (_jnp.sqrt(v_hat) + ADAM_EPS) + ADAM_WD * w)
    return delta, m, v


def reference(x, ffn1_w_gate, ffn1_w_up, ffn1_w_down, ln1_g, ln1_b, w_in, b_in, ret_gn_g, conv_k, conv_b, conv_ln_g, conv_ln_b, w_ret_o, w_conv_o, w_out, ln2_g, ln2_b, ffn2_w_gate, ffn2_w_up, ffn2_w_down, ln3_g, ln3_b, loss_target, m_ffn1_w_gate, m_ffn1_w_up, m_ffn1_w_down, m_ln1_g, m_ln1_b, m_w_in, m_b_in, m_ret_gn_g, m_conv_k, m_conv_b, m_conv_ln_g, m_conv_ln_b, m_w_ret_o, m_w_conv_o, m_w_out, m_ln2_g, m_ln2_b, m_ffn2_w_gate, m_ffn2_w_up, m_ffn2_w_down, m_ln3_g, m_ln3_b, v_ffn1_w_gate, v_ffn1_w_up, v_ffn1_w_down, v_ln1_g, v_ln1_b, v_w_in, v_b_in, v_ret_gn_g, v_conv_k, v_conv_b, v_conv_ln_g, v_conv_ln_b, v_w_ret_o, v_w_conv_o, v_w_out, v_ln2_g, v_ln2_b, v_ffn2_w_gate, v_ffn2_w_up, v_ffn2_w_down, v_ln3_g, v_ln3_b):
    given = dict(x=x, ffn1_w_gate=ffn1_w_gate, ffn1_w_up=ffn1_w_up, ffn1_w_down=ffn1_w_down, ln1_g=ln1_g, ln1_b=ln1_b, w_in=w_in, b_in=b_in, ret_gn_g=ret_gn_g, conv_k=conv_k, conv_b=conv_b, conv_ln_g=conv_ln_g, conv_ln_b=conv_ln_b, w_ret_o=w_ret_o, w_conv_o=w_conv_o, w_out=w_out, ln2_g=ln2_g, ln2_b=ln2_b, ffn2_w_gate=ffn2_w_gate, ffn2_w_up=ffn2_w_up, ffn2_w_down=ffn2_w_down, ln3_g=ln3_g, ln3_b=ln3_b, loss_target=loss_target, m_ffn1_w_gate=m_ffn1_w_gate, m_ffn1_w_up=m_ffn1_w_up, m_ffn1_w_down=m_ffn1_w_down, m_ln1_g=m_ln1_g, m_ln1_b=m_ln1_b, m_w_in=m_w_in, m_b_in=m_b_in, m_ret_gn_g=m_ret_gn_g, m_conv_k=m_conv_k, m_conv_b=m_conv_b, m_conv_ln_g=m_conv_ln_g, m_conv_ln_b=m_conv_ln_b, m_w_ret_o=m_w_ret_o, m_w_conv_o=m_w_conv_o, m_w_out=m_w_out, m_ln2_g=m_ln2_g, m_ln2_b=m_ln2_b, m_ffn2_w_gate=m_ffn2_w_gate, m_ffn2_w_up=m_ffn2_w_up, m_ffn2_w_down=m_ffn2_w_down, m_ln3_g=m_ln3_g, m_ln3_b=m_ln3_b, v_ffn1_w_gate=v_ffn1_w_gate, v_ffn1_w_up=v_ffn1_w_up, v_ffn1_w_down=v_ffn1_w_down, v_ln1_g=v_ln1_g, v_ln1_b=v_ln1_b, v_w_in=v_w_in, v_b_in=v_b_in, v_ret_gn_g=v_ret_gn_g, v_conv_k=v_conv_k, v_conv_b=v_conv_b, v_conv_ln_g=v_conv_ln_g, v_conv_ln_b=v_conv_ln_b, v_w_ret_o=v_w_ret_o, v_w_conv_o=v_w_conv_o, v_w_out=v_w_out, v_ln2_g=v_ln2_g, v_ln2_b=v_ln2_b, v_ffn2_w_gate=v_ffn2_w_gate, v_ffn2_w_up=v_ffn2_w_up, v_ffn2_w_down=v_ffn2_w_down, v_ln3_g=v_ln3_g, v_ln3_b=v_ln3_b)
    weights = {n: given[n] for n in TWIN_WEIGHTS}
    shared = {n: given[n] for n in SHARED_INPUTS}
    per_example = {n: given[n] for n in ['x']}
    grad_fn = _jax.value_and_grad(_loss, argnums=(0, 1))

    def one_microbatch(ex, loss_target):
        ex = dict(ex)
        diff = ex.pop(TWIN_DIFF_INPUT)
        return grad_fn(weights, diff, {**shared, **ex}, loss_target)

    if N_MICROBATCH == 1:
        loss, (grad_w, grad_x) = one_microbatch(per_example, given["loss_target"])
    else:
        def body(carry, xs):
            loss_sum, grad_sum = carry
            l_k, (gw_k, gx_k) = one_microbatch(xs[0], xs[1])
            with _jax.named_scope("update"):
                return (loss_sum + l_k, _jax.tree.map(_jnp.add, grad_sum, gw_k)), gx_k

        init = (_jnp.zeros((), _jnp.float32), _jax.tree.map(_jnp.zeros_like, weights))
        (loss, grad_w), grad_x = _jax.lax.scan(body, init, (per_example, given["loss_target"]))
    with _jax.named_scope("update"):
        delta_w, new_m, new_v = {}, {}, {}
        for n in TWIN_WEIGHTS:
            delta_w[n], new_m[n], new_v[n] = _adamw(weights[n], grad_w[n], given["m_" + n], given["v_" + n])
    return (loss, grad_x, *[grad_w[n] for n in TWIN_WEIGHTS], *[delta_w[n] for n in TWIN_WEIGHTS],
            *[new_m[n] for n in TWIN_WEIGHTS], *[new_v[n] for n in TWIN_WEIGHTS])
```

```python
import functools
import math

import jax
import jax.numpy as jnp
from jax import lax
from jax.experimental import pallas as pl
from jax.experimental.pallas import tpu as pltpu

F32 = jnp.float32
BF16 = jnp.bfloat16

N_DEV = 8
LN_EPS = 1e-5
ALPHA = 2.0 ** 0.25
RET_DK = 128
RET_DV = 256
RET_CHUNK = 128
ROPE_BASE = 10000.0
CONV_WIDTH = 31
HALO = 32
ADAM_LR, ADAM_B1, ADAM_B2, ADAM_EPS, ADAM_WD, ADAM_STEP = 0.001, 0.9, 0.999, 1e-08, 0.01, 10
PACK_W = 1024
VMEM_LIMIT = 52 * 1024 * 1024

BIG = ['ffn1_w_gate', 'ffn1_w_up', 'ffn1_w_down', 'w_in', 'w_ret_o', 'w_conv_o', 'w_out',
       'ffn2_w_gate', 'ffn2_w_up', 'ffn2_w_down', 'conv_k']
COL_SHARDED = {'ffn1_w_gate', 'ffn1_w_up', 'w_in', 'ffn2_w_gate', 'ffn2_w_up', 'conv_k'}
SMALL = ['ln1_g', 'ln1_b', 'b_in', 'ret_gn_g', 'conv_b', 'conv_ln_g', 'conv_ln_b', 'ln2_g', 'ln2_b', 'ln3_g', 'ln3_b']
WEIGHTS = ['ffn1_w_gate', 'ffn1_w_up', 'ffn1_w_down', 'ln1_g', 'ln1_b', 'w_in', 'b_in', 'ret_gn_g', 'conv_k', 'conv_b',
           'conv_ln_g', 'conv_ln_b', 'w_ret_o', 'w_conv_o', 'w_out', 'ln2_g', 'ln2_b', 'ffn2_w_gate', 'ffn2_w_up',
           'ffn2_w_down', 'ln3_g', 'ln3_b']


def _params(sem=None):
    return pltpu.CompilerParams(dimension_semantics=sem, vmem_limit_bytes=VMEM_LIMIT)


def _sigmoid(x):
    return jax.nn.sigmoid(x)


def _dsilu(x, sg):
    return sg * (1.0 + x * (1.0 - sg))


def _fit(dim, want):
    if dim <= want:
        return dim
    return max(t for t in range(128, want + 1, 128) if dim % t == 0)


def _dot(a, b, ta=False, tb=False):
    dn = (((0,) if ta else (1,), (1,) if tb else (0,)), ((), ()))
    return lax.dot_general(a, b, dn, preferred_element_type=F32)


def _mm(name, As, Bs, prods, epi, out_dtypes, *, ta=False, tb=False, tm, tn, tk, extras=(), i_outer=True):
    a0, b0 = As[0], Bs[0]
    M, K = (a0.shape[1], a0.shape[0]) if ta else a0.shape
    N = b0.shape[0] if tb else b0.shape[1]
    tm, tn, tk = _fit(M, tm), _fit(N, tn), _fit(K, tk)
    assert M % tm == 0 and N % tn == 0 and K % tk == 0, (name, M, N, K, tm, tn, tk)
    gi, gj, gk = M // tm, N // tn, K // tk
    grid = (gi, gj, gk) if i_outer else (gj, gi, gk)

    def ij(g0, g1):
        return (g0, g1) if i_outer else (g1, g0)

    def amap(g0, g1, k):
        i, _ = ij(g0, g1)
        return (k, i) if ta else (i, k)

    def bmap(g0, g1, k):
        _, j = ij(g0, g1)
        return (j, k) if tb else (k, j)

    in_specs = [pl.BlockSpec((tk, tm) if ta else (tm, tk), amap) for _ in As]
    in_specs += [pl.BlockSpec((tn, tk) if tb else (tk, tn), bmap) for _ in Bs]
    args = list(As) + list(Bs)
    for arr, kind, coloff in extras:
        assert coloff % tn == 0
        off = coloff // tn
        if kind == 'mn':
            in_specs.append(pl.BlockSpec((tm, tn), lambda g0, g1, k, off=off: (ij(g0, g1)[0], ij(g0, g1)[1] + off)))
        else:
            in_specs.append(pl.BlockSpec((1, tn), lambda g0, g1, k, off=off: (0, ij(g0, g1)[1] + off)))
        args.append(arr)
    out_shape = [jax.ShapeDtypeStruct((M, N), dt) for dt in out_dtypes]
    out_specs = [pl.BlockSpec((tm, tn), lambda g0, g1, k: ij(g0, g1)) for _ in out_dtypes]
    n_a, n_b, n_e, n_o, n_p = len(As), len(Bs), len(extras), len(out_dtypes), len(prods)

    def body(*refs):
        a_refs = refs[:n_a]
        b_refs = refs[n_a:n_a + n_b]
        e_refs = refs[n_a + n_b:n_a + n_b + n_e]
        o_refs = refs[n_a + n_b + n_e:n_a + n_b + n_e + n_o]
        acc_refs = refs[n_a + n_b + n_e + n_o:]
        k = pl.program_id(2)

        @pl.when(k == 0)
        def _():
            for acc in acc_refs:
                acc[...] = jnp.zeros_like(acc)

        for p, terms in enumerate(prods):
            for ai, bi in terms:
                acc_refs[p][...] += _dot(a_refs[ai][...], b_refs[bi][...], ta, tb)

        @pl.when(k == gk - 1)
        def _():
            res = epi([acc[...] for acc in acc_refs], *[e[...] for e in e_refs])
            for o, r in zip(o_refs, res):
                o[...] = r.astype(o.dtype)

    res = pl.pallas_call(
        body, name=name, grid=grid, in_specs=in_specs, out_specs=out_specs, out_shape=out_shape,
        scratch_shapes=[pltpu.VMEM((tm, tn), F32) for _ in range(n_p)],
        compiler_params=_params(("arbitrary", "arbitrary", "arbitrary")),
    )(*args)
    return res


def _rows(name, fn, ins, outs, *, T, tb):
    tb = min(tb, T)
    assert T % tb == 0
    in_specs, args = [], []
    for arr, kind, width, cb in ins:
        if kind == 'r':
            in_specs.append(pl.BlockSpec((tb, width), lambda i, cb=cb: (i, cb)))
        else:
            in_specs.append(pl.BlockSpec((1, width), lambda i, cb=cb: (0, cb)))
        args.append(arr)
    out_shape, out_specs = [], []
    for kind, width, dtype in outs:
        if kind == 'r':
            out_shape.append(jax.ShapeDtypeStruct((T, width), dtype))
            out_specs.append(pl.BlockSpec((tb, width), lambda i: (i, 0)))
        elif kind == 'c':
            out_shape.append(jax.ShapeDtypeStruct((T, 1), dtype))
            out_specs.append(pl.BlockSpec((tb, 1), lambda i: (i, 0)))
        else:
            out_shape.append(jax.ShapeDtypeStruct((1, width), F32))
            out_specs.append(pl.BlockSpec((1, width), lambda i: (0, 0)))
    n_in = len(ins)

    def body(*refs):
        i = pl.program_id(0)
        vals = fn(*[r[...] for r in refs[:n_in]])
        for (kind, _, _), o, v in zip(outs, refs[n_in:], vals):
            if kind == 'a':
                @pl.when(i == 0)
                def _(o=o):
                    o[...] = jnp.zeros_like(o)

                o[...] += v
            else:
                o[...] = v.astype(o.dtype)

    return pl.pallas_call(
        body, name=name, grid=(T // tb,), in_specs=in_specs, out_specs=out_specs, out_shape=out_shape,
        compiler_params=_params(("arbitrary",)),
    )(*args)


def _colsum(v):
    return jnp.sum(v, axis=0, keepdims=True)


def _ln_stats(z):
    mu = jnp.mean(z, axis=-1, keepdims=True)
    d = z - mu
    var = jnp.mean(d * d, axis=-1, keepdims=True)
    rstd = lax.rsqrt(var + LN_EPS)
    return d * rstd, rstd


def _ln_bwd_math(dy, xhat, rstd, g):
    dxh = dy * g
    m1 = jnp.mean(dxh, axis=-1, keepdims=True)
    m2 = jnp.mean(dxh * xhat, axis=-1, keepdims=True)
    return rstd * (dxh - m1 - xhat * m2)


def _ln_fwd(name, z, g, b, T, D):
    def fn(z, g, b):
        xhat, rstd = _ln_stats(z)
        y = xhat * g + b
        return [y, y, xhat, rstd]

    return _rows(name, fn, [(z, 'r', D, 0), (g, 'v', D, 0), (b, 'v', D, 0)],
                 [('r', D, F32), ('r', D, BF16), ('r', D, F32), ('c', 1, F32)], T=T, tb=512)


def _ln_bwd(name, dy, xhat, rstd, g, scale, T, D):
    def fn(dy, xhat, rstd, g):
        dz = _ln_bwd_math(dy, xhat, rstd, g)
        return [dz, dz * scale, _colsum(dy * xhat), _colsum(dy)]

    return _rows(name, fn, [(dy, 'r', D, 0), (xhat, 'r', D, 0), (rstd, 'r', 1, 0), (g, 'v', D, 0)],
                 [('r', D, F32), ('r', D, BF16), ('a', D, F32), ('a', D, F32)], T=T, tb=512)


def _ln_loss_bwd(name, z, g, b, tgt, T, D):
    def fn(z, g, b, tgt):
        xhat, rstd = _ln_stats(z)
        err = xhat * g + b - tgt
        row_loss = 0.5 * jnp.mean(err * err, axis=-1, keepdims=True)
        loss = jnp.broadcast_to(jnp.sum(row_loss, axis=0, keepdims=True), (1, 128))
        dy = err * (1.0 / D)
        dz = _ln_bwd_math(dy, xhat, rstd, g)
        return [dz, dz * 0.5, _colsum(dy * xhat), _colsum(dy), loss]

    return _rows(name, fn, [(z, 'r', D, 0), (g, 'v', D, 0), (b, 'v', D, 0), (tgt, 'r', D, 0)],
                 [('r', D, F32), ('r', D, BF16), ('a', D, F32), ('a', D, F32), ('a', 128, F32)], T=T, tb=512)


def _ffn_fwd(tag, xb, x, wg, wu, wd):
    def epi_gu(accs):
        a, b = accs
        return [a, b, a * _sigmoid(a) * b]

    a, b, s = _mm(tag + "_gate_up", [xb], [wg, wu], [[(0, 0)], [(0, 1)]], epi_gu, [F32, F32, BF16],
                  tm=512, tn=1408, tk=1024)

    def epi_down(accs, xres):
        return [ALPHA * xres + 0.5 * accs[0]]

    (z,) = _mm(tag + "_down", [s], [wd], [[(0, 0)]], epi_down, [F32], tm=512, tn=1024, tk=1408,
               extras=[(x, 'mn', 0)])
    return a, b, s, z


def _ffn_bwd(tag, dzh, dz, xb, a, b, s, wg, wu, wd, gdt):
    def epi_ds(accs, a, b):
        ds = accs[0]
        sg = _sigmoid(a)
        return [ds * b * _dsilu(a, sg), ds * a * sg]

    da, db = _mm(tag + "_ds", [dzh], [wd], [[(0, 0)]], epi_ds, [BF16, BF16], tb=True, tm=512, tn=1408, tk=1024,
                 extras=[(a, 'mn', 0), (b, 'mn', 0)])
    ident = lambda accs: accs
    (dwd,) = _mm(tag + "_dwd", [s], [dzh], [[(0, 0)]], ident, [gdt], ta=True, tm=1408, tn=1024, tk=1024)
    dwg, dwu = _mm(tag + "_dwgu", [xb], [da, db], [[(0, 0)], [(0, 1)]], ident, [gdt, gdt], ta=True,
                   tm=512, tn=1408, tk=1024)

    def epi_dx(accs, dzres):
        return [ALPHA * dzres + accs[0]]

    (dx,) = _mm(tag + "_dx", [da, db], [wg, wu], [[(0, 0), (1, 1)]], epi_dx, [F32], tb=True,
                tm=512, tn=1024, tk=1408, extras=[(dz, 'mn', 0)])
    return dwg, dwu, dwd, dx


def _ret_tables(H, T):
    C = RET_CHUNK
    log_g = jnp.log(1.0 - jnp.exp2(-5.0 - jnp.arange(H, dtype=F32)))
    idx = jnp.arange(C, dtype=F32)
    diff = idx[:, None] - idx[None, :]
    dm = jnp.where(diff[None] >= 0, jnp.exp(jnp.maximum(diff, 0.0)[None] * log_g[:, None, None]), 0.0)
    xi = jnp.exp((idx[None, :] + 1.0) * log_g[:, None])[:, :, None]
    zeta = jnp.exp((C - 1.0 - idx)[None, :] * log_g[:, None])[:, :, None]
    gc = jnp.broadcast_to(jnp.exp(C * log_g)[:, None, None], (H, 1, RET_DV))
    half = RET_DK // 2
    freqs = ROPE_BASE ** (-jnp.arange(half, dtype=F32) / half)
    ang = jnp.arange(T, dtype=F32)[:, None] * freqs[None, :]
    cos, sin = jnp.cos(ang), jnp.sin(ang)
    cosf = jnp.concatenate([cos, cos], axis=1)
    sins = jnp.concatenate([-sin, sin], axis=1)
    return dm, xi, zeta, gc, cosf, sins


def _rot(x, cosf, sins):
    return x * cosf + pltpu.roll(x, RET_DK // 2, 1) * sins


def _rot_bwd(dy, cosf, sins):
    return dy * cosf + pltpu.roll(dy * sins, RET_DK // 2, 1)


def _ret_specs(H, rev, NC):
    C = RET_CHUNK
    nn = (lambda n: NC - 1 - n) if rev else (lambda n: n)
    return [
        pl.BlockSpec((C, RET_DK), lambda h, n: (nn(n), h)),
        pl.BlockSpec((C, RET_DK), lambda h, n: (nn(n), H + h)),
        pl.BlockSpec((C, RET_DV), lambda h, n: (nn(n), H + h)),
        pl.BlockSpec((C, RET_DV), lambda h, n: (nn(n), 2 * H + h)),
        pl.BlockSpec((C, RET_DK), lambda h, n: (nn(n), 0)),
        pl.BlockSpec((C, RET_DK), lambda h, n: (nn(n), 0)),
        pl.BlockSpec((1, RET_DV), lambda h, n: (0, h)),
        pl.BlockSpec((None, C, C), lambda h, n: (h, 0, 0)),
        pl.BlockSpec((None, C, 1), lambda h, n: (h, 0, 0)),
        pl.BlockSpec((None, C, 1), lambda h, n: (h, 0, 0)),
        pl.BlockSpec((None, 1, RET_DV), lambda h, n: (h, 0, 0)),
    ]


def _ret_fwd(proj, gn_g, tabs, H, T):
    C, NC = RET_CHUNK, T // RET_CHUNK
    dm, xi, zeta, gc, cosf, sins = tabs
    scale = RET_DK ** -0.5

    def body(q_ref, k_ref, v_ref, g_ref, cos_ref, sin_ref, gn_ref, dm_ref, xi_ref, zt_ref, gc_ref,
             r_ref, ri_ref, st_ref, state):
        @pl.when(pl.program_id(1) == 0)
        def _():
            state[...] = jnp.zeros_like(state)

        cs, sn = cos_ref[...], sin_ref[...]
        qr = _rot(q_ref[...], cs, sn) * scale
        kr = _rot(k_ref[...], cs, sn)
        qb, kb, vb = qr.astype(BF16), kr.astype(BF16), v_ref[...].astype(BF16)
        st = state[...]
        stb = st.astype(BF16)
        s = _dot(qb, kb, tb=True) * dm_ref[...]
        r = _dot(s.astype(BF16), vb) + _dot(qb, stb) * xi_ref[...]
        st_ref[...] = stb
        state[...] = gc_ref[...] * st + _dot((kr * zt_ref[...]).astype(BF16), vb, ta=True)
        rhat, _ = _ln_stats(r)
        g = g_ref[...]
        r_ref[...] = r
        ri_ref[...] = (g * _sigmoid(g) * (rhat * gn_ref[...])).astype(BF16)

    VW = H * RET_DV
    return pl.pallas_call(
        body, name="ret_fwd", grid=(H, NC), in_specs=_ret_specs(H, False, NC),
        out_specs=[pl.BlockSpec((C, RET_DV), lambda h, n: (n, h)),
                   pl.BlockSpec((C, RET_DV), lambda h, n: (n, h)),
                   pl.BlockSpec((None, None, RET_DK, RET_DV), lambda h, n: (h, n, 0, 0))],
        out_shape=[jax.ShapeDtypeStruct((T, VW), F32), jax.ShapeDtypeStruct((T, VW), BF16),
                   jax.ShapeDtypeStruct((H, NC, RET_DK, RET_DV), BF16)],
        scratch_shapes=[pltpu.VMEM((RET_DK, RET_DV), F32)],
        compiler_params=_params(("arbitrary", "arbitrary")),
    )(proj, proj, proj, proj, cosf, sins, gn_g, dm, xi, zeta, gc)


def _ret_bwd(dri, r, states, proj, gn_g, tabs, H, T):
    C, NC = RET_CHUNK, T // RET_CHUNK
    dm, xi, zeta, gc, cosf, sins = tabs
    scale = RET_DK ** -0.5

    def body(q_ref, k_ref, v_ref, g_ref, cos_ref, sin_ref, gn_ref, dm_ref, xi_ref, zt_ref, gc_ref,
             dri_ref, r_ref, st_ref, dq_ref, dk_ref, dv_ref, dg_ref, dgn_ref, dstate):
        @pl.when(pl.program_id(1) == 0)
        def _():
            dstate[...] = jnp.zeros_like(dstate)
            dgn_ref[...] = jnp.zeros_like(dgn_ref)

        cs, sn = cos_ref[...], sin_ref[...]
        qr = _rot(q_ref[...], cs, sn) * scale
        kr = _rot(k_ref[...], cs, sn)
        qb, kb, vb = qr.astype(BF16), kr.astype(BF16), v_ref[...].astype(BF16)
        xi_c, zt_c, dmask = xi_ref[...], zt_ref[...], dm_ref[...]
        rhat, rstd = _ln_stats(r_ref[...])
        g, gn, dpre = g_ref[...], gn_ref[...], dri_ref[...]
        sg = _sigmoid(g)
        dg_ref[...] = (dpre * (rhat * gn) * _dsilu(g, sg)).astype(BF16)
        drn = dpre * (g * sg)
        dgn_ref[...] += _colsum(drn * rhat)
        dr = _ln_bwd_math(drn, rhat, rstd, gn)
        drb = dr.astype(BF16)
        ds1 = dstate[...]
        ds1b = ds1.astype(BF16)
        sb = (_dot(qb, kb, tb=True) * dmask).astype(BF16)
        kzb = (kr * zt_c).astype(BF16)
        dv_ref[...] = (_dot(sb, drb, ta=True) + _dot(kzb, ds1b)).astype(BF16)
        dsb = (_dot(drb, vb, tb=True) * dmask).astype(BF16)
        dq = _dot(dsb, kb) + _dot(drb, st_ref[...], tb=True) * xi_c
        dk = _dot(dsb, qb, ta=True) + _dot(vb, ds1b, tb=True) * zt_c
        dstate[...] = gc_ref[...] * ds1 + _dot((qr * xi_c).astype(BF16), drb, ta=True)
        dq_ref[...] = _rot_bwd(dq * scale, cs, sn).astype(BF16)
        dk_ref[...] = _rot_bwd(dk, cs, sn).astype(BF16)

    VW, QW = H * RET_DV, H * RET_DK
    rv = lambda n: NC - 1 - n
    in_specs = _ret_specs(H, True, NC) + [
        pl.BlockSpec((C, RET_DV), lambda h, n: (rv(n), h)),
        pl.BlockSpec((C, RET_DV), lambda h, n: (rv(n), h)),
        pl.BlockSpec((None, None, RET_DK, RET_DV), lambda h, n: (h, rv(n), 0, 0)),
    ]
    return pl.pallas_call(
        body, name="ret_bwd", grid=(H, NC), in_specs=in_specs,
        out_specs=[pl.BlockSpec((C, RET_DK), lambda h, n: (rv(n), h)),
                   pl.BlockSpec((C, RET_DK), lambda h, n: (rv(n), h)),
                   pl.BlockSpec((C, RET_DV), lambda h, n: (rv(n), h)),
                   pl.BlockSpec((C, RET_DV), lambda h, n: (rv(n), h)),
                   pl.BlockSpec((1, RET_DV), lambda h, n: (0, h))],
        out_shape=[jax.ShapeDtypeStruct((T, QW), BF16), jax.ShapeDtypeStruct((T, QW), BF16),
                   jax.ShapeDtypeStruct((T, VW), BF16), jax.ShapeDtypeStruct((T, VW), BF16),
                   jax.ShapeDtypeStruct((1, VW), F32)],
        scratch_shapes=[pltpu.VMEM((RET_DK, RET_DV), F32)],
        compiler_params=_params(("arbitrary", "arbitrary")),
    )(proj, proj, proj, proj, cosf, sins, gn_g, dm, xi, zeta, gc, dri, r, states)


CONV_CW = 128
CONV_TB = 512


def _conv_fwd(proj, kpad, bias, off_a, CC, T):
    tb, cw = min(CONV_TB, T), CONV_CW
    hb = tb // HALO
    ca, cb = off_a // cw, (off_a + CC) // cw

    def body(a_ref, b_ref, ap_ref, bp_ref, k_ref, bias_ref, u1_ref, win):
        i = pl.program_id(0)
        keep = (i > 0).astype(F32)
        win[0:HALO, :] = ap_ref[...] * _sigmoid(bp_ref[...]) * keep
        win[HALO:, :] = a_ref[...] * _sigmoid(b_ref[...])
        acc = jnp.broadcast_to(bias_ref[...], (tb, cw))
        for w in range(CONV_WIDTH):
            acc = acc + k_ref[w:w + 1, :] * win[pl.ds(HALO - (CONV_WIDTH - 1) + w, tb), :]
        u1_ref[...] = acc

    prev = lambda i: jnp.maximum(i * hb - 1, 0)
    return pl.pallas_call(
        body, name="conv_fwd", grid=(T // tb, CC // cw),
        in_specs=[pl.BlockSpec((tb, cw), lambda i, c: (i, ca + c)),
                  pl.BlockSpec((tb, cw), lambda i, c: (i, cb + c)),
                  pl.BlockSpec((HALO, cw), lambda i, c: (prev(i), ca + c)),
                  pl.BlockSpec((HALO, cw), lambda i, c: (prev(i), cb + c)),
                  pl.BlockSpec((HALO, cw), lambda i, c: (0, c)),
                  pl.BlockSpec((1, cw), lambda i, c: (0, c))],
        out_specs=pl.BlockSpec((tb, cw), lambda i, c: (i, c)),
        out_shape=jax.ShapeDtypeStruct((T, CC), F32),
        scratch_shapes=[pltpu.VMEM((tb + HALO, cw), F32)],
        compiler_params=_params(("arbitrary", "arbitrary")),
    )(proj, proj, proj, proj, kpad, bias)


def _conv_bwd(du1, proj, kpad, off_a, CC, T):
    tb, cw = min(CONV_TB, T), CONV_CW
    hb = tb // HALO
    nt = T // tb
    ca, cb = off_a // cw, (off_a + CC) // cw

    def body(d_ref, dn_ref, a_ref, b_ref, ap_ref, bp_ref, k_ref, da_ref, db_ref, dk_ref, winu, wind):
        i = pl.program_id(1)
        a, b = a_ref[...], b_ref[...]
        sgb = _sigmoid(b)
        winu[0:HALO, :] = ap_ref[...] * _sigmoid(bp_ref[...]) * (i > 0).astype(F32)
        winu[HALO:, :] = a * sgb
        d = d_ref[...]
        wind[0:tb, :] = d
        wind[tb:, :] = dn_ref[...] * (i < nt - 1).astype(F32)

        @pl.when(i == 0)
        def _():
            dk_ref[...] = jnp.zeros_like(dk_ref)

        du0 = jnp.zeros((tb, cw), F32)
        for w in range(CONV_WIDTH):
            du0 = du0 + k_ref[w:w + 1, :] * wind[pl.ds(CONV_WIDTH - 1 - w, tb), :]
            dk_ref[w:w + 1, :] += _colsum(winu[pl.ds(HALO - (CONV_WIDTH - 1) + w, tb), :] * d)
        da_ref[...] = (du0 * sgb).astype(BF16)
        db_ref[...] = (du0 * a * sgb * (1.0 - sgb)).astype(BF16)

    prev = lambda i: jnp.maximum(i * hb - 1, 0)
    nxt = lambda i: jnp.minimum((i + 1) * hb, T // HALO - 1)
    return pl.pallas_call(
        body, name="conv_bwd", grid=(CC // cw, nt),
        in_specs=[pl.BlockSpec((tb, cw), lambda c, i: (i, c)),
                  pl.BlockSpec((HALO, cw), lambda c, i: (nxt(i), c)),
                  pl.BlockSpec((tb, cw), lambda c, i: (i, ca + c)),
                  pl.BlockSpec((tb, cw), lambda c, i: (i, cb + c)),
                  pl.BlockSpec((HALO, cw), lambda c, i: (prev(i), ca + c)),
                  pl.BlockSpec((HALO, cw), lambda c, i: (prev(i), cb + c)),
                  pl.BlockSpec((HALO, cw), lambda c, i: (0, c))],
        out_specs=[pl.BlockSpec((tb, cw), lambda c, i: (i, c)),
                   pl.BlockSpec((tb, cw), lambda c, i: (i, c)),
                   pl.BlockSpec((HALO, cw), lambda c, i: (0, c))],
        out_shape=[jax.ShapeDtypeStruct((T, CC), BF16), jax.ShapeDtypeStruct((T, CC), BF16),
                   jax.ShapeDtypeStruct((HALO, CC), F32)],
        scratch_shapes=[pltpu.VMEM((tb + HALO, cw), F32), pltpu.VMEM((tb + HALO, cw), F32)],
        compiler_params=_params(("arbitrary", "arbitrary")),
    )(du1, du1, proj, proj, proj, proj, kpad)


def _local_step(x, tgt, W, P, gdt=BF16):
    T, D = x.shape
    F = W['ffn1_w_gate'].shape[1]
    VW = P['ret_gn_g'].shape[1]
    H = VW // RET_DV
    QW = H * RET_DK
    CC = P['conv_b'].shape[1]
    off_glu = 2 * QW + 2 * VW
    off_gate = off_glu + 2 * CC
    ident = lambda accs: accs
    xb = x.astype(BF16)

    a1, b1, s1, z1 = _ffn_fwd("ffn1", xb, x, W['ffn1_w_gate'], W['ffn1_w_up'], W['ffn1_w_down'])
    x1, x1b, xh1, rs1 = _ln_fwd("ln1", z1, P['ln1_g'], P['ln1_b'], T, D)

    (proj,) = _mm("w_in", [x1b], [W['w_in']], [[(0, 0)]], lambda accs, bias: [accs[0] + bias], [F32],
                  tm=512, tn=1024, tk=1024, extras=[(P['b_in'], 'n', 0)], i_outer=False)
    tabs = _ret_tables(H, T)
    r, ret_in, states = _ret_fwd(proj, P['ret_gn_g'], tabs, H, T)
    kpad = jnp.pad(W['conv_k'].astype(F32), ((0, HALO - CONV_WIDTH), (0, 0)))
    u1 = _conv_fwd(proj, kpad, P['conv_b'], off_glu, CC, T)

    def conv_ln(u1, g, b):
        xhat, rstd = _ln_stats(u1)
        u2 = xhat * g + b
        return [xhat, rstd, u2 * _sigmoid(u2)]

    xhc, rsc, u3 = _rows("conv_ln", conv_ln, [(u1, 'r', CC, 0), (P['conv_ln_g'], 'v', CC, 0), (P['conv_ln_b'], 'v', CC, 0)],
                         [('r', CC, F32), ('c', 1, F32), ('r', CC, BF16)], T=T, tb=512)
    (ret_out,) = _mm("ret_o", [ret_in], [W['w_ret_o']], [[(0, 0)]], ident, [F32], tm=512, tn=1024, tk=2048)

    def epi_merge(accs, ret_out, gr, gc):
        conv_out = accs[0]
        return [conv_out, _sigmoid(gr) * ret_out + _sigmoid(gc) * conv_out]

    conv_out, merged = _mm("conv_o_merge", [u3], [W['w_conv_o']], [[(0, 0)]], epi_merge, [F32, BF16],
                           tm=512, tn=D, tk=1024,
                           extras=[(ret_out, 'mn', 0), (proj, 'mn', off_gate), (proj, 'mn', off_gate + D)])
    (z2,) = _mm("w_out", [merged], [W['w_out']], [[(0, 0)]], lambda accs, xr: [ALPHA * xr + accs[0]], [F32],
                tm=512, tn=1024, tk=1024, extras=[(x1, 'mn', 0)])
    x2, x2b, xh2, rs2 = _ln_fwd("ln2", z2, P['ln2_g'], P['ln2_b'], T, D)
    a2, b2, s2, z3 = _ffn_fwd("ffn2", x2b, x2, W['ffn2_w_gate'], W['ffn2_w_up'], W['ffn2_w_down'])
    dz3, dz3h, g_ln3_g, g_ln3_b, loss = _ln_loss_bwd("ln3_loss", z3, P['ln3_g'], P['ln3_b'], tgt, T, D)

    G, S = {}, {'ln3_g': g_ln3_g, 'ln3_b': g_ln3_b}
    G['ffn2_w_gate'], G['ffn2_w_up'], G['ffn2_w_down'], dy2 = _ffn_bwd(
        "ffn2b", dz3h, dz3, x2b, a2, b2, s2, W['ffn2_w_gate'], W['ffn2_w_up'], W['ffn2_w_down'], gdt)
    dz2, dz2b, S['ln2_g'], S['ln2_b'] = _ln_bwd("ln2b", dy2, xh2, rs2, P['ln2_g'], 1.0, T, D)

    (G['w_out'],) = _mm("d_w_out", [merged], [dz2b], [[(0, 0)]], ident, [gdt], ta=True, tm=1024, tn=1024, tk=1024)

    def epi_dmerge(accs, ret_out, conv_out, gr, gc):
        dm_ = accs[0]
        sr, sc = _sigmoid(gr), _sigmoid(gc)
        return [dm_ * sr, dm_ * sc, dm_ * ret_out * sr * (1.0 - sr), dm_ * conv_out * sc * (1.0 - sc)]

    dret_out, dconv_out, dgate_r, dgate_c = _mm(
        "d_merge", [dz2b], [W['w_out']], [[(0, 0)]], epi_dmerge, [BF16, BF16, BF16, BF16], tb=True,
        tm=512, tn=D, tk=1024,
        extras=[(ret_out, 'mn', 0), (conv_out, 'mn', 0), (proj, 'mn', off_gate), (proj, 'mn', off_gate + D)])
    (G['w_ret_o'],) = _mm("d_w_ret_o", [ret_in], [dret_out], [[(0, 0)]], ident, [gdt], ta=True, tm=1024, tn=1024, tk=1024)
    (G['w_conv_o'],) = _mm("d_w_conv_o", [u3], [dconv_out], [[(0, 0)]], ident, [gdt], ta=True, tm=1024, tn=1024, tk=1024)
    (dri,) = _mm("d_ret_in", [dret_out], [W['w_ret_o']], [[(0, 0)]], ident, [F32], tb=True, tm=512, tn=1024, tk=1024)
    dq, dk, dv, dg, S['ret_gn_g'] = _ret_bwd(dri, r, states, proj, P['ret_gn_g'], tabs, H, T)

    def epi_du2(accs, xhat, g, b):
        u2 = xhat * g + b
        return [accs[0] * _dsilu(u2, _sigmoid(u2))]

    (du2,) = _mm("d_u3", [dconv_out], [W['w_conv_o']], [[(0, 0)]], epi_du2, [F32], tb=True, tm=512, tn=CC, tk=1024,
                 extras=[(xhc, 'mn', 0), (P['conv_ln_g'], 'n', 0), (P['conv_ln_b'], 'n', 0)])

    def conv_ln_bwd(du2, xhat, rstd, g):
        du1 = _ln_bwd_math(du2, xhat, rstd, g)
        return [du1, _colsum(du2 * xhat), _colsum(du2), _colsum(du1)]

    du1, S['conv_ln_g'], S['conv_ln_b'], S['conv_b'] = _rows(
        "conv_ln_bwd", conv_ln_bwd, [(du2, 'r', CC, 0), (xhc, 'r', CC, 0), (rsc, 'r', 1, 0), (P['conv_ln_g'], 'v', CC, 0)],
        [('r', CC, F32), ('a', CC, F32), ('a', CC, F32), ('a', CC, F32)], T=T, tb=512)
    dglu_a, dglu_b, dkpad = _conv_bwd(du1, proj, kpad, off_glu, CC, T)
    G['conv_k'] = dkpad[:CONV_WIDTH].astype(gdt)

    dproj = jnp.concatenate([dq, dk, dv, dg, dglu_a, dglu_b, dgate_r, dgate_c], axis=1)
    IN_W = dproj.shape[1]
    (S['b_in'],) = _rows("d_b_in", lambda d: [_colsum(d.astype(F32))], [(dproj, 'r', IN_W, 0)], [('a', IN_W, F32)],
                         T=T, tb=256)
    (G['w_in'],) = _mm("d_w_in", [x1b], [dproj], [[(0, 0)]], ident, [gdt], ta=True, tm=1024, tn=1024, tk=1024)
    (dy1,) = _mm("d_x1", [dproj], [W['w_in']], [[(0, 0)]], lambda accs, dzr: [ALPHA * dzr + accs[0]], [F32], tb=True,
                 tm=512, tn=1024, tk=2048, extras=[(dz2, 'mn', 0)])
    dz1, dz1h, S['ln1_g'], S['ln1_b'] = _ln_bwd("ln1b", dy1, xh1, rs1, P['ln1_g'], 0.5, T, D)
    G['ffn1_w_gate'], G['ffn1_w_up'], G['ffn1_w_down'], grad_x = _ffn_bwd(
        "ffn1b", dz1h, dz1, xb, a1, b1, s1, W['ffn1_w_gate'], W['ffn1_w_up'], W['ffn1_w_down'], gdt)
    return loss[0, 0], grad_x, G, S


MESH = pl.DeviceIdType.MESH
ANY = pl.BlockSpec(memory_space=pl.ANY)


def _coords():
    return lax.axis_index("x"), lax.axis_index("y"), lax.axis_index("c")


def _flip(k, x, y, c):
    return (1 - x if k & 4 else x, 1 - y if k & 2 else y, 1 - c if k & 1 else c)


def _lin(p):
    return 4 * p[0] + 2 * p[1] + p[2]


def _all_gather(name, blk):
    def body(x_ref, out_ref, send_sems, recv_sems, local_sem):
        x, y, c = _coords()
        me, sibling = (x, y, c), (x, y, 1 - c)
        chips = [_flip(4, x, y, c), _flip(2, x, y, c), _flip(6, x, y, c)]

        def copy(k, block, to, src=None):
            slot = out_ref.at[_lin(block)]
            return pltpu.make_async_remote_copy(
                src_ref=slot if src is None else src, dst_ref=slot, send_sem=send_sems.at[k], recv_sem=recv_sems.at[k],
                device_id=to, device_id_type=MESH)

        mine = pltpu.make_async_copy(x_ref, out_ref.at[_lin(me)], local_sem)
        mine.start()
        first = [copy(0, me, sibling, src=x_ref)] + [copy(1 + j, me, chip, src=x_ref) for j, chip in enumerate(chips)]
        for cp in first:
            cp.start()
        passed = [copy(4 + j, chip, sibling) for j, chip in enumerate(chips)]
        for j, chip in enumerate(chips):
            copy(1 + j, chip, me).wait_recv()
            passed[j].start()
        copy(0, sibling, me).wait_recv()
        for j, chip in enumerate(chips):
            copy(4 + j, (chip[0], chip[1], 1 - c), me).wait_recv()
        for cp in first + passed:
            cp.wait_send()
        mine.wait()

    return pl.pallas_call(
        body, name=name, out_shape=jax.ShapeDtypeStruct((N_DEV,) + blk.shape, blk.dtype),
        in_specs=[ANY], out_specs=ANY,
        scratch_shapes=[pltpu.SemaphoreType.DMA((7,)), pltpu.SemaphoreType.DMA((7,)), pltpu.SemaphoreType.DMA(())],
        compiler_params=pltpu.CompilerParams(has_side_effects=True),
    )(blk)


def _exchange(name, g):
    def body(g_ref, out_ref, send_sems, recv_sems, local_sem):
        x, y, c = _coords()
        me = _lin((x, y, c))
        mine = pltpu.make_async_copy(g_ref.at[me], out_ref.at[me], local_sem)
        mine.start()

        def copy(k):
            peer = _flip(k, x, y, c)
            return pltpu.make_async_remote_copy(
                src_ref=g_ref.at[_lin(peer)], dst_ref=out_ref.at[me], send_sem=send_sems.at[k - 1],
                recv_sem=recv_sems.at[k - 1], device_id=peer, device_id_type=MESH)

        def landed(k):
            peer = _flip(k, x, y, c)
            return pltpu.make_async_remote_copy(
                src_ref=g_ref.at[me], dst_ref=out_ref.at[_lin(peer)], send_sem=send_sems.at[k - 1],
                recv_sem=recv_sems.at[k - 1], device_id=peer, device_id_type=MESH)

        sends = [copy(k) for k in range(1, N_DEV)]
        for cp in sends:
            cp.start()
        for k in range(1, N_DEV):
            landed(k).wait_recv()
        for cp in sends:
            cp.wait_send()
        mine.wait()

    return pl.pallas_call(
        body, name=name, out_shape=jax.ShapeDtypeStruct(g.shape, g.dtype),
        in_specs=[ANY], out_specs=ANY,
        scratch_shapes=[pltpu.SemaphoreType.DMA((7,)), pltpu.SemaphoreType.DMA((7,)), pltpu.SemaphoreType.DMA(())],
        compiler_params=pltpu.CompilerParams(has_side_effects=True),
    )(g)


def _adamw(name, parts, w, m, v, tb):
    n, R, Wd = parts.shape
    assert R % tb == 0
    c1 = 1.0 - ADAM_B1 ** ADAM_STEP
    c2 = 1.0 - ADAM_B2 ** ADAM_STEP

    def body(p_ref, w_ref, m_ref, v_ref, g_ref, d_ref, nm_ref, nv_ref):
        g = p_ref[0].astype(F32)
        for s in range(1, n):
            g = g + p_ref[s].astype(F32)
        nm = ADAM_B1 * m_ref[...] + (1.0 - ADAM_B1) * g
        nv = ADAM_B2 * v_ref[...] + (1.0 - ADAM_B2) * (g * g)
        g_ref[...] = g
        nm_ref[...] = nm
        nv_ref[...] = nv
        d_ref[...] = -ADAM_LR * ((nm / c1) / (jnp.sqrt(nv / c2) + ADAM_EPS) + ADAM_WD * w_ref[...])

    row = pl.BlockSpec((tb, Wd), lambda i: (i, 0))
    return pl.pallas_call(
        body, name=name, grid=(R // tb,),
        in_specs=[pl.BlockSpec((n, tb, Wd), lambda i: (0, i, 0)), row, row, row],
        out_specs=[row, row, row, row], out_shape=[jax.ShapeDtypeStruct((R, Wd), F32)] * 4,
        compiler_params=_params(("arbitrary",)),
    )(parts, w, m, v)


def _pack(mats, R, dtype):
    flat = jnp.concatenate([a.reshape(-1).astype(dtype) for a in mats])
    return jnp.pad(flat, (0, R * PACK_W - flat.shape[0])).reshape(R, PACK_W)


def _unpack(buf, shapes):
    flat, out, off = buf.reshape(-1), [], 0
    for shp in shapes:
        n = shp[0] * shp[1]
        out.append(flat[off:off + n].reshape(shp))
        off += n
    return out


def _row_tile(R, unit, cap):
    best = unit
    for t in range(unit, cap + 1, unit):
        if R % t == 0:
            best = t
    return best


def kernel(x, ffn1_w_gate, ffn1_w_up, ffn1_w_down, ln1_g, ln1_b, w_in, b_in, ret_gn_g, conv_k, conv_b, conv_ln_g, conv_ln_b, w_ret_o, w_conv_o, w_out, ln2_g, ln2_b, ffn2_w_gate, ffn2_w_up, ffn2_w_down, ln3_g, ln3_b, loss_target, m_ffn1_w_gate, m_ffn1_w_up, m_ffn1_w_down, m_ln1_g, m_ln1_b, m_w_in, m_b_in, m_ret_gn_g, m_conv_k, m_conv_b, m_conv_ln_g, m_conv_ln_b, m_w_ret_o, m_w_conv_o, m_w_out, m_ln2_g, m_ln2_b, m_ffn2_w_gate, m_ffn2_w_up, m_ffn2_w_down, m_ln3_g, m_ln3_b, v_ffn1_w_gate, v_ffn1_w_up, v_ffn1_w_down, v_ln1_g, v_ln1_b, v_w_in, v_b_in, v_ret_gn_g, v_conv_k, v_conv_b, v_conv_ln_g, v_conv_ln_b, v_w_ret_o, v_w_conv_o, v_w_out, v_ln2_g, v_ln2_b, v_ffn2_w_gate, v_ffn2_w_up, v_ffn2_w_down, v_ln3_g, v_ln3_b):
    given = dict(locals())
    wts = {n: given[n] for n in WEIGHTS}
    mom = {n: given['m_' + n] for n in WEIGHTS}
    var = {n: given['v_' + n] for n in WEIGHTS}

    def shard2d(a):
        return a.reshape(a.shape[-3] * a.shape[-2] if a.ndim == 4 else a.shape[-2], a.shape[-1])

    shapes = [shard2d(wts[n]).shape for n in BIG]
    n_el = sum(s[0] * s[1] for s in shapes)
    R = -(-n_el // (PACK_W * 16)) * 16

    gathered = _all_gather("gather_weights", _pack([shard2d(wts[n]) for n in BIG], R, BF16))
    per_dev = [_unpack(gathered[j], shapes) for j in range(N_DEV)]
    W = {}
    for i, n in enumerate(BIG):
        W[n] = jnp.concatenate([per_dev[j][i] for j in range(N_DEV)], axis=1 if n in COL_SHARDED else 0)
    P = {n: wts[n].reshape(1, -1) for n in SMALL}

    loss, grad_x, G, S = _local_step(x[0], loss_target[0], W, P)

    def shard_of(n, j):
        g = G[n]
        if n in COL_SHARDED:
            cs = g.shape[1] // N_DEV
            return g[:, j * cs:(j + 1) * cs]
        rs = g.shape[0] // N_DEV
        return g[j * rs:(j + 1) * rs]

    packed_g = jnp.stack([_pack([shard_of(n, j) for n in BIG], R, BF16) for j in range(N_DEV)])
    parts = _exchange("exchange_grads", packed_g)
    tb = _row_tile(R, 16, 128)
    pk = lambda d: _pack([shard2d(d[n]) for n in BIG], R, F32)
    big = _adamw("adamw_matrices", parts, pk(wts), pk(mom), pk(var), tb)
    big = [dict(zip(BIG, _unpack(b, shapes))) for b in big]

    small_shapes = [(1, wts[n].shape[-1]) for n in SMALL]
    ns = sum(s[1] for s in small_shapes)
    RS = -(-ns // (128 * 8)) * 8
    pks = lambda d: jnp.pad(jnp.concatenate([d[n].reshape(-1) for n in SMALL]), (0, RS * 128 - ns)).reshape(RS, 128)
    small_parts = _all_gather("gather_vector_grads", pks(S))
    small = _adamw("adamw_vectors", small_parts, pks(wts), pks(mom), pks(var), RS)
    small = [dict(zip(SMALL, _unpack(b, small_shapes))) for b in small]

    loss = lax.psum(loss, ("x", "y", "c"))
    outs = [loss, grad_x[None]]
    for k in range(4):
        for n in WEIGHTS:
            src = big[k] if n in BIG else small[k]
            outs.append(src[n].reshape(wts[n].shape))
    return tuple(outs)
```

```python
import functools
import math

import jax
import jax.numpy as jnp
from jax import lax
from jax.experimental import pallas as pl
from jax.experimental.pallas import tpu as pltpu

F32 = jnp.float32
BF16 = jnp.bfloat16

N_DEV = 8
LN_EPS = 1e-5
ALPHA = 2.0 ** 0.25
RET_DK = 128
RET_DV = 256
RET_CHUNK = 128
ROPE_BASE = 10000.0
CONV_WIDTH = 31
HALO = 32
ADAM_LR, ADAM_B1, ADAM_B2, ADAM_EPS, ADAM_WD, ADAM_STEP = 0.001, 0.9, 0.999, 1e-08, 0.01, 10
VMEM_LIMIT = 52 * 1024 * 1024

BIG = ['ffn1_w_gate', 'ffn1_w_up', 'ffn1_w_down', 'w_in', 'w_ret_o', 'w_conv_o', 'w_out',
       'ffn2_w_gate', 'ffn2_w_up', 'ffn2_w_down', 'conv_k']
COL_SHARDED = {'ffn1_w_gate', 'ffn1_w_up', 'w_in', 'ffn2_w_gate', 'ffn2_w_up', 'conv_k'}
SMALL = ['ln1_g', 'ln1_b', 'b_in', 'ret_gn_g', 'conv_b', 'conv_ln_g', 'conv_ln_b', 'ln2_g', 'ln2_b', 'ln3_g', 'ln3_b']
WEIGHTS = ['ffn1_w_gate', 'ffn1_w_up', 'ffn1_w_down', 'ln1_g', 'ln1_b', 'w_in', 'b_in', 'ret_gn_g', 'conv_k', 'conv_b',
           'conv_ln_g', 'conv_ln_b', 'w_ret_o', 'w_conv_o', 'w_out', 'ln2_g', 'ln2_b', 'ffn2_w_gate', 'ffn2_w_up',
           'ffn2_w_down', 'ln3_g', 'ln3_b']


def _params(sem=None):
    return pltpu.CompilerParams(dimension_semantics=sem, vmem_limit_bytes=VMEM_LIMIT)


def _sigmoid(x):
    return jax.nn.sigmoid(x)


def _dsilu(x, sg):
    return sg * (1.0 + x * (1.0 - sg))


def _fit(dim, want):
    if dim <= want:
        return dim
    return max(t for t in range(128, want + 1, 128) if dim % t == 0)


def _dot(a, b, ta=False, tb=False):
    dn = (((0,) if ta else (1,), (1,) if tb else (0,)), ((), ()))
    return lax.dot_general(a, b, dn, preferred_element_type=F32)


def _mm(name, As, Bs, prods, epi, out_dtypes, *, ta=False, tb=False, tm, tn, tk, extras=(), i_outer=True,
        b3=False, o3=False):
    a0, b0 = As[0], Bs[0]
    M, K = (a0.shape[1], a0.shape[0]) if ta else a0.shape
    if b3:
        S, rows, cs = b0.shape
        N = rows if tb else S * cs
        assert K == (S * cs if tb else rows)
        tn, tk = (tn, cs) if tb else (cs, tk)
    else:
        N = b0.shape[0] if tb else b0.shape[1]
    tm, tn, tk = _fit(M, tm), _fit(N, tn), _fit(K, tk)
    assert M % tm == 0 and N % tn == 0 and K % tk == 0, (name, M, N, K, tm, tn, tk)
    gi, gj, gk = M // tm, N // tn, K // tk
    grid = (gi, gj, gk) if i_outer else (gj, gi, gk)

    def ij(g0, g1):
        return (g0, g1) if i_outer else (g1, g0)

    def amap(g0, g1, k):
        i, _ = ij(g0, g1)
        return (k, i) if ta else (i, k)

    def bmap(g0, g1, k):
        _, j = ij(g0, g1)
        return (j, k) if tb else (k, j)

    def bmap3(g0, g1, k):
        _, j = ij(g0, g1)
        return (k, j, 0) if tb else (j, k, 0)

    in_specs = [pl.BlockSpec((tk, tm) if ta else (tm, tk), amap) for _ in As]
    if b3:
        in_specs += [pl.BlockSpec((None, tn, tk) if tb else (None, tk, tn), bmap3) for _ in Bs]
    else:
        in_specs += [pl.BlockSpec((tn, tk) if tb else (tk, tn), bmap) for _ in Bs]
    args = list(As) + list(Bs)
    for arr, kind, coloff in extras:
        assert coloff % tn == 0
        off = coloff // tn
        if kind == 'mn':
            in_specs.append(pl.BlockSpec((tm, tn), lambda g0, g1, k, off=off: (ij(g0, g1)[0], ij(g0, g1)[1] + off)))
        else:
            in_specs.append(pl.BlockSpec((1, tn), lambda g0, g1, k, off=off: (0, ij(g0, g1)[1] + off)))
        args.append(arr)
    if o3:
        out_shape = [jax.ShapeDtypeStruct((gj, M, tn), dt) for dt in out_dtypes]
        out_specs = [pl.BlockSpec((None, tm, tn), lambda g0, g1, k: (ij(g0, g1)[1], ij(g0, g1)[0], 0))
                     for _ in out_dtypes]
    else:
        out_shape = [jax.ShapeDtypeStruct((M, N), dt) for dt in out_dtypes]
        out_specs = [pl.BlockSpec((tm, tn), lambda g0, g1, k: ij(g0, g1)) for _ in out_dtypes]
    n_a, n_b, n_e, n_o, n_p = len(As), len(Bs), len(extras), len(out_dtypes), len(prods)

    def body(*refs):
        a_refs = refs[:n_a]
        b_refs = refs[n_a:n_a + n_b]
        e_refs = refs[n_a + n_b:n_a + n_b + n_e]
        o_refs = refs[n_a + n_b + n_e:n_a + n_b + n_e + n_o]
        acc_refs = refs[n_a + n_b + n_e + n_o:]
        k = pl.program_id(2)

        @pl.when(k == 0)
        def _():
            for acc in acc_refs:
                acc[...] = jnp.zeros_like(acc)

        for p, terms in enumerate(prods):
            for ai, bi in terms:
                acc_refs[p][...] += _dot(a_refs[ai][...], b_refs[bi][...], ta, tb)

        @pl.when(k == gk - 1)
        def _():
            res = epi([acc[...] for acc in acc_refs], *[e[...] for e in e_refs])
            for o, r in zip(o_refs, res):
                o[...] = r.astype(o.dtype)

    res = pl.pallas_call(
        body, name=name, grid=grid, in_specs=in_specs, out_specs=out_specs, out_shape=out_shape,
        scratch_shapes=[pltpu.VMEM((tm, tn), F32) for _ in range(n_p)],
        compiler_params=_params(("arbitrary", "arbitrary", "arbitrary")),
    )(*args)
    return res


def _rows(name, fn, ins, outs, *, T, tb):
    tb = min(tb, T)
    assert T % tb == 0
    in_specs, args = [], []
    for arr, kind, width, cb in ins:
        if kind == 'r':
            in_specs.append(pl.BlockSpec((tb, width), lambda i, cb=cb: (i, cb)))
        else:
            in_specs.append(pl.BlockSpec((1, width), lambda i, cb=cb: (0, cb)))
        args.append(arr)
    out_shape, out_specs = [], []
    for kind, width, dtype in outs:
        if kind == 'r':
            out_shape.append(jax.ShapeDtypeStruct((T, width), dtype))
            out_specs.append(pl.BlockSpec((tb, width), lambda i: (i, 0)))
        elif kind == 'c':
            out_shape.append(jax.ShapeDtypeStruct((T, 1), dtype))
            out_specs.append(pl.BlockSpec((tb, 1), lambda i: (i, 0)))
        else:
            out_shape.append(jax.ShapeDtypeStruct((1, width), F32))
            out_specs.append(pl.BlockSpec((1, width), lambda i: (0, 0)))
    n_in = len(ins)

    def body(*refs):
        i = pl.program_id(0)
        vals = fn(*[r[...] for r in refs[:n_in]])
        for (kind, _, _), o, v in zip(outs, refs[n_in:], vals):
            if kind == 'a':
                @pl.when(i == 0)
                def _(o=o):
                    o[...] = jnp.zeros_like(o)

                o[...] += v
            else:
                o[...] = v.astype(o.dtype)

    return pl.pallas_call(
        body, name=name, grid=(T // tb,), in_specs=in_specs, out_specs=out_specs, out_shape=out_shape,
        compiler_params=_params(("arbitrary",)),
    )(*args)


def _colsum(v):
    return jnp.sum(v, axis=0, keepdims=True)


def _ln_stats(z):
    mu = jnp.mean(z, axis=-1, keepdims=True)
    d = z - mu
    var = jnp.mean(d * d, axis=-1, keepdims=True)
    rstd = lax.rsqrt(var + LN_EPS)
    return d * rstd, rstd


def _ln_bwd_math(dy, xhat, rstd, g):
    dxh = dy * g
    m1 = jnp.mean(dxh, axis=-1, keepdims=True)
    m2 = jnp.mean(dxh * xhat, axis=-1, keepdims=True)
    return rstd * (dxh - m1 - xhat * m2)


def _ln_fwd(name, z, g, b, T, D):
    def fn(z, g, b):
        xhat, rstd = _ln_stats(z)
        y = xhat * g + b
        return [y, y, xhat, rstd]

    return _rows(name, fn, [(z, 'r', D, 0), (g, 'v', D, 0), (b, 'v', D, 0)],
                 [('r', D, F32), ('r', D, BF16), ('r', D, F32), ('c', 1, F32)], T=T, tb=512)


def _ln_bwd(name, dy, xhat, rstd, g, scale, T, D):
    def fn(dy, xhat, rstd, g):
        dz = _ln_bwd_math(dy, xhat, rstd, g)
        return [dz, dz * scale, _colsum(dy * xhat), _colsum(dy)]

    return _rows(name, fn, [(dy, 'r', D, 0), (xhat, 'r', D, 0), (rstd, 'r', 1, 0), (g, 'v', D, 0)],
                 [('r', D, F32), ('r', D, BF16), ('a', D, F32), ('a', D, F32)], T=T, tb=512)


def _ln_loss_bwd(name, z, g, b, tgt, T, D):
    def fn(z, g, b, tgt):
        xhat, rstd = _ln_stats(z)
        err = xhat * g + b - tgt
        row_loss = 0.5 * jnp.mean(err * err, axis=-1, keepdims=True)
        loss = jnp.broadcast_to(jnp.sum(row_loss, axis=0, keepdims=True), (1, 128))
        dy = err * (1.0 / D)
        dz = _ln_bwd_math(dy, xhat, rstd, g)
        return [dz, dz * 0.5, _colsum(dy * xhat), _colsum(dy), loss]

    return _rows(name, fn, [(z, 'r', D, 0), (g, 'v', D, 0), (b, 'v', D, 0), (tgt, 'r', D, 0)],
                 [('r', D, F32), ('r', D, BF16), ('a', D, F32), ('a', D, F32), ('a', 128, F32)], T=T, tb=512)


def _ffn_fwd(tag, xb, x, wg, wu, wd):
    def epi_gu(accs):
        a, b = accs
        return [a, b, a * _sigmoid(a) * b]

    a, b, s = _mm(tag + "_gate_up", [xb], [wg, wu], [[(0, 0)], [(0, 1)]], epi_gu, [F32, F32, BF16],
                  tm=512, tn=1408, tk=1024)

    def epi_down(accs, xres):
        return [ALPHA * xres + 0.5 * accs[0]]

    (z,) = _mm(tag + "_down", [s], [wd], [[(0, 0)]], epi_down, [F32], tm=512, tn=1024, tk=1408,
               extras=[(x, 'mn', 0)])
    return a, b, s, z


def _ffn_bwd(tag, dzh, dz, xb, a, b, s, wg, wu, wd, gdt):
    def epi_ds(accs, a, b):
        ds = accs[0]
        sg = _sigmoid(a)
        return [ds * b * _dsilu(a, sg), ds * a * sg]

    da, db = _mm(tag + "_ds", [dzh], [wd], [[(0, 0)]], epi_ds, [BF16, BF16], tb=True, tm=512, tn=1408, tk=1024,
                 extras=[(a, 'mn', 0), (b, 'mn', 0)])
    ident = lambda accs: accs
    (dwd,) = _mm(tag + "_dwd", [s], [dzh], [[(0, 0)]], ident, [gdt], ta=True, tm=1408, tn=1024, tk=1024)
    dwg, dwu = _mm(tag + "_dwgu", [xb], [da, db], [[(0, 0)], [(0, 1)]], ident, [gdt, gdt], ta=True,
                   tm=512, tn=1408, tk=1024)

    def epi_dx(accs, dzres):
        return [ALPHA * dzres + accs[0]]

    (dx,) = _mm(tag + "_dx", [da, db], [wg, wu], [[(0, 0), (1, 1)]], epi_dx, [F32], tb=True,
                tm=512, tn=1024, tk=1408, extras=[(dz, 'mn', 0)])
    return dwg, dwu, dwd, dx


def _ret_tables(H, T):
    C = RET_CHUNK
    log_g = jnp.log(1.0 - jnp.exp2(-5.0 - jnp.arange(H, dtype=F32)))
    idx = jnp.arange(C, dtype=F32)
    diff = idx[:, None] - idx[None, :]
    dm = jnp.where(diff[None] >= 0, jnp.exp(jnp.maximum(diff, 0.0)[None] * log_g[:, None, None]), 0.0)
    xi = jnp.exp((idx[None, :] + 1.0) * log_g[:, None])[:, :, None]
    zeta = jnp.exp((C - 1.0 - idx)[None, :] * log_g[:, None])[:, :, None]
    gc = jnp.broadcast_to(jnp.exp(C * log_g)[:, None, None], (H, 1, RET_DV))
    half = RET_DK // 2
    freqs = ROPE_BASE ** (-jnp.arange(half, dtype=F32) / half)
    ang = jnp.arange(T, dtype=F32)[:, None] * freqs[None, :]
    cos, sin = jnp.cos(ang), jnp.sin(ang)
    cosf = jnp.concatenate([cos, cos], axis=1)
    sins = jnp.concatenate([-sin, sin], axis=1)
    return dm, xi, zeta, gc, cosf, sins


def _rot(x, cosf, sins):
    return x * cosf + pltpu.roll(x, RET_DK // 2, 1) * sins


def _rot_bwd(dy, cosf, sins):
    return dy * cosf + pltpu.roll(dy * sins, RET_DK // 2, 1)


RET_HB = 4


def _ret_specs(H, HB, rev, NC):
    C, G = RET_CHUNK, H // HB
    nn = (lambda n: NC - 1 - n) if rev else (lambda n: n)
    return [
        pl.BlockSpec((C, HB * RET_DK), lambda h, n: (nn(n), h)),
        pl.BlockSpec((C, HB * RET_DK), lambda h, n: (nn(n), G + h)),
        pl.BlockSpec((C, HB * RET_DV), lambda h, n: (nn(n), G + h)),
        pl.BlockSpec((C, HB * RET_DV), lambda h, n: (nn(n), 2 * G + h)),
        pl.BlockSpec((C, RET_DK), lambda h, n: (nn(n), 0)),
        pl.BlockSpec((C, RET_DK), lambda h, n: (nn(n), 0)),
        pl.BlockSpec((1, HB * RET_DV), lambda h, n: (0, h)),
        pl.BlockSpec((HB, C, C), lambda h, n: (h, 0, 0)),
        pl.BlockSpec((HB, C, 1), lambda h, n: (h, 0, 0)),
        pl.BlockSpec((HB, C, 1), lambda h, n: (h, 0, 0)),
        pl.BlockSpec((HB, 1, RET_DV), lambda h, n: (h, 0, 0)),
    ]


def _ret_fwd(proj, gn_g, tabs, H, T):
    C, NC = RET_CHUNK, T // RET_CHUNK
    HB = min(RET_HB, H)
    dm, xi, zeta, gc, cosf, sins = tabs
    scale = RET_DK ** -0.5

    def body(q_ref, k_ref, v_ref, g_ref, cos_ref, sin_ref, gn_ref, dm_ref, xi_ref, zt_ref, gc_ref,
             r_ref, ri_ref, st_ref, state):
        @pl.when(pl.program_id(1) == 0)
        def _():
            state[...] = jnp.zeros_like(state)

        cs, sn = cos_ref[...], sin_ref[...]
        for hh in range(HB):
            qk = slice(hh * RET_DK, (hh + 1) * RET_DK)
            vv = slice(hh * RET_DV, (hh + 1) * RET_DV)
            qr = _rot(q_ref[:, qk], cs, sn) * scale
            kr = _rot(k_ref[:, qk], cs, sn)
            qb, kb, vb = qr.astype(BF16), kr.astype(BF16), v_ref[:, vv].astype(BF16)
            st = state[hh]
            stb = st.astype(BF16)
            s = _dot(qb, kb, tb=True) * dm_ref[hh]
            r = _dot(s.astype(BF16), vb) + _dot(qb, stb) * xi_ref[hh]
            st_ref[hh] = stb
            state[hh] = gc_ref[hh] * st + _dot((kr * zt_ref[hh]).astype(BF16), vb, ta=True)
            rhat, _ = _ln_stats(r)
            g = g_ref[:, vv]
            r_ref[:, vv] = r
            ri_ref[:, vv] = (g * _sigmoid(g) * (rhat * gn_ref[:, vv])).astype(BF16)

    VW = H * RET_DV
    return pl.pallas_call(
        body, name="ret_fwd", grid=(H // HB, NC), in_specs=_ret_specs(H, HB, False, NC),
        out_specs=[pl.BlockSpec((C, HB * RET_DV), lambda h, n: (n, h)),
                   pl.BlockSpec((C, HB * RET_DV), lambda h, n: (n, h)),
                   pl.BlockSpec((HB, None, RET_DK, RET_DV), lambda h, n: (h, n, 0, 0))],
        out_shape=[jax.ShapeDtypeStruct((T, VW), F32), jax.ShapeDtypeStruct((T, VW), BF16),
                   jax.ShapeDtypeStruct((H, NC, RET_DK, RET_DV), BF16)],
        scratch_shapes=[pltpu.VMEM((HB, RET_DK, RET_DV), F32)],
        compiler_params=_params(("arbitrary", "arbitrary")),
    )(proj, proj, proj, proj, cosf, sins, gn_g, dm, xi, zeta, gc)


def _ret_bwd(dri, r, states, proj, gn_g, tabs, H, T):
    C, NC = RET_CHUNK, T // RET_CHUNK
    HB = min(RET_HB, H)
    dm, xi, zeta, gc, cosf, sins = tabs
    scale = RET_DK ** -0.5

    def body(q_ref, k_ref, v_ref, g_ref, cos_ref, sin_ref, gn_ref, dm_ref, xi_ref, zt_ref, gc_ref,
             dri_ref, r_ref, st_ref, dq_ref, dk_ref, dv_ref, dg_ref, dgn_ref, dstate):
        @pl.when(pl.program_id(1) == 0)
        def _():
            dstate[...] = jnp.zeros_like(dstate)
            dgn_ref[...] = jnp.zeros_like(dgn_ref)

        cs, sn = cos_ref[...], sin_ref[...]
        for hh in range(HB):
            qk = slice(hh * RET_DK, (hh + 1) * RET_DK)
            vv = slice(hh * RET_DV, (hh + 1) * RET_DV)
            qr = _rot(q_ref[:, qk], cs, sn) * scale
            kr = _rot(k_ref[:, qk], cs, sn)
            qb, kb, vb = qr.astype(BF16), kr.astype(BF16), v_ref[:, vv].astype(BF16)
            xi_c, zt_c, dmask = xi_ref[hh], zt_ref[hh], dm_ref[hh]
            rhat, rstd = _ln_stats(r_ref[:, vv])
            g, gn, dpre = g_ref[:, vv], gn_ref[:, vv], dri_ref[:, vv]
            sg = _sigmoid(g)
            dg_ref[:, vv] = (dpre * (rhat * gn) * _dsilu(g, sg)).astype(BF16)
            drn = dpre * (g * sg)
            dgn_ref[:, vv] += _colsum(drn * rhat)
            drb = _ln_bwd_math(drn, rhat, rstd, gn).astype(BF16)
            ds1 = dstate[hh]
            ds1b = ds1.astype(BF16)
            sb = (_dot(qb, kb, tb=True) * dmask).astype(BF16)
            kzb = (kr * zt_c).astype(BF16)
            dv_ref[:, vv] = (_dot(sb, drb, ta=True) + _dot(kzb, ds1b)).astype(BF16)
            dsb = (_dot(drb, vb, tb=True) * dmask).astype(BF16)
            dq = _dot(dsb, kb) + _dot(drb, st_ref[hh], tb=True) * xi_c
            dk = _dot(dsb, qb, ta=True) + _dot(vb, ds1b, tb=True) * zt_c
            dstate[hh] = gc_ref[hh] * ds1 + _dot((qr * xi_c).astype(BF16), drb, ta=True)
            dq_ref[:, qk] = _rot_bwd(dq * scale, cs, sn).astype(BF16)
            dk_ref[:, qk] = _rot_bwd(dk, cs, sn).astype(BF16)

    VW, QW = H * RET_DV, H * RET_DK
    rv = lambda n: NC - 1 - n
    in_specs = _ret_specs(H, HB, True, NC) + [
        pl.BlockSpec((C, HB * RET_DV), lambda h, n: (rv(n), h)),
        pl.BlockSpec((C, HB * RET_DV), lambda h, n: (rv(n), h)),
        pl.BlockSpec((HB, None, RET_DK, RET_DV), lambda h, n: (h, rv(n), 0, 0)),
    ]
    return pl.pallas_call(
        body, name="ret_bwd", grid=(H // HB, NC), in_specs=in_specs,
        out_specs=[pl.BlockSpec((C, HB * RET_DK), lambda h, n: (rv(n), h)),
                   pl.BlockSpec((C, HB * RET_DK), lambda h, n: (rv(n), h)),
                   pl.BlockSpec((C, HB * RET_DV), lambda h, n: (rv(n), h)),
                   pl.BlockSpec((C, HB * RET_DV), lambda h, n: (rv(n), h)),
                   pl.BlockSpec((1, HB * RET_DV), lambda h, n: (0, h))],
        out_shape=[jax.ShapeDtypeStruct((T, QW), BF16), jax.ShapeDtypeStruct((T, QW), BF16),
                   jax.ShapeDtypeStruct((T, VW), BF16), jax.ShapeDtypeStruct((T, VW), BF16),
                   jax.ShapeDtypeStruct((1, VW), F32)],
        scratch_shapes=[pltpu.VMEM((HB, RET_DK, RET_DV), F32)],
        compiler_params=_params(("arbitrary", "arbitrary")),
    )(proj, proj, proj, proj, cosf, sins, gn_g, dm, xi, zeta, gc, dri, r, states)


CONV_CW = 128
CONV_TB = 512


def _conv_fwd(proj, kpad, bias, off_a, CC, T):
    tb, cw = min(CONV_TB, T), CONV_CW
    hb = tb // HALO
    ca, cb = off_a // cw, (off_a + CC) // cw

    def body(a_ref, b_ref, ap_ref, bp_ref, k_ref, bias_ref, u1_ref, win):
        i = pl.program_id(0)
        keep = (i > 0).astype(F32)
        win[0:HALO, :] = ap_ref[...] * _sigmoid(bp_ref[...]) * keep
        win[HALO:, :] = a_ref[...] * _sigmoid(b_ref[...])
        acc = jnp.broadcast_to(bias_ref[...], (tb, cw))
        for w in range(CONV_WIDTH):
            acc = acc + k_ref[w:w + 1, :] * win[pl.ds(HALO - (CONV_WIDTH - 1) + w, tb), :]
        u1_ref[...] = acc

    prev = lambda i: jnp.maximum(i * hb - 1, 0)
    return pl.pallas_call(
        body, name="conv_fwd", grid=(T // tb, CC // cw),
        in_specs=[pl.BlockSpec((tb, cw), lambda i, c: (i, ca + c)),
                  pl.BlockSpec((tb, cw), lambda i, c: (i, cb + c)),
                  pl.BlockSpec((HALO, cw), lambda i, c: (prev(i), ca + c)),
                  pl.BlockSpec((HALO, cw), lambda i, c: (prev(i), cb + c)),
                  pl.BlockSpec((HALO, cw), lambda i, c: (0, c)),
                  pl.BlockSpec((1, cw), lambda i, c: (0, c))],
        out_specs=pl.BlockSpec((tb, cw), lambda i, c: (i, c)),
        out_shape=jax.ShapeDtypeStruct((T, CC), F32),
        scratch_shapes=[pltpu.VMEM((tb + HALO, cw), F32)],
        compiler_params=_params(("arbitrary", "arbitrary")),
    )(proj, proj, proj, proj, kpad, bias)


def _conv_bwd(du1, proj, kpad, off_a, CC, T):
    tb, cw = min(CONV_TB, T), CONV_CW
    hb = tb // HALO
    nt = T // tb
    ca, cb = off_a // cw, (off_a + CC) // cw

    def body(d_ref, dn_ref, a_ref, b_ref, ap_ref, bp_ref, k_ref, da_ref, db_ref, dk_ref, winu, wind):
        i = pl.program_id(1)
        a, b = a_ref[...], b_ref[...]
        sgb = _sigmoid(b)
        winu[0:HALO, :] = ap_ref[...] * _sigmoid(bp_ref[...]) * (i > 0).astype(F32)
        winu[HALO:, :] = a * sgb
        d = d_ref[...]
        wind[0:tb, :] = d
        wind[tb:, :] = dn_ref[...] * (i < nt - 1).astype(F32)

        @pl.when(i == 0)
        def _():
            dk_ref[...] = jnp.zeros_like(dk_ref)

        du0 = jnp.zeros((tb, cw), F32)
        for w in range(CONV_WIDTH):
            du0 = du0 + k_ref[w:w + 1, :] * wind[pl.ds(CONV_WIDTH - 1 - w, tb), :]
            dk_ref[w:w + 1, :] += _colsum(winu[pl.ds(HALO - (CONV_WIDTH - 1) + w, tb), :] * d)
        da_ref[...] = (du0 * sgb).astype(BF16)
        db_ref[...] = (du0 * a * sgb * (1.0 - sgb)).astype(BF16)

    prev = lambda i: jnp.maximum(i * hb - 1, 0)
    nxt = lambda i: jnp.minimum((i + 1) * hb, T // HALO - 1)
    return pl.pallas_call(
        body, name="conv_bwd", grid=(CC // cw, nt),
        in_specs=[pl.BlockSpec((tb, cw), lambda c, i: (i, c)),
                  pl.BlockSpec((HALO, cw), lambda c, i: (nxt(i), c)),
                  pl.BlockSpec((tb, cw), lambda c, i: (i, ca + c)),
                  pl.BlockSpec((tb, cw), lambda c, i: (i, cb + c)),
                  pl.BlockSpec((HALO, cw), lambda c, i: (prev(i), ca + c)),
                  pl.BlockSpec((HALO, cw), lambda c, i: (prev(i), cb + c)),
                  pl.BlockSpec((HALO, cw), lambda c, i: (0, c))],
        out_specs=[pl.BlockSpec((tb, cw), lambda c, i: (i, c)),
                   pl.BlockSpec((tb, cw), lambda c, i: (i, c)),
                   pl.BlockSpec((HALO, cw), lambda c, i: (0, c))],
        out_shape=[jax.ShapeDtypeStruct((T, CC), BF16), jax.ShapeDtypeStruct((T, CC), BF16),
                   jax.ShapeDtypeStruct((HALO, CC), F32)],
        scratch_shapes=[pltpu.VMEM((tb + HALO, cw), F32), pltpu.VMEM((tb + HALO, cw), F32)],
        compiler_params=_params(("arbitrary", "arbitrary")),
    )(du1, du1, proj, proj, proj, proj, kpad)


def _local_step(x, tgt, W, P, gdt=BF16):
    T, D = x.shape
    F = W['ffn1_w_gate'].shape[1]
    VW = P['ret_gn_g'].shape[1]
    H = VW // RET_DV
    QW = H * RET_DK
    CC = P['conv_b'].shape[1]
    off_glu = 2 * QW + 2 * VW
    off_gate = off_glu + 2 * CC
    ident = lambda accs: accs
    xb = x.astype(BF16)

    a1, b1, s1, z1 = _ffn_fwd("ffn1", xb, x, W['ffn1_w_gate'], W['ffn1_w_up'], W['ffn1_w_down'])
    x1, x1b, xh1, rs1 = _ln_fwd("ln1", z1, P['ln1_g'], P['ln1_b'], T, D)

    (proj,) = _mm("w_in", [x1b], [W['w_in']], [[(0, 0)]], lambda accs, bias: [accs[0] + bias], [F32],
                  tm=512, tn=0, tk=1024, extras=[(P['b_in'], 'n', 0)], i_outer=False, b3=True)
    tabs = _ret_tables(H, T)
    r, ret_in, states = _ret_fwd(proj, P['ret_gn_g'], tabs, H, T)
    kpad = jnp.pad(W['conv_k'].astype(F32), ((0, HALO - CONV_WIDTH), (0, 0)))
    u1 = _conv_fwd(proj, kpad, P['conv_b'], off_glu, CC, T)

    def conv_ln(u1, g, b):
        xhat, rstd = _ln_stats(u1)
        u2 = xhat * g + b
        return [xhat, rstd, u2 * _sigmoid(u2)]

    xhc, rsc, u3 = _rows("conv_ln", conv_ln, [(u1, 'r', CC, 0), (P['conv_ln_g'], 'v', CC, 0), (P['conv_ln_b'], 'v', CC, 0)],
                         [('r', CC, F32), ('c', 1, F32), ('r', CC, BF16)], T=T, tb=512)
    (ret_out,) = _mm("ret_o", [ret_in], [W['w_ret_o']], [[(0, 0)]], ident, [F32], tm=512, tn=1024, tk=2048)

    def epi_merge(accs, ret_out, gr, gc):
        conv_out = accs[0]
        return [conv_out, _sigmoid(gr) * ret_out + _sigmoid(gc) * conv_out]

    conv_out, merged = _mm("conv_o_merge", [u3], [W['w_conv_o']], [[(0, 0)]], epi_merge, [F32, BF16],
                           tm=512, tn=D, tk=1024,
                           extras=[(ret_out, 'mn', 0), (proj, 'mn', off_gate), (proj, 'mn', off_gate + D)])
    (z2,) = _mm("w_out", [merged], [W['w_out']], [[(0, 0)]], lambda accs, xr: [ALPHA * xr + accs[0]], [F32],
                tm=512, tn=1024, tk=1024, extras=[(x1, 'mn', 0)])
    x2, x2b, xh2, rs2 = _ln_fwd("ln2", z2, P['ln2_g'], P['ln2_b'], T, D)
    a2, b2, s2, z3 = _ffn_fwd("ffn2", x2b, x2, W['ffn2_w_gate'], W['ffn2_w_up'], W['ffn2_w_down'])
    dz3, dz3h, g_ln3_g, g_ln3_b, loss = _ln_loss_bwd("ln3_loss", z3, P['ln3_g'], P['ln3_b'], tgt, T, D)

    G, S = {}, {'ln3_g': g_ln3_g, 'ln3_b': g_ln3_b}
    G['ffn2_w_gate'], G['ffn2_w_up'], G['ffn2_w_down'], dy2 = _ffn_bwd(
        "ffn2b", dz3h, dz3, x2b, a2, b2, s2, W['ffn2_w_gate'], W['ffn2_w_up'], W['ffn2_w_down'], gdt)
    dz2, dz2b, S['ln2_g'], S['ln2_b'] = _ln_bwd("ln2b", dy2, xh2, rs2, P['ln2_g'], 1.0, T, D)

    (G['w_out'],) = _mm("d_w_out", [merged], [dz2b], [[(0, 0)]], ident, [gdt], ta=True, tm=1024, tn=1024, tk=1024)

    def epi_dmerge(accs, ret_out, conv_out, gr, gc):
        dm_ = accs[0]
        sr, sc = _sigmoid(gr), _sigmoid(gc)
        return [dm_ * sr, dm_ * sc, dm_ * ret_out * sr * (1.0 - sr), dm_ * conv_out * sc * (1.0 - sc)]

    dret_out, dconv_out, dgate_r, dgate_c = _mm(
        "d_merge", [dz2b], [W['w_out']], [[(0, 0)]], epi_dmerge, [BF16, BF16, BF16, BF16], tb=True,
        tm=512, tn=D, tk=1024,
        extras=[(ret_out, 'mn', 0), (conv_out, 'mn', 0), (proj, 'mn', off_gate), (proj, 'mn', off_gate + D)])
    (G['w_ret_o'],) = _mm("d_w_ret_o", [ret_in], [dret_out], [[(0, 0)]], ident, [gdt], ta=True, tm=1024, tn=1024, tk=1024)
    (G['w_conv_o'],) = _mm("d_w_conv_o", [u3], [dconv_out], [[(0, 0)]], ident, [gdt], ta=True, tm=1024, tn=1024, tk=1024)
    (dri,) = _mm("d_ret_in", [dret_out], [W['w_ret_o']], [[(0, 0)]], ident, [F32], tb=True, tm=512, tn=1024, tk=1024)
    dq, dk, dv, dg, S['ret_gn_g'] = _ret_bwd(dri, r, states, proj, P['ret_gn_g'], tabs, H, T)

    def epi_du2(accs, xhat, g, b):
        u2 = xhat * g + b
        return [accs[0] * _dsilu(u2, _sigmoid(u2))]

    (du2,) = _mm("d_u3", [dconv_out], [W['w_conv_o']], [[(0, 0)]], epi_du2, [F32], tb=True, tm=512, tn=CC, tk=1024,
                 extras=[(xhc, 'mn', 0), (P['conv_ln_g'], 'n', 0), (P['conv_ln_b'], 'n', 0)])

    def conv_ln_bwd(du2, xhat, rstd, g):
        du1 = _ln_bwd_math(du2, xhat, rstd, g)
        return [du1, _colsum(du2 * xhat), _colsum(du2), _colsum(du1)]

    du1, S['conv_ln_g'], S['conv_ln_b'], S['conv_b'] = _rows(
        "conv_ln_bwd", conv_ln_bwd, [(du2, 'r', CC, 0), (xhc, 'r', CC, 0), (rsc, 'r', 1, 0), (P['conv_ln_g'], 'v', CC, 0)],
        [('r', CC, F32), ('a', CC, F32), ('a', CC, F32), ('a', CC, F32)], T=T, tb=512)
    dglu_a, dglu_b, dkpad = _conv_bwd(du1, proj, kpad, off_glu, CC, T)
    G['conv_k'] = dkpad[:CONV_WIDTH].astype(gdt)

    dproj = jnp.concatenate([dq, dk, dv, dg, dglu_a, dglu_b, dgate_r, dgate_c], axis=1)
    IN_W = dproj.shape[1]
    (S['b_in'],) = _rows("d_b_in", lambda d: [_colsum(d.astype(F32))], [(dproj, 'r', IN_W, 0)], [('a', IN_W, F32)],
                         T=T, tb=256)
    (G['w_in'],) = _mm("d_w_in", [x1b], [dproj], [[(0, 0)]], ident, [gdt], ta=True, o3=True,
                       tm=1024, tn=W['w_in'].shape[2], tk=1024)
    (dy1,) = _mm("d_x1", [dproj], [W['w_in']], [[(0, 0)]], lambda accs, dzr: [ALPHA * dzr + accs[0]], [F32], tb=True,
                 b3=True, tm=512, tn=1024, tk=0, extras=[(dz2, 'mn', 0)])
    dz1, dz1h, S['ln1_g'], S['ln1_b'] = _ln_bwd("ln1b", dy1, xh1, rs1, P['ln1_g'], 0.5, T, D)
    G['ffn1_w_gate'], G['ffn1_w_up'], G['ffn1_w_down'], grad_x = _ffn_bwd(
        "ffn1b", dz1h, dz1, xb, a1, b1, s1, W['ffn1_w_gate'], W['ffn1_w_up'], W['ffn1_w_down'], gdt)
    return loss[0, 0], grad_x, G, S


MESH = pl.DeviceIdType.MESH
ANY = pl.BlockSpec(memory_space=pl.ANY)


def _coords():
    return lax.axis_index("x"), lax.axis_index("y"), lax.axis_index("c")


def _flip(k, x, y, c):
    return (1 - x if k & 4 else x, 1 - y if k & 2 else y, 1 - c if k & 1 else c)


def _lin(p):
    return 4 * p[0] + 2 * p[1] + p[2]


def _all_gather(name, blks):
    nb = len(blks)

    def body(*refs):
        x_refs, out_refs = refs[:nb], refs[nb:2 * nb]
        send_sems, recv_sems, local_sems = refs[2 * nb:]
        x, y, c = _coords()
        me, sibling = (x, y, c), (x, y, 1 - c)
        chips = [_flip(4, x, y, c), _flip(2, x, y, c), _flip(6, x, y, c)]

        def copy(k, w, block, to, src=None):
            slot = out_refs[w].at[_lin(block)]
            return pltpu.make_async_remote_copy(
                src_ref=slot if src is None else src, dst_ref=slot, send_sem=send_sems.at[k * nb + w],
                recv_sem=recv_sems.at[k * nb + w], device_id=to, device_id_type=MESH)

        mines = [pltpu.make_async_copy(x_refs[w], out_refs[w].at[_lin(me)], local_sems.at[w]) for w in range(nb)]
        first, passed = [], []
        for w in range(nb):
            mines[w].start()
            first.append(copy(0, w, me, sibling, src=x_refs[w]))
            first += [copy(1 + j, w, me, chip, src=x_refs[w]) for j, chip in enumerate(chips)]
        for cp in first:
            cp.start()
        for w in range(nb):
            for j, chip in enumerate(chips):
                copy(1 + j, w, chip, me).wait_recv()
                passed.append(copy(4 + j, w, chip, sibling))
                passed[-1].start()
        for w in range(nb):
            copy(0, w, sibling, me).wait_recv()
            for j, chip in enumerate(chips):
                copy(4 + j, w, (chip[0], chip[1], 1 - c), me).wait_recv()
        for cp in first + passed:
            cp.wait_send()
        for mine in mines:
            mine.wait()

    return pl.pallas_call(
        body, name=name, out_shape=[jax.ShapeDtypeStruct((N_DEV,) + b.shape, b.dtype) for b in blks],
        in_specs=[ANY] * nb, out_specs=[ANY] * nb,
        scratch_shapes=[pltpu.SemaphoreType.DMA((7 * nb,)), pltpu.SemaphoreType.DMA((7 * nb,)),
                        pltpu.SemaphoreType.DMA((nb,))],
        compiler_params=pltpu.CompilerParams(has_side_effects=True),
    )(*blks)


def _exchange(name, gs):
    nb = len(gs)

    def body(*refs):
        g_refs, out_refs = refs[:nb], refs[nb:2 * nb]
        send_sems, recv_sems, local_sems = refs[2 * nb:]
        x, y, c = _coords()
        me = _lin((x, y, c))

        def copy(k, w, landing):
            peer = _flip(k, x, y, c)
            src, dst = (me, _lin(peer)) if landing else (_lin(peer), me)
            return pltpu.make_async_remote_copy(
                src_ref=g_refs[w].at[src], dst_ref=out_refs[w].at[dst], send_sem=send_sems.at[(k - 1) * nb + w],
                recv_sem=recv_sems.at[(k - 1) * nb + w], device_id=peer, device_id_type=MESH)

        mines = [pltpu.make_async_copy(g_refs[w].at[me], out_refs[w].at[me], local_sems.at[w]) for w in range(nb)]
        sends = [copy(k, w, False) for w in range(nb) for k in range(1, N_DEV)]
        for mine in mines:
            mine.start()
        for cp in sends:
            cp.start()
        for w in range(nb):
            for k in range(1, N_DEV):
                copy(k, w, True).wait_recv()
        for cp in sends:
            cp.wait_send()
        for mine in mines:
            mine.wait()

    return pl.pallas_call(
        body, name=name, out_shape=[jax.ShapeDtypeStruct(g.shape, g.dtype) for g in gs],
        in_specs=[ANY] * nb, out_specs=[ANY] * nb,
        scratch_shapes=[pltpu.SemaphoreType.DMA((7 * nb,)), pltpu.SemaphoreType.DMA((7 * nb,)),
                        pltpu.SemaphoreType.DMA((nb,))],
        compiler_params=pltpu.CompilerParams(has_side_effects=True),
    )(*gs)


def _adamw(name, parts, w, m, v, tb):
    n, R, Wd = parts.shape
    assert R % tb == 0
    c1 = 1.0 - ADAM_B1 ** ADAM_STEP
    c2 = 1.0 - ADAM_B2 ** ADAM_STEP

    def body(p_ref, w_ref, m_ref, v_ref, g_ref, d_ref, nm_ref, nv_ref):
        g = p_ref[0].astype(F32)
        for s in range(1, n):
            g = g + p_ref[s].astype(F32)
        nm = ADAM_B1 * m_ref[...] + (1.0 - ADAM_B1) * g
        nv = ADAM_B2 * v_ref[...] + (1.0 - ADAM_B2) * (g * g)
        g_ref[...] = g
        nm_ref[...] = nm
        nv_ref[...] = nv
        d_ref[...] = -ADAM_LR * ((nm / c1) / (jnp.sqrt(nv / c2) + ADAM_EPS) + ADAM_WD * w_ref[...])

    row = pl.BlockSpec((tb, Wd), lambda i: (i, 0))
    return pl.pallas_call(
        body, name=name, grid=(R // tb,),
        in_specs=[pl.BlockSpec((n, tb, Wd), lambda i: (0, i, 0)), row, row, row],
        out_specs=[row, row, row, row], out_shape=[jax.ShapeDtypeStruct((R, Wd), F32)] * 4,
        compiler_params=_params(("arbitrary",)),
    )(parts, w, m, v)


def _unpack(buf, shapes):
    flat, out, off = buf.reshape(-1), [], 0
    for shp in shapes:
        n = shp[0] * shp[1]
        out.append(flat[off:off + n].reshape(shp))
        off += n
    return out


def _row_tile(R, unit, cap):
    best = unit
    for t in range(unit, cap + 1, unit):
        if R % t == 0:
            best = t
    return best


def kernel(x, ffn1_w_gate, ffn1_w_up, ffn1_w_down, ln1_g, ln1_b, w_in, b_in, ret_gn_g, conv_k, conv_b, conv_ln_g, conv_ln_b, w_ret_o, w_conv_o, w_out, ln2_g, ln2_b, ffn2_w_gate, ffn2_w_up, ffn2_w_down, ln3_g, ln3_b, loss_target, m_ffn1_w_gate, m_ffn1_w_up, m_ffn1_w_down, m_ln1_g, m_ln1_b, m_w_in, m_b_in, m_ret_gn_g, m_conv_k, m_conv_b, m_conv_ln_g, m_conv_ln_b, m_w_ret_o, m_w_conv_o, m_w_out, m_ln2_g, m_ln2_b, m_ffn2_w_gate, m_ffn2_w_up, m_ffn2_w_down, m_ln3_g, m_ln3_b, v_ffn1_w_gate, v_ffn1_w_up, v_ffn1_w_down, v_ln1_g, v_ln1_b, v_w_in, v_b_in, v_ret_gn_g, v_conv_k, v_conv_b, v_conv_ln_g, v_conv_ln_b, v_w_ret_o, v_w_conv_o, v_w_out, v_ln2_g, v_ln2_b, v_ffn2_w_gate, v_ffn2_w_up, v_ffn2_w_down, v_ln3_g, v_ln3_b):
    given = dict(locals())
    wts = {n: given[n] for n in WEIGHTS}
    mom = {n: given['m_' + n] for n in WEIGHTS}
    var = {n: given['v_' + n] for n in WEIGHTS}

    def shard2d(a):
        return a.reshape(a.shape[-3] * a.shape[-2] if a.ndim == 4 else a.shape[-2], a.shape[-1])

    gathered = dict(zip(BIG, _all_gather("gather_weights", [shard2d(wts[n]).astype(BF16) for n in BIG])))
    W = {}
    for n in BIG:
        g = gathered[n]
        if n == 'w_in':
            W[n] = g
        elif n in COL_SHARDED:
            W[n] = jnp.transpose(g, (1, 0, 2)).reshape(g.shape[1], N_DEV * g.shape[2])
        else:
            W[n] = g.reshape(N_DEV * g.shape[1], g.shape[2])
    P = {n: wts[n].reshape(1, -1) for n in SMALL}

    loss, grad_x, G, S = _local_step(x[0], loss_target[0], W, P)

    def by_owner(n):
        g = G[n]
        if n == 'w_in':
            return g
        if n in COL_SHARDED:
            return jnp.transpose(g.reshape(g.shape[0], N_DEV, g.shape[1] // N_DEV), (1, 0, 2))
        return g.reshape(N_DEV, g.shape[0] // N_DEV, g.shape[1])

    parts = dict(zip(BIG, _exchange("exchange_grads", [by_owner(n) for n in BIG])))
    res = {}
    for n in BIG:
        rows, cols = parts[n].shape[1:]
        tb = rows if rows % 16 else _row_tile(rows, 16, max(16, (256 * 1024) // cols))
        res[n] = _adamw("adamw_" + n, parts[n], shard2d(wts[n]), shard2d(mom[n]), shard2d(var[n]), tb)

    small_shapes = [(1, wts[n].shape[-1]) for n in SMALL]
    ns = sum(s[1] for s in small_shapes)
    RS = -(-ns // (128 * 8)) * 8
    pks = lambda d: jnp.pad(jnp.concatenate([d[n].reshape(-1) for n in SMALL]), (0, RS * 128 - ns)).reshape(RS, 128)
    (small_parts,) = _all_gather("gather_vector_grads", [pks(S)])
    small = _adamw("adamw_vectors", small_parts, pks(wts), pks(mom), pks(var), RS)
    for n, vals in zip(SMALL, zip(*[_unpack(b, small_shapes) for b in small])):
        res[n] = vals

    loss = lax.psum(loss, ("x", "y", "c"))
    outs = [loss, grad_x[None]]
    for k in range(4):
        for n in WEIGHTS:
            outs.append(res[n][k].reshape(wts[n].shape))
    return tuple(outs)
```

```python
import functools
import math

import jax
import jax.numpy as jnp
from jax import lax
from jax.experimental import pallas as pl
from jax.experimental.pallas import tpu as pltpu

F32 = jnp.float32
BF16 = jnp.bfloat16

N_DEV = 8
LN_EPS = 1e-5
ALPHA = 2.0 ** 0.25
RET_DK = 128
RET_DV = 256
RET_CHUNK = 128
ROPE_BASE = 10000.0
CONV_WIDTH = 31
HALO = 32
ADAM_LR, ADAM_B1, ADAM_B2, ADAM_EPS, ADAM_WD, ADAM_STEP = 0.001, 0.9, 0.999, 1e-08, 0.01, 10
VMEM_LIMIT = 52 * 1024 * 1024
MESH = pl.DeviceIdType.MESH
ANY = pl.BlockSpec(memory_space=pl.ANY)

BIG = ['ffn1_w_gate', 'ffn1_w_up', 'ffn1_w_down', 'w_in', 'w_ret_o', 'w_conv_o', 'w_out',
       'ffn2_w_gate', 'ffn2_w_up', 'ffn2_w_down', 'conv_k']
COL_SHARDED = {'ffn1_w_gate', 'ffn1_w_up', 'w_in', 'ffn2_w_gate', 'ffn2_w_up', 'conv_k'}
SMALL = ['ln1_g', 'ln1_b', 'b_in', 'ret_gn_g', 'conv_b', 'conv_ln_g', 'conv_ln_b', 'ln2_g', 'ln2_b', 'ln3_g', 'ln3_b']
WEIGHTS = ['ffn1_w_gate', 'ffn1_w_up', 'ffn1_w_down', 'ln1_g', 'ln1_b', 'w_in', 'b_in', 'ret_gn_g', 'conv_k', 'conv_b',
           'conv_ln_g', 'conv_ln_b', 'w_ret_o', 'w_conv_o', 'w_out', 'ln2_g', 'ln2_b', 'ffn2_w_gate', 'ffn2_w_up',
           'ffn2_w_down', 'ln3_g', 'ln3_b']


def _params(sem=None):
    return pltpu.CompilerParams(dimension_semantics=sem, vmem_limit_bytes=VMEM_LIMIT)


def _sigmoid(x):
    return jax.nn.sigmoid(x)


def _dsilu(x, sg):
    return sg * (1.0 + x * (1.0 - sg))


def _fit(dim, want):
    if dim <= want:
        return dim
    return max(t for t in range(128, want + 1, 128) if dim % t == 0)


def _dot(a, b, ta=False, tb=False):
    dn = (((0,) if ta else (1,), (1,) if tb else (0,)), ((), ()))
    return lax.dot_general(a, b, dn, preferred_element_type=F32)


def _mm(name, As, Bs, prods, epi, out_dtypes, *, ta=False, tb=False, tm, tn, tk, extras=(), i_outer=True,
        b3=False, o3=False, rider=None):
    a0, b0 = As[0], Bs[0]
    M, K = (a0.shape[1], a0.shape[0]) if ta else a0.shape
    if b3:
        S, rows, cs = b0.shape
        N = rows if tb else S * cs
        assert K == (S * cs if tb else rows)
        tn, tk = (tn, cs) if tb else (cs, tk)
    else:
        N = b0.shape[0] if tb else b0.shape[1]
    tm, tn, tk = _fit(M, tm), _fit(N, tn), _fit(K, tk)
    assert M % tm == 0 and N % tn == 0 and K % tk == 0, (name, M, N, K, tm, tn, tk)
    gi, gj, gk = M // tm, N // tn, K // tk
    grid = (gi, gj, gk) if i_outer else (gj, gi, gk)

    def ij(g0, g1):
        return (g0, g1) if i_outer else (g1, g0)

    def amap(g0, g1, k):
        i, _ = ij(g0, g1)
        return (k, i) if ta else (i, k)

    def bmap(g0, g1, k):
        _, j = ij(g0, g1)
        return (j, k) if tb else (k, j)

    def bmap3(g0, g1, k):
        _, j = ij(g0, g1)
        return (k, j, 0) if tb else (j, k, 0)

    in_specs = [pl.BlockSpec((tk, tm) if ta else (tm, tk), amap) for _ in As]
    if b3:
        in_specs += [pl.BlockSpec((None, tn, tk) if tb else (None, tk, tn), bmap3) for _ in Bs]
    else:
        in_specs += [pl.BlockSpec((tn, tk) if tb else (tk, tn), bmap) for _ in Bs]
    args = list(As) + list(Bs)
    for arr, kind, coloff in extras:
        assert coloff % tn == 0
        off = coloff // tn
        if kind == 'mn':
            in_specs.append(pl.BlockSpec((tm, tn), lambda g0, g1, k, off=off: (ij(g0, g1)[0], ij(g0, g1)[1] + off)))
        else:
            in_specs.append(pl.BlockSpec((1, tn), lambda g0, g1, k, off=off: (0, ij(g0, g1)[1] + off)))
        args.append(arr)
    if o3:
        out_shape = [jax.ShapeDtypeStruct((gj, M, tn), dt) for dt in out_dtypes]
        out_specs = [pl.BlockSpec((None, tm, tn), lambda g0, g1, k: (ij(g0, g1)[1], ij(g0, g1)[0], 0))
                     for _ in out_dtypes]
    else:
        out_shape = [jax.ShapeDtypeStruct((M, N), dt) for dt in out_dtypes]
        out_specs = [pl.BlockSpec((tm, tn), lambda g0, g1, k: ij(g0, g1)) for _ in out_dtypes]
    n_a, n_b, n_e, n_o, n_p = len(As), len(Bs), len(extras), len(out_dtypes), len(prods)
    scratch = [pltpu.VMEM((tm, tn), F32) for _ in range(n_p)]
    if rider is not None:
        in_specs, out_specs = in_specs + [ANY] * len(rider.ins), out_specs + [ANY] * len(rider.out_shape)
        args, out_shape, scratch = args + rider.ins, out_shape + rider.out_shape, scratch + rider.scratch
    n_in, n_out = len(args), len(out_shape)

    def body(*refs):
        a_refs = refs[:n_a]
        b_refs = refs[n_a:n_a + n_b]
        e_refs = refs[n_a + n_b:n_a + n_b + n_e]
        o_refs = refs[n_in:n_in + n_o]
        acc_refs = refs[n_in + n_out:n_in + n_out + n_p]
        k = pl.program_id(2)
        if rider is not None:
            step = (pl.program_id(0) * grid[1] + pl.program_id(1)) * gk + k
            ride = (step, grid[0] * grid[1] * gk, refs[n_a + n_b + n_e:n_in], refs[n_in + n_o:n_in + n_out],
                    refs[n_in + n_out + n_p:])
            rider.begin(*ride)

        @pl.when(k == 0)
        def _():
            for acc in acc_refs:
                acc[...] = jnp.zeros_like(acc)

        for p, terms in enumerate(prods):
            for ai, bi in terms:
                acc_refs[p][...] += _dot(a_refs[ai][...], b_refs[bi][...], ta, tb)

        @pl.when(k == gk - 1)
        def _():
            res = epi([acc[...] for acc in acc_refs], *[e[...] for e in e_refs])
            for o, r in zip(o_refs, res):
                o[...] = r.astype(o.dtype)

        if rider is not None:
            rider.end(*ride)

    res = pl.pallas_call(
        body, name=name, grid=grid, in_specs=in_specs, out_specs=out_specs, out_shape=out_shape,
        scratch_shapes=scratch, compiler_params=_params(("arbitrary", "arbitrary", "arbitrary")),
    )(*args)
    if rider is not None:
        rider.results = res[n_o:]
    return res[:n_o]


def _rows(name, fn, ins, outs, *, T, tb):
    tb = min(tb, T)
    assert T % tb == 0
    in_specs, args = [], []
    for arr, kind, width, cb in ins:
        if kind == 'r':
            in_specs.append(pl.BlockSpec((tb, width), lambda i, cb=cb: (i, cb)))
        else:
            in_specs.append(pl.BlockSpec((1, width), lambda i, cb=cb: (0, cb)))
        args.append(arr)
    out_shape, out_specs = [], []
    for kind, width, dtype in outs:
        if kind == 'r':
            out_shape.append(jax.ShapeDtypeStruct((T, width), dtype))
            out_specs.append(pl.BlockSpec((tb, width), lambda i: (i, 0)))
        elif kind == 'c':
            out_shape.append(jax.ShapeDtypeStruct((T, 1), dtype))
            out_specs.append(pl.BlockSpec((tb, 1), lambda i: (i, 0)))
        else:
            out_shape.append(jax.ShapeDtypeStruct((1, width), F32))
            out_specs.append(pl.BlockSpec((1, width), lambda i: (0, 0)))
    n_in = len(ins)

    def body(*refs):
        i = pl.program_id(0)
        vals = fn(*[r[...] for r in refs[:n_in]])
        for (kind, _, _), o, v in zip(outs, refs[n_in:], vals):
            if kind == 'a':
                @pl.when(i == 0)
                def _(o=o):
                    o[...] = jnp.zeros_like(o)

                o[...] += v
            else:
                o[...] = v.astype(o.dtype)

    return pl.pallas_call(
        body, name=name, grid=(T // tb,), in_specs=in_specs, out_specs=out_specs, out_shape=out_shape,
        compiler_params=_params(("arbitrary",)),
    )(*args)


def _colsum(v):
    return jnp.sum(v, axis=0, keepdims=True)


def _ln_stats(z):
    mu = jnp.mean(z, axis=-1, keepdims=True)
    d = z - mu
    var = jnp.mean(d * d, axis=-1, keepdims=True)
    rstd = lax.rsqrt(var + LN_EPS)
    return d * rstd, rstd


def _ln_bwd_math(dy, xhat, rstd, g):
    dxh = dy * g
    m1 = jnp.mean(dxh, axis=-1, keepdims=True)
    m2 = jnp.mean(dxh * xhat, axis=-1, keepdims=True)
    return rstd * (dxh - m1 - xhat * m2)


def _ln_fwd(name, z, g, b, T, D):
    def fn(z, g, b):
        xhat, rstd = _ln_stats(z)
        y = xhat * g + b
        return [y, y, xhat, rstd]

    return _rows(name, fn, [(z, 'r', D, 0), (g, 'v', D, 0), (b, 'v', D, 0)],
                 [('r', D, F32), ('r', D, BF16), ('r', D, F32), ('c', 1, F32)], T=T, tb=512)


def _ln_bwd(name, dy, xhat, rstd, g, scale, T, D):
    def fn(dy, xhat, rstd, g):
        dz = _ln_bwd_math(dy, xhat, rstd, g)
        return [dz, dz * scale, _colsum(dy * xhat), _colsum(dy)]

    return _rows(name, fn, [(dy, 'r', D, 0), (xhat, 'r', D, 0), (rstd, 'r', 1, 0), (g, 'v', D, 0)],
                 [('r', D, F32), ('r', D, BF16), ('a', D, F32), ('a', D, F32)], T=T, tb=512)


def _ln_loss_bwd(name, z, g, b, tgt, T, D):
    def fn(z, g, b, tgt):
        xhat, rstd = _ln_stats(z)
        err = xhat * g + b - tgt
        row_loss = 0.5 * jnp.mean(err * err, axis=-1, keepdims=True)
        loss = jnp.broadcast_to(jnp.sum(row_loss, axis=0, keepdims=True), (1, 128))
        dy = err * (1.0 / D)
        dz = _ln_bwd_math(dy, xhat, rstd, g)
        return [dz, dz * 0.5, _colsum(dy * xhat), _colsum(dy), loss]

    return _rows(name, fn, [(z, 'r', D, 0), (g, 'v', D, 0), (b, 'v', D, 0), (tgt, 'r', D, 0)],
                 [('r', D, F32), ('r', D, BF16), ('a', D, F32), ('a', D, F32), ('a', 128, F32)], T=T, tb=512)


class _Net:
    def __init__(self, comm, G):
        self.comm, self.G = comm, G

    def gather(self, names):
        return self.comm.gather(names) if self.comm else None

    def exchange(self, names):
        return self.comm.exchange(names, self.G) if self.comm else None

    def done(self, rider):
        if rider is not None:
            self.comm.collect(rider)


def _ffn_fwd(tag, xb, x, W, names, net, rider=None):
    def epi_gu(accs):
        a, b = accs
        return [a, b, a * _sigmoid(a) * b]

    ng, nu, nd = names
    a, b, s = _mm(tag + "_gate_up", [xb], [W[ng], W[nu]], [[(0, 0)], [(0, 1)]], epi_gu, [F32, F32, BF16],
                  tm=512, tn=1408, tk=1024, rider=rider)
    net.done(rider)

    def epi_down(accs, xres):
        return [ALPHA * xres + 0.5 * accs[0]]

    (z,) = _mm(tag + "_down", [s], [W[nd]], [[(0, 0)]], epi_down, [F32], tm=512, tn=1024, tk=1408,
               extras=[(x, 'mn', 0)])
    return a, b, s, z


def _ffn_bwd(tag, dzh, dz, xb, a, b, s, W, names, gdt, G, net, ride):
    ng, nu, nd = names

    def epi_ds(accs, a, b):
        ds = accs[0]
        sg = _sigmoid(a)
        return [ds * b * _dsilu(a, sg), ds * a * sg]

    da, db = _mm(tag + "_ds", [dzh], [W[nd]], [[(0, 0)]], epi_ds, [BF16, BF16], tb=True, tm=512, tn=1408, tk=1024,
                 extras=[(a, 'mn', 0), (b, 'mn', 0)])
    ident = lambda accs: accs
    (G[nd],) = _mm(tag + "_dwd", [s], [dzh], [[(0, 0)]], ident, [gdt], ta=True, tm=1408, tn=1024, tk=1024)
    rider = net.exchange([nd]) if ride else None
    G[ng], G[nu] = _mm(tag + "_dwgu", [xb], [da, db], [[(0, 0)], [(0, 1)]], ident, [gdt, gdt], ta=True,
                       tm=512, tn=1408, tk=1024, rider=rider)
    net.done(rider)

    def epi_dx(accs, dzres):
        return [ALPHA * dzres + accs[0]]

    rider = net.exchange([ng, nu]) if ride else None
    (dx,) = _mm(tag + "_dx", [da, db], [W[ng], W[nu]], [[(0, 0), (1, 1)]], epi_dx, [F32], tb=True,
                tm=512, tn=1024, tk=1408, extras=[(dz, 'mn', 0)], rider=rider)
    net.done(rider)
    return dx


def _ret_tables(H, T):
    C = RET_CHUNK
    log_g = jnp.log(1.0 - jnp.exp2(-5.0 - jnp.arange(H, dtype=F32)))
    idx = jnp.arange(C, dtype=F32)
    diff = idx[:, None] - idx[None, :]
    dm = jnp.where(diff[None] >= 0, jnp.exp(jnp.maximum(diff, 0.0)[None] * log_g[:, None, None]), 0.0)
    xi = jnp.exp((idx[None, :] + 1.0) * log_g[:, None])[:, :, None]
    zeta = jnp.exp((C - 1.0 - idx)[None, :] * log_g[:, None])[:, :, None]
    gc = jnp.broadcast_to(jnp.exp(C * log_g)[:, None, None], (H, 1, RET_DV))
    half = RET_DK // 2
    freqs = ROPE_BASE ** (-jnp.arange(half, dtype=F32) / half)
    ang = jnp.arange(T, dtype=F32)[:, None] * freqs[None, :]
    cos, sin = jnp.cos(ang), jnp.sin(ang)
    cosf = jnp.concatenate([cos, cos], axis=1)
    sins = jnp.concatenate([-sin, sin], axis=1)
    return dm, xi, zeta, gc, cosf, sins


def _rot(x, cosf, sins):
    return x * cosf + pltpu.roll(x, RET_DK // 2, 1) * sins


def _rot_bwd(dy, cosf, sins):
    return dy * cosf + pltpu.roll(dy * sins, RET_DK // 2, 1)


RET_HB = 4


def _ret_specs(H, HB, rev, NC):
    C, G = RET_CHUNK, H // HB
    nn = (lambda n: NC - 1 - n) if rev else (lambda n: n)
    return [
        pl.BlockSpec((C, HB * RET_DK), lambda h, n: (nn(n), h)),
        pl.BlockSpec((C, HB * RET_DK), lambda h, n: (nn(n), G + h)),
        pl.BlockSpec((C, HB * RET_DV), lambda h, n: (nn(n), G + h)),
        pl.BlockSpec((C, HB * RET_DV), lambda h, n: (nn(n), 2 * G + h)),
        pl.BlockSpec((C, RET_DK), lambda h, n: (nn(n), 0)),
        pl.BlockSpec((C, RET_DK), lambda h, n: (nn(n), 0)),
        pl.BlockSpec((1, HB * RET_DV), lambda h, n: (0, h)),
        pl.BlockSpec((HB, C, C), lambda h, n: (h, 0, 0)),
        pl.BlockSpec((HB, C, 1), lambda h, n: (h, 0, 0)),
        pl.BlockSpec((HB, C, 1), lambda h, n: (h, 0, 0)),
        pl.BlockSpec((HB, 1, RET_DV), lambda h, n: (h, 0, 0)),
    ]


def _ret_fwd(proj, gn_g, tabs, H, T):
    C, NC = RET_CHUNK, T // RET_CHUNK
    HB = min(RET_HB, H)
    dm, xi, zeta, gc, cosf, sins = tabs
    scale = RET_DK ** -0.5

    def body(q_ref, k_ref, v_ref, g_ref, cos_ref, sin_ref, gn_ref, dm_ref, xi_ref, zt_ref, gc_ref,
             r_ref, ri_ref, st_ref, state):
        @pl.when(pl.program_id(1) == 0)
        def _():
            state[...] = jnp.zeros_like(state)

        cs, sn = cos_ref[...], sin_ref[...]
        for hh in range(HB):
            qk = slice(hh * RET_DK, (hh + 1) * RET_DK)
            vv = slice(hh * RET_DV, (hh + 1) * RET_DV)
            qr = _rot(q_ref[:, qk], cs, sn) * scale
            kr = _rot(k_ref[:, qk], cs, sn)
            qb, kb, vb = qr.astype(BF16), kr.astype(BF16), v_ref[:, vv].astype(BF16)
            st = state[hh]
            stb = st.astype(BF16)
            s = _dot(qb, kb, tb=True) * dm_ref[hh]
            r = _dot(s.astype(BF16), vb) + _dot(qb, stb) * xi_ref[hh]
            st_ref[hh] = stb
            state[hh] = gc_ref[hh] * st + _dot((kr * zt_ref[hh]).astype(BF16), vb, ta=True)
            rhat, _ = _ln_stats(r)
            g = g_ref[:, vv]
            r_ref[:, vv] = r
            ri_ref[:, vv] = (g * _sigmoid(g) * (rhat * gn_ref[:, vv])).astype(BF16)

    VW = H * RET_DV
    return pl.pallas_call(
        body, name="ret_fwd", grid=(H // HB, NC), in_specs=_ret_specs(H, HB, False, NC),
        out_specs=[pl.BlockSpec((C, HB * RET_DV), lambda h, n: (n, h)),
                   pl.BlockSpec((C, HB * RET_DV), lambda h, n: (n, h)),
                   pl.BlockSpec((HB, None, RET_DK, RET_DV), lambda h, n: (h, n, 0, 0))],
        out_shape=[jax.ShapeDtypeStruct((T, VW), F32), jax.ShapeDtypeStruct((T, VW), BF16),
                   jax.ShapeDtypeStruct((H, NC, RET_DK, RET_DV), BF16)],
        scratch_shapes=[pltpu.VMEM((HB, RET_DK, RET_DV), F32)],
        compiler_params=_params(("arbitrary", "arbitrary")),
    )(proj, proj, proj, proj, cosf, sins, gn_g, dm, xi, zeta, gc)


def _hosted_call(body, rider, *, name, grid, in_specs, out_specs, out_shape, scratch, args):
    n_in, n_out, n_scr = len(args), len(out_shape), len(scratch)
    if rider is None:
        hosted = body
    else:
        n_ri, n_ro = len(rider.ins), len(rider.out_shape)
        in_specs, out_specs = in_specs + [ANY] * n_ri, out_specs + [ANY] * n_ro
        args, out_shape, scratch = args + rider.ins, out_shape + rider.out_shape, scratch + rider.scratch

        def hosted(*refs):
            o0, s0 = n_in + n_ri, n_in + n_ri + n_out + n_ro
            step = pl.program_id(0) * grid[1] + pl.program_id(1)
            ride = (step, grid[0] * grid[1], refs[n_in:o0], refs[o0 + n_out:s0], refs[s0 + n_scr:])
            rider.begin(*ride)
            body(*refs[:n_in], *refs[o0:o0 + n_out], *refs[s0:s0 + n_scr])
            rider.end(*ride)

    res = pl.pallas_call(
        hosted, name=name, grid=grid, in_specs=in_specs, out_specs=out_specs, out_shape=out_shape,
        scratch_shapes=scratch, compiler_params=_params(("arbitrary", "arbitrary")),
    )(*args)
    if rider is not None:
        rider.results = res[n_out:]
    return res[:n_out]


def _ret_bwd(dri, r, states, proj, gn_g, tabs, H, T, rider=None):
    C, NC = RET_CHUNK, T // RET_CHUNK
    HB = min(RET_HB, H)
    dm, xi, zeta, gc, cosf, sins = tabs
    scale = RET_DK ** -0.5

    def body(q_ref, k_ref, v_ref, g_ref, cos_ref, sin_ref, gn_ref, dm_ref, xi_ref, zt_ref, gc_ref,
             dri_ref, r_ref, st_ref, dq_ref, dk_ref, dv_ref, dg_ref, dgn_ref, dstate):
        @pl.when(pl.program_id(1) == 0)
        def _():
            dstate[...] = jnp.zeros_like(dstate)
            dgn_ref[...] = jnp.zeros_like(dgn_ref)

        cs, sn = cos_ref[...], sin_ref[...]
        for hh in range(HB):
            qk = slice(hh * RET_DK, (hh + 1) * RET_DK)
            vv = slice(hh * RET_DV, (hh + 1) * RET_DV)
            qr = _rot(q_ref[:, qk], cs, sn) * scale
            kr = _rot(k_ref[:, qk], cs, sn)
            qb, kb, vb = qr.astype(BF16), kr.astype(BF16), v_ref[:, vv].astype(BF16)
            xi_c, zt_c, dmask = xi_ref[hh], zt_ref[hh], dm_ref[hh]
            rhat, rstd = _ln_stats(r_ref[:, vv])
            g, gn, dpre = g_ref[:, vv], gn_ref[:, vv], dri_ref[:, vv]
            sg = _sigmoid(g)
            dg_ref[:, vv] = (dpre * (rhat * gn) * _dsilu(g, sg)).astype(BF16)
            drn = dpre * (g * sg)
            dgn_ref[:, vv] += _colsum(drn * rhat)
            drb = _ln_bwd_math(drn, rhat, rstd, gn).astype(BF16)
            ds1 = dstate[hh]
            ds1b = ds1.astype(BF16)
            sb = (_dot(qb, kb, tb=True) * dmask).astype(BF16)
            kzb = (kr * zt_c).astype(BF16)
            dv_ref[:, vv] = (_dot(sb, drb, ta=True) + _dot(kzb, ds1b)).astype(BF16)
            dsb = (_dot(drb, vb, tb=True) * dmask).astype(BF16)
            dq = _dot(dsb, kb) + _dot(drb, st_ref[hh], tb=True) * xi_c
            dk = _dot(dsb, qb, ta=True) + _dot(vb, ds1b, tb=True) * zt_c
            dstate[hh] = gc_ref[hh] * ds1 + _dot((qr * xi_c).astype(BF16), drb, ta=True)
            dq_ref[:, qk] = _rot_bwd(dq * scale, cs, sn).astype(BF16)
            dk_ref[:, qk] = _rot_bwd(dk, cs, sn).astype(BF16)

    VW, QW = H * RET_DV, H * RET_DK
    rv = lambda n: NC - 1 - n
    in_specs = _ret_specs(H, HB, True, NC) + [
        pl.BlockSpec((C, HB * RET_DV), lambda h, n: (rv(n), h)),
        pl.BlockSpec((C, HB * RET_DV), lambda h, n: (rv(n), h)),
        pl.BlockSpec((HB, None, RET_DK, RET_DV), lambda h, n: (h, rv(n), 0, 0)),
    ]
    return _hosted_call(
        body, rider, name="ret_bwd", grid=(H // HB, NC), in_specs=in_specs,
        out_specs=[pl.BlockSpec((C, HB * RET_DK), lambda h, n: (rv(n), h)),
                   pl.BlockSpec((C, HB * RET_DK), lambda h, n: (rv(n), h)),
                   pl.BlockSpec((C, HB * RET_DV), lambda h, n: (rv(n), h)),
                   pl.BlockSpec((C, HB * RET_DV), lambda h, n: (rv(n), h)),
                   pl.BlockSpec((1, HB * RET_DV), lambda h, n: (0, h))],
        out_shape=[jax.ShapeDtypeStruct((T, QW), BF16), jax.ShapeDtypeStruct((T, QW), BF16),
                   jax.ShapeDtypeStruct((T, VW), BF16), jax.ShapeDtypeStruct((T, VW), BF16),
                   jax.ShapeDtypeStruct((1, VW), F32)],
        scratch=[pltpu.VMEM((HB, RET_DK, RET_DV), F32)],
        args=[proj, proj, proj, proj, cosf, sins, gn_g, dm, xi, zeta, gc, dri, r, states])


CONV_CW = 128
CONV_TB = 512


def _conv_fwd(proj, kpad, bias, off_a, CC, T):
    tb, cw = min(CONV_TB, T), CONV_CW
    hb = tb // HALO
    ca, cb = off_a // cw, (off_a + CC) // cw

    def body(a_ref, b_ref, ap_ref, bp_ref, k_ref, bias_ref, u1_ref, win):
        i = pl.program_id(0)
        keep = (i > 0).astype(F32)
        win[0:HALO, :] = ap_ref[...] * _sigmoid(bp_ref[...]) * keep
        win[HALO:, :] = a_ref[...] * _sigmoid(b_ref[...])
        acc = jnp.broadcast_to(bias_ref[...], (tb, cw))
        for w in range(CONV_WIDTH):
            acc = acc + k_ref[w:w + 1, :] * win[pl.ds(HALO - (CONV_WIDTH - 1) + w, tb), :]
        u1_ref[...] = acc

    prev = lambda i: jnp.maximum(i * hb - 1, 0)
    return pl.pallas_call(
        body, name="conv_fwd", grid=(T // tb, CC // cw),
        in_specs=[pl.BlockSpec((tb, cw), lambda i, c: (i, ca + c)),
                  pl.BlockSpec((tb, cw), lambda i, c: (i, cb + c)),
                  pl.BlockSpec((HALO, cw), lambda i, c: (prev(i), ca + c)),
                  pl.BlockSpec((HALO, cw), lambda i, c: (prev(i), cb + c)),
                  pl.BlockSpec((HALO, cw), lambda i, c: (0, c)),
                  pl.BlockSpec((1, cw), lambda i, c: (0, c))],
        out_specs=pl.BlockSpec((tb, cw), lambda i, c: (i, c)),
        out_shape=jax.ShapeDtypeStruct((T, CC), F32),
        scratch_shapes=[pltpu.VMEM((tb + HALO, cw), F32)],
        compiler_params=_params(("arbitrary", "arbitrary")),
    )(proj, proj, proj, proj, kpad, bias)


def _conv_bwd(du1, proj, kpad, off_a, CC, T, rider=None):
    tb, cw = min(CONV_TB, T), CONV_CW
    hb = tb // HALO
    nt = T // tb
    ca, cb = off_a // cw, (off_a + CC) // cw

    def body(d_ref, dn_ref, a_ref, b_ref, ap_ref, bp_ref, k_ref, da_ref, db_ref, dk_ref, winu, wind):
        i = pl.program_id(1)
        a, b = a_ref[...], b_ref[...]
        sgb = _sigmoid(b)
        winu[0:HALO, :] = ap_ref[...] * _sigmoid(bp_ref[...]) * (i > 0).astype(F32)
        winu[HALO:, :] = a * sgb
        d = d_ref[...]
        wind[0:tb, :] = d
        wind[tb:, :] = dn_ref[...] * (i < nt - 1).astype(F32)

        @pl.when(i == 0)
        def _():
            dk_ref[...] = jnp.zeros_like(dk_ref)

        du0 = jnp.zeros((tb, cw), F32)
        for w in range(CONV_WIDTH):
            du0 = du0 + k_ref[w:w + 1, :] * wind[pl.ds(CONV_WIDTH - 1 - w, tb), :]
            dk_ref[w:w + 1, :] += _colsum(winu[pl.ds(HALO - (CONV_WIDTH - 1) + w, tb), :] * d)
        da_ref[...] = (du0 * sgb).astype(BF16)
        db_ref[...] = (du0 * a * sgb * (1.0 - sgb)).astype(BF16)

    prev = lambda i: jnp.maximum(i * hb - 1, 0)
    nxt = lambda i: jnp.minimum((i + 1) * hb, T // HALO - 1)
    return _hosted_call(
        body, rider, name="conv_bwd", grid=(CC // cw, nt),
        in_specs=[pl.BlockSpec((tb, cw), lambda c, i: (i, c)),
                  pl.BlockSpec((HALO, cw), lambda c, i: (nxt(i), c)),
                  pl.BlockSpec((tb, cw), lambda c, i: (i, ca + c)),
                  pl.BlockSpec((tb, cw), lambda c, i: (i, cb + c)),
                  pl.BlockSpec((HALO, cw), lambda c, i: (prev(i), ca + c)),
                  pl.BlockSpec((HALO, cw), lambda c, i: (prev(i), cb + c)),
                  pl.BlockSpec((HALO, cw), lambda c, i: (0, c))],
        out_specs=[pl.BlockSpec((tb, cw), lambda c, i: (i, c)),
                   pl.BlockSpec((tb, cw), lambda c, i: (i, c)),
                   pl.BlockSpec((HALO, cw), lambda c, i: (0, c))],
        out_shape=[jax.ShapeDtypeStruct((T, CC), BF16), jax.ShapeDtypeStruct((T, CC), BF16),
                   jax.ShapeDtypeStruct((HALO, CC), F32)],
        scratch=[pltpu.VMEM((tb + HALO, cw), F32), pltpu.VMEM((tb + HALO, cw), F32)],
        args=[du1, du1, proj, proj, proj, proj, kpad])


FFN1 = ('ffn1_w_gate', 'ffn1_w_up', 'ffn1_w_down')
FFN2 = ('ffn2_w_gate', 'ffn2_w_up', 'ffn2_w_down')


def _local_step(x, tgt, W, P, gdt=BF16, comm=None):
    T, D = x.shape
    G = {}
    net = _Net(comm, G)
    if comm is not None:
        W = comm.W
        first = net.gather(['ffn1_w_gate', 'ffn1_w_up'])
        _run_rider("gather_ffn1_in", first)
        net.done(first)
    VW = P['ret_gn_g'].shape[1]
    H = VW // RET_DV
    QW = H * RET_DK
    CC = P['conv_b'].shape[1]
    off_glu = 2 * QW + 2 * VW
    off_gate = off_glu + 2 * CC
    ident = lambda accs: accs
    xb = x.astype(BF16)

    a1, b1, s1, z1 = _ffn_fwd("ffn1", xb, x, W, FFN1, net, rider=net.gather(['ffn1_w_down', 'w_in']))
    x1, x1b, xh1, rs1 = _ln_fwd("ln1", z1, P['ln1_g'], P['ln1_b'], T, D)

    rest = net.gather(['conv_k', 'w_ret_o', 'w_conv_o', 'w_out'] + list(FFN2))
    (proj,) = _mm("w_in", [x1b], [W['w_in']], [[(0, 0)]], lambda accs, bias: [accs[0] + bias], [F32],
                  tm=512, tn=0, tk=1024, extras=[(P['b_in'], 'n', 0)], i_outer=False, b3=True, rider=rest)
    net.done(rest)
    tabs = _ret_tables(H, T)
    r, ret_in, states = _ret_fwd(proj, P['ret_gn_g'], tabs, H, T)
    kpad = jnp.pad(W['conv_k'].astype(F32), ((0, HALO - CONV_WIDTH), (0, 0)))
    u1 = _conv_fwd(proj, kpad, P['conv_b'], off_glu, CC, T)

    def conv_ln(u1, g, b):
        xhat, rstd = _ln_stats(u1)
        u2 = xhat * g + b
        return [xhat, rstd, u2 * _sigmoid(u2)]

    xhc, rsc, u3 = _rows("conv_ln", conv_ln, [(u1, 'r', CC, 0), (P['conv_ln_g'], 'v', CC, 0), (P['conv_ln_b'], 'v', CC, 0)],
                         [('r', CC, F32), ('c', 1, F32), ('r', CC, BF16)], T=T, tb=512)
    (ret_out,) = _mm("ret_o", [ret_in], [W['w_ret_o']], [[(0, 0)]], ident, [F32], tm=512, tn=1024, tk=2048)

    def epi_merge(accs, ret_out, gr, gc):
        conv_out = accs[0]
        return [conv_out, _sigmoid(gr) * ret_out + _sigmoid(gc) * conv_out]

    conv_out, merged = _mm("conv_o_merge", [u3], [W['w_conv_o']], [[(0, 0)]], epi_merge, [F32, BF16],
                           tm=512, tn=D, tk=1024,
                           extras=[(ret_out, 'mn', 0), (proj, 'mn', off_gate), (proj, 'mn', off_gate + D)])
    (z2,) = _mm("w_out", [merged], [W['w_out']], [[(0, 0)]], lambda accs, xr: [ALPHA * xr + accs[0]], [F32],
                tm=512, tn=1024, tk=1024, extras=[(x1, 'mn', 0)])
    x2, x2b, xh2, rs2 = _ln_fwd("ln2", z2, P['ln2_g'], P['ln2_b'], T, D)
    a2, b2, s2, z3 = _ffn_fwd("ffn2", x2b, x2, W, FFN2, net)
    dz3, dz3h, g_ln3_g, g_ln3_b, loss = _ln_loss_bwd("ln3_loss", z3, P['ln3_g'], P['ln3_b'], tgt, T, D)

    S = {'ln3_g': g_ln3_g, 'ln3_b': g_ln3_b}
    dy2 = _ffn_bwd("ffn2b", dz3h, dz3, x2b, a2, b2, s2, W, FFN2, gdt, G, net, False)
    dz2, dz2b, S['ln2_g'], S['ln2_b'] = _ln_bwd("ln2b", dy2, xh2, rs2, P['ln2_g'], 1.0, T, D)

    (G['w_out'],) = _mm("d_w_out", [merged], [dz2b], [[(0, 0)]], ident, [gdt], ta=True, tm=1024, tn=1024, tk=1024)

    def epi_dmerge(accs, ret_out, conv_out, gr, gc):
        dm_ = accs[0]
        sr, sc = _sigmoid(gr), _sigmoid(gc)
        return [dm_ * sr, dm_ * sc, dm_ * ret_out * sr * (1.0 - sr), dm_ * conv_out * sc * (1.0 - sc)]

    dret_out, dconv_out, dgate_r, dgate_c = _mm(
        "d_merge", [dz2b], [W['w_out']], [[(0, 0)]], epi_dmerge, [BF16, BF16, BF16, BF16], tb=True,
        tm=512, tn=D, tk=1024,
        extras=[(ret_out, 'mn', 0), (conv_out, 'mn', 0), (proj, 'mn', off_gate), (proj, 'mn', off_gate + D)])
    (G['w_ret_o'],) = _mm("d_w_ret_o", [ret_in], [dret_out], [[(0, 0)]], ident, [gdt], ta=True, tm=1024, tn=1024, tk=1024)
    (G['w_conv_o'],) = _mm("d_w_conv_o", [u3], [dconv_out], [[(0, 0)]], ident, [gdt], ta=True, tm=1024, tn=1024, tk=1024)
    (dri,) = _mm("d_ret_in", [dret_out], [W['w_ret_o']], [[(0, 0)]], ident, [F32], tb=True, tm=512, tn=1024, tk=1024)
    rider = net.exchange(list(FFN2))
    dq, dk, dv, dg, S['ret_gn_g'] = _ret_bwd(dri, r, states, proj, P['ret_gn_g'], tabs, H, T, rider=rider)
    net.done(rider)

    def epi_du2(accs, xhat, g, b):
        u2 = xhat * g + b
        return [accs[0] * _dsilu(u2, _sigmoid(u2))]

    (du2,) = _mm("d_u3", [dconv_out], [W['w_conv_o']], [[(0, 0)]], epi_du2, [F32], tb=True, tm=512, tn=CC, tk=1024,
                 extras=[(xhc, 'mn', 0), (P['conv_ln_g'], 'n', 0), (P['conv_ln_b'], 'n', 0)])

    def conv_ln_bwd(du2, xhat, rstd, g):
        du1 = _ln_bwd_math(du2, xhat, rstd, g)
        return [du1, _colsum(du2 * xhat), _colsum(du2), _colsum(du1)]

    du1, S['conv_ln_g'], S['conv_ln_b'], S['conv_b'] = _rows(
        "conv_ln_bwd", conv_ln_bwd, [(du2, 'r', CC, 0), (xhc, 'r', CC, 0), (rsc, 'r', 1, 0), (P['conv_ln_g'], 'v', CC, 0)],
        [('r', CC, F32), ('a', CC, F32), ('a', CC, F32), ('a', CC, F32)], T=T, tb=512)
    rider = net.exchange(['w_out', 'w_ret_o', 'w_conv_o'])
    dglu_a, dglu_b, dkpad = _conv_bwd(du1, proj, kpad, off_glu, CC, T, rider=rider)
    net.done(rider)
    G['conv_k'] = dkpad[:CONV_WIDTH].astype(gdt)

    dproj = jnp.concatenate([dq, dk, dv, dg, dglu_a, dglu_b, dgate_r, dgate_c], axis=1)
    IN_W = dproj.shape[1]
    (S['b_in'],) = _rows("d_b_in", lambda d: [_colsum(d.astype(F32))], [(dproj, 'r', IN_W, 0)], [('a', IN_W, F32)],
                         T=T, tb=256)
    (G['w_in'],) = _mm("d_w_in", [x1b], [dproj], [[(0, 0)]], ident, [gdt], ta=True, o3=True,
                       tm=1024, tn=W['w_in'].shape[2], tk=1024)
    rider = net.exchange(['w_in', 'conv_k'])
    (dy1,) = _mm("d_x1", [dproj], [W['w_in']], [[(0, 0)]], lambda accs, dzr: [ALPHA * dzr + accs[0]], [F32], tb=True,
                 b3=True, tm=512, tn=1024, tk=0, extras=[(dz2, 'mn', 0)], rider=rider)
    net.done(rider)
    dz1, dz1h, S['ln1_g'], S['ln1_b'] = _ln_bwd("ln1b", dy1, xh1, rs1, P['ln1_g'], 0.5, T, D)
    grad_x = _ffn_bwd("ffn1b", dz1h, dz1, xb, a1, b1, s1, W, FFN1, gdt, G, net, True)
    return loss[0, 0], grad_x, G, S


def _coords():
    return lax.axis_index("x"), lax.axis_index("y"), lax.axis_index("c")


def _flip(k, x, y, c):
    return (1 - x if k & 4 else x, 1 - y if k & 2 else y, 1 - c if k & 1 else c)


def _lin(p):
    return 4 * p[0] + 2 * p[1] + p[2]


class _Rider:
    def __init__(self, ins, out_shape):
        nb = len(ins)
        self.ins, self.out_shape, self.results = list(ins), list(out_shape), None
        self.scratch = [pltpu.SemaphoreType.DMA((7 * nb,)), pltpu.SemaphoreType.DMA((7 * nb,)),
                        pltpu.SemaphoreType.DMA((nb,))]

    def begin(self, step, n_steps, ins, outs, sems):
        @pl.when(step == 0)
        def _():
            self.start(ins, outs, sems)

        @pl.when(step == n_steps // 2)
        def _():
            self.mid(ins, outs, sems)

    def end(self, step, n_steps, ins, outs, sems):
        @pl.when(step == n_steps - 1)
        def _():
            self.finish(ins, outs, sems)

    def mid(self, ins, outs, sems):
        pass


class _GatherRider(_Rider):
    def __init__(self, blks):
        super().__init__(blks, [jax.ShapeDtypeStruct((N_DEV,) + b.shape, b.dtype) for b in blks])

    def _copies(self, x_refs, out_refs, sems):
        nb = len(x_refs)
        send_sems, recv_sems, local_sems = sems
        x, y, c = _coords()
        me, sibling = (x, y, c), (x, y, 1 - c)
        chips = [_flip(4, x, y, c), _flip(2, x, y, c), _flip(6, x, y, c)]

        def copy(k, w, block, to, src=None):
            slot = out_refs[w].at[_lin(block)]
            return pltpu.make_async_remote_copy(
                src_ref=slot if src is None else src, dst_ref=slot, send_sem=send_sems.at[k * nb + w],
                recv_sem=recv_sems.at[k * nb + w], device_id=to, device_id_type=MESH)

        mines = [pltpu.make_async_copy(x_refs[w], out_refs[w].at[_lin(me)], local_sems.at[w]) for w in range(nb)]
        first = [copy(0, w, me, sibling, src=x_refs[w]) for w in range(nb)]
        first += [copy(1 + j, w, me, chip, src=x_refs[w]) for w in range(nb) for j, chip in enumerate(chips)]
        landed = [(copy(1 + j, w, chip, me), copy(4 + j, w, chip, sibling)) for w in range(nb) for j, chip in enumerate(chips)]
        from_sibling = [copy(0, w, sibling, me) for w in range(nb)]
        from_sibling += [copy(4 + j, w, (chip[0], chip[1], 1 - c), me) for w in range(nb) for j, chip in enumerate(chips)]
        return mines, first, landed, from_sibling

    def start(self, ins, outs, sems):
        mines, first, _, _ = self._copies(ins, outs, sems)
        for cp in mines + first:
            cp.start()

    def mid(self, ins, outs, sems):
        for arrival, onward in self._copies(ins, outs, sems)[2]:
            arrival.wait_recv()
            onward.start()

    def finish(self, ins, outs, sems):
        mines, first, landed, from_sibling = self._copies(ins, outs, sems)
        for cp in from_sibling:
            cp.wait_recv()
        for cp in first + [onward for _, onward in landed]:
            cp.wait_send()
        for mine in mines:
            mine.wait()


class _ExchangeRider(_Rider):
    def __init__(self, gs):
        super().__init__(gs, [jax.ShapeDtypeStruct(g.shape, g.dtype) for g in gs])

    def _copies(self, g_refs, out_refs, sems):
        nb = len(g_refs)
        send_sems, recv_sems, local_sems = sems
        x, y, c = _coords()
        me = _lin((x, y, c))

        def copy(k, w, landing):
            peer = _flip(k, x, y, c)
            src, dst = (me, _lin(peer)) if landing else (_lin(peer), me)
            return pltpu.make_async_remote_copy(
                src_ref=g_refs[w].at[src], dst_ref=out_refs[w].at[dst], send_sem=send_sems.at[(k - 1) * nb + w],
                recv_sem=recv_sems.at[(k - 1) * nb + w], device_id=peer, device_id_type=MESH)

        mines = [pltpu.make_async_copy(g_refs[w].at[me], out_refs[w].at[me], local_sems.at[w]) for w in range(nb)]
        sends = [copy(k, w, False) for w in range(nb) for k in range(1, N_DEV)]
        landings = [copy(k, w, True) for w in range(nb) for k in range(1, N_DEV)]
        return mines, sends, landings

    def start(self, ins, outs, sems):
        mines, sends, _ = self._copies(ins, outs, sems)
        for cp in mines + sends:
            cp.start()

    def finish(self, ins, outs, sems):
        mines, sends, landings = self._copies(ins, outs, sems)
        for cp in landings:
            cp.wait_recv()
        for cp in sends:
            cp.wait_send()
        for mine in mines:
            mine.wait()


def _run_rider(name, rider):
    n_in, n_out = len(rider.ins), len(rider.out_shape)

    def body(*refs):
        ride = (refs[:n_in], refs[n_in:n_in + n_out], refs[n_in + n_out:])
        rider.start(*ride)
        rider.mid(*ride)
        rider.finish(*ride)

    rider.results = pl.pallas_call(
        body, name=name, out_shape=rider.out_shape, in_specs=[ANY] * n_in, out_specs=[ANY] * n_out,
        scratch_shapes=rider.scratch, compiler_params=pltpu.CompilerParams(has_side_effects=True),
    )(*rider.ins)
    return rider.results


def _as_matrix(name, g):
    if name == 'w_in':
        return g
    if name in COL_SHARDED:
        return jnp.transpose(g, (1, 0, 2)).reshape(g.shape[1], N_DEV * g.shape[2])
    return g.reshape(N_DEV * g.shape[1], g.shape[2])


def _by_owner(name, g):
    if name == 'w_in':
        return g
    if name in COL_SHARDED:
        return jnp.transpose(g.reshape(g.shape[0], N_DEV, g.shape[1] // N_DEV), (1, 0, 2))
    return g.reshape(N_DEV, g.shape[0] // N_DEV, g.shape[1])


class _Comm:
    def __init__(self, shards):
        self.shards, self.W, self.parts = shards, {}, {}

    def gather(self, names):
        rider = _GatherRider([self.shards[n] for n in names])
        rider.names, rider.sink = names, 'W'
        return rider

    def exchange(self, names, G):
        rider = _ExchangeRider([_by_owner(n, G[n]) for n in names])
        rider.names, rider.sink = names, 'parts'
        return rider

    def collect(self, rider):
        for n, res in zip(rider.names, rider.results):
            if rider.sink == 'W':
                self.W[n] = _as_matrix(n, res)
            else:
                self.parts[n] = res


def _adamw(name, parts, w, m, v, tb):
    n, R, Wd = parts.shape
    assert R % tb == 0
    c1 = 1.0 - ADAM_B1 ** ADAM_STEP
    c2 = 1.0 - ADAM_B2 ** ADAM_STEP

    def body(p_ref, w_ref, m_ref, v_ref, g_ref, d_ref, nm_ref, nv_ref):
        g = p_ref[0].astype(F32)
        for s in range(1, n):
            g = g + p_ref[s].astype(F32)
        nm = ADAM_B1 * m_ref[...] + (1.0 - ADAM_B1) * g
        nv = ADAM_B2 * v_ref[...] + (1.0 - ADAM_B2) * (g * g)
        g_ref[...] = g
        nm_ref[...] = nm
        nv_ref[...] = nv
        d_ref[...] = -ADAM_LR * ((nm / c1) / (jnp.sqrt(nv / c2) + ADAM_EPS) + ADAM_WD * w_ref[...])

    row = pl.BlockSpec((tb, Wd), lambda i: (i, 0))
    return pl.pallas_call(
        body, name=name, grid=(R // tb,),
        in_specs=[pl.BlockSpec((n, tb, Wd), lambda i: (0, i, 0)), row, row, row],
        out_specs=[row, row, row, row], out_shape=[jax.ShapeDtypeStruct((R, Wd), F32)] * 4,
        compiler_params=_params(("arbitrary",)),
    )(parts, w, m, v)


def _unpack(buf, shapes):
    flat, out, off = buf.reshape(-1), [], 0
    for shp in shapes:
        n = shp[0] * shp[1]
        out.append(flat[off:off + n].reshape(shp))
        off += n
    return out


def _row_tile(R, unit, cap):
    best = unit
    for t in range(unit, cap + 1, unit):
        if R % t == 0:
            best = t
    return best


def kernel(x, ffn1_w_gate, ffn1_w_up, ffn1_w_down, ln1_g, ln1_b, w_in, b_in, ret_gn_g, conv_k, conv_b, conv_ln_g, conv_ln_b, w_ret_o, w_conv_o, w_out, ln2_g, ln2_b, ffn2_w_gate, ffn2_w_up, ffn2_w_down, ln3_g, ln3_b, loss_target, m_ffn1_w_gate, m_ffn1_w_up, m_ffn1_w_down, m_ln1_g, m_ln1_b, m_w_in, m_b_in, m_ret_gn_g, m_conv_k, m_conv_b, m_conv_ln_g, m_conv_ln_b, m_w_ret_o, m_w_conv_o, m_w_out, m_ln2_g, m_ln2_b, m_ffn2_w_gate, m_ffn2_w_up, m_ffn2_w_down, m_ln3_g, m_ln3_b, v_ffn1_w_gate, v_ffn1_w_up, v_ffn1_w_down, v_ln1_g, v_ln1_b, v_w_in, v_b_in, v_ret_gn_g, v_conv_k, v_conv_b, v_conv_ln_g, v_conv_ln_b, v_w_ret_o, v_w_conv_o, v_w_out, v_ln2_g, v_ln2_b, v_ffn2_w_gate, v_ffn2_w_up, v_ffn2_w_down, v_ln3_g, v_ln3_b):
    given = dict(locals())
    wts = {n: given[n] for n in WEIGHTS}
    mom = {n: given['m_' + n] for n in WEIGHTS}
    var = {n: given['v_' + n] for n in WEIGHTS}

    def shard2d(a):
        return a.reshape(a.shape[-3] * a.shape[-2] if a.ndim == 4 else a.shape[-2], a.shape[-1])

    comm = _Comm({n: shard2d(wts[n]).astype(BF16) for n in BIG})
    P = {n: wts[n].reshape(1, -1) for n in SMALL}
    loss, grad_x, _, S = _local_step(x[0], loss_target[0], None, P, comm=comm)

    parts = comm.parts
    res = {}
    for n in BIG:
        rows, cols = parts[n].shape[1:]
        tb = rows if rows % 16 else _row_tile(rows, 16, max(16, (256 * 1024) // cols))
        res[n] = _adamw("adamw_" + n, parts[n], shard2d(wts[n]), shard2d(mom[n]), shard2d(var[n]), tb)

    small_shapes = [(1, wts[n].shape[-1]) for n in SMALL]
    ns = sum(s[1] for s in small_shapes)
    RS = -(-ns // (128 * 8)) * 8
    pks = lambda d: jnp.pad(jnp.concatenate([d[n].reshape(-1) for n in SMALL]), (0, RS * 128 - ns)).reshape(RS, 128)
    (small_parts,) = _run_rider("gather_vector_grads", _GatherRider([pks(S)]))
    small = _adamw("adamw_vectors", small_parts, pks(wts), pks(mom), pks(var), RS)
    for n, vals in zip(SMALL, zip(*[_unpack(b, small_shapes) for b in small])):
        res[n] = vals

    loss = lax.psum(loss, ("x", "y", "c"))
    outs = [loss, grad_x[None]]
    for k in range(4):
        for n in WEIGHTS:
            outs.append(res[n][k].reshape(wts[n].shape))
    return tuple(outs)
```

```python
import functools
import math

import jax
import jax.numpy as jnp
from jax import lax
from jax.experimental import pallas as pl
from jax.experimental.pallas import tpu as pltpu

F32 = jnp.float32
BF16 = jnp.bfloat16

N_DEV = 8
LN_EPS = 1e-5
ALPHA = 2.0 ** 0.25
RET_DK = 128
RET_DV = 256
RET_CHUNK = 128
ROPE_BASE = 10000.0
CONV_WIDTH = 31
HALO = 32
ADAM_LR, ADAM_B1, ADAM_B2, ADAM_EPS, ADAM_WD, ADAM_STEP = 0.001, 0.9, 0.999, 1e-08, 0.01, 10
VMEM_LIMIT = 52 * 1024 * 1024
MESH = pl.DeviceIdType.MESH
ANY = pl.BlockSpec(memory_space=pl.ANY)

BIG = ['ffn1_w_gate', 'ffn1_w_up', 'ffn1_w_down', 'w_in', 'w_ret_o', 'w_conv_o', 'w_out',
       'ffn2_w_gate', 'ffn2_w_up', 'ffn2_w_down', 'conv_k']
COL_SHARDED = {'ffn1_w_gate', 'ffn1_w_up', 'w_in', 'ffn2_w_gate', 'ffn2_w_up', 'conv_k'}
SMALL = ['ln1_g', 'ln1_b', 'b_in', 'ret_gn_g', 'conv_b', 'conv_ln_g', 'conv_ln_b', 'ln2_g', 'ln2_b', 'ln3_g', 'ln3_b']
WEIGHTS = ['ffn1_w_gate', 'ffn1_w_up', 'ffn1_w_down', 'ln1_g', 'ln1_b', 'w_in', 'b_in', 'ret_gn_g', 'conv_k', 'conv_b',
           'conv_ln_g', 'conv_ln_b', 'w_ret_o', 'w_conv_o', 'w_out', 'ln2_g', 'ln2_b', 'ffn2_w_gate', 'ffn2_w_up',
           'ffn2_w_down', 'ln3_g', 'ln3_b']


def _params(sem=None):
    return pltpu.CompilerParams(dimension_semantics=sem, vmem_limit_bytes=VMEM_LIMIT)


def _sigmoid(x):
    return jax.nn.sigmoid(x)


def _dsilu(x, sg):
    return sg * (1.0 + x * (1.0 - sg))


def _fit(dim, want):
    if dim <= want:
        return dim
    return max(t for t in range(128, want + 1, 128) if dim % t == 0)


def _dot(a, b, ta=False, tb=False):
    dn = (((0,) if ta else (1,), (1,) if tb else (0,)), ((), ()))
    return lax.dot_general(a, b, dn, preferred_element_type=F32)


def _mm(name, As, Bs, prods, epi, out_dtypes, *, ta=False, tb=False, tm, tn, tk, extras=(), i_outer=True,
        b3=False, o3=False, rider=None):
    a0, b0 = As[0], Bs[0]
    M, K = (a0.shape[1], a0.shape[0]) if ta else a0.shape
    if b3:
        S, rows, cs = b0.shape
        N = rows if tb else S * cs
        assert K == (S * cs if tb else rows)
        tn, tk = (tn, cs) if tb else (cs, tk)
    else:
        N = b0.shape[0] if tb else b0.shape[1]
    tm, tn, tk = _fit(M, tm), _fit(N, tn), _fit(K, tk)
    assert M % tm == 0 and N % tn == 0 and K % tk == 0, (name, M, N, K, tm, tn, tk)
    gi, gj, gk = M // tm, N // tn, K // tk
    grid = (gi, gj, gk) if i_outer else (gj, gi, gk)

    def ij(g0, g1):
        return (g0, g1) if i_outer else (g1, g0)

    def amap(g0, g1, k):
        i, _ = ij(g0, g1)
        return (k, i) if ta else (i, k)

    def bmap(g0, g1, k):
        _, j = ij(g0, g1)
        return (j, k) if tb else (k, j)

    def bmap3(g0, g1, k):
        _, j = ij(g0, g1)
        return (k, j, 0) if tb else (j, k, 0)

    in_specs = [pl.BlockSpec((tk, tm) if ta else (tm, tk), amap) for _ in As]
    if b3:
        in_specs += [pl.BlockSpec((None, tn, tk) if tb else (None, tk, tn), bmap3) for _ in Bs]
    else:
        in_specs += [pl.BlockSpec((tn, tk) if tb else (tk, tn), bmap) for _ in Bs]
    args = list(As) + list(Bs)
    for arr, kind, coloff in extras:
        assert coloff % tn == 0
        off = coloff // tn
        if kind == 'mn':
            in_specs.append(pl.BlockSpec((tm, tn), lambda g0, g1, k, off=off: (ij(g0, g1)[0], ij(g0, g1)[1] + off)))
        else:
            in_specs.append(pl.BlockSpec((1, tn), lambda g0, g1, k, off=off: (0, ij(g0, g1)[1] + off)))
        args.append(arr)
    if o3:
        out_shape = [jax.ShapeDtypeStruct((gj, M, tn), dt) for dt in out_dtypes]
        out_specs = [pl.BlockSpec((None, tm, tn), lambda g0, g1, k: (ij(g0, g1)[1], ij(g0, g1)[0], 0))
                     for _ in out_dtypes]
    else:
        out_shape = [jax.ShapeDtypeStruct((M, N), dt) for dt in out_dtypes]
        out_specs = [pl.BlockSpec((tm, tn), lambda g0, g1, k: ij(g0, g1)) for _ in out_dtypes]
    n_a, n_b, n_e, n_o = len(As), len(Bs), len(extras), len(out_dtypes)
    n_p = len(prods) if gk > 1 else 0
    scratch = [pltpu.VMEM((tm, tn), F32) for _ in range(n_p)]
    if rider is not None:
        in_specs, out_specs = in_specs + [ANY] * len(rider.ins), out_specs + [ANY] * len(rider.out_shape)
        args, out_shape, scratch = args + rider.ins, out_shape + rider.out_shape, scratch + rider.scratch
    n_in, n_out = len(args), len(out_shape)

    def body(*refs):
        a_refs = refs[:n_a]
        b_refs = refs[n_a:n_a + n_b]
        e_refs = refs[n_a + n_b:n_a + n_b + n_e]
        o_refs = refs[n_in:n_in + n_o]
        acc_refs = refs[n_in + n_out:n_in + n_out + n_p]
        k = pl.program_id(2)
        if rider is not None:
            step = (pl.program_id(0) * grid[1] + pl.program_id(1)) * gk + k
            ride = (step, grid[0] * grid[1] * gk, refs[n_a + n_b + n_e:n_in], refs[n_in + n_o:n_in + n_out],
                    refs[n_in + n_out + n_p:])
            rider.begin(*ride)

        def finish(accs):
            for o, r in zip(o_refs, epi(accs, *[e[...] for e in e_refs])):
                o[...] = r.astype(o.dtype)

        if gk == 1:
            finish([functools.reduce(jnp.add, [_dot(a_refs[ai][...], b_refs[bi][...], ta, tb) for ai, bi in terms])
                    for terms in prods])
        else:
            @pl.when(k == 0)
            def _():
                for acc in acc_refs:
                    acc[...] = jnp.zeros_like(acc)

            for p, terms in enumerate(prods):
                for ai, bi in terms:
                    acc_refs[p][...] += _dot(a_refs[ai][...], b_refs[bi][...], ta, tb)

            @pl.when(k == gk - 1)
            def _():
                finish([acc[...] for acc in acc_refs])

        if rider is not None:
            rider.end(*ride)

    res = pl.pallas_call(
        body, name=name, grid=grid, in_specs=in_specs, out_specs=out_specs, out_shape=out_shape,
        scratch_shapes=scratch, compiler_params=_params(("arbitrary", "arbitrary", "arbitrary")),
    )(*args)
    if rider is not None:
        rider.results = res[n_o:]
    return res[:n_o]


def _rows(name, fn, ins, outs, *, T, tb):
    tb = min(tb, T)
    assert T % tb == 0
    in_specs, args = [], []
    for arr, kind, width, cb in ins:
        if kind == 'r':
            in_specs.append(pl.BlockSpec((tb, width), lambda i, cb=cb: (i, cb)))
        else:
            in_specs.append(pl.BlockSpec((1, width), lambda i, cb=cb: (0, cb)))
        args.append(arr)
    out_shape, out_specs = [], []
    for kind, width, dtype in outs:
        if kind == 'r':
            out_shape.append(jax.ShapeDtypeStruct((T, width), dtype))
            out_specs.append(pl.BlockSpec((tb, width), lambda i: (i, 0)))
        elif kind == 'c':
            out_shape.append(jax.ShapeDtypeStruct((T, 1), dtype))
            out_specs.append(pl.BlockSpec((tb, 1), lambda i: (i, 0)))
        else:
            out_shape.append(jax.ShapeDtypeStruct((1, width), F32))
            out_specs.append(pl.BlockSpec((1, width), lambda i: (0, 0)))
    n_in = len(ins)

    def body(*refs):
        i = pl.program_id(0)
        vals = fn(*[r[...] for r in refs[:n_in]])
        for (kind, _, _), o, v in zip(outs, refs[n_in:], vals):
            if kind == 'a':
                @pl.when(i == 0)
                def _(o=o):
                    o[...] = jnp.zeros_like(o)

                o[...] += v
            else:
                o[...] = v.astype(o.dtype)

    return pl.pallas_call(
        body, name=name, grid=(T // tb,), in_specs=in_specs, out_specs=out_specs, out_shape=out_shape,
        compiler_params=_params(("arbitrary",)),
    )(*args)


def _colsum(v):
    return jnp.sum(v, axis=0, keepdims=True)


def _ln_stats(z):
    mu = jnp.mean(z, axis=-1, keepdims=True)
    d = z - mu
    var = jnp.mean(d * d, axis=-1, keepdims=True)
    rstd = lax.rsqrt(var + LN_EPS)
    return d * rstd, rstd


def _ln_bwd_math(dy, xhat, rstd, g):
    dxh = dy * g
    m1 = jnp.mean(dxh, axis=-1, keepdims=True)
    m2 = jnp.mean(dxh * xhat, axis=-1, keepdims=True)
    return rstd * (dxh - m1 - xhat * m2)


def _ln_fwd(name, z, g, b, T, D):
    def fn(z, g, b):
        xhat, rstd = _ln_stats(z)
        y = xhat * g + b
        return [y, y, xhat, rstd]

    return _rows(name, fn, [(z, 'r', D, 0), (g, 'v', D, 0), (b, 'v', D, 0)],
                 [('r', D, F32), ('r', D, BF16), ('r', D, F32), ('c', 1, F32)], T=T, tb=512)


def _ln_bwd(name, dy, xhat, rstd, g, scale, T, D):
    def fn(dy, xhat, rstd, g):
        dz = _ln_bwd_math(dy, xhat, rstd, g)
        return [dz, dz * scale, _colsum(dy * xhat), _colsum(dy)]

    return _rows(name, fn, [(dy, 'r', D, 0), (xhat, 'r', D, 0), (rstd, 'r', 1, 0), (g, 'v', D, 0)],
                 [('r', D, F32), ('r', D, BF16), ('a', D, F32), ('a', D, F32)], T=T, tb=512)


def _ln_loss_bwd(name, z, g, b, tgt, T, D):
    def fn(z, g, b, tgt):
        xhat, rstd = _ln_stats(z)
        err = xhat * g + b - tgt
        row_loss = 0.5 * jnp.mean(err * err, axis=-1, keepdims=True)
        loss = jnp.broadcast_to(jnp.sum(row_loss, axis=0, keepdims=True), (1, 128))
        dy = err * (1.0 / D)
        dz = _ln_bwd_math(dy, xhat, rstd, g)
        return [dz, dz * 0.5, _colsum(dy * xhat), _colsum(dy), loss]

    return _rows(name, fn, [(z, 'r', D, 0), (g, 'v', D, 0), (b, 'v', D, 0), (tgt, 'r', D, 0)],
                 [('r', D, F32), ('r', D, BF16), ('a', D, F32), ('a', D, F32), ('a', 128, F32)], T=T, tb=512)


class _Net:
    def __init__(self, comm, G):
        self.comm, self.G = comm, G

    def gather(self, names):
        return self.comm.gather(names) if self.comm else None

    def exchange(self, names):
        return self.comm.exchange(names, self.G) if self.comm else None

    def done(self, rider):
        if rider is not None:
            self.comm.collect(rider)


def _ffn_fwd(tag, xb, x, W, names, net, rider=None):
    def epi_gu(accs):
        a, b = accs
        return [a, b, a * _sigmoid(a) * b]

    ng, nu, nd = names
    a, b, s = _mm(tag + "_gate_up", [xb], [W[ng], W[nu]], [[(0, 0)], [(0, 1)]], epi_gu, [F32, F32, BF16],
                  tm=2048, tn=256, tk=1024, rider=rider)
    net.done(rider)

    def epi_down(accs, xres):
        return [ALPHA * xres + 0.5 * accs[0]]

    (z,) = _mm(tag + "_down", [s], [W[nd]], [[(0, 0)]], epi_down, [F32], tm=1024, tn=1024, tk=1408,
               extras=[(x, 'mn', 0)])
    return a, b, s, z


def _ffn_bwd(tag, dzh, dz, xb, a, b, s, W, names, gdt, G, net, ride):
    ng, nu, nd = names

    def epi_ds(accs, a, b):
        ds = accs[0]
        sg = _sigmoid(a)
        return [ds * b * _dsilu(a, sg), ds * a * sg]

    da, db = _mm(tag + "_ds", [dzh], [W[nd]], [[(0, 0)]], epi_ds, [BF16, BF16], tb=True, tm=2048, tn=256, tk=1024,
                 extras=[(a, 'mn', 0), (b, 'mn', 0)])
    ident = lambda accs: accs
    (G[nd],) = _mm(tag + "_dwd", [s], [dzh], [[(0, 0)]], ident, [gdt], ta=True, tm=1408, tn=1024, tk=1024)
    rider = net.exchange([nd]) if ride else None
    G[ng], G[nu] = _mm(tag + "_dwgu", [xb], [da, db], [[(0, 0)], [(0, 1)]], ident, [gdt, gdt], ta=True,
                       tm=1024, tn=1408, tk=1024, rider=rider)
    net.done(rider)

    def epi_dx(accs, dzres):
        return [ALPHA * dzres + accs[0]]

    rider = net.exchange([ng, nu]) if ride else None
    (dx,) = _mm(tag + "_dx", [da, db], [W[ng], W[nu]], [[(0, 0), (1, 1)]], epi_dx, [F32], tb=True,
                tm=1024, tn=1024, tk=1408, extras=[(dz, 'mn', 0)], rider=rider)
    net.done(rider)
    return dx


def _ret_tables(H, T):
    C = RET_CHUNK
    log_g = jnp.log(1.0 - jnp.exp2(-5.0 - jnp.arange(H, dtype=F32)))
    idx = jnp.arange(C, dtype=F32)
    diff = idx[:, None] - idx[None, :]
    dm = jnp.where(diff[None] >= 0, jnp.exp(jnp.maximum(diff, 0.0)[None] * log_g[:, None, None]), 0.0)
    xi = jnp.exp((idx[None, :] + 1.0) * log_g[:, None])[:, :, None]
    zeta = jnp.exp((C - 1.0 - idx)[None, :] * log_g[:, None])[:, :, None]
    gc = jnp.broadcast_to(jnp.exp(C * log_g)[:, None, None], (H, 1, RET_DV))
    half = RET_DK // 2
    freqs = ROPE_BASE ** (-jnp.arange(half, dtype=F32) / half)
    ang = jnp.arange(T, dtype=F32)[:, None] * freqs[None, :]
    cos, sin = jnp.cos(ang), jnp.sin(ang)
    cosf = jnp.concatenate([cos, cos], axis=1)
    sins = jnp.concatenate([-sin, sin], axis=1)
    return dm, xi, zeta, gc, cosf, sins


def _rot(x, cosf, sins):
    return x * cosf + pltpu.roll(x, RET_DK // 2, 1) * sins


def _rot_bwd(dy, cosf, sins):
    return dy * cosf + pltpu.roll(dy * sins, RET_DK // 2, 1)


RET_HB = 4


def _ret_specs(H, HB, rev, NC):
    C, G = RET_CHUNK, H // HB
    nn = (lambda n: NC - 1 - n) if rev else (lambda n: n)
    return [
        pl.BlockSpec((C, HB * RET_DK), lambda h, n: (nn(n), h)),
        pl.BlockSpec((C, HB * RET_DK), lambda h, n: (nn(n), G + h)),
        pl.BlockSpec((C, HB * RET_DV), lambda h, n: (nn(n), G + h)),
        pl.BlockSpec((C, HB * RET_DV), lambda h, n: (nn(n), 2 * G + h)),
        pl.BlockSpec((C, RET_DK), lambda h, n: (nn(n), 0)),
        pl.BlockSpec((C, RET_DK), lambda h, n: (nn(n), 0)),
        pl.BlockSpec((1, HB * RET_DV), lambda h, n: (0, h)),
        pl.BlockSpec((HB, C, C), lambda h, n: (h, 0, 0)),
        pl.BlockSpec((HB, C, 1), lambda h, n: (h, 0, 0)),
        pl.BlockSpec((HB, C, 1), lambda h, n: (h, 0, 0)),
        pl.BlockSpec((HB, 1, RET_DV), lambda h, n: (h, 0, 0)),
    ]


def _ret_fwd(proj, gn_g, tabs, H, T, rider=None):
    C, NC = RET_CHUNK, T // RET_CHUNK
    HB = min(RET_HB, H)
    dm, xi, zeta, gc, cosf, sins = tabs
    scale = RET_DK ** -0.5

    def body(q_ref, k_ref, v_ref, g_ref, cos_ref, sin_ref, gn_ref, dm_ref, xi_ref, zt_ref, gc_ref,
             r_ref, ri_ref, st_ref, state):
        @pl.when(pl.program_id(1) == 0)
        def _():
            state[...] = jnp.zeros_like(state)

        cs, sn = cos_ref[...], sin_ref[...]
        for hh in range(HB):
            qk = slice(hh * RET_DK, (hh + 1) * RET_DK)
            vv = slice(hh * RET_DV, (hh + 1) * RET_DV)
            qr = _rot(q_ref[:, qk], cs, sn) * scale
            kr = _rot(k_ref[:, qk], cs, sn)
            qb, kb, vb = qr.astype(BF16), kr.astype(BF16), v_ref[:, vv].astype(BF16)
            st = state[hh]
            stb = st.astype(BF16)
            s = _dot(qb, kb, tb=True) * dm_ref[hh]
            r = _dot(s.astype(BF16), vb) + _dot(qb, stb) * xi_ref[hh]
            st_ref[hh] = stb
            state[hh] = gc_ref[hh] * st + _dot((kr * zt_ref[hh]).astype(BF16), vb, ta=True)
            rhat, _ = _ln_stats(r)
            g = g_ref[:, vv]
            r_ref[:, vv] = r
            ri_ref[:, vv] = (g * _sigmoid(g) * (rhat * gn_ref[:, vv])).astype(BF16)

    VW = H * RET_DV
    return _hosted_call(
        body, rider, name="ret_fwd", grid=(H // HB, NC), in_specs=_ret_specs(H, HB, False, NC),
        out_specs=[pl.BlockSpec((C, HB * RET_DV), lambda h, n: (n, h)),
                   pl.BlockSpec((C, HB * RET_DV), lambda h, n: (n, h)),
                   pl.BlockSpec((HB, None, RET_DK, RET_DV), lambda h, n: (h, n, 0, 0))],
        out_shape=[jax.ShapeDtypeStruct((T, VW), F32), jax.ShapeDtypeStruct((T, VW), BF16),
                   jax.ShapeDtypeStruct((H, NC, RET_DK, RET_DV), BF16)],
        scratch=[pltpu.VMEM((HB, RET_DK, RET_DV), F32)],
        args=[proj, proj, proj, proj, cosf, sins, gn_g, dm, xi, zeta, gc])


def _hosted_call(body, rider, *, name, grid, in_specs, out_specs, out_shape, scratch, args):
    n_in, n_out, n_scr = len(args), len(out_shape), len(scratch)
    if rider is None:
        hosted = body
    else:
        n_ri, n_ro = len(rider.ins), len(rider.out_shape)
        in_specs, out_specs = in_specs + [ANY] * n_ri, out_specs + [ANY] * n_ro
        args, out_shape, scratch = args + rider.ins, out_shape + rider.out_shape, scratch + rider.scratch

        def hosted(*refs):
            o0, s0 = n_in + n_ri, n_in + n_ri + n_out + n_ro
            step = pl.program_id(0) * grid[1] + pl.program_id(1)
            ride = (step, grid[0] * grid[1], refs[n_in:o0], refs[o0 + n_out:s0], refs[s0 + n_scr:])
            rider.begin(*ride)
            body(*refs[:n_in], *refs[o0:o0 + n_out], *refs[s0:s0 + n_scr])
            rider.end(*ride)

    res = pl.pallas_call(
        hosted, name=name, grid=grid, in_specs=in_specs, out_specs=out_specs, out_shape=out_shape,
        scratch_shapes=scratch, compiler_params=_params(("arbitrary", "arbitrary")),
    )(*args)
    if rider is not None:
        rider.results = res[n_out:]
    return res[:n_out]


def _ret_bwd(dri, r, states, proj, gn_g, tabs, H, T, rider=None):
    C, NC = RET_CHUNK, T // RET_CHUNK
    HB = min(RET_HB, H)
    dm, xi, zeta, gc, cosf, sins = tabs
    scale = RET_DK ** -0.5

    def body(q_ref, k_ref, v_ref, g_ref, cos_ref, sin_ref, gn_ref, dm_ref, xi_ref, zt_ref, gc_ref,
             dri_ref, r_ref, st_ref, dq_ref, dk_ref, dv_ref, dg_ref, dgn_ref, dstate):
        @pl.when(pl.program_id(1) == 0)
        def _():
            dstate[...] = jnp.zeros_like(dstate)
            dgn_ref[...] = jnp.zeros_like(dgn_ref)

        cs, sn = cos_ref[...], sin_ref[...]
        for hh in range(HB):
            qk = slice(hh * RET_DK, (hh + 1) * RET_DK)
            vv = slice(hh * RET_DV, (hh + 1) * RET_DV)
            qr = _rot(q_ref[:, qk], cs, sn) * scale
            kr = _rot(k_ref[:, qk], cs, sn)
            qb, kb, vb = qr.astype(BF16), kr.astype(BF16), v_ref[:, vv].astype(BF16)
            xi_c, zt_c, dmask = xi_ref[hh], zt_ref[hh], dm_ref[hh]
            rhat, rstd = _ln_stats(r_ref[:, vv])
            g, gn, dpre = g_ref[:, vv], gn_ref[:, vv], dri_ref[:, vv]
            sg = _sigmoid(g)
            dg_ref[:, vv] = (dpre * (rhat * gn) * _dsilu(g, sg)).astype(BF16)
            drn = dpre * (g * sg)
            dgn_ref[:, vv] += _colsum(drn * rhat)
            drb = _ln_bwd_math(drn, rhat, rstd, gn).astype(BF16)
            ds1 = dstate[hh]
            ds1b = ds1.astype(BF16)
            sb = (_dot(qb, kb, tb=True) * dmask).astype(BF16)
            kzb = (kr * zt_c).astype(BF16)
            dv_ref[:, vv] = (_dot(sb, drb, ta=True) + _dot(kzb, ds1b)).astype(BF16)
            dsb = (_dot(drb, vb, tb=True) * dmask).astype(BF16)
            dq = _dot(dsb, kb) + _dot(drb, st_ref[hh], tb=True) * xi_c
            dk = _dot(dsb, qb, ta=True) + _dot(vb, ds1b, tb=True) * zt_c
            dstate[hh] = gc_ref[hh] * ds1 + _dot((qr * xi_c).astype(BF16), drb, ta=True)
            dq_ref[:, qk] = _rot_bwd(dq * scale, cs, sn).astype(BF16)
            dk_ref[:, qk] = _rot_bwd(dk, cs, sn).astype(BF16)

    VW, QW = H * RET_DV, H * RET_DK
    rv = lambda n: NC - 1 - n
    in_specs = _ret_specs(H, HB, True, NC) + [
        pl.BlockSpec((C, HB * RET_DV), lambda h, n: (rv(n), h)),
        pl.BlockSpec((C, HB * RET_DV), lambda h, n: (rv(n), h)),
        pl.BlockSpec((HB, None, RET_DK, RET_DV), lambda h, n: (h, rv(n), 0, 0)),
    ]
    return _hosted_call(
        body, rider, name="ret_bwd", grid=(H // HB, NC), in_specs=in_specs,
        out_specs=[pl.BlockSpec((C, HB * RET_DK), lambda h, n: (rv(n), h)),
                   pl.BlockSpec((C, HB * RET_DK), lambda h, n: (rv(n), h)),
                   pl.BlockSpec((C, HB * RET_DV), lambda h, n: (rv(n), h)),
                   pl.BlockSpec((C, HB * RET_DV), lambda h, n: (rv(n), h)),
                   pl.BlockSpec((1, HB * RET_DV), lambda h, n: (0, h))],
        out_shape=[jax.ShapeDtypeStruct((T, QW), BF16), jax.ShapeDtypeStruct((T, QW), BF16),
                   jax.ShapeDtypeStruct((T, VW), BF16), jax.ShapeDtypeStruct((T, VW), BF16),
                   jax.ShapeDtypeStruct((1, VW), F32)],
        scratch=[pltpu.VMEM((HB, RET_DK, RET_DV), F32)],
        args=[proj, proj, proj, proj, cosf, sins, gn_g, dm, xi, zeta, gc, dri, r, states])


CONV_CW = 128
CONV_TB = 512


def _conv_fwd(proj, kpad, bias, off_a, CC, T):
    tb, cw = min(CONV_TB, T), CONV_CW
    hb = tb // HALO
    ca, cb = off_a // cw, (off_a + CC) // cw

    def body(a_ref, b_ref, ap_ref, bp_ref, k_ref, bias_ref, u1_ref, win):
        i = pl.program_id(0)
        keep = (i > 0).astype(F32)
        win[0:HALO, :] = ap_ref[...] * _sigmoid(bp_ref[...]) * keep
        win[HALO:, :] = a_ref[...] * _sigmoid(b_ref[...])
        acc = jnp.broadcast_to(bias_ref[...], (tb, cw))
        for w in range(CONV_WIDTH):
            acc = acc + k_ref[w:w + 1, :] * win[pl.ds(HALO - (CONV_WIDTH - 1) + w, tb), :]
        u1_ref[...] = acc

    prev = lambda i: jnp.maximum(i * hb - 1, 0)
    return pl.pallas_call(
        body, name="conv_fwd", grid=(T // tb, CC // cw),
        in_specs=[pl.BlockSpec((tb, cw), lambda i, c: (i, ca + c)),
                  pl.BlockSpec((tb, cw), lambda i, c: (i, cb + c)),
                  pl.BlockSpec((HALO, cw), lambda i, c: (prev(i), ca + c)),
                  pl.BlockSpec((HALO, cw), lambda i, c: (prev(i), cb + c)),
                  pl.BlockSpec((HALO, cw), lambda i, c: (0, c)),
                  pl.BlockSpec((1, cw), lambda i, c: (0, c))],
        out_specs=pl.BlockSpec((tb, cw), lambda i, c: (i, c)),
        out_shape=jax.ShapeDtypeStruct((T, CC), F32),
        scratch_shapes=[pltpu.VMEM((tb + HALO, cw), F32)],
        compiler_params=_params(("arbitrary", "arbitrary")),
    )(proj, proj, proj, proj, kpad, bias)


def _conv_bwd(du1, proj, kpad, off_a, CC, T, rider=None):
    tb, cw = min(CONV_TB, T), CONV_CW
    hb = tb // HALO
    nt = T // tb
    ca, cb = off_a // cw, (off_a + CC) // cw

    def body(d_ref, dn_ref, a_ref, b_ref, ap_ref, bp_ref, k_ref, da_ref, db_ref, dk_ref, winu, wind):
        i = pl.program_id(1)
        a, b = a_ref[...], b_ref[...]
        sgb = _sigmoid(b)
        winu[0:HALO, :] = ap_ref[...] * _sigmoid(bp_ref[...]) * (i > 0).astype(F32)
        winu[HALO:, :] = a * sgb
        d = d_ref[...]
        wind[0:tb, :] = d
        wind[tb:, :] = dn_ref[...] * (i < nt - 1).astype(F32)

        @pl.when(i == 0)
        def _():
            dk_ref[...] = jnp.zeros_like(dk_ref)

        du0 = jnp.zeros((tb, cw), F32)
        for w in range(CONV_WIDTH):
            du0 = du0 + k_ref[w:w + 1, :] * wind[pl.ds(CONV_WIDTH - 1 - w, tb), :]
            dk_ref[w:w + 1, :] += _colsum(winu[pl.ds(HALO - (CONV_WIDTH - 1) + w, tb), :] * d)
        da_ref[...] = (du0 * sgb).astype(BF16)
        db_ref[...] = (du0 * a * sgb * (1.0 - sgb)).astype(BF16)

    prev = lambda i: jnp.maximum(i * hb - 1, 0)
    nxt = lambda i: jnp.minimum((i + 1) * hb, T // HALO - 1)
    return _hosted_call(
        body, rider, name="conv_bwd", grid=(CC // cw, nt),
        in_specs=[pl.BlockSpec((tb, cw), lambda c, i: (i, c)),
                  pl.BlockSpec((HALO, cw), lambda c, i: (nxt(i), c)),
                  pl.BlockSpec((tb, cw), lambda c, i: (i, ca + c)),
                  pl.BlockSpec((tb, cw), lambda c, i: (i, cb + c)),
                  pl.BlockSpec((HALO, cw), lambda c, i: (prev(i), ca + c)),
                  pl.BlockSpec((HALO, cw), lambda c, i: (prev(i), cb + c)),
                  pl.BlockSpec((HALO, cw), lambda c, i: (0, c))],
        out_specs=[pl.BlockSpec((tb, cw), lambda c, i: (i, c)),
                   pl.BlockSpec((tb, cw), lambda c, i: (i, c)),
                   pl.BlockSpec((HALO, cw), lambda c, i: (0, c))],
        out_shape=[jax.ShapeDtypeStruct((T, CC), BF16), jax.ShapeDtypeStruct((T, CC), BF16),
                   jax.ShapeDtypeStruct((HALO, CC), F32)],
        scratch=[pltpu.VMEM((tb + HALO, cw), F32), pltpu.VMEM((tb + HALO, cw), F32)],
        args=[du1, du1, proj, proj, proj, proj, kpad])


FFN1 = ('ffn1_w_gate', 'ffn1_w_up', 'ffn1_w_down')
FFN2 = ('ffn2_w_gate', 'ffn2_w_up', 'ffn2_w_down')


def _local_step(x, tgt, W, P, gdt=BF16, comm=None):
    T, D = x.shape
    G = {}
    net = _Net(comm, G)
    if comm is not None:
        W = comm.W
        first = net.gather(['ffn1_w_gate', 'ffn1_w_up'])
        _run_rider("gather_ffn1_in", first)
        net.done(first)
    VW = P['ret_gn_g'].shape[1]
    H = VW // RET_DV
    QW = H * RET_DK
    CC = P['conv_b'].shape[1]
    off_glu = 2 * QW + 2 * VW
    off_gate = off_glu + 2 * CC
    ident = lambda accs: accs
    xb = x.astype(BF16)

    a1, b1, s1, z1 = _ffn_fwd("ffn1", xb, x, W, FFN1, net, rider=net.gather(['ffn1_w_down', 'w_in']))
    x1, x1b, xh1, rs1 = _ln_fwd("ln1", z1, P['ln1_g'], P['ln1_b'], T, D)

    rest = net.gather(['conv_k', 'w_ret_o', 'w_conv_o', 'w_out'])
    (proj,) = _mm("w_in", [x1b], [W['w_in']], [[(0, 0)]], lambda accs, bias: [accs[0] + bias], [F32],
                  tm=2048, tn=0, tk=1024, extras=[(P['b_in'], 'n', 0)], i_outer=False, b3=True, rider=rest)
    net.done(rest)
    tabs = _ret_tables(H, T)
    rider = net.gather(list(FFN2))
    r, ret_in, states = _ret_fwd(proj, P['ret_gn_g'], tabs, H, T, rider=rider)
    net.done(rider)
    kpad = jnp.pad(W['conv_k'].astype(F32), ((0, HALO - CONV_WIDTH), (0, 0)))
    u1 = _conv_fwd(proj, kpad, P['conv_b'], off_glu, CC, T)

    def conv_ln(u1, g, b):
        xhat, rstd = _ln_stats(u1)
        u2 = xhat * g + b
        return [xhat, rstd, u2 * _sigmoid(u2)]

    xhc, rsc, u3 = _rows("conv_ln", conv_ln, [(u1, 'r', CC, 0), (P['conv_ln_g'], 'v', CC, 0), (P['conv_ln_b'], 'v', CC, 0)],
                         [('r', CC, F32), ('c', 1, F32), ('r', CC, BF16)], T=T, tb=512)
    (ret_out,) = _mm("ret_o", [ret_in], [W['w_ret_o']], [[(0, 0)]], ident, [F32], tm=1024, tn=1024, tk=2048)

    def epi_merge(accs, ret_out, gr, gc):
        conv_out = accs[0]
        return [conv_out, _sigmoid(gr) * ret_out + _sigmoid(gc) * conv_out]

    conv_out, merged = _mm("conv_o_merge", [u3], [W['w_conv_o']], [[(0, 0)]], epi_merge, [F32, BF16],
                           tm=512, tn=D, tk=1024,
                           extras=[(ret_out, 'mn', 0), (proj, 'mn', off_gate), (proj, 'mn', off_gate + D)])
    (z2,) = _mm("w_out", [merged], [W['w_out']], [[(0, 0)]], lambda accs, xr: [ALPHA * xr + accs[0]], [F32],
                tm=1024, tn=1024, tk=1024, extras=[(x1, 'mn', 0)])
    x2, x2b, xh2, rs2 = _ln_fwd("ln2", z2, P['ln2_g'], P['ln2_b'], T, D)
    a2, b2, s2, z3 = _ffn_fwd("ffn2", x2b, x2, W, FFN2, net)
    dz3, dz3h, g_ln3_g, g_ln3_b, loss = _ln_loss_bwd("ln3_loss", z3, P['ln3_g'], P['ln3_b'], tgt, T, D)

    S = {'ln3_g': g_ln3_g, 'ln3_b': g_ln3_b}
    dy2 = _ffn_bwd("ffn2b", dz3h, dz3, x2b, a2, b2, s2, W, FFN2, gdt, G, net, False)
    dz2, dz2b, S['ln2_g'], S['ln2_b'] = _ln_bwd("ln2b", dy2, xh2, rs2, P['ln2_g'], 1.0, T, D)

    (G['w_out'],) = _mm("d_w_out", [merged], [dz2b], [[(0, 0)]], ident, [gdt], ta=True, tm=1024, tn=1024, tk=1024)

    def epi_dmerge(accs, ret_out, conv_out, gr, gc):
        dm_ = accs[0]
        sr, sc = _sigmoid(gr), _sigmoid(gc)
        return [dm_ * sr, dm_ * sc, dm_ * ret_out * sr * (1.0 - sr), dm_ * conv_out * sc * (1.0 - sc)]

    dret_out, dconv_out, dgate_r, dgate_c = _mm(
        "d_merge", [dz2b], [W['w_out']], [[(0, 0)]], epi_dmerge, [BF16, BF16, BF16, BF16], tb=True,
        tm=512, tn=D, tk=1024,
        extras=[(ret_out, 'mn', 0), (conv_out, 'mn', 0), (proj, 'mn', off_gate), (proj, 'mn', off_gate + D)])
    (G['w_ret_o'],) = _mm("d_w_ret_o", [ret_in], [dret_out], [[(0, 0)]], ident, [gdt], ta=True, tm=1024, tn=1024, tk=1024)
    (G['w_conv_o'],) = _mm("d_w_conv_o", [u3], [dconv_out], [[(0, 0)]], ident, [gdt], ta=True, tm=1024, tn=1024, tk=1024)
    (dri,) = _mm("d_ret_in", [dret_out], [W['w_ret_o']], [[(0, 0)]], ident, [F32], tb=True, tm=1024, tn=1024, tk=1024)
    rider = net.exchange(list(FFN2))
    dq, dk, dv, dg, S['ret_gn_g'] = _ret_bwd(dri, r, states, proj, P['ret_gn_g'], tabs, H, T, rider=rider)
    net.done(rider)

    def epi_du2(accs, xhat, g, b):
        u2 = xhat * g + b
        return [accs[0] * _dsilu(u2, _sigmoid(u2))]

    (du2,) = _mm("d_u3", [dconv_out], [W['w_conv_o']], [[(0, 0)]], epi_du2, [F32], tb=True, tm=512, tn=CC, tk=1024,
                 extras=[(xhc, 'mn', 0), (P['conv_ln_g'], 'n', 0), (P['conv_ln_b'], 'n', 0)])

    def conv_ln_bwd(du2, xhat, rstd, g):
        du1 = _ln_bwd_math(du2, xhat, rstd, g)
        return [du1, _colsum(du2 * xhat), _colsum(du2), _colsum(du1)]

    du1, S['conv_ln_g'], S['conv_ln_b'], S['conv_b'] = _rows(
        "conv_ln_bwd", conv_ln_bwd, [(du2, 'r', CC, 0), (xhc, 'r', CC, 0), (rsc, 'r', 1, 0), (P['conv_ln_g'], 'v', CC, 0)],
        [('r', CC, F32), ('a', CC, F32), ('a', CC, F32), ('a', CC, F32)], T=T, tb=512)
    rider = net.exchange(['w_out', 'w_ret_o', 'w_conv_o'])
    dglu_a, dglu_b, dkpad = _conv_bwd(du1, proj, kpad, off_glu, CC, T, rider=rider)
    net.done(rider)
    G['conv_k'] = dkpad[:CONV_WIDTH].astype(gdt)

    dproj = jnp.concatenate([dq, dk, dv, dg, dglu_a, dglu_b, dgate_r, dgate_c], axis=1)
    IN_W = dproj.shape[1]
    (S['b_in'],) = _rows("d_b_in", lambda d: [_colsum(d.astype(F32))], [(dproj, 'r', IN_W, 0)], [('a', IN_W, F32)],
                         T=T, tb=256)
    (G['w_in'],) = _mm("d_w_in", [x1b], [dproj], [[(0, 0)]], ident, [gdt], ta=True, o3=True,
                       tm=1024, tn=W['w_in'].shape[2], tk=1024)
    rider = net.exchange(['w_in', 'conv_k'])
    (dy1,) = _mm("d_x1", [dproj], [W['w_in']], [[(0, 0)]], lambda accs, dzr: [ALPHA * dzr + accs[0]], [F32], tb=True,
                 b3=True, tm=1024, tn=1024, tk=0, extras=[(dz2, 'mn', 0)], rider=rider)
    net.done(rider)
    dz1, dz1h, S['ln1_g'], S['ln1_b'] = _ln_bwd("ln1b", dy1, xh1, rs1, P['ln1_g'], 0.5, T, D)
    grad_x = _ffn_bwd("ffn1b", dz1h, dz1, xb, a1, b1, s1, W, FFN1, gdt, G, net, True)
    return loss[0, 0], grad_x, G, S


def _coords():
    return lax.axis_index("x"), lax.axis_index("y"), lax.axis_index("c")


def _flip(k, x, y, c):
    return (1 - x if k & 4 else x, 1 - y if k & 2 else y, 1 - c if k & 1 else c)


def _lin(p):
    return 4 * p[0] + 2 * p[1] + p[2]


class _Rider:
    def __init__(self, ins, out_shape):
        nb = len(ins)
        self.ins, self.out_shape, self.results = list(ins), list(out_shape), None
        self.scratch = [pltpu.SemaphoreType.DMA((7 * nb,)), pltpu.SemaphoreType.DMA((7 * nb,)),
                        pltpu.SemaphoreType.DMA((nb,))]

    def begin(self, step, n_steps, ins, outs, sems):
        @pl.when(step == 0)
        def _():
            self.start(ins, outs, sems)

        @pl.when(step == n_steps // 2)
        def _():
            self.mid(ins, outs, sems)

    def end(self, step, n_steps, ins, outs, sems):
        @pl.when(step == n_steps - 1)
        def _():
            self.finish(ins, outs, sems)

    def mid(self, ins, outs, sems):
        pass


class _GatherRider(_Rider):
    def __init__(self, blks):
        super().__init__(blks, [jax.ShapeDtypeStruct((N_DEV,) + b.shape, b.dtype) for b in blks])

    def _copies(self, x_refs, out_refs, sems):
        nb = len(x_refs)
        send_sems, recv_sems, local_sems = sems
        x, y, c = _coords()
        me, sibling = (x, y, c), (x, y, 1 - c)
        chips = [_flip(4, x, y, c), _flip(2, x, y, c), _flip(6, x, y, c)]

        def copy(k, w, block, to, src=None):
            slot = out_refs[w].at[_lin(block)]
            return pltpu.make_async_remote_copy(
                src_ref=slot if src is None else src, dst_ref=slot, send_sem=send_sems.at[k * nb + w],
                recv_sem=recv_sems.at[k * nb + w], device_id=to, device_id_type=MESH)

        mines = [pltpu.make_async_copy(x_refs[w], out_refs[w].at[_lin(me)], local_sems.at[w]) for w in range(nb)]
        first = [copy(0, w, me, sibling, src=x_refs[w]) for w in range(nb)]
        first += [copy(1 + j, w, me, chip, src=x_refs[w]) for w in range(nb) for j, chip in enumerate(chips)]
        landed = [(copy(1 + j, w, chip, me), copy(4 + j, w, chip, sibling)) for w in range(nb) for j, chip in enumerate(chips)]
        from_sibling = [copy(0, w, sibling, me) for w in range(nb)]
        from_sibling += [copy(4 + j, w, (chip[0], chip[1], 1 - c), me) for w in range(nb) for j, chip in enumerate(chips)]
        return mines, first, landed, from_sibling

    def start(self, ins, outs, sems):
        mines, first, _, _ = self._copies(ins, outs, sems)
        for cp in mines + first:
            cp.start()

    def mid(self, ins, outs, sems):
        for arrival, onward in self._copies(ins, outs, sems)[2]:
            arrival.wait_recv()
            onward.start()

    def finish(self, ins, outs, sems):
        mines, first, landed, from_sibling = self._copies(ins, outs, sems)
        for cp in from_sibling:
            cp.wait_recv()
        for cp in first + [onward for _, onward in landed]:
            cp.wait_send()
        for mine in mines:
            mine.wait()


class _ExchangeRider(_Rider):
    def __init__(self, gs):
        super().__init__(gs, [jax.ShapeDtypeStruct(g.shape, g.dtype) for g in gs])

    def _copies(self, g_refs, out_refs, sems):
        nb = len(g_refs)
        send_sems, recv_sems, local_sems = sems
        x, y, c = _coords()
        me = _lin((x, y, c))

        def copy(k, w, landing):
            peer = _flip(k, x, y, c)
            src, dst = (me, _lin(peer)) if landing else (_lin(peer), me)
            return pltpu.make_async_remote_copy(
                src_ref=g_refs[w].at[src], dst_ref=out_refs[w].at[dst], send_sem=send_sems.at[(k - 1) * nb + w],
                recv_sem=recv_sems.at[(k - 1) * nb + w], device_id=peer, device_id_type=MESH)

        mines = [pltpu.make_async_copy(g_refs[w].at[me], out_refs[w].at[me], local_sems.at[w]) for w in range(nb)]
        sends = [copy(k, w, False) for w in range(nb) for k in range(1, N_DEV)]
        landings = [copy(k, w, True) for w in range(nb) for k in range(1, N_DEV)]
        return mines, sends, landings

    def start(self, ins, outs, sems):
        mines, sends, _ = self._copies(ins, outs, sems)
        for cp in mines + sends:
            cp.start()

    def finish(self, ins, outs, sems):
        mines, sends, landings = self._copies(ins, outs, sems)
        for cp in landings:
            cp.wait_recv()
        for cp in sends:
            cp.wait_send()
        for mine in mines:
            mine.wait()


def _run_rider(name, rider):
    n_in, n_out = len(rider.ins), len(rider.out_shape)

    def body(*refs):
        ride = (refs[:n_in], refs[n_in:n_in + n_out], refs[n_in + n_out:])
        rider.start(*ride)
        rider.mid(*ride)
        rider.finish(*ride)

    rider.results = pl.pallas_call(
        body, name=name, out_shape=rider.out_shape, in_specs=[ANY] * n_in, out_specs=[ANY] * n_out,
        scratch_shapes=rider.scratch, compiler_params=pltpu.CompilerParams(has_side_effects=True),
    )(*rider.ins)
    return rider.results


def _as_matrix(name, g):
    if name == 'w_in':
        return g
    if name in COL_SHARDED:
        return jnp.transpose(g, (1, 0, 2)).reshape(g.shape[1], N_DEV * g.shape[2])
    return g.reshape(N_DEV * g.shape[1], g.shape[2])


def _by_owner(name, g):
    if name == 'w_in':
        return g
    if name in COL_SHARDED:
        return jnp.transpose(g.reshape(g.shape[0], N_DEV, g.shape[1] // N_DEV), (1, 0, 2))
    return g.reshape(N_DEV, g.shape[0] // N_DEV, g.shape[1])


class _Comm:
    def __init__(self, shards):
        self.shards, self.W, self.parts = shards, {}, {}

    def gather(self, names):
        rider = _GatherRider([self.shards[n] for n in names])
        rider.names, rider.sink = names, 'W'
        return rider

    def exchange(self, names, G):
        rider = _ExchangeRider([_by_owner(n, G[n]) for n in names])
        rider.names, rider.sink = names, 'parts'
        return rider

    def collect(self, rider):
        for n, res in zip(rider.names, rider.results):
            if rider.sink == 'W':
                self.W[n] = _as_matrix(n, res)
            else:
                self.parts[n] = res


def _adamw(name, parts, w, m, v, tb):
    n, R, Wd = parts.shape
    assert R % tb == 0
    c1 = 1.0 - ADAM_B1 ** ADAM_STEP
    c2 = 1.0 - ADAM_B2 ** ADAM_STEP

    def body(p_ref, w_ref, m_ref, v_ref, g_ref, d_ref, nm_ref, nv_ref):
        g = p_ref[0].astype(F32)
        for s in range(1, n):
            g = g + p_ref[s].astype(F32)
        nm = ADAM_B1 * m_ref[...] + (1.0 - ADAM_B1) * g
        nv = ADAM_B2 * v_ref[...] + (1.0 - ADAM_B2) * (g * g)
        g_ref[...] = g
        nm_ref[...] = nm
        nv_ref[...] = nv
        d_ref[...] = -ADAM_LR * ((nm / c1) / (jnp.sqrt(nv / c2) + ADAM_EPS) + ADAM_WD * w_ref[...])

    row = pl.BlockSpec((tb, Wd), lambda i: (i, 0))
    return pl.pallas_call(
        body, name=name, grid=(R // tb,),
        in_specs=[pl.BlockSpec((n, tb, Wd), lambda i: (0, i, 0)), row, row, row],
        out_specs=[row, row, row, row], out_shape=[jax.ShapeDtypeStruct((R, Wd), F32)] * 4,
        compiler_params=_params(("arbitrary",)),
    )(parts, w, m, v)


def _unpack(buf, shapes):
    flat, out, off = buf.reshape(-1), [], 0
    for shp in shapes:
        n = shp[0] * shp[1]
        out.append(flat[off:off + n].reshape(shp))
        off += n
    return out


def _row_tile(R, unit, cap):
    best = unit
    for t in range(unit, cap + 1, unit):
        if R % t == 0:
            best = t
    return best


def kernel(x, ffn1_w_gate, ffn1_w_up, ffn1_w_down, ln1_g, ln1_b, w_in, b_in, ret_gn_g, conv_k, conv_b, conv_ln_g, conv_ln_b, w_ret_o, w_conv_o, w_out, ln2_g, ln2_b, ffn2_w_gate, ffn2_w_up, ffn2_w_down, ln3_g, ln3_b, loss_target, m_ffn1_w_gate, m_ffn1_w_up, m_ffn1_w_down, m_ln1_g, m_ln1_b, m_w_in, m_b_in, m_ret_gn_g, m_conv_k, m_conv_b, m_conv_ln_g, m_conv_ln_b, m_w_ret_o, m_w_conv_o, m_w_out, m_ln2_g, m_ln2_b, m_ffn2_w_gate, m_ffn2_w_up, m_ffn2_w_down, m_ln3_g, m_ln3_b, v_ffn1_w_gate, v_ffn1_w_up, v_ffn1_w_down, v_ln1_g, v_ln1_b, v_w_in, v_b_in, v_ret_gn_g, v_conv_k, v_conv_b, v_conv_ln_g, v_conv_ln_b, v_w_ret_o, v_w_conv_o, v_w_out, v_ln2_g, v_ln2_b, v_ffn2_w_gate, v_ffn2_w_up, v_ffn2_w_down, v_ln3_g, v_ln3_b):
    given = dict(locals())
    wts = {n: given[n] for n in WEIGHTS}
    mom = {n: given['m_' + n] for n in WEIGHTS}
    var = {n: given['v_' + n] for n in WEIGHTS}

    def shard2d(a):
        return a.reshape(a.shape[-3] * a.shape[-2] if a.ndim == 4 else a.shape[-2], a.shape[-1])

    comm = _Comm({n: shard2d(wts[n]).astype(BF16) for n in BIG})
    P = {n: wts[n].reshape(1, -1) for n in SMALL}
    loss, grad_x, _, S = _local_step(x[0], loss_target[0], None, P, comm=comm)

    parts = comm.parts
    res = {}
    for n in BIG:
        rows, cols = parts[n].shape[1:]
        tb = rows if rows % 16 else _row_tile(rows, 16, max(16, (256 * 1024) // cols))
        res[n] = _adamw("adamw_" + n, parts[n], shard2d(wts[n]), shard2d(mom[n]), shard2d(var[n]), tb)

    small_shapes = [(1, wts[n].shape[-1]) for n in SMALL]
    ns = sum(s[1] for s in small_shapes)
    RS = -(-ns // (128 * 8)) * 8
    pks = lambda d: jnp.pad(jnp.concatenate([d[n].reshape(-1) for n in SMALL]), (0, RS * 128 - ns)).reshape(RS, 128)
    (small_parts,) = _run_rider("gather_vector_grads", _GatherRider([pks(S)]))
    small = _adamw("adamw_vectors", small_parts, pks(wts), pks(mom), pks(var), RS)
    for n, vals in zip(SMALL, zip(*[_unpack(b, small_shapes) for b in small])):
        res[n] = vals

    loss = lax.psum(loss, ("x", "y", "c"))
    outs = [loss, grad_x[None]]
    for k in range(4):
        for n in WEIGHTS:
            outs.append(res[n][k].reshape(wts[n].shape))
    return tuple(outs)
```

```python
import functools
import math

import jax
import jax.numpy as jnp
from jax import lax
from jax.experimental import pallas as pl
from jax.experimental.pallas import tpu as pltpu

F32 = jnp.float32
BF16 = jnp.bfloat16

N_DEV = 8
LN_EPS = 1e-5
ALPHA = 2.0 ** 0.25
RET_DK = 128
RET_DV = 256
RET_CHUNK = 128
ROPE_BASE = 10000.0
CONV_WIDTH = 31
HALO = 32
ADAM_LR, ADAM_B1, ADAM_B2, ADAM_EPS, ADAM_WD, ADAM_STEP = 0.001, 0.9, 0.999, 1e-08, 0.01, 10
VMEM_LIMIT = 52 * 1024 * 1024
MESH = pl.DeviceIdType.MESH
ANY = pl.BlockSpec(memory_space=pl.ANY)

BIG = ['ffn1_w_gate', 'ffn1_w_up', 'ffn1_w_down', 'w_in', 'w_ret_o', 'w_conv_o', 'w_out',
       'ffn2_w_gate', 'ffn2_w_up', 'ffn2_w_down', 'conv_k']
COL_SHARDED = {'ffn1_w_gate', 'ffn1_w_up', 'w_in', 'ffn2_w_gate', 'ffn2_w_up', 'conv_k'}
SMALL = ['ln1_g', 'ln1_b', 'b_in', 'ret_gn_g', 'conv_b', 'conv_ln_g', 'conv_ln_b', 'ln2_g', 'ln2_b', 'ln3_g', 'ln3_b']
WEIGHTS = ['ffn1_w_gate', 'ffn1_w_up', 'ffn1_w_down', 'ln1_g', 'ln1_b', 'w_in', 'b_in', 'ret_gn_g', 'conv_k', 'conv_b',
           'conv_ln_g', 'conv_ln_b', 'w_ret_o', 'w_conv_o', 'w_out', 'ln2_g', 'ln2_b', 'ffn2_w_gate', 'ffn2_w_up',
           'ffn2_w_down', 'ln3_g', 'ln3_b']


def _params(sem=None):
    return pltpu.CompilerParams(dimension_semantics=sem, vmem_limit_bytes=VMEM_LIMIT)


def _sigmoid(x):
    return jax.nn.sigmoid(x)


def _dsilu(x, sg):
    return sg * (1.0 + x * (1.0 - sg))


def _fit(dim, want):
    if dim <= want:
        return dim
    return max(t for t in range(128, want + 1, 128) if dim % t == 0)


def _dot(a, b, ta=False, tb=False):
    dn = (((0,) if ta else (1,), (1,) if tb else (0,)), ((), ()))
    return lax.dot_general(a, b, dn, preferred_element_type=F32)


def _mm(name, As, Bs, prods, epi, out_dtypes, *, ta=False, tb=False, tm, tn, tk, extras=(), i_outer=True,
        b3=False, o3=False, rider=None):
    a0, b0 = As[0], Bs[0]
    M, K = (a0.shape[1], a0.shape[0]) if ta else a0.shape
    if b3:
        S, rows, cs = b0.shape
        N = rows if tb else S * cs
        assert K == (S * cs if tb else rows)
        tn, tk = (tn, cs) if tb else (cs, tk)
    else:
        N = b0.shape[0] if tb else b0.shape[1]
    tm, tn, tk = _fit(M, tm), _fit(N, tn), _fit(K, tk)
    assert M % tm == 0 and N % tn == 0 and K % tk == 0, (name, M, N, K, tm, tn, tk)
    gi, gj, gk = M // tm, N // tn, K // tk
    grid = (gi, gj, gk) if i_outer else (gj, gi, gk)

    def ij(g0, g1):
        return (g0, g1) if i_outer else (g1, g0)

    def amap(g0, g1, k):
        i, _ = ij(g0, g1)
        return (k, i) if ta else (i, k)

    def bmap(g0, g1, k):
        _, j = ij(g0, g1)
        return (j, k) if tb else (k, j)

    def bmap3(g0, g1, k):
        _, j = ij(g0, g1)
        return (k, j, 0) if tb else (j, k, 0)

    in_specs = [pl.BlockSpec((tk, tm) if ta else (tm, tk), amap) for _ in As]
    if b3:
        in_specs += [pl.BlockSpec((None, tn, tk) if tb else (None, tk, tn), bmap3) for _ in Bs]
    else:
        in_specs += [pl.BlockSpec((tn, tk) if tb else (tk, tn), bmap) for _ in Bs]
    args = list(As) + list(Bs)
    for arr, kind, coloff in extras:
        assert coloff % tn == 0
        off = coloff // tn
        if kind == 'mn':
            in_specs.append(pl.BlockSpec((tm, tn), lambda g0, g1, k, off=off: (ij(g0, g1)[0], ij(g0, g1)[1] + off)))
        else:
            in_specs.append(pl.BlockSpec((1, tn), lambda g0, g1, k, off=off: (0, ij(g0, g1)[1] + off)))
        args.append(arr)
    if o3:
        out_shape = [jax.ShapeDtypeStruct((gj, M, tn), dt) for dt in out_dtypes]
        out_specs = [pl.BlockSpec((None, tm, tn), lambda g0, g1, k: (ij(g0, g1)[1], ij(g0, g1)[0], 0))
                     for _ in out_dtypes]
    else:
        out_shape = [jax.ShapeDtypeStruct((M, N), dt) for dt in out_dtypes]
        out_specs = [pl.BlockSpec((tm, tn), lambda g0, g1, k: ij(g0, g1)) for _ in out_dtypes]
    n_a, n_b, n_e, n_o = len(As), len(Bs), len(extras), len(out_dtypes)
    n_p = len(prods) if gk > 1 else 0
    scratch = [pltpu.VMEM((tm, tn), F32) for _ in range(n_p)]
    if rider is not None:
        in_specs, out_specs = in_specs + [ANY] * len(rider.ins), out_specs + [ANY] * len(rider.out_shape)
        args, out_shape, scratch = args + rider.ins, out_shape + rider.out_shape, scratch + rider.scratch
    n_in, n_out = len(args), len(out_shape)

    def body(*refs):
        a_refs = refs[:n_a]
        b_refs = refs[n_a:n_a + n_b]
        e_refs = refs[n_a + n_b:n_a + n_b + n_e]
        o_refs = refs[n_in:n_in + n_o]
        acc_refs = refs[n_in + n_out:n_in + n_out + n_p]
        k = pl.program_id(2)
        if rider is not None:
            step = (pl.program_id(0) * grid[1] + pl.program_id(1)) * gk + k
            ride = (step, grid[0] * grid[1] * gk, refs[n_a + n_b + n_e:n_in], refs[n_in + n_o:n_in + n_out],
                    refs[n_in + n_out + n_p:])
            rider.begin(*ride)

        def finish(accs):
            for o, r in zip(o_refs, epi(accs, *[e[...] for e in e_refs])):
                o[...] = r.astype(o.dtype)

        if gk == 1:
            finish([functools.reduce(jnp.add, [_dot(a_refs[ai][...], b_refs[bi][...], ta, tb) for ai, bi in terms])
                    for terms in prods])
        else:
            @pl.when(k == 0)
            def _():
                for acc in acc_refs:
                    acc[...] = jnp.zeros_like(acc)

            for p, terms in enumerate(prods):
                for ai, bi in terms:
                    acc_refs[p][...] += _dot(a_refs[ai][...], b_refs[bi][...], ta, tb)

            @pl.when(k == gk - 1)
            def _():
                finish([acc[...] for acc in acc_refs])

        if rider is not None:
            rider.end(*ride)

    aliases = {} if rider is None else {n_a + n_b + n_e + p: n_o + o for p, o in rider.aliases.items()}
    res = pl.pallas_call(
        body, name=name, grid=grid, in_specs=in_specs, out_specs=out_specs, out_shape=out_shape,
        scratch_shapes=scratch, input_output_aliases=aliases,
        compiler_params=_params(("arbitrary", "arbitrary", "arbitrary")),
    )(*args)
    if rider is not None:
        rider.results = res[n_o:]
    return res[:n_o]


def _rows(name, fn, ins, outs, *, T, tb):
    tb = min(tb, T)
    assert T % tb == 0
    in_specs, args = [], []
    for arr, kind, width, cb in ins:
        if kind == 'r':
            in_specs.append(pl.BlockSpec((tb, width), lambda i, cb=cb: (i, cb)))
        else:
            in_specs.append(pl.BlockSpec((1, width), lambda i, cb=cb: (0, cb)))
        args.append(arr)
    out_shape, out_specs = [], []
    for kind, width, dtype in outs:
        if kind == 'r':
            out_shape.append(jax.ShapeDtypeStruct((T, width), dtype))
            out_specs.append(pl.BlockSpec((tb, width), lambda i: (i, 0)))
        elif kind == 'c':
            out_shape.append(jax.ShapeDtypeStruct((T, 1), dtype))
            out_specs.append(pl.BlockSpec((tb, 1), lambda i: (i, 0)))
        else:
            out_shape.append(jax.ShapeDtypeStruct((1, width), F32))
            out_specs.append(pl.BlockSpec((1, width), lambda i: (0, 0)))
    n_in = len(ins)

    def body(*refs):
        i = pl.program_id(0)
        vals = fn(*[r[...] for r in refs[:n_in]])
        for (kind, _, _), o, v in zip(outs, refs[n_in:], vals):
            if kind == 'a':
                @pl.when(i == 0)
                def _(o=o):
                    o[...] = jnp.zeros_like(o)

                o[...] += v
            else:
                o[...] = v.astype(o.dtype)

    return pl.pallas_call(
        body, name=name, grid=(T // tb,), in_specs=in_specs, out_specs=out_specs, out_shape=out_shape,
        compiler_params=_params(("arbitrary",)),
    )(*args)


def _colsum(v):
    return jnp.sum(v, axis=0, keepdims=True)


def _ln_stats(z):
    mu = jnp.mean(z, axis=-1, keepdims=True)
    d = z - mu
    var = jnp.mean(d * d, axis=-1, keepdims=True)
    rstd = lax.rsqrt(var + LN_EPS)
    return d * rstd, rstd


def _ln_bwd_math(dy, xhat, rstd, g):
    dxh = dy * g
    m1 = jnp.mean(dxh, axis=-1, keepdims=True)
    m2 = jnp.mean(dxh * xhat, axis=-1, keepdims=True)
    return rstd * (dxh - m1 - xhat * m2)


def _ln_fwd(name, z, g, b, T, D):
    def fn(z, g, b):
        xhat, rstd = _ln_stats(z)
        y = xhat * g + b
        return [y, y, xhat, rstd]

    return _rows(name, fn, [(z, 'r', D, 0), (g, 'v', D, 0), (b, 'v', D, 0)],
                 [('r', D, F32), ('r', D, BF16), ('r', D, F32), ('c', 1, F32)], T=T, tb=512)


def _ln_bwd(name, dy, xhat, rstd, g, scale, T, D):
    def fn(dy, xhat, rstd, g):
        dz = _ln_bwd_math(dy, xhat, rstd, g)
        return [dz, dz * scale, _colsum(dy * xhat), _colsum(dy)]

    return _rows(name, fn, [(dy, 'r', D, 0), (xhat, 'r', D, 0), (rstd, 'r', 1, 0), (g, 'v', D, 0)],
                 [('r', D, F32), ('r', D, BF16), ('a', D, F32), ('a', D, F32)], T=T, tb=512)


def _ln_loss_bwd(name, z, g, b, tgt, T, D):
    def fn(z, g, b, tgt):
        xhat, rstd = _ln_stats(z)
        err = xhat * g + b - tgt
        row_loss = 0.5 * jnp.mean(err * err, axis=-1, keepdims=True)
        loss = jnp.broadcast_to(jnp.sum(row_loss, axis=0, keepdims=True), (1, 128))
        dy = err * (1.0 / D)
        dz = _ln_bwd_math(dy, xhat, rstd, g)
        return [dz, dz * 0.5, _colsum(dy * xhat), _colsum(dy), loss]

    return _rows(name, fn, [(z, 'r', D, 0), (g, 'v', D, 0), (b, 'v', D, 0), (tgt, 'r', D, 0)],
                 [('r', D, F32), ('r', D, BF16), ('a', D, F32), ('a', D, F32), ('a', 128, F32)], T=T, tb=512)


class _Net:
    def __init__(self, comm, G):
        self.comm, self.G = comm, G

    def gather(self, names, part=None):
        return self.comm.gather(names, part) if self.comm else None

    def exchange(self, names, part=None):
        return self.comm.exchange(names, self.G, part) if self.comm else None

    def done(self, rider):
        if rider is not None:
            self.comm.collect(rider)


def _ffn_fwd(tag, xb, x, W, names, net, rider=None, rider_down=None):
    def epi_gu(accs):
        a, b = accs
        return [a, b, a * _sigmoid(a) * b]

    ng, nu, nd = names
    a, b, s = _mm(tag + "_gate_up", [xb], [W[ng], W[nu]], [[(0, 0)], [(0, 1)]], epi_gu, [F32, F32, BF16],
                  tm=2048, tn=256, tk=1024, rider=rider)
    net.done(rider)

    def epi_down(accs, xres):
        return [ALPHA * xres + 0.5 * accs[0]]

    rider_down = rider_down() if rider_down else None
    (z,) = _mm(tag + "_down", [s], [W[nd]], [[(0, 0)]], epi_down, [F32], tm=1024, tn=1024, tk=1408,
               extras=[(x, 'mn', 0)], rider=rider_down)
    net.done(rider_down)
    return a, b, s, z


def _ffn_bwd(tag, dzh, dz, xb, a, b, s, W, names, gdt, G, net, ride, pre=(None, None)):
    ng, nu, nd = names

    def epi_ds(accs, a, b):
        ds = accs[0]
        sg = _sigmoid(a)
        return [ds * b * _dsilu(a, sg), ds * a * sg]

    rider = pre[0]() if pre[0] else None
    da, db = _mm(tag + "_ds", [dzh], [W[nd]], [[(0, 0)]], epi_ds, [BF16, BF16], tb=True, tm=2048, tn=256, tk=1024,
                 extras=[(a, 'mn', 0), (b, 'mn', 0)], rider=rider)
    net.done(rider)
    ident = lambda accs: accs
    rider = pre[1]() if pre[1] else None
    (G[nd],) = _mm(tag + "_dwd", [s], [dzh], [[(0, 0)]], ident, [gdt], ta=True, tm=1408, tn=1024, tk=1024,
                   rider=rider)
    net.done(rider)
    rider = net.exchange([nd]) if ride else None
    G[ng], G[nu] = _mm(tag + "_dwgu", [xb], [da, db], [[(0, 0)], [(0, 1)]], ident, [gdt, gdt], ta=True,
                       tm=1024, tn=1408, tk=1024, rider=rider)
    net.done(rider)

    def epi_dx(accs, dzres):
        return [ALPHA * dzres + accs[0]]

    rider = net.exchange([ng, nu]) if ride else None
    (dx,) = _mm(tag + "_dx", [da, db], [W[ng], W[nu]], [[(0, 0), (1, 1)]], epi_dx, [F32], tb=True,
                tm=1024, tn=1024, tk=1408, extras=[(dz, 'mn', 0)], rider=rider)
    net.done(rider)
    return dx


def _ret_tables(H, T):
    C = RET_CHUNK
    log_g = jnp.log(1.0 - jnp.exp2(-5.0 - jnp.arange(H, dtype=F32)))
    idx = jnp.arange(C, dtype=F32)
    diff = idx[:, None] - idx[None, :]
    dm = jnp.where(diff[None] >= 0, jnp.exp(jnp.maximum(diff, 0.0)[None] * log_g[:, None, None]), 0.0)
    xi = jnp.exp((idx[None, :] + 1.0) * log_g[:, None])[:, :, None]
    zeta = jnp.exp((C - 1.0 - idx)[None, :] * log_g[:, None])[:, :, None]
    gc = jnp.broadcast_to(jnp.exp(C * log_g)[:, None, None], (H, 1, RET_DV))
    half = RET_DK // 2
    freqs = ROPE_BASE ** (-jnp.arange(half, dtype=F32) / half)
    ang = jnp.arange(T, dtype=F32)[:, None] * freqs[None, :]
    cos, sin = jnp.cos(ang), jnp.sin(ang)
    cosf = jnp.concatenate([cos, cos], axis=1)
    sins = jnp.concatenate([-sin, sin], axis=1)
    return dm, xi, zeta, gc, cosf, sins


def _rot(x, cosf, sins):
    return x * cosf + pltpu.roll(x, RET_DK // 2, 1) * sins


def _rot_bwd(dy, cosf, sins):
    return dy * cosf + pltpu.roll(dy * sins, RET_DK // 2, 1)


RET_HB = 4


def _ret_specs(H, HB, rev, NC):
    C, G = RET_CHUNK, H // HB
    nn = (lambda n: NC - 1 - n) if rev else (lambda n: n)
    return [
        pl.BlockSpec((C, HB * RET_DK), lambda h, n: (nn(n), h)),
        pl.BlockSpec((C, HB * RET_DK), lambda h, n: (nn(n), G + h)),
        pl.BlockSpec((C, HB * RET_DV), lambda h, n: (nn(n), G + h)),
        pl.BlockSpec((C, HB * RET_DV), lambda h, n: (nn(n), 2 * G + h)),
        pl.BlockSpec((C, RET_DK), lambda h, n: (nn(n), 0)),
        pl.BlockSpec((C, RET_DK), lambda h, n: (nn(n), 0)),
        pl.BlockSpec((1, HB * RET_DV), lambda h, n: (0, h)),
        pl.BlockSpec((HB, C, C), lambda h, n: (h, 0, 0)),
        pl.BlockSpec((HB, C, 1), lambda h, n: (h, 0, 0)),
        pl.BlockSpec((HB, C, 1), lambda h, n: (h, 0, 0)),
        pl.BlockSpec((HB, 1, RET_DV), lambda h, n: (h, 0, 0)),
    ]


def _ret_fwd(proj, gn_g, tabs, H, T, rider=None):
    C, NC = RET_CHUNK, T // RET_CHUNK
    HB = min(RET_HB, H)
    dm, xi, zeta, gc, cosf, sins = tabs
    scale = RET_DK ** -0.5

    def body(q_ref, k_ref, v_ref, g_ref, cos_ref, sin_ref, gn_ref, dm_ref, xi_ref, zt_ref, gc_ref,
             r_ref, ri_ref, st_ref, state):
        @pl.when(pl.program_id(1) == 0)
        def _():
            state[...] = jnp.zeros_like(state)

        cs, sn = cos_ref[...], sin_ref[...]
        for hh in range(HB):
            qk = slice(hh * RET_DK, (hh + 1) * RET_DK)
            vv = slice(hh * RET_DV, (hh + 1) * RET_DV)
            qr = _rot(q_ref[:, qk], cs, sn) * scale
            kr = _rot(k_ref[:, qk], cs, sn)
            qb, kb, vb = qr.astype(BF16), kr.astype(BF16), v_ref[:, vv].astype(BF16)
            st = state[hh]
            stb = st.astype(BF16)
            s = _dot(qb, kb, tb=True) * dm_ref[hh]
            r = _dot(s.astype(BF16), vb) + _dot(qb, stb) * xi_ref[hh]
            st_ref[hh] = stb
            state[hh] = gc_ref[hh] * st + _dot((kr * zt_ref[hh]).astype(BF16), vb, ta=True)
            rhat, _ = _ln_stats(r)
            g = g_ref[:, vv]
            r_ref[:, vv] = r
            ri_ref[:, vv] = (g * _sigmoid(g) * (rhat * gn_ref[:, vv])).astype(BF16)

    VW = H * RET_DV
    return _hosted_call(
        body, rider, name="ret_fwd", grid=(H // HB, NC), in_specs=_ret_specs(H, HB, False, NC),
        out_specs=[pl.BlockSpec((C, HB * RET_DV), lambda h, n: (n, h)),
                   pl.BlockSpec((C, HB * RET_DV), lambda h, n: (n, h)),
                   pl.BlockSpec((HB, None, RET_DK, RET_DV), lambda h, n: (h, n, 0, 0))],
        out_shape=[jax.ShapeDtypeStruct((T, VW), F32), jax.ShapeDtypeStruct((T, VW), BF16),
                   jax.ShapeDtypeStruct((H, NC, RET_DK, RET_DV), BF16)],
        scratch=[pltpu.VMEM((HB, RET_DK, RET_DV), F32)],
        args=[proj, proj, proj, proj, cosf, sins, gn_g, dm, xi, zeta, gc])


def _hosted_call(body, rider, *, name, grid, in_specs, out_specs, out_shape, scratch, args):
    n_in, n_out, n_scr = len(args), len(out_shape), len(scratch)
    if rider is None:
        hosted = body
    else:
        n_ri, n_ro = len(rider.ins), len(rider.out_shape)
        in_specs, out_specs = in_specs + [ANY] * n_ri, out_specs + [ANY] * n_ro
        args, out_shape, scratch = args + rider.ins, out_shape + rider.out_shape, scratch + rider.scratch

        def hosted(*refs):
            o0, s0 = n_in + n_ri, n_in + n_ri + n_out + n_ro
            step = pl.program_id(0) * grid[1] + pl.program_id(1)
            ride = (step, grid[0] * grid[1], refs[n_in:o0], refs[o0 + n_out:s0], refs[s0 + n_scr:])
            rider.begin(*ride)
            body(*refs[:n_in], *refs[o0:o0 + n_out], *refs[s0:s0 + n_scr])
            rider.end(*ride)

    aliases = {} if rider is None else {n_in + p: n_out + o for p, o in rider.aliases.items()}
    res = pl.pallas_call(
        hosted, name=name, grid=grid, in_specs=in_specs, out_specs=out_specs, out_shape=out_shape,
        scratch_shapes=scratch, input_output_aliases=aliases, compiler_params=_params(("arbitrary", "arbitrary")),
    )(*args)
    if rider is not None:
        rider.results = res[n_out:]
    return res[:n_out]


def _ret_bwd(dri, r, states, proj, gn_g, tabs, H, T, rider=None):
    C, NC = RET_CHUNK, T // RET_CHUNK
    HB = min(RET_HB, H)
    dm, xi, zeta, gc, cosf, sins = tabs
    scale = RET_DK ** -0.5

    def body(q_ref, k_ref, v_ref, g_ref, cos_ref, sin_ref, gn_ref, dm_ref, xi_ref, zt_ref, gc_ref,
             dri_ref, r_ref, st_ref, dq_ref, dk_ref, dv_ref, dg_ref, dgn_ref, dstate):
        @pl.when(pl.program_id(1) == 0)
        def _():
            dstate[...] = jnp.zeros_like(dstate)
            dgn_ref[...] = jnp.zeros_like(dgn_ref)

        cs, sn = cos_ref[...], sin_ref[...]
        for hh in range(HB):
            qk = slice(hh * RET_DK, (hh + 1) * RET_DK)
            vv = slice(hh * RET_DV, (hh + 1) * RET_DV)
            qr = _rot(q_ref[:, qk], cs, sn) * scale
            kr = _rot(k_ref[:, qk], cs, sn)
            qb, kb, vb = qr.astype(BF16), kr.astype(BF16), v_ref[:, vv].astype(BF16)
            xi_c, zt_c, dmask = xi_ref[hh], zt_ref[hh], dm_ref[hh]
            rhat, rstd = _ln_stats(r_ref[:, vv])
            g, gn, dpre = g_ref[:, vv], gn_ref[:, vv], dri_ref[:, vv]
            sg = _sigmoid(g)
            dg_ref[:, vv] = (dpre * (rhat * gn) * _dsilu(g, sg)).astype(BF16)
            drn = dpre * (g * sg)
            dgn_ref[:, vv] += _colsum(drn * rhat)
            drb = _ln_bwd_math(drn, rhat, rstd, gn).astype(BF16)
            ds1 = dstate[hh]
            ds1b = ds1.astype(BF16)
            sb = (_dot(qb, kb, tb=True) * dmask).astype(BF16)
            kzb = (kr * zt_c).astype(BF16)
            dv_ref[:, vv] = (_dot(sb, drb, ta=True) + _dot(kzb, ds1b)).astype(BF16)
            dsb = (_dot(drb, vb, tb=True) * dmask).astype(BF16)
            dq = _dot(dsb, kb) + _dot(drb, st_ref[hh], tb=True) * xi_c
            dk = _dot(dsb, qb, ta=True) + _dot(vb, ds1b, tb=True) * zt_c
            dstate[hh] = gc_ref[hh] * ds1 + _dot((qr * xi_c).astype(BF16), drb, ta=True)
            dq_ref[:, qk] = _rot_bwd(dq * scale, cs, sn).astype(BF16)
            dk_ref[:, qk] = _rot_bwd(dk, cs, sn).astype(BF16)

    VW, QW = H * RET_DV, H * RET_DK
    rv = lambda n: NC - 1 - n
    in_specs = _ret_specs(H, HB, True, NC) + [
        pl.BlockSpec((C, HB * RET_DV), lambda h, n: (rv(n), h)),
        pl.BlockSpec((C, HB * RET_DV), lambda h, n: (rv(n), h)),
        pl.BlockSpec((HB, None, RET_DK, RET_DV), lambda h, n: (h, rv(n), 0, 0)),
    ]
    return _hosted_call(
        body, rider, name="ret_bwd", grid=(H // HB, NC), in_specs=in_specs,
        out_specs=[pl.BlockSpec((C, HB * RET_DK), lambda h, n: (rv(n), h)),
                   pl.BlockSpec((C, HB * RET_DK), lambda h, n: (rv(n), h)),
                   pl.BlockSpec((C, HB * RET_DV), lambda h, n: (rv(n), h)),
                   pl.BlockSpec((C, HB * RET_DV), lambda h, n: (rv(n), h)),
                   pl.BlockSpec((1, HB * RET_DV), lambda h, n: (0, h))],
        out_shape=[jax.ShapeDtypeStruct((T, QW), BF16), jax.ShapeDtypeStruct((T, QW), BF16),
                   jax.ShapeDtypeStruct((T, VW), BF16), jax.ShapeDtypeStruct((T, VW), BF16),
                   jax.ShapeDtypeStruct((1, VW), F32)],
        scratch=[pltpu.VMEM((HB, RET_DK, RET_DV), F32)],
        args=[proj, proj, proj, proj, cosf, sins, gn_g, dm, xi, zeta, gc, dri, r, states])


CONV_CW = 128
CONV_TB = 512


def _conv_fwd(proj, kpad, bias, off_a, CC, T):
    tb, cw = min(CONV_TB, T), CONV_CW
    hb = tb // HALO
    ca, cb = off_a // cw, (off_a + CC) // cw

    def body(a_ref, b_ref, ap_ref, bp_ref, k_ref, bias_ref, u1_ref, win):
        i = pl.program_id(0)
        keep = (i > 0).astype(F32)
        win[0:HALO, :] = ap_ref[...] * _sigmoid(bp_ref[...]) * keep
        win[HALO:, :] = a_ref[...] * _sigmoid(b_ref[...])
        acc = jnp.broadcast_to(bias_ref[...], (tb, cw))
        for w in range(CONV_WIDTH):
            acc = acc + k_ref[w:w + 1, :] * win[pl.ds(HALO - (CONV_WIDTH - 1) + w, tb), :]
        u1_ref[...] = acc

    prev = lambda i: jnp.maximum(i * hb - 1, 0)
    return pl.pallas_call(
        body, name="conv_fwd", grid=(T // tb, CC // cw),
        in_specs=[pl.BlockSpec((tb, cw), lambda i, c: (i, ca + c)),
                  pl.BlockSpec((tb, cw), lambda i, c: (i, cb + c)),
                  pl.BlockSpec((HALO, cw), lambda i, c: (prev(i), ca + c)),
                  pl.BlockSpec((HALO, cw), lambda i, c: (prev(i), cb + c)),
                  pl.BlockSpec((HALO, cw), lambda i, c: (0, c)),
                  pl.BlockSpec((1, cw), lambda i, c: (0, c))],
        out_specs=pl.BlockSpec((tb, cw), lambda i, c: (i, c)),
        out_shape=jax.ShapeDtypeStruct((T, CC), F32),
        scratch_shapes=[pltpu.VMEM((tb + HALO, cw), F32)],
        compiler_params=_params(("arbitrary", "arbitrary")),
    )(proj, proj, proj, proj, kpad, bias)


def _conv_bwd(du1, proj, kpad, off_a, CC, T, rider=None):
    tb, cw = min(CONV_TB, T), CONV_CW
    hb = tb // HALO
    nt = T // tb
    ca, cb = off_a // cw, (off_a + CC) // cw

    def body(d_ref, dn_ref, a_ref, b_ref, ap_ref, bp_ref, k_ref, da_ref, db_ref, dk_ref, winu, wind):
        i = pl.program_id(1)
        a, b = a_ref[...], b_ref[...]
        sgb = _sigmoid(b)
        winu[0:HALO, :] = ap_ref[...] * _sigmoid(bp_ref[...]) * (i > 0).astype(F32)
        winu[HALO:, :] = a * sgb
        d = d_ref[...]
        wind[0:tb, :] = d
        wind[tb:, :] = dn_ref[...] * (i < nt - 1).astype(F32)

        @pl.when(i == 0)
        def _():
            dk_ref[...] = jnp.zeros_like(dk_ref)

        du0 = jnp.zeros((tb, cw), F32)
        for w in range(CONV_WIDTH):
            du0 = du0 + k_ref[w:w + 1, :] * wind[pl.ds(CONV_WIDTH - 1 - w, tb), :]
            dk_ref[w:w + 1, :] += _colsum(winu[pl.ds(HALO - (CONV_WIDTH - 1) + w, tb), :] * d)
        da_ref[...] = (du0 * sgb).astype(BF16)
        db_ref[...] = (du0 * a * sgb * (1.0 - sgb)).astype(BF16)

    prev = lambda i: jnp.maximum(i * hb - 1, 0)
    nxt = lambda i: jnp.minimum((i + 1) * hb, T // HALO - 1)
    return _hosted_call(
        body, rider, name="conv_bwd", grid=(CC // cw, nt),
        in_specs=[pl.BlockSpec((tb, cw), lambda c, i: (i, c)),
                  pl.BlockSpec((HALO, cw), lambda c, i: (nxt(i), c)),
                  pl.BlockSpec((tb, cw), lambda c, i: (i, ca + c)),
                  pl.BlockSpec((tb, cw), lambda c, i: (i, cb + c)),
                  pl.BlockSpec((HALO, cw), lambda c, i: (prev(i), ca + c)),
                  pl.BlockSpec((HALO, cw), lambda c, i: (prev(i), cb + c)),
                  pl.BlockSpec((HALO, cw), lambda c, i: (0, c))],
        out_specs=[pl.BlockSpec((tb, cw), lambda c, i: (i, c)),
                   pl.BlockSpec((tb, cw), lambda c, i: (i, c)),
                   pl.BlockSpec((HALO, cw), lambda c, i: (0, c))],
        out_shape=[jax.ShapeDtypeStruct((T, CC), BF16), jax.ShapeDtypeStruct((T, CC), BF16),
                   jax.ShapeDtypeStruct((HALO, CC), F32)],
        scratch=[pltpu.VMEM((tb + HALO, cw), F32), pltpu.VMEM((tb + HALO, cw), F32)],
        args=[du1, du1, proj, proj, proj, proj, kpad])


FFN1 = ('ffn1_w_gate', 'ffn1_w_up', 'ffn1_w_down')
FFN2 = ('ffn2_w_gate', 'ffn2_w_up', 'ffn2_w_down')


def _local_step(x, tgt, W, P, gdt=BF16, comm=None):
    T, D = x.shape
    G = {}
    net = _Net(comm, G)
    if comm is not None:
        W = comm.W
        first = net.gather(['ffn1_w_gate', 'ffn1_w_up'])
        _run_rider("gather_ffn1_in", first)
        net.done(first)
    VW = P['ret_gn_g'].shape[1]
    H = VW // RET_DV
    QW = H * RET_DK
    CC = P['conv_b'].shape[1]
    off_glu = 2 * QW + 2 * VW
    off_gate = off_glu + 2 * CC
    ident = lambda accs: accs
    xb = x.astype(BF16)

    a1, b1, s1, z1 = _ffn_fwd("ffn1", xb, x, W, FFN1, net,
                              rider=net.gather(['ffn1_w_down', 'w_in'], {'w_in': (0, D // 2, False)}),
                              rider_down=lambda: net.gather(['w_in'], {'w_in': (D // 2, D // 2, True)}))
    x1, x1b, xh1, rs1 = _ln_fwd("ln1", z1, P['ln1_g'], P['ln1_b'], T, D)

    rest = net.gather(['conv_k', 'w_ret_o', 'w_conv_o', 'w_out'])
    (proj,) = _mm("w_in", [x1b], [W['w_in']], [[(0, 0)]], lambda accs, bias: [accs[0] + bias], [F32],
                  tm=2048, tn=0, tk=1024, extras=[(P['b_in'], 'n', 0)], i_outer=False, b3=True, rider=rest)
    net.done(rest)
    tabs = _ret_tables(H, T)
    rider = net.gather(list(FFN2))
    r, ret_in, states = _ret_fwd(proj, P['ret_gn_g'], tabs, H, T, rider=rider)
    net.done(rider)
    kpad = jnp.pad(W['conv_k'].astype(F32), ((0, HALO - CONV_WIDTH), (0, 0)))
    u1 = _conv_fwd(proj, kpad, P['conv_b'], off_glu, CC, T)

    def conv_ln(u1, g, b):
        xhat, rstd = _ln_stats(u1)
        u2 = xhat * g + b
        return [xhat, rstd, u2 * _sigmoid(u2)]

    xhc, rsc, u3 = _rows("conv_ln", conv_ln, [(u1, 'r', CC, 0), (P['conv_ln_g'], 'v', CC, 0), (P['conv_ln_b'], 'v', CC, 0)],
                         [('r', CC, F32), ('c', 1, F32), ('r', CC, BF16)], T=T, tb=512)
    (ret_out,) = _mm("ret_o", [ret_in], [W['w_ret_o']], [[(0, 0)]], ident, [F32], tm=1024, tn=1024, tk=2048)

    def epi_merge(accs, ret_out, gr, gc):
        conv_out = accs[0]
        return [conv_out, _sigmoid(gr) * ret_out + _sigmoid(gc) * conv_out]

    conv_out, merged = _mm("conv_o_merge", [u3], [W['w_conv_o']], [[(0, 0)]], epi_merge, [F32, BF16],
                           tm=512, tn=D, tk=1024,
                           extras=[(ret_out, 'mn', 0), (proj, 'mn', off_gate), (proj, 'mn', off_gate + D)])
    (z2,) = _mm("w_out", [merged], [W['w_out']], [[(0, 0)]], lambda accs, xr: [ALPHA * xr + accs[0]], [F32],
                tm=1024, tn=1024, tk=1024, extras=[(x1, 'mn', 0)])
    x2, x2b, xh2, rs2 = _ln_fwd("ln2", z2, P['ln2_g'], P['ln2_b'], T, D)
    a2, b2, s2, z3 = _ffn_fwd("ffn2", x2b, x2, W, FFN2, net)
    dz3, dz3h, g_ln3_g, g_ln3_b, loss = _ln_loss_bwd("ln3_loss", z3, P['ln3_g'], P['ln3_b'], tgt, T, D)

    S = {'ln3_g': g_ln3_g, 'ln3_b': g_ln3_b}
    dy2 = _ffn_bwd("ffn2b", dz3h, dz3, x2b, a2, b2, s2, W, FFN2, gdt, G, net, False)
    dz2, dz2b, S['ln2_g'], S['ln2_b'] = _ln_bwd("ln2b", dy2, xh2, rs2, P['ln2_g'], 1.0, T, D)

    (G['w_out'],) = _mm("d_w_out", [merged], [dz2b], [[(0, 0)]], ident, [gdt], ta=True, tm=1024, tn=1024, tk=1024)

    def epi_dmerge(accs, ret_out, conv_out, gr, gc):
        dm_ = accs[0]
        sr, sc = _sigmoid(gr), _sigmoid(gc)
        return [dm_ * sr, dm_ * sc, dm_ * ret_out * sr * (1.0 - sr), dm_ * conv_out * sc * (1.0 - sc)]

    dret_out, dconv_out, dgate_r, dgate_c = _mm(
        "d_merge", [dz2b], [W['w_out']], [[(0, 0)]], epi_dmerge, [BF16, BF16, BF16, BF16], tb=True,
        tm=512, tn=D, tk=1024,
        extras=[(ret_out, 'mn', 0), (conv_out, 'mn', 0), (proj, 'mn', off_gate), (proj, 'mn', off_gate + D)])
    (G['w_ret_o'],) = _mm("d_w_ret_o", [ret_in], [dret_out], [[(0, 0)]], ident, [gdt], ta=True, tm=1024, tn=1024, tk=1024)
    (G['w_conv_o'],) = _mm("d_w_conv_o", [u3], [dconv_out], [[(0, 0)]], ident, [gdt], ta=True, tm=1024, tn=1024, tk=1024)
    (dri,) = _mm("d_ret_in", [dret_out], [W['w_ret_o']], [[(0, 0)]], ident, [F32], tb=True, tm=1024, tn=1024, tk=1024)
    rider = net.exchange(list(FFN2))
    dq, dk, dv, dg, S['ret_gn_g'] = _ret_bwd(dri, r, states, proj, P['ret_gn_g'], tabs, H, T, rider=rider)
    net.done(rider)

    def epi_du2(accs, xhat, g, b):
        u2 = xhat * g + b
        return [accs[0] * _dsilu(u2, _sigmoid(u2))]

    (du2,) = _mm("d_u3", [dconv_out], [W['w_conv_o']], [[(0, 0)]], epi_du2, [F32], tb=True, tm=512, tn=CC, tk=1024,
                 extras=[(xhc, 'mn', 0), (P['conv_ln_g'], 'n', 0), (P['conv_ln_b'], 'n', 0)])

    def conv_ln_bwd(du2, xhat, rstd, g):
        du1 = _ln_bwd_math(du2, xhat, rstd, g)
        return [du1, _colsum(du2 * xhat), _colsum(du2), _colsum(du1)]

    du1, S['conv_ln_g'], S['conv_ln_b'], S['conv_b'] = _rows(
        "conv_ln_bwd", conv_ln_bwd, [(du2, 'r', CC, 0), (xhc, 'r', CC, 0), (rsc, 'r', 1, 0), (P['conv_ln_g'], 'v', CC, 0)],
        [('r', CC, F32), ('a', CC, F32), ('a', CC, F32), ('a', CC, F32)], T=T, tb=512)
    rider = net.exchange(['w_out', 'w_ret_o', 'w_conv_o'])
    dglu_a, dglu_b, dkpad = _conv_bwd(du1, proj, kpad, off_glu, CC, T, rider=rider)
    net.done(rider)
    G['conv_k'] = dkpad[:CONV_WIDTH].astype(gdt)

    dproj = jnp.concatenate([dq, dk, dv, dg, dglu_a, dglu_b, dgate_r, dgate_c], axis=1)
    IN_W = dproj.shape[1]
    (S['b_in'],) = _rows("d_b_in", lambda d: [_colsum(d.astype(F32))], [(dproj, 'r', IN_W, 0)], [('a', IN_W, F32)],
                         T=T, tb=256)
    (G['w_in'],) = _mm("d_w_in", [x1b], [dproj], [[(0, 0)]], ident, [gdt], ta=True, o3=True,
                       tm=1024, tn=W['w_in'].shape[2], tk=1024)
    cuts = [0, (9 * D) // 16, (53 * D) // 64, D]
    w_in_rows = [{'w_in': (cuts[i], cuts[i + 1] - cuts[i], i == 2)} for i in range(3)]
    rider = net.exchange(['w_in', 'conv_k'], w_in_rows[0])
    (dy1,) = _mm("d_x1", [dproj], [W['w_in']], [[(0, 0)]], lambda accs, dzr: [ALPHA * dzr + accs[0]], [F32], tb=True,
                 b3=True, tm=1024, tn=1024, tk=0, extras=[(dz2, 'mn', 0)], rider=rider)
    net.done(rider)
    dz1, dz1h, S['ln1_g'], S['ln1_b'] = _ln_bwd("ln1b", dy1, xh1, rs1, P['ln1_g'], 0.5, T, D)
    grad_x = _ffn_bwd("ffn1b", dz1h, dz1, xb, a1, b1, s1, W, FFN1, gdt, G, net, True,
                      pre=(lambda: net.exchange(['w_in'], w_in_rows[1]), lambda: net.exchange(['w_in'], w_in_rows[2])))
    return loss[0, 0], grad_x, G, S


def _coords():
    return lax.axis_index("x"), lax.axis_index("y"), lax.axis_index("c")


def _flip(k, x, y, c):
    return (1 - x if k & 4 else x, 1 - y if k & 2 else y, 1 - c if k & 1 else c)


def _lin(p):
    return 4 * p[0] + 2 * p[1] + p[2]


class _Rider:
    def __init__(self, ins, out_shape, rows=None, fill=None):
        nb = len(ins)
        self.rows = rows or [None] * nb
        fill = fill or [None] * nb
        self.aliases = {nb + i: w for i, w in enumerate(w for w in range(nb) if fill[w] is not None)}
        self.ins = list(ins) + [f for f in fill if f is not None]
        self.out_shape, self.results = list(out_shape), None
        self.scratch = [pltpu.SemaphoreType.DMA((7 * nb,)), pltpu.SemaphoreType.DMA((7 * nb,)),
                        pltpu.SemaphoreType.DMA((nb,))]

    def span(self, w, ref, *slot):
        if self.rows[w] is None:
            return ref.at[slot] if slot else ref
        return ref.at[(*slot, pl.ds(*self.rows[w]))]

    def begin(self, step, n_steps, ins, outs, sems):
        @pl.when(step == 0)
        def _():
            self.start(ins, outs, sems)

        @pl.when(step == n_steps - 1)
        def _():
            self.mid(ins, outs, sems)

    def end(self, step, n_steps, ins, outs, sems):
        @pl.when(step == n_steps - 1)
        def _():
            self.finish(ins, outs, sems)

    def mid(self, ins, outs, sems):
        pass


class _GatherRider(_Rider):
    def __init__(self, blks, rows=None, fill=None):
        super().__init__(blks, [jax.ShapeDtypeStruct((N_DEV,) + b.shape, b.dtype) for b in blks], rows, fill)

    def _copies(self, x_refs, out_refs, sems):
        nb = len(self.out_shape)
        send_sems, recv_sems, local_sems = sems
        x, y, c = _coords()
        me, sibling = (x, y, c), (x, y, 1 - c)
        chips = [_flip(4, x, y, c), _flip(2, x, y, c), _flip(6, x, y, c)]
        own = [self.span(w, x_refs[w]) for w in range(nb)]

        def copy(k, w, block, to, src=None):
            slot = self.span(w, out_refs[w], _lin(block))
            return pltpu.make_async_remote_copy(
                src_ref=slot if src is None else src, dst_ref=slot, send_sem=send_sems.at[k * nb + w],
                recv_sem=recv_sems.at[k * nb + w], device_id=to, device_id_type=MESH)

        mines = [pltpu.make_async_copy(own[w], self.span(w, out_refs[w], _lin(me)), local_sems.at[w])
                 for w in range(nb)]
        first = [copy(0, w, me, sibling, src=own[w]) for w in range(nb)]
        first += [copy(1 + j, w, me, chip, src=own[w]) for w in range(nb) for j, chip in enumerate(chips)]
        landed = [(copy(1 + j, w, chip, me), copy(4 + j, w, chip, sibling)) for w in range(nb) for j, chip in enumerate(chips)]
        from_sibling = [copy(0, w, sibling, me) for w in range(nb)]
        from_sibling += [copy(4 + j, w, (chip[0], chip[1], 1 - c), me) for w in range(nb) for j, chip in enumerate(chips)]
        return mines, first, landed, from_sibling

    def start(self, ins, outs, sems):
        mines, first, _, _ = self._copies(ins, outs, sems)
        for cp in mines + first:
            cp.start()

    def mid(self, ins, outs, sems):
        for arrival, onward in self._copies(ins, outs, sems)[2]:
            arrival.wait_recv()
            onward.start()

    def finish(self, ins, outs, sems):
        mines, first, landed, from_sibling = self._copies(ins, outs, sems)
        for cp in from_sibling:
            cp.wait_recv()
        for cp in first + [onward for _, onward in landed]:
            cp.wait_send()
        for mine in mines:
            mine.wait()


class _ExchangeRider(_Rider):
    def __init__(self, gs, rows=None, fill=None):
        super().__init__(gs, [jax.ShapeDtypeStruct(g.shape, g.dtype) for g in gs], rows, fill)

    def _copies(self, g_refs, out_refs, sems):
        nb = len(self.out_shape)
        send_sems, recv_sems, local_sems = sems
        x, y, c = _coords()
        me = _lin((x, y, c))

        def copy(k, w, landing):
            peer = _flip(k, x, y, c)
            src, dst = (me, _lin(peer)) if landing else (_lin(peer), me)
            return pltpu.make_async_remote_copy(
                src_ref=self.span(w, g_refs[w], src), dst_ref=self.span(w, out_refs[w], dst),
                send_sem=send_sems.at[(k - 1) * nb + w], recv_sem=recv_sems.at[(k - 1) * nb + w],
                device_id=peer, device_id_type=MESH)

        mines = [pltpu.make_async_copy(self.span(w, g_refs[w], me), self.span(w, out_refs[w], me), local_sems.at[w])
                 for w in range(nb)]
        sends = [copy(k, w, False) for w in range(nb) for k in range(1, N_DEV)]
        landings = [copy(k, w, True) for w in range(nb) for k in range(1, N_DEV)]
        return mines, sends, landings

    def start(self, ins, outs, sems):
        mines, sends, _ = self._copies(ins, outs, sems)
        for cp in mines + sends:
            cp.start()

    def finish(self, ins, outs, sems):
        mines, sends, landings = self._copies(ins, outs, sems)
        for cp in landings:
            cp.wait_recv()
        for cp in sends:
            cp.wait_send()
        for mine in mines:
            mine.wait()


def _run_rider(name, rider):
    n_in, n_out = len(rider.ins), len(rider.out_shape)

    def body(*refs):
        ride = (refs[:n_in], refs[n_in:n_in + n_out], refs[n_in + n_out:])
        rider.start(*ride)
        rider.mid(*ride)
        rider.finish(*ride)

    rider.results = pl.pallas_call(
        body, name=name, out_shape=rider.out_shape, in_specs=[ANY] * n_in, out_specs=[ANY] * n_out,
        scratch_shapes=rider.scratch, input_output_aliases=dict(rider.aliases),
        compiler_params=pltpu.CompilerParams(has_side_effects=True),
    )(*rider.ins)
    return rider.results


def _as_matrix(name, g):
    if name == 'w_in':
        return g
    if name in COL_SHARDED:
        return jnp.transpose(g, (1, 0, 2)).reshape(g.shape[1], N_DEV * g.shape[2])
    return g.reshape(N_DEV * g.shape[1], g.shape[2])


def _by_owner(name, g):
    if name == 'w_in':
        return g
    if name in COL_SHARDED:
        return jnp.transpose(g.reshape(g.shape[0], N_DEV, g.shape[1] // N_DEV), (1, 0, 2))
    return g.reshape(N_DEV, g.shape[0] // N_DEV, g.shape[1])


class _Comm:
    def __init__(self, shards):
        self.shards, self.W, self.parts, self.partial, self.sent = shards, {}, {}, {}, {}

    def _ride(self, cls, names, srcs, part, sink):
        part = part or {}
        rider = cls(srcs, rows=[part[n][:2] if n in part else None for n in names],
                    fill=[self.partial.pop((sink, n), None) for n in names])
        rider.names, rider.sink = names, sink
        rider.unfinished = {n for n in names if n in part and not part[n][2]}
        return rider

    def gather(self, names, part=None):
        return self._ride(_GatherRider, names, [self.shards[n] for n in names], part, 'W')

    def exchange(self, names, G, part=None):
        for n in names:
            if n not in self.sent:
                self.sent[n] = _by_owner(n, G[n])
        return self._ride(_ExchangeRider, names, [self.sent[n] for n in names], part, 'parts')

    def collect(self, rider):
        for n, res in zip(rider.names, rider.results):
            if n in rider.unfinished:
                self.partial[(rider.sink, n)] = res
            elif rider.sink == 'W':
                self.W[n] = _as_matrix(n, res)
            else:
                self.parts[n] = res


def _adamw(name, parts, w, m, v, tb):
    n, R, Wd = parts.shape
    assert R % tb == 0
    c1 = 1.0 - ADAM_B1 ** ADAM_STEP
    c2 = 1.0 - ADAM_B2 ** ADAM_STEP

    def body(p_ref, w_ref, m_ref, v_ref, g_ref, d_ref, nm_ref, nv_ref):
        g = p_ref[0].astype(F32)
        for s in range(1, n):
            g = g + p_ref[s].astype(F32)
        nm = ADAM_B1 * m_ref[...] + (1.0 - ADAM_B1) * g
        nv = ADAM_B2 * v_ref[...] + (1.0 - ADAM_B2) * (g * g)
        g_ref[...] = g
        nm_ref[...] = nm
        nv_ref[...] = nv
        d_ref[...] = -ADAM_LR * ((nm / c1) / (jnp.sqrt(nv / c2) + ADAM_EPS) + ADAM_WD * w_ref[...])

    row = pl.BlockSpec((tb, Wd), lambda i: (i, 0))
    return pl.pallas_call(
        body, name=name, grid=(R // tb,),
        in_specs=[pl.BlockSpec((n, tb, Wd), lambda i: (0, i, 0)), row, row, row],
        out_specs=[row, row, row, row], out_shape=[jax.ShapeDtypeStruct((R, Wd), F32)] * 4,
        compiler_params=_params(("arbitrary",)),
    )(parts, w, m, v)


def _unpack(buf, shapes):
    flat, out, off = buf.reshape(-1), [], 0
    for shp in shapes:
        n = shp[0] * shp[1]
        out.append(flat[off:off + n].reshape(shp))
        off += n
    return out


def _row_tile(R, unit, cap):
    best = unit
    for t in range(unit, cap + 1, unit):
        if R % t == 0:
            best = t
    return best


def kernel(x, ffn1_w_gate, ffn1_w_up, ffn1_w_down, ln1_g, ln1_b, w_in, b_in, ret_gn_g, conv_k, conv_b, conv_ln_g, conv_ln_b, w_ret_o, w_conv_o, w_out, ln2_g, ln2_b, ffn2_w_gate, ffn2_w_up, ffn2_w_down, ln3_g, ln3_b, loss_target, m_ffn1_w_gate, m_ffn1_w_up, m_ffn1_w_down, m_ln1_g, m_ln1_b, m_w_in, m_b_in, m_ret_gn_g, m_conv_k, m_conv_b, m_conv_ln_g, m_conv_ln_b, m_w_ret_o, m_w_conv_o, m_w_out, m_ln2_g, m_ln2_b, m_ffn2_w_gate, m_ffn2_w_up, m_ffn2_w_down, m_ln3_g, m_ln3_b, v_ffn1_w_gate, v_ffn1_w_up, v_ffn1_w_down, v_ln1_g, v_ln1_b, v_w_in, v_b_in, v_ret_gn_g, v_conv_k, v_conv_b, v_conv_ln_g, v_conv_ln_b, v_w_ret_o, v_w_conv_o, v_w_out, v_ln2_g, v_ln2_b, v_ffn2_w_gate, v_ffn2_w_up, v_ffn2_w_down, v_ln3_g, v_ln3_b):
    given = dict(locals())
    wts = {n: given[n] for n in WEIGHTS}
    mom = {n: given['m_' + n] for n in WEIGHTS}
    var = {n: given['v_' + n] for n in WEIGHTS}

    def shard2d(a):
        return a.reshape(a.shape[-3] * a.shape[-2] if a.ndim == 4 else a.shape[-2], a.shape[-1])

    comm = _Comm({n: shard2d(wts[n]).astype(BF16) for n in BIG})
    P = {n: wts[n].reshape(1, -1) for n in SMALL}
    loss, grad_x, _, S = _local_step(x[0], loss_target[0], None, P, comm=comm)

    parts = comm.parts
    res = {}
    for n in BIG:
        rows, cols = parts[n].shape[1:]
        tb = rows if rows % 16 else _row_tile(rows, 16, max(16, (256 * 1024) // cols))
        res[n] = _adamw("adamw_" + n, parts[n], shard2d(wts[n]), shard2d(mom[n]), shard2d(var[n]), tb)

    small_shapes = [(1, wts[n].shape[-1]) for n in SMALL]
    ns = sum(s[1] for s in small_shapes)
    RS = -(-ns // (128 * 8)) * 8
    pks = lambda d: jnp.pad(jnp.concatenate([d[n].reshape(-1) for n in SMALL]), (0, RS * 128 - ns)).reshape(RS, 128)
    (small_parts,) = _run_rider("gather_vector_grads", _GatherRider([pks(S)]))
    small = _adamw("adamw_vectors", small_parts, pks(wts), pks(mom), pks(var), RS)
    for n, vals in zip(SMALL, zip(*[_unpack(b, small_shapes) for b in small])):
        res[n] = vals

    loss = lax.psum(loss, ("x", "y", "c"))
    outs = [loss, grad_x[None]]
    for k in range(4):
        for n in WEIGHTS:
            outs.append(res[n][k].reshape(wts[n].shape))
    return tuple(outs)
```

```python
import functools
import math

import jax
import jax.numpy as jnp
from jax import lax
from jax.experimental import pallas as pl
from jax.experimental.pallas import tpu as pltpu

F32 = jnp.float32
BF16 = jnp.bfloat16

N_DEV = 8
LN_EPS = 1e-5
ALPHA = 2.0 ** 0.25
RET_DK = 128
RET_DV = 256
RET_CHUNK = 128
ROPE_BASE = 10000.0
CONV_WIDTH = 31
HALO = 32
ADAM_LR, ADAM_B1, ADAM_B2, ADAM_EPS, ADAM_WD, ADAM_STEP = 0.001, 0.9, 0.999, 1e-08, 0.01, 10
VMEM_LIMIT = 52 * 1024 * 1024
MESH = pl.DeviceIdType.MESH
ANY = pl.BlockSpec(memory_space=pl.ANY)

BIG = ['ffn1_w_gate', 'ffn1_w_up', 'ffn1_w_down', 'w_in', 'w_ret_o', 'w_conv_o', 'w_out',
       'ffn2_w_gate', 'ffn2_w_up', 'ffn2_w_down', 'conv_k']
COL_SHARDED = {'ffn1_w_gate', 'ffn1_w_up', 'w_in', 'ffn2_w_gate', 'ffn2_w_up', 'conv_k'}
SMALL = ['ln1_g', 'ln1_b', 'b_in', 'ret_gn_g', 'conv_b', 'conv_ln_g', 'conv_ln_b', 'ln2_g', 'ln2_b', 'ln3_g', 'ln3_b']
WEIGHTS = ['ffn1_w_gate', 'ffn1_w_up', 'ffn1_w_down', 'ln1_g', 'ln1_b', 'w_in', 'b_in', 'ret_gn_g', 'conv_k', 'conv_b',
           'conv_ln_g', 'conv_ln_b', 'w_ret_o', 'w_conv_o', 'w_out', 'ln2_g', 'ln2_b', 'ffn2_w_gate', 'ffn2_w_up',
           'ffn2_w_down', 'ln3_g', 'ln3_b']


def _params(sem=None):
    return pltpu.CompilerParams(dimension_semantics=sem, vmem_limit_bytes=VMEM_LIMIT)


def _sigmoid(x):
    return jax.nn.sigmoid(x)


def _dsilu(x, sg):
    return sg * (1.0 + x * (1.0 - sg))


def _fit(dim, want):
    if dim <= want:
        return dim
    return max(t for t in range(128, want + 1, 128) if dim % t == 0)


def _dot(a, b, ta=False, tb=False):
    dn = (((0,) if ta else (1,), (1,) if tb else (0,)), ((), ()))
    return lax.dot_general(a, b, dn, preferred_element_type=F32)


def _mm(name, As, Bs, prods, epi, out_dtypes, *, ta=False, tb=False, tm, tn, tk, extras=(), i_outer=True,
        b3=False, o3=False, rider=None):
    a0, b0 = As[0], Bs[0]
    M, K = (a0.shape[1], a0.shape[0]) if ta else a0.shape
    if b3:
        S, rows, cs = b0.shape
        N = rows if tb else S * cs
        assert K == (S * cs if tb else rows)
        tn, tk = (tn, cs) if tb else (cs, tk)
    else:
        N = b0.shape[0] if tb else b0.shape[1]
    tm, tn, tk = _fit(M, tm), _fit(N, tn), _fit(K, tk)
    assert M % tm == 0 and N % tn == 0 and K % tk == 0, (name, M, N, K, tm, tn, tk)
    gi, gj, gk = M // tm, N // tn, K // tk
    grid = (gi, gj, gk) if i_outer else (gj, gi, gk)

    def ij(g0, g1):
        return (g0, g1) if i_outer else (g1, g0)

    def amap(g0, g1, k):
        i, _ = ij(g0, g1)
        return (k, i) if ta else (i, k)

    def bmap(g0, g1, k):
        _, j = ij(g0, g1)
        return (j, k) if tb else (k, j)

    def bmap3(g0, g1, k):
        _, j = ij(g0, g1)
        return (k, j, 0) if tb else (j, k, 0)

    in_specs = [pl.BlockSpec((tk, tm) if ta else (tm, tk), amap) for _ in As]
    if b3:
        in_specs += [pl.BlockSpec((None, tn, tk) if tb else (None, tk, tn), bmap3) for _ in Bs]
    else:
        in_specs += [pl.BlockSpec((tn, tk) if tb else (tk, tn), bmap) for _ in Bs]
    args = list(As) + list(Bs)
    for arr, kind, coloff in extras:
        assert coloff % tn == 0
        off = coloff // tn
        if kind == 'mn':
            in_specs.append(pl.BlockSpec((tm, tn), lambda g0, g1, k, off=off: (ij(g0, g1)[0], ij(g0, g1)[1] + off)))
        else:
            in_specs.append(pl.BlockSpec((1, tn), lambda g0, g1, k, off=off: (0, ij(g0, g1)[1] + off)))
        args.append(arr)
    if o3:
        out_shape = [jax.ShapeDtypeStruct((gj, M, tn), dt) for dt in out_dtypes]
        out_specs = [pl.BlockSpec((None, tm, tn), lambda g0, g1, k: (ij(g0, g1)[1], ij(g0, g1)[0], 0))
                     for _ in out_dtypes]
    else:
        out_shape = [jax.ShapeDtypeStruct((M, N), dt) for dt in out_dtypes]
        out_specs = [pl.BlockSpec((tm, tn), lambda g0, g1, k: ij(g0, g1)) for _ in out_dtypes]
    n_a, n_b, n_e, n_o = len(As), len(Bs), len(extras), len(out_dtypes)
    n_p = len(prods) if gk > 1 else 0
    scratch = [pltpu.VMEM((tm, tn), F32) for _ in range(n_p)]
    if rider is not None:
        in_specs, out_specs = in_specs + [ANY] * len(rider.ins), out_specs + [ANY] * len(rider.out_shape)
        args, out_shape, scratch = args + rider.ins, out_shape + rider.out_shape, scratch + rider.scratch
    n_in, n_out = len(args), len(out_shape)

    def body(*refs):
        a_refs = refs[:n_a]
        b_refs = refs[n_a:n_a + n_b]
        e_refs = refs[n_a + n_b:n_a + n_b + n_e]
        o_refs = refs[n_in:n_in + n_o]
        acc_refs = refs[n_in + n_out:n_in + n_out + n_p]
        k = pl.program_id(2)
        if rider is not None:
            step = (pl.program_id(0) * grid[1] + pl.program_id(1)) * gk + k
            ride = (step, grid[0] * grid[1] * gk, refs[n_a + n_b + n_e:n_in], refs[n_in + n_o:n_in + n_out],
                    refs[n_in + n_out + n_p:])
            rider.begin(*ride)

        def finish(accs):
            for o, r in zip(o_refs, epi(accs, *[e[...] for e in e_refs])):
                o[...] = r.astype(o.dtype)

        if gk == 1:
            finish([functools.reduce(jnp.add, [_dot(a_refs[ai][...], b_refs[bi][...], ta, tb) for ai, bi in terms])
                    for terms in prods])
        else:
            @pl.when(k == 0)
            def _():
                for acc in acc_refs:
                    acc[...] = jnp.zeros_like(acc)

            for p, terms in enumerate(prods):
                for ai, bi in terms:
                    acc_refs[p][...] += _dot(a_refs[ai][...], b_refs[bi][...], ta, tb)

            @pl.when(k == gk - 1)
            def _():
                finish([acc[...] for acc in acc_refs])

        if rider is not None:
            rider.end(*ride)

    aliases = {} if rider is None else {n_a + n_b + n_e + p: n_o + o for p, o in rider.aliases.items()}
    res = pl.pallas_call(
        body, name=name, grid=grid, in_specs=in_specs, out_specs=out_specs, out_shape=out_shape,
        scratch_shapes=scratch, input_output_aliases=aliases,
        compiler_params=_params(("arbitrary", "arbitrary", "arbitrary")),
    )(*args)
    if rider is not None:
        rider.results = res[n_o:]
    return res[:n_o]


def _rows(name, fn, ins, outs, *, T, tb):
    tb = min(tb, T)
    assert T % tb == 0
    in_specs, args = [], []
    for arr, kind, width, cb in ins:
        if kind == 'r':
            in_specs.append(pl.BlockSpec((tb, width), lambda i, cb=cb: (i, cb)))
        else:
            in_specs.append(pl.BlockSpec((1, width), lambda i, cb=cb: (0, cb)))
        args.append(arr)
    out_shape, out_specs = [], []
    for kind, width, dtype in outs:
        if kind == 'r':
            out_shape.append(jax.ShapeDtypeStruct((T, width), dtype))
            out_specs.append(pl.BlockSpec((tb, width), lambda i: (i, 0)))
        elif kind == 'c':
            out_shape.append(jax.ShapeDtypeStruct((T, 1), dtype))
            out_specs.append(pl.BlockSpec((tb, 1), lambda i: (i, 0)))
        else:
            out_shape.append(jax.ShapeDtypeStruct((1, width), F32))
            out_specs.append(pl.BlockSpec((1, width), lambda i: (0, 0)))
    n_in = len(ins)

    def body(*refs):
        i = pl.program_id(0)
        vals = fn(*[r[...] for r in refs[:n_in]])
        for (kind, _, _), o, v in zip(outs, refs[n_in:], vals):
            if kind == 'a':
                @pl.when(i == 0)
                def _(o=o):
                    o[...] = jnp.zeros_like(o)

                o[...] += v
            else:
                o[...] = v.astype(o.dtype)

    return pl.pallas_call(
        body, name=name, grid=(T // tb,), in_specs=in_specs, out_specs=out_specs, out_shape=out_shape,
        compiler_params=_params(("arbitrary",)),
    )(*args)


def _colsum(v):
    return jnp.sum(v, axis=0, keepdims=True)


def _ln_stats(z):
    mu = jnp.mean(z, axis=-1, keepdims=True)
    d = z - mu
    var = jnp.mean(d * d, axis=-1, keepdims=True)
    rstd = lax.rsqrt(var + LN_EPS)
    return d * rstd, rstd


def _ln_bwd_math(dy, xhat, rstd, g):
    dxh = dy * g
    m1 = jnp.mean(dxh, axis=-1, keepdims=True)
    m2 = jnp.mean(dxh * xhat, axis=-1, keepdims=True)
    return rstd * (dxh - m1 - xhat * m2)


def _ln_fwd(name, z, g, b, T, D):
    def fn(z, g, b):
        xhat, rstd = _ln_stats(z)
        y = xhat * g + b
        return [y, y, xhat, rstd]

    return _rows(name, fn, [(z, 'r', D, 0), (g, 'v', D, 0), (b, 'v', D, 0)],
                 [('r', D, F32), ('r', D, BF16), ('r', D, F32), ('c', 1, F32)], T=T, tb=512)


def _ln_bwd(name, dy, xhat, rstd, g, scale, T, D):
    def fn(dy, xhat, rstd, g):
        dz = _ln_bwd_math(dy, xhat, rstd, g)
        return [dz, dz * scale, _colsum(dy * xhat), _colsum(dy)]

    return _rows(name, fn, [(dy, 'r', D, 0), (xhat, 'r', D, 0), (rstd, 'r', 1, 0), (g, 'v', D, 0)],
                 [('r', D, F32), ('r', D, BF16), ('a', D, F32), ('a', D, F32)], T=T, tb=512)


def _ln_loss_bwd(name, z, g, b, tgt, T, D):
    def fn(z, g, b, tgt):
        xhat, rstd = _ln_stats(z)
        err = xhat * g + b - tgt
        row_loss = 0.5 * jnp.mean(err * err, axis=-1, keepdims=True)
        loss = jnp.broadcast_to(jnp.sum(row_loss, axis=0, keepdims=True), (1, 128))
        dy = err * (1.0 / D)
        dz = _ln_bwd_math(dy, xhat, rstd, g)
        return [dz, dz * 0.5, _colsum(dy * xhat), _colsum(dy), loss]

    return _rows(name, fn, [(z, 'r', D, 0), (g, 'v', D, 0), (b, 'v', D, 0), (tgt, 'r', D, 0)],
                 [('r', D, F32), ('r', D, BF16), ('a', D, F32), ('a', D, F32), ('a', 128, F32)], T=T, tb=512)


class _Net:
    def __init__(self, comm, G):
        self.comm, self.G = comm, G

    def gather(self, names, part=None):
        return self.comm.gather(names, part) if self.comm else None

    def exchange(self, names, part=None):
        return self.comm.exchange(names, self.G, part) if self.comm else None

    def done(self, rider):
        if rider is not None:
            self.comm.collect(rider)


def _ffn_fwd(tag, xb, x, W, names, net, rider=None, rider_down=None):
    def epi_gu(accs):
        a, b = accs
        return [a, b, a * _sigmoid(a) * b]

    ng, nu, nd = names
    a, b, s = _mm(tag + "_gate_up", [xb], [W[ng], W[nu]], [[(0, 0)], [(0, 1)]], epi_gu, [F32, F32, BF16],
                  tm=2048, tn=256, tk=1024, rider=rider)
    net.done(rider)

    def epi_down(accs, xres):
        return [ALPHA * xres + 0.5 * accs[0]]

    rider_down = rider_down() if rider_down else None
    (z,) = _mm(tag + "_down", [s], [W[nd]], [[(0, 0)]], epi_down, [F32], tm=1024, tn=1024, tk=1408,
               extras=[(x, 'mn', 0)], rider=rider_down)
    net.done(rider_down)
    return a, b, s, z


def _ffn_bwd(tag, dzh, dz, xb, a, b, s, W, names, gdt, G, net, ride, pre=(None, None)):
    ng, nu, nd = names

    def epi_ds(accs, a, b):
        ds = accs[0]
        sg = _sigmoid(a)
        return [ds * b * _dsilu(a, sg), ds * a * sg]

    rider = pre[0]() if pre[0] else None
    da, db = _mm(tag + "_ds", [dzh], [W[nd]], [[(0, 0)]], epi_ds, [BF16, BF16], tb=True, tm=2048, tn=256, tk=1024,
                 extras=[(a, 'mn', 0), (b, 'mn', 0)], rider=rider)
    net.done(rider)
    ident = lambda accs: accs
    rider = pre[1]() if pre[1] else None
    (G[nd],) = _mm(tag + "_dwd", [s], [dzh], [[(0, 0)]], ident, [gdt], ta=True, tm=1408, tn=1024, tk=1024,
                   rider=rider)
    net.done(rider)
    if ride:
        rider = net.exchange([nd])
        (G[ng],) = _mm(tag + "_dwg", [xb], [da], [[(0, 0)]], ident, [gdt], ta=True, tm=1024, tn=1408, tk=1024,
                       rider=rider)
        net.done(rider)
        rider = net.exchange([ng])
        (G[nu],) = _mm(tag + "_dwu", [xb], [db], [[(0, 0)]], ident, [gdt], ta=True, tm=1024, tn=1408, tk=1024,
                       rider=rider)
        net.done(rider)
    else:
        G[ng], G[nu] = _mm(tag + "_dwgu", [xb], [da, db], [[(0, 0)], [(0, 1)]], ident, [gdt, gdt], ta=True,
                           tm=1024, tn=1408, tk=1024)

    def epi_dx(accs, dzres):
        return [ALPHA * dzres + accs[0]]

    rider = net.exchange([nu]) if ride else None
    (dx,) = _mm(tag + "_dx", [da, db], [W[ng], W[nu]], [[(0, 0), (1, 1)]], epi_dx, [F32], tb=True,
                tm=1024, tn=1024, tk=1408, extras=[(dz, 'mn', 0)], rider=rider)
    net.done(rider)
    return dx


def _ret_tables(H, T):
    C = RET_CHUNK
    log_g = jnp.log(1.0 - jnp.exp2(-5.0 - jnp.arange(H, dtype=F32)))
    idx = jnp.arange(C, dtype=F32)
    diff = idx[:, None] - idx[None, :]
    dm = jnp.where(diff[None] >= 0, jnp.exp(jnp.maximum(diff, 0.0)[None] * log_g[:, None, None]), 0.0)
    xi = jnp.exp((idx[None, :] + 1.0) * log_g[:, None])[:, :, None]
    zeta = jnp.exp((C - 1.0 - idx)[None, :] * log_g[:, None])[:, :, None]
    gc = jnp.broadcast_to(jnp.exp(C * log_g)[:, None, None], (H, 1, RET_DV))
    half = RET_DK // 2
    freqs = ROPE_BASE ** (-jnp.arange(half, dtype=F32) / half)
    ang = jnp.arange(T, dtype=F32)[:, None] * freqs[None, :]
    cos, sin = jnp.cos(ang), jnp.sin(ang)
    cosf = jnp.concatenate([cos, cos], axis=1)
    sins = jnp.concatenate([-sin, sin], axis=1)
    return dm, xi, zeta, gc, cosf, sins


def _rot(x, cosf, sins):
    return x * cosf + pltpu.roll(x, RET_DK // 2, 1) * sins


def _rot_bwd(dy, cosf, sins):
    return dy * cosf + pltpu.roll(dy * sins, RET_DK // 2, 1)


RET_HB = 8


def _ret_specs(H, HB, rev, NC):
    C, G = RET_CHUNK, H // HB
    nn = (lambda n: NC - 1 - n) if rev else (lambda n: n)
    return [
        pl.BlockSpec((C, HB * RET_DK), lambda h, n: (nn(n), h)),
        pl.BlockSpec((C, HB * RET_DK), lambda h, n: (nn(n), G + h)),
        pl.BlockSpec((C, HB * RET_DV), lambda h, n: (nn(n), G + h)),
        pl.BlockSpec((C, HB * RET_DV), lambda h, n: (nn(n), 2 * G + h)),
        pl.BlockSpec((C, RET_DK), lambda h, n: (nn(n), 0)),
        pl.BlockSpec((C, RET_DK), lambda h, n: (nn(n), 0)),
        pl.BlockSpec((1, HB * RET_DV), lambda h, n: (0, h)),
        pl.BlockSpec((HB, C, C), lambda h, n: (h, 0, 0)),
        pl.BlockSpec((HB, C, 1), lambda h, n: (h, 0, 0)),
        pl.BlockSpec((HB, C, 1), lambda h, n: (h, 0, 0)),
        pl.BlockSpec((HB, 1, RET_DV), lambda h, n: (h, 0, 0)),
    ]


def _ret_fwd(proj, gn_g, tabs, H, T, rider=None):
    C, NC = RET_CHUNK, T // RET_CHUNK
    HB = min(RET_HB, H)
    dm, xi, zeta, gc, cosf, sins = tabs
    scale = RET_DK ** -0.5

    def body(q_ref, k_ref, v_ref, g_ref, cos_ref, sin_ref, gn_ref, dm_ref, xi_ref, zt_ref, gc_ref,
             r_ref, ri_ref, st_ref, state):
        @pl.when(pl.program_id(1) == 0)
        def _():
            state[...] = jnp.zeros_like(state)

        cs, sn = cos_ref[...], sin_ref[...]
        for hh in range(HB):
            qk = slice(hh * RET_DK, (hh + 1) * RET_DK)
            vv = slice(hh * RET_DV, (hh + 1) * RET_DV)
            qr = _rot(q_ref[:, qk], cs, sn) * scale
            kr = _rot(k_ref[:, qk], cs, sn)
            qb, kb, vb = qr.astype(BF16), kr.astype(BF16), v_ref[:, vv].astype(BF16)
            st = state[hh]
            stb = st.astype(BF16)
            s = _dot(qb, kb, tb=True) * dm_ref[hh]
            r = _dot(s.astype(BF16), vb) + _dot(qb, stb) * xi_ref[hh]
            st_ref[hh] = stb
            state[hh] = gc_ref[hh] * st + _dot((kr * zt_ref[hh]).astype(BF16), vb, ta=True)
            rhat, _ = _ln_stats(r)
            g = g_ref[:, vv]
            r_ref[:, vv] = r
            ri_ref[:, vv] = (g * _sigmoid(g) * (rhat * gn_ref[:, vv])).astype(BF16)

    VW = H * RET_DV
    return _hosted_call(
        body, rider, name="ret_fwd", grid=(H // HB, NC), in_specs=_ret_specs(H, HB, False, NC),
        out_specs=[pl.BlockSpec((C, HB * RET_DV), lambda h, n: (n, h)),
                   pl.BlockSpec((C, HB * RET_DV), lambda h, n: (n, h)),
                   pl.BlockSpec((HB, None, RET_DK, RET_DV), lambda h, n: (h, n, 0, 0))],
        out_shape=[jax.ShapeDtypeStruct((T, VW), F32), jax.ShapeDtypeStruct((T, VW), BF16),
                   jax.ShapeDtypeStruct((H, NC, RET_DK, RET_DV), BF16)],
        scratch=[pltpu.VMEM((HB, RET_DK, RET_DV), F32)],
        args=[proj, proj, proj, proj, cosf, sins, gn_g, dm, xi, zeta, gc])


def _hosted_call(body, rider, *, name, grid, in_specs, out_specs, out_shape, scratch, args):
    n_in, n_out, n_scr = len(args), len(out_shape), len(scratch)
    if rider is None:
        hosted = body
    else:
        n_ri, n_ro = len(rider.ins), len(rider.out_shape)
        in_specs, out_specs = in_specs + [ANY] * n_ri, out_specs + [ANY] * n_ro
        args, out_shape, scratch = args + rider.ins, out_shape + rider.out_shape, scratch + rider.scratch

        def hosted(*refs):
            o0, s0 = n_in + n_ri, n_in + n_ri + n_out + n_ro
            step = pl.program_id(0) * grid[1] + pl.program_id(1)
            ride = (step, grid[0] * grid[1], refs[n_in:o0], refs[o0 + n_out:s0], refs[s0 + n_scr:])
            rider.begin(*ride)
            body(*refs[:n_in], *refs[o0:o0 + n_out], *refs[s0:s0 + n_scr])
            rider.end(*ride)

    aliases = {} if rider is None else {n_in + p: n_out + o for p, o in rider.aliases.items()}
    res = pl.pallas_call(
        hosted, name=name, grid=grid, in_specs=in_specs, out_specs=out_specs, out_shape=out_shape,
        scratch_shapes=scratch, input_output_aliases=aliases, compiler_params=_params(("arbitrary", "arbitrary")),
    )(*args)
    if rider is not None:
        rider.results = res[n_out:]
    return res[:n_out]


def _ret_bwd(dri, r, states, proj, gn_g, tabs, H, T, in_w, rider=None):
    C, NC = RET_CHUNK, T // RET_CHUNK
    HB = min(RET_HB, H)
    dm, xi, zeta, gc, cosf, sins = tabs
    scale = RET_DK ** -0.5

    def body(q_ref, k_ref, v_ref, g_ref, cos_ref, sin_ref, gn_ref, dm_ref, xi_ref, zt_ref, gc_ref,
             dri_ref, r_ref, st_ref, dp_ref, dgn_ref, dstate):
        @pl.when(pl.program_id(1) == 0)
        def _():
            dstate[...] = jnp.zeros_like(dstate)
            dgn_ref[...] = jnp.zeros_like(dgn_ref)

        cs, sn = cos_ref[...], sin_ref[...]
        for hh in range(HB):
            qk = slice(hh * RET_DK, (hh + 1) * RET_DK)
            vv = slice(hh * RET_DV, (hh + 1) * RET_DV)
            qr = _rot(q_ref[:, qk], cs, sn) * scale
            kr = _rot(k_ref[:, qk], cs, sn)
            qb, kb, vb = qr.astype(BF16), kr.astype(BF16), v_ref[:, vv].astype(BF16)
            xi_c, zt_c, dmask = xi_ref[hh], zt_ref[hh], dm_ref[hh]
            rhat, rstd = _ln_stats(r_ref[:, vv])
            g, gn, dpre = g_ref[:, vv], gn_ref[:, vv], dri_ref[:, vv]
            sg = _sigmoid(g)
            dp_ref[:, 2 * QW + VW + hh * RET_DV:2 * QW + VW + (hh + 1) * RET_DV] = (
                dpre * (rhat * gn) * _dsilu(g, sg)).astype(BF16)
            drn = dpre * (g * sg)
            dgn_ref[:, vv] += _colsum(drn * rhat)
            drb = _ln_bwd_math(drn, rhat, rstd, gn).astype(BF16)
            ds1 = dstate[hh]
            ds1b = ds1.astype(BF16)
            sb = (_dot(qb, kb, tb=True) * dmask).astype(BF16)
            kzb = (kr * zt_c).astype(BF16)
            dp_ref[:, 2 * QW + hh * RET_DV:2 * QW + (hh + 1) * RET_DV] = (
                _dot(sb, drb, ta=True) + _dot(kzb, ds1b)).astype(BF16)
            dsb = (_dot(drb, vb, tb=True) * dmask).astype(BF16)
            dq = _dot(dsb, kb) + _dot(drb, st_ref[hh], tb=True) * xi_c
            dk = _dot(dsb, qb, ta=True) + _dot(vb, ds1b, tb=True) * zt_c
            dstate[hh] = gc_ref[hh] * ds1 + _dot((qr * xi_c).astype(BF16), drb, ta=True)
            dp_ref[:, qk] = _rot_bwd(dq * scale, cs, sn).astype(BF16)
            dp_ref[:, QW + hh * RET_DK:QW + (hh + 1) * RET_DK] = _rot_bwd(dk, cs, sn).astype(BF16)

    VW, QW = H * RET_DV, H * RET_DK
    rv = lambda n: NC - 1 - n
    in_specs = _ret_specs(H, HB, True, NC) + [
        pl.BlockSpec((C, HB * RET_DV), lambda h, n: (rv(n), h)),
        pl.BlockSpec((C, HB * RET_DV), lambda h, n: (rv(n), h)),
        pl.BlockSpec((HB, None, RET_DK, RET_DV), lambda h, n: (h, rv(n), 0, 0)),
    ]
    assert HB == H
    return _hosted_call(
        body, rider, name="ret_bwd", grid=(1, NC), in_specs=in_specs,
        out_specs=[pl.BlockSpec((C, 2 * QW + 2 * VW), lambda h, n: (rv(n), 0)),
                   pl.BlockSpec((1, VW), lambda h, n: (0, 0))],
        out_shape=[jax.ShapeDtypeStruct((T, in_w), BF16), jax.ShapeDtypeStruct((1, VW), F32)],
        scratch=[pltpu.VMEM((HB, RET_DK, RET_DV), F32)],
        args=[proj, proj, proj, proj, cosf, sins, gn_g, dm, xi, zeta, gc, dri, r, states])


CONV_CW = 128
CONV_TB = 512


def _conv_fwd(proj, kpad, bias, off_a, CC, T):
    tb, cw = min(CONV_TB, T), CONV_CW
    hb = tb // HALO
    ca, cb = off_a // cw, (off_a + CC) // cw

    def body(a_ref, b_ref, ap_ref, bp_ref, k_ref, bias_ref, u1_ref, win):
        i = pl.program_id(0)
        keep = (i > 0).astype(F32)
        win[0:HALO, :] = ap_ref[...] * _sigmoid(bp_ref[...]) * keep
        win[HALO:, :] = a_ref[...] * _sigmoid(b_ref[...])
        acc = jnp.broadcast_to(bias_ref[...], (tb, cw))
        for w in range(CONV_WIDTH):
            acc = acc + k_ref[w:w + 1, :] * win[pl.ds(HALO - (CONV_WIDTH - 1) + w, tb), :]
        u1_ref[...] = acc

    prev = lambda i: jnp.maximum(i * hb - 1, 0)
    return pl.pallas_call(
        body, name="conv_fwd", grid=(T // tb, CC // cw),
        in_specs=[pl.BlockSpec((tb, cw), lambda i, c: (i, ca + c)),
                  pl.BlockSpec((tb, cw), lambda i, c: (i, cb + c)),
                  pl.BlockSpec((HALO, cw), lambda i, c: (prev(i), ca + c)),
                  pl.BlockSpec((HALO, cw), lambda i, c: (prev(i), cb + c)),
                  pl.BlockSpec((HALO, cw), lambda i, c: (0, c)),
                  pl.BlockSpec((1, cw), lambda i, c: (0, c))],
        out_specs=pl.BlockSpec((tb, cw), lambda i, c: (i, c)),
        out_shape=jax.ShapeDtypeStruct((T, CC), F32),
        scratch_shapes=[pltpu.VMEM((tb + HALO, cw), F32)],
        compiler_params=_params(("arbitrary", "arbitrary")),
    )(proj, proj, proj, proj, kpad, bias)


def _conv_bwd(du1, proj, kpad, off_a, CC, T, rider=None):
    tb, cw = min(CONV_TB, T), CONV_CW
    hb = tb // HALO
    nt = T // tb
    ca, cb = off_a // cw, (off_a + CC) // cw

    def body(d_ref, dn_ref, a_ref, b_ref, ap_ref, bp_ref, k_ref, da_ref, db_ref, dk_ref, winu, wind):
        i = pl.program_id(1)
        a, b = a_ref[...], b_ref[...]
        sgb = _sigmoid(b)
        winu[0:HALO, :] = ap_ref[...] * _sigmoid(bp_ref[...]) * (i > 0).astype(F32)
        winu[HALO:, :] = a * sgb
        d = d_ref[...]
        wind[0:tb, :] = d
        wind[tb:, :] = dn_ref[...] * (i < nt - 1).astype(F32)

        @pl.when(i == 0)
        def _():
            dk_ref[...] = jnp.zeros_like(dk_ref)

        du0 = jnp.zeros((tb, cw), F32)
        for w in range(CONV_WIDTH):
            du0 = du0 + k_ref[w:w + 1, :] * wind[pl.ds(CONV_WIDTH - 1 - w, tb), :]
            dk_ref[w:w + 1, :] += _colsum(winu[pl.ds(HALO - (CONV_WIDTH - 1) + w, tb), :] * d)
        da_ref[...] = (du0 * sgb).astype(BF16)
        db_ref[...] = (du0 * a * sgb * (1.0 - sgb)).astype(BF16)

    prev = lambda i: jnp.maximum(i * hb - 1, 0)
    nxt = lambda i: jnp.minimum((i + 1) * hb, T // HALO - 1)
    return _hosted_call(
        body, rider, name="conv_bwd", grid=(CC // cw, nt),
        in_specs=[pl.BlockSpec((tb, cw), lambda c, i: (i, c)),
                  pl.BlockSpec((HALO, cw), lambda c, i: (nxt(i), c)),
                  pl.BlockSpec((tb, cw), lambda c, i: (i, ca + c)),
                  pl.BlockSpec((tb, cw), lambda c, i: (i, cb + c)),
                  pl.BlockSpec((HALO, cw), lambda c, i: (prev(i), ca + c)),
                  pl.BlockSpec((HALO, cw), lambda c, i: (prev(i), cb + c)),
                  pl.BlockSpec((HALO, cw), lambda c, i: (0, c))],
        out_specs=[pl.BlockSpec((tb, cw), lambda c, i: (i, c)),
                   pl.BlockSpec((tb, cw), lambda c, i: (i, c)),
                   pl.BlockSpec((HALO, cw), lambda c, i: (0, c))],
        out_shape=[jax.ShapeDtypeStruct((T, CC), BF16), jax.ShapeDtypeStruct((T, CC), BF16),
                   jax.ShapeDtypeStruct((HALO, CC), F32)],
        scratch=[pltpu.VMEM((tb + HALO, cw), F32), pltpu.VMEM((tb + HALO, cw), F32)],
        args=[du1, du1, proj, proj, proj, proj, kpad])


FFN1 = ('ffn1_w_gate', 'ffn1_w_up', 'ffn1_w_down')
FFN2 = ('ffn2_w_gate', 'ffn2_w_up', 'ffn2_w_down')


def _local_step(x, tgt, W, P, gdt=BF16, comm=None):
    T, D = x.shape
    G = {}
    net = _Net(comm, G)
    if comm is not None:
        W = comm.W
        first = net.gather(['ffn1_w_gate', 'ffn1_w_up'])
        _run_rider("gather_ffn1_in", first)
        net.done(first)
    VW = P['ret_gn_g'].shape[1]
    H = VW // RET_DV
    QW = H * RET_DK
    CC = P['conv_b'].shape[1]
    off_glu = 2 * QW + 2 * VW
    off_gate = off_glu + 2 * CC
    ident = lambda accs: accs
    xb = x.astype(BF16)

    a1, b1, s1, z1 = _ffn_fwd("ffn1", xb, x, W, FFN1, net,
                              rider=net.gather(['ffn1_w_down', 'w_in'], {'w_in': (0, D // 2, False)}),
                              rider_down=lambda: net.gather(['w_in'], {'w_in': (D // 2, D // 2, True)}))
    x1, x1b, xh1, rs1 = _ln_fwd("ln1", z1, P['ln1_g'], P['ln1_b'], T, D)

    rest = net.gather(['conv_k', 'w_ret_o', 'w_conv_o', 'w_out'])
    (proj,) = _mm("w_in", [x1b], [W['w_in']], [[(0, 0)]], lambda accs, bias: [accs[0] + bias], [F32],
                  tm=2048, tn=0, tk=1024, extras=[(P['b_in'], 'n', 0)], i_outer=False, b3=True, rider=rest)
    net.done(rest)
    tabs = _ret_tables(H, T)
    rider = net.gather(list(FFN2))
    r, ret_in, states = _ret_fwd(proj, P['ret_gn_g'], tabs, H, T, rider=rider)
    net.done(rider)
    kpad = jnp.pad(W['conv_k'].astype(F32), ((0, HALO - CONV_WIDTH), (0, 0)))
    u1 = _conv_fwd(proj, kpad, P['conv_b'], off_glu, CC, T)

    def conv_ln(u1, g, b):
        xhat, rstd = _ln_stats(u1)
        u2 = xhat * g + b
        return [xhat, rstd, u2 * _sigmoid(u2)]

    xhc, rsc, u3 = _rows("conv_ln", conv_ln, [(u1, 'r', CC, 0), (P['conv_ln_g'], 'v', CC, 0), (P['conv_ln_b'], 'v', CC, 0)],
                         [('r', CC, F32), ('c', 1, F32), ('r', CC, BF16)], T=T, tb=512)
    (ret_out,) = _mm("ret_o", [ret_in], [W['w_ret_o']], [[(0, 0)]], ident, [F32], tm=1024, tn=1024, tk=2048)

    def epi_merge(accs, ret_out, gr, gc):
        conv_out = accs[0]
        return [conv_out, _sigmoid(gr) * ret_out + _sigmoid(gc) * conv_out]

    conv_out, merged = _mm("conv_o_merge", [u3], [W['w_conv_o']], [[(0, 0)]], epi_merge, [F32, BF16],
                           tm=512, tn=D, tk=1024,
                           extras=[(ret_out, 'mn', 0), (proj, 'mn', off_gate), (proj, 'mn', off_gate + D)])
    (z2,) = _mm("w_out", [merged], [W['w_out']], [[(0, 0)]], lambda accs, xr: [ALPHA * xr + accs[0]], [F32],
                tm=1024, tn=1024, tk=1024, extras=[(x1, 'mn', 0)])
    x2, x2b, xh2, rs2 = _ln_fwd("ln2", z2, P['ln2_g'], P['ln2_b'], T, D)
    a2, b2, s2, z3 = _ffn_fwd("ffn2", x2b, x2, W, FFN2, net)
    dz3, dz3h, g_ln3_g, g_ln3_b, loss = _ln_loss_bwd("ln3_loss", z3, P['ln3_g'], P['ln3_b'], tgt, T, D)

    S = {'ln3_g': g_ln3_g, 'ln3_b': g_ln3_b}
    dy2 = _ffn_bwd("ffn2b", dz3h, dz3, x2b, a2, b2, s2, W, FFN2, gdt, G, net, False)
    dz2, dz2b, S['ln2_g'], S['ln2_b'] = _ln_bwd("ln2b", dy2, xh2, rs2, P['ln2_g'], 1.0, T, D)

    (G['w_out'],) = _mm("d_w_out", [merged], [dz2b], [[(0, 0)]], ident, [gdt], ta=True, tm=1024, tn=1024, tk=1024)

    def epi_dmerge(accs, ret_out, conv_out, gr, gc):
        dm_ = accs[0]
        sr, sc = _sigmoid(gr), _sigmoid(gc)
        return [dm_ * sr, dm_ * sc, dm_ * ret_out * sr * (1.0 - sr), dm_ * conv_out * sc * (1.0 - sc)]

    dret_out, dconv_out, dgate_r, dgate_c = _mm(
        "d_merge", [dz2b], [W['w_out']], [[(0, 0)]], epi_dmerge, [BF16, BF16, BF16, BF16], tb=True,
        tm=512, tn=D, tk=1024,
        extras=[(ret_out, 'mn', 0), (conv_out, 'mn', 0), (proj, 'mn', off_gate), (proj, 'mn', off_gate + D)])
    (G['w_ret_o'],) = _mm("d_w_ret_o", [ret_in], [dret_out], [[(0, 0)]], ident, [gdt], ta=True, tm=1024, tn=1024, tk=1024)
    (G['w_conv_o'],) = _mm("d_w_conv_o", [u3], [dconv_out], [[(0, 0)]], ident, [gdt], ta=True, tm=1024, tn=1024, tk=1024)
    (dri,) = _mm("d_ret_in", [dret_out], [W['w_ret_o']], [[(0, 0)]], ident, [F32], tb=True, tm=1024, tn=1024, tk=1024)
    rider = net.exchange(list(FFN2))
    dproj, S['ret_gn_g'] = _ret_bwd(dri, r, states, proj, P['ret_gn_g'], tabs, H, T, proj.shape[1], rider=rider)
    net.done(rider)

    def epi_du2(accs, xhat, g, b):
        u2 = xhat * g + b
        return [accs[0] * _dsilu(u2, _sigmoid(u2))]

    (du2,) = _mm("d_u3", [dconv_out], [W['w_conv_o']], [[(0, 0)]], epi_du2, [F32], tb=True, tm=512, tn=CC, tk=1024,
                 extras=[(xhc, 'mn', 0), (P['conv_ln_g'], 'n', 0), (P['conv_ln_b'], 'n', 0)])

    def conv_ln_bwd(du2, xhat, rstd, g):
        du1 = _ln_bwd_math(du2, xhat, rstd, g)
        return [du1, _colsum(du2 * xhat), _colsum(du2), _colsum(du1)]

    du1, S['conv_ln_g'], S['conv_ln_b'], S['conv_b'] = _rows(
        "conv_ln_bwd", conv_ln_bwd, [(du2, 'r', CC, 0), (xhc, 'r', CC, 0), (rsc, 'r', 1, 0), (P['conv_ln_g'], 'v', CC, 0)],
        [('r', CC, F32), ('a', CC, F32), ('a', CC, F32), ('a', CC, F32)], T=T, tb=512)
    rider = net.exchange(['w_out', 'w_ret_o', 'w_conv_o'])
    dglu_a, dglu_b, dkpad = _conv_bwd(du1, proj, kpad, off_glu, CC, T, rider=rider)
    net.done(rider)
    G['conv_k'] = dkpad[:CONV_WIDTH].astype(gdt)

    for off, piece in ((off_glu, dglu_a), (off_glu + CC, dglu_b), (off_gate, dgate_r), (off_gate + D, dgate_c)):
        dproj = lax.dynamic_update_slice(dproj, piece, (0, off))
    IN_W = dproj.shape[1]
    (S['b_in'],) = _rows("d_b_in", lambda d: [_colsum(d.astype(F32))], [(dproj, 'r', IN_W, 0)], [('a', IN_W, F32)],
                         T=T, tb=256)
    (G['w_in'],) = _mm("d_w_in", [x1b], [dproj], [[(0, 0)]], ident, [gdt], ta=True, o3=True,
                       tm=1024, tn=W['w_in'].shape[2], tk=1024)
    cuts = [0, (9 * D) // 16, (53 * D) // 64, D]
    w_in_rows = [{'w_in': (cuts[i], cuts[i + 1] - cuts[i], i == 2)} for i in range(3)]
    rider = net.exchange(['w_in', 'conv_k'], w_in_rows[0])
    (dy1,) = _mm("d_x1", [dproj], [W['w_in']], [[(0, 0)]], lambda accs, dzr: [ALPHA * dzr + accs[0]], [F32], tb=True,
                 b3=True, tm=1024, tn=1024, tk=0, extras=[(dz2, 'mn', 0)], rider=rider)
    net.done(rider)
    dz1, dz1h, S['ln1_g'], S['ln1_b'] = _ln_bwd("ln1b", dy1, xh1, rs1, P['ln1_g'], 0.5, T, D)
    grad_x = _ffn_bwd("ffn1b", dz1h, dz1, xb, a1, b1, s1, W, FFN1, gdt, G, net, True,
                      pre=(lambda: net.exchange(['w_in'], w_in_rows[1]), lambda: net.exchange(['w_in'], w_in_rows[2])))
    return loss[0, 0], grad_x, G, S


def _coords():
    return lax.axis_index("x"), lax.axis_index("y"), lax.axis_index("c")


def _flip(k, x, y, c):
    return (1 - x if k & 4 else x, 1 - y if k & 2 else y, 1 - c if k & 1 else c)


def _lin(p):
    return 4 * p[0] + 2 * p[1] + p[2]


class _Rider:
    def __init__(self, ins, out_shape, rows=None, fill=None):
        nb = len(ins)
        self.rows = rows or [None] * nb
        fill = fill or [None] * nb
        self.aliases = {nb + i: w for i, w in enumerate(w for w in range(nb) if fill[w] is not None)}
        self.ins = list(ins) + [f for f in fill if f is not None]
        self.out_shape, self.results = list(out_shape), None
        self.scratch = [pltpu.SemaphoreType.DMA((7 * nb,)), pltpu.SemaphoreType.DMA((7 * nb,)),
                        pltpu.SemaphoreType.DMA((nb,))]

    def span(self, w, ref, *slot):
        if self.rows[w] is None:
            return ref.at[slot] if slot else ref
        return ref.at[(*slot, pl.ds(*self.rows[w]))]

    def begin(self, step, n_steps, ins, outs, sems):
        @pl.when(step == 0)
        def _():
            self.start(ins, outs, sems)

        @pl.when(step == n_steps - 1)
        def _():
            self.mid(ins, outs, sems)

    def end(self, step, n_steps, ins, outs, sems):
        @pl.when(step == n_steps - 1)
        def _():
            self.finish(ins, outs, sems)

    def mid(self, ins, outs, sems):
        pass


class _GatherRider(_Rider):
    def __init__(self, blks, rows=None, fill=None):
        super().__init__(blks, [jax.ShapeDtypeStruct((N_DEV,) + b.shape, b.dtype) for b in blks], rows, fill)

    def _copies(self, x_refs, out_refs, sems):
        nb = len(self.out_shape)
        send_sems, recv_sems, local_sems = sems
        x, y, c = _coords()
        me, sibling = (x, y, c), (x, y, 1 - c)
        chips = [_flip(4, x, y, c), _flip(2, x, y, c), _flip(6, x, y, c)]
        own = [self.span(w, x_refs[w]) for w in range(nb)]

        def copy(k, w, block, to, src=None):
            slot = self.span(w, out_refs[w], _lin(block))
            return pltpu.make_async_remote_copy(
                src_ref=slot if src is None else src, dst_ref=slot, send_sem=send_sems.at[k * nb + w],
                recv_sem=recv_sems.at[k * nb + w], device_id=to, device_id_type=MESH)

        mines = [pltpu.make_async_copy(own[w], self.span(w, out_refs[w], _lin(me)), local_sems.at[w])
                 for w in range(nb)]
        first = [copy(0, w, me, sibling, src=own[w]) for w in range(nb)]
        first += [copy(1 + j, w, me, chip, src=own[w]) for w in range(nb) for j, chip in enumerate(chips)]
        landed = [(copy(1 + j, w, chip, me), copy(4 + j, w, chip, sibling)) for w in range(nb) for j, chip in enumerate(chips)]
        from_sibling = [copy(0, w, sibling, me) for w in range(nb)]
        from_sibling += [copy(4 + j, w, (chip[0], chip[1], 1 - c), me) for w in range(nb) for j, chip in enumerate(chips)]
        return mines, first, landed, from_sibling

    def start(self, ins, outs, sems):
        mines, first, _, _ = self._copies(ins, outs, sems)
        for cp in mines + first:
            cp.start()

    def mid(self, ins, outs, sems):
        for arrival, onward in self._copies(ins, outs, sems)[2]:
            arrival.wait_recv()
            onward.start()

    def finish(self, ins, outs, sems):
        mines, first, landed, from_sibling = self._copies(ins, outs, sems)
        for cp in from_sibling:
            cp.wait_recv()
        for cp in first + [onward for _, onward in landed]:
            cp.wait_send()
        for mine in mines:
            mine.wait()


class _ExchangeRider(_Rider):
    def __init__(self, gs, rows=None, fill=None):
        super().__init__(gs, [jax.ShapeDtypeStruct(g.shape, g.dtype) for g in gs], rows, fill)

    def _copies(self, g_refs, out_refs, sems):
        nb = len(self.out_shape)
        send_sems, recv_sems, local_sems = sems
        x, y, c = _coords()
        me = _lin((x, y, c))

        def copy(k, w, landing):
            peer = _flip(k, x, y, c)
            src, dst = (me, _lin(peer)) if landing else (_lin(peer), me)
            return pltpu.make_async_remote_copy(
                src_ref=self.span(w, g_refs[w], src), dst_ref=self.span(w, out_refs[w], dst),
                send_sem=send_sems.at[(k - 1) * nb + w], recv_sem=recv_sems.at[(k - 1) * nb + w],
                device_id=peer, device_id_type=MESH)

        mines = [pltpu.make_async_copy(self.span(w, g_refs[w], me), self.span(w, out_refs[w], me), local_sems.at[w])
                 for w in range(nb)]
        sends = [copy(k, w, False) for w in range(nb) for k in range(1, N_DEV)]
        landings = [copy(k, w, True) for w in range(nb) for k in range(1, N_DEV)]
        return mines, sends, landings

    def start(self, ins, outs, sems):
        mines, sends, _ = self._copies(ins, outs, sems)
        for cp in mines + sends:
            cp.start()

    def finish(self, ins, outs, sems):
        mines, sends, landings = self._copies(ins, outs, sems)
        for cp in landings:
            cp.wait_recv()
        for cp in sends:
            cp.wait_send()
        for mine in mines:
            mine.wait()


def _run_rider(name, rider):
    n_in, n_out = len(rider.ins), len(rider.out_shape)

    def body(*refs):
        ride = (refs[:n_in], refs[n_in:n_in + n_out], refs[n_in + n_out:])
        rider.start(*ride)
        rider.mid(*ride)
        rider.finish(*ride)

    rider.results = pl.pallas_call(
        body, name=name, out_shape=rider.out_shape, in_specs=[ANY] * n_in, out_specs=[ANY] * n_out,
        scratch_shapes=rider.scratch, input_output_aliases=dict(rider.aliases),
        compiler_params=pltpu.CompilerParams(has_side_effects=True),
    )(*rider.ins)
    return rider.results


def _as_matrix(name, g):
    if name == 'w_in':
        return g
    if name in COL_SHARDED:
        return jnp.transpose(g, (1, 0, 2)).reshape(g.shape[1], N_DEV * g.shape[2])
    return g.reshape(N_DEV * g.shape[1], g.shape[2])


def _by_owner(name, g):
    if name == 'w_in':
        return g
    if name in COL_SHARDED:
        return jnp.transpose(g.reshape(g.shape[0], N_DEV, g.shape[1] // N_DEV), (1, 0, 2))
    return g.reshape(N_DEV, g.shape[0] // N_DEV, g.shape[1])


class _Comm:
    def __init__(self, shards):
        self.shards, self.W, self.parts, self.partial, self.sent = shards, {}, {}, {}, {}

    def _ride(self, cls, names, srcs, part, sink):
        part = part or {}
        rider = cls(srcs, rows=[part[n][:2] if n in part else None for n in names],
                    fill=[self.partial.pop((sink, n), None) for n in names])
        rider.names, rider.sink = names, sink
        rider.unfinished = {n for n in names if n in part and not part[n][2]}
        return rider

    def gather(self, names, part=None):
        return self._ride(_GatherRider, names, [self.shards[n] for n in names], part, 'W')

    def exchange(self, names, G, part=None):
        for n in names:
            if n not in self.sent:
                self.sent[n] = _by_owner(n, G[n])
        return self._ride(_ExchangeRider, names, [self.sent[n] for n in names], part, 'parts')

    def collect(self, rider):
        for n, res in zip(rider.names, rider.results):
            if n in rider.unfinished:
                self.partial[(rider.sink, n)] = res
            elif rider.sink == 'W':
                self.W[n] = _as_matrix(n, res)
            else:
                self.parts[n] = res


def _adamw_math(p_ref, w_ref, m_ref, v_ref, g_ref, d_ref, nm_ref, nv_ref):
    c1 = 1.0 - ADAM_B1 ** ADAM_STEP
    c2 = 1.0 - ADAM_B2 ** ADAM_STEP
    g = p_ref[0].astype(F32)
    for s in range(1, p_ref.shape[0]):
        g = g + p_ref[s].astype(F32)
    nm = ADAM_B1 * m_ref[...] + (1.0 - ADAM_B1) * g
    nv = ADAM_B2 * v_ref[...] + (1.0 - ADAM_B2) * (g * g)
    g_ref[...] = g
    nm_ref[...] = nm
    nv_ref[...] = nv
    d_ref[...] = -ADAM_LR * ((nm / c1) / (jnp.sqrt(nv / c2) + ADAM_EPS) + ADAM_WD * w_ref[...])


def _adamw_vectors(parts, ws, ms, vs):
    k = len(ws)

    def body(*refs):
        for i in range(k):
            _adamw_math(refs[i], refs[k + i], refs[2 * k + i], refs[3 * k + i], *refs[4 * k + 4 * i:4 * k + 4 * i + 4])

    return pl.pallas_call(
        body, name="adamw_vectors", out_shape=[jax.ShapeDtypeStruct(w.shape, F32) for w in ws for _ in range(4)],
        compiler_params=_params(),
    )(*parts, *ws, *ms, *vs)


def _adamw(name, parts, w, m, v, tb):
    n, R, Wd = parts.shape
    assert R % tb == 0
    body = functools.partial(_adamw_math)

    row = pl.BlockSpec((tb, Wd), lambda i: (i, 0))
    return pl.pallas_call(
        body, name=name, grid=(R // tb,),
        in_specs=[pl.BlockSpec((n, tb, Wd), lambda i: (0, i, 0)), row, row, row],
        out_specs=[row, row, row, row], out_shape=[jax.ShapeDtypeStruct((R, Wd), F32)] * 4,
        compiler_params=_params(("arbitrary",)),
    )(parts, w, m, v)


def _row_tile(R, unit, cap):
    best = unit
    for t in range(unit, cap + 1, unit):
        if R % t == 0:
            best = t
    return best


def kernel(x, ffn1_w_gate, ffn1_w_up, ffn1_w_down, ln1_g, ln1_b, w_in, b_in, ret_gn_g, conv_k, conv_b, conv_ln_g, conv_ln_b, w_ret_o, w_conv_o, w_out, ln2_g, ln2_b, ffn2_w_gate, ffn2_w_up, ffn2_w_down, ln3_g, ln3_b, loss_target, m_ffn1_w_gate, m_ffn1_w_up, m_ffn1_w_down, m_ln1_g, m_ln1_b, m_w_in, m_b_in, m_ret_gn_g, m_conv_k, m_conv_b, m_conv_ln_g, m_conv_ln_b, m_w_ret_o, m_w_conv_o, m_w_out, m_ln2_g, m_ln2_b, m_ffn2_w_gate, m_ffn2_w_up, m_ffn2_w_down, m_ln3_g, m_ln3_b, v_ffn1_w_gate, v_ffn1_w_up, v_ffn1_w_down, v_ln1_g, v_ln1_b, v_w_in, v_b_in, v_ret_gn_g, v_conv_k, v_conv_b, v_conv_ln_g, v_conv_ln_b, v_w_ret_o, v_w_conv_o, v_w_out, v_ln2_g, v_ln2_b, v_ffn2_w_gate, v_ffn2_w_up, v_ffn2_w_down, v_ln3_g, v_ln3_b):
    given = dict(locals())
    wts = {n: given[n] for n in WEIGHTS}
    mom = {n: given['m_' + n] for n in WEIGHTS}
    var = {n: given['v_' + n] for n in WEIGHTS}

    def shard2d(a):
        return a.reshape(a.shape[-3] * a.shape[-2] if a.ndim == 4 else a.shape[-2], a.shape[-1])

    comm = _Comm({n: shard2d(wts[n]).astype(BF16) for n in BIG})
    P = {n: wts[n].reshape(1, -1) for n in SMALL}
    loss, grad_x, _, S = _local_step(x[0], loss_target[0], None, P, comm=comm)

    parts = comm.parts
    res = {}
    for n in BIG:
        rows, cols = parts[n].shape[1:]
        tb = rows if rows % 16 else _row_tile(rows, 16, max(16, (256 * 1024) // cols))
        res[n] = _adamw("adamw_" + n, parts[n], shard2d(wts[n]), shard2d(mom[n]), shard2d(var[n]), tb)

    vec_parts = _run_rider("gather_vector_grads", _GatherRider([S[n] for n in SMALL]))
    vec = _adamw_vectors(vec_parts, [P[n] for n in SMALL], [mom[n].reshape(1, -1) for n in SMALL],
                         [var[n].reshape(1, -1) for n in SMALL])
    for i, n in enumerate(SMALL):
        res[n] = vec[4 * i:4 * i + 4]

    loss = lax.psum(loss, ("x", "y", "c"))
    outs = [loss, grad_x[None]]
    for k in range(4):
        for n in WEIGHTS:
            outs.append(res[n][k].reshape(wts[n].shape))
    return tuple(outs)
```

```python
import functools
import math

import jax
import jax.numpy as jnp
from jax import lax
from jax.experimental import pallas as pl
from jax.experimental.pallas import tpu as pltpu

F32 = jnp.float32
BF16 = jnp.bfloat16

N_DEV = 8
LN_EPS = 1e-5
ALPHA = 2.0 ** 0.25
RET_DK = 128
RET_DV = 256
RET_CHUNK = 128
ROPE_BASE = 10000.0
CONV_WIDTH = 31
HALO = 32
ADAM_LR, ADAM_B1, ADAM_B2, ADAM_EPS, ADAM_WD, ADAM_STEP = 0.001, 0.9, 0.999, 1e-08, 0.01, 10
VMEM_LIMIT = 52 * 1024 * 1024
MESH = pl.DeviceIdType.MESH
ANY = pl.BlockSpec(memory_space=pl.ANY)

BIG = ['ffn1_w_gate', 'ffn1_w_up', 'ffn1_w_down', 'w_in', 'w_ret_o', 'w_conv_o', 'w_out',
       'ffn2_w_gate', 'ffn2_w_up', 'ffn2_w_down', 'conv_k']
COL_SHARDED = {'ffn1_w_gate', 'ffn1_w_up', 'w_in', 'ffn2_w_gate', 'ffn2_w_up', 'conv_k'}
SMALL = ['ln1_g', 'ln1_b', 'b_in', 'ret_gn_g', 'conv_b', 'conv_ln_g', 'conv_ln_b', 'ln2_g', 'ln2_b', 'ln3_g', 'ln3_b']
WEIGHTS = ['ffn1_w_gate', 'ffn1_w_up', 'ffn1_w_down', 'ln1_g', 'ln1_b', 'w_in', 'b_in', 'ret_gn_g', 'conv_k', 'conv_b',
           'conv_ln_g', 'conv_ln_b', 'w_ret_o', 'w_conv_o', 'w_out', 'ln2_g', 'ln2_b', 'ffn2_w_gate', 'ffn2_w_up',
           'ffn2_w_down', 'ln3_g', 'ln3_b']


def _params(sem=None):
    return pltpu.CompilerParams(dimension_semantics=sem, vmem_limit_bytes=VMEM_LIMIT)


def _sigmoid(x):
    return jax.nn.sigmoid(x)


def _dsilu(x, sg):
    return sg * (1.0 + x * (1.0 - sg))


def _fit(dim, want):
    if dim <= want:
        return dim
    return max(t for t in range(128, want + 1, 128) if dim % t == 0)


def _dot(a, b, ta=False, tb=False):
    dn = (((0,) if ta else (1,), (1,) if tb else (0,)), ((), ()))
    return lax.dot_general(a, b, dn, preferred_element_type=F32)


def _mm(name, As, Bs, prods, epi, out_dtypes, *, ta=False, tb=False, tm, tn, tk, extras=(), i_outer=True,
        b3=False, o3=False, rider=None):
    a0, b0 = As[0], Bs[0]
    M, K = (a0.shape[1], a0.shape[0]) if ta else a0.shape
    if b3:
        S, rows, cs = b0.shape
        N = rows if tb else S * cs
        assert K == (S * cs if tb else rows)
        tn, tk = (tn, cs) if tb else (cs, tk)
    else:
        N = b0.shape[0] if tb else b0.shape[1]
    tm, tn, tk = _fit(M, tm), _fit(N, tn), _fit(K, tk)
    assert M % tm == 0 and N % tn == 0 and K % tk == 0, (name, M, N, K, tm, tn, tk)
    gi, gj, gk = M // tm, N // tn, K // tk
    grid = (gi, gj, gk) if i_outer else (gj, gi, gk)

    def ij(g0, g1):
        return (g0, g1) if i_outer else (g1, g0)

    def amap(g0, g1, k):
        i, _ = ij(g0, g1)
        return (k, i) if ta else (i, k)

    def bmap(g0, g1, k):
        _, j = ij(g0, g1)
        return (j, k) if tb else (k, j)

    def bmap3(g0, g1, k):
        _, j = ij(g0, g1)
        return (k, j, 0) if tb else (j, k, 0)

    in_specs = [pl.BlockSpec((tk, tm) if ta else (tm, tk), amap) for _ in As]
    if b3:
        in_specs += [pl.BlockSpec((None, tn, tk) if tb else (None, tk, tn), bmap3) for _ in Bs]
    else:
        in_specs += [pl.BlockSpec((tn, tk) if tb else (tk, tn), bmap) for _ in Bs]
    args = list(As) + list(Bs)
    for arr, kind, coloff in extras:
        assert coloff % tn == 0
        off = coloff // tn
        if kind == 'mn':
            in_specs.append(pl.BlockSpec((tm, tn), lambda g0, g1, k, off=off: (ij(g0, g1)[0], ij(g0, g1)[1] + off)))
        else:
            in_specs.append(pl.BlockSpec((1, tn), lambda g0, g1, k, off=off: (0, ij(g0, g1)[1] + off)))
        args.append(arr)
    if o3:
        out_shape = [jax.ShapeDtypeStruct((gj, M, tn), dt) for dt in out_dtypes]
        out_specs = [pl.BlockSpec((None, tm, tn), lambda g0, g1, k: (ij(g0, g1)[1], ij(g0, g1)[0], 0))
                     for _ in out_dtypes]
    else:
        out_shape = [jax.ShapeDtypeStruct((M, N), dt) for dt in out_dtypes]
        out_specs = [pl.BlockSpec((tm, tn), lambda g0, g1, k: ij(g0, g1)) for _ in out_dtypes]
    n_a, n_b, n_e, n_o = len(As), len(Bs), len(extras), len(out_dtypes)
    n_p = len(prods) if gk > 1 else 0
    scratch = [pltpu.VMEM((tm, tn), F32) for _ in range(n_p)]
    if rider is not None:
        in_specs, out_specs = in_specs + [ANY] * len(rider.ins), out_specs + [ANY] * len(rider.out_shape)
        args, out_shape, scratch = args + rider.ins, out_shape + rider.out_shape, scratch + rider.scratch
    n_in, n_out = len(args), len(out_shape)

    def body(*refs):
        a_refs = refs[:n_a]
        b_refs = refs[n_a:n_a + n_b]
        e_refs = refs[n_a + n_b:n_a + n_b + n_e]
        o_refs = refs[n_in:n_in + n_o]
        acc_refs = refs[n_in + n_out:n_in + n_out + n_p]
        k = pl.program_id(2)
        if rider is not None:
            step = (pl.program_id(0) * grid[1] + pl.program_id(1)) * gk + k
            ride = (step, grid[0] * grid[1] * gk, refs[n_a + n_b + n_e:n_in], refs[n_in + n_o:n_in + n_out],
                    refs[n_in + n_out + n_p:])
            rider.begin(*ride)

        def finish(accs):
            for o, r in zip(o_refs, epi(accs, *[e[...].astype(F32) for e in e_refs])):
                o[...] = r.astype(o.dtype)

        if gk == 1:
            finish([functools.reduce(jnp.add, [_dot(a_refs[ai][...], b_refs[bi][...], ta, tb) for ai, bi in terms])
                    for terms in prods])
        else:
            @pl.when(k == 0)
            def _():
                for acc in acc_refs:
                    acc[...] = jnp.zeros_like(acc)

            for p, terms in enumerate(prods):
                for ai, bi in terms:
                    acc_refs[p][...] += _dot(a_refs[ai][...], b_refs[bi][...], ta, tb)

            @pl.when(k == gk - 1)
            def _():
                finish([acc[...] for acc in acc_refs])

        if rider is not None:
            rider.end(*ride)

    aliases = {} if rider is None else {n_a + n_b + n_e + p: n_o + o for p, o in rider.aliases.items()}
    res = pl.pallas_call(
        body, name=name, grid=grid, in_specs=in_specs, out_specs=out_specs, out_shape=out_shape,
        scratch_shapes=scratch, input_output_aliases=aliases,
        compiler_params=_params(("arbitrary", "arbitrary", "arbitrary")),
    )(*args)
    if rider is not None:
        rider.results = res[n_o:]
    return res[:n_o]


def _rows(name, fn, ins, outs, *, T, tb):
    tb = min(tb, T)
    assert T % tb == 0
    in_specs, args = [], []
    for arr, kind, width, cb in ins:
        if kind == 'r':
            in_specs.append(pl.BlockSpec((tb, width), lambda i, cb=cb: (i, cb)))
        else:
            in_specs.append(pl.BlockSpec((1, width), lambda i, cb=cb: (0, cb)))
        args.append(arr)
    out_shape, out_specs = [], []
    for kind, width, dtype in outs:
        if kind == 'r':
            out_shape.append(jax.ShapeDtypeStruct((T, width), dtype))
            out_specs.append(pl.BlockSpec((tb, width), lambda i: (i, 0)))
        elif kind == 'c':
            out_shape.append(jax.ShapeDtypeStruct((T, 1), dtype))
            out_specs.append(pl.BlockSpec((tb, 1), lambda i: (i, 0)))
        else:
            out_shape.append(jax.ShapeDtypeStruct((1, width), F32))
            out_specs.append(pl.BlockSpec((1, width), lambda i: (0, 0)))
    n_in = len(ins)

    def body(*refs):
        i = pl.program_id(0)
        vals = fn(*[r[...] for r in refs[:n_in]])
        for (kind, _, _), o, v in zip(outs, refs[n_in:], vals):
            if kind == 'a':
                @pl.when(i == 0)
                def _(o=o):
                    o[...] = jnp.zeros_like(o)

                o[...] += v
            else:
                o[...] = v.astype(o.dtype)

    return pl.pallas_call(
        body, name=name, grid=(T // tb,), in_specs=in_specs, out_specs=out_specs, out_shape=out_shape,
        compiler_params=_params(("arbitrary",)),
    )(*args)


def _colsum(v):
    return jnp.sum(v, axis=0, keepdims=True)


def _ln_stats(z):
    mu = jnp.mean(z, axis=-1, keepdims=True)
    d = z - mu
    var = jnp.mean(d * d, axis=-1, keepdims=True)
    rstd = lax.rsqrt(var + LN_EPS)
    return d * rstd, rstd


def _ln_bwd_math(dy, xhat, rstd, g):
    dxh = dy * g
    m1 = jnp.mean(dxh, axis=-1, keepdims=True)
    m2 = jnp.mean(dxh * xhat, axis=-1, keepdims=True)
    return rstd * (dxh - m1 - xhat * m2)


def _ln_fwd(name, z, g, b, T, D):
    def fn(z, g, b):
        xhat, rstd = _ln_stats(z)
        y = xhat * g + b
        return [y, y, xhat, rstd]

    return _rows(name, fn, [(z, 'r', D, 0), (g, 'v', D, 0), (b, 'v', D, 0)],
                 [('r', D, F32), ('r', D, BF16), ('r', D, F32), ('c', 1, F32)], T=T, tb=512)


def _ln_bwd(name, dy, xhat, rstd, g, scale, T, D):
    def fn(dy, xhat, rstd, g):
        dz = _ln_bwd_math(dy, xhat, rstd, g)
        return [dz, dz * scale, _colsum(dy * xhat), _colsum(dy)]

    return _rows(name, fn, [(dy, 'r', D, 0), (xhat, 'r', D, 0), (rstd, 'r', 1, 0), (g, 'v', D, 0)],
                 [('r', D, F32), ('r', D, BF16), ('a', D, F32), ('a', D, F32)], T=T, tb=512)


def _ln_loss_bwd(name, z, g, b, tgt, T, D):
    def fn(z, g, b, tgt):
        xhat, rstd = _ln_stats(z)
        err = xhat * g + b - tgt
        row_loss = 0.5 * jnp.mean(err * err, axis=-1, keepdims=True)
        loss = jnp.broadcast_to(jnp.sum(row_loss, axis=0, keepdims=True), (1, 128))
        dy = err * (1.0 / D)
        dz = _ln_bwd_math(dy, xhat, rstd, g)
        return [dz, dz * 0.5, _colsum(dy * xhat), _colsum(dy), loss]

    return _rows(name, fn, [(z, 'r', D, 0), (g, 'v', D, 0), (b, 'v', D, 0), (tgt, 'r', D, 0)],
                 [('r', D, F32), ('r', D, BF16), ('a', D, F32), ('a', D, F32), ('a', 128, F32)], T=T, tb=512)


class _Net:
    def __init__(self, comm, G):
        self.comm, self.G = comm, G

    def gather(self, names, part=None):
        return self.comm.gather(names, part) if self.comm else None

    def exchange(self, names, part=None):
        return self.comm.exchange(names, self.G, part) if self.comm else None

    def done(self, rider):
        if rider is not None:
            self.comm.collect(rider)


def _ffn_fwd(tag, xb, x, W, names, net, rider=None, rider_down=None):
    def epi_gu(accs):
        a, b = accs
        return [a, b, a * _sigmoid(a) * b]

    ng, nu, nd = names
    a, b, s = _mm(tag + "_gate_up", [xb], [W[ng], W[nu]], [[(0, 0)], [(0, 1)]], epi_gu, [BF16, BF16, BF16],
                  tm=2048, tn=256, tk=1024, rider=rider)
    net.done(rider)

    def epi_down(accs, xres):
        return [ALPHA * xres + 0.5 * accs[0]]

    rider_down = rider_down() if rider_down else None
    (z,) = _mm(tag + "_down", [s], [W[nd]], [[(0, 0)]], epi_down, [F32], tm=1024, tn=1024, tk=1408,
               extras=[(x, 'mn', 0)], rider=rider_down)
    net.done(rider_down)
    return a, b, s, z


def _ffn_bwd(tag, dzh, dz, xb, a, b, s, W, names, gdt, G, net, ride, pre=(None, None)):
    ng, nu, nd = names

    def epi_ds(accs, a, b):
        ds = accs[0]
        sg = _sigmoid(a)
        return [ds * b * _dsilu(a, sg), ds * a * sg]

    rider = pre[0]() if pre[0] else None
    da, db = _mm(tag + "_ds", [dzh], [W[nd]], [[(0, 0)]], epi_ds, [BF16, BF16], tb=True, tm=2048, tn=256, tk=1024,
                 extras=[(a, 'mn', 0), (b, 'mn', 0)], rider=rider)
    net.done(rider)
    ident = lambda accs: accs
    rider = pre[1]() if pre[1] else None
    (G[nd],) = _mm(tag + "_dwd", [s], [dzh], [[(0, 0)]], ident, [gdt], ta=True, tm=1408, tn=1024, tk=1024,
                   rider=rider)
    net.done(rider)
    if ride:
        rider = net.exchange([nd])
        (G[ng],) = _mm(tag + "_dwg", [xb], [da], [[(0, 0)]], ident, [gdt], ta=True, tm=1024, tn=1408, tk=1024,
                       rider=rider)
        net.done(rider)
        rider = net.exchange([ng])
        (G[nu],) = _mm(tag + "_dwu", [xb], [db], [[(0, 0)]], ident, [gdt], ta=True, tm=1024, tn=1408, tk=1024,
                       rider=rider)
        net.done(rider)
    else:
        G[ng], G[nu] = _mm(tag + "_dwgu", [xb], [da, db], [[(0, 0)], [(0, 1)]], ident, [gdt, gdt], ta=True,
                           tm=1024, tn=1408, tk=1024)

    def epi_dx(accs, dzres):
        return [ALPHA * dzres + accs[0]]

    rider = net.exchange([nu]) if ride else None
    (dx,) = _mm(tag + "_dx", [da, db], [W[ng], W[nu]], [[(0, 0), (1, 1)]], epi_dx, [F32], tb=True,
                tm=1024, tn=1024, tk=1408, extras=[(dz, 'mn', 0)], rider=rider)
    net.done(rider)
    return dx


def _ret_tables(H, T):
    C = RET_CHUNK
    log_g = jnp.log(1.0 - jnp.exp2(-5.0 - jnp.arange(H, dtype=F32)))
    idx = jnp.arange(C, dtype=F32)
    diff = idx[:, None] - idx[None, :]
    dm = jnp.where(diff[None] >= 0, jnp.exp(jnp.maximum(diff, 0.0)[None] * log_g[:, None, None]), 0.0)
    xi = jnp.exp((idx[None, :] + 1.0) * log_g[:, None])[:, :, None]
    zeta = jnp.exp((C - 1.0 - idx)[None, :] * log_g[:, None])[:, :, None]
    gc = jnp.broadcast_to(jnp.exp(C * log_g)[:, None, None], (H, 1, RET_DV))
    half = RET_DK // 2
    freqs = ROPE_BASE ** (-jnp.arange(half, dtype=F32) / half)
    ang = jnp.arange(T, dtype=F32)[:, None] * freqs[None, :]
    cos, sin = jnp.cos(ang), jnp.sin(ang)
    cosf = jnp.concatenate([cos, cos], axis=1)
    sins = jnp.concatenate([-sin, sin], axis=1)
    return dm, xi, zeta, gc, cosf, sins


def _rot(x, cosf, sins):
    return x * cosf + pltpu.roll(x, RET_DK // 2, 1) * sins


def _rot_bwd(dy, cosf, sins):
    return dy * cosf + pltpu.roll(dy * sins, RET_DK // 2, 1)


RET_HB = 8


def _ret_specs(H, HB, rev, NC):
    C, G = RET_CHUNK, H // HB
    nn = (lambda n: NC - 1 - n) if rev else (lambda n: n)
    return [
        pl.BlockSpec((C, HB * RET_DK), lambda h, n: (nn(n), h)),
        pl.BlockSpec((C, HB * RET_DK), lambda h, n: (nn(n), G + h)),
        pl.BlockSpec((C, HB * RET_DV), lambda h, n: (nn(n), G + h)),
        pl.BlockSpec((C, HB * RET_DV), lambda h, n: (nn(n), 2 * G + h)),
        pl.BlockSpec((C, RET_DK), lambda h, n: (nn(n), 0)),
        pl.BlockSpec((C, RET_DK), lambda h, n: (nn(n), 0)),
        pl.BlockSpec((1, HB * RET_DV), lambda h, n: (0, h)),
        pl.BlockSpec((HB, C, C), lambda h, n: (h, 0, 0)),
        pl.BlockSpec((HB, C, 1), lambda h, n: (h, 0, 0)),
        pl.BlockSpec((HB, C, 1), lambda h, n: (h, 0, 0)),
        pl.BlockSpec((HB, 1, RET_DV), lambda h, n: (h, 0, 0)),
    ]


def _ret_fwd(proj, gn_g, tabs, H, T, rider=None):
    C, NC = RET_CHUNK, T // RET_CHUNK
    HB = min(RET_HB, H)
    dm, xi, zeta, gc, cosf, sins = tabs
    scale = RET_DK ** -0.5

    def body(q_ref, k_ref, v_ref, g_ref, cos_ref, sin_ref, gn_ref, dm_ref, xi_ref, zt_ref, gc_ref,
             r_ref, ri_ref, st_ref, state):
        @pl.when(pl.program_id(1) == 0)
        def _():
            state[...] = jnp.zeros_like(state)

        cs, sn = cos_ref[...], sin_ref[...]
        for hh in range(HB):
            qk = slice(hh * RET_DK, (hh + 1) * RET_DK)
            vv = slice(hh * RET_DV, (hh + 1) * RET_DV)
            qr = _rot(q_ref[:, qk].astype(F32), cs, sn) * scale
            kr = _rot(k_ref[:, qk].astype(F32), cs, sn)
            qb, kb, vb = qr.astype(BF16), kr.astype(BF16), v_ref[:, vv].astype(BF16)
            st = state[hh]
            stb = st.astype(BF16)
            s = _dot(qb, kb, tb=True) * dm_ref[hh]
            r = _dot(s.astype(BF16), vb) + _dot(qb, stb) * xi_ref[hh]
            st_ref[hh] = stb
            state[hh] = gc_ref[hh] * st + _dot((kr * zt_ref[hh]).astype(BF16), vb, ta=True)
            rhat, _ = _ln_stats(r)
            g = g_ref[:, vv].astype(F32)
            r_ref[:, vv] = r
            ri_ref[:, vv] = (g * _sigmoid(g) * (rhat * gn_ref[:, vv])).astype(BF16)

    VW = H * RET_DV
    return _hosted_call(
        body, rider, name="ret_fwd", grid=(H // HB, NC), in_specs=_ret_specs(H, HB, False, NC),
        out_specs=[pl.BlockSpec((C, HB * RET_DV), lambda h, n: (n, h)),
                   pl.BlockSpec((C, HB * RET_DV), lambda h, n: (n, h)),
                   pl.BlockSpec((HB, None, RET_DK, RET_DV), lambda h, n: (h, n, 0, 0))],
        out_shape=[jax.ShapeDtypeStruct((T, VW), F32), jax.ShapeDtypeStruct((T, VW), BF16),
                   jax.ShapeDtypeStruct((H, NC, RET_DK, RET_DV), BF16)],
        scratch=[pltpu.VMEM((HB, RET_DK, RET_DV), F32)],
        args=[proj, proj, proj, proj, cosf, sins, gn_g, dm, xi, zeta, gc])


def _hosted_call(body, rider, *, name, grid, in_specs, out_specs, out_shape, scratch, args):
    n_in, n_out, n_scr = len(args), len(out_shape), len(scratch)
    if rider is None:
        hosted = body
    else:
        n_ri, n_ro = len(rider.ins), len(rider.out_shape)
        in_specs, out_specs = in_specs + [ANY] * n_ri, out_specs + [ANY] * n_ro
        args, out_shape, scratch = args + rider.ins, out_shape + rider.out_shape, scratch + rider.scratch

        def hosted(*refs):
            o0, s0 = n_in + n_ri, n_in + n_ri + n_out + n_ro
            step = pl.program_id(0) * grid[1] + pl.program_id(1)
            ride = (step, grid[0] * grid[1], refs[n_in:o0], refs[o0 + n_out:s0], refs[s0 + n_scr:])
            rider.begin(*ride)
            body(*refs[:n_in], *refs[o0:o0 + n_out], *refs[s0:s0 + n_scr])
            rider.end(*ride)

    aliases = {} if rider is None else {n_in + p: n_out + o for p, o in rider.aliases.items()}
    res = pl.pallas_call(
        hosted, name=name, grid=grid, in_specs=in_specs, out_specs=out_specs, out_shape=out_shape,
        scratch_shapes=scratch, input_output_aliases=aliases, compiler_params=_params(("arbitrary", "arbitrary")),
    )(*args)
    if rider is not None:
        rider.results = res[n_out:]
    return res[:n_out]


def _ret_bwd(dri, r, states, proj, gn_g, tabs, H, T, in_w, rider=None):
    C, NC = RET_CHUNK, T // RET_CHUNK
    HB = min(RET_HB, H)
    dm, xi, zeta, gc, cosf, sins = tabs
    scale = RET_DK ** -0.5

    def body(q_ref, k_ref, v_ref, g_ref, cos_ref, sin_ref, gn_ref, dm_ref, xi_ref, zt_ref, gc_ref,
             dri_ref, r_ref, st_ref, dp_ref, dgn_ref, dstate):
        @pl.when(pl.program_id(1) == 0)
        def _():
            dstate[...] = jnp.zeros_like(dstate)
            dgn_ref[...] = jnp.zeros_like(dgn_ref)

        cs, sn = cos_ref[...], sin_ref[...]
        for hh in range(HB):
            qk = slice(hh * RET_DK, (hh + 1) * RET_DK)
            vv = slice(hh * RET_DV, (hh + 1) * RET_DV)
            qr = _rot(q_ref[:, qk].astype(F32), cs, sn) * scale
            kr = _rot(k_ref[:, qk].astype(F32), cs, sn)
            qb, kb, vb = qr.astype(BF16), kr.astype(BF16), v_ref[:, vv].astype(BF16)
            xi_c, zt_c, dmask = xi_ref[hh], zt_ref[hh], dm_ref[hh]
            rhat, rstd = _ln_stats(r_ref[:, vv])
            g, gn, dpre = g_ref[:, vv].astype(F32), gn_ref[:, vv], dri_ref[:, vv]
            sg = _sigmoid(g)
            dp_ref[:, 2 * QW + VW + hh * RET_DV:2 * QW + VW + (hh + 1) * RET_DV] = (
                dpre * (rhat * gn) * _dsilu(g, sg)).astype(BF16)
            drn = dpre * (g * sg)
            dgn_ref[:, vv] += _colsum(drn * rhat)
            drb = _ln_bwd_math(drn, rhat, rstd, gn).astype(BF16)
            ds1 = dstate[hh]
            ds1b = ds1.astype(BF16)
            sb = (_dot(qb, kb, tb=True) * dmask).astype(BF16)
            kzb = (kr * zt_c).astype(BF16)
            dp_ref[:, 2 * QW + hh * RET_DV:2 * QW + (hh + 1) * RET_DV] = (
                _dot(sb, drb, ta=True) + _dot(kzb, ds1b)).astype(BF16)
            dsb = (_dot(drb, vb, tb=True) * dmask).astype(BF16)
            dq = _dot(dsb, kb) + _dot(drb, st_ref[hh], tb=True) * xi_c
            dk = _dot(dsb, qb, ta=True) + _dot(vb, ds1b, tb=True) * zt_c
            dstate[hh] = gc_ref[hh] * ds1 + _dot((qr * xi_c).astype(BF16), drb, ta=True)
            dp_ref[:, qk] = _rot_bwd(dq * scale, cs, sn).astype(BF16)
            dp_ref[:, QW + hh * RET_DK:QW + (hh + 1) * RET_DK] = _rot_bwd(dk, cs, sn).astype(BF16)

    VW, QW = H * RET_DV, H * RET_DK
    rv = lambda n: NC - 1 - n
    in_specs = _ret_specs(H, HB, True, NC) + [
        pl.BlockSpec((C, HB * RET_DV), lambda h, n: (rv(n), h)),
        pl.BlockSpec((C, HB * RET_DV), lambda h, n: (rv(n), h)),
        pl.BlockSpec((HB, None, RET_DK, RET_DV), lambda h, n: (h, rv(n), 0, 0)),
    ]
    assert HB == H
    return _hosted_call(
        body, rider, name="ret_bwd", grid=(1, NC), in_specs=in_specs,
        out_specs=[pl.BlockSpec((C, 2 * QW + 2 * VW), lambda h, n: (rv(n), 0)),
                   pl.BlockSpec((1, VW), lambda h, n: (0, 0))],
        out_shape=[jax.ShapeDtypeStruct((T, in_w), BF16), jax.ShapeDtypeStruct((1, VW), F32)],
        scratch=[pltpu.VMEM((HB, RET_DK, RET_DV), F32)],
        args=[proj, proj, proj, proj, cosf, sins, gn_g, dm, xi, zeta, gc, dri, r, states])


CONV_CW = 128
CONV_TB = 512


def _conv_fwd(proj, kpad, bias, off_a, CC, T):
    tb, cw = min(CONV_TB, T), CONV_CW
    hb = tb // HALO
    ca, cb = off_a // cw, (off_a + CC) // cw

    def body(a_ref, b_ref, ap_ref, bp_ref, k_ref, bias_ref, u1_ref, win):
        i = pl.program_id(0)
        keep = (i > 0).astype(F32)
        win[0:HALO, :] = ap_ref[...].astype(F32) * _sigmoid(bp_ref[...].astype(F32)) * keep
        win[HALO:, :] = a_ref[...].astype(F32) * _sigmoid(b_ref[...].astype(F32))
        acc = jnp.broadcast_to(bias_ref[...], (tb, cw))
        for w in range(CONV_WIDTH):
            acc = acc + k_ref[w:w + 1, :] * win[pl.ds(HALO - (CONV_WIDTH - 1) + w, tb), :]
        u1_ref[...] = acc

    prev = lambda i: jnp.maximum(i * hb - 1, 0)
    return pl.pallas_call(
        body, name="conv_fwd", grid=(T // tb, CC // cw),
        in_specs=[pl.BlockSpec((tb, cw), lambda i, c: (i, ca + c)),
                  pl.BlockSpec((tb, cw), lambda i, c: (i, cb + c)),
                  pl.BlockSpec((HALO, cw), lambda i, c: (prev(i), ca + c)),
                  pl.BlockSpec((HALO, cw), lambda i, c: (prev(i), cb + c)),
                  pl.BlockSpec((HALO, cw), lambda i, c: (0, c)),
                  pl.BlockSpec((1, cw), lambda i, c: (0, c))],
        out_specs=pl.BlockSpec((tb, cw), lambda i, c: (i, c)),
        out_shape=jax.ShapeDtypeStruct((T, CC), F32),
        scratch_shapes=[pltpu.VMEM((tb + HALO, cw), F32)],
        compiler_params=_params(("arbitrary", "arbitrary")),
    )(proj, proj, proj, proj, kpad, bias)


def _conv_bwd(du1, proj, kpad, off_a, CC, T, rider=None):
    tb, cw = min(CONV_TB, T), CONV_CW
    hb = tb // HALO
    nt = T // tb
    ca, cb = off_a // cw, (off_a + CC) // cw

    def body(d_ref, dn_ref, a_ref, b_ref, ap_ref, bp_ref, k_ref, da_ref, db_ref, dk_ref, winu, wind):
        i = pl.program_id(1)
        a, b = a_ref[...].astype(F32), b_ref[...].astype(F32)
        sgb = _sigmoid(b)
        winu[0:HALO, :] = ap_ref[...].astype(F32) * _sigmoid(bp_ref[...].astype(F32)) * (i > 0).astype(F32)
        winu[HALO:, :] = a * sgb
        d = d_ref[...]
        wind[0:tb, :] = d
        wind[tb:, :] = dn_ref[...] * (i < nt - 1).astype(F32)

        @pl.when(i == 0)
        def _():
            dk_ref[...] = jnp.zeros_like(dk_ref)

        du0 = jnp.zeros((tb, cw), F32)
        for w in range(CONV_WIDTH):
            du0 = du0 + k_ref[w:w + 1, :] * wind[pl.ds(CONV_WIDTH - 1 - w, tb), :]
            dk_ref[w:w + 1, :] += _colsum(winu[pl.ds(HALO - (CONV_WIDTH - 1) + w, tb), :] * d)
        da_ref[...] = (du0 * sgb).astype(BF16)
        db_ref[...] = (du0 * a * sgb * (1.0 - sgb)).astype(BF16)

    prev = lambda i: jnp.maximum(i * hb - 1, 0)
    nxt = lambda i: jnp.minimum((i + 1) * hb, T // HALO - 1)
    return _hosted_call(
        body, rider, name="conv_bwd", grid=(CC // cw, nt),
        in_specs=[pl.BlockSpec((tb, cw), lambda c, i: (i, c)),
                  pl.BlockSpec((HALO, cw), lambda c, i: (nxt(i), c)),
                  pl.BlockSpec((tb, cw), lambda c, i: (i, ca + c)),
                  pl.BlockSpec((tb, cw), lambda c, i: (i, cb + c)),
                  pl.BlockSpec((HALO, cw), lambda c, i: (prev(i), ca + c)),
                  pl.BlockSpec((HALO, cw), lambda c, i: (prev(i), cb + c)),
                  pl.BlockSpec((HALO, cw), lambda c, i: (0, c))],
        out_specs=[pl.BlockSpec((tb, cw), lambda c, i: (i, c)),
                   pl.BlockSpec((tb, cw), lambda c, i: (i, c)),
                   pl.BlockSpec((HALO, cw), lambda c, i: (0, c))],
        out_shape=[jax.ShapeDtypeStruct((T, CC), BF16), jax.ShapeDtypeStruct((T, CC), BF16),
                   jax.ShapeDtypeStruct((HALO, CC), F32)],
        scratch=[pltpu.VMEM((tb + HALO, cw), F32), pltpu.VMEM((tb + HALO, cw), F32)],
        args=[du1, du1, proj, proj, proj, proj, kpad])


FFN1 = ('ffn1_w_gate', 'ffn1_w_up', 'ffn1_w_down')
FFN2 = ('ffn2_w_gate', 'ffn2_w_up', 'ffn2_w_down')


def _local_step(x, tgt, W, P, gdt=BF16, comm=None):
    T, D = x.shape
    G = {}
    net = _Net(comm, G)
    if comm is not None:
        W = comm.W
        first = net.gather(['ffn1_w_gate', 'ffn1_w_up'])
        _run_rider("gather_ffn1_in", first)
        net.done(first)
    VW = P['ret_gn_g'].shape[1]
    H = VW // RET_DV
    QW = H * RET_DK
    CC = P['conv_b'].shape[1]
    off_glu = 2 * QW + 2 * VW
    off_gate = off_glu + 2 * CC
    ident = lambda accs: accs
    xb = x.astype(BF16)

    a1, b1, s1, z1 = _ffn_fwd("ffn1", xb, x, W, FFN1, net,
                              rider=net.gather(['ffn1_w_down', 'w_in'], {'w_in': (0, D // 2, False)}),
                              rider_down=lambda: net.gather(['w_in'], {'w_in': (D // 2, D // 2, True)}))
    x1, x1b, xh1, rs1 = _ln_fwd("ln1", z1, P['ln1_g'], P['ln1_b'], T, D)

    rest = net.gather(['conv_k', 'w_ret_o', 'w_conv_o', 'w_out'])
    (proj,) = _mm("w_in", [x1b], [W['w_in']], [[(0, 0)]], lambda accs, bias: [accs[0] + bias], [BF16],
                  tm=2048, tn=0, tk=1024, extras=[(P['b_in'], 'n', 0)], i_outer=False, b3=True, rider=rest)
    net.done(rest)
    tabs = _ret_tables(H, T)
    rider = net.gather(list(FFN2))
    r, ret_in, states = _ret_fwd(proj, P['ret_gn_g'], tabs, H, T, rider=rider)
    net.done(rider)
    kpad = jnp.pad(W['conv_k'].astype(F32), ((0, HALO - CONV_WIDTH), (0, 0)))
    u1 = _conv_fwd(proj, kpad, P['conv_b'], off_glu, CC, T)

    def conv_ln(u1, g, b):
        xhat, rstd = _ln_stats(u1)
        u2 = xhat * g + b
        return [xhat, rstd, u2 * _sigmoid(u2)]

    xhc, rsc, u3 = _rows("conv_ln", conv_ln, [(u1, 'r', CC, 0), (P['conv_ln_g'], 'v', CC, 0), (P['conv_ln_b'], 'v', CC, 0)],
                         [('r', CC, F32), ('c', 1, F32), ('r', CC, BF16)], T=T, tb=512)
    (ret_out,) = _mm("ret_o", [ret_in], [W['w_ret_o']], [[(0, 0)]], ident, [F32], tm=1024, tn=1024, tk=2048)

    def epi_merge(accs, ret_out, gr, gc):
        conv_out = accs[0]
        return [conv_out, _sigmoid(gr) * ret_out + _sigmoid(gc) * conv_out]

    conv_out, merged = _mm("conv_o_merge", [u3], [W['w_conv_o']], [[(0, 0)]], epi_merge, [F32, BF16],
                           tm=512, tn=D, tk=1024,
                           extras=[(ret_out, 'mn', 0), (proj, 'mn', off_gate), (proj, 'mn', off_gate + D)])
    (z2,) = _mm("w_out", [merged], [W['w_out']], [[(0, 0)]], lambda accs, xr: [ALPHA * xr + accs[0]], [F32],
                tm=1024, tn=1024, tk=1024, extras=[(x1, 'mn', 0)])
    x2, x2b, xh2, rs2 = _ln_fwd("ln2", z2, P['ln2_g'], P['ln2_b'], T, D)
    a2, b2, s2, z3 = _ffn_fwd("ffn2", x2b, x2, W, FFN2, net)
    dz3, dz3h, g_ln3_g, g_ln3_b, loss = _ln_loss_bwd("ln3_loss", z3, P['ln3_g'], P['ln3_b'], tgt, T, D)

    S = {'ln3_g': g_ln3_g, 'ln3_b': g_ln3_b}
    dy2 = _ffn_bwd("ffn2b", dz3h, dz3, x2b, a2, b2, s2, W, FFN2, gdt, G, net, False)
    dz2, dz2b, S['ln2_g'], S['ln2_b'] = _ln_bwd("ln2b", dy2, xh2, rs2, P['ln2_g'], 1.0, T, D)

    (G['w_out'],) = _mm("d_w_out", [merged], [dz2b], [[(0, 0)]], ident, [gdt], ta=True, tm=1024, tn=1024, tk=1024)

    def epi_dmerge(accs, ret_out, conv_out, gr, gc):
        dm_ = accs[0]
        sr, sc = _sigmoid(gr), _sigmoid(gc)
        return [dm_ * sr, dm_ * sc, dm_ * ret_out * sr * (1.0 - sr), dm_ * conv_out * sc * (1.0 - sc)]

    dret_out, dconv_out, dgate_r, dgate_c = _mm(
        "d_merge", [dz2b], [W['w_out']], [[(0, 0)]], epi_dmerge, [BF16, BF16, BF16, BF16], tb=True,
        tm=512, tn=D, tk=1024,
        extras=[(ret_out, 'mn', 0), (conv_out, 'mn', 0), (proj, 'mn', off_gate), (proj, 'mn', off_gate + D)])
    (G['w_ret_o'],) = _mm("d_w_ret_o", [ret_in], [dret_out], [[(0, 0)]], ident, [gdt], ta=True, tm=1024, tn=1024, tk=1024)
    (G['w_conv_o'],) = _mm("d_w_conv_o", [u3], [dconv_out], [[(0, 0)]], ident, [gdt], ta=True, tm=1024, tn=1024, tk=1024)
    (dri,) = _mm("d_ret_in", [dret_out], [W['w_ret_o']], [[(0, 0)]], ident, [F32], tb=True, tm=1024, tn=1024, tk=1024)
    rider = net.exchange(list(FFN2))
    dproj, S['ret_gn_g'] = _ret_bwd(dri, r, states, proj, P['ret_gn_g'], tabs, H, T, proj.shape[1], rider=rider)
    net.done(rider)

    def epi_du2(accs, xhat, g, b):
        u2 = xhat * g + b
        return [accs[0] * _dsilu(u2, _sigmoid(u2))]

    (du2,) = _mm("d_u3", [dconv_out], [W['w_conv_o']], [[(0, 0)]], epi_du2, [F32], tb=True, tm=512, tn=CC, tk=1024,
                 extras=[(xhc, 'mn', 0), (P['conv_ln_g'], 'n', 0), (P['conv_ln_b'], 'n', 0)])

    def conv_ln_bwd(du2, xhat, rstd, g):
        du1 = _ln_bwd_math(du2, xhat, rstd, g)
        return [du1, _colsum(du2 * xhat), _colsum(du2), _colsum(du1)]

    du1, S['conv_ln_g'], S['conv_ln_b'], S['conv_b'] = _rows(
        "conv_ln_bwd", conv_ln_bwd, [(du2, 'r', CC, 0), (xhc, 'r', CC, 0), (rsc, 'r', 1, 0), (P['conv_ln_g'], 'v', CC, 0)],
        [('r', CC, F32), ('a', CC, F32), ('a', CC, F32), ('a', CC, F32)], T=T, tb=512)
    rider = net.exchange(['w_out', 'w_ret_o', 'w_conv_o'])
    dglu_a, dglu_b, dkpad = _conv_bwd(du1, proj, kpad, off_glu, CC, T, rider=rider)
    net.done(rider)
    G['conv_k'] = dkpad[:CONV_WIDTH].astype(gdt)

    for off, piece in ((off_glu, dglu_a), (off_glu + CC, dglu_b), (off_gate, dgate_r), (off_gate + D, dgate_c)):
        dproj = lax.dynamic_update_slice(dproj, piece, (0, off))
    IN_W = dproj.shape[1]
    (S['b_in'],) = _rows("d_b_in", lambda d: [_colsum(d.astype(F32))], [(dproj, 'r', IN_W, 0)], [('a', IN_W, F32)],
                         T=T, tb=256)
    (G['w_in'],) = _mm("d_w_in", [x1b], [dproj], [[(0, 0)]], ident, [gdt], ta=True, o3=True,
                       tm=1024, tn=W['w_in'].shape[2], tk=1024)
    cuts = [0, (9 * D) // 16, (53 * D) // 64, D]
    w_in_rows = [{'w_in': (cuts[i], cuts[i + 1] - cuts[i], i == 2)} for i in range(3)]
    rider = net.exchange(['w_in', 'conv_k'], w_in_rows[0])
    (dy1,) = _mm("d_x1", [dproj], [W['w_in']], [[(0, 0)]], lambda accs, dzr: [ALPHA * dzr + accs[0]], [F32], tb=True,
                 b3=True, tm=1024, tn=1024, tk=0, extras=[(dz2, 'mn', 0)], rider=rider)
    net.done(rider)
    dz1, dz1h, S['ln1_g'], S['ln1_b'] = _ln_bwd("ln1b", dy1, xh1, rs1, P['ln1_g'], 0.5, T, D)
    grad_x = _ffn_bwd("ffn1b", dz1h, dz1, xb, a1, b1, s1, W, FFN1, gdt, G, net, True,
                      pre=(lambda: net.exchange(['w_in'], w_in_rows[1]), lambda: net.exchange(['w_in'], w_in_rows[2])))
    return loss[0, 0], grad_x, G, S


def _coords():
    return lax.axis_index("x"), lax.axis_index("y"), lax.axis_index("c")


def _flip(k, x, y, c):
    return (1 - x if k & 4 else x, 1 - y if k & 2 else y, 1 - c if k & 1 else c)


def _lin(p):
    return 4 * p[0] + 2 * p[1] + p[2]


class _Rider:
    def __init__(self, ins, out_shape, rows=None, fill=None):
        nb = len(ins)
        self.rows = rows or [None] * nb
        fill = fill or [None] * nb
        self.aliases = {nb + i: w for i, w in enumerate(w for w in range(nb) if fill[w] is not None)}
        self.ins = list(ins) + [f for f in fill if f is not None]
        self.out_shape, self.results = list(out_shape), None
        self.scratch = [pltpu.SemaphoreType.DMA((7 * nb,)), pltpu.SemaphoreType.DMA((7 * nb,)),
                        pltpu.SemaphoreType.DMA((nb,))]

    def span(self, w, ref, *slot):
        if self.rows[w] is None:
            return ref.at[slot] if slot else ref
        return ref.at[(*slot, pl.ds(*self.rows[w]))]

    def begin(self, step, n_steps, ins, outs, sems):
        @pl.when(step == 0)
        def _():
            self.start(ins, outs, sems)

        @pl.when(step == n_steps - 1)
        def _():
            self.mid(ins, outs, sems)

    def end(self, step, n_steps, ins, outs, sems):
        @pl.when(step == n_steps - 1)
        def _():
            self.finish(ins, outs, sems)

    def mid(self, ins, outs, sems):
        pass


class _GatherRider(_Rider):
    def __init__(self, blks, rows=None, fill=None):
        super().__init__(blks, [jax.ShapeDtypeStruct((N_DEV,) + b.shape, b.dtype) for b in blks], rows, fill)

    def _copies(self, x_refs, out_refs, sems):
        nb = len(self.out_shape)
        send_sems, recv_sems, local_sems = sems
        x, y, c = _coords()
        me, sibling = (x, y, c), (x, y, 1 - c)
        chips = [_flip(4, x, y, c), _flip(2, x, y, c), _flip(6, x, y, c)]
        own = [self.span(w, x_refs[w]) for w in range(nb)]

        def copy(k, w, block, to, src=None):
            slot = self.span(w, out_refs[w], _lin(block))
            return pltpu.make_async_remote_copy(
                src_ref=slot if src is None else src, dst_ref=slot, send_sem=send_sems.at[k * nb + w],
                recv_sem=recv_sems.at[k * nb + w], device_id=to, device_id_type=MESH)

        mines = [pltpu.make_async_copy(own[w], self.span(w, out_refs[w], _lin(me)), local_sems.at[w])
                 for w in range(nb)]
        first = [copy(0, w, me, sibling, src=own[w]) for w in range(nb)]
        first += [copy(1 + j, w, me, chip, src=own[w]) for w in range(nb) for j, chip in enumerate(chips)]
        landed = [(copy(1 + j, w, chip, me), copy(4 + j, w, chip, sibling)) for w in range(nb) for j, chip in enumerate(chips)]
        from_sibling = [copy(0, w, sibling, me) for w in range(nb)]
        from_sibling += [copy(4 + j, w, (chip[0], chip[1], 1 - c), me) for w in range(nb) for j, chip in enumerate(chips)]
        return mines, first, landed, from_sibling

    def start(self, ins, outs, sems):
        mines, first, _, _ = self._copies(ins, outs, sems)
        for cp in mines + first:
            cp.start()

    def mid(self, ins, outs, sems):
        for arrival, onward in self._copies(ins, outs, sems)[2]:
            arrival.wait_recv()
            onward.start()

    def finish(self, ins, outs, sems):
        mines, first, landed, from_sibling = self._copies(ins, outs, sems)
        for cp in from_sibling:
            cp.wait_recv()
        for cp in first + [onward for _, onward in landed]:
            cp.wait_send()
        for mine in mines:
            mine.wait()


class _ExchangeRider(_Rider):
    def __init__(self, gs, rows=None, fill=None):
        super().__init__(gs, [jax.ShapeDtypeStruct(g.shape, g.dtype) for g in gs], rows, fill)

    def _copies(self, g_refs, out_refs, sems):
        nb = len(self.out_shape)
        send_sems, recv_sems, local_sems = sems
        x, y, c = _coords()
        me = _lin((x, y, c))

        def copy(k, w, landing):
            peer = _flip(k, x, y, c)
            src, dst = (me, _lin(peer)) if landing else (_lin(peer), me)
            return pltpu.make_async_remote_copy(
                src_ref=self.span(w, g_refs[w], src), dst_ref=self.span(w, out_refs[w], dst),
                send_sem=send_sems.at[(k - 1) * nb + w], recv_sem=recv_sems.at[(k - 1) * nb + w],
                device_id=peer, device_id_type=MESH)

        mines = [pltpu.make_async_copy(self.span(w, g_refs[w], me), self.span(w, out_refs[w], me), local_sems.at[w])
                 for w in range(nb)]
        sends = [copy(k, w, False) for w in range(nb) for k in range(1, N_DEV)]
        landings = [copy(k, w, True) for w in range(nb) for k in range(1, N_DEV)]
        return mines, sends, landings

    def start(self, ins, outs, sems):
        mines, sends, _ = self._copies(ins, outs, sems)
        for cp in mines + sends:
            cp.start()

    def finish(self, ins, outs, sems):
        mines, sends, landings = self._copies(ins, outs, sems)
        for cp in landings:
            cp.wait_recv()
        for cp in sends:
            cp.wait_send()
        for mine in mines:
            mine.wait()


def _run_rider(name, rider):
    n_in, n_out = len(rider.ins), len(rider.out_shape)

    def body(*refs):
        ride = (refs[:n_in], refs[n_in:n_in + n_out], refs[n_in + n_out:])
        rider.start(*ride)
        rider.mid(*ride)
        rider.finish(*ride)

    rider.results = pl.pallas_call(
        body, name=name, out_shape=rider.out_shape, in_specs=[ANY] * n_in, out_specs=[ANY] * n_out,
        scratch_shapes=rider.scratch, input_output_aliases=dict(rider.aliases),
        compiler_params=pltpu.CompilerParams(has_side_effects=True),
    )(*rider.ins)
    return rider.results


def _as_matrix(name, g):
    if name == 'w_in':
        return g
    if name in COL_SHARDED:
        return jnp.transpose(g, (1, 0, 2)).reshape(g.shape[1], N_DEV * g.shape[2])
    return g.reshape(N_DEV * g.shape[1], g.shape[2])


def _by_owner(name, g):
    if name == 'w_in':
        return g
    if name in COL_SHARDED:
        return jnp.transpose(g.reshape(g.shape[0], N_DEV, g.shape[1] // N_DEV), (1, 0, 2))
    return g.reshape(N_DEV, g.shape[0] // N_DEV, g.shape[1])


class _Comm:
    def __init__(self, shards):
        self.shards, self.W, self.parts, self.partial, self.sent = shards, {}, {}, {}, {}

    def _ride(self, cls, names, srcs, part, sink):
        part = part or {}
        rider = cls(srcs, rows=[part[n][:2] if n in part else None for n in names],
                    fill=[self.partial.pop((sink, n), None) for n in names])
        rider.names, rider.sink = names, sink
        rider.unfinished = {n for n in names if n in part and not part[n][2]}
        return rider

    def gather(self, names, part=None):
        return self._ride(_GatherRider, names, [self.shards[n] for n in names], part, 'W')

    def exchange(self, names, G, part=None):
        for n in names:
            if n not in self.sent:
                self.sent[n] = _by_owner(n, G[n])
        return self._ride(_ExchangeRider, names, [self.sent[n] for n in names], part, 'parts')

    def collect(self, rider):
        for n, res in zip(rider.names, rider.results):
            if n in rider.unfinished:
                self.partial[(rider.sink, n)] = res
            elif rider.sink == 'W':
                self.W[n] = _as_matrix(n, res)
            else:
                self.parts[n] = res


def _adamw_math(p_ref, w_ref, m_ref, v_ref, g_ref, d_ref, nm_ref, nv_ref):
    c1 = 1.0 - ADAM_B1 ** ADAM_STEP
    c2 = 1.0 - ADAM_B2 ** ADAM_STEP
    g = p_ref[0].astype(F32)
    for s in range(1, p_ref.shape[0]):
        g = g + p_ref[s].astype(F32)
    nm = ADAM_B1 * m_ref[...] + (1.0 - ADAM_B1) * g
    nv = ADAM_B2 * v_ref[...] + (1.0 - ADAM_B2) * (g * g)
    g_ref[...] = g
    nm_ref[...] = nm
    nv_ref[...] = nv
    d_ref[...] = -ADAM_LR * ((nm / c1) / (jnp.sqrt(nv / c2) + ADAM_EPS) + ADAM_WD * w_ref[...])


def _adamw_vectors(parts, ws, ms, vs):
    k = len(ws)

    def body(*refs):
        for i in range(k):
            _adamw_math(refs[i], refs[k + i], refs[2 * k + i], refs[3 * k + i], *refs[4 * k + 4 * i:4 * k + 4 * i + 4])

    return pl.pallas_call(
        body, name="adamw_vectors", out_shape=[jax.ShapeDtypeStruct(w.shape, F32) for w in ws for _ in range(4)],
        compiler_params=_params(),
    )(*parts, *ws, *ms, *vs)


def _adamw(name, parts, w, m, v, tb):
    n, R, Wd = parts.shape
    assert R % tb == 0
    body = functools.partial(_adamw_math)

    row = pl.BlockSpec((tb, Wd), lambda i: (i, 0))
    return pl.pallas_call(
        body, name=name, grid=(R // tb,),
        in_specs=[pl.BlockSpec((n, tb, Wd), lambda i: (0, i, 0)), row, row, row],
        out_specs=[row, row, row, row], out_shape=[jax.ShapeDtypeStruct((R, Wd), F32)] * 4,
        compiler_params=_params(("arbitrary",)),
    )(parts, w, m, v)


def _row_tile(R, unit, cap):
    best = unit
    for t in range(unit, cap + 1, unit):
        if R % t == 0:
            best = t
    return best


def kernel(x, ffn1_w_gate, ffn1_w_up, ffn1_w_down, ln1_g, ln1_b, w_in, b_in, ret_gn_g, conv_k, conv_b, conv_ln_g, conv_ln_b, w_ret_o, w_conv_o, w_out, ln2_g, ln2_b, ffn2_w_gate, ffn2_w_up, ffn2_w_down, ln3_g, ln3_b, loss_target, m_ffn1_w_gate, m_ffn1_w_up, m_ffn1_w_down, m_ln1_g, m_ln1_b, m_w_in, m_b_in, m_ret_gn_g, m_conv_k, m_conv_b, m_conv_ln_g, m_conv_ln_b, m_w_ret_o, m_w_conv_o, m_w_out, m_ln2_g, m_ln2_b, m_ffn2_w_gate, m_ffn2_w_up, m_ffn2_w_down, m_ln3_g, m_ln3_b, v_ffn1_w_gate, v_ffn1_w_up, v_ffn1_w_down, v_ln1_g, v_ln1_b, v_w_in, v_b_in, v_ret_gn_g, v_conv_k, v_conv_b, v_conv_ln_g, v_conv_ln_b, v_w_ret_o, v_w_conv_o, v_w_out, v_ln2_g, v_ln2_b, v_ffn2_w_gate, v_ffn2_w_up, v_ffn2_w_down, v_ln3_g, v_ln3_b):
    given = dict(locals())
    wts = {n: given[n] for n in WEIGHTS}
    mom = {n: given['m_' + n] for n in WEIGHTS}
    var = {n: given['v_' + n] for n in WEIGHTS}

    def shard2d(a):
        return a.reshape(a.shape[-3] * a.shape[-2] if a.ndim == 4 else a.shape[-2], a.shape[-1])

    comm = _Comm({n: shard2d(wts[n]).astype(BF16) for n in BIG})
    P = {n: wts[n].reshape(1, -1) for n in SMALL}
    loss, grad_x, _, S = _local_step(x[0], loss_target[0], None, P, comm=comm)

    parts = comm.parts
    res = {}
    for n in BIG:
        rows, cols = parts[n].shape[1:]
        tb = rows if rows % 16 else _row_tile(rows, 16, max(16, (256 * 1024) // cols))
        res[n] = _adamw("adamw_" + n, parts[n], shard2d(wts[n]), shard2d(mom[n]), shard2d(var[n]), tb)

    vec_parts = _run_rider("gather_vector_grads", _GatherRider([S[n] for n in SMALL]))
    vec = _adamw_vectors(vec_parts, [P[n] for n in SMALL], [mom[n].reshape(1, -1) for n in SMALL],
                         [var[n].reshape(1, -1) for n in SMALL])
    for i, n in enumerate(SMALL):
        res[n] = vec[4 * i:4 * i + 4]

    loss = lax.psum(loss, ("x", "y", "c"))
    outs = [loss, grad_x[None]]
    for k in range(4):
        for n in WEIGHTS:
            outs.append(res[n][k].reshape(wts[n].shape))
    return tuple(outs)
```

```python
import functools
import math

import jax
import jax.numpy as jnp
from jax import lax
from jax.experimental import pallas as pl
from jax.experimental.pallas import tpu as pltpu

F32 = jnp.float32
BF16 = jnp.bfloat16

N_DEV = 8
LN_EPS = 1e-5
ALPHA = 2.0 ** 0.25
RET_DK = 128
RET_DV = 256
RET_CHUNK = 256
ROPE_BASE = 10000.0
CONV_WIDTH = 31
HALO = 32
ADAM_LR, ADAM_B1, ADAM_B2, ADAM_EPS, ADAM_WD, ADAM_STEP = 0.001, 0.9, 0.999, 1e-08, 0.01, 10
VMEM_LIMIT = 52 * 1024 * 1024
MESH = pl.DeviceIdType.MESH
ANY = pl.BlockSpec(memory_space=pl.ANY)

BIG = ['ffn1_w_gate', 'ffn1_w_up', 'ffn1_w_down', 'w_in', 'w_ret_o', 'w_conv_o', 'w_out',
       'ffn2_w_gate', 'ffn2_w_up', 'ffn2_w_down', 'conv_k']
COL_SHARDED = {'ffn1_w_gate', 'ffn1_w_up', 'w_in', 'ffn2_w_gate', 'ffn2_w_up', 'conv_k'}
SMALL = ['ln1_g', 'ln1_b', 'b_in', 'ret_gn_g', 'conv_b', 'conv_ln_g', 'conv_ln_b', 'ln2_g', 'ln2_b', 'ln3_g', 'ln3_b']
WEIGHTS = ['ffn1_w_gate', 'ffn1_w_up', 'ffn1_w_down', 'ln1_g', 'ln1_b', 'w_in', 'b_in', 'ret_gn_g', 'conv_k', 'conv_b',
           'conv_ln_g', 'conv_ln_b', 'w_ret_o', 'w_conv_o', 'w_out', 'ln2_g', 'ln2_b', 'ffn2_w_gate', 'ffn2_w_up',
           'ffn2_w_down', 'ln3_g', 'ln3_b']


def _params(sem=None):
    return pltpu.CompilerParams(dimension_semantics=sem, vmem_limit_bytes=VMEM_LIMIT)


def _sigmoid(x):
    return jax.nn.sigmoid(x)


def _dsilu(x, sg):
    return sg * (1.0 + x * (1.0 - sg))


def _fit(dim, want):
    if dim <= want:
        return dim
    return max(t for t in range(128, want + 1, 128) if dim % t == 0)


def _dot(a, b, ta=False, tb=False):
    dn = (((0,) if ta else (1,), (1,) if tb else (0,)), ((), ()))
    return lax.dot_general(a, b, dn, preferred_element_type=F32)


def _mm(name, As, Bs, prods, epi, out_dtypes, *, ta=False, tb=False, tm, tn, tk, extras=(), i_outer=True,
        b3=False, o3=False, rider=None):
    a0, b0 = As[0], Bs[0]
    M, K = (a0.shape[1], a0.shape[0]) if ta else a0.shape
    if b3:
        S, rows, cs = b0.shape
        N = rows if tb else S * cs
        assert K == (S * cs if tb else rows)
        tn, tk = (tn, cs) if tb else (cs, tk)
    else:
        N = b0.shape[0] if tb else b0.shape[1]
    tm, tn, tk = _fit(M, tm), _fit(N, tn), _fit(K, tk)
    assert M % tm == 0 and N % tn == 0 and K % tk == 0, (name, M, N, K, tm, tn, tk)
    gi, gj, gk = M // tm, N // tn, K // tk
    grid = (gi, gj, gk) if i_outer else (gj, gi, gk)

    def ij(g0, g1):
        return (g0, g1) if i_outer else (g1, g0)

    def amap(g0, g1, k):
        i, _ = ij(g0, g1)
        return (k, i) if ta else (i, k)

    def bmap(g0, g1, k):
        _, j = ij(g0, g1)
        return (j, k) if tb else (k, j)

    def bmap3(g0, g1, k):
        _, j = ij(g0, g1)
        return (k, j, 0) if tb else (j, k, 0)

    in_specs = [pl.BlockSpec((tk, tm) if ta else (tm, tk), amap) for _ in As]
    if b3:
        in_specs += [pl.BlockSpec((None, tn, tk) if tb else (None, tk, tn), bmap3) for _ in Bs]
    else:
        in_specs += [pl.BlockSpec((tn, tk) if tb else (tk, tn), bmap) for _ in Bs]
    args = list(As) + list(Bs)
    for arr, kind, coloff in extras:
        assert coloff % tn == 0
        off = coloff // tn
        if kind == 'mn':
            in_specs.append(pl.BlockSpec((tm, tn), lambda g0, g1, k, off=off: (ij(g0, g1)[0], ij(g0, g1)[1] + off)))
        else:
            in_specs.append(pl.BlockSpec((1, tn), lambda g0, g1, k, off=off: (0, ij(g0, g1)[1] + off)))
        args.append(arr)
    if o3:
        out_shape = [jax.ShapeDtypeStruct((gj, M, tn), dt) for dt in out_dtypes]
        out_specs = [pl.BlockSpec((None, tm, tn), lambda g0, g1, k: (ij(g0, g1)[1], ij(g0, g1)[0], 0))
                     for _ in out_dtypes]
    else:
        out_shape = [jax.ShapeDtypeStruct((M, N), dt) for dt in out_dtypes]
        out_specs = [pl.BlockSpec((tm, tn), lambda g0, g1, k: ij(g0, g1)) for _ in out_dtypes]
    n_a, n_b, n_e, n_o = len(As), len(Bs), len(extras), len(out_dtypes)
    n_p = len(prods) if gk > 1 else 0
    scratch = [pltpu.VMEM((tm, tn), F32) for _ in range(n_p)]
    if rider is not None:
        in_specs, out_specs = in_specs + [ANY] * len(rider.ins), out_specs + [ANY] * len(rider.out_shape)
        args, out_shape, scratch = args + rider.ins, out_shape + rider.out_shape, scratch + rider.scratch
    n_in, n_out = len(args), len(out_shape)

    def body(*refs):
        a_refs = refs[:n_a]
        b_refs = refs[n_a:n_a + n_b]
        e_refs = refs[n_a + n_b:n_a + n_b + n_e]
        o_refs = refs[n_in:n_in + n_o]
        acc_refs = refs[n_in + n_out:n_in + n_out + n_p]
        k = pl.program_id(2)
        if rider is not None:
            step = (pl.program_id(0) * grid[1] + pl.program_id(1)) * gk + k
            ride = (step, grid[0] * grid[1] * gk, refs[n_a + n_b + n_e:n_in], refs[n_in + n_o:n_in + n_out],
                    refs[n_in + n_out + n_p:])
            rider.begin(*ride)

        def finish(accs):
            for o, r in zip(o_refs, epi(accs, *[e[...].astype(F32) for e in e_refs])):
                o[...] = r.astype(o.dtype)

        if gk == 1:
            finish([functools.reduce(jnp.add, [_dot(a_refs[ai][...], b_refs[bi][...], ta, tb) for ai, bi in terms])
                    for terms in prods])
        else:
            @pl.when(k == 0)
            def _():
                for acc in acc_refs:
                    acc[...] = jnp.zeros_like(acc)

            for p, terms in enumerate(prods):
                for ai, bi in terms:
                    acc_refs[p][...] += _dot(a_refs[ai][...], b_refs[bi][...], ta, tb)

            @pl.when(k == gk - 1)
            def _():
                finish([acc[...] for acc in acc_refs])

        if rider is not None:
            rider.end(*ride)

    aliases = {} if rider is None else {n_a + n_b + n_e + p: n_o + o for p, o in rider.aliases.items()}
    res = pl.pallas_call(
        body, name=name, grid=grid, in_specs=in_specs, out_specs=out_specs, out_shape=out_shape,
        scratch_shapes=scratch, input_output_aliases=aliases,
        compiler_params=_params(("arbitrary", "arbitrary", "arbitrary")),
    )(*args)
    if rider is not None:
        rider.results = res[n_o:]
    return res[:n_o]


def _rows(name, fn, ins, outs, *, T, tb):
    tb = min(tb, T)
    assert T % tb == 0
    in_specs, args = [], []
    for arr, kind, width, cb in ins:
        if kind == 'r':
            in_specs.append(pl.BlockSpec((tb, width), lambda i, cb=cb: (i, cb)))
        else:
            in_specs.append(pl.BlockSpec((1, width), lambda i, cb=cb: (0, cb)))
        args.append(arr)
    out_shape, out_specs = [], []
    for kind, width, dtype in outs:
        if kind == 'r':
            out_shape.append(jax.ShapeDtypeStruct((T, width), dtype))
            out_specs.append(pl.BlockSpec((tb, width), lambda i: (i, 0)))
        elif kind == 'c':
            out_shape.append(jax.ShapeDtypeStruct((T, 1), dtype))
            out_specs.append(pl.BlockSpec((tb, 1), lambda i: (i, 0)))
        else:
            out_shape.append(jax.ShapeDtypeStruct((1, width), F32))
            out_specs.append(pl.BlockSpec((1, width), lambda i: (0, 0)))
    n_in = len(ins)

    def body(*refs):
        i = pl.program_id(0)
        vals = fn(*[r[...] for r in refs[:n_in]])
        for (kind, _, _), o, v in zip(outs, refs[n_in:], vals):
            if kind == 'a':
                @pl.when(i == 0)
                def _(o=o):
                    o[...] = jnp.zeros_like(o)

                o[...] += v
            else:
                o[...] = v.astype(o.dtype)

    return pl.pallas_call(
        body, name=name, grid=(T // tb,), in_specs=in_specs, out_specs=out_specs, out_shape=out_shape,
        compiler_params=_params(("arbitrary",)),
    )(*args)


def _colsum(v):
    return jnp.sum(v, axis=0, keepdims=True)


def _ln_stats(z):
    mu = jnp.mean(z, axis=-1, keepdims=True)
    d = z - mu
    var = jnp.mean(d * d, axis=-1, keepdims=True)
    rstd = lax.rsqrt(var + LN_EPS)
    return d * rstd, rstd


def _ln_bwd_math(dy, xhat, rstd, g):
    dxh = dy * g
    m1 = jnp.mean(dxh, axis=-1, keepdims=True)
    m2 = jnp.mean(dxh * xhat, axis=-1, keepdims=True)
    return rstd * (dxh - m1 - xhat * m2)


def _ln_fwd(name, z, g, b, T, D):
    def fn(z, g, b):
        xhat, rstd = _ln_stats(z)
        y = xhat * g + b
        return [y, y, xhat, rstd]

    return _rows(name, fn, [(z, 'r', D, 0), (g, 'v', D, 0), (b, 'v', D, 0)],
                 [('r', D, F32), ('r', D, BF16), ('r', D, F32), ('c', 1, F32)], T=T, tb=512)


def _ln_bwd(name, dy, xhat, rstd, g, scale, T, D):
    def fn(dy, xhat, rstd, g):
        dz = _ln_bwd_math(dy, xhat, rstd, g)
        return [dz, dz * scale, _colsum(dy * xhat), _colsum(dy)]

    return _rows(name, fn, [(dy, 'r', D, 0), (xhat, 'r', D, 0), (rstd, 'r', 1, 0), (g, 'v', D, 0)],
                 [('r', D, F32), ('r', D, BF16), ('a', D, F32), ('a', D, F32)], T=T, tb=512)


def _ln_loss_bwd(name, z, g, b, tgt, T, D):
    def fn(z, g, b, tgt):
        xhat, rstd = _ln_stats(z)
        err = xhat * g + b - tgt
        row_loss = 0.5 * jnp.mean(err * err, axis=-1, keepdims=True)
        loss = jnp.broadcast_to(jnp.sum(row_loss, axis=0, keepdims=True), (1, 128))
        dy = err * (1.0 / D)
        dz = _ln_bwd_math(dy, xhat, rstd, g)
        return [dz, dz * 0.5, _colsum(dy * xhat), _colsum(dy), loss]

    return _rows(name, fn, [(z, 'r', D, 0), (g, 'v', D, 0), (b, 'v', D, 0), (tgt, 'r', D, 0)],
                 [('r', D, F32), ('r', D, BF16), ('a', D, F32), ('a', D, F32), ('a', 128, F32)], T=T, tb=512)


class _Net:
    def __init__(self, comm, G):
        self.comm, self.G = comm, G

    def gather(self, names, part=None):
        return self.comm.gather(names, part) if self.comm else None

    def exchange(self, names, part=None):
        return self.comm.exchange(names, self.G, part) if self.comm else None

    def done(self, rider):
        if rider is not None:
            self.comm.collect(rider)


def _ffn_fwd(tag, xb, x, W, names, net, rider=None, rider_down=None):
    def epi_gu(accs):
        a, b = accs
        return [a, b, a * _sigmoid(a) * b]

    ng, nu, nd = names
    a, b, s = _mm(tag + "_gate_up", [xb], [W[ng], W[nu]], [[(0, 0)], [(0, 1)]], epi_gu, [BF16, BF16, BF16],
                  tm=2048, tn=256, tk=1024, rider=rider)
    net.done(rider)

    def epi_down(accs, xres):
        return [ALPHA * xres + 0.5 * accs[0]]

    rider_down = rider_down() if rider_down else None
    (z,) = _mm(tag + "_down", [s], [W[nd]], [[(0, 0)]], epi_down, [F32], tm=1024, tn=1024, tk=1408,
               extras=[(x, 'mn', 0)], rider=rider_down)
    net.done(rider_down)
    return a, b, s, z


def _ffn_bwd(tag, dzh, dz, xb, a, b, s, W, names, gdt, G, net, ride, pre=(None, None)):
    ng, nu, nd = names

    def epi_ds(accs, a, b):
        ds = accs[0]
        sg = _sigmoid(a)
        return [ds * b * _dsilu(a, sg), ds * a * sg]

    rider = pre[0]() if pre[0] else None
    da, db = _mm(tag + "_ds", [dzh], [W[nd]], [[(0, 0)]], epi_ds, [BF16, BF16], tb=True, tm=2048, tn=256, tk=1024,
                 extras=[(a, 'mn', 0), (b, 'mn', 0)], rider=rider)
    net.done(rider)
    ident = lambda accs: accs
    rider = pre[1]() if pre[1] else None
    (G[nd],) = _mm(tag + "_dwd", [s], [dzh], [[(0, 0)]], ident, [gdt], ta=True, tm=1408, tn=1024, tk=1024,
                   rider=rider)
    net.done(rider)
    if ride:
        rider = net.exchange([nd])
        (G[ng],) = _mm(tag + "_dwg", [xb], [da], [[(0, 0)]], ident, [gdt], ta=True, tm=1024, tn=1408, tk=1024,
                       rider=rider)
        net.done(rider)
        rider = net.exchange([ng])
        (G[nu],) = _mm(tag + "_dwu", [xb], [db], [[(0, 0)]], ident, [gdt], ta=True, tm=1024, tn=1408, tk=1024,
                       rider=rider)
        net.done(rider)
    else:
        G[ng], G[nu] = _mm(tag + "_dwgu", [xb], [da, db], [[(0, 0)], [(0, 1)]], ident, [gdt, gdt], ta=True,
                           tm=1024, tn=1408, tk=1024)

    def epi_dx(accs, dzres):
        return [ALPHA * dzres + accs[0]]

    rider = net.exchange([nu]) if ride else None
    (dx,) = _mm(tag + "_dx", [da, db], [W[ng], W[nu]], [[(0, 0), (1, 1)]], epi_dx, [F32], tb=True,
                tm=1024, tn=1024, tk=1408, extras=[(dz, 'mn', 0)], rider=rider)
    net.done(rider)
    return dx


def _ret_tables(H, T):
    C = RET_CHUNK
    log_g = jnp.log(1.0 - jnp.exp2(-5.0 - jnp.arange(H, dtype=F32)))
    idx = jnp.arange(C, dtype=F32)
    diff = idx[:, None] - idx[None, :]
    dm = jnp.where(diff[None] >= 0, jnp.exp(jnp.maximum(diff, 0.0)[None] * log_g[:, None, None]), 0.0)
    xi = jnp.exp((idx[None, :] + 1.0) * log_g[:, None])[:, :, None]
    zeta = jnp.exp((C - 1.0 - idx)[None, :] * log_g[:, None])[:, :, None]
    gc = jnp.broadcast_to(jnp.exp(C * log_g)[:, None, None], (H, 1, RET_DV))
    half = RET_DK // 2
    freqs = ROPE_BASE ** (-jnp.arange(half, dtype=F32) / half)
    ang = jnp.arange(T, dtype=F32)[:, None] * freqs[None, :]
    cos, sin = jnp.cos(ang), jnp.sin(ang)
    cosf = jnp.concatenate([cos, cos], axis=1)
    sins = jnp.concatenate([-sin, sin], axis=1)
    return dm, xi, zeta, gc, cosf, sins


def _rot(x, cosf, sins):
    return x * cosf + pltpu.roll(x, RET_DK // 2, 1) * sins


def _rot_bwd(dy, cosf, sins):
    return dy * cosf + pltpu.roll(dy * sins, RET_DK // 2, 1)


RET_HB = 8


def _ret_specs(H, HB, rev, NC):
    C, G = RET_CHUNK, H // HB
    nn = (lambda n: NC - 1 - n) if rev else (lambda n: n)
    return [
        pl.BlockSpec((C, HB * RET_DK), lambda h, n: (nn(n), h)),
        pl.BlockSpec((C, HB * RET_DK), lambda h, n: (nn(n), G + h)),
        pl.BlockSpec((C, HB * RET_DV), lambda h, n: (nn(n), G + h)),
        pl.BlockSpec((C, HB * RET_DV), lambda h, n: (nn(n), 2 * G + h)),
        pl.BlockSpec((C, RET_DK), lambda h, n: (nn(n), 0)),
        pl.BlockSpec((C, RET_DK), lambda h, n: (nn(n), 0)),
        pl.BlockSpec((1, HB * RET_DV), lambda h, n: (0, h)),
        pl.BlockSpec((HB, C, C), lambda h, n: (h, 0, 0)),
        pl.BlockSpec((HB, C, 1), lambda h, n: (h, 0, 0)),
        pl.BlockSpec((HB, C, 1), lambda h, n: (h, 0, 0)),
        pl.BlockSpec((HB, 1, RET_DV), lambda h, n: (h, 0, 0)),
    ]


def _ret_fwd(proj, gn_g, tabs, H, T, rider=None):
    C, NC = RET_CHUNK, T // RET_CHUNK
    HB = min(RET_HB, H)
    dm, xi, zeta, gc, cosf, sins = tabs
    scale = RET_DK ** -0.5

    def body(q_ref, k_ref, v_ref, g_ref, cos_ref, sin_ref, gn_ref, dm_ref, xi_ref, zt_ref, gc_ref,
             r_ref, ri_ref, st_ref, state):
        @pl.when(pl.program_id(1) == 0)
        def _():
            state[...] = jnp.zeros_like(state)

        cs, sn = cos_ref[...], sin_ref[...]
        for hh in range(HB):
            qk = slice(hh * RET_DK, (hh + 1) * RET_DK)
            vv = slice(hh * RET_DV, (hh + 1) * RET_DV)
            qr = _rot(q_ref[:, qk].astype(F32), cs, sn) * scale
            kr = _rot(k_ref[:, qk].astype(F32), cs, sn)
            qb, kb, vb = qr.astype(BF16), kr.astype(BF16), v_ref[:, vv].astype(BF16)
            st = state[hh]
            stb = st.astype(BF16)
            s = _dot(qb, kb, tb=True) * dm_ref[hh]
            r = _dot(s.astype(BF16), vb) + _dot(qb, stb) * xi_ref[hh]
            st_ref[hh] = stb
            state[hh] = gc_ref[hh] * st + _dot((kr * zt_ref[hh]).astype(BF16), vb, ta=True)
            rhat, _ = _ln_stats(r)
            g = g_ref[:, vv].astype(F32)
            r_ref[:, vv] = r
            ri_ref[:, vv] = (g * _sigmoid(g) * (rhat * gn_ref[:, vv])).astype(BF16)

    VW = H * RET_DV
    return _hosted_call(
        body, rider, name="ret_fwd", grid=(H // HB, NC), in_specs=_ret_specs(H, HB, False, NC),
        out_specs=[pl.BlockSpec((C, HB * RET_DV), lambda h, n: (n, h)),
                   pl.BlockSpec((C, HB * RET_DV), lambda h, n: (n, h)),
                   pl.BlockSpec((HB, None, RET_DK, RET_DV), lambda h, n: (h, n, 0, 0))],
        out_shape=[jax.ShapeDtypeStruct((T, VW), F32), jax.ShapeDtypeStruct((T, VW), BF16),
                   jax.ShapeDtypeStruct((H, NC, RET_DK, RET_DV), BF16)],
        scratch=[pltpu.VMEM((HB, RET_DK, RET_DV), F32)],
        args=[proj, proj, proj, proj, cosf, sins, gn_g, dm, xi, zeta, gc])


def _hosted_call(body, rider, *, name, grid, in_specs, out_specs, out_shape, scratch, args):
    n_in, n_out, n_scr = len(args), len(out_shape), len(scratch)
    if rider is None:
        hosted = body
    else:
        n_ri, n_ro = len(rider.ins), len(rider.out_shape)
        in_specs, out_specs = in_specs + [ANY] * n_ri, out_specs + [ANY] * n_ro
        args, out_shape, scratch = args + rider.ins, out_shape + rider.out_shape, scratch + rider.scratch

        def hosted(*refs):
            o0, s0 = n_in + n_ri, n_in + n_ri + n_out + n_ro
            step = pl.program_id(0) * grid[1] + pl.program_id(1)
            ride = (step, grid[0] * grid[1], refs[n_in:o0], refs[o0 + n_out:s0], refs[s0 + n_scr:])
            rider.begin(*ride)
            body(*refs[:n_in], *refs[o0:o0 + n_out], *refs[s0:s0 + n_scr])
            rider.end(*ride)

    aliases = {} if rider is None else {n_in + p: n_out + o for p, o in rider.aliases.items()}
    res = pl.pallas_call(
        hosted, name=name, grid=grid, in_specs=in_specs, out_specs=out_specs, out_shape=out_shape,
        scratch_shapes=scratch, input_output_aliases=aliases, compiler_params=_params(("arbitrary", "arbitrary")),
    )(*args)
    if rider is not None:
        rider.results = res[n_out:]
    return res[:n_out]


def _ret_bwd(dri, r, states, proj, gn_g, tabs, H, T, in_w, rider=None):
    C, NC = RET_CHUNK, T // RET_CHUNK
    HB = min(RET_HB, H)
    dm, xi, zeta, gc, cosf, sins = tabs
    scale = RET_DK ** -0.5

    def body(q_ref, k_ref, v_ref, g_ref, cos_ref, sin_ref, gn_ref, dm_ref, xi_ref, zt_ref, gc_ref,
             dri_ref, r_ref, st_ref, dp_ref, dgn_ref, dstate):
        @pl.when(pl.program_id(1) == 0)
        def _():
            dstate[...] = jnp.zeros_like(dstate)
            dgn_ref[...] = jnp.zeros_like(dgn_ref)

        cs, sn = cos_ref[...], sin_ref[...]
        for hh in range(HB):
            qk = slice(hh * RET_DK, (hh + 1) * RET_DK)
            vv = slice(hh * RET_DV, (hh + 1) * RET_DV)
            qr = _rot(q_ref[:, qk].astype(F32), cs, sn) * scale
            kr = _rot(k_ref[:, qk].astype(F32), cs, sn)
            qb, kb, vb = qr.astype(BF16), kr.astype(BF16), v_ref[:, vv].astype(BF16)
            xi_c, zt_c, dmask = xi_ref[hh], zt_ref[hh], dm_ref[hh]
            rhat, rstd = _ln_stats(r_ref[:, vv])
            g, gn, dpre = g_ref[:, vv].astype(F32), gn_ref[:, vv], dri_ref[:, vv]
            sg = _sigmoid(g)
            dp_ref[:, 2 * QW + VW + hh * RET_DV:2 * QW + VW + (hh + 1) * RET_DV] = (
                dpre * (rhat * gn) * _dsilu(g, sg)).astype(BF16)
            drn = dpre * (g * sg)
            dgn_ref[:, vv] += _colsum(drn * rhat)
            drb = _ln_bwd_math(drn, rhat, rstd, gn).astype(BF16)
            ds1 = dstate[hh]
            ds1b = ds1.astype(BF16)
            sb = (_dot(qb, kb, tb=True) * dmask).astype(BF16)
            kzb = (kr * zt_c).astype(BF16)
            dp_ref[:, 2 * QW + hh * RET_DV:2 * QW + (hh + 1) * RET_DV] = (
                _dot(sb, drb, ta=True) + _dot(kzb, ds1b)).astype(BF16)
            dsb = (_dot(drb, vb, tb=True) * dmask).astype(BF16)
            dq = _dot(dsb, kb) + _dot(drb, st_ref[hh], tb=True) * xi_c
            dk = _dot(dsb, qb, ta=True) + _dot(vb, ds1b, tb=True) * zt_c
            dstate[hh] = gc_ref[hh] * ds1 + _dot((qr * xi_c).astype(BF16), drb, ta=True)
            dp_ref[:, qk] = _rot_bwd(dq * scale, cs, sn).astype(BF16)
            dp_ref[:, QW + hh * RET_DK:QW + (hh + 1) * RET_DK] = _rot_bwd(dk, cs, sn).astype(BF16)

    VW, QW = H * RET_DV, H * RET_DK
    rv = lambda n: NC - 1 - n
    in_specs = _ret_specs(H, HB, True, NC) + [
        pl.BlockSpec((C, HB * RET_DV), lambda h, n: (rv(n), h)),
        pl.BlockSpec((C, HB * RET_DV), lambda h, n: (rv(n), h)),
        pl.BlockSpec((HB, None, RET_DK, RET_DV), lambda h, n: (h, rv(n), 0, 0)),
    ]
    assert HB == H
    return _hosted_call(
        body, rider, name="ret_bwd", grid=(1, NC), in_specs=in_specs,
        out_specs=[pl.BlockSpec((C, 2 * QW + 2 * VW), lambda h, n: (rv(n), 0)),
                   pl.BlockSpec((1, VW), lambda h, n: (0, 0))],
        out_shape=[jax.ShapeDtypeStruct((T, in_w), BF16), jax.ShapeDtypeStruct((1, VW), F32)],
        scratch=[pltpu.VMEM((HB, RET_DK, RET_DV), F32)],
        args=[proj, proj, proj, proj, cosf, sins, gn_g, dm, xi, zeta, gc, dri, r, states])


CONV_CW = 128
CONV_TB = 512


SUBLANES = 8


def _shift_copies(win, shifted, tb):
    n = tb + HALO - SUBLANES
    for s in range(1, SUBLANES):
        shifted[s - 1] = win[pl.ds(s, n), :]


def _tap(win, shifted, off, tb):
    s = off % SUBLANES
    if s == 0:
        return win[pl.ds(off, tb), :]
    return shifted[s - 1, pl.ds(off - s, tb), :]


def _conv_fwd(proj, kpad, bias, off_a, CC, T):
    tb, cw = min(CONV_TB, T), CONV_CW
    hb = tb // HALO
    ca, cb = off_a // cw, (off_a + CC) // cw

    def body(a_ref, b_ref, ap_ref, bp_ref, k_ref, bias_ref, u1_ref, win, shifted):
        i = pl.program_id(0)
        keep = (i > 0).astype(F32)
        win[0:HALO, :] = ap_ref[...].astype(F32) * _sigmoid(bp_ref[...].astype(F32)) * keep
        win[HALO:, :] = a_ref[...].astype(F32) * _sigmoid(b_ref[...].astype(F32))
        _shift_copies(win, shifted, tb)
        acc = jnp.broadcast_to(bias_ref[...], (tb, cw))
        for w in range(CONV_WIDTH):
            acc = acc + k_ref[w:w + 1, :] * _tap(win, shifted, HALO - (CONV_WIDTH - 1) + w, tb)
        u1_ref[...] = acc

    prev = lambda i: jnp.maximum(i * hb - 1, 0)
    return pl.pallas_call(
        body, name="conv_fwd", grid=(T // tb, CC // cw),
        in_specs=[pl.BlockSpec((tb, cw), lambda i, c: (i, ca + c)),
                  pl.BlockSpec((tb, cw), lambda i, c: (i, cb + c)),
                  pl.BlockSpec((HALO, cw), lambda i, c: (prev(i), ca + c)),
                  pl.BlockSpec((HALO, cw), lambda i, c: (prev(i), cb + c)),
                  pl.BlockSpec((HALO, cw), lambda i, c: (0, c)),
                  pl.BlockSpec((1, cw), lambda i, c: (0, c))],
        out_specs=pl.BlockSpec((tb, cw), lambda i, c: (i, c)),
        out_shape=jax.ShapeDtypeStruct((T, CC), F32),
        scratch_shapes=[pltpu.VMEM((tb + HALO, cw), F32), pltpu.VMEM((SUBLANES - 1, tb + HALO - SUBLANES, cw), F32)],
        compiler_params=_params(("arbitrary", "arbitrary")),
    )(proj, proj, proj, proj, kpad, bias)


def _conv_bwd(du1, proj, kpad, off_a, CC, T, rider=None):
    tb, cw = min(CONV_TB, T), CONV_CW
    hb = tb // HALO
    nt = T // tb
    ca, cb = off_a // cw, (off_a + CC) // cw

    def body(d_ref, dn_ref, a_ref, b_ref, ap_ref, bp_ref, k_ref, da_ref, db_ref, dk_ref, winu, wind, shu, shd):
        i = pl.program_id(1)
        a, b = a_ref[...].astype(F32), b_ref[...].astype(F32)
        sgb = _sigmoid(b)
        winu[0:HALO, :] = ap_ref[...].astype(F32) * _sigmoid(bp_ref[...].astype(F32)) * (i > 0).astype(F32)
        winu[HALO:, :] = a * sgb
        d = d_ref[...]
        wind[0:tb, :] = d
        wind[tb:, :] = dn_ref[...] * (i < nt - 1).astype(F32)

        @pl.when(i == 0)
        def _():
            dk_ref[...] = jnp.zeros_like(dk_ref)

        _shift_copies(winu, shu, tb)
        _shift_copies(wind, shd, tb)
        du0 = jnp.zeros((tb, cw), F32)
        for w in range(CONV_WIDTH):
            du0 = du0 + k_ref[w:w + 1, :] * _tap(wind, shd, CONV_WIDTH - 1 - w, tb)
            dk_ref[w:w + 1, :] += _colsum(_tap(winu, shu, HALO - (CONV_WIDTH - 1) + w, tb) * d)
        da_ref[...] = (du0 * sgb).astype(BF16)
        db_ref[...] = (du0 * a * sgb * (1.0 - sgb)).astype(BF16)

    prev = lambda i: jnp.maximum(i * hb - 1, 0)
    nxt = lambda i: jnp.minimum((i + 1) * hb, T // HALO - 1)
    return _hosted_call(
        body, rider, name="conv_bwd", grid=(CC // cw, nt),
        in_specs=[pl.BlockSpec((tb, cw), lambda c, i: (i, c)),
                  pl.BlockSpec((HALO, cw), lambda c, i: (nxt(i), c)),
                  pl.BlockSpec((tb, cw), lambda c, i: (i, ca + c)),
                  pl.BlockSpec((tb, cw), lambda c, i: (i, cb + c)),
                  pl.BlockSpec((HALO, cw), lambda c, i: (prev(i), ca + c)),
                  pl.BlockSpec((HALO, cw), lambda c, i: (prev(i), cb + c)),
                  pl.BlockSpec((HALO, cw), lambda c, i: (0, c))],
        out_specs=[pl.BlockSpec((tb, cw), lambda c, i: (i, c)),
                   pl.BlockSpec((tb, cw), lambda c, i: (i, c)),
                   pl.BlockSpec((HALO, cw), lambda c, i: (0, c))],
        out_shape=[jax.ShapeDtypeStruct((T, CC), BF16), jax.ShapeDtypeStruct((T, CC), BF16),
                   jax.ShapeDtypeStruct((HALO, CC), F32)],
        scratch=[pltpu.VMEM((tb + HALO, cw), F32), pltpu.VMEM((tb + HALO, cw), F32),
                 pltpu.VMEM((SUBLANES - 1, tb + HALO - SUBLANES, cw), F32),
                 pltpu.VMEM((SUBLANES - 1, tb + HALO - SUBLANES, cw), F32)],
        args=[du1, du1, proj, proj, proj, proj, kpad])


FFN1 = ('ffn1_w_gate', 'ffn1_w_up', 'ffn1_w_down')
FFN2 = ('ffn2_w_gate', 'ffn2_w_up', 'ffn2_w_down')


def _local_step(x, tgt, W, P, gdt=BF16, comm=None):
    T, D = x.shape
    G = {}
    net = _Net(comm, G)
    if comm is not None:
        W = comm.W
        first = net.gather(['ffn1_w_gate', 'ffn1_w_up'])
        _run_rider("gather_ffn1_in", first)
        net.done(first)
    VW = P['ret_gn_g'].shape[1]
    H = VW // RET_DV
    QW = H * RET_DK
    CC = P['conv_b'].shape[1]
    off_glu = 2 * QW + 2 * VW
    off_gate = off_glu + 2 * CC
    ident = lambda accs: accs
    xb = x.astype(BF16)

    a1, b1, s1, z1 = _ffn_fwd("ffn1", xb, x, W, FFN1, net,
                              rider=net.gather(['ffn1_w_down', 'w_in'], {'w_in': (0, D // 2, False)}),
                              rider_down=lambda: net.gather(['w_in'], {'w_in': (D // 2, D // 2, True)}))
    x1, x1b, xh1, rs1 = _ln_fwd("ln1", z1, P['ln1_g'], P['ln1_b'], T, D)

    rest = net.gather(['conv_k', 'w_ret_o', 'w_conv_o', 'w_out'])
    (proj,) = _mm("w_in", [x1b], [W['w_in']], [[(0, 0)]], lambda accs, bias: [accs[0] + bias], [F32],
                  tm=2048, tn=0, tk=1024, extras=[(P['b_in'], 'n', 0)], i_outer=False, b3=True, rider=rest)
    net.done(rest)
    tabs = _ret_tables(H, T)
    rider = net.gather(list(FFN2))
    r, ret_in, states = _ret_fwd(proj, P['ret_gn_g'], tabs, H, T, rider=rider)
    net.done(rider)
    kpad = jnp.pad(W['conv_k'].astype(F32), ((0, HALO - CONV_WIDTH), (0, 0)))
    u1 = _conv_fwd(proj, kpad, P['conv_b'], off_glu, CC, T)

    def conv_ln(u1, g, b):
        xhat, rstd = _ln_stats(u1)
        u2 = xhat * g + b
        return [xhat, rstd, u2 * _sigmoid(u2)]

    xhc, rsc, u3 = _rows("conv_ln", conv_ln, [(u1, 'r', CC, 0), (P['conv_ln_g'], 'v', CC, 0), (P['conv_ln_b'], 'v', CC, 0)],
                         [('r', CC, F32), ('c', 1, F32), ('r', CC, BF16)], T=T, tb=512)
    (ret_out,) = _mm("ret_o", [ret_in], [W['w_ret_o']], [[(0, 0)]], ident, [F32], tm=1024, tn=1024, tk=2048)

    def epi_merge(accs, ret_out, gr, gc):
        conv_out = accs[0]
        return [conv_out, _sigmoid(gr) * ret_out + _sigmoid(gc) * conv_out]

    conv_out, merged = _mm("conv_o_merge", [u3], [W['w_conv_o']], [[(0, 0)]], epi_merge, [F32, BF16],
                           tm=512, tn=D, tk=1024,
                           extras=[(ret_out, 'mn', 0), (proj, 'mn', off_gate), (proj, 'mn', off_gate + D)])
    (z2,) = _mm("w_out", [merged], [W['w_out']], [[(0, 0)]], lambda accs, xr: [ALPHA * xr + accs[0]], [F32],
                tm=1024, tn=1024, tk=1024, extras=[(x1, 'mn', 0)])
    x2, x2b, xh2, rs2 = _ln_fwd("ln2", z2, P['ln2_g'], P['ln2_b'], T, D)
    a2, b2, s2, z3 = _ffn_fwd("ffn2", x2b, x2, W, FFN2, net)
    dz3, dz3h, g_ln3_g, g_ln3_b, loss = _ln_loss_bwd("ln3_loss", z3, P['ln3_g'], P['ln3_b'], tgt, T, D)

    S = {'ln3_g': g_ln3_g, 'ln3_b': g_ln3_b}
    dy2 = _ffn_bwd("ffn2b", dz3h, dz3, x2b, a2, b2, s2, W, FFN2, gdt, G, net, False)
    dz2, dz2b, S['ln2_g'], S['ln2_b'] = _ln_bwd("ln2b", dy2, xh2, rs2, P['ln2_g'], 1.0, T, D)

    (G['w_out'],) = _mm("d_w_out", [merged], [dz2b], [[(0, 0)]], ident, [gdt], ta=True, tm=1024, tn=1024, tk=1024)

    def epi_dmerge(accs, ret_out, conv_out, gr, gc):
        dm_ = accs[0]
        sr, sc = _sigmoid(gr), _sigmoid(gc)
        return [dm_ * sr, dm_ * sc, dm_ * ret_out * sr * (1.0 - sr), dm_ * conv_out * sc * (1.0 - sc)]

    dret_out, dconv_out, dgate_r, dgate_c = _mm(
        "d_merge", [dz2b], [W['w_out']], [[(0, 0)]], epi_dmerge, [BF16, BF16, BF16, BF16], tb=True,
        tm=512, tn=D, tk=1024,
        extras=[(ret_out, 'mn', 0), (conv_out, 'mn', 0), (proj, 'mn', off_gate), (proj, 'mn', off_gate + D)])
    (G['w_ret_o'],) = _mm("d_w_ret_o", [ret_in], [dret_out], [[(0, 0)]], ident, [gdt], ta=True, tm=1024, tn=1024, tk=1024)
    (G['w_conv_o'],) = _mm("d_w_conv_o", [u3], [dconv_out], [[(0, 0)]], ident, [gdt], ta=True, tm=1024, tn=1024, tk=1024)
    (dri,) = _mm("d_ret_in", [dret_out], [W['w_ret_o']], [[(0, 0)]], ident, [F32], tb=True, tm=1024, tn=1024, tk=1024)
    rider = net.exchange(list(FFN2))
    dproj, S['ret_gn_g'] = _ret_bwd(dri, r, states, proj, P['ret_gn_g'], tabs, H, T, proj.shape[1], rider=rider)
    net.done(rider)

    def epi_du2(accs, xhat, g, b):
        u2 = xhat * g + b
        return [accs[0] * _dsilu(u2, _sigmoid(u2))]

    (du2,) = _mm("d_u3", [dconv_out], [W['w_conv_o']], [[(0, 0)]], epi_du2, [F32], tb=True, tm=512, tn=CC, tk=1024,
                 extras=[(xhc, 'mn', 0), (P['conv_ln_g'], 'n', 0), (P['conv_ln_b'], 'n', 0)])

    def conv_ln_bwd(du2, xhat, rstd, g):
        du1 = _ln_bwd_math(du2, xhat, rstd, g)
        return [du1, _colsum(du2 * xhat), _colsum(du2), _colsum(du1)]

    du1, S['conv_ln_g'], S['conv_ln_b'], S['conv_b'] = _rows(
        "conv_ln_bwd", conv_ln_bwd, [(du2, 'r', CC, 0), (xhc, 'r', CC, 0), (rsc, 'r', 1, 0), (P['conv_ln_g'], 'v', CC, 0)],
        [('r', CC, F32), ('a', CC, F32), ('a', CC, F32), ('a', CC, F32)], T=T, tb=512)
    rider = net.exchange(['w_out', 'w_ret_o', 'w_conv_o'])
    dglu_a, dglu_b, dkpad = _conv_bwd(du1, proj, kpad, off_glu, CC, T, rider=rider)
    net.done(rider)
    G['conv_k'] = dkpad[:CONV_WIDTH].astype(gdt)

    for off, piece in ((off_glu, dglu_a), (off_glu + CC, dglu_b), (off_gate, dgate_r), (off_gate + D, dgate_c)):
        dproj = lax.dynamic_update_slice(dproj, piece, (0, off))
    IN_W = dproj.shape[1]
    (S['b_in'],) = _rows("d_b_in", lambda d: [_colsum(d.astype(F32))], [(dproj, 'r', IN_W, 0)], [('a', IN_W, F32)],
                         T=T, tb=256)
    (G['w_in'],) = _mm("d_w_in", [x1b], [dproj], [[(0, 0)]], ident, [gdt], ta=True, o3=True,
                       tm=1024, tn=W['w_in'].shape[2], tk=1024)
    cuts = [0, (9 * D) // 16, (53 * D) // 64, D]
    w_in_rows = [{'w_in': (cuts[i], cuts[i + 1] - cuts[i], i == 2)} for i in range(3)]
    rider = net.exchange(['w_in', 'conv_k'], w_in_rows[0])
    (dy1,) = _mm("d_x1", [dproj], [W['w_in']], [[(0, 0)]], lambda accs, dzr: [ALPHA * dzr + accs[0]], [F32], tb=True,
                 b3=True, tm=1024, tn=1024, tk=0, extras=[(dz2, 'mn', 0)], rider=rider)
    net.done(rider)
    dz1, dz1h, S['ln1_g'], S['ln1_b'] = _ln_bwd("ln1b", dy1, xh1, rs1, P['ln1_g'], 0.5, T, D)
    grad_x = _ffn_bwd("ffn1b", dz1h, dz1, xb, a1, b1, s1, W, FFN1, gdt, G, net, True,
                      pre=(lambda: net.exchange(['w_in'], w_in_rows[1]), lambda: net.exchange(['w_in'], w_in_rows[2])))
    return loss[0, 0], grad_x, G, S


def _coords():
    return lax.axis_index("x"), lax.axis_index("y"), lax.axis_index("c")


def _flip(k, x, y, c):
    return (1 - x if k & 4 else x, 1 - y if k & 2 else y, 1 - c if k & 1 else c)


def _lin(p):
    return 4 * p[0] + 2 * p[1] + p[2]


class _Rider:
    def __init__(self, ins, out_shape, rows=None, fill=None):
        nb = len(ins)
        self.rows = rows or [None] * nb
        fill = fill or [None] * nb
        self.aliases = {nb + i: w for i, w in enumerate(w for w in range(nb) if fill[w] is not None)}
        self.ins = list(ins) + [f for f in fill if f is not None]
        self.out_shape, self.results = list(out_shape), None
        self.scratch = [pltpu.SemaphoreType.DMA((7 * nb,)), pltpu.SemaphoreType.DMA((7 * nb,)),
                        pltpu.SemaphoreType.DMA((nb,))]

    def span(self, w, ref, *slot):
        if self.rows[w] is None:
            return ref.at[slot] if slot else ref
        return ref.at[(*slot, pl.ds(*self.rows[w]))]

    def begin(self, step, n_steps, ins, outs, sems):
        @pl.when(step == 0)
        def _():
            self.start(ins, outs, sems)

        @pl.when(step == n_steps - 1)
        def _():
            self.mid(ins, outs, sems)

    def end(self, step, n_steps, ins, outs, sems):
        @pl.when(step == n_steps - 1)
        def _():
            self.finish(ins, outs, sems)

    def mid(self, ins, outs, sems):
        pass


class _GatherRider(_Rider):
    def __init__(self, blks, rows=None, fill=None):
        super().__init__(blks, [jax.ShapeDtypeStruct((N_DEV,) + b.shape, b.dtype) for b in blks], rows, fill)

    def _copies(self, x_refs, out_refs, sems):
        nb = len(self.out_shape)
        send_sems, recv_sems, local_sems = sems
        x, y, c = _coords()
        me, sibling = (x, y, c), (x, y, 1 - c)
        chips = [_flip(4, x, y, c), _flip(2, x, y, c), _flip(6, x, y, c)]
        own = [self.span(w, x_refs[w]) for w in range(nb)]

        def copy(k, w, block, to, src=None):
            slot = self.span(w, out_refs[w], _lin(block))
            return pltpu.make_async_remote_copy(
                src_ref=slot if src is None else src, dst_ref=slot, send_sem=send_sems.at[k * nb + w],
                recv_sem=recv_sems.at[k * nb + w], device_id=to, device_id_type=MESH)

        mines = [pltpu.make_async_copy(own[w], self.span(w, out_refs[w], _lin(me)), local_sems.at[w])
                 for w in range(nb)]
        first = [copy(0, w, me, sibling, src=own[w]) for w in range(nb)]
        first += [copy(1 + j, w, me, chip, src=own[w]) for w in range(nb) for j, chip in enumerate(chips)]
        landed = [(copy(1 + j, w, chip, me), copy(4 + j, w, chip, sibling)) for w in range(nb) for j, chip in enumerate(chips)]
        from_sibling = [copy(0, w, sibling, me) for w in range(nb)]
        from_sibling += [copy(4 + j, w, (chip[0], chip[1], 1 - c), me) for w in range(nb) for j, chip in enumerate(chips)]
        return mines, first, landed, from_sibling

    def start(self, ins, outs, sems):
        mines, first, _, _ = self._copies(ins, outs, sems)
        for cp in mines + first:
            cp.start()

    def mid(self, ins, outs, sems):
        for arrival, onward in self._copies(ins, outs, sems)[2]:
            arrival.wait_recv()
            onward.start()

    def finish(self, ins, outs, sems):
        mines, first, landed, from_sibling = self._copies(ins, outs, sems)
        for cp in from_sibling:
            cp.wait_recv()
        for cp in first + [onward for _, onward in landed]:
            cp.wait_send()
        for mine in mines:
            mine.wait()


class _ExchangeRider(_Rider):
    def __init__(self, gs, rows=None, fill=None):
        super().__init__(gs, [jax.ShapeDtypeStruct(g.shape, g.dtype) for g in gs], rows, fill)

    def _copies(self, g_refs, out_refs, sems):
        nb = len(self.out_shape)
        send_sems, recv_sems, local_sems = sems
        x, y, c = _coords()
        me = _lin((x, y, c))

        def copy(k, w, landing):
            peer = _flip(k, x, y, c)
            src, dst = (me, _lin(peer)) if landing else (_lin(peer), me)
            return pltpu.make_async_remote_copy(
                src_ref=self.span(w, g_refs[w], src), dst_ref=self.span(w, out_refs[w], dst),
                send_sem=send_sems.at[(k - 1) * nb + w], recv_sem=recv_sems.at[(k - 1) * nb + w],
                device_id=peer, device_id_type=MESH)

        mines = [pltpu.make_async_copy(self.span(w, g_refs[w], me), self.span(w, out_refs[w], me), local_sems.at[w])
                 for w in range(nb)]
        sends = [copy(k, w, False) for w in range(nb) for k in range(1, N_DEV)]
        landings = [copy(k, w, True) for w in range(nb) for k in range(1, N_DEV)]
        return mines, sends, landings

    def start(self, ins, outs, sems):
        mines, sends, _ = self._copies(ins, outs, sems)
        for cp in mines + sends:
            cp.start()

    def finish(self, ins, outs, sems):
        mines, sends, landings = self._copies(ins, outs, sems)
        for cp in landings:
            cp.wait_recv()
        for cp in sends:
            cp.wait_send()
        for mine in mines:
            mine.wait()


def _run_rider(name, rider):
    n_in, n_out = len(rider.ins), len(rider.out_shape)

    def body(*refs):
        ride = (refs[:n_in], refs[n_in:n_in + n_out], refs[n_in + n_out:])
        rider.start(*ride)
        rider.mid(*ride)
        rider.finish(*ride)

    rider.results = pl.pallas_call(
        body, name=name, out_shape=rider.out_shape, in_specs=[ANY] * n_in, out_specs=[ANY] * n_out,
        scratch_shapes=rider.scratch, input_output_aliases=dict(rider.aliases),
        compiler_params=pltpu.CompilerParams(has_side_effects=True),
    )(*rider.ins)
    return rider.results


def _as_matrix(name, g):
    if name == 'w_in':
        return g
    if name in COL_SHARDED:
        return jnp.transpose(g, (1, 0, 2)).reshape(g.shape[1], N_DEV * g.shape[2])
    return g.reshape(N_DEV * g.shape[1], g.shape[2])


def _by_owner(name, g):
    if name == 'w_in':
        return g
    if name in COL_SHARDED:
        return jnp.transpose(g.reshape(g.shape[0], N_DEV, g.shape[1] // N_DEV), (1, 0, 2))
    return g.reshape(N_DEV, g.shape[0] // N_DEV, g.shape[1])


class _Comm:
    def __init__(self, shards):
        self.shards, self.W, self.parts, self.partial, self.sent = shards, {}, {}, {}, {}

    def _ride(self, cls, names, srcs, part, sink):
        part = part or {}
        rider = cls(srcs, rows=[part[n][:2] if n in part else None for n in names],
                    fill=[self.partial.pop((sink, n), None) for n in names])
        rider.names, rider.sink = names, sink
        rider.unfinished = {n for n in names if n in part and not part[n][2]}
        return rider

    def gather(self, names, part=None):
        return self._ride(_GatherRider, names, [self.shards[n] for n in names], part, 'W')

    def exchange(self, names, G, part=None):
        for n in names:
            if n not in self.sent:
                self.sent[n] = _by_owner(n, G[n])
        return self._ride(_ExchangeRider, names, [self.sent[n] for n in names], part, 'parts')

    def collect(self, rider):
        for n, res in zip(rider.names, rider.results):
            if n in rider.unfinished:
                self.partial[(rider.sink, n)] = res
            elif rider.sink == 'W':
                self.W[n] = _as_matrix(n, res)
            else:
                self.parts[n] = res


def _adamw_math(p_ref, w_ref, m_ref, v_ref, g_ref, d_ref, nm_ref, nv_ref):
    c1 = 1.0 - ADAM_B1 ** ADAM_STEP
    c2 = 1.0 - ADAM_B2 ** ADAM_STEP
    g = p_ref[0].astype(F32)
    for s in range(1, p_ref.shape[0]):
        g = g + p_ref[s].astype(F32)
    nm = ADAM_B1 * m_ref[...] + (1.0 - ADAM_B1) * g
    nv = ADAM_B2 * v_ref[...] + (1.0 - ADAM_B2) * (g * g)
    g_ref[...] = g
    nm_ref[...] = nm
    nv_ref[...] = nv
    d_ref[...] = -ADAM_LR * ((nm / c1) / (jnp.sqrt(nv / c2) + ADAM_EPS) + ADAM_WD * w_ref[...])


def _adamw_vectors(parts, ws, ms, vs):
    k = len(ws)

    def body(*refs):
        for i in range(k):
            _adamw_math(refs[i], refs[k + i], refs[2 * k + i], refs[3 * k + i], *refs[4 * k + 4 * i:4 * k + 4 * i + 4])

    return pl.pallas_call(
        body, name="adamw_vectors", out_shape=[jax.ShapeDtypeStruct(w.shape, F32) for w in ws for _ in range(4)],
        compiler_params=_params(),
    )(*parts, *ws, *ms, *vs)


def _adamw(name, parts, w, m, v, tb):
    n, R, Wd = parts.shape
    assert R % tb == 0
    body = functools.partial(_adamw_math)

    row = pl.BlockSpec((tb, Wd), lambda i: (i, 0))
    return pl.pallas_call(
        body, name=name, grid=(R // tb,),
        in_specs=[pl.BlockSpec((n, tb, Wd), lambda i: (0, i, 0)), row, row, row],
        out_specs=[row, row, row, row], out_shape=[jax.ShapeDtypeStruct((R, Wd), F32)] * 4,
        compiler_params=_params(("arbitrary",)),
    )(parts, w, m, v)


def _row_tile(R, unit, cap):
    best = unit
    for t in range(unit, cap + 1, unit):
        if R % t == 0:
            best = t
    return best


def kernel(x, ffn1_w_gate, ffn1_w_up, ffn1_w_down, ln1_g, ln1_b, w_in, b_in, ret_gn_g, conv_k, conv_b, conv_ln_g, conv_ln_b, w_ret_o, w_conv_o, w_out, ln2_g, ln2_b, ffn2_w_gate, ffn2_w_up, ffn2_w_down, ln3_g, ln3_b, loss_target, m_ffn1_w_gate, m_ffn1_w_up, m_ffn1_w_down, m_ln1_g, m_ln1_b, m_w_in, m_b_in, m_ret_gn_g, m_conv_k, m_conv_b, m_conv_ln_g, m_conv_ln_b, m_w_ret_o, m_w_conv_o, m_w_out, m_ln2_g, m_ln2_b, m_ffn2_w_gate, m_ffn2_w_up, m_ffn2_w_down, m_ln3_g, m_ln3_b, v_ffn1_w_gate, v_ffn1_w_up, v_ffn1_w_down, v_ln1_g, v_ln1_b, v_w_in, v_b_in, v_ret_gn_g, v_conv_k, v_conv_b, v_conv_ln_g, v_conv_ln_b, v_w_ret_o, v_w_conv_o, v_w_out, v_ln2_g, v_ln2_b, v_ffn2_w_gate, v_ffn2_w_up, v_ffn2_w_down, v_ln3_g, v_ln3_b):
    given = dict(locals())
    wts = {n: given[n] for n in WEIGHTS}
    mom = {n: given['m_' + n] for n in WEIGHTS}
    var = {n: given['v_' + n] for n in WEIGHTS}

    def shard2d(a):
        return a.reshape(a.shape[-3] * a.shape[-2] if a.ndim == 4 else a.shape[-2], a.shape[-1])

    comm = _Comm({n: shard2d(wts[n]).astype(BF16) for n in BIG})
    P = {n: wts[n].reshape(1, -1) for n in SMALL}
    loss, grad_x, _, S = _local_step(x[0], loss_target[0], None, P, comm=comm)

    parts = comm.parts
    res = {}
    for n in BIG:
        rows, cols = parts[n].shape[1:]
        tb = rows if rows % 16 else _row_tile(rows, 16, max(16, (256 * 1024) // cols))
        res[n] = _adamw("adamw_" + n, parts[n], shard2d(wts[n]), shard2d(mom[n]), shard2d(var[n]), tb)

    vec_parts = _run_rider("gather_vector_grads", _GatherRider([S[n] for n in SMALL]))
    vec = _adamw_vectors(vec_parts, [P[n] for n in SMALL], [mom[n].reshape(1, -1) for n in SMALL],
                         [var[n].reshape(1, -1) for n in SMALL])
    for i, n in enumerate(SMALL):
        res[n] = vec[4 * i:4 * i + 4]

    loss = lax.psum(loss, ("x", "y", "c"))
    outs = [loss, grad_x[None]]
    for k in range(4):
        for n in WEIGHTS:
            outs.append(res[n][k].reshape(wts[n].shape))
    return tuple(outs)
```

```python
import functools
import math

import jax
import jax.numpy as jnp
from jax import lax
from jax.experimental import pallas as pl
from jax.experimental.pallas import tpu as pltpu

F32 = jnp.float32
BF16 = jnp.bfloat16

N_DEV = 8
LN_EPS = 1e-5
ALPHA = 2.0 ** 0.25
RET_DK = 128
RET_DV = 256
RET_CHUNK = 256
ROPE_BASE = 10000.0
CONV_WIDTH = 31
HALO = 32
ADAM_LR, ADAM_B1, ADAM_B2, ADAM_EPS, ADAM_WD, ADAM_STEP = 0.001, 0.9, 0.999, 1e-08, 0.01, 10
VMEM_LIMIT = 52 * 1024 * 1024
MESH = pl.DeviceIdType.MESH
ANY = pl.BlockSpec(memory_space=pl.ANY)

BIG = ['ffn1_w_gate', 'ffn1_w_up', 'ffn1_w_down', 'w_in', 'w_ret_o', 'w_conv_o', 'w_out',
       'ffn2_w_gate', 'ffn2_w_up', 'ffn2_w_down', 'conv_k']
COL_SHARDED = {'ffn1_w_gate', 'ffn1_w_up', 'w_in', 'ffn2_w_gate', 'ffn2_w_up', 'conv_k'}
SMALL = ['ln1_g', 'ln1_b', 'b_in', 'ret_gn_g', 'conv_b', 'conv_ln_g', 'conv_ln_b', 'ln2_g', 'ln2_b', 'ln3_g', 'ln3_b']
WEIGHTS = ['ffn1_w_gate', 'ffn1_w_up', 'ffn1_w_down', 'ln1_g', 'ln1_b', 'w_in', 'b_in', 'ret_gn_g', 'conv_k', 'conv_b',
           'conv_ln_g', 'conv_ln_b', 'w_ret_o', 'w_conv_o', 'w_out', 'ln2_g', 'ln2_b', 'ffn2_w_gate', 'ffn2_w_up',
           'ffn2_w_down', 'ln3_g', 'ln3_b']


def _params(sem=None):
    return pltpu.CompilerParams(dimension_semantics=sem, vmem_limit_bytes=VMEM_LIMIT)


def _sigmoid(x):
    return jax.nn.sigmoid(x)


def _dsilu(x, sg):
    return sg * (1.0 + x * (1.0 - sg))


def _fit(dim, want):
    if dim <= want:
        return dim
    return max(t for t in range(128, want + 1, 128) if dim % t == 0)


def _dot(a, b, ta=False, tb=False):
    dn = (((0,) if ta else (1,), (1,) if tb else (0,)), ((), ()))
    return lax.dot_general(a, b, dn, preferred_element_type=F32)


def _mm(name, As, Bs, prods, epi, out_dtypes, *, ta=False, tb=False, tm, tn, tk, extras=(), i_outer=True,
        b3=False, o3=False, rider=None):
    a0, b0 = As[0], Bs[0]
    M, K = (a0.shape[1], a0.shape[0]) if ta else a0.shape
    if b3:
        S, rows, cs = b0.shape
        N = rows if tb else S * cs
        assert K == (S * cs if tb else rows)
        tn, tk = (tn, cs) if tb else (cs, tk)
    else:
        N = b0.shape[0] if tb else b0.shape[1]
    tm, tn, tk = _fit(M, tm), _fit(N, tn), _fit(K, tk)
    assert M % tm == 0 and N % tn == 0 and K % tk == 0, (name, M, N, K, tm, tn, tk)
    gi, gj, gk = M // tm, N // tn, K // tk
    grid = (gi, gj, gk) if i_outer else (gj, gi, gk)

    def ij(g0, g1):
        return (g0, g1) if i_outer else (g1, g0)

    def amap(g0, g1, k):
        i, _ = ij(g0, g1)
        return (k, i) if ta else (i, k)

    def bmap(g0, g1, k):
        _, j = ij(g0, g1)
        return (j, k) if tb else (k, j)

    def bmap3(g0, g1, k):
        _, j = ij(g0, g1)
        return (k, j, 0) if tb else (j, k, 0)

    in_specs = [pl.BlockSpec((tk, tm) if ta else (tm, tk), amap) for _ in As]
    if b3:
        in_specs += [pl.BlockSpec((None, tn, tk) if tb else (None, tk, tn), bmap3) for _ in Bs]
    else:
        in_specs += [pl.BlockSpec((tn, tk) if tb else (tk, tn), bmap) for _ in Bs]
    args = list(As) + list(Bs)
    for arr, kind, coloff in extras:
        assert coloff % tn == 0
        off = coloff // tn
        if kind == 'mn':
            in_specs.append(pl.BlockSpec((tm, tn), lambda g0, g1, k, off=off: (ij(g0, g1)[0], ij(g0, g1)[1] + off)))
        else:
            in_specs.append(pl.BlockSpec((1, tn), lambda g0, g1, k, off=off: (0, ij(g0, g1)[1] + off)))
        args.append(arr)
    if o3:
        out_shape = [jax.ShapeDtypeStruct((gj, M, tn), dt) for dt in out_dtypes]
        out_specs = [pl.BlockSpec((None, tm, tn), lambda g0, g1, k: (ij(g0, g1)[1], ij(g0, g1)[0], 0))
                     for _ in out_dtypes]
    else:
        out_shape = [jax.ShapeDtypeStruct((M, N), dt) for dt in out_dtypes]
        out_specs = [pl.BlockSpec((tm, tn), lambda g0, g1, k: ij(g0, g1)) for _ in out_dtypes]
    n_a, n_b, n_e, n_o = len(As), len(Bs), len(extras), len(out_dtypes)
    n_p = len(prods) if gk > 1 else 0
    scratch = [pltpu.VMEM((tm, tn), F32) for _ in range(n_p)]
    if rider is not None:
        in_specs, out_specs = in_specs + [ANY] * len(rider.ins), out_specs + [ANY] * len(rider.out_shape)
        args, out_shape, scratch = args + rider.ins, out_shape + rider.out_shape, scratch + rider.scratch
    n_in, n_out = len(args), len(out_shape)

    def body(*refs):
        a_refs = refs[:n_a]
        b_refs = refs[n_a:n_a + n_b]
        e_refs = refs[n_a + n_b:n_a + n_b + n_e]
        o_refs = refs[n_in:n_in + n_o]
        acc_refs = refs[n_in + n_out:n_in + n_out + n_p]
        k = pl.program_id(2)
        if rider is not None:
            step = (pl.program_id(0) * grid[1] + pl.program_id(1)) * gk + k
            ride = (step, grid[0] * grid[1] * gk, refs[n_a + n_b + n_e:n_in], refs[n_in + n_o:n_in + n_out],
                    refs[n_in + n_out + n_p:])
            rider.begin(*ride)

        def finish(accs):
            for o, r in zip(o_refs, epi(accs, *[e[...].astype(F32) for e in e_refs])):
                o[...] = r.astype(o.dtype)

        if gk == 1:
            finish([functools.reduce(jnp.add, [_dot(a_refs[ai][...], b_refs[bi][...], ta, tb) for ai, bi in terms])
                    for terms in prods])
        else:
            @pl.when(k == 0)
            def _():
                for acc in acc_refs:
                    acc[...] = jnp.zeros_like(acc)

            for p, terms in enumerate(prods):
                for ai, bi in terms:
                    acc_refs[p][...] += _dot(a_refs[ai][...], b_refs[bi][...], ta, tb)

            @pl.when(k == gk - 1)
            def _():
                finish([acc[...] for acc in acc_refs])

        if rider is not None:
            rider.end(*ride)

    aliases = {} if rider is None else {n_a + n_b + n_e + p: n_o + o for p, o in rider.aliases.items()}
    res = pl.pallas_call(
        body, name=name, grid=grid, in_specs=in_specs, out_specs=out_specs, out_shape=out_shape,
        scratch_shapes=scratch, input_output_aliases=aliases,
        compiler_params=_params(("arbitrary", "arbitrary", "arbitrary")),
    )(*args)
    if rider is not None:
        rider.results = res[n_o:]
    return res[:n_o]


def _rows(name, fn, ins, outs, *, T, tb):
    tb = min(tb, T)
    assert T % tb == 0
    in_specs, args = [], []
    for arr, kind, width, cb in ins:
        if kind == 'r':
            in_specs.append(pl.BlockSpec((tb, width), lambda i, cb=cb: (i, cb)))
        else:
            in_specs.append(pl.BlockSpec((1, width), lambda i, cb=cb: (0, cb)))
        args.append(arr)
    out_shape, out_specs = [], []
    for kind, width, dtype in outs:
        if kind == 'r':
            out_shape.append(jax.ShapeDtypeStruct((T, width), dtype))
            out_specs.append(pl.BlockSpec((tb, width), lambda i: (i, 0)))
        elif kind == 'c':
            out_shape.append(jax.ShapeDtypeStruct((T, 1), dtype))
            out_specs.append(pl.BlockSpec((tb, 1), lambda i: (i, 0)))
        else:
            out_shape.append(jax.ShapeDtypeStruct((1, width), F32))
            out_specs.append(pl.BlockSpec((1, width), lambda i: (0, 0)))
    n_in = len(ins)

    def body(*refs):
        i = pl.program_id(0)
        vals = fn(*[r[...] for r in refs[:n_in]])
        for (kind, _, _), o, v in zip(outs, refs[n_in:], vals):
            if kind == 'a':
                @pl.when(i == 0)
                def _(o=o):
                    o[...] = jnp.zeros_like(o)

                o[...] += v
            else:
                o[...] = v.astype(o.dtype)

    return pl.pallas_call(
        body, name=name, grid=(T // tb,), in_specs=in_specs, out_specs=out_specs, out_shape=out_shape,
        compiler_params=_params(("arbitrary",)),
    )(*args)


def _colsum(v):
    return jnp.sum(v, axis=0, keepdims=True)


def _ln_stats(z):
    mu = jnp.mean(z, axis=-1, keepdims=True)
    d = z - mu
    var = jnp.mean(d * d, axis=-1, keepdims=True)
    rstd = lax.rsqrt(var + LN_EPS)
    return d * rstd, rstd


def _ln_bwd_math(dy, xhat, rstd, g):
    dxh = dy * g
    m1 = jnp.mean(dxh, axis=-1, keepdims=True)
    m2 = jnp.mean(dxh * xhat, axis=-1, keepdims=True)
    return rstd * (dxh - m1 - xhat * m2)


def _ln_fwd(name, z, g, b, T, D):
    def fn(z, g, b):
        xhat, rstd = _ln_stats(z)
        y = xhat * g + b
        return [y, y, xhat, rstd]

    return _rows(name, fn, [(z, 'r', D, 0), (g, 'v', D, 0), (b, 'v', D, 0)],
                 [('r', D, F32), ('r', D, BF16), ('r', D, F32), ('c', 1, F32)], T=T, tb=512)


def _ln_bwd(name, dy, xhat, rstd, g, scale, T, D):
    def fn(dy, xhat, rstd, g):
        dz = _ln_bwd_math(dy, xhat, rstd, g)
        return [dz, dz * scale, _colsum(dy * xhat), _colsum(dy)]

    return _rows(name, fn, [(dy, 'r', D, 0), (xhat, 'r', D, 0), (rstd, 'r', 1, 0), (g, 'v', D, 0)],
                 [('r', D, F32), ('r', D, BF16), ('a', D, F32), ('a', D, F32)], T=T, tb=512)


def _ln_loss_bwd(name, z, g, b, tgt, T, D):
    def fn(z, g, b, tgt):
        xhat, rstd = _ln_stats(z)
        err = xhat * g + b - tgt
        row_loss = 0.5 * jnp.mean(err * err, axis=-1, keepdims=True)
        loss = jnp.broadcast_to(jnp.sum(row_loss, axis=0, keepdims=True), (1, 128))
        dy = err * (1.0 / D)
        dz = _ln_bwd_math(dy, xhat, rstd, g)
        return [dz, dz * 0.5, _colsum(dy * xhat), _colsum(dy), loss]

    return _rows(name, fn, [(z, 'r', D, 0), (g, 'v', D, 0), (b, 'v', D, 0), (tgt, 'r', D, 0)],
                 [('r', D, F32), ('r', D, BF16), ('a', D, F32), ('a', D, F32), ('a', 128, F32)], T=T, tb=512)


class _Net:
    def __init__(self, comm, G):
        self.comm, self.G = comm, G

    def gather(self, names, part=None):
        return self.comm.gather(names, part) if self.comm else None

    def exchange(self, names, part=None):
        return self.comm.exchange(names, self.G, part) if self.comm else None

    def done(self, rider):
        if rider is not None:
            self.comm.collect(rider)


def _ffn_fwd(tag, xb, x, W, names, net, rider=None, rider_down=None):
    def epi_gu(accs):
        a, b = accs
        return [a, b, a * _sigmoid(a) * b]

    ng, nu, nd = names
    a, b, s = _mm(tag + "_gate_up", [xb], [W[ng], W[nu]], [[(0, 0)], [(0, 1)]], epi_gu, [BF16, BF16, BF16],
                  tm=2048, tn=256, tk=1024, rider=rider)
    net.done(rider)

    def epi_down(accs, xres):
        return [ALPHA * xres + 0.5 * accs[0]]

    rider_down = rider_down() if rider_down else None
    (z,) = _mm(tag + "_down", [s], [W[nd]], [[(0, 0)]], epi_down, [F32], tm=1024, tn=1024, tk=1408,
               extras=[(x, 'mn', 0)], rider=rider_down)
    net.done(rider_down)
    return a, b, s, z


def _ffn_bwd(tag, dzh, dz, xb, a, b, s, W, names, gdt, G, net, ride, pre=(None, None)):
    ng, nu, nd = names

    def epi_ds(accs, a, b):
        ds = accs[0]
        sg = _sigmoid(a)
        return [ds * b * _dsilu(a, sg), ds * a * sg]

    rider = pre[0]() if pre[0] else None
    da, db = _mm(tag + "_ds", [dzh], [W[nd]], [[(0, 0)]], epi_ds, [BF16, BF16], tb=True, tm=2048, tn=256, tk=1024,
                 extras=[(a, 'mn', 0), (b, 'mn', 0)], rider=rider)
    net.done(rider)
    ident = lambda accs: accs
    rider = pre[1]() if pre[1] else None
    (G[nd],) = _mm(tag + "_dwd", [s], [dzh], [[(0, 0)]], ident, [gdt], ta=True, tm=1408, tn=1024, tk=1024,
                   rider=rider)
    net.done(rider)
    if ride == 'all':
        rider = net.exchange([nd])
        (G[ng],) = _mm(tag + "_dwg", [xb], [da], [[(0, 0)]], ident, [gdt], ta=True, tm=1024, tn=1408, tk=1024,
                       rider=rider)
        net.done(rider)
        rider = net.exchange([ng])
        (G[nu],) = _mm(tag + "_dwu", [xb], [db], [[(0, 0)]], ident, [gdt], ta=True, tm=1024, tn=1408, tk=1024,
                       rider=rider)
        net.done(rider)
    else:
        G[ng], G[nu] = _mm(tag + "_dwgu", [xb], [da, db], [[(0, 0)], [(0, 1)]], ident, [gdt, gdt], ta=True,
                           tm=1024, tn=1408, tk=1024)

    def epi_dx(accs, dzres):
        return [ALPHA * dzres + accs[0]]

    rider = net.exchange([nu] if ride == 'all' else [nd]) if ride else None
    (dx,) = _mm(tag + "_dx", [da, db], [W[ng], W[nu]], [[(0, 0), (1, 1)]], epi_dx, [F32], tb=True,
                tm=1024, tn=1024, tk=1408, extras=[(dz, 'mn', 0)], rider=rider)
    net.done(rider)
    return dx


def _ret_tables(H, T):
    C = RET_CHUNK
    log_g = jnp.log(1.0 - jnp.exp2(-5.0 - jnp.arange(H, dtype=F32)))
    idx = jnp.arange(C, dtype=F32)
    diff = idx[:, None] - idx[None, :]
    dm = jnp.where(diff[None] >= 0, jnp.exp(jnp.maximum(diff, 0.0)[None] * log_g[:, None, None]), 0.0)
    xi = jnp.exp((idx[None, :] + 1.0) * log_g[:, None])[:, :, None]
    zeta = jnp.exp((C - 1.0 - idx)[None, :] * log_g[:, None])[:, :, None]
    gc = jnp.broadcast_to(jnp.exp(C * log_g)[:, None, None], (H, 1, RET_DV))
    half = RET_DK // 2
    freqs = ROPE_BASE ** (-jnp.arange(half, dtype=F32) / half)
    ang = jnp.arange(T, dtype=F32)[:, None] * freqs[None, :]
    cos, sin = jnp.cos(ang), jnp.sin(ang)
    cosf = jnp.concatenate([cos, cos], axis=1)
    sins = jnp.concatenate([-sin, sin], axis=1)
    return dm, xi, zeta, gc, cosf, sins


def _rot(x, cosf, sins):
    return x * cosf + pltpu.roll(x, RET_DK // 2, 1) * sins


def _rot_bwd(dy, cosf, sins):
    return dy * cosf + pltpu.roll(dy * sins, RET_DK // 2, 1)


RET_HB = 8


def _ret_specs(H, HB, rev, NC):
    C, G = RET_CHUNK, H // HB
    nn = (lambda n: NC - 1 - n) if rev else (lambda n: n)
    return [
        pl.BlockSpec((C, HB * RET_DK), lambda h, n: (nn(n), h)),
        pl.BlockSpec((C, HB * RET_DK), lambda h, n: (nn(n), G + h)),
        pl.BlockSpec((C, HB * RET_DV), lambda h, n: (nn(n), G + h)),
        pl.BlockSpec((C, HB * RET_DV), lambda h, n: (nn(n), 2 * G + h)),
        pl.BlockSpec((C, RET_DK), lambda h, n: (nn(n), 0)),
        pl.BlockSpec((C, RET_DK), lambda h, n: (nn(n), 0)),
        pl.BlockSpec((1, HB * RET_DV), lambda h, n: (0, h)),
        pl.BlockSpec((HB, C, C), lambda h, n: (h, 0, 0)),
        pl.BlockSpec((HB, C, 1), lambda h, n: (h, 0, 0)),
        pl.BlockSpec((HB, C, 1), lambda h, n: (h, 0, 0)),
        pl.BlockSpec((HB, 1, RET_DV), lambda h, n: (h, 0, 0)),
    ]


def _ret_fwd(proj, gn_g, tabs, H, T, rider=None):
    C, NC = RET_CHUNK, T // RET_CHUNK
    HB = min(RET_HB, H)
    dm, xi, zeta, gc, cosf, sins = tabs
    scale = RET_DK ** -0.5

    def body(q_ref, k_ref, v_ref, g_ref, cos_ref, sin_ref, gn_ref, dm_ref, xi_ref, zt_ref, gc_ref,
             r_ref, ri_ref, st_ref, state):
        @pl.when(pl.program_id(1) == 0)
        def _():
            state[...] = jnp.zeros_like(state)

        cs, sn = cos_ref[...], sin_ref[...]
        for hh in range(HB):
            qk = slice(hh * RET_DK, (hh + 1) * RET_DK)
            vv = slice(hh * RET_DV, (hh + 1) * RET_DV)
            qr = _rot(q_ref[:, qk].astype(F32), cs, sn) * scale
            kr = _rot(k_ref[:, qk].astype(F32), cs, sn)
            qb, kb, vb = qr.astype(BF16), kr.astype(BF16), v_ref[:, vv].astype(BF16)
            st = state[hh]
            stb = st.astype(BF16)
            s = _dot(qb, kb, tb=True) * dm_ref[hh]
            r = _dot(s.astype(BF16), vb) + _dot(qb, stb) * xi_ref[hh]
            st_ref[hh] = stb
            state[hh] = gc_ref[hh] * st + _dot((kr * zt_ref[hh]).astype(BF16), vb, ta=True)
            rhat, _ = _ln_stats(r)
            g = g_ref[:, vv].astype(F32)
            r_ref[:, vv] = r
            ri_ref[:, vv] = (g * _sigmoid(g) * (rhat * gn_ref[:, vv])).astype(BF16)

    VW = H * RET_DV
    return _hosted_call(
        body, rider, name="ret_fwd", grid=(H // HB, NC), in_specs=_ret_specs(H, HB, False, NC),
        out_specs=[pl.BlockSpec((C, HB * RET_DV), lambda h, n: (n, h)),
                   pl.BlockSpec((C, HB * RET_DV), lambda h, n: (n, h)),
                   pl.BlockSpec((HB, None, RET_DK, RET_DV), lambda h, n: (h, n, 0, 0))],
        out_shape=[jax.ShapeDtypeStruct((T, VW), F32), jax.ShapeDtypeStruct((T, VW), BF16),
                   jax.ShapeDtypeStruct((H, NC, RET_DK, RET_DV), BF16)],
        scratch=[pltpu.VMEM((HB, RET_DK, RET_DV), F32)],
        args=[proj, proj, proj, proj, cosf, sins, gn_g, dm, xi, zeta, gc])


def _hosted_call(body, rider, *, name, grid, in_specs, out_specs, out_shape, scratch, args):
    n_in, n_out, n_scr = len(args), len(out_shape), len(scratch)
    if rider is None:
        hosted = body
    else:
        n_ri, n_ro = len(rider.ins), len(rider.out_shape)
        in_specs, out_specs = in_specs + [ANY] * n_ri, out_specs + [ANY] * n_ro
        args, out_shape, scratch = args + rider.ins, out_shape + rider.out_shape, scratch + rider.scratch

        def hosted(*refs):
            o0, s0 = n_in + n_ri, n_in + n_ri + n_out + n_ro
            step = pl.program_id(0) * grid[1] + pl.program_id(1)
            ride = (step, grid[0] * grid[1], refs[n_in:o0], refs[o0 + n_out:s0], refs[s0 + n_scr:])
            rider.begin(*ride)
            body(*refs[:n_in], *refs[o0:o0 + n_out], *refs[s0:s0 + n_scr])
            rider.end(*ride)

    aliases = {} if rider is None else {n_in + p: n_out + o for p, o in rider.aliases.items()}
    res = pl.pallas_call(
        hosted, name=name, grid=grid, in_specs=in_specs, out_specs=out_specs, out_shape=out_shape,
        scratch_shapes=scratch, input_output_aliases=aliases, compiler_params=_params(("arbitrary", "arbitrary")),
    )(*args)
    if rider is not None:
        rider.results = res[n_out:]
    return res[:n_out]


def _ret_bwd(dri, r, states, proj, gn_g, tabs, H, T, in_w, rider=None):
    C, NC = RET_CHUNK, T // RET_CHUNK
    HB = min(RET_HB, H)
    dm, xi, zeta, gc, cosf, sins = tabs
    scale = RET_DK ** -0.5

    def body(q_ref, k_ref, v_ref, g_ref, cos_ref, sin_ref, gn_ref, dm_ref, xi_ref, zt_ref, gc_ref,
             dri_ref, r_ref, st_ref, dp_ref, dgn_ref, dstate):
        @pl.when(pl.program_id(1) == 0)
        def _():
            dstate[...] = jnp.zeros_like(dstate)
            dgn_ref[...] = jnp.zeros_like(dgn_ref)

        cs, sn = cos_ref[...], sin_ref[...]
        for hh in range(HB):
            qk = slice(hh * RET_DK, (hh + 1) * RET_DK)
            vv = slice(hh * RET_DV, (hh + 1) * RET_DV)
            qr = _rot(q_ref[:, qk].astype(F32), cs, sn) * scale
            kr = _rot(k_ref[:, qk].astype(F32), cs, sn)
            qb, kb, vb = qr.astype(BF16), kr.astype(BF16), v_ref[:, vv].astype(BF16)
            xi_c, zt_c, dmask = xi_ref[hh], zt_ref[hh], dm_ref[hh]
            rhat, rstd = _ln_stats(r_ref[:, vv])
            g, gn, dpre = g_ref[:, vv].astype(F32), gn_ref[:, vv], dri_ref[:, vv]
            sg = _sigmoid(g)
            dp_ref[:, 2 * QW + VW + hh * RET_DV:2 * QW + VW + (hh + 1) * RET_DV] = (
                dpre * (rhat * gn) * _dsilu(g, sg)).astype(BF16)
            drn = dpre * (g * sg)
            dgn_ref[:, vv] += _colsum(drn * rhat)
            drb = _ln_bwd_math(drn, rhat, rstd, gn).astype(BF16)
            ds1 = dstate[hh]
            ds1b = ds1.astype(BF16)
            sb = (_dot(qb, kb, tb=True) * dmask).astype(BF16)
            kzb = (kr * zt_c).astype(BF16)
            dp_ref[:, 2 * QW + hh * RET_DV:2 * QW + (hh + 1) * RET_DV] = (
                _dot(sb, drb, ta=True) + _dot(kzb, ds1b)).astype(BF16)
            dsb = (_dot(drb, vb, tb=True) * dmask).astype(BF16)
            dq = _dot(dsb, kb) + _dot(drb, st_ref[hh], tb=True) * xi_c
            dk = _dot(dsb, qb, ta=True) + _dot(vb, ds1b, tb=True) * zt_c
            dstate[hh] = gc_ref[hh] * ds1 + _dot((qr * xi_c).astype(BF16), drb, ta=True)
            dp_ref[:, qk] = _rot_bwd(dq * scale, cs, sn).astype(BF16)
            dp_ref[:, QW + hh * RET_DK:QW + (hh + 1) * RET_DK] = _rot_bwd(dk, cs, sn).astype(BF16)

    VW, QW = H * RET_DV, H * RET_DK
    rv = lambda n: NC - 1 - n
    in_specs = _ret_specs(H, HB, True, NC) + [
        pl.BlockSpec((C, HB * RET_DV), lambda h, n: (rv(n), h)),
        pl.BlockSpec((C, HB * RET_DV), lambda h, n: (rv(n), h)),
        pl.BlockSpec((HB, None, RET_DK, RET_DV), lambda h, n: (h, rv(n), 0, 0)),
    ]
    assert HB == H
    return _hosted_call(
        body, rider, name="ret_bwd", grid=(1, NC), in_specs=in_specs,
        out_specs=[pl.BlockSpec((C, 2 * QW + 2 * VW), lambda h, n: (rv(n), 0)),
                   pl.BlockSpec((1, VW), lambda h, n: (0, 0))],
        out_shape=[jax.ShapeDtypeStruct((T, in_w), BF16), jax.ShapeDtypeStruct((1, VW), F32)],
        scratch=[pltpu.VMEM((HB, RET_DK, RET_DV), F32)],
        args=[proj, proj, proj, proj, cosf, sins, gn_g, dm, xi, zeta, gc, dri, r, states])


CONV_CW = 128
CONV_TB = 512


SUBLANES = 8


def _shift_copies(win, shifted, tb):
    n = tb + HALO - SUBLANES
    for s in range(1, SUBLANES):
        shifted[s - 1] = win[pl.ds(s, n), :]


def _tap(win, shifted, off, tb):
    s = off % SUBLANES
    if s == 0:
        return win[pl.ds(off, tb), :]
    return shifted[s - 1, pl.ds(off - s, tb), :]


def _conv_fwd(proj, kpad, bias, off_a, CC, T, rider=None):
    tb, cw = min(CONV_TB, T), CONV_CW
    hb = tb // HALO
    ca, cb = off_a // cw, (off_a + CC) // cw

    def body(a_ref, b_ref, ap_ref, bp_ref, k_ref, bias_ref, u1_ref, win, shifted):
        i = pl.program_id(0)
        keep = (i > 0).astype(F32)
        win[0:HALO, :] = ap_ref[...].astype(F32) * _sigmoid(bp_ref[...].astype(F32)) * keep
        win[HALO:, :] = a_ref[...].astype(F32) * _sigmoid(b_ref[...].astype(F32))
        _shift_copies(win, shifted, tb)
        acc = jnp.broadcast_to(bias_ref[...], (tb, cw))
        for w in range(CONV_WIDTH):
            acc = acc + k_ref[w:w + 1, :] * _tap(win, shifted, HALO - (CONV_WIDTH - 1) + w, tb)
        u1_ref[...] = acc

    prev = lambda i: jnp.maximum(i * hb - 1, 0)
    (u1,) = _hosted_call(
        body, rider, name="conv_fwd", grid=(T // tb, CC // cw),
        in_specs=[pl.BlockSpec((tb, cw), lambda i, c: (i, ca + c)),
                  pl.BlockSpec((tb, cw), lambda i, c: (i, cb + c)),
                  pl.BlockSpec((HALO, cw), lambda i, c: (prev(i), ca + c)),
                  pl.BlockSpec((HALO, cw), lambda i, c: (prev(i), cb + c)),
                  pl.BlockSpec((HALO, cw), lambda i, c: (0, c)),
                  pl.BlockSpec((1, cw), lambda i, c: (0, c))],
        out_specs=[pl.BlockSpec((tb, cw), lambda i, c: (i, c))],
        out_shape=[jax.ShapeDtypeStruct((T, CC), F32)],
        scratch=[pltpu.VMEM((tb + HALO, cw), F32), pltpu.VMEM((SUBLANES - 1, tb + HALO - SUBLANES, cw), F32)],
        args=[proj, proj, proj, proj, kpad, bias])
    return u1


def _conv_bwd(du1, proj, kpad, off_a, CC, T, rider=None):
    tb, cw = min(CONV_TB, T), CONV_CW
    hb = tb // HALO
    nt = T // tb
    ca, cb = off_a // cw, (off_a + CC) // cw

    def body(d_ref, dn_ref, a_ref, b_ref, ap_ref, bp_ref, k_ref, da_ref, db_ref, dk_ref, winu, wind, shu, shd):
        i = pl.program_id(1)
        a, b = a_ref[...].astype(F32), b_ref[...].astype(F32)
        sgb = _sigmoid(b)
        winu[0:HALO, :] = ap_ref[...].astype(F32) * _sigmoid(bp_ref[...].astype(F32)) * (i > 0).astype(F32)
        winu[HALO:, :] = a * sgb
        d = d_ref[...]
        wind[0:tb, :] = d
        wind[tb:, :] = dn_ref[...] * (i < nt - 1).astype(F32)

        @pl.when(i == 0)
        def _():
            dk_ref[...] = jnp.zeros_like(dk_ref)

        _shift_copies(winu, shu, tb)
        _shift_copies(wind, shd, tb)
        du0 = jnp.zeros((tb, cw), F32)
        for w in range(CONV_WIDTH):
            du0 = du0 + k_ref[w:w + 1, :] * _tap(wind, shd, CONV_WIDTH - 1 - w, tb)
            dk_ref[w:w + 1, :] += _colsum(_tap(winu, shu, HALO - (CONV_WIDTH - 1) + w, tb) * d)
        da_ref[...] = (du0 * sgb).astype(BF16)
        db_ref[...] = (du0 * a * sgb * (1.0 - sgb)).astype(BF16)

    prev = lambda i: jnp.maximum(i * hb - 1, 0)
    nxt = lambda i: jnp.minimum((i + 1) * hb, T // HALO - 1)
    return _hosted_call(
        body, rider, name="conv_bwd", grid=(CC // cw, nt),
        in_specs=[pl.BlockSpec((tb, cw), lambda c, i: (i, c)),
                  pl.BlockSpec((HALO, cw), lambda c, i: (nxt(i), c)),
                  pl.BlockSpec((tb, cw), lambda c, i: (i, ca + c)),
                  pl.BlockSpec((tb, cw), lambda c, i: (i, cb + c)),
                  pl.BlockSpec((HALO, cw), lambda c, i: (prev(i), ca + c)),
                  pl.BlockSpec((HALO, cw), lambda c, i: (prev(i), cb + c)),
                  pl.BlockSpec((HALO, cw), lambda c, i: (0, c))],
        out_specs=[pl.BlockSpec((tb, cw), lambda c, i: (i, c)),
                   pl.BlockSpec((tb, cw), lambda c, i: (i, c)),
                   pl.BlockSpec((HALO, cw), lambda c, i: (0, c))],
        out_shape=[jax.ShapeDtypeStruct((T, CC), BF16), jax.ShapeDtypeStruct((T, CC), BF16),
                   jax.ShapeDtypeStruct((HALO, CC), F32)],
        scratch=[pltpu.VMEM((tb + HALO, cw), F32), pltpu.VMEM((tb + HALO, cw), F32),
                 pltpu.VMEM((SUBLANES - 1, tb + HALO - SUBLANES, cw), F32),
                 pltpu.VMEM((SUBLANES - 1, tb + HALO - SUBLANES, cw), F32)],
        args=[du1, du1, proj, proj, proj, proj, kpad])


FFN1 = ('ffn1_w_gate', 'ffn1_w_up', 'ffn1_w_down')
FFN2 = ('ffn2_w_gate', 'ffn2_w_up', 'ffn2_w_down')


def _local_step(x, tgt, W, P, gdt=BF16, comm=None):
    T, D = x.shape
    G = {}
    net = _Net(comm, G)
    if comm is not None:
        W = comm.W
        first = net.gather(['ffn1_w_gate', 'ffn1_w_up'])
        _run_rider("gather_ffn1_in", first)
        net.done(first)
    VW = P['ret_gn_g'].shape[1]
    H = VW // RET_DV
    QW = H * RET_DK
    CC = P['conv_b'].shape[1]
    off_glu = 2 * QW + 2 * VW
    off_gate = off_glu + 2 * CC
    ident = lambda accs: accs
    xb = x.astype(BF16)

    a1, b1, s1, z1 = _ffn_fwd("ffn1", xb, x, W, FFN1, net,
                              rider=net.gather(['ffn1_w_down', 'w_in'], {'w_in': (0, D // 2, False)}),
                              rider_down=lambda: net.gather(['w_in'], {'w_in': (D // 2, D // 2, True)}))
    x1, x1b, xh1, rs1 = _ln_fwd("ln1", z1, P['ln1_g'], P['ln1_b'], T, D)

    rest = net.gather(['conv_k', 'w_ret_o', 'w_conv_o', 'w_out'])
    (proj,) = _mm("w_in", [x1b], [W['w_in']], [[(0, 0)]], lambda accs, bias: [accs[0] + bias], [F32],
                  tm=2048, tn=0, tk=1024, extras=[(P['b_in'], 'n', 0)], i_outer=False, b3=True, rider=rest)
    net.done(rest)
    tabs = _ret_tables(H, T)
    rider = net.gather(['ffn2_w_gate', 'ffn2_w_up'])
    r, ret_in, states = _ret_fwd(proj, P['ret_gn_g'], tabs, H, T, rider=rider)
    net.done(rider)
    kpad = jnp.pad(W['conv_k'].astype(F32), ((0, HALO - CONV_WIDTH), (0, 0)))
    rider = net.gather(['ffn2_w_down'])
    u1 = _conv_fwd(proj, kpad, P['conv_b'], off_glu, CC, T, rider=rider)
    net.done(rider)

    def conv_ln(u1, g, b):
        xhat, rstd = _ln_stats(u1)
        u2 = xhat * g + b
        return [xhat, rstd, u2 * _sigmoid(u2)]

    xhc, rsc, u3 = _rows("conv_ln", conv_ln, [(u1, 'r', CC, 0), (P['conv_ln_g'], 'v', CC, 0), (P['conv_ln_b'], 'v', CC, 0)],
                         [('r', CC, F32), ('c', 1, F32), ('r', CC, BF16)], T=T, tb=512)
    (ret_out,) = _mm("ret_o", [ret_in], [W['w_ret_o']], [[(0, 0)]], ident, [F32], tm=1024, tn=1024, tk=2048)

    def epi_merge(accs, ret_out, gr, gc):
        conv_out = accs[0]
        return [conv_out, _sigmoid(gr) * ret_out + _sigmoid(gc) * conv_out]

    conv_out, merged = _mm("conv_o_merge", [u3], [W['w_conv_o']], [[(0, 0)]], epi_merge, [F32, BF16],
                           tm=512, tn=D, tk=1024,
                           extras=[(ret_out, 'mn', 0), (proj, 'mn', off_gate), (proj, 'mn', off_gate + D)])
    (z2,) = _mm("w_out", [merged], [W['w_out']], [[(0, 0)]], lambda accs, xr: [ALPHA * xr + accs[0]], [F32],
                tm=1024, tn=1024, tk=1024, extras=[(x1, 'mn', 0)])
    x2, x2b, xh2, rs2 = _ln_fwd("ln2", z2, P['ln2_g'], P['ln2_b'], T, D)
    a2, b2, s2, z3 = _ffn_fwd("ffn2", x2b, x2, W, FFN2, net)
    dz3, dz3h, g_ln3_g, g_ln3_b, loss = _ln_loss_bwd("ln3_loss", z3, P['ln3_g'], P['ln3_b'], tgt, T, D)

    S = {'ln3_g': g_ln3_g, 'ln3_b': g_ln3_b}
    dy2 = _ffn_bwd("ffn2b", dz3h, dz3, x2b, a2, b2, s2, W, FFN2, gdt, G, net, 'down')
    dz2, dz2b, S['ln2_g'], S['ln2_b'] = _ln_bwd("ln2b", dy2, xh2, rs2, P['ln2_g'], 1.0, T, D)

    (G['w_out'],) = _mm("d_w_out", [merged], [dz2b], [[(0, 0)]], ident, [gdt], ta=True, tm=1024, tn=1024, tk=1024)

    def epi_dmerge(accs, ret_out, conv_out, gr, gc):
        dm_ = accs[0]
        sr, sc = _sigmoid(gr), _sigmoid(gc)
        return [dm_ * sr, dm_ * sc, dm_ * ret_out * sr * (1.0 - sr), dm_ * conv_out * sc * (1.0 - sc)]

    dret_out, dconv_out, dgate_r, dgate_c = _mm(
        "d_merge", [dz2b], [W['w_out']], [[(0, 0)]], epi_dmerge, [BF16, BF16, BF16, BF16], tb=True,
        tm=512, tn=D, tk=1024,
        extras=[(ret_out, 'mn', 0), (conv_out, 'mn', 0), (proj, 'mn', off_gate), (proj, 'mn', off_gate + D)])
    (G['w_ret_o'],) = _mm("d_w_ret_o", [ret_in], [dret_out], [[(0, 0)]], ident, [gdt], ta=True, tm=1024, tn=1024, tk=1024)
    (G['w_conv_o'],) = _mm("d_w_conv_o", [u3], [dconv_out], [[(0, 0)]], ident, [gdt], ta=True, tm=1024, tn=1024, tk=1024)
    (dri,) = _mm("d_ret_in", [dret_out], [W['w_ret_o']], [[(0, 0)]], ident, [F32], tb=True, tm=1024, tn=1024, tk=1024)
    rider = net.exchange(['ffn2_w_gate', 'ffn2_w_up'])
    dproj, S['ret_gn_g'] = _ret_bwd(dri, r, states, proj, P['ret_gn_g'], tabs, H, T, proj.shape[1], rider=rider)
    net.done(rider)

    def epi_du2(accs, xhat, g, b):
        u2 = xhat * g + b
        return [accs[0] * _dsilu(u2, _sigmoid(u2))]

    (du2,) = _mm("d_u3", [dconv_out], [W['w_conv_o']], [[(0, 0)]], epi_du2, [F32], tb=True, tm=512, tn=CC, tk=1024,
                 extras=[(xhc, 'mn', 0), (P['conv_ln_g'], 'n', 0), (P['conv_ln_b'], 'n', 0)])

    def conv_ln_bwd(du2, xhat, rstd, g):
        du1 = _ln_bwd_math(du2, xhat, rstd, g)
        return [du1, _colsum(du2 * xhat), _colsum(du2), _colsum(du1)]

    du1, S['conv_ln_g'], S['conv_ln_b'], S['conv_b'] = _rows(
        "conv_ln_bwd", conv_ln_bwd, [(du2, 'r', CC, 0), (xhc, 'r', CC, 0), (rsc, 'r', 1, 0), (P['conv_ln_g'], 'v', CC, 0)],
        [('r', CC, F32), ('a', CC, F32), ('a', CC, F32), ('a', CC, F32)], T=T, tb=512)
    rider = net.exchange(['w_out', 'w_ret_o', 'w_conv_o'])
    dglu_a, dglu_b, dkpad = _conv_bwd(du1, proj, kpad, off_glu, CC, T, rider=rider)
    net.done(rider)
    G['conv_k'] = dkpad[:CONV_WIDTH].astype(gdt)

    for off, piece in ((off_glu, dglu_a), (off_glu + CC, dglu_b), (off_gate, dgate_r), (off_gate + D, dgate_c)):
        dproj = lax.dynamic_update_slice(dproj, piece, (0, off))
    IN_W = dproj.shape[1]
    (S['b_in'],) = _rows("d_b_in", lambda d: [_colsum(d.astype(F32))], [(dproj, 'r', IN_W, 0)], [('a', IN_W, F32)],
                         T=T, tb=256)
    (G['w_in'],) = _mm("d_w_in", [x1b], [dproj], [[(0, 0)]], ident, [gdt], ta=True, o3=True,
                       tm=1024, tn=W['w_in'].shape[2], tk=1024)
    cuts = [0, (9 * D) // 16, (53 * D) // 64, D]
    w_in_rows = [{'w_in': (cuts[i], cuts[i + 1] - cuts[i], i == 2)} for i in range(3)]
    rider = net.exchange(['w_in', 'conv_k'], w_in_rows[0])
    (dy1,) = _mm("d_x1", [dproj], [W['w_in']], [[(0, 0)]], lambda accs, dzr: [ALPHA * dzr + accs[0]], [F32], tb=True,
                 b3=True, tm=1024, tn=1024, tk=0, extras=[(dz2, 'mn', 0)], rider=rider)
    net.done(rider)
    dz1, dz1h, S['ln1_g'], S['ln1_b'] = _ln_bwd("ln1b", dy1, xh1, rs1, P['ln1_g'], 0.5, T, D)
    grad_x = _ffn_bwd("ffn1b", dz1h, dz1, xb, a1, b1, s1, W, FFN1, gdt, G, net, 'all',
                      pre=(lambda: net.exchange(['w_in'], w_in_rows[1]), lambda: net.exchange(['w_in'], w_in_rows[2])))
    return loss[0, 0], grad_x, G, S


def _coords():
    return lax.axis_index("x"), lax.axis_index("y"), lax.axis_index("c")


def _flip(k, x, y, c):
    return (1 - x if k & 4 else x, 1 - y if k & 2 else y, 1 - c if k & 1 else c)


def _lin(p):
    return 4 * p[0] + 2 * p[1] + p[2]


class _Rider:
    def __init__(self, ins, out_shape, rows=None, fill=None):
        nb = len(ins)
        self.rows = rows or [None] * nb
        fill = fill or [None] * nb
        self.aliases = {nb + i: w for i, w in enumerate(w for w in range(nb) if fill[w] is not None)}
        self.ins = list(ins) + [f for f in fill if f is not None]
        self.out_shape, self.results = list(out_shape), None
        self.scratch = [pltpu.SemaphoreType.DMA((7 * nb,)), pltpu.SemaphoreType.DMA((7 * nb,)),
                        pltpu.SemaphoreType.DMA((nb,))]

    def span(self, w, ref, *slot):
        if self.rows[w] is None:
            return ref.at[slot] if slot else ref
        return ref.at[(*slot, pl.ds(*self.rows[w]))]

    def begin(self, step, n_steps, ins, outs, sems):
        @pl.when(step == 0)
        def _():
            self.start(ins, outs, sems)

        @pl.when(step == n_steps - 1)
        def _():
            self.mid(ins, outs, sems)

    def end(self, step, n_steps, ins, outs, sems):
        @pl.when(step == n_steps - 1)
        def _():
            self.finish(ins, outs, sems)

    def mid(self, ins, outs, sems):
        pass


class _GatherRider(_Rider):
    def __init__(self, blks, rows=None, fill=None):
        super().__init__(blks, [jax.ShapeDtypeStruct((N_DEV,) + b.shape, b.dtype) for b in blks], rows, fill)

    def _copies(self, x_refs, out_refs, sems):
        nb = len(self.out_shape)
        send_sems, recv_sems, local_sems = sems
        x, y, c = _coords()
        me, sibling = (x, y, c), (x, y, 1 - c)
        chips = [_flip(4, x, y, c), _flip(2, x, y, c), _flip(6, x, y, c)]
        own = [self.span(w, x_refs[w]) for w in range(nb)]

        def copy(k, w, block, to, src=None):
            slot = self.span(w, out_refs[w], _lin(block))
            return pltpu.make_async_remote_copy(
                src_ref=slot if src is None else src, dst_ref=slot, send_sem=send_sems.at[k * nb + w],
                recv_sem=recv_sems.at[k * nb + w], device_id=to, device_id_type=MESH)

        mines = [pltpu.make_async_copy(own[w], self.span(w, out_refs[w], _lin(me)), local_sems.at[w])
                 for w in range(nb)]
        first = [copy(0, w, me, sibling, src=own[w]) for w in range(nb)]
        first += [copy(1 + j, w, me, chip, src=own[w]) for w in range(nb) for j, chip in enumerate(chips)]
        landed = [(copy(1 + j, w, chip, me), copy(4 + j, w, chip, sibling)) for w in range(nb) for j, chip in enumerate(chips)]
        from_sibling = [copy(0, w, sibling, me) for w in range(nb)]
        from_sibling += [copy(4 + j, w, (chip[0], chip[1], 1 - c), me) for w in range(nb) for j, chip in enumerate(chips)]
        return mines, first, landed, from_sibling

    def start(self, ins, outs, sems):
        mines, first, _, _ = self._copies(ins, outs, sems)
        for cp in mines + first:
            cp.start()

    def mid(self, ins, outs, sems):
        for arrival, onward in self._copies(ins, outs, sems)[2]:
            arrival.wait_recv()
            onward.start()

    def finish(self, ins, outs, sems):
        mines, first, landed, from_sibling = self._copies(ins, outs, sems)
        for cp in from_sibling:
            cp.wait_recv()
        for cp in first + [onward for _, onward in landed]:
            cp.wait_send()
        for mine in mines:
            mine.wait()


class _ExchangeRider(_Rider):
    def __init__(self, gs, rows=None, fill=None):
        super().__init__(gs, [jax.ShapeDtypeStruct(g.shape, g.dtype) for g in gs], rows, fill)

    def _copies(self, g_refs, out_refs, sems):
        nb = len(self.out_shape)
        send_sems, recv_sems, local_sems = sems
        x, y, c = _coords()
        me = _lin((x, y, c))

        def copy(k, w, landing):
            peer = _flip(k, x, y, c)
            src, dst = (me, _lin(peer)) if landing else (_lin(peer), me)
            return pltpu.make_async_remote_copy(
                src_ref=self.span(w, g_refs[w], src), dst_ref=self.span(w, out_refs[w], dst),
                send_sem=send_sems.at[(k - 1) * nb + w], recv_sem=recv_sems.at[(k - 1) * nb + w],
                device_id=peer, device_id_type=MESH)

        mines = [pltpu.make_async_copy(self.span(w, g_refs[w], me), self.span(w, out_refs[w], me), local_sems.at[w])
                 for w in range(nb)]
        sends = [copy(k, w, False) for w in range(nb) for k in range(1, N_DEV)]
        landings = [copy(k, w, True) for w in range(nb) for k in range(1, N_DEV)]
        return mines, sends, landings

    def start(self, ins, outs, sems):
        mines, sends, _ = self._copies(ins, outs, sems)
        for cp in mines + sends:
            cp.start()

    def finish(self, ins, outs, sems):
        mines, sends, landings = self._copies(ins, outs, sems)
        for cp in landings:
            cp.wait_recv()
        for cp in sends:
            cp.wait_send()
        for mine in mines:
            mine.wait()


def _run_rider(name, rider):
    n_in, n_out = len(rider.ins), len(rider.out_shape)

    def body(*refs):
        ride = (refs[:n_in], refs[n_in:n_in + n_out], refs[n_in + n_out:])
        rider.start(*ride)
        rider.mid(*ride)
        rider.finish(*ride)

    rider.results = pl.pallas_call(
        body, name=name, out_shape=rider.out_shape, in_specs=[ANY] * n_in, out_specs=[ANY] * n_out,
        scratch_shapes=rider.scratch, input_output_aliases=dict(rider.aliases),
        compiler_params=pltpu.CompilerParams(has_side_effects=True),
    )(*rider.ins)
    return rider.results


def _as_matrix(name, g):
    if name == 'w_in':
        return g
    if name in COL_SHARDED:
        return jnp.transpose(g, (1, 0, 2)).reshape(g.shape[1], N_DEV * g.shape[2])
    return g.reshape(N_DEV * g.shape[1], g.shape[2])


def _by_owner(name, g):
    if name == 'w_in':
        return g
    if name in COL_SHARDED:
        return jnp.transpose(g.reshape(g.shape[0], N_DEV, g.shape[1] // N_DEV), (1, 0, 2))
    return g.reshape(N_DEV, g.shape[0] // N_DEV, g.shape[1])


class _Comm:
    def __init__(self, shards):
        self.shards, self.W, self.parts, self.partial, self.sent = shards, {}, {}, {}, {}

    def _ride(self, cls, names, srcs, part, sink):
        part = part or {}
        rider = cls(srcs, rows=[part[n][:2] if n in part else None for n in names],
                    fill=[self.partial.pop((sink, n), None) for n in names])
        rider.names, rider.sink = names, sink
        rider.unfinished = {n for n in names if n in part and not part[n][2]}
        return rider

    def gather(self, names, part=None):
        return self._ride(_GatherRider, names, [self.shards[n] for n in names], part, 'W')

    def exchange(self, names, G, part=None):
        for n in names:
            if n not in self.sent:
                self.sent[n] = _by_owner(n, G[n])
        return self._ride(_ExchangeRider, names, [self.sent[n] for n in names], part, 'parts')

    def collect(self, rider):
        for n, res in zip(rider.names, rider.results):
            if n in rider.unfinished:
                self.partial[(rider.sink, n)] = res
            elif rider.sink == 'W':
                self.W[n] = _as_matrix(n, res)
            else:
                self.parts[n] = res


def _adamw_math(p_ref, w_ref, m_ref, v_ref, g_ref, d_ref, nm_ref, nv_ref):
    c1 = 1.0 - ADAM_B1 ** ADAM_STEP
    c2 = 1.0 - ADAM_B2 ** ADAM_STEP
    g = p_ref[0].astype(F32)
    for s in range(1, p_ref.shape[0]):
        g = g + p_ref[s].astype(F32)
    nm = ADAM_B1 * m_ref[...] + (1.0 - ADAM_B1) * g
    nv = ADAM_B2 * v_ref[...] + (1.0 - ADAM_B2) * (g * g)
    g_ref[...] = g
    nm_ref[...] = nm
    nv_ref[...] = nv
    d_ref[...] = -ADAM_LR * ((nm / c1) / (jnp.sqrt(nv / c2) + ADAM_EPS) + ADAM_WD * w_ref[...])


def _adamw_vectors(parts, ws, ms, vs):
    k = len(ws)

    def body(*refs):
        for i in range(k):
            _adamw_math(refs[i], refs[k + i], refs[2 * k + i], refs[3 * k + i], *refs[4 * k + 4 * i:4 * k + 4 * i + 4])

    return pl.pallas_call(
        body, name="adamw_vectors", out_shape=[jax.ShapeDtypeStruct(w.shape, F32) for w in ws for _ in range(4)],
        compiler_params=_params(),
    )(*parts, *ws, *ms, *vs)


def _adamw(name, parts, w, m, v, tb):
    n, R, Wd = parts.shape
    assert R % tb == 0
    body = functools.partial(_adamw_math)

    row = pl.BlockSpec((tb, Wd), lambda i: (i, 0))
    return pl.pallas_call(
        body, name=name, grid=(R // tb,),
        in_specs=[pl.BlockSpec((n, tb, Wd), lambda i: (0, i, 0)), row, row, row],
        out_specs=[row, row, row, row], out_shape=[jax.ShapeDtypeStruct((R, Wd), F32)] * 4,
        compiler_params=_params(("arbitrary",)),
    )(parts, w, m, v)


def _row_tile(R, unit, cap):
    best = unit
    for t in range(unit, cap + 1, unit):
        if R % t == 0:
            best = t
    return best


def kernel(x, ffn1_w_gate, ffn1_w_up, ffn1_w_down, ln1_g, ln1_b, w_in, b_in, ret_gn_g, conv_k, conv_b, conv_ln_g, conv_ln_b, w_ret_o, w_conv_o, w_out, ln2_g, ln2_b, ffn2_w_gate, ffn2_w_up, ffn2_w_down, ln3_g, ln3_b, loss_target, m_ffn1_w_gate, m_ffn1_w_up, m_ffn1_w_down, m_ln1_g, m_ln1_b, m_w_in, m_b_in, m_ret_gn_g, m_conv_k, m_conv_b, m_conv_ln_g, m_conv_ln_b, m_w_ret_o, m_w_conv_o, m_w_out, m_ln2_g, m_ln2_b, m_ffn2_w_gate, m_ffn2_w_up, m_ffn2_w_down, m_ln3_g, m_ln3_b, v_ffn1_w_gate, v_ffn1_w_up, v_ffn1_w_down, v_ln1_g, v_ln1_b, v_w_in, v_b_in, v_ret_gn_g, v_conv_k, v_conv_b, v_conv_ln_g, v_conv_ln_b, v_w_ret_o, v_w_conv_o, v_w_out, v_ln2_g, v_ln2_b, v_ffn2_w_gate, v_ffn2_w_up, v_ffn2_w_down, v_ln3_g, v_ln3_b):
    given = dict(locals())
    wts = {n: given[n] for n in WEIGHTS}
    mom = {n: given['m_' + n] for n in WEIGHTS}
    var = {n: given['v_' + n] for n in WEIGHTS}

    def shard2d(a):
        return a.reshape(a.shape[-3] * a.shape[-2] if a.ndim == 4 else a.shape[-2], a.shape[-1])

    comm = _Comm({n: shard2d(wts[n]).astype(BF16) for n in BIG})
    P = {n: wts[n].reshape(1, -1) for n in SMALL}
    loss, grad_x, _, S = _local_step(x[0], loss_target[0], None, P, comm=comm)

    parts = comm.parts
    res = {}
    for n in BIG:
        rows, cols = parts[n].shape[1:]
        tb = rows if rows % 16 else _row_tile(rows, 16, max(16, (256 * 1024) // cols))
        res[n] = _adamw("adamw_" + n, parts[n], shard2d(wts[n]), shard2d(mom[n]), shard2d(var[n]), tb)

    vec_parts = _run_rider("gather_vector_grads", _GatherRider([S[n] for n in SMALL]))
    vec = _adamw_vectors(vec_parts, [P[n] for n in SMALL], [mom[n].reshape(1, -1) for n in SMALL],
                         [var[n].reshape(1, -1) for n in SMALL])
    for i, n in enumerate(SMALL):
        res[n] = vec[4 * i:4 * i + 4]

    loss = lax.psum(loss, ("x", "y", "c"))
    outs = [loss, grad_x[None]]
    for k in range(4):
        for n in WEIGHTS:
            outs.append(res[n][k].reshape(wts[n].shape))
    return tuple(outs)
```

```python
import functools
import math

import jax
import jax.numpy as jnp
from jax import lax
from jax.experimental import pallas as pl
from jax.experimental.pallas import tpu as pltpu

F32 = jnp.float32
BF16 = jnp.bfloat16

N_DEV = 8
LN_EPS = 1e-5
ALPHA = 2.0 ** 0.25
RET_DK = 128
RET_DV = 256
RET_CHUNK = 256
ROPE_BASE = 10000.0
CONV_WIDTH = 31
HALO = 32
ADAM_LR, ADAM_B1, ADAM_B2, ADAM_EPS, ADAM_WD, ADAM_STEP = 0.001, 0.9, 0.999, 1e-08, 0.01, 10
VMEM_LIMIT = 52 * 1024 * 1024
MESH = pl.DeviceIdType.MESH
ANY = pl.BlockSpec(memory_space=pl.ANY)

BIG = ['ffn1_w_gate', 'ffn1_w_up', 'ffn1_w_down', 'w_in', 'w_ret_o', 'w_conv_o', 'w_out',
       'ffn2_w_gate', 'ffn2_w_up', 'ffn2_w_down', 'conv_k']
COL_SHARDED = {'ffn1_w_gate', 'ffn1_w_up', 'w_in', 'ffn2_w_gate', 'ffn2_w_up', 'conv_k'}
SMALL = ['ln1_g', 'ln1_b', 'b_in', 'ret_gn_g', 'conv_b', 'conv_ln_g', 'conv_ln_b', 'ln2_g', 'ln2_b', 'ln3_g', 'ln3_b']
WEIGHTS = ['ffn1_w_gate', 'ffn1_w_up', 'ffn1_w_down', 'ln1_g', 'ln1_b', 'w_in', 'b_in', 'ret_gn_g', 'conv_k', 'conv_b',
           'conv_ln_g', 'conv_ln_b', 'w_ret_o', 'w_conv_o', 'w_out', 'ln2_g', 'ln2_b', 'ffn2_w_gate', 'ffn2_w_up',
           'ffn2_w_down', 'ln3_g', 'ln3_b']


def _params(sem=None):
    return pltpu.CompilerParams(dimension_semantics=sem, vmem_limit_bytes=VMEM_LIMIT)


def _sigmoid(x):
    return jax.nn.sigmoid(x)


def _dsilu(x, sg):
    return sg * (1.0 + x * (1.0 - sg))


def _fit(dim, want):
    if dim <= want:
        return dim
    return max(t for t in range(128, want + 1, 128) if dim % t == 0)


def _dot(a, b, ta=False, tb=False):
    dn = (((0,) if ta else (1,), (1,) if tb else (0,)), ((), ()))
    return lax.dot_general(a, b, dn, preferred_element_type=F32)


def _mm(name, As, Bs, prods, epi, out_dtypes, *, ta=False, tb=False, tm, tn, tk, extras=(), i_outer=True,
        b3=False, o3=False, rider=None, bsum=False):
    a0, b0 = As[0], Bs[0]
    M, K = (a0.shape[1], a0.shape[0]) if ta else a0.shape
    if b3:
        S, rows, cs = b0.shape
        N = rows if tb else S * cs
        assert K == (S * cs if tb else rows)
        tn, tk = (tn, cs) if tb else (cs, tk)
    else:
        N = b0.shape[0] if tb else b0.shape[1]
    tm, tn, tk = _fit(M, tm), _fit(N, tn), _fit(K, tk)
    assert M % tm == 0 and N % tn == 0 and K % tk == 0, (name, M, N, K, tm, tn, tk)
    gi, gj, gk = M // tm, N // tn, K // tk
    grid = (gi, gj, gk) if i_outer else (gj, gi, gk)

    def ij(g0, g1):
        return (g0, g1) if i_outer else (g1, g0)

    def amap(g0, g1, k):
        i, _ = ij(g0, g1)
        return (k, i) if ta else (i, k)

    def bmap(g0, g1, k):
        _, j = ij(g0, g1)
        return (j, k) if tb else (k, j)

    def bmap3(g0, g1, k):
        _, j = ij(g0, g1)
        return (k, j, 0) if tb else (j, k, 0)

    in_specs = [pl.BlockSpec((tk, tm) if ta else (tm, tk), amap) for _ in As]
    if b3:
        in_specs += [pl.BlockSpec((None, tn, tk) if tb else (None, tk, tn), bmap3) for _ in Bs]
    else:
        in_specs += [pl.BlockSpec((tn, tk) if tb else (tk, tn), bmap) for _ in Bs]
    args = list(As) + list(Bs)
    for arr, kind, coloff in extras:
        assert coloff % tn == 0
        off = coloff // tn
        if kind == 'mn':
            in_specs.append(pl.BlockSpec((tm, tn), lambda g0, g1, k, off=off: (ij(g0, g1)[0], ij(g0, g1)[1] + off)))
        else:
            in_specs.append(pl.BlockSpec((1, tn), lambda g0, g1, k, off=off: (0, ij(g0, g1)[1] + off)))
        args.append(arr)
    if o3:
        out_shape = [jax.ShapeDtypeStruct((gj, M, tn), dt) for dt in out_dtypes]
        out_specs = [pl.BlockSpec((None, tm, tn), lambda g0, g1, k: (ij(g0, g1)[1], ij(g0, g1)[0], 0))
                     for _ in out_dtypes]
    else:
        out_shape = [jax.ShapeDtypeStruct((M, N), dt) for dt in out_dtypes]
        out_specs = [pl.BlockSpec((tm, tn), lambda g0, g1, k: ij(g0, g1)) for _ in out_dtypes]
    if bsum:
        assert gi == 1 and not tb and not b3
        out_shape.append(jax.ShapeDtypeStruct((1, N), F32))
        out_specs.append(pl.BlockSpec((1, tn), lambda g0, g1, k: (0, ij(g0, g1)[1])))
    n_a, n_b, n_e, n_o = len(As), len(Bs), len(extras), len(out_shape)
    n_p = len(prods) if gk > 1 else 0
    scratch = [pltpu.VMEM((tm, tn), F32) for _ in range(n_p)]
    if rider is not None:
        in_specs, out_specs = in_specs + [ANY] * len(rider.ins), out_specs + [ANY] * len(rider.out_shape)
        args, out_shape, scratch = args + rider.ins, out_shape + rider.out_shape, scratch + rider.scratch
    n_in, n_out = len(args), len(out_shape)

    def body(*refs):
        a_refs = refs[:n_a]
        b_refs = refs[n_a:n_a + n_b]
        e_refs = refs[n_a + n_b:n_a + n_b + n_e]
        o_refs = refs[n_in:n_in + n_o]
        acc_refs = refs[n_in + n_out:n_in + n_out + n_p]
        k = pl.program_id(2)
        if rider is not None:
            step = (pl.program_id(0) * grid[1] + pl.program_id(1)) * gk + k
            ride = (step, grid[0] * grid[1] * gk, refs[n_a + n_b + n_e:n_in], refs[n_in + n_o:n_in + n_out],
                    refs[n_in + n_out + n_p:])
            rider.begin(*ride)

        def finish(accs):
            for o, r in zip(o_refs, epi(accs, *[e[...].astype(F32) for e in e_refs])):
                o[...] = r.astype(o.dtype)

        if bsum:
            @pl.when(k == 0)
            def _():
                o_refs[-1][...] = jnp.zeros_like(o_refs[-1])

            o_refs[-1][...] += _colsum(b_refs[0][...].astype(F32))

        if gk == 1:
            finish([functools.reduce(jnp.add, [_dot(a_refs[ai][...], b_refs[bi][...], ta, tb) for ai, bi in terms])
                    for terms in prods])
        else:
            @pl.when(k == 0)
            def _():
                for acc in acc_refs:
                    acc[...] = jnp.zeros_like(acc)

            for p, terms in enumerate(prods):
                for ai, bi in terms:
                    acc_refs[p][...] += _dot(a_refs[ai][...], b_refs[bi][...], ta, tb)

            @pl.when(k == gk - 1)
            def _():
                finish([acc[...] for acc in acc_refs])

        if rider is not None:
            rider.end(*ride)

    aliases = {} if rider is None else {n_a + n_b + n_e + p: n_o + o for p, o in rider.aliases.items()}
    res = pl.pallas_call(
        body, name=name, grid=grid, in_specs=in_specs, out_specs=out_specs, out_shape=out_shape,
        scratch_shapes=scratch, input_output_aliases=aliases,
        compiler_params=_params(("arbitrary", "arbitrary", "arbitrary")),
    )(*args)
    if rider is not None:
        rider.results = res[n_o:]
    return res[:n_o]


def _rows(name, fn, ins, outs, *, T, tb):
    tb = min(tb, T)
    assert T % tb == 0
    in_specs, args = [], []
    for arr, kind, width, cb in ins:
        if kind == 'r':
            in_specs.append(pl.BlockSpec((tb, width), lambda i, cb=cb: (i, cb)))
        else:
            in_specs.append(pl.BlockSpec((1, width), lambda i, cb=cb: (0, cb)))
        args.append(arr)
    out_shape, out_specs = [], []
    for kind, width, dtype in outs:
        if kind == 'r':
            out_shape.append(jax.ShapeDtypeStruct((T, width), dtype))
            out_specs.append(pl.BlockSpec((tb, width), lambda i: (i, 0)))
        elif kind == 'c':
            out_shape.append(jax.ShapeDtypeStruct((T, 1), dtype))
            out_specs.append(pl.BlockSpec((tb, 1), lambda i: (i, 0)))
        else:
            out_shape.append(jax.ShapeDtypeStruct((1, width), F32))
            out_specs.append(pl.BlockSpec((1, width), lambda i: (0, 0)))
    n_in = len(ins)

    def body(*refs):
        i = pl.program_id(0)
        vals = fn(*[r[...] for r in refs[:n_in]])
        for (kind, _, _), o, v in zip(outs, refs[n_in:], vals):
            if kind == 'a':
                @pl.when(i == 0)
                def _(o=o):
                    o[...] = jnp.zeros_like(o)

                o[...] += v
            else:
                o[...] = v.astype(o.dtype)

    return pl.pallas_call(
        body, name=name, grid=(T // tb,), in_specs=in_specs, out_specs=out_specs, out_shape=out_shape,
        compiler_params=_params(("arbitrary",)),
    )(*args)


def _colsum(v):
    return jnp.sum(v, axis=0, keepdims=True)


def _ln_stats(z):
    mu = jnp.mean(z, axis=-1, keepdims=True)
    d = z - mu
    var = jnp.mean(d * d, axis=-1, keepdims=True)
    rstd = lax.rsqrt(var + LN_EPS)
    return d * rstd, rstd


def _ln_bwd_math(dy, xhat, rstd, g):
    dxh = dy * g
    m1 = jnp.mean(dxh, axis=-1, keepdims=True)
    m2 = jnp.mean(dxh * xhat, axis=-1, keepdims=True)
    return rstd * (dxh - m1 - xhat * m2)


def _ln_fwd(name, z, g, b, T, D):
    def fn(z, g, b):
        xhat, rstd = _ln_stats(z)
        y = xhat * g + b
        return [y, y, xhat, rstd]

    return _rows(name, fn, [(z, 'r', D, 0), (g, 'v', D, 0), (b, 'v', D, 0)],
                 [('r', D, F32), ('r', D, BF16), ('r', D, F32), ('c', 1, F32)], T=T, tb=512)


def _ln_bwd(name, dy, xhat, rstd, g, scale, T, D):
    def fn(dy, xhat, rstd, g):
        dz = _ln_bwd_math(dy, xhat, rstd, g)
        return [dz, dz * scale, _colsum(dy * xhat), _colsum(dy)]

    return _rows(name, fn, [(dy, 'r', D, 0), (xhat, 'r', D, 0), (rstd, 'r', 1, 0), (g, 'v', D, 0)],
                 [('r', D, F32), ('r', D, BF16), ('a', D, F32), ('a', D, F32)], T=T, tb=512)


def _ln_loss_bwd(name, z, g, b, tgt, T, D):
    def fn(z, g, b, tgt):
        xhat, rstd = _ln_stats(z)
        err = xhat * g + b - tgt
        row_loss = 0.5 * jnp.mean(err * err, axis=-1, keepdims=True)
        loss = jnp.broadcast_to(jnp.sum(row_loss, axis=0, keepdims=True), (1, 128))
        dy = err * (1.0 / D)
        dz = _ln_bwd_math(dy, xhat, rstd, g)
        return [dz, dz * 0.5, _colsum(dy * xhat), _colsum(dy), loss]

    return _rows(name, fn, [(z, 'r', D, 0), (g, 'v', D, 0), (b, 'v', D, 0), (tgt, 'r', D, 0)],
                 [('r', D, F32), ('r', D, BF16), ('a', D, F32), ('a', D, F32), ('a', 128, F32)], T=T, tb=512)


class _Net:
    def __init__(self, comm, G):
        self.comm, self.G = comm, G

    def gather(self, names, part=None):
        return self.comm.gather(names, part) if self.comm else None

    def exchange(self, names, part=None):
        return self.comm.exchange(names, self.G, part) if self.comm else None

    def done(self, rider):
        if rider is not None:
            self.comm.collect(rider)


def _ffn_fwd(tag, xb, x, W, names, net, rider=None, rider_down=None):
    def epi_gu(accs):
        a, b = accs
        return [a, b, a * _sigmoid(a) * b]

    ng, nu, nd = names
    a, b, s = _mm(tag + "_gate_up", [xb], [W[ng], W[nu]], [[(0, 0)], [(0, 1)]], epi_gu, [BF16, BF16, BF16],
                  tm=1024, tn=1408, tk=1024, rider=rider)
    net.done(rider)

    def epi_down(accs, xres):
        return [ALPHA * xres + 0.5 * accs[0]]

    rider_down = rider_down() if rider_down else None
    (z,) = _mm(tag + "_down", [s], [W[nd]], [[(0, 0)]], epi_down, [F32], tm=1024, tn=1024, tk=1408,
               extras=[(x, 'mn', 0)], rider=rider_down)
    net.done(rider_down)
    return a, b, s, z


def _ffn_bwd(tag, dzh, dz, xb, a, b, s, W, names, gdt, G, net, ride, pre=(None, None)):
    ng, nu, nd = names

    def epi_ds(accs, a, b):
        ds = accs[0]
        sg = _sigmoid(a)
        return [ds * b * _dsilu(a, sg), ds * a * sg]

    rider = pre[0]() if pre[0] else None
    da, db = _mm(tag + "_ds", [dzh], [W[nd]], [[(0, 0)]], epi_ds, [BF16, BF16], tb=True, tm=1024, tn=1408, tk=1024,
                 extras=[(a, 'mn', 0), (b, 'mn', 0)], rider=rider)
    net.done(rider)
    ident = lambda accs: accs
    rider = pre[1]() if pre[1] else None
    (G[nd],) = _mm(tag + "_dwd", [s], [dzh], [[(0, 0)]], ident, [gdt], ta=True, tm=1408, tn=1024, tk=1024,
                   rider=rider)
    net.done(rider)
    if ride == 'all':
        rider = net.exchange([nd])
        (G[ng],) = _mm(tag + "_dwg", [xb], [da], [[(0, 0)]], ident, [gdt], ta=True, tm=1024, tn=1408, tk=1024,
                       rider=rider)
        net.done(rider)
        rider = net.exchange([ng])
        (G[nu],) = _mm(tag + "_dwu", [xb], [db], [[(0, 0)]], ident, [gdt], ta=True, tm=1024, tn=1408, tk=1024,
                       rider=rider)
        net.done(rider)
    else:
        G[ng], G[nu] = _mm(tag + "_dwgu", [xb], [da, db], [[(0, 0)], [(0, 1)]], ident, [gdt, gdt], ta=True,
                           tm=1024, tn=1408, tk=1024)

    def epi_dx(accs, dzres):
        return [ALPHA * dzres + accs[0]]

    rider = net.exchange([nu] if ride == 'all' else [nd]) if ride else None
    (dx,) = _mm(tag + "_dx", [da, db], [W[ng], W[nu]], [[(0, 0), (1, 1)]], epi_dx, [F32], tb=True,
                tm=1024, tn=1024, tk=1408, extras=[(dz, 'mn', 0)], rider=rider)
    net.done(rider)
    return dx


def _ret_tables(H, T):
    C = RET_CHUNK
    log_g = jnp.log(1.0 - jnp.exp2(-5.0 - jnp.arange(H, dtype=F32)))
    idx = jnp.arange(C, dtype=F32)
    diff = idx[:, None] - idx[None, :]
    dm = jnp.where(diff[None] >= 0, jnp.exp(jnp.maximum(diff, 0.0)[None] * log_g[:, None, None]), 0.0)
    xi = jnp.exp((idx[None, :] + 1.0) * log_g[:, None])[:, :, None]
    zeta = jnp.exp((C - 1.0 - idx)[None, :] * log_g[:, None])[:, :, None]
    gc = jnp.broadcast_to(jnp.exp(C * log_g)[:, None, None], (H, 1, RET_DV))
    half = RET_DK // 2
    freqs = ROPE_BASE ** (-jnp.arange(half, dtype=F32) / half)
    ang = jnp.arange(T, dtype=F32)[:, None] * freqs[None, :]
    cos, sin = jnp.cos(ang), jnp.sin(ang)
    cosf = jnp.concatenate([cos, cos], axis=1)
    sins = jnp.concatenate([-sin, sin], axis=1)
    return dm, xi, zeta, gc, cosf, sins


def _rot(x, cosf, sins):
    return x * cosf + pltpu.roll(x, RET_DK // 2, 1) * sins


def _rot_bwd(dy, cosf, sins):
    return dy * cosf + pltpu.roll(dy * sins, RET_DK // 2, 1)


RET_HB = 8


def _ret_specs(H, HB, rev, NC):
    C, G = RET_CHUNK, H // HB
    nn = (lambda n: NC - 1 - n) if rev else (lambda n: n)
    return [
        pl.BlockSpec((C, HB * RET_DK), lambda h, n: (nn(n), h)),
        pl.BlockSpec((C, HB * RET_DK), lambda h, n: (nn(n), G + h)),
        pl.BlockSpec((C, HB * RET_DV), lambda h, n: (nn(n), G + h)),
        pl.BlockSpec((C, HB * RET_DV), lambda h, n: (nn(n), 2 * G + h)),
        pl.BlockSpec((C, RET_DK), lambda h, n: (nn(n), 0)),
        pl.BlockSpec((C, RET_DK), lambda h, n: (nn(n), 0)),
        pl.BlockSpec((1, HB * RET_DV), lambda h, n: (0, h)),
        pl.BlockSpec((HB, C, C), lambda h, n: (h, 0, 0)),
        pl.BlockSpec((HB, C, 1), lambda h, n: (h, 0, 0)),
        pl.BlockSpec((HB, C, 1), lambda h, n: (h, 0, 0)),
        pl.BlockSpec((HB, 1, RET_DV), lambda h, n: (h, 0, 0)),
    ]


def _ret_fwd(proj, gn_g, tabs, H, T, rider=None):
    C, NC = RET_CHUNK, T // RET_CHUNK
    HB = min(RET_HB, H)
    dm, xi, zeta, gc, cosf, sins = tabs
    scale = RET_DK ** -0.5

    def body(q_ref, k_ref, v_ref, g_ref, cos_ref, sin_ref, gn_ref, dm_ref, xi_ref, zt_ref, gc_ref,
             r_ref, ri_ref, st_ref, state):
        @pl.when(pl.program_id(1) == 0)
        def _():
            state[...] = jnp.zeros_like(state)

        cs, sn = cos_ref[...], sin_ref[...]
        for hh in range(HB):
            qk = slice(hh * RET_DK, (hh + 1) * RET_DK)
            vv = slice(hh * RET_DV, (hh + 1) * RET_DV)
            qr = _rot(q_ref[:, qk].astype(F32), cs, sn) * scale
            kr = _rot(k_ref[:, qk].astype(F32), cs, sn)
            qb, kb, vb = qr.astype(BF16), kr.astype(BF16), v_ref[:, vv].astype(BF16)
            st = state[hh]
            stb = st.astype(BF16)
            s = _dot(qb, kb, tb=True) * dm_ref[hh]
            r = _dot(s.astype(BF16), vb) + _dot(qb, stb) * xi_ref[hh]
            st_ref[hh] = stb
            state[hh] = gc_ref[hh] * st + _dot((kr * zt_ref[hh]).astype(BF16), vb, ta=True)
            rhat, _ = _ln_stats(r)
            g = g_ref[:, vv].astype(F32)
            r_ref[:, vv] = r
            ri_ref[:, vv] = (g * _sigmoid(g) * (rhat * gn_ref[:, vv])).astype(BF16)

    VW = H * RET_DV
    return _hosted_call(
        body, rider, name="ret_fwd", grid=(H // HB, NC), in_specs=_ret_specs(H, HB, False, NC),
        out_specs=[pl.BlockSpec((C, HB * RET_DV), lambda h, n: (n, h)),
                   pl.BlockSpec((C, HB * RET_DV), lambda h, n: (n, h)),
                   pl.BlockSpec((HB, None, RET_DK, RET_DV), lambda h, n: (h, n, 0, 0))],
        out_shape=[jax.ShapeDtypeStruct((T, VW), F32), jax.ShapeDtypeStruct((T, VW), BF16),
                   jax.ShapeDtypeStruct((H, NC, RET_DK, RET_DV), BF16)],
        scratch=[pltpu.VMEM((HB, RET_DK, RET_DV), F32)],
        args=[proj, proj, proj, proj, cosf, sins, gn_g, dm, xi, zeta, gc])


def _hosted_call(body, rider, *, name, grid, in_specs, out_specs, out_shape, scratch, args):
    n_in, n_out, n_scr = len(args), len(out_shape), len(scratch)
    if rider is None:
        hosted = body
    else:
        n_ri, n_ro = len(rider.ins), len(rider.out_shape)
        in_specs, out_specs = in_specs + [ANY] * n_ri, out_specs + [ANY] * n_ro
        args, out_shape, scratch = args + rider.ins, out_shape + rider.out_shape, scratch + rider.scratch

        def hosted(*refs):
            o0, s0 = n_in + n_ri, n_in + n_ri + n_out + n_ro
            step = pl.program_id(0) * grid[1] + pl.program_id(1)
            ride = (step, grid[0] * grid[1], refs[n_in:o0], refs[o0 + n_out:s0], refs[s0 + n_scr:])
            rider.begin(*ride)
            body(*refs[:n_in], *refs[o0:o0 + n_out], *refs[s0:s0 + n_scr])
            rider.end(*ride)

    aliases = {} if rider is None else {n_in + p: n_out + o for p, o in rider.aliases.items()}
    res = pl.pallas_call(
        hosted, name=name, grid=grid, in_specs=in_specs, out_specs=out_specs, out_shape=out_shape,
        scratch_shapes=scratch, input_output_aliases=aliases, compiler_params=_params(("arbitrary", "arbitrary")),
    )(*args)
    if rider is not None:
        rider.results = res[n_out:]
    return res[:n_out]


def _ret_bwd(dri, r, states, proj, gn_g, tabs, H, T, in_w, rider=None):
    C, NC = RET_CHUNK, T // RET_CHUNK
    HB = min(RET_HB, H)
    dm, xi, zeta, gc, cosf, sins = tabs
    scale = RET_DK ** -0.5

    def body(q_ref, k_ref, v_ref, g_ref, cos_ref, sin_ref, gn_ref, dm_ref, xi_ref, zt_ref, gc_ref,
             dri_ref, r_ref, st_ref, dp_ref, dgn_ref, dstate):
        @pl.when(pl.program_id(1) == 0)
        def _():
            dstate[...] = jnp.zeros_like(dstate)
            dgn_ref[...] = jnp.zeros_like(dgn_ref)

        cs, sn = cos_ref[...], sin_ref[...]
        for hh in range(HB):
            qk = slice(hh * RET_DK, (hh + 1) * RET_DK)
            vv = slice(hh * RET_DV, (hh + 1) * RET_DV)
            qr = _rot(q_ref[:, qk].astype(F32), cs, sn) * scale
            kr = _rot(k_ref[:, qk].astype(F32), cs, sn)
            qb, kb, vb = qr.astype(BF16), kr.astype(BF16), v_ref[:, vv].astype(BF16)
            xi_c, zt_c, dmask = xi_ref[hh], zt_ref[hh], dm_ref[hh]
            rhat, rstd = _ln_stats(r_ref[:, vv])
            g, gn, dpre = g_ref[:, vv].astype(F32), gn_ref[:, vv], dri_ref[:, vv]
            sg = _sigmoid(g)
            dp_ref[:, 2 * QW + VW + hh * RET_DV:2 * QW + VW + (hh + 1) * RET_DV] = (
                dpre * (rhat * gn) * _dsilu(g, sg)).astype(BF16)
            drn = dpre * (g * sg)
            dgn_ref[:, vv] += _colsum(drn * rhat)
            drb = _ln_bwd_math(drn, rhat, rstd, gn).astype(BF16)
            ds1 = dstate[hh]
            ds1b = ds1.astype(BF16)
            sb = (_dot(qb, kb, tb=True) * dmask).astype(BF16)
            kzb = (kr * zt_c).astype(BF16)
            dp_ref[:, 2 * QW + hh * RET_DV:2 * QW + (hh + 1) * RET_DV] = (
                _dot(sb, drb, ta=True) + _dot(kzb, ds1b)).astype(BF16)
            dsb = (_dot(drb, vb, tb=True) * dmask).astype(BF16)
            dq = _dot(dsb, kb) + _dot(drb, st_ref[hh], tb=True) * xi_c
            dk = _dot(dsb, qb, ta=True) + _dot(vb, ds1b, tb=True) * zt_c
            dstate[hh] = gc_ref[hh] * ds1 + _dot((qr * xi_c).astype(BF16), drb, ta=True)
            dp_ref[:, qk] = _rot_bwd(dq * scale, cs, sn).astype(BF16)
            dp_ref[:, QW + hh * RET_DK:QW + (hh + 1) * RET_DK] = _rot_bwd(dk, cs, sn).astype(BF16)

    VW, QW = H * RET_DV, H * RET_DK
    rv = lambda n: NC - 1 - n
    in_specs = _ret_specs(H, HB, True, NC) + [
        pl.BlockSpec((C, HB * RET_DV), lambda h, n: (rv(n), h)),
        pl.BlockSpec((C, HB * RET_DV), lambda h, n: (rv(n), h)),
        pl.BlockSpec((HB, None, RET_DK, RET_DV), lambda h, n: (h, rv(n), 0, 0)),
    ]
    assert HB == H
    return _hosted_call(
        body, rider, name="ret_bwd", grid=(1, NC), in_specs=in_specs,
        out_specs=[pl.BlockSpec((C, 2 * QW + 2 * VW), lambda h, n: (rv(n), 0)),
                   pl.BlockSpec((1, VW), lambda h, n: (0, 0))],
        out_shape=[jax.ShapeDtypeStruct((T, in_w), BF16), jax.ShapeDtypeStruct((1, VW), F32)],
        scratch=[pltpu.VMEM((HB, RET_DK, RET_DV), F32)],
        args=[proj, proj, proj, proj, cosf, sins, gn_g, dm, xi, zeta, gc, dri, r, states])


CONV_CW = 256
CONV_TB = 512


SUBLANES = 8


def _shift_copies(win, shifted, tb):
    n = tb + HALO - SUBLANES
    for s in range(1, SUBLANES):
        shifted[s - 1] = win[pl.ds(s, n), :]


def _tap(win, shifted, off, tb):
    s = off % SUBLANES
    if s == 0:
        return win[pl.ds(off, tb), :]
    return shifted[s - 1, pl.ds(off - s, tb), :]


def _conv_fwd(proj, kpad, bias, off_a, CC, T, rider=None):
    tb, cw = min(CONV_TB, T), CONV_CW
    hb = tb // HALO
    ca, cb = off_a // cw, (off_a + CC) // cw

    def body(a_ref, b_ref, ap_ref, bp_ref, k_ref, bias_ref, u1_ref, win, shifted):
        i = pl.program_id(0)
        keep = (i > 0).astype(F32)
        win[0:HALO, :] = ap_ref[...].astype(F32) * _sigmoid(bp_ref[...].astype(F32)) * keep
        win[HALO:, :] = a_ref[...].astype(F32) * _sigmoid(b_ref[...].astype(F32))
        _shift_copies(win, shifted, tb)
        acc = jnp.broadcast_to(bias_ref[...], (tb, cw))
        for w in range(CONV_WIDTH):
            acc = acc + k_ref[w:w + 1, :] * _tap(win, shifted, HALO - (CONV_WIDTH - 1) + w, tb)
        u1_ref[...] = acc

    prev = lambda i: jnp.maximum(i * hb - 1, 0)
    (u1,) = _hosted_call(
        body, rider, name="conv_fwd", grid=(T // tb, CC // cw),
        in_specs=[pl.BlockSpec((tb, cw), lambda i, c: (i, ca + c)),
                  pl.BlockSpec((tb, cw), lambda i, c: (i, cb + c)),
                  pl.BlockSpec((HALO, cw), lambda i, c: (prev(i), ca + c)),
                  pl.BlockSpec((HALO, cw), lambda i, c: (prev(i), cb + c)),
                  pl.BlockSpec((HALO, cw), lambda i, c: (0, c)),
                  pl.BlockSpec((1, cw), lambda i, c: (0, c))],
        out_specs=[pl.BlockSpec((tb, cw), lambda i, c: (i, c))],
        out_shape=[jax.ShapeDtypeStruct((T, CC), F32)],
        scratch=[pltpu.VMEM((tb + HALO, cw), F32), pltpu.VMEM((SUBLANES - 1, tb + HALO - SUBLANES, cw), F32)],
        args=[proj, proj, proj, proj, kpad, bias])
    return u1


def _conv_bwd(du1, proj, kpad, off_a, CC, T, rider=None):
    tb, cw = min(CONV_TB, T), CONV_CW
    hb = tb // HALO
    nt = T // tb
    ca, cb = off_a // cw, (off_a + CC) // cw

    def body(d_ref, dn_ref, a_ref, b_ref, ap_ref, bp_ref, k_ref, da_ref, db_ref, dk_ref, winu, wind, shu, shd):
        i = pl.program_id(1)
        a, b = a_ref[...].astype(F32), b_ref[...].astype(F32)
        sgb = _sigmoid(b)
        winu[0:HALO, :] = ap_ref[...].astype(F32) * _sigmoid(bp_ref[...].astype(F32)) * (i > 0).astype(F32)
        winu[HALO:, :] = a * sgb
        d = d_ref[...]
        wind[0:tb, :] = d
        wind[tb:, :] = dn_ref[...] * (i < nt - 1).astype(F32)

        @pl.when(i == 0)
        def _():
            dk_ref[...] = jnp.zeros_like(dk_ref)

        _shift_copies(winu, shu, tb)
        _shift_copies(wind, shd, tb)
        du0 = jnp.zeros((tb, cw), F32)
        for w in range(CONV_WIDTH):
            du0 = du0 + k_ref[w:w + 1, :] * _tap(wind, shd, CONV_WIDTH - 1 - w, tb)
            dk_ref[w:w + 1, :] += _colsum(_tap(winu, shu, HALO - (CONV_WIDTH - 1) + w, tb) * d)
        da_ref[...] = (du0 * sgb).astype(BF16)
        db_ref[...] = (du0 * a * sgb * (1.0 - sgb)).astype(BF16)

    prev = lambda i: jnp.maximum(i * hb - 1, 0)
    nxt = lambda i: jnp.minimum((i + 1) * hb, T // HALO - 1)
    return _hosted_call(
        body, rider, name="conv_bwd", grid=(CC // cw, nt),
        in_specs=[pl.BlockSpec((tb, cw), lambda c, i: (i, c)),
                  pl.BlockSpec((HALO, cw), lambda c, i: (nxt(i), c)),
                  pl.BlockSpec((tb, cw), lambda c, i: (i, ca + c)),
                  pl.BlockSpec((tb, cw), lambda c, i: (i, cb + c)),
                  pl.BlockSpec((HALO, cw), lambda c, i: (prev(i), ca + c)),
                  pl.BlockSpec((HALO, cw), lambda c, i: (prev(i), cb + c)),
                  pl.BlockSpec((HALO, cw), lambda c, i: (0, c))],
        out_specs=[pl.BlockSpec((tb, cw), lambda c, i: (i, c)),
                   pl.BlockSpec((tb, cw), lambda c, i: (i, c)),
                   pl.BlockSpec((HALO, cw), lambda c, i: (0, c))],
        out_shape=[jax.ShapeDtypeStruct((T, CC), BF16), jax.ShapeDtypeStruct((T, CC), BF16),
                   jax.ShapeDtypeStruct((HALO, CC), F32)],
        scratch=[pltpu.VMEM((tb + HALO, cw), F32), pltpu.VMEM((tb + HALO, cw), F32),
                 pltpu.VMEM((SUBLANES - 1, tb + HALO - SUBLANES, cw), F32),
                 pltpu.VMEM((SUBLANES - 1, tb + HALO - SUBLANES, cw), F32)],
        args=[du1, du1, proj, proj, proj, proj, kpad])


FFN1 = ('ffn1_w_gate', 'ffn1_w_up', 'ffn1_w_down')
FFN2 = ('ffn2_w_gate', 'ffn2_w_up', 'ffn2_w_down')


def _local_step(x, tgt, W, P, gdt=BF16, comm=None):
    T, D = x.shape
    G = {}
    net = _Net(comm, G)
    if comm is not None:
        W = comm.W
        first = net.gather(['ffn1_w_gate', 'ffn1_w_up'])
        _run_rider("gather_ffn1_in", first)
        net.done(first)
    VW = P['ret_gn_g'].shape[1]
    H = VW // RET_DV
    QW = H * RET_DK
    CC = P['conv_b'].shape[1]
    off_glu = 2 * QW + 2 * VW
    off_gate = off_glu + 2 * CC
    ident = lambda accs: accs
    xb = x.astype(BF16)

    a1, b1, s1, z1 = _ffn_fwd("ffn1", xb, x, W, FFN1, net,
                              rider=net.gather(['ffn1_w_down', 'w_in'], {'w_in': (0, D // 2, False)}),
                              rider_down=lambda: net.gather(['w_in'], {'w_in': (D // 2, D // 2, True)}))
    x1, x1b, xh1, rs1 = _ln_fwd("ln1", z1, P['ln1_g'], P['ln1_b'], T, D)

    rest = net.gather(['conv_k', 'w_ret_o', 'w_conv_o', 'w_out'])
    (proj,) = _mm("w_in", [x1b], [W['w_in']], [[(0, 0)]], lambda accs, bias: [accs[0] + bias], [F32],
                  tm=2048, tn=0, tk=1024, extras=[(P['b_in'], 'n', 0)], i_outer=False, b3=True, rider=rest)
    net.done(rest)
    tabs = _ret_tables(H, T)
    rider = net.gather(['ffn2_w_gate', 'ffn2_w_up'])
    r, ret_in, states = _ret_fwd(proj, P['ret_gn_g'], tabs, H, T, rider=rider)
    net.done(rider)
    kpad = jnp.pad(W['conv_k'].astype(F32), ((0, HALO - CONV_WIDTH), (0, 0)))
    rider = net.gather(['ffn2_w_down'])
    u1 = _conv_fwd(proj, kpad, P['conv_b'], off_glu, CC, T, rider=rider)
    net.done(rider)

    def conv_ln(u1, g, b):
        xhat, rstd = _ln_stats(u1)
        u2 = xhat * g + b
        return [xhat, rstd, u2 * _sigmoid(u2)]

    xhc, rsc, u3 = _rows("conv_ln", conv_ln, [(u1, 'r', CC, 0), (P['conv_ln_g'], 'v', CC, 0), (P['conv_ln_b'], 'v', CC, 0)],
                         [('r', CC, F32), ('c', 1, F32), ('r', CC, BF16)], T=T, tb=512)
    (ret_out,) = _mm("ret_o", [ret_in], [W['w_ret_o']], [[(0, 0)]], ident, [F32], tm=1024, tn=1024, tk=2048)

    def epi_merge(accs, ret_out, gr, gc):
        conv_out = accs[0]
        return [conv_out, _sigmoid(gr) * ret_out + _sigmoid(gc) * conv_out]

    conv_out, merged = _mm("conv_o_merge", [u3], [W['w_conv_o']], [[(0, 0)]], epi_merge, [F32, BF16],
                           tm=512, tn=D, tk=1024,
                           extras=[(ret_out, 'mn', 0), (proj, 'mn', off_gate), (proj, 'mn', off_gate + D)])
    (z2,) = _mm("w_out", [merged], [W['w_out']], [[(0, 0)]], lambda accs, xr: [ALPHA * xr + accs[0]], [F32],
                tm=1024, tn=1024, tk=1024, extras=[(x1, 'mn', 0)])
    x2, x2b, xh2, rs2 = _ln_fwd("ln2", z2, P['ln2_g'], P['ln2_b'], T, D)
    a2, b2, s2, z3 = _ffn_fwd("ffn2", x2b, x2, W, FFN2, net)
    dz3, dz3h, g_ln3_g, g_ln3_b, loss = _ln_loss_bwd("ln3_loss", z3, P['ln3_g'], P['ln3_b'], tgt, T, D)

    S = {'ln3_g': g_ln3_g, 'ln3_b': g_ln3_b}
    dy2 = _ffn_bwd("ffn2b", dz3h, dz3, x2b, a2, b2, s2, W, FFN2, gdt, G, net, 'down')
    dz2, dz2b, S['ln2_g'], S['ln2_b'] = _ln_bwd("ln2b", dy2, xh2, rs2, P['ln2_g'], 1.0, T, D)

    (G['w_out'],) = _mm("d_w_out", [merged], [dz2b], [[(0, 0)]], ident, [gdt], ta=True, tm=1024, tn=1024, tk=1024)

    def epi_dmerge(accs, ret_out, conv_out, gr, gc):
        dm_ = accs[0]
        sr, sc = _sigmoid(gr), _sigmoid(gc)
        return [dm_ * sr, dm_ * sc, dm_ * ret_out * sr * (1.0 - sr), dm_ * conv_out * sc * (1.0 - sc)]

    dret_out, dconv_out, dgate_r, dgate_c = _mm(
        "d_merge", [dz2b], [W['w_out']], [[(0, 0)]], epi_dmerge, [BF16, BF16, BF16, BF16], tb=True,
        tm=512, tn=D, tk=1024,
        extras=[(ret_out, 'mn', 0), (conv_out, 'mn', 0), (proj, 'mn', off_gate), (proj, 'mn', off_gate + D)])
    (G['w_ret_o'],) = _mm("d_w_ret_o", [ret_in], [dret_out], [[(0, 0)]], ident, [gdt], ta=True, tm=1024, tn=1024, tk=1024)
    (G['w_conv_o'],) = _mm("d_w_conv_o", [u3], [dconv_out], [[(0, 0)]], ident, [gdt], ta=True, tm=1024, tn=1024, tk=1024)
    (dri,) = _mm("d_ret_in", [dret_out], [W['w_ret_o']], [[(0, 0)]], ident, [F32], tb=True, tm=1024, tn=1024, tk=1024)
    rider = net.exchange(['ffn2_w_gate', 'ffn2_w_up'])
    dproj, S['ret_gn_g'] = _ret_bwd(dri, r, states, proj, P['ret_gn_g'], tabs, H, T, proj.shape[1], rider=rider)
    net.done(rider)

    def epi_du2(accs, xhat, g, b):
        u2 = xhat * g + b
        return [accs[0] * _dsilu(u2, _sigmoid(u2))]

    (du2,) = _mm("d_u3", [dconv_out], [W['w_conv_o']], [[(0, 0)]], epi_du2, [F32], tb=True, tm=512, tn=CC, tk=1024,
                 extras=[(xhc, 'mn', 0), (P['conv_ln_g'], 'n', 0), (P['conv_ln_b'], 'n', 0)])

    def conv_ln_bwd(du2, xhat, rstd, g):
        du1 = _ln_bwd_math(du2, xhat, rstd, g)
        return [du1, _colsum(du2 * xhat), _colsum(du2), _colsum(du1)]

    du1, S['conv_ln_g'], S['conv_ln_b'], S['conv_b'] = _rows(
        "conv_ln_bwd", conv_ln_bwd, [(du2, 'r', CC, 0), (xhc, 'r', CC, 0), (rsc, 'r', 1, 0), (P['conv_ln_g'], 'v', CC, 0)],
        [('r', CC, F32), ('a', CC, F32), ('a', CC, F32), ('a', CC, F32)], T=T, tb=512)
    rider = net.exchange(['w_out', 'w_ret_o', 'w_conv_o'])
    dglu_a, dglu_b, dkpad = _conv_bwd(du1, proj, kpad, off_glu, CC, T, rider=rider)
    net.done(rider)
    G['conv_k'] = dkpad[:CONV_WIDTH].astype(gdt)

    for off, piece in ((off_glu, dglu_a), (off_glu + CC, dglu_b), (off_gate, dgate_r), (off_gate + D, dgate_c)):
        dproj = lax.dynamic_update_slice(dproj, piece, (0, off))
    IN_W = dproj.shape[1]
    G['w_in'], S['b_in'] = _mm("d_w_in", [x1b], [dproj], [[(0, 0)]], ident, [gdt], ta=True, o3=True, bsum=True,
                               tm=1024, tn=W['w_in'].shape[2], tk=1024)
    cuts = [0, (9 * D) // 16, (53 * D) // 64, D]
    w_in_rows = [{'w_in': (cuts[i], cuts[i + 1] - cuts[i], i == 2)} for i in range(3)]
    rider = net.exchange(['w_in', 'conv_k'], w_in_rows[0])
    (dy1,) = _mm("d_x1", [dproj], [W['w_in']], [[(0, 0)]], lambda accs, dzr: [ALPHA * dzr + accs[0]], [F32], tb=True,
                 b3=True, tm=1024, tn=1024, tk=0, extras=[(dz2, 'mn', 0)], rider=rider)
    net.done(rider)
    dz1, dz1h, S['ln1_g'], S['ln1_b'] = _ln_bwd("ln1b", dy1, xh1, rs1, P['ln1_g'], 0.5, T, D)
    grad_x = _ffn_bwd("ffn1b", dz1h, dz1, xb, a1, b1, s1, W, FFN1, gdt, G, net, 'all',
                      pre=(lambda: net.exchange(['w_in'], w_in_rows[1]), lambda: net.exchange(['w_in'], w_in_rows[2])))
    return loss, grad_x, G, S


def _coords():
    return lax.axis_index("x"), lax.axis_index("y"), lax.axis_index("c")


def _flip(k, x, y, c):
    return (1 - x if k & 4 else x, 1 - y if k & 2 else y, 1 - c if k & 1 else c)


def _lin(p):
    return 4 * p[0] + 2 * p[1] + p[2]


class _Rider:
    def __init__(self, ins, out_shape, rows=None, fill=None):
        nb = len(ins)
        self.rows = rows or [None] * nb
        fill = fill or [None] * nb
        self.aliases = {nb + i: w for i, w in enumerate(w for w in range(nb) if fill[w] is not None)}
        self.ins = list(ins) + [f for f in fill if f is not None]
        self.out_shape, self.results = list(out_shape), None
        self.scratch = [pltpu.SemaphoreType.DMA((7 * nb,)), pltpu.SemaphoreType.DMA((7 * nb,)),
                        pltpu.SemaphoreType.DMA((nb,))]

    def span(self, w, ref, *slot):
        if self.rows[w] is None:
            return ref.at[slot] if slot else ref
        return ref.at[(*slot, pl.ds(*self.rows[w]))]

    def begin(self, step, n_steps, ins, outs, sems):
        @pl.when(step == 0)
        def _():
            self.start(ins, outs, sems)

        @pl.when(step == n_steps - 1)
        def _():
            self.mid(ins, outs, sems)

    def end(self, step, n_steps, ins, outs, sems):
        @pl.when(step == n_steps - 1)
        def _():
            self.finish(ins, outs, sems)

    def mid(self, ins, outs, sems):
        pass


class _GatherRider(_Rider):
    def __init__(self, blks, rows=None, fill=None):
        super().__init__(blks, [jax.ShapeDtypeStruct((N_DEV,) + b.shape, b.dtype) for b in blks], rows, fill)

    def _copies(self, x_refs, out_refs, sems):
        nb = len(self.out_shape)
        send_sems, recv_sems, local_sems = sems
        x, y, c = _coords()
        me, sibling = (x, y, c), (x, y, 1 - c)
        chips = [_flip(4, x, y, c), _flip(2, x, y, c), _flip(6, x, y, c)]
        own = [self.span(w, x_refs[w]) for w in range(nb)]

        def copy(k, w, block, to, src=None):
            slot = self.span(w, out_refs[w], _lin(block))
            return pltpu.make_async_remote_copy(
                src_ref=slot if src is None else src, dst_ref=slot, send_sem=send_sems.at[k * nb + w],
                recv_sem=recv_sems.at[k * nb + w], device_id=to, device_id_type=MESH)

        mines = [pltpu.make_async_copy(own[w], self.span(w, out_refs[w], _lin(me)), local_sems.at[w])
                 for w in range(nb)]
        first = [copy(0, w, me, sibling, src=own[w]) for w in range(nb)]
        first += [copy(1 + j, w, me, chip, src=own[w]) for w in range(nb) for j, chip in enumerate(chips)]
        landed = [(copy(1 + j, w, chip, me), copy(4 + j, w, chip, sibling)) for w in range(nb) for j, chip in enumerate(chips)]
        from_sibling = [copy(0, w, sibling, me) for w in range(nb)]
        from_sibling += [copy(4 + j, w, (chip[0], chip[1], 1 - c), me) for w in range(nb) for j, chip in enumerate(chips)]
        return mines, first, landed, from_sibling

    def start(self, ins, outs, sems):
        mines, first, _, _ = self._copies(ins, outs, sems)
        for cp in mines + first:
            cp.start()

    def mid(self, ins, outs, sems):
        for arrival, onward in self._copies(ins, outs, sems)[2]:
            arrival.wait_recv()
            onward.start()

    def finish(self, ins, outs, sems):
        mines, first, landed, from_sibling = self._copies(ins, outs, sems)
        for cp in from_sibling:
            cp.wait_recv()
        for cp in first + [onward for _, onward in landed]:
            cp.wait_send()
        for mine in mines:
            mine.wait()


class _ExchangeRider(_Rider):
    def __init__(self, gs, rows=None, fill=None):
        super().__init__(gs, [jax.ShapeDtypeStruct(g.shape, g.dtype) for g in gs], rows, fill)

    def _copies(self, g_refs, out_refs, sems):
        nb = len(self.out_shape)
        send_sems, recv_sems, local_sems = sems
        x, y, c = _coords()
        me = _lin((x, y, c))

        def copy(k, w, landing):
            peer = _flip(k, x, y, c)
            src, dst = (me, _lin(peer)) if landing else (_lin(peer), me)
            return pltpu.make_async_remote_copy(
                src_ref=self.span(w, g_refs[w], src), dst_ref=self.span(w, out_refs[w], dst),
                send_sem=send_sems.at[(k - 1) * nb + w], recv_sem=recv_sems.at[(k - 1) * nb + w],
                device_id=peer, device_id_type=MESH)

        mines = [pltpu.make_async_copy(self.span(w, g_refs[w], me), self.span(w, out_refs[w], me), local_sems.at[w])
                 for w in range(nb)]
        sends = [copy(k, w, False) for w in range(nb) for k in range(1, N_DEV)]
        landings = [copy(k, w, True) for w in range(nb) for k in range(1, N_DEV)]
        return mines, sends, landings

    def start(self, ins, outs, sems):
        mines, sends, _ = self._copies(ins, outs, sems)
        for cp in mines + sends:
            cp.start()

    def finish(self, ins, outs, sems):
        mines, sends, landings = self._copies(ins, outs, sems)
        for cp in landings:
            cp.wait_recv()
        for cp in sends:
            cp.wait_send()
        for mine in mines:
            mine.wait()


def _run_rider(name, rider):
    n_in, n_out = len(rider.ins), len(rider.out_shape)

    def body(*refs):
        ride = (refs[:n_in], refs[n_in:n_in + n_out], refs[n_in + n_out:])
        rider.start(*ride)
        rider.mid(*ride)
        rider.finish(*ride)

    rider.results = pl.pallas_call(
        body, name=name, out_shape=rider.out_shape, in_specs=[ANY] * n_in, out_specs=[ANY] * n_out,
        scratch_shapes=rider.scratch, input_output_aliases=dict(rider.aliases),
        compiler_params=pltpu.CompilerParams(has_side_effects=True),
    )(*rider.ins)
    return rider.results


def _as_matrix(name, g):
    if name == 'w_in':
        return g
    if name in COL_SHARDED:
        return jnp.transpose(g, (1, 0, 2)).reshape(g.shape[1], N_DEV * g.shape[2])
    return g.reshape(N_DEV * g.shape[1], g.shape[2])


def _by_owner(name, g):
    if name == 'w_in':
        return g
    if name in COL_SHARDED:
        return jnp.transpose(g.reshape(g.shape[0], N_DEV, g.shape[1] // N_DEV), (1, 0, 2))
    return g.reshape(N_DEV, g.shape[0] // N_DEV, g.shape[1])


class _Comm:
    def __init__(self, shards):
        self.shards, self.W, self.parts, self.partial, self.sent = shards, {}, {}, {}, {}

    def _ride(self, cls, names, srcs, part, sink):
        part = part or {}
        rider = cls(srcs, rows=[part[n][:2] if n in part else None for n in names],
                    fill=[self.partial.pop((sink, n), None) for n in names])
        rider.names, rider.sink = names, sink
        rider.unfinished = {n for n in names if n in part and not part[n][2]}
        return rider

    def gather(self, names, part=None):
        return self._ride(_GatherRider, names, [self.shards[n] for n in names], part, 'W')

    def exchange(self, names, G, part=None):
        for n in names:
            if n not in self.sent:
                self.sent[n] = _by_owner(n, G[n])
        return self._ride(_ExchangeRider, names, [self.sent[n] for n in names], part, 'parts')

    def collect(self, rider):
        for n, res in zip(rider.names, rider.results):
            if n in rider.unfinished:
                self.partial[(rider.sink, n)] = res
            elif rider.sink == 'W':
                self.W[n] = _as_matrix(n, res)
            else:
                self.parts[n] = res


def _adamw_math(p_ref, w_ref, m_ref, v_ref, g_ref, d_ref, nm_ref, nv_ref):
    c1 = 1.0 - ADAM_B1 ** ADAM_STEP
    c2 = 1.0 - ADAM_B2 ** ADAM_STEP
    g = p_ref[0].astype(F32)
    for s in range(1, p_ref.shape[0]):
        g = g + p_ref[s].astype(F32)
    nm = ADAM_B1 * m_ref[...] + (1.0 - ADAM_B1) * g
    nv = ADAM_B2 * v_ref[...] + (1.0 - ADAM_B2) * (g * g)
    g_ref[...] = g
    nm_ref[...] = nm
    nv_ref[...] = nv
    d_ref[...] = -ADAM_LR * ((nm / c1) / (jnp.sqrt(nv / c2) + ADAM_EPS) + ADAM_WD * w_ref[...])


def _adamw_vectors(parts, ws, ms, vs, loss_parts):
    k = len(ws)

    def body(*refs):
        for i in range(k):
            _adamw_math(refs[i], refs[k + i], refs[2 * k + i], refs[3 * k + i], *refs[4 * k + 1 + 4 * i:4 * k + 5 + 4 * i])
        lp, lo = refs[4 * k], refs[8 * k + 1]
        lo[...] = functools.reduce(jnp.add, [lp[s] for s in range(lp.shape[0])])

    return pl.pallas_call(
        body, name="adamw_vectors",
        out_shape=[jax.ShapeDtypeStruct(w.shape, F32) for w in ws for _ in range(4)] + [jax.ShapeDtypeStruct((1, 128), F32)],
        compiler_params=_params(),
    )(*parts, *ws, *ms, *vs, loss_parts)


def _adamw(name, parts, w, m, v, tb):
    n, R, Wd = parts.shape
    assert R % tb == 0
    body = functools.partial(_adamw_math)

    row = pl.BlockSpec((tb, Wd), lambda i: (i, 0))
    return pl.pallas_call(
        body, name=name, grid=(R // tb,),
        in_specs=[pl.BlockSpec((n, tb, Wd), lambda i: (0, i, 0)), row, row, row],
        out_specs=[row, row, row, row], out_shape=[jax.ShapeDtypeStruct((R, Wd), F32)] * 4,
        compiler_params=_params(("arbitrary",)),
    )(parts, w, m, v)


def _row_tile(R, unit, cap):
    best = unit
    for t in range(unit, cap + 1, unit):
        if R % t == 0:
            best = t
    return best


def kernel(x, ffn1_w_gate, ffn1_w_up, ffn1_w_down, ln1_g, ln1_b, w_in, b_in, ret_gn_g, conv_k, conv_b, conv_ln_g, conv_ln_b, w_ret_o, w_conv_o, w_out, ln2_g, ln2_b, ffn2_w_gate, ffn2_w_up, ffn2_w_down, ln3_g, ln3_b, loss_target, m_ffn1_w_gate, m_ffn1_w_up, m_ffn1_w_down, m_ln1_g, m_ln1_b, m_w_in, m_b_in, m_ret_gn_g, m_conv_k, m_conv_b, m_conv_ln_g, m_conv_ln_b, m_w_ret_o, m_w_conv_o, m_w_out, m_ln2_g, m_ln2_b, m_ffn2_w_gate, m_ffn2_w_up, m_ffn2_w_down, m_ln3_g, m_ln3_b, v_ffn1_w_gate, v_ffn1_w_up, v_ffn1_w_down, v_ln1_g, v_ln1_b, v_w_in, v_b_in, v_ret_gn_g, v_conv_k, v_conv_b, v_conv_ln_g, v_conv_ln_b, v_w_ret_o, v_w_conv_o, v_w_out, v_ln2_g, v_ln2_b, v_ffn2_w_gate, v_ffn2_w_up, v_ffn2_w_down, v_ln3_g, v_ln3_b):
    given = dict(locals())
    wts = {n: given[n] for n in WEIGHTS}
    mom = {n: given['m_' + n] for n in WEIGHTS}
    var = {n: given['v_' + n] for n in WEIGHTS}

    def shard2d(a):
        return a.reshape(a.shape[-3] * a.shape[-2] if a.ndim == 4 else a.shape[-2], a.shape[-1])

    comm = _Comm({n: shard2d(wts[n]).astype(BF16) for n in BIG})
    P = {n: wts[n].reshape(1, -1) for n in SMALL}
    loss, grad_x, _, S = _local_step(x[0], loss_target[0], None, P, comm=comm)

    parts = comm.parts
    res = {}
    for n in BIG:
        rows, cols = parts[n].shape[1:]
        tb = rows if rows % 16 else _row_tile(rows, 16, max(16, (256 * 1024) // cols))
        res[n] = _adamw("adamw_" + n, parts[n], shard2d(wts[n]), shard2d(mom[n]), shard2d(var[n]), tb)

    vec_parts = _run_rider("gather_vector_grads", _GatherRider([S[n] for n in SMALL] + [loss]))
    vec = _adamw_vectors(vec_parts[:-1], [P[n] for n in SMALL], [mom[n].reshape(1, -1) for n in SMALL],
                         [var[n].reshape(1, -1) for n in SMALL], vec_parts[-1])
    for i, n in enumerate(SMALL):
        res[n] = vec[4 * i:4 * i + 4]

    outs = [vec[-1][0, 0], grad_x[None]]
    for k in range(4):
        for n in WEIGHTS:
            outs.append(res[n][k].reshape(wts[n].shape))
    return tuple(outs)
```

```python
import functools
import math

import jax
import jax.numpy as jnp
from jax import lax
from jax.experimental import pallas as pl
from jax.experimental.pallas import tpu as pltpu

F32 = jnp.float32
BF16 = jnp.bfloat16

N_DEV = 8
LN_EPS = 1e-5
ALPHA = 2.0 ** 0.25
RET_DK = 128
RET_DV = 256
RET_CHUNK = 256
ROPE_BASE = 10000.0
CONV_WIDTH = 31
HALO = 32
ADAM_LR, ADAM_B1, ADAM_B2, ADAM_EPS, ADAM_WD, ADAM_STEP = 0.001, 0.9, 0.999, 1e-08, 0.01, 10
VMEM_LIMIT = 52 * 1024 * 1024
MESH = pl.DeviceIdType.MESH
ANY = pl.BlockSpec(memory_space=pl.ANY)

BIG = ['ffn1_w_gate', 'ffn1_w_up', 'ffn1_w_down', 'w_in', 'w_ret_o', 'w_conv_o', 'w_out',
       'ffn2_w_gate', 'ffn2_w_up', 'ffn2_w_down', 'conv_k']
COL_SHARDED = {'ffn1_w_gate', 'ffn1_w_up', 'w_in', 'ffn2_w_gate', 'ffn2_w_up', 'conv_k'}
SMALL = ['ln1_g', 'ln1_b', 'b_in', 'ret_gn_g', 'conv_b', 'conv_ln_g', 'conv_ln_b', 'ln2_g', 'ln2_b', 'ln3_g', 'ln3_b']
WEIGHTS = ['ffn1_w_gate', 'ffn1_w_up', 'ffn1_w_down', 'ln1_g', 'ln1_b', 'w_in', 'b_in', 'ret_gn_g', 'conv_k', 'conv_b',
           'conv_ln_g', 'conv_ln_b', 'w_ret_o', 'w_conv_o', 'w_out', 'ln2_g', 'ln2_b', 'ffn2_w_gate', 'ffn2_w_up',
           'ffn2_w_down', 'ln3_g', 'ln3_b']


def _params(sem=None):
    return pltpu.CompilerParams(dimension_semantics=sem, vmem_limit_bytes=VMEM_LIMIT)


def _sigmoid(x):
    return jax.nn.sigmoid(x)


def _dsilu(x, sg):
    return sg * (1.0 + x * (1.0 - sg))


def _fit(dim, want):
    if dim <= want:
        return dim
    return max(t for t in range(128, want + 1, 128) if dim % t == 0)


def _dot(a, b, ta=False, tb=False):
    dn = (((0,) if ta else (1,), (1,) if tb else (0,)), ((), ()))
    return lax.dot_general(a, b, dn, preferred_element_type=F32)


def _mm(name, As, Bs, prods, epi, out_dtypes, *, ta=False, tb=False, tm, tn, tk, extras=(), i_outer=True,
        b3=False, o3=False, rider=None, bsum=False, epi_rows=0):
    a0, b0 = As[0], Bs[0]
    M, K = (a0.shape[1], a0.shape[0]) if ta else a0.shape
    if b3:
        S, rows, cs = b0.shape
        N = rows if tb else S * cs
        assert K == (S * cs if tb else rows)
        tn, tk = (tn, cs) if tb else (cs, tk)
    else:
        N = b0.shape[0] if tb else b0.shape[1]
    tm, tn, tk = _fit(M, tm), _fit(N, tn), _fit(K, tk)
    assert M % tm == 0 and N % tn == 0 and K % tk == 0, (name, M, N, K, tm, tn, tk)
    gi, gj, gk = M // tm, N // tn, K // tk
    grid = (gi, gj, gk) if i_outer else (gj, gi, gk)

    def ij(g0, g1):
        return (g0, g1) if i_outer else (g1, g0)

    def amap(g0, g1, k):
        i, _ = ij(g0, g1)
        return (k, i) if ta else (i, k)

    def bmap(g0, g1, k):
        _, j = ij(g0, g1)
        return (j, k) if tb else (k, j)

    def bmap3(g0, g1, k):
        _, j = ij(g0, g1)
        return (k, j, 0) if tb else (j, k, 0)

    in_specs = [pl.BlockSpec((tk, tm) if ta else (tm, tk), amap) for _ in As]
    if b3:
        in_specs += [pl.BlockSpec((None, tn, tk) if tb else (None, tk, tn), bmap3) for _ in Bs]
    else:
        in_specs += [pl.BlockSpec((tn, tk) if tb else (tk, tn), bmap) for _ in Bs]
    args = list(As) + list(Bs)
    for arr, kind, coloff in extras:
        assert coloff % tn == 0
        off = coloff // tn
        if kind == 'mn':
            in_specs.append(pl.BlockSpec((tm, tn), lambda g0, g1, k, off=off: (ij(g0, g1)[0], ij(g0, g1)[1] + off)))
        else:
            in_specs.append(pl.BlockSpec((1, tn), lambda g0, g1, k, off=off: (0, ij(g0, g1)[1] + off)))
        args.append(arr)
    if o3:
        out_shape = [jax.ShapeDtypeStruct((gj, M, tn), dt) for dt in out_dtypes]
        out_specs = [pl.BlockSpec((None, tm, tn), lambda g0, g1, k: (ij(g0, g1)[1], ij(g0, g1)[0], 0))
                     for _ in out_dtypes]
    else:
        out_shape = [jax.ShapeDtypeStruct((M, N), dt) for dt in out_dtypes]
        out_specs = [pl.BlockSpec((tm, tn), lambda g0, g1, k: ij(g0, g1)) for _ in out_dtypes]
    if bsum:
        assert gi == 1 and not tb and not b3
        out_shape.append(jax.ShapeDtypeStruct((1, N), F32))
        out_specs.append(pl.BlockSpec((1, tn), lambda g0, g1, k: (0, ij(g0, g1)[1])))
    n_a, n_b, n_e, n_o = len(As), len(Bs), len(extras), len(out_shape)
    n_p = len(prods) if gk > 1 else 0
    scratch = [pltpu.VMEM((tm, tn), F32) for _ in range(n_p)]
    if rider is not None:
        in_specs, out_specs = in_specs + [ANY] * len(rider.ins), out_specs + [ANY] * len(rider.out_shape)
        args, out_shape, scratch = args + rider.ins, out_shape + rider.out_shape, scratch + rider.scratch
    n_in, n_out = len(args), len(out_shape)

    def body(*refs):
        a_refs = refs[:n_a]
        b_refs = refs[n_a:n_a + n_b]
        e_refs = refs[n_a + n_b:n_a + n_b + n_e]
        o_refs = refs[n_in:n_in + n_o]
        acc_refs = refs[n_in + n_out:n_in + n_out + n_p]
        k = pl.program_id(2)
        if rider is not None:
            step = (pl.program_id(0) * grid[1] + pl.program_id(1)) * gk + k
            ride = (step, grid[0] * grid[1] * gk, refs[n_a + n_b + n_e:n_in], refs[n_in + n_o:n_in + n_out],
                    refs[n_in + n_out + n_p:])
            rider.begin(*ride)

        def finish(accs, rows=slice(None)):
            ex = [(e[rows, :] if kind == 'mn' else e[...]).astype(F32) for e, (_, kind, _) in zip(e_refs, extras)]
            for o, r in zip(o_refs, epi(accs, *ex)):
                o[rows, :] = r.astype(o.dtype)

        if bsum:
            @pl.when(k == 0)
            def _():
                o_refs[-1][...] = jnp.zeros_like(o_refs[-1])

            o_refs[-1][...] += _colsum(b_refs[0][...].astype(F32))

        if gk == 1:
            sub = tm if ta or not epi_rows else _fit(tm, epi_rows)
            for r0 in range(0, tm, sub):
                rows = slice(None) if ta else slice(r0, r0 + sub)
                finish([functools.reduce(jnp.add, [_dot(a_refs[ai][...] if ta else a_refs[ai][rows, :],
                                                        b_refs[bi][...], ta, tb) for ai, bi in terms])
                        for terms in prods], rows)
        else:
            @pl.when(k == 0)
            def _():
                for acc in acc_refs:
                    acc[...] = jnp.zeros_like(acc)

            for p, terms in enumerate(prods):
                for ai, bi in terms:
                    acc_refs[p][...] += _dot(a_refs[ai][...], b_refs[bi][...], ta, tb)

            @pl.when(k == gk - 1)
            def _():
                finish([acc[...] for acc in acc_refs])

        if rider is not None:
            rider.end(*ride)

    aliases = {} if rider is None else {n_a + n_b + n_e + p: n_o + o for p, o in rider.aliases.items()}
    res = pl.pallas_call(
        body, name=name, grid=grid, in_specs=in_specs, out_specs=out_specs, out_shape=out_shape,
        scratch_shapes=scratch, input_output_aliases=aliases,
        compiler_params=_params(("arbitrary", "arbitrary", "arbitrary")),
    )(*args)
    if rider is not None:
        rider.results = res[n_o:]
    return res[:n_o]


def _rows(name, fn, ins, outs, *, T, tb):
    tb = min(tb, T)
    assert T % tb == 0
    in_specs, args = [], []
    for arr, kind, width, cb in ins:
        if kind == 'r':
            in_specs.append(pl.BlockSpec((tb, width), lambda i, cb=cb: (i, cb)))
        else:
            in_specs.append(pl.BlockSpec((1, width), lambda i, cb=cb: (0, cb)))
        args.append(arr)
    out_shape, out_specs = [], []
    for kind, width, dtype in outs:
        if kind == 'r':
            out_shape.append(jax.ShapeDtypeStruct((T, width), dtype))
            out_specs.append(pl.BlockSpec((tb, width), lambda i: (i, 0)))
        elif kind == 'c':
            out_shape.append(jax.ShapeDtypeStruct((T, 1), dtype))
            out_specs.append(pl.BlockSpec((tb, 1), lambda i: (i, 0)))
        else:
            out_shape.append(jax.ShapeDtypeStruct((1, width), F32))
            out_specs.append(pl.BlockSpec((1, width), lambda i: (0, 0)))
    n_in = len(ins)

    def body(*refs):
        i = pl.program_id(0)
        vals = fn(*[r[...] for r in refs[:n_in]])
        for (kind, _, _), o, v in zip(outs, refs[n_in:], vals):
            if kind == 'a':
                @pl.when(i == 0)
                def _(o=o):
                    o[...] = jnp.zeros_like(o)

                o[...] += v
            else:
                o[...] = v.astype(o.dtype)

    return pl.pallas_call(
        body, name=name, grid=(T // tb,), in_specs=in_specs, out_specs=out_specs, out_shape=out_shape,
        compiler_params=_params(("arbitrary",)),
    )(*args)


def _colsum(v):
    return jnp.sum(v, axis=0, keepdims=True)


def _ln_stats(z):
    mu = jnp.mean(z, axis=-1, keepdims=True)
    d = z - mu
    var = jnp.mean(d * d, axis=-1, keepdims=True)
    rstd = lax.rsqrt(var + LN_EPS)
    return d * rstd, rstd


def _ln_bwd_math(dy, xhat, rstd, g):
    dxh = dy * g
    m1 = jnp.mean(dxh, axis=-1, keepdims=True)
    m2 = jnp.mean(dxh * xhat, axis=-1, keepdims=True)
    return rstd * (dxh - m1 - xhat * m2)


def _ln_fwd(name, z, g, b, T, D):
    def fn(z, g, b):
        xhat, rstd = _ln_stats(z)
        y = xhat * g + b
        return [y, y, xhat, rstd]

    return _rows(name, fn, [(z, 'r', D, 0), (g, 'v', D, 0), (b, 'v', D, 0)],
                 [('r', D, F32), ('r', D, BF16), ('r', D, F32), ('c', 1, F32)], T=T, tb=512)


def _ln_bwd(name, dy, xhat, rstd, g, scale, T, D):
    def fn(dy, xhat, rstd, g):
        dz = _ln_bwd_math(dy, xhat, rstd, g)
        return [dz, dz * scale, _colsum(dy * xhat), _colsum(dy)]

    return _rows(name, fn, [(dy, 'r', D, 0), (xhat, 'r', D, 0), (rstd, 'r', 1, 0), (g, 'v', D, 0)],
                 [('r', D, F32), ('r', D, BF16), ('a', D, F32), ('a', D, F32)], T=T, tb=512)


def _ln_loss_bwd(name, z, g, b, tgt, T, D):
    def fn(z, g, b, tgt):
        xhat, rstd = _ln_stats(z)
        err = xhat * g + b - tgt
        row_loss = 0.5 * jnp.mean(err * err, axis=-1, keepdims=True)
        loss = jnp.broadcast_to(jnp.sum(row_loss, axis=0, keepdims=True), (1, 128))
        dy = err * (1.0 / D)
        dz = _ln_bwd_math(dy, xhat, rstd, g)
        return [dz, dz * 0.5, _colsum(dy * xhat), _colsum(dy), loss]

    return _rows(name, fn, [(z, 'r', D, 0), (g, 'v', D, 0), (b, 'v', D, 0), (tgt, 'r', D, 0)],
                 [('r', D, F32), ('r', D, BF16), ('a', D, F32), ('a', D, F32), ('a', 128, F32)], T=T, tb=512)


class _Net:
    def __init__(self, comm, G):
        self.comm, self.G = comm, G

    def gather(self, names, part=None):
        return self.comm.gather(names, part) if self.comm else None

    def exchange(self, names, part=None):
        return self.comm.exchange(names, self.G, part) if self.comm else None

    def done(self, rider):
        if rider is not None:
            self.comm.collect(rider)


def _ffn_fwd(tag, xb, x, W, names, net, rider=None, rider_down=None):
    def epi_gu(accs):
        a, b = accs
        return [a, b, a * _sigmoid(a) * b]

    ng, nu, nd = names
    a, b, s = _mm(tag + "_gate_up", [xb], [W[ng], W[nu]], [[(0, 0)], [(0, 1)]], epi_gu, [BF16, BF16, BF16],
                  tm=1024, tn=1408, tk=1024, rider=rider, epi_rows=256)
    net.done(rider)

    def epi_down(accs, xres):
        return [ALPHA * xres + 0.5 * accs[0]]

    rider_down = rider_down() if rider_down else None
    (z,) = _mm(tag + "_down", [s], [W[nd]], [[(0, 0)]], epi_down, [F32], tm=1024, tn=1024, tk=1408,
               extras=[(x, 'mn', 0)], rider=rider_down)
    net.done(rider_down)
    return a, b, s, z


def _ffn_bwd(tag, dzh, dz, xb, a, b, s, W, names, gdt, G, net, ride, pre=(None, None)):
    ng, nu, nd = names

    def epi_ds(accs, a, b):
        ds = accs[0]
        sg = _sigmoid(a)
        return [ds * b * _dsilu(a, sg), ds * a * sg]

    rider = pre[0]() if pre[0] else None
    da, db = _mm(tag + "_ds", [dzh], [W[nd]], [[(0, 0)]], epi_ds, [BF16, BF16], tb=True, tm=1024, tn=1408, tk=1024, epi_rows=256,
                 extras=[(a, 'mn', 0), (b, 'mn', 0)], rider=rider)
    net.done(rider)
    ident = lambda accs: accs
    rider = pre[1]() if pre[1] else None
    (G[nd],) = _mm(tag + "_dwd", [s], [dzh], [[(0, 0)]], ident, [gdt], ta=True, tm=1408, tn=1024, tk=1024,
                   rider=rider)
    net.done(rider)
    if ride == 'all':
        rider = net.exchange([nd])
        (G[ng],) = _mm(tag + "_dwg", [xb], [da], [[(0, 0)]], ident, [gdt], ta=True, tm=1024, tn=1408, tk=1024,
                       rider=rider)
        net.done(rider)
        rider = net.exchange([ng])
        (G[nu],) = _mm(tag + "_dwu", [xb], [db], [[(0, 0)]], ident, [gdt], ta=True, tm=1024, tn=1408, tk=1024,
                       rider=rider)
        net.done(rider)
    else:
        G[ng], G[nu] = _mm(tag + "_dwgu", [xb], [da, db], [[(0, 0)], [(0, 1)]], ident, [gdt, gdt], ta=True,
                           tm=1024, tn=1408, tk=1024)

    def epi_dx(accs, dzres):
        return [ALPHA * dzres + accs[0]]

    rider = net.exchange([nu] if ride == 'all' else [nd]) if ride else None
    (dx,) = _mm(tag + "_dx", [da, db], [W[ng], W[nu]], [[(0, 0), (1, 1)]], epi_dx, [F32], tb=True,
                tm=1024, tn=1024, tk=1408, extras=[(dz, 'mn', 0)], rider=rider)
    net.done(rider)
    return dx


def _ret_tables(H, T):
    C = RET_CHUNK
    log_g = jnp.log(1.0 - jnp.exp2(-5.0 - jnp.arange(H, dtype=F32)))
    idx = jnp.arange(C, dtype=F32)
    diff = idx[:, None] - idx[None, :]
    dm = jnp.where(diff[None] >= 0, jnp.exp(jnp.maximum(diff, 0.0)[None] * log_g[:, None, None]), 0.0)
    xi = jnp.exp((idx[None, :] + 1.0) * log_g[:, None])[:, :, None]
    zeta = jnp.exp((C - 1.0 - idx)[None, :] * log_g[:, None])[:, :, None]
    gc = jnp.broadcast_to(jnp.exp(C * log_g)[:, None, None], (H, 1, RET_DV))
    half = RET_DK // 2
    freqs = ROPE_BASE ** (-jnp.arange(half, dtype=F32) / half)
    ang = jnp.arange(T, dtype=F32)[:, None] * freqs[None, :]
    cos, sin = jnp.cos(ang), jnp.sin(ang)
    cosf = jnp.concatenate([cos, cos], axis=1)
    sins = jnp.concatenate([-sin, sin], axis=1)
    return dm, xi, zeta, gc, cosf, sins


def _rot(x, cosf, sins):
    return x * cosf + pltpu.roll(x, RET_DK // 2, 1) * sins


def _rot_bwd(dy, cosf, sins):
    return dy * cosf + pltpu.roll(dy * sins, RET_DK // 2, 1)


RET_HB = 8


def _ret_specs(H, HB, rev, NC):
    C, G = RET_CHUNK, H // HB
    nn = (lambda n: NC - 1 - n) if rev else (lambda n: n)
    return [
        pl.BlockSpec((C, HB * RET_DK), lambda h, n: (nn(n), h)),
        pl.BlockSpec((C, HB * RET_DK), lambda h, n: (nn(n), G + h)),
        pl.BlockSpec((C, HB * RET_DV), lambda h, n: (nn(n), G + h)),
        pl.BlockSpec((C, HB * RET_DV), lambda h, n: (nn(n), 2 * G + h)),
        pl.BlockSpec((C, RET_DK), lambda h, n: (nn(n), 0)),
        pl.BlockSpec((C, RET_DK), lambda h, n: (nn(n), 0)),
        pl.BlockSpec((1, HB * RET_DV), lambda h, n: (0, h)),
        pl.BlockSpec((HB, C, C), lambda h, n: (h, 0, 0)),
        pl.BlockSpec((HB, C, 1), lambda h, n: (h, 0, 0)),
        pl.BlockSpec((HB, C, 1), lambda h, n: (h, 0, 0)),
        pl.BlockSpec((HB, 1, RET_DV), lambda h, n: (h, 0, 0)),
    ]


def _ret_fwd(proj, gn_g, tabs, H, T, rider=None):
    C, NC = RET_CHUNK, T // RET_CHUNK
    HB = min(RET_HB, H)
    dm, xi, zeta, gc, cosf, sins = tabs
    scale = RET_DK ** -0.5

    def body(q_ref, k_ref, v_ref, g_ref, cos_ref, sin_ref, gn_ref, dm_ref, xi_ref, zt_ref, gc_ref,
             r_ref, ri_ref, st_ref, state):
        @pl.when(pl.program_id(1) == 0)
        def _():
            state[...] = jnp.zeros_like(state)

        cs, sn = cos_ref[...], sin_ref[...]
        for hh in range(HB):
            qk = slice(hh * RET_DK, (hh + 1) * RET_DK)
            vv = slice(hh * RET_DV, (hh + 1) * RET_DV)
            qr = _rot(q_ref[:, qk].astype(F32), cs, sn) * scale
            kr = _rot(k_ref[:, qk].astype(F32), cs, sn)
            qb, kb, vb = qr.astype(BF16), kr.astype(BF16), v_ref[:, vv].astype(BF16)
            st = state[hh]
            stb = st.astype(BF16)
            s = _dot(qb, kb, tb=True) * dm_ref[hh]
            r = _dot(s.astype(BF16), vb) + _dot(qb, stb) * xi_ref[hh]
            st_ref[hh] = stb
            state[hh] = gc_ref[hh] * st + _dot((kr * zt_ref[hh]).astype(BF16), vb, ta=True)
            rhat, _ = _ln_stats(r)
            g = g_ref[:, vv].astype(F32)
            r_ref[:, vv] = r
            ri_ref[:, vv] = (g * _sigmoid(g) * (rhat * gn_ref[:, vv])).astype(BF16)

    VW = H * RET_DV
    return _hosted_call(
        body, rider, name="ret_fwd", grid=(H // HB, NC), in_specs=_ret_specs(H, HB, False, NC),
        out_specs=[pl.BlockSpec((C, HB * RET_DV), lambda h, n: (n, h)),
                   pl.BlockSpec((C, HB * RET_DV), lambda h, n: (n, h)),
                   pl.BlockSpec((HB, None, RET_DK, RET_DV), lambda h, n: (h, n, 0, 0))],
        out_shape=[jax.ShapeDtypeStruct((T, VW), F32), jax.ShapeDtypeStruct((T, VW), BF16),
                   jax.ShapeDtypeStruct((H, NC, RET_DK, RET_DV), BF16)],
        scratch=[pltpu.VMEM((HB, RET_DK, RET_DV), F32)],
        args=[proj, proj, proj, proj, cosf, sins, gn_g, dm, xi, zeta, gc])


def _hosted_call(body, rider, *, name, grid, in_specs, out_specs, out_shape, scratch, args):
    n_in, n_out, n_scr = len(args), len(out_shape), len(scratch)
    if rider is None:
        hosted = body
    else:
        n_ri, n_ro = len(rider.ins), len(rider.out_shape)
        in_specs, out_specs = in_specs + [ANY] * n_ri, out_specs + [ANY] * n_ro
        args, out_shape, scratch = args + rider.ins, out_shape + rider.out_shape, scratch + rider.scratch

        def hosted(*refs):
            o0, s0 = n_in + n_ri, n_in + n_ri + n_out + n_ro
            step = pl.program_id(0) * grid[1] + pl.program_id(1)
            ride = (step, grid[0] * grid[1], refs[n_in:o0], refs[o0 + n_out:s0], refs[s0 + n_scr:])
            rider.begin(*ride)
            body(*refs[:n_in], *refs[o0:o0 + n_out], *refs[s0:s0 + n_scr])
            rider.end(*ride)

    aliases = {} if rider is None else {n_in + p: n_out + o for p, o in rider.aliases.items()}
    res = pl.pallas_call(
        hosted, name=name, grid=grid, in_specs=in_specs, out_specs=out_specs, out_shape=out_shape,
        scratch_shapes=scratch, input_output_aliases=aliases, compiler_params=_params(("arbitrary", "arbitrary")),
    )(*args)
    if rider is not None:
        rider.results = res[n_out:]
    return res[:n_out]


def _ret_bwd(dri, r, states, proj, gn_g, tabs, H, T, in_w, rider=None):
    C, NC = RET_CHUNK, T // RET_CHUNK
    HB = min(RET_HB, H)
    dm, xi, zeta, gc, cosf, sins = tabs
    scale = RET_DK ** -0.5

    def body(q_ref, k_ref, v_ref, g_ref, cos_ref, sin_ref, gn_ref, dm_ref, xi_ref, zt_ref, gc_ref,
             dri_ref, r_ref, st_ref, dp_ref, dgn_ref, dstate):
        @pl.when(pl.program_id(1) == 0)
        def _():
            dstate[...] = jnp.zeros_like(dstate)
            dgn_ref[...] = jnp.zeros_like(dgn_ref)

        cs, sn = cos_ref[...], sin_ref[...]
        for hh in range(HB):
            qk = slice(hh * RET_DK, (hh + 1) * RET_DK)
            vv = slice(hh * RET_DV, (hh + 1) * RET_DV)
            qr = _rot(q_ref[:, qk].astype(F32), cs, sn) * scale
            kr = _rot(k_ref[:, qk].astype(F32), cs, sn)
            qb, kb, vb = qr.astype(BF16), kr.astype(BF16), v_ref[:, vv].astype(BF16)
            xi_c, zt_c, dmask = xi_ref[hh], zt_ref[hh], dm_ref[hh]
            rhat, rstd = _ln_stats(r_ref[:, vv])
            g, gn, dpre = g_ref[:, vv].astype(F32), gn_ref[:, vv], dri_ref[:, vv]
            sg = _sigmoid(g)
            dp_ref[:, 2 * QW + VW + hh * RET_DV:2 * QW + VW + (hh + 1) * RET_DV] = (
                dpre * (rhat * gn) * _dsilu(g, sg)).astype(BF16)
            drn = dpre * (g * sg)
            dgn_ref[:, vv] += _colsum(drn * rhat)
            drb = _ln_bwd_math(drn, rhat, rstd, gn).astype(BF16)
            ds1 = dstate[hh]
            ds1b = ds1.astype(BF16)
            sb = (_dot(qb, kb, tb=True) * dmask).astype(BF16)
            kzb = (kr * zt_c).astype(BF16)
            dp_ref[:, 2 * QW + hh * RET_DV:2 * QW + (hh + 1) * RET_DV] = (
                _dot(sb, drb, ta=True) + _dot(kzb, ds1b)).astype(BF16)
            dsb = (_dot(drb, vb, tb=True) * dmask).astype(BF16)
            dq = _dot(dsb, kb) + _dot(drb, st_ref[hh], tb=True) * xi_c
            dk = _dot(dsb, qb, ta=True) + _dot(vb, ds1b, tb=True) * zt_c
            dstate[hh] = gc_ref[hh] * ds1 + _dot((qr * xi_c).astype(BF16), drb, ta=True)
            dp_ref[:, qk] = _rot_bwd(dq * scale, cs, sn).astype(BF16)
            dp_ref[:, QW + hh * RET_DK:QW + (hh + 1) * RET_DK] = _rot_bwd(dk, cs, sn).astype(BF16)

    VW, QW = H * RET_DV, H * RET_DK
    rv = lambda n: NC - 1 - n
    in_specs = _ret_specs(H, HB, True, NC) + [
        pl.BlockSpec((C, HB * RET_DV), lambda h, n: (rv(n), h)),
        pl.BlockSpec((C, HB * RET_DV), lambda h, n: (rv(n), h)),
        pl.BlockSpec((HB, None, RET_DK, RET_DV), lambda h, n: (h, rv(n), 0, 0)),
    ]
    assert HB == H
    return _hosted_call(
        body, rider, name="ret_bwd", grid=(1, NC), in_specs=in_specs,
        out_specs=[pl.BlockSpec((C, 2 * QW + 2 * VW), lambda h, n: (rv(n), 0)),
                   pl.BlockSpec((1, VW), lambda h, n: (0, 0))],
        out_shape=[jax.ShapeDtypeStruct((T, in_w), BF16), jax.ShapeDtypeStruct((1, VW), F32)],
        scratch=[pltpu.VMEM((HB, RET_DK, RET_DV), F32)],
        args=[proj, proj, proj, proj, cosf, sins, gn_g, dm, xi, zeta, gc, dri, r, states])


CONV_CW = 128
CONV_TB = 512


SUBLANES = 8


def _shift_copies(win, shifted, tb):
    n = tb + HALO - SUBLANES
    for s in range(1, SUBLANES):
        shifted[s - 1] = win[pl.ds(s, n), :]


def _tap(win, shifted, off, tb):
    s = off % SUBLANES
    if s == 0:
        return win[pl.ds(off, tb), :]
    return shifted[s - 1, pl.ds(off - s, tb), :]


def _conv_fwd(proj, kpad, bias, off_a, CC, T, rider=None):
    tb, cw = min(CONV_TB, T), CONV_CW
    hb = tb // HALO
    ca, cb = off_a // cw, (off_a + CC) // cw

    def body(a_ref, b_ref, ap_ref, bp_ref, k_ref, bias_ref, u1_ref, win, shifted):
        i = pl.program_id(0)
        keep = (i > 0).astype(F32)
        win[0:HALO, :] = ap_ref[...].astype(F32) * _sigmoid(bp_ref[...].astype(F32)) * keep
        win[HALO:, :] = a_ref[...].astype(F32) * _sigmoid(b_ref[...].astype(F32))
        _shift_copies(win, shifted, tb)
        acc = jnp.broadcast_to(bias_ref[...], (tb, cw))
        for w in range(CONV_WIDTH):
            acc = acc + k_ref[w:w + 1, :] * _tap(win, shifted, HALO - (CONV_WIDTH - 1) + w, tb)
        u1_ref[...] = acc

    prev = lambda i: jnp.maximum(i * hb - 1, 0)
    (u1,) = _hosted_call(
        body, rider, name="conv_fwd", grid=(T // tb, CC // cw),
        in_specs=[pl.BlockSpec((tb, cw), lambda i, c: (i, ca + c)),
                  pl.BlockSpec((tb, cw), lambda i, c: (i, cb + c)),
                  pl.BlockSpec((HALO, cw), lambda i, c: (prev(i), ca + c)),
                  pl.BlockSpec((HALO, cw), lambda i, c: (prev(i), cb + c)),
                  pl.BlockSpec((HALO, cw), lambda i, c: (0, c)),
                  pl.BlockSpec((1, cw), lambda i, c: (0, c))],
        out_specs=[pl.BlockSpec((tb, cw), lambda i, c: (i, c))],
        out_shape=[jax.ShapeDtypeStruct((T, CC), F32)],
        scratch=[pltpu.VMEM((tb + HALO, cw), F32), pltpu.VMEM((SUBLANES - 1, tb + HALO - SUBLANES, cw), F32)],
        args=[proj, proj, proj, proj, kpad, bias])
    return u1


def _conv_bwd(du1, proj, kpad, off_a, CC, T, rider=None):
    tb, cw = min(CONV_TB, T), CONV_CW
    hb = tb // HALO
    nt = T // tb
    ca, cb = off_a // cw, (off_a + CC) // cw

    def body(d_ref, dn_ref, a_ref, b_ref, ap_ref, bp_ref, k_ref, da_ref, db_ref, dk_ref, winu, wind, shu, shd):
        i = pl.program_id(1)
        a, b = a_ref[...].astype(F32), b_ref[...].astype(F32)
        sgb = _sigmoid(b)
        winu[0:HALO, :] = ap_ref[...].astype(F32) * _sigmoid(bp_ref[...].astype(F32)) * (i > 0).astype(F32)
        winu[HALO:, :] = a * sgb
        d = d_ref[...]
        wind[0:tb, :] = d
        wind[tb:, :] = dn_ref[...] * (i < nt - 1).astype(F32)

        @pl.when(i == 0)
        def _():
            dk_ref[...] = jnp.zeros_like(dk_ref)

        _shift_copies(winu, shu, tb)
        _shift_copies(wind, shd, tb)
        du0 = jnp.zeros((tb, cw), F32)
        for w in range(CONV_WIDTH):
            du0 = du0 + k_ref[w:w + 1, :] * _tap(wind, shd, CONV_WIDTH - 1 - w, tb)
            dk_ref[w:w + 1, :] += _colsum(_tap(winu, shu, HALO - (CONV_WIDTH - 1) + w, tb) * d)
        da_ref[...] = (du0 * sgb).astype(BF16)
        db_ref[...] = (du0 * a * sgb * (1.0 - sgb)).astype(BF16)

    prev = lambda i: jnp.maximum(i * hb - 1, 0)
    nxt = lambda i: jnp.minimum((i + 1) * hb, T // HALO - 1)
    return _hosted_call(
        body, rider, name="conv_bwd", grid=(CC // cw, nt),
        in_specs=[pl.BlockSpec((tb, cw), lambda c, i: (i, c)),
                  pl.BlockSpec((HALO, cw), lambda c, i: (nxt(i), c)),
                  pl.BlockSpec((tb, cw), lambda c, i: (i, ca + c)),
                  pl.BlockSpec((tb, cw), lambda c, i: (i, cb + c)),
                  pl.BlockSpec((HALO, cw), lambda c, i: (prev(i), ca + c)),
                  pl.BlockSpec((HALO, cw), lambda c, i: (prev(i), cb + c)),
                  pl.BlockSpec((HALO, cw), lambda c, i: (0, c))],
        out_specs=[pl.BlockSpec((tb, cw), lambda c, i: (i, c)),
                   pl.BlockSpec((tb, cw), lambda c, i: (i, c)),
                   pl.BlockSpec((HALO, cw), lambda c, i: (0, c))],
        out_shape=[jax.ShapeDtypeStruct((T, CC), BF16), jax.ShapeDtypeStruct((T, CC), BF16),
                   jax.ShapeDtypeStruct((HALO, CC), F32)],
        scratch=[pltpu.VMEM((tb + HALO, cw), F32), pltpu.VMEM((tb + HALO, cw), F32),
                 pltpu.VMEM((SUBLANES - 1, tb + HALO - SUBLANES, cw), F32),
                 pltpu.VMEM((SUBLANES - 1, tb + HALO - SUBLANES, cw), F32)],
        args=[du1, du1, proj, proj, proj, proj, kpad])


FFN1 = ('ffn1_w_gate', 'ffn1_w_up', 'ffn1_w_down')
FFN2 = ('ffn2_w_gate', 'ffn2_w_up', 'ffn2_w_down')


def _local_step(x, tgt, W, P, gdt=BF16, comm=None):
    T, D = x.shape
    G = {}
    net = _Net(comm, G)
    if comm is not None:
        W = comm.W
        first = net.gather(['ffn1_w_gate', 'ffn1_w_up'])
        _run_rider("gather_ffn1_in", first)
        net.done(first)
    VW = P['ret_gn_g'].shape[1]
    H = VW // RET_DV
    QW = H * RET_DK
    CC = P['conv_b'].shape[1]
    off_glu = 2 * QW + 2 * VW
    off_gate = off_glu + 2 * CC
    ident = lambda accs: accs
    xb = x.astype(BF16)

    a1, b1, s1, z1 = _ffn_fwd("ffn1", xb, x, W, FFN1, net,
                              rider=net.gather(['ffn1_w_down', 'w_in'], {'w_in': (0, D // 2, False)}),
                              rider_down=lambda: net.gather(['w_in'], {'w_in': (D // 2, D // 2, True)}))
    x1, x1b, xh1, rs1 = _ln_fwd("ln1", z1, P['ln1_g'], P['ln1_b'], T, D)

    rest = net.gather(['conv_k', 'w_ret_o', 'w_conv_o', 'w_out'])
    (proj,) = _mm("w_in", [x1b], [W['w_in']], [[(0, 0)]], lambda accs, bias: [accs[0] + bias], [F32],
                  tm=2048, tn=0, tk=1024, extras=[(P['b_in'], 'n', 0)], i_outer=False, b3=True, rider=rest)
    net.done(rest)
    tabs = _ret_tables(H, T)
    rider = net.gather(['ffn2_w_gate', 'ffn2_w_up'])
    r, ret_in, states = _ret_fwd(proj, P['ret_gn_g'], tabs, H, T, rider=rider)
    net.done(rider)
    kpad = jnp.pad(W['conv_k'].astype(F32), ((0, HALO - CONV_WIDTH), (0, 0)))
    rider = net.gather(['ffn2_w_down'])
    u1 = _conv_fwd(proj, kpad, P['conv_b'], off_glu, CC, T, rider=rider)
    net.done(rider)

    def conv_ln(u1, g, b):
        xhat, rstd = _ln_stats(u1)
        u2 = xhat * g + b
        return [xhat, rstd, u2 * _sigmoid(u2)]

    xhc, rsc, u3 = _rows("conv_ln", conv_ln, [(u1, 'r', CC, 0), (P['conv_ln_g'], 'v', CC, 0), (P['conv_ln_b'], 'v', CC, 0)],
                         [('r', CC, F32), ('c', 1, F32), ('r', CC, BF16)], T=T, tb=512)
    (ret_out,) = _mm("ret_o", [ret_in], [W['w_ret_o']], [[(0, 0)]], ident, [F32], tm=1024, tn=1024, tk=2048)

    def epi_merge(accs, ret_out, gr, gc):
        conv_out = accs[0]
        return [conv_out, _sigmoid(gr) * ret_out + _sigmoid(gc) * conv_out]

    conv_out, merged = _mm("conv_o_merge", [u3], [W['w_conv_o']], [[(0, 0)]], epi_merge, [F32, BF16],
                           tm=512, tn=D, tk=1024, epi_rows=256,
                           extras=[(ret_out, 'mn', 0), (proj, 'mn', off_gate), (proj, 'mn', off_gate + D)])
    (z2,) = _mm("w_out", [merged], [W['w_out']], [[(0, 0)]], lambda accs, xr: [ALPHA * xr + accs[0]], [F32],
                tm=1024, tn=1024, tk=1024, extras=[(x1, 'mn', 0)])
    x2, x2b, xh2, rs2 = _ln_fwd("ln2", z2, P['ln2_g'], P['ln2_b'], T, D)
    a2, b2, s2, z3 = _ffn_fwd("ffn2", x2b, x2, W, FFN2, net)
    dz3, dz3h, g_ln3_g, g_ln3_b, loss = _ln_loss_bwd("ln3_loss", z3, P['ln3_g'], P['ln3_b'], tgt, T, D)

    S = {'ln3_g': g_ln3_g, 'ln3_b': g_ln3_b}
    dy2 = _ffn_bwd("ffn2b", dz3h, dz3, x2b, a2, b2, s2, W, FFN2, gdt, G, net, 'down')
    dz2, dz2b, S['ln2_g'], S['ln2_b'] = _ln_bwd("ln2b", dy2, xh2, rs2, P['ln2_g'], 1.0, T, D)

    (G['w_out'],) = _mm("d_w_out", [merged], [dz2b], [[(0, 0)]], ident, [gdt], ta=True, tm=1024, tn=1024, tk=1024)

    def epi_dmerge(accs, ret_out, conv_out, gr, gc):
        dm_ = accs[0]
        sr, sc = _sigmoid(gr), _sigmoid(gc)
        return [dm_ * sr, dm_ * sc, dm_ * ret_out * sr * (1.0 - sr), dm_ * conv_out * sc * (1.0 - sc)]

    dret_out, dconv_out, dgate_r, dgate_c = _mm(
        "d_merge", [dz2b], [W['w_out']], [[(0, 0)]], epi_dmerge, [BF16, BF16, BF16, BF16], tb=True,
        tm=512, tn=D, tk=1024, epi_rows=256,
        extras=[(ret_out, 'mn', 0), (conv_out, 'mn', 0), (proj, 'mn', off_gate), (proj, 'mn', off_gate + D)])
    (G['w_ret_o'],) = _mm("d_w_ret_o", [ret_in], [dret_out], [[(0, 0)]], ident, [gdt], ta=True, tm=1024, tn=1024, tk=1024)
    (G['w_conv_o'],) = _mm("d_w_conv_o", [u3], [dconv_out], [[(0, 0)]], ident, [gdt], ta=True, tm=1024, tn=1024, tk=1024)
    (dri,) = _mm("d_ret_in", [dret_out], [W['w_ret_o']], [[(0, 0)]], ident, [F32], tb=True, tm=1024, tn=1024, tk=1024)
    rider = net.exchange(['ffn2_w_gate', 'ffn2_w_up'])
    dproj, S['ret_gn_g'] = _ret_bwd(dri, r, states, proj, P['ret_gn_g'], tabs, H, T, proj.shape[1], rider=rider)
    net.done(rider)

    def epi_du2(accs, xhat, g, b):
        u2 = xhat * g + b
        return [accs[0] * _dsilu(u2, _sigmoid(u2))]

    (du2,) = _mm("d_u3", [dconv_out], [W['w_conv_o']], [[(0, 0)]], epi_du2, [F32], tb=True, tm=512, tn=CC, tk=1024, epi_rows=256,
                 extras=[(xhc, 'mn', 0), (P['conv_ln_g'], 'n', 0), (P['conv_ln_b'], 'n', 0)])

    def conv_ln_bwd(du2, xhat, rstd, g):
        du1 = _ln_bwd_math(du2, xhat, rstd, g)
        return [du1, _colsum(du2 * xhat), _colsum(du2), _colsum(du1)]

    du1, S['conv_ln_g'], S['conv_ln_b'], S['conv_b'] = _rows(
        "conv_ln_bwd", conv_ln_bwd, [(du2, 'r', CC, 0), (xhc, 'r', CC, 0), (rsc, 'r', 1, 0), (P['conv_ln_g'], 'v', CC, 0)],
        [('r', CC, F32), ('a', CC, F32), ('a', CC, F32), ('a', CC, F32)], T=T, tb=512)
    rider = net.exchange(['w_out', 'w_ret_o', 'w_conv_o'])
    dglu_a, dglu_b, dkpad = _conv_bwd(du1, proj, kpad, off_glu, CC, T, rider=rider)
    net.done(rider)
    G['conv_k'] = dkpad[:CONV_WIDTH].astype(gdt)

    for off, piece in ((off_glu, dglu_a), (off_glu + CC, dglu_b), (off_gate, dgate_r), (off_gate + D, dgate_c)):
        dproj = lax.dynamic_update_slice(dproj, piece, (0, off))
    IN_W = dproj.shape[1]
    G['w_in'], S['b_in'] = _mm("d_w_in", [x1b], [dproj], [[(0, 0)]], ident, [gdt], ta=True, o3=True, bsum=True,
                               tm=1024, tn=W['w_in'].shape[2], tk=1024)
    cuts = [0, (9 * D) // 16, (53 * D) // 64, D]
    w_in_rows = [{'w_in': (cuts[i], cuts[i + 1] - cuts[i], i == 2)} for i in range(3)]
    rider = net.exchange(['w_in', 'conv_k'], w_in_rows[0])
    (dy1,) = _mm("d_x1", [dproj], [W['w_in']], [[(0, 0)]], lambda accs, dzr: [ALPHA * dzr + accs[0]], [F32], tb=True,
                 b3=True, tm=1024, tn=1024, tk=0, extras=[(dz2, 'mn', 0)], rider=rider)
    net.done(rider)
    dz1, dz1h, S['ln1_g'], S['ln1_b'] = _ln_bwd("ln1b", dy1, xh1, rs1, P['ln1_g'], 0.5, T, D)
    grad_x = _ffn_bwd("ffn1b", dz1h, dz1, xb, a1, b1, s1, W, FFN1, gdt, G, net, 'all',
                      pre=(lambda: net.exchange(['w_in'], w_in_rows[1]), lambda: net.exchange(['w_in'], w_in_rows[2])))
    return loss, grad_x, G, S


def _coords():
    return lax.axis_index("x"), lax.axis_index("y"), lax.axis_index("c")


def _flip(k, x, y, c):
    return (1 - x if k & 4 else x, 1 - y if k & 2 else y, 1 - c if k & 1 else c)


def _lin(p):
    return 4 * p[0] + 2 * p[1] + p[2]


class _Rider:
    def __init__(self, ins, out_shape, rows=None, fill=None):
        nb = len(ins)
        self.rows = rows or [None] * nb
        fill = fill or [None] * nb
        self.aliases = {nb + i: w for i, w in enumerate(w for w in range(nb) if fill[w] is not None)}
        self.ins = list(ins) + [f for f in fill if f is not None]
        self.out_shape, self.results = list(out_shape), None
        self.scratch = [pltpu.SemaphoreType.DMA((7 * nb,)), pltpu.SemaphoreType.DMA((7 * nb,)),
                        pltpu.SemaphoreType.DMA((nb,))]

    def span(self, w, ref, *slot):
        if self.rows[w] is None:
            return ref.at[slot] if slot else ref
        return ref.at[(*slot, pl.ds(*self.rows[w]))]

    def begin(self, step, n_steps, ins, outs, sems):
        @pl.when(step == 0)
        def _():
            self.start(ins, outs, sems)

        @pl.when(step == n_steps - 1)
        def _():
            self.mid(ins, outs, sems)

    def end(self, step, n_steps, ins, outs, sems):
        @pl.when(step == n_steps - 1)
        def _():
            self.finish(ins, outs, sems)

    def mid(self, ins, outs, sems):
        pass


class _GatherRider(_Rider):
    def __init__(self, blks, rows=None, fill=None):
        super().__init__(blks, [jax.ShapeDtypeStruct((N_DEV,) + b.shape, b.dtype) for b in blks], rows, fill)

    def _copies(self, x_refs, out_refs, sems):
        nb = len(self.out_shape)
        send_sems, recv_sems, local_sems = sems
        x, y, c = _coords()
        me, sibling = (x, y, c), (x, y, 1 - c)
        chips = [_flip(4, x, y, c), _flip(2, x, y, c), _flip(6, x, y, c)]
        own = [self.span(w, x_refs[w]) for w in range(nb)]

        def copy(k, w, block, to, src=None):
            slot = self.span(w, out_refs[w], _lin(block))
            return pltpu.make_async_remote_copy(
                src_ref=slot if src is None else src, dst_ref=slot, send_sem=send_sems.at[k * nb + w],
                recv_sem=recv_sems.at[k * nb + w], device_id=to, device_id_type=MESH)

        mines = [pltpu.make_async_copy(own[w], self.span(w, out_refs[w], _lin(me)), local_sems.at[w])
                 for w in range(nb)]
        first = [copy(0, w, me, sibling, src=own[w]) for w in range(nb)]
        first += [copy(1 + j, w, me, chip, src=own[w]) for w in range(nb) for j, chip in enumerate(chips)]
        landed = [(copy(1 + j, w, chip, me), copy(4 + j, w, chip, sibling)) for w in range(nb) for j, chip in enumerate(chips)]
        from_sibling = [copy(0, w, sibling, me) for w in range(nb)]
        from_sibling += [copy(4 + j, w, (chip[0], chip[1], 1 - c), me) for w in range(nb) for j, chip in enumerate(chips)]
        return mines, first, landed, from_sibling

    def start(self, ins, outs, sems):
        mines, first, _, _ = self._copies(ins, outs, sems)
        for cp in mines + first:
            cp.start()

    def mid(self, ins, outs, sems):
        for arrival, onward in self._copies(ins, outs, sems)[2]:
            arrival.wait_recv()
            onward.start()

    def finish(self, ins, outs, sems):
        mines, first, landed, from_sibling = self._copies(ins, outs, sems)
        for cp in from_sibling:
            cp.wait_recv()
        for cp in first + [onward for _, onward in landed]:
            cp.wait_send()
        for mine in mines:
            mine.wait()


class _ExchangeRider(_Rider):
    def __init__(self, gs, rows=None, fill=None):
        super().__init__(gs, [jax.ShapeDtypeStruct(g.shape, g.dtype) for g in gs], rows, fill)

    def _copies(self, g_refs, out_refs, sems):
        nb = len(self.out_shape)
        send_sems, recv_sems, local_sems = sems
        x, y, c = _coords()
        me = _lin((x, y, c))

        def copy(k, w, landing):
            peer = _flip(k, x, y, c)
            src, dst = (me, _lin(peer)) if landing else (_lin(peer), me)
            return pltpu.make_async_remote_copy(
                src_ref=self.span(w, g_refs[w], src), dst_ref=self.span(w, out_refs[w], dst),
                send_sem=send_sems.at[(k - 1) * nb + w], recv_sem=recv_sems.at[(k - 1) * nb + w],
                device_id=peer, device_id_type=MESH)

        mines = [pltpu.make_async_copy(self.span(w, g_refs[w], me), self.span(w, out_refs[w], me), local_sems.at[w])
                 for w in range(nb)]
        sends = [copy(k, w, False) for w in range(nb) for k in range(1, N_DEV)]
        landings = [copy(k, w, True) for w in range(nb) for k in range(1, N_DEV)]
        return mines, sends, landings

    def start(self, ins, outs, sems):
        mines, sends, _ = self._copies(ins, outs, sems)
        for cp in mines + sends:
            cp.start()

    def finish(self, ins, outs, sems):
        mines, sends, landings = self._copies(ins, outs, sems)
        for cp in landings:
            cp.wait_recv()
        for cp in sends:
            cp.wait_send()
        for mine in mines:
            mine.wait()


def _run_rider(name, rider):
    n_in, n_out = len(rider.ins), len(rider.out_shape)

    def body(*refs):
        ride = (refs[:n_in], refs[n_in:n_in + n_out], refs[n_in + n_out:])
        rider.start(*ride)
        rider.mid(*ride)
        rider.finish(*ride)

    rider.results = pl.pallas_call(
        body, name=name, out_shape=rider.out_shape, in_specs=[ANY] * n_in, out_specs=[ANY] * n_out,
        scratch_shapes=rider.scratch, input_output_aliases=dict(rider.aliases),
        compiler_params=pltpu.CompilerParams(has_side_effects=True),
    )(*rider.ins)
    return rider.results


def _as_matrix(name, g):
    if name == 'w_in':
        return g
    if name in COL_SHARDED:
        return jnp.transpose(g, (1, 0, 2)).reshape(g.shape[1], N_DEV * g.shape[2])
    return g.reshape(N_DEV * g.shape[1], g.shape[2])


def _by_owner(name, g):
    if name == 'w_in':
        return g
    if name in COL_SHARDED:
        return jnp.transpose(g.reshape(g.shape[0], N_DEV, g.shape[1] // N_DEV), (1, 0, 2))
    return g.reshape(N_DEV, g.shape[0] // N_DEV, g.shape[1])


class _Comm:
    def __init__(self, shards):
        self.shards, self.W, self.parts, self.partial, self.sent = shards, {}, {}, {}, {}

    def _ride(self, cls, names, srcs, part, sink):
        part = part or {}
        rider = cls(srcs, rows=[part[n][:2] if n in part else None for n in names],
                    fill=[self.partial.pop((sink, n), None) for n in names])
        rider.names, rider.sink = names, sink
        rider.unfinished = {n for n in names if n in part and not part[n][2]}
        return rider

    def gather(self, names, part=None):
        return self._ride(_GatherRider, names, [self.shards[n] for n in names], part, 'W')

    def exchange(self, names, G, part=None):
        for n in names:
            if n not in self.sent:
                self.sent[n] = _by_owner(n, G[n])
        return self._ride(_ExchangeRider, names, [self.sent[n] for n in names], part, 'parts')

    def collect(self, rider):
        for n, res in zip(rider.names, rider.results):
            if n in rider.unfinished:
                self.partial[(rider.sink, n)] = res
            elif rider.sink == 'W':
                self.W[n] = _as_matrix(n, res)
            else:
                self.parts[n] = res


def _adamw_math(p_ref, w_ref, m_ref, v_ref, g_ref, d_ref, nm_ref, nv_ref):
    c1 = 1.0 - ADAM_B1 ** ADAM_STEP
    c2 = 1.0 - ADAM_B2 ** ADAM_STEP
    g = p_ref[0].astype(F32)
    for s in range(1, p_ref.shape[0]):
        g = g + p_ref[s].astype(F32)
    nm = ADAM_B1 * m_ref[...] + (1.0 - ADAM_B1) * g
    nv = ADAM_B2 * v_ref[...] + (1.0 - ADAM_B2) * (g * g)
    g_ref[...] = g
    nm_ref[...] = nm
    nv_ref[...] = nv
    d_ref[...] = -ADAM_LR * ((nm / c1) / (jnp.sqrt(nv / c2) + ADAM_EPS) + ADAM_WD * w_ref[...])


def _adamw_vectors(parts, ws, ms, vs, loss_parts):
    k = len(ws)

    def body(*refs):
        for i in range(k):
            _adamw_math(refs[i], refs[k + i], refs[2 * k + i], refs[3 * k + i], *refs[4 * k + 1 + 4 * i:4 * k + 5 + 4 * i])
        lp, lo = refs[4 * k], refs[8 * k + 1]
        lo[...] = functools.reduce(jnp.add, [lp[s] for s in range(lp.shape[0])])

    return pl.pallas_call(
        body, name="adamw_vectors",
        out_shape=[jax.ShapeDtypeStruct(w.shape, F32) for w in ws for _ in range(4)] + [jax.ShapeDtypeStruct((1, 128), F32)],
        compiler_params=_params(),
    )(*parts, *ws, *ms, *vs, loss_parts)


def _adamw(name, parts, w, m, v, tb):
    n, R, Wd = parts.shape
    assert R % tb == 0
    body = functools.partial(_adamw_math)

    row = pl.BlockSpec((tb, Wd), lambda i: (i, 0))
    return pl.pallas_call(
        body, name=name, grid=(R // tb,),
        in_specs=[pl.BlockSpec((n, tb, Wd), lambda i: (0, i, 0)), row, row, row],
        out_specs=[row, row, row, row], out_shape=[jax.ShapeDtypeStruct((R, Wd), F32)] * 4,
        compiler_params=_params(("arbitrary",)),
    )(parts, w, m, v)


def _row_tile(R, unit, cap):
    best = unit
    for t in range(unit, cap + 1, unit):
        if R % t == 0:
            best = t
    return best


def kernel(x, ffn1_w_gate, ffn1_w_up, ffn1_w_down, ln1_g, ln1_b, w_in, b_in, ret_gn_g, conv_k, conv_b, conv_ln_g, conv_ln_b, w_ret_o, w_conv_o, w_out, ln2_g, ln2_b, ffn2_w_gate, ffn2_w_up, ffn2_w_down, ln3_g, ln3_b, loss_target, m_ffn1_w_gate, m_ffn1_w_up, m_ffn1_w_down, m_ln1_g, m_ln1_b, m_w_in, m_b_in, m_ret_gn_g, m_conv_k, m_conv_b, m_conv_ln_g, m_conv_ln_b, m_w_ret_o, m_w_conv_o, m_w_out, m_ln2_g, m_ln2_b, m_ffn2_w_gate, m_ffn2_w_up, m_ffn2_w_down, m_ln3_g, m_ln3_b, v_ffn1_w_gate, v_ffn1_w_up, v_ffn1_w_down, v_ln1_g, v_ln1_b, v_w_in, v_b_in, v_ret_gn_g, v_conv_k, v_conv_b, v_conv_ln_g, v_conv_ln_b, v_w_ret_o, v_w_conv_o, v_w_out, v_ln2_g, v_ln2_b, v_ffn2_w_gate, v_ffn2_w_up, v_ffn2_w_down, v_ln3_g, v_ln3_b):
    given = dict(locals())
    wts = {n: given[n] for n in WEIGHTS}
    mom = {n: given['m_' + n] for n in WEIGHTS}
    var = {n: given['v_' + n] for n in WEIGHTS}

    def shard2d(a):
        return a.reshape(a.shape[-3] * a.shape[-2] if a.ndim == 4 else a.shape[-2], a.shape[-1])

    comm = _Comm({n: shard2d(wts[n]).astype(BF16) for n in BIG})
    P = {n: wts[n].reshape(1, -1) for n in SMALL}
    loss, grad_x, _, S = _local_step(x[0], loss_target[0], None, P, comm=comm)

    parts = comm.parts
    res = {}
    for n in BIG:
        rows, cols = parts[n].shape[1:]
        tb = rows if rows % 16 else _row_tile(rows, 16, max(16, (256 * 1024) // cols))
        res[n] = _adamw("adamw_" + n, parts[n], shard2d(wts[n]), shard2d(mom[n]), shard2d(var[n]), tb)

    vec_parts = _run_rider("gather_vector_grads", _GatherRider([S[n] for n in SMALL] + [loss]))
    vec = _adamw_vectors(vec_parts[:-1], [P[n] for n in SMALL], [mom[n].reshape(1, -1) for n in SMALL],
                         [var[n].reshape(1, -1) for n in SMALL], vec_parts[-1])
    for i, n in enumerate(SMALL):
        res[n] = vec[4 * i:4 * i + 4]

    outs = [vec[-1][0, 0], grad_x[None]]
    for k in range(4):
        for n in WEIGHTS:
            outs.append(res[n][k].reshape(wts[n].shape))
    return tuple(outs)
```

```python
import functools
import math

import jax
import jax.numpy as jnp
from jax import lax
from jax.experimental import pallas as pl
from jax.experimental.pallas import tpu as pltpu

F32 = jnp.float32
BF16 = jnp.bfloat16

N_DEV = 8
LN_EPS = 1e-5
ALPHA = 2.0 ** 0.25
RET_DK = 128
RET_DV = 256
RET_CHUNK = 256
ROPE_BASE = 10000.0
CONV_WIDTH = 31
HALO = 32
ADAM_LR, ADAM_B1, ADAM_B2, ADAM_EPS, ADAM_WD, ADAM_STEP = 0.001, 0.9, 0.999, 1e-08, 0.01, 10
VMEM_LIMIT = 52 * 1024 * 1024
MESH = pl.DeviceIdType.MESH
ANY = pl.BlockSpec(memory_space=pl.ANY)

BIG = ['ffn1_w_gate', 'ffn1_w_up', 'ffn1_w_down', 'w_in', 'w_ret_o', 'w_conv_o', 'w_out',
       'ffn2_w_gate', 'ffn2_w_up', 'ffn2_w_down', 'conv_k']
COL_SHARDED = {'ffn1_w_gate', 'ffn1_w_up', 'w_in', 'ffn2_w_gate', 'ffn2_w_up', 'conv_k'}
SMALL = ['ln1_g', 'ln1_b', 'b_in', 'ret_gn_g', 'conv_b', 'conv_ln_g', 'conv_ln_b', 'ln2_g', 'ln2_b', 'ln3_g', 'ln3_b']
WEIGHTS = ['ffn1_w_gate', 'ffn1_w_up', 'ffn1_w_down', 'ln1_g', 'ln1_b', 'w_in', 'b_in', 'ret_gn_g', 'conv_k', 'conv_b',
           'conv_ln_g', 'conv_ln_b', 'w_ret_o', 'w_conv_o', 'w_out', 'ln2_g', 'ln2_b', 'ffn2_w_gate', 'ffn2_w_up',
           'ffn2_w_down', 'ln3_g', 'ln3_b']


def _params(sem=None):
    return pltpu.CompilerParams(dimension_semantics=sem, vmem_limit_bytes=VMEM_LIMIT)


def _sigmoid(x):
    return jax.nn.sigmoid(x)


def _dsilu(x, sg):
    return sg * (1.0 + x * (1.0 - sg))


def _fit(dim, want):
    if dim <= want:
        return dim
    return max(t for t in range(128, want + 1, 128) if dim % t == 0)


def _dot(a, b, ta=False, tb=False):
    dn = (((0,) if ta else (1,), (1,) if tb else (0,)), ((), ()))
    return lax.dot_general(a, b, dn, preferred_element_type=F32)


def _mm(name, As, Bs, prods, epi, out_dtypes, *, ta=False, tb=False, tm, tn, tk, extras=(), i_outer=True,
        b3=False, o3=False, rider=None, bsum=False, epi_rows=0):
    a0, b0 = As[0], Bs[0]
    M, K = (a0.shape[1], a0.shape[0]) if ta else a0.shape
    if b3:
        S, rows, cs = b0.shape
        N = rows if tb else S * cs
        assert K == (S * cs if tb else rows)
        tn, tk = (tn, cs) if tb else (cs, tk)
    else:
        N = b0.shape[0] if tb else b0.shape[1]
    tm, tn, tk = _fit(M, tm), _fit(N, tn), _fit(K, tk)
    assert M % tm == 0 and N % tn == 0 and K % tk == 0, (name, M, N, K, tm, tn, tk)
    gi, gj, gk = M // tm, N // tn, K // tk
    grid = (gi, gj, gk) if i_outer else (gj, gi, gk)

    def ij(g0, g1):
        return (g0, g1) if i_outer else (g1, g0)

    def amap(g0, g1, k):
        i, _ = ij(g0, g1)
        return (k, i) if ta else (i, k)

    def bmap(g0, g1, k):
        _, j = ij(g0, g1)
        return (j, k) if tb else (k, j)

    def bmap3(g0, g1, k):
        _, j = ij(g0, g1)
        return (k, j, 0) if tb else (j, k, 0)

    in_specs = [pl.BlockSpec((tk, tm) if ta else (tm, tk), amap) for _ in As]
    if b3:
        in_specs += [pl.BlockSpec((None, tn, tk) if tb else (None, tk, tn), bmap3) for _ in Bs]
    else:
        in_specs += [pl.BlockSpec((tn, tk) if tb else (tk, tn), bmap) for _ in Bs]
    args = list(As) + list(Bs)
    for arr, kind, coloff in extras:
        assert coloff % tn == 0
        off = coloff // tn
        if kind == 'mn':
            in_specs.append(pl.BlockSpec((tm, tn), lambda g0, g1, k, off=off: (ij(g0, g1)[0], ij(g0, g1)[1] + off)))
        else:
            in_specs.append(pl.BlockSpec((1, tn), lambda g0, g1, k, off=off: (0, ij(g0, g1)[1] + off)))
        args.append(arr)
    if o3:
        out_shape = [jax.ShapeDtypeStruct((gj, M, tn), dt) for dt in out_dtypes]
        out_specs = [pl.BlockSpec((None, tm, tn), lambda g0, g1, k: (ij(g0, g1)[1], ij(g0, g1)[0], 0))
                     for _ in out_dtypes]
    else:
        out_shape = [jax.ShapeDtypeStruct((M, N), dt) for dt in out_dtypes]
        out_specs = [pl.BlockSpec((tm, tn), lambda g0, g1, k: ij(g0, g1)) for _ in out_dtypes]
    if bsum:
        assert gi == 1 and not tb and not b3
        out_shape.append(jax.ShapeDtypeStruct((1, N), F32))
        out_specs.append(pl.BlockSpec((1, tn), lambda g0, g1, k: (0, ij(g0, g1)[1])))
    n_a, n_b, n_e, n_o = len(As), len(Bs), len(extras), len(out_shape)
    n_p = len(prods) if gk > 1 else 0
    scratch = [pltpu.VMEM((tm, tn), F32) for _ in range(n_p)]
    if rider is not None:
        in_specs, out_specs = in_specs + [ANY] * len(rider.ins), out_specs + [ANY] * len(rider.out_shape)
        args, out_shape, scratch = args + rider.ins, out_shape + rider.out_shape, scratch + rider.scratch
    n_in, n_out = len(args), len(out_shape)

    def body(*refs):
        a_refs = refs[:n_a]
        b_refs = refs[n_a:n_a + n_b]
        e_refs = refs[n_a + n_b:n_a + n_b + n_e]
        o_refs = refs[n_in:n_in + n_o]
        acc_refs = refs[n_in + n_out:n_in + n_out + n_p]
        k = pl.program_id(2)
        if rider is not None:
            step = (pl.program_id(0) * grid[1] + pl.program_id(1)) * gk + k
            ride = (step, grid[0] * grid[1] * gk, refs[n_a + n_b + n_e:n_in], refs[n_in + n_o:n_in + n_out],
                    refs[n_in + n_out + n_p:])
            rider.begin(*ride)

        def finish(accs, rows=slice(None)):
            ex = [(e[rows, :] if kind == 'mn' else e[...]).astype(F32) for e, (_, kind, _) in zip(e_refs, extras)]
            for o, r in zip(o_refs, epi(accs, *ex)):
                o[rows, :] = r.astype(o.dtype)

        if bsum:
            @pl.when(k == 0)
            def _():
                o_refs[-1][...] = jnp.zeros_like(o_refs[-1])

            o_refs[-1][...] += _colsum(b_refs[0][...].astype(F32))

        if gk == 1:
            sub = tm if ta or not epi_rows else _fit(tm, epi_rows)
            for r0 in range(0, tm, sub):
                rows = slice(None) if ta else slice(r0, r0 + sub)
                finish([functools.reduce(jnp.add, [_dot(a_refs[ai][...] if ta else a_refs[ai][rows, :],
                                                        b_refs[bi][...], ta, tb) for ai, bi in terms])
                        for terms in prods], rows)
        else:
            @pl.when(k == 0)
            def _():
                for acc in acc_refs:
                    acc[...] = jnp.zeros_like(acc)

            for p, terms in enumerate(prods):
                for ai, bi in terms:
                    acc_refs[p][...] += _dot(a_refs[ai][...], b_refs[bi][...], ta, tb)

            @pl.when(k == gk - 1)
            def _():
                finish([acc[...] for acc in acc_refs])

        if rider is not None:
            rider.end(*ride)

    aliases = {} if rider is None else {n_a + n_b + n_e + p: n_o + o for p, o in rider.aliases.items()}
    res = pl.pallas_call(
        body, name=name, grid=grid, in_specs=in_specs, out_specs=out_specs, out_shape=out_shape,
        scratch_shapes=scratch, input_output_aliases=aliases,
        compiler_params=_params(("arbitrary", "arbitrary", "arbitrary")),
    )(*args)
    if rider is not None:
        rider.results = res[n_o:]
    return res[:n_o]


def _rows(name, fn, ins, outs, *, T, tb):
    tb = min(tb, T)
    assert T % tb == 0
    in_specs, args = [], []
    for arr, kind, width, cb in ins:
        if kind == 'r':
            in_specs.append(pl.BlockSpec((tb, width), lambda i, cb=cb: (i, cb)))
        else:
            in_specs.append(pl.BlockSpec((1, width), lambda i, cb=cb: (0, cb)))
        args.append(arr)
    out_shape, out_specs = [], []
    for kind, width, dtype in outs:
        if kind == 'r':
            out_shape.append(jax.ShapeDtypeStruct((T, width), dtype))
            out_specs.append(pl.BlockSpec((tb, width), lambda i: (i, 0)))
        elif kind == 'c':
            out_shape.append(jax.ShapeDtypeStruct((T, 1), dtype))
            out_specs.append(pl.BlockSpec((tb, 1), lambda i: (i, 0)))
        else:
            out_shape.append(jax.ShapeDtypeStruct((1, width), F32))
            out_specs.append(pl.BlockSpec((1, width), lambda i: (0, 0)))
    n_in = len(ins)

    def body(*refs):
        i = pl.program_id(0)
        vals = fn(*[r[...] for r in refs[:n_in]])
        for (kind, _, _), o, v in zip(outs, refs[n_in:], vals):
            if kind == 'a':
                @pl.when(i == 0)
                def _(o=o):
                    o[...] = jnp.zeros_like(o)

                o[...] += v
            else:
                o[...] = v.astype(o.dtype)

    return pl.pallas_call(
        body, name=name, grid=(T // tb,), in_specs=in_specs, out_specs=out_specs, out_shape=out_shape,
        compiler_params=_params(("arbitrary",)),
    )(*args)


def _colsum(v):
    return jnp.sum(v, axis=0, keepdims=True)


def _ln_stats(z):
    mu = jnp.mean(z, axis=-1, keepdims=True)
    d = z - mu
    var = jnp.mean(d * d, axis=-1, keepdims=True)
    rstd = lax.rsqrt(var + LN_EPS)
    return d * rstd, rstd


def _ln_bwd_math(dy, xhat, rstd, g):
    dxh = dy * g
    m1 = jnp.mean(dxh, axis=-1, keepdims=True)
    m2 = jnp.mean(dxh * xhat, axis=-1, keepdims=True)
    return rstd * (dxh - m1 - xhat * m2)


def _ln_fwd(name, z, g, b, T, D):
    def fn(z, g, b):
        xhat, rstd = _ln_stats(z)
        y = xhat * g + b
        return [y, y, xhat, rstd]

    return _rows(name, fn, [(z, 'r', D, 0), (g, 'v', D, 0), (b, 'v', D, 0)],
                 [('r', D, F32), ('r', D, BF16), ('r', D, F32), ('c', 1, F32)], T=T, tb=512)


def _ln_bwd(name, dy, xhat, rstd, g, scale, T, D):
    def fn(dy, xhat, rstd, g):
        dz = _ln_bwd_math(dy, xhat, rstd, g)
        return [dz, dz * scale, _colsum(dy * xhat), _colsum(dy)]

    return _rows(name, fn, [(dy, 'r', D, 0), (xhat, 'r', D, 0), (rstd, 'r', 1, 0), (g, 'v', D, 0)],
                 [('r', D, F32), ('r', D, BF16), ('a', D, F32), ('a', D, F32)], T=T, tb=512)


def _ln_loss_bwd(name, z, g, b, tgt, T, D):
    def fn(z, g, b, tgt):
        xhat, rstd = _ln_stats(z)
        err = xhat * g + b - tgt
        row_loss = 0.5 * jnp.mean(err * err, axis=-1, keepdims=True)
        loss = jnp.broadcast_to(jnp.sum(row_loss, axis=0, keepdims=True), (1, 128))
        dy = err * (1.0 / D)
        dz = _ln_bwd_math(dy, xhat, rstd, g)
        return [dz, dz * 0.5, _colsum(dy * xhat), _colsum(dy), loss]

    return _rows(name, fn, [(z, 'r', D, 0), (g, 'v', D, 0), (b, 'v', D, 0), (tgt, 'r', D, 0)],
                 [('r', D, F32), ('r', D, BF16), ('a', D, F32), ('a', D, F32), ('a', 128, F32)], T=T, tb=512)


class _Net:
    def __init__(self, comm, G):
        self.comm, self.G = comm, G

    def gather(self, names, part=None):
        return self.comm.gather(names, part) if self.comm else None

    def exchange(self, names, part=None):
        return self.comm.exchange(names, self.G, part) if self.comm else None

    def done(self, rider):
        if rider is not None:
            self.comm.collect(rider)


def _ffn_fwd(tag, xb, x, W, names, net, rider=None, rider_down=None):
    def epi_gu(accs):
        a, b = accs
        return [a, b, a * _sigmoid(a) * b]

    ng, nu, nd = names
    a, b, s = _mm(tag + "_gate_up", [xb], [W[ng], W[nu]], [[(0, 0)], [(0, 1)]], epi_gu, [BF16, BF16, BF16],
                  tm=1024, tn=1408, tk=1024, rider=rider, epi_rows=256)
    net.done(rider)

    def epi_down(accs, xres):
        return [ALPHA * xres + 0.5 * accs[0]]

    rider_down = rider_down() if rider_down else None
    (z,) = _mm(tag + "_down", [s], [W[nd]], [[(0, 0)]], epi_down, [F32], tm=1024, tn=1024, tk=1408,
               extras=[(x, 'mn', 0)], rider=rider_down)
    net.done(rider_down)
    return a, b, s, z


def _ffn_bwd(tag, dzh, dz, xb, a, b, s, W, names, gdt, G, net, ride, pre=(None, None)):
    ng, nu, nd = names

    def epi_ds(accs, a, b):
        ds = accs[0]
        sg = _sigmoid(a)
        return [ds * b * _dsilu(a, sg), ds * a * sg]

    rider = pre[0]() if pre[0] else None
    da, db = _mm(tag + "_ds", [dzh], [W[nd]], [[(0, 0)]], epi_ds, [BF16, BF16], tb=True, tm=1024, tn=1408, tk=1024, epi_rows=256,
                 extras=[(a, 'mn', 0), (b, 'mn', 0)], rider=rider)
    net.done(rider)
    ident = lambda accs: accs
    rider = pre[1]() if pre[1] else None
    (G[nd],) = _mm(tag + "_dwd", [s], [dzh], [[(0, 0)]], ident, [gdt], ta=True, tm=1408, tn=1024, tk=1024,
                   rider=rider)
    net.done(rider)
    if ride == 'all':
        rider = net.exchange([nd])
        (G[ng],) = _mm(tag + "_dwg", [xb], [da], [[(0, 0)]], ident, [gdt], ta=True, tm=1024, tn=1408, tk=1024,
                       rider=rider)
        net.done(rider)
        rider = net.exchange([ng])
        (G[nu],) = _mm(tag + "_dwu", [xb], [db], [[(0, 0)]], ident, [gdt], ta=True, tm=1024, tn=1408, tk=1024,
                       rider=rider)
        net.done(rider)
    else:
        G[ng], G[nu] = _mm(tag + "_dwgu", [xb], [da, db], [[(0, 0)], [(0, 1)]], ident, [gdt, gdt], ta=True,
                           tm=1024, tn=1408, tk=1024)

    def epi_dx(accs, dzres):
        return [ALPHA * dzres + accs[0]]

    rider = net.exchange([nu] if ride == 'all' else [nd]) if ride else None
    (dx,) = _mm(tag + "_dx", [da, db], [W[ng], W[nu]], [[(0, 0), (1, 1)]], epi_dx, [F32], tb=True,
                tm=1024, tn=1024, tk=1408, extras=[(dz, 'mn', 0)], rider=rider)
    net.done(rider)
    return dx


def _ret_tables(H, T):
    C = RET_CHUNK
    log_g = jnp.log(1.0 - jnp.exp2(-5.0 - jnp.arange(H, dtype=F32)))
    idx = jnp.arange(C, dtype=F32)
    diff = idx[:, None] - idx[None, :]
    dm = jnp.where(diff[None] >= 0, jnp.exp(jnp.maximum(diff, 0.0)[None] * log_g[:, None, None]), 0.0)
    xi = jnp.exp((idx[None, :] + 1.0) * log_g[:, None])[:, :, None]
    zeta = jnp.exp((C - 1.0 - idx)[None, :] * log_g[:, None])[:, :, None]
    gc = jnp.broadcast_to(jnp.exp(C * log_g)[:, None, None], (H, 1, RET_DV))
    half = RET_DK // 2
    freqs = ROPE_BASE ** (-jnp.arange(half, dtype=F32) / half)
    ang = jnp.arange(T, dtype=F32)[:, None] * freqs[None, :]
    cos, sin = jnp.cos(ang), jnp.sin(ang)
    cosf = jnp.concatenate([cos, cos], axis=1)
    sins = jnp.concatenate([-sin, sin], axis=1)
    return dm, xi, zeta, gc, cosf, sins


def _rot(x, cosf, sins):
    return x * cosf + pltpu.roll(x, RET_DK // 2, 1) * sins


def _rot_bwd(dy, cosf, sins):
    return dy * cosf + pltpu.roll(dy * sins, RET_DK // 2, 1)


RET_HB = 8


def _ret_specs(H, HB, rev, NC):
    C, G = RET_CHUNK, H // HB
    nn = (lambda n: NC - 1 - n) if rev else (lambda n: n)
    return [
        pl.BlockSpec((C, HB * RET_DK), lambda h, n: (nn(n), h)),
        pl.BlockSpec((C, HB * RET_DK), lambda h, n: (nn(n), G + h)),
        pl.BlockSpec((C, HB * RET_DV), lambda h, n: (nn(n), G + h)),
        pl.BlockSpec((C, HB * RET_DV), lambda h, n: (nn(n), 2 * G + h)),
        pl.BlockSpec((C, RET_DK), lambda h, n: (nn(n), 0)),
        pl.BlockSpec((C, RET_DK), lambda h, n: (nn(n), 0)),
        pl.BlockSpec((1, HB * RET_DV), lambda h, n: (0, h)),
        pl.BlockSpec((HB, C, C), lambda h, n: (h, 0, 0)),
        pl.BlockSpec((HB, C, 1), lambda h, n: (h, 0, 0)),
        pl.BlockSpec((HB, C, 1), lambda h, n: (h, 0, 0)),
        pl.BlockSpec((HB, 1, RET_DV), lambda h, n: (h, 0, 0)),
    ]


def _ret_fwd(proj, gn_g, tabs, H, T, rider=None):
    C, NC = RET_CHUNK, T // RET_CHUNK
    HB = min(RET_HB, H)
    dm, xi, zeta, gc, cosf, sins = tabs
    scale = RET_DK ** -0.5

    def body(q_ref, k_ref, v_ref, g_ref, cos_ref, sin_ref, gn_ref, dm_ref, xi_ref, zt_ref, gc_ref,
             r_ref, ri_ref, st_ref, state):
        @pl.when(pl.program_id(1) == 0)
        def _():
            state[...] = jnp.zeros_like(state)

        cs, sn = cos_ref[...], sin_ref[...]
        hs = range(HB)
        qk = [slice(h * RET_DK, (h + 1) * RET_DK) for h in hs]
        vv = [slice(h * RET_DV, (h + 1) * RET_DV) for h in hs]
        kr = [_rot(k_ref[:, qk[h]].astype(F32), cs, sn) for h in hs]
        qb = [(_rot(q_ref[:, qk[h]].astype(F32), cs, sn) * scale).astype(BF16) for h in hs]
        kb = [kr[h].astype(BF16) for h in hs]
        kzb = [(kr[h] * zt_ref[h]).astype(BF16) for h in hs]
        vb = [v_ref[:, vv[h]].astype(BF16) for h in hs]
        st = [state[h] for h in hs]
        stb = [st[h].astype(BF16) for h in hs]
        sb = [(_dot(qb[h], kb[h], tb=True) * dm_ref[h]).astype(BF16) for h in hs]
        cross = [_dot(qb[h], stb[h]) for h in hs]
        kv = [_dot(kzb[h], vb[h], ta=True) for h in hs]
        intra = [_dot(sb[h], vb[h]) for h in hs]
        for h in hs:
            st_ref[h] = stb[h]
            state[h] = gc_ref[h] * st[h] + kv[h]
        for h in hs:
            r = intra[h] + cross[h] * xi_ref[h]
            rhat, _ = _ln_stats(r)
            g = g_ref[:, vv[h]].astype(F32)
            r_ref[:, vv[h]] = r
            ri_ref[:, vv[h]] = (g * _sigmoid(g) * (rhat * gn_ref[:, vv[h]])).astype(BF16)

    VW = H * RET_DV
    return _hosted_call(
        body, rider, name="ret_fwd", grid=(H // HB, NC), in_specs=_ret_specs(H, HB, False, NC),
        out_specs=[pl.BlockSpec((C, HB * RET_DV), lambda h, n: (n, h)),
                   pl.BlockSpec((C, HB * RET_DV), lambda h, n: (n, h)),
                   pl.BlockSpec((HB, None, RET_DK, RET_DV), lambda h, n: (h, n, 0, 0))],
        out_shape=[jax.ShapeDtypeStruct((T, VW), F32), jax.ShapeDtypeStruct((T, VW), BF16),
                   jax.ShapeDtypeStruct((H, NC, RET_DK, RET_DV), BF16)],
        scratch=[pltpu.VMEM((HB, RET_DK, RET_DV), F32)],
        args=[proj, proj, proj, proj, cosf, sins, gn_g, dm, xi, zeta, gc])


def _hosted_call(body, rider, *, name, grid, in_specs, out_specs, out_shape, scratch, args):
    n_in, n_out, n_scr = len(args), len(out_shape), len(scratch)
    if rider is None:
        hosted = body
    else:
        n_ri, n_ro = len(rider.ins), len(rider.out_shape)
        in_specs, out_specs = in_specs + [ANY] * n_ri, out_specs + [ANY] * n_ro
        args, out_shape, scratch = args + rider.ins, out_shape + rider.out_shape, scratch + rider.scratch

        def hosted(*refs):
            o0, s0 = n_in + n_ri, n_in + n_ri + n_out + n_ro
            step = pl.program_id(0) * grid[1] + pl.program_id(1)
            ride = (step, grid[0] * grid[1], refs[n_in:o0], refs[o0 + n_out:s0], refs[s0 + n_scr:])
            rider.begin(*ride)
            body(*refs[:n_in], *refs[o0:o0 + n_out], *refs[s0:s0 + n_scr])
            rider.end(*ride)

    aliases = {} if rider is None else {n_in + p: n_out + o for p, o in rider.aliases.items()}
    res = pl.pallas_call(
        hosted, name=name, grid=grid, in_specs=in_specs, out_specs=out_specs, out_shape=out_shape,
        scratch_shapes=scratch, input_output_aliases=aliases, compiler_params=_params(("arbitrary", "arbitrary")),
    )(*args)
    if rider is not None:
        rider.results = res[n_out:]
    return res[:n_out]


def _ret_bwd(dri, r, states, proj, gn_g, tabs, H, T, in_w, rider=None):
    C, NC = RET_CHUNK, T // RET_CHUNK
    HB = min(RET_HB, H)
    dm, xi, zeta, gc, cosf, sins = tabs
    scale = RET_DK ** -0.5

    def body(q_ref, k_ref, v_ref, g_ref, cos_ref, sin_ref, gn_ref, dm_ref, xi_ref, zt_ref, gc_ref,
             dri_ref, r_ref, st_ref, dp_ref, dgn_ref, dstate):
        @pl.when(pl.program_id(1) == 0)
        def _():
            dstate[...] = jnp.zeros_like(dstate)
            dgn_ref[...] = jnp.zeros_like(dgn_ref)

        cs, sn = cos_ref[...], sin_ref[...]
        hs = range(HB)
        qk = [slice(h * RET_DK, (h + 1) * RET_DK) for h in hs]
        vv = [slice(h * RET_DV, (h + 1) * RET_DV) for h in hs]
        qr = [_rot(q_ref[:, qk[h]].astype(F32), cs, sn) * scale for h in hs]
        kr = [_rot(k_ref[:, qk[h]].astype(F32), cs, sn) for h in hs]
        qb = [qr[h].astype(BF16) for h in hs]
        kb = [kr[h].astype(BF16) for h in hs]
        vb = [v_ref[:, vv[h]].astype(BF16) for h in hs]
        qxb = [(qr[h] * xi_ref[h]).astype(BF16) for h in hs]
        kzb = [(kr[h] * zt_ref[h]).astype(BF16) for h in hs]
        drb = []
        for h in hs:
            rhat, rstd = _ln_stats(r_ref[:, vv[h]])
            g, gn, dpre = g_ref[:, vv[h]].astype(F32), gn_ref[:, vv[h]], dri_ref[:, vv[h]]
            sg = _sigmoid(g)
            dp_ref[:, 2 * QW + VW + h * RET_DV:2 * QW + VW + (h + 1) * RET_DV] = (
                dpre * (rhat * gn) * _dsilu(g, sg)).astype(BF16)
            drn = dpre * (g * sg)
            dgn_ref[:, vv[h]] += _colsum(drn * rhat)
            drb.append(_ln_bwd_math(drn, rhat, rstd, gn).astype(BF16))
        ds1 = [dstate[h] for h in hs]
        ds1b = [ds1[h].astype(BF16) for h in hs]
        sb = [(_dot(qb[h], kb[h], tb=True) * dm_ref[h]).astype(BF16) for h in hs]
        dsb = [(_dot(drb[h], vb[h], tb=True) * dm_ref[h]).astype(BF16) for h in hs]
        dq_x = [_dot(drb[h], st_ref[h], tb=True) for h in hs]
        dk_x = [_dot(vb[h], ds1b[h], tb=True) for h in hs]
        dv_x = [_dot(kzb[h], ds1b[h]) for h in hs]
        dst = [_dot(qxb[h], drb[h], ta=True) for h in hs]
        for h in hs:
            dstate[h] = gc_ref[h] * ds1[h] + dst[h]
        dv_i = [_dot(sb[h], drb[h], ta=True) for h in hs]
        dq_i = [_dot(dsb[h], kb[h]) for h in hs]
        dk_i = [_dot(dsb[h], qb[h], ta=True) for h in hs]
        for h in hs:
            dp_ref[:, 2 * QW + h * RET_DV:2 * QW + (h + 1) * RET_DV] = (dv_i[h] + dv_x[h]).astype(BF16)
            dq = dq_i[h] + dq_x[h] * xi_ref[h]
            dk = dk_i[h] + dk_x[h] * zt_ref[h]
            dp_ref[:, qk[h]] = _rot_bwd(dq * scale, cs, sn).astype(BF16)
            dp_ref[:, QW + h * RET_DK:QW + (h + 1) * RET_DK] = _rot_bwd(dk, cs, sn).astype(BF16)

    VW, QW = H * RET_DV, H * RET_DK
    rv = lambda n: NC - 1 - n
    in_specs = _ret_specs(H, HB, True, NC) + [
        pl.BlockSpec((C, HB * RET_DV), lambda h, n: (rv(n), h)),
        pl.BlockSpec((C, HB * RET_DV), lambda h, n: (rv(n), h)),
        pl.BlockSpec((HB, None, RET_DK, RET_DV), lambda h, n: (h, rv(n), 0, 0)),
    ]
    assert HB == H
    return _hosted_call(
        body, rider, name="ret_bwd", grid=(1, NC), in_specs=in_specs,
        out_specs=[pl.BlockSpec((C, 2 * QW + 2 * VW), lambda h, n: (rv(n), 0)),
                   pl.BlockSpec((1, VW), lambda h, n: (0, 0))],
        out_shape=[jax.ShapeDtypeStruct((T, in_w), BF16), jax.ShapeDtypeStruct((1, VW), F32)],
        scratch=[pltpu.VMEM((HB, RET_DK, RET_DV), F32)],
        args=[proj, proj, proj, proj, cosf, sins, gn_g, dm, xi, zeta, gc, dri, r, states])


CONV_CW = 128
CONV_TB = 512


SUBLANES = 8


def _shift_copies(win, shifted, tb):
    n = tb + HALO - SUBLANES
    for s in range(1, SUBLANES):
        shifted[s - 1] = win[pl.ds(s, n), :]


def _tap(win, shifted, off, tb):
    s = off % SUBLANES
    if s == 0:
        return win[pl.ds(off, tb), :]
    return shifted[s - 1, pl.ds(off - s, tb), :]


def _conv_fwd(proj, kpad, bias, off_a, CC, T, rider=None):
    tb, cw = min(CONV_TB, T), CONV_CW
    hb = tb // HALO
    ca, cb = off_a // cw, (off_a + CC) // cw

    def body(a_ref, b_ref, ap_ref, bp_ref, k_ref, bias_ref, u1_ref, win, shifted):
        i = pl.program_id(0)
        keep = (i > 0).astype(F32)
        win[0:HALO, :] = ap_ref[...].astype(F32) * _sigmoid(bp_ref[...].astype(F32)) * keep
        win[HALO:, :] = a_ref[...].astype(F32) * _sigmoid(b_ref[...].astype(F32))
        _shift_copies(win, shifted, tb)
        acc = jnp.broadcast_to(bias_ref[...], (tb, cw))
        for w in range(CONV_WIDTH):
            acc = acc + k_ref[w:w + 1, :] * _tap(win, shifted, HALO - (CONV_WIDTH - 1) + w, tb)
        u1_ref[...] = acc

    prev = lambda i: jnp.maximum(i * hb - 1, 0)
    (u1,) = _hosted_call(
        body, rider, name="conv_fwd", grid=(T // tb, CC // cw),
        in_specs=[pl.BlockSpec((tb, cw), lambda i, c: (i, ca + c)),
                  pl.BlockSpec((tb, cw), lambda i, c: (i, cb + c)),
                  pl.BlockSpec((HALO, cw), lambda i, c: (prev(i), ca + c)),
                  pl.BlockSpec((HALO, cw), lambda i, c: (prev(i), cb + c)),
                  pl.BlockSpec((HALO, cw), lambda i, c: (0, c)),
                  pl.BlockSpec((1, cw), lambda i, c: (0, c))],
        out_specs=[pl.BlockSpec((tb, cw), lambda i, c: (i, c))],
        out_shape=[jax.ShapeDtypeStruct((T, CC), F32)],
        scratch=[pltpu.VMEM((tb + HALO, cw), F32), pltpu.VMEM((SUBLANES - 1, tb + HALO - SUBLANES, cw), F32)],
        args=[proj, proj, proj, proj, kpad, bias])
    return u1


def _conv_bwd(du1, proj, kpad, off_a, CC, T, rider=None):
    tb, cw = min(CONV_TB, T), CONV_CW
    hb = tb // HALO
    nt = T // tb
    ca, cb = off_a // cw, (off_a + CC) // cw

    def body(d_ref, dn_ref, a_ref, b_ref, ap_ref, bp_ref, k_ref, da_ref, db_ref, dk_ref, winu, wind, shu, shd):
        i = pl.program_id(1)
        a, b = a_ref[...].astype(F32), b_ref[...].astype(F32)
        sgb = _sigmoid(b)
        winu[0:HALO, :] = ap_ref[...].astype(F32) * _sigmoid(bp_ref[...].astype(F32)) * (i > 0).astype(F32)
        winu[HALO:, :] = a * sgb
        d = d_ref[...]
        wind[0:tb, :] = d
        wind[tb:, :] = dn_ref[...] * (i < nt - 1).astype(F32)

        @pl.when(i == 0)
        def _():
            dk_ref[...] = jnp.zeros_like(dk_ref)

        _shift_copies(winu, shu, tb)
        _shift_copies(wind, shd, tb)
        du0 = jnp.zeros((tb, cw), F32)
        for w in range(CONV_WIDTH):
            du0 = du0 + k_ref[w:w + 1, :] * _tap(wind, shd, CONV_WIDTH - 1 - w, tb)
            dk_ref[w:w + 1, :] += _colsum(_tap(winu, shu, HALO - (CONV_WIDTH - 1) + w, tb) * d)
        da_ref[...] = (du0 * sgb).astype(BF16)
        db_ref[...] = (du0 * a * sgb * (1.0 - sgb)).astype(BF16)

    prev = lambda i: jnp.maximum(i * hb - 1, 0)
    nxt = lambda i: jnp.minimum((i + 1) * hb, T // HALO - 1)
    return _hosted_call(
        body, rider, name="conv_bwd", grid=(CC // cw, nt),
        in_specs=[pl.BlockSpec((tb, cw), lambda c, i: (i, c)),
                  pl.BlockSpec((HALO, cw), lambda c, i: (nxt(i), c)),
                  pl.BlockSpec((tb, cw), lambda c, i: (i, ca + c)),
                  pl.BlockSpec((tb, cw), lambda c, i: (i, cb + c)),
                  pl.BlockSpec((HALO, cw), lambda c, i: (prev(i), ca + c)),
                  pl.BlockSpec((HALO, cw), lambda c, i: (prev(i), cb + c)),
                  pl.BlockSpec((HALO, cw), lambda c, i: (0, c))],
        out_specs=[pl.BlockSpec((tb, cw), lambda c, i: (i, c)),
                   pl.BlockSpec((tb, cw), lambda c, i: (i, c)),
                   pl.BlockSpec((HALO, cw), lambda c, i: (0, c))],
        out_shape=[jax.ShapeDtypeStruct((T, CC), BF16), jax.ShapeDtypeStruct((T, CC), BF16),
                   jax.ShapeDtypeStruct((HALO, CC), F32)],
        scratch=[pltpu.VMEM((tb + HALO, cw), F32), pltpu.VMEM((tb + HALO, cw), F32),
                 pltpu.VMEM((SUBLANES - 1, tb + HALO - SUBLANES, cw), F32),
                 pltpu.VMEM((SUBLANES - 1, tb + HALO - SUBLANES, cw), F32)],
        args=[du1, du1, proj, proj, proj, proj, kpad])


FFN1 = ('ffn1_w_gate', 'ffn1_w_up', 'ffn1_w_down')
FFN2 = ('ffn2_w_gate', 'ffn2_w_up', 'ffn2_w_down')


def _local_step(x, tgt, W, P, gdt=BF16, comm=None):
    T, D = x.shape
    G = {}
    net = _Net(comm, G)
    if comm is not None:
        W = comm.W
        first = net.gather(['ffn1_w_gate', 'ffn1_w_up'])
        _run_rider("gather_ffn1_in", first)
        net.done(first)
    VW = P['ret_gn_g'].shape[1]
    H = VW // RET_DV
    QW = H * RET_DK
    CC = P['conv_b'].shape[1]
    off_glu = 2 * QW + 2 * VW
    off_gate = off_glu + 2 * CC
    ident = lambda accs: accs
    xb = x.astype(BF16)

    a1, b1, s1, z1 = _ffn_fwd("ffn1", xb, x, W, FFN1, net,
                              rider=net.gather(['ffn1_w_down', 'w_in'], {'w_in': (0, D // 2, False)}),
                              rider_down=lambda: net.gather(['w_in'], {'w_in': (D // 2, D // 2, True)}))
    x1, x1b, xh1, rs1 = _ln_fwd("ln1", z1, P['ln1_g'], P['ln1_b'], T, D)

    rest = net.gather(['conv_k', 'w_ret_o', 'w_conv_o', 'w_out'])
    (proj,) = _mm("w_in", [x1b], [W['w_in']], [[(0, 0)]], lambda accs, bias: [accs[0] + bias], [F32],
                  tm=2048, tn=0, tk=1024, extras=[(P['b_in'], 'n', 0)], i_outer=False, b3=True, rider=rest)
    net.done(rest)
    tabs = _ret_tables(H, T)
    rider = net.gather(['ffn2_w_gate', 'ffn2_w_up'])
    r, ret_in, states = _ret_fwd(proj, P['ret_gn_g'], tabs, H, T, rider=rider)
    net.done(rider)
    kpad = jnp.pad(W['conv_k'].astype(F32), ((0, HALO - CONV_WIDTH), (0, 0)))
    rider = net.gather(['ffn2_w_down'])
    u1 = _conv_fwd(proj, kpad, P['conv_b'], off_glu, CC, T, rider=rider)
    net.done(rider)

    def conv_ln(u1, g, b):
        xhat, rstd = _ln_stats(u1)
        u2 = xhat * g + b
        return [xhat, rstd, u2 * _sigmoid(u2)]

    xhc, rsc, u3 = _rows("conv_ln", conv_ln, [(u1, 'r', CC, 0), (P['conv_ln_g'], 'v', CC, 0), (P['conv_ln_b'], 'v', CC, 0)],
                         [('r', CC, F32), ('c', 1, F32), ('r', CC, BF16)], T=T, tb=512)
    (ret_out,) = _mm("ret_o", [ret_in], [W['w_ret_o']], [[(0, 0)]], ident, [F32], tm=1024, tn=1024, tk=2048)

    def epi_merge(accs, ret_out, gr, gc):
        conv_out = accs[0]
        return [conv_out, _sigmoid(gr) * ret_out + _sigmoid(gc) * conv_out]

    conv_out, merged = _mm("conv_o_merge", [u3], [W['w_conv_o']], [[(0, 0)]], epi_merge, [F32, BF16],
                           tm=512, tn=D, tk=1024, epi_rows=256,
                           extras=[(ret_out, 'mn', 0), (proj, 'mn', off_gate), (proj, 'mn', off_gate + D)])
    (z2,) = _mm("w_out", [merged], [W['w_out']], [[(0, 0)]], lambda accs, xr: [ALPHA * xr + accs[0]], [F32],
                tm=1024, tn=1024, tk=1024, extras=[(x1, 'mn', 0)])
    x2, x2b, xh2, rs2 = _ln_fwd("ln2", z2, P['ln2_g'], P['ln2_b'], T, D)
    a2, b2, s2, z3 = _ffn_fwd("ffn2", x2b, x2, W, FFN2, net)
    dz3, dz3h, g_ln3_g, g_ln3_b, loss = _ln_loss_bwd("ln3_loss", z3, P['ln3_g'], P['ln3_b'], tgt, T, D)

    S = {'ln3_g': g_ln3_g, 'ln3_b': g_ln3_b}
    dy2 = _ffn_bwd("ffn2b", dz3h, dz3, x2b, a2, b2, s2, W, FFN2, gdt, G, net, 'down')
    dz2, dz2b, S['ln2_g'], S['ln2_b'] = _ln_bwd("ln2b", dy2, xh2, rs2, P['ln2_g'], 1.0, T, D)

    (G['w_out'],) = _mm("d_w_out", [merged], [dz2b], [[(0, 0)]], ident, [gdt], ta=True, tm=1024, tn=1024, tk=1024)

    def epi_dmerge(accs, ret_out, conv_out, gr, gc):
        dm_ = accs[0]
        sr, sc = _sigmoid(gr), _sigmoid(gc)
        return [dm_ * sr, dm_ * sc, dm_ * ret_out * sr * (1.0 - sr), dm_ * conv_out * sc * (1.0 - sc)]

    dret_out, dconv_out, dgate_r, dgate_c = _mm(
        "d_merge", [dz2b], [W['w_out']], [[(0, 0)]], epi_dmerge, [BF16, BF16, BF16, BF16], tb=True,
        tm=512, tn=D, tk=1024, epi_rows=256,
        extras=[(ret_out, 'mn', 0), (conv_out, 'mn', 0), (proj, 'mn', off_gate), (proj, 'mn', off_gate + D)])
    (G['w_ret_o'],) = _mm("d_w_ret_o", [ret_in], [dret_out], [[(0, 0)]], ident, [gdt], ta=True, tm=1024, tn=1024, tk=1024)
    (G['w_conv_o'],) = _mm("d_w_conv_o", [u3], [dconv_out], [[(0, 0)]], ident, [gdt], ta=True, tm=1024, tn=1024, tk=1024)
    (dri,) = _mm("d_ret_in", [dret_out], [W['w_ret_o']], [[(0, 0)]], ident, [F32], tb=True, tm=1024, tn=1024, tk=1024)
    rider = net.exchange(['ffn2_w_gate', 'ffn2_w_up'])
    dproj, S['ret_gn_g'] = _ret_bwd(dri, r, states, proj, P['ret_gn_g'], tabs, H, T, proj.shape[1], rider=rider)
    net.done(rider)

    def epi_du2(accs, xhat, g, b):
        u2 = xhat * g + b
        return [accs[0] * _dsilu(u2, _sigmoid(u2))]

    (du2,) = _mm("d_u3", [dconv_out], [W['w_conv_o']], [[(0, 0)]], epi_du2, [F32], tb=True, tm=512, tn=CC, tk=1024, epi_rows=256,
                 extras=[(xhc, 'mn', 0), (P['conv_ln_g'], 'n', 0), (P['conv_ln_b'], 'n', 0)])

    def conv_ln_bwd(du2, xhat, rstd, g):
        du1 = _ln_bwd_math(du2, xhat, rstd, g)
        return [du1, _colsum(du2 * xhat), _colsum(du2), _colsum(du1)]

    du1, S['conv_ln_g'], S['conv_ln_b'], S['conv_b'] = _rows(
        "conv_ln_bwd", conv_ln_bwd, [(du2, 'r', CC, 0), (xhc, 'r', CC, 0), (rsc, 'r', 1, 0), (P['conv_ln_g'], 'v', CC, 0)],
        [('r', CC, F32), ('a', CC, F32), ('a', CC, F32), ('a', CC, F32)], T=T, tb=512)
    rider = net.exchange(['w_out', 'w_ret_o', 'w_conv_o'])
    dglu_a, dglu_b, dkpad = _conv_bwd(du1, proj, kpad, off_glu, CC, T, rider=rider)
    net.done(rider)
    G['conv_k'] = dkpad[:CONV_WIDTH].astype(gdt)

    for off, piece in ((off_glu, dglu_a), (off_glu + CC, dglu_b), (off_gate, dgate_r), (off_gate + D, dgate_c)):
        dproj = lax.dynamic_update_slice(dproj, piece, (0, off))
    IN_W = dproj.shape[1]
    G['w_in'], S['b_in'] = _mm("d_w_in", [x1b], [dproj], [[(0, 0)]], ident, [gdt], ta=True, o3=True, bsum=True,
                               tm=1024, tn=W['w_in'].shape[2], tk=1024)
    cuts = [0, (9 * D) // 16, (53 * D) // 64, D]
    w_in_rows = [{'w_in': (cuts[i], cuts[i + 1] - cuts[i], i == 2)} for i in range(3)]
    rider = net.exchange(['w_in', 'conv_k'], w_in_rows[0])
    (dy1,) = _mm("d_x1", [dproj], [W['w_in']], [[(0, 0)]], lambda accs, dzr: [ALPHA * dzr + accs[0]], [F32], tb=True,
                 b3=True, tm=1024, tn=1024, tk=0, extras=[(dz2, 'mn', 0)], rider=rider)
    net.done(rider)
    dz1, dz1h, S['ln1_g'], S['ln1_b'] = _ln_bwd("ln1b", dy1, xh1, rs1, P['ln1_g'], 0.5, T, D)
    grad_x = _ffn_bwd("ffn1b", dz1h, dz1, xb, a1, b1, s1, W, FFN1, gdt, G, net, 'all',
                      pre=(lambda: net.exchange(['w_in'], w_in_rows[1]), lambda: net.exchange(['w_in'], w_in_rows[2])))
    return loss, grad_x, G, S


def _coords():
    return lax.axis_index("x"), lax.axis_index("y"), lax.axis_index("c")


def _flip(k, x, y, c):
    return (1 - x if k & 4 else x, 1 - y if k & 2 else y, 1 - c if k & 1 else c)


def _lin(p):
    return 4 * p[0] + 2 * p[1] + p[2]


class _Rider:
    def __init__(self, ins, out_shape, rows=None, fill=None):
        nb = len(ins)
        self.rows = rows or [None] * nb
        fill = fill or [None] * nb
        self.aliases = {nb + i: w for i, w in enumerate(w for w in range(nb) if fill[w] is not None)}
        self.ins = list(ins) + [f for f in fill if f is not None]
        self.out_shape, self.results = list(out_shape), None
        self.scratch = [pltpu.SemaphoreType.DMA((7 * nb,)), pltpu.SemaphoreType.DMA((7 * nb,)),
                        pltpu.SemaphoreType.DMA((nb,))]

    def span(self, w, ref, *slot):
        if self.rows[w] is None:
            return ref.at[slot] if slot else ref
        return ref.at[(*slot, pl.ds(*self.rows[w]))]

    def begin(self, step, n_steps, ins, outs, sems):
        @pl.when(step == 0)
        def _():
            self.start(ins, outs, sems)

        @pl.when(step == n_steps - 1)
        def _():
            self.mid(ins, outs, sems)

    def end(self, step, n_steps, ins, outs, sems):
        @pl.when(step == n_steps - 1)
        def _():
            self.finish(ins, outs, sems)

    def mid(self, ins, outs, sems):
        pass


class _GatherRider(_Rider):
    def __init__(self, blks, rows=None, fill=None):
        super().__init__(blks, [jax.ShapeDtypeStruct((N_DEV,) + b.shape, b.dtype) for b in blks], rows, fill)

    def _copies(self, x_refs, out_refs, sems):
        nb = len(self.out_shape)
        send_sems, recv_sems, local_sems = sems
        x, y, c = _coords()
        me, sibling = (x, y, c), (x, y, 1 - c)
        chips = [_flip(4, x, y, c), _flip(2, x, y, c), _flip(6, x, y, c)]
        own = [self.span(w, x_refs[w]) for w in range(nb)]

        def copy(k, w, block, to, src=None):
            slot = self.span(w, out_refs[w], _lin(block))
            return pltpu.make_async_remote_copy(
                src_ref=slot if src is None else src, dst_ref=slot, send_sem=send_sems.at[k * nb + w],
                recv_sem=recv_sems.at[k * nb + w], device_id=to, device_id_type=MESH)

        mines = [pltpu.make_async_copy(own[w], self.span(w, out_refs[w], _lin(me)), local_sems.at[w])
                 for w in range(nb)]
        first = [copy(0, w, me, sibling, src=own[w]) for w in range(nb)]
        first += [copy(1 + j, w, me, chip, src=own[w]) for w in range(nb) for j, chip in enumerate(chips)]
        landed = [(copy(1 + j, w, chip, me), copy(4 + j, w, chip, sibling)) for w in range(nb) for j, chip in enumerate(chips)]
        from_sibling = [copy(0, w, sibling, me) for w in range(nb)]
        from_sibling += [copy(4 + j, w, (chip[0], chip[1], 1 - c), me) for w in range(nb) for j, chip in enumerate(chips)]
        return mines, first, landed, from_sibling

    def start(self, ins, outs, sems):
        mines, first, _, _ = self._copies(ins, outs, sems)
        for cp in mines + first:
            cp.start()

    def mid(self, ins, outs, sems):
        for arrival, onward in self._copies(ins, outs, sems)[2]:
            arrival.wait_recv()
            onward.start()

    def finish(self, ins, outs, sems):
        mines, first, landed, from_sibling = self._copies(ins, outs, sems)
        for cp in from_sibling:
            cp.wait_recv()
        for cp in first + [onward for _, onward in landed]:
            cp.wait_send()
        for mine in mines:
            mine.wait()


class _ExchangeRider(_Rider):
    def __init__(self, gs, rows=None, fill=None):
        super().__init__(gs, [jax.ShapeDtypeStruct(g.shape, g.dtype) for g in gs], rows, fill)

    def _copies(self, g_refs, out_refs, sems):
        nb = len(self.out_shape)
        send_sems, recv_sems, local_sems = sems
        x, y, c = _coords()
        me = _lin((x, y, c))

        def copy(k, w, landing):
            peer = _flip(k, x, y, c)
            src, dst = (me, _lin(peer)) if landing else (_lin(peer), me)
            return pltpu.make_async_remote_copy(
                src_ref=self.span(w, g_refs[w], src), dst_ref=self.span(w, out_refs[w], dst),
                send_sem=send_sems.at[(k - 1) * nb + w], recv_sem=recv_sems.at[(k - 1) * nb + w],
                device_id=peer, device_id_type=MESH)

        mines = [pltpu.make_async_copy(self.span(w, g_refs[w], me), self.span(w, out_refs[w], me), local_sems.at[w])
                 for w in range(nb)]
        sends = [copy(k, w, False) for w in range(nb) for k in range(1, N_DEV)]
        landings = [copy(k, w, True) for w in range(nb) for k in range(1, N_DEV)]
        return mines, sends, landings

    def start(self, ins, outs, sems):
        mines, sends, _ = self._copies(ins, outs, sems)
        for cp in mines + sends:
            cp.start()

    def finish(self, ins, outs, sems):
        mines, sends, landings = self._copies(ins, outs, sems)
        for cp in landings:
            cp.wait_recv()
        for cp in sends:
            cp.wait_send()
        for mine in mines:
            mine.wait()


def _run_rider(name, rider):
    n_in, n_out = len(rider.ins), len(rider.out_shape)

    def body(*refs):
        ride = (refs[:n_in], refs[n_in:n_in + n_out], refs[n_in + n_out:])
        rider.start(*ride)
        rider.mid(*ride)
        rider.finish(*ride)

    rider.results = pl.pallas_call(
        body, name=name, out_shape=rider.out_shape, in_specs=[ANY] * n_in, out_specs=[ANY] * n_out,
        scratch_shapes=rider.scratch, input_output_aliases=dict(rider.aliases),
        compiler_params=pltpu.CompilerParams(has_side_effects=True),
    )(*rider.ins)
    return rider.results


def _as_matrix(name, g):
    if name == 'w_in':
        return g
    if name in COL_SHARDED:
        return jnp.transpose(g, (1, 0, 2)).reshape(g.shape[1], N_DEV * g.shape[2])
    return g.reshape(N_DEV * g.shape[1], g.shape[2])


def _by_owner(name, g):
    if name == 'w_in':
        return g
    if name in COL_SHARDED:
        return jnp.transpose(g.reshape(g.shape[0], N_DEV, g.shape[1] // N_DEV), (1, 0, 2))
    return g.reshape(N_DEV, g.shape[0] // N_DEV, g.shape[1])


class _Comm:
    def __init__(self, shards):
        self.shards, self.W, self.parts, self.partial, self.sent = shards, {}, {}, {}, {}

    def _ride(self, cls, names, srcs, part, sink):
        part = part or {}
        rider = cls(srcs, rows=[part[n][:2] if n in part else None for n in names],
                    fill=[self.partial.pop((sink, n), None) for n in names])
        rider.names, rider.sink = names, sink
        rider.unfinished = {n for n in names if n in part and not part[n][2]}
        return rider

    def gather(self, names, part=None):
        return self._ride(_GatherRider, names, [self.shards[n] for n in names], part, 'W')

    def exchange(self, names, G, part=None):
        for n in names:
            if n not in self.sent:
                self.sent[n] = _by_owner(n, G[n])
        return self._ride(_ExchangeRider, names, [self.sent[n] for n in names], part, 'parts')

    def collect(self, rider):
        for n, res in zip(rider.names, rider.results):
            if n in rider.unfinished:
                self.partial[(rider.sink, n)] = res
            elif rider.sink == 'W':
                self.W[n] = _as_matrix(n, res)
            else:
                self.parts[n] = res


def _adamw_math(p_ref, w_ref, m_ref, v_ref, g_ref, d_ref, nm_ref, nv_ref):
    c1 = 1.0 - ADAM_B1 ** ADAM_STEP
    c2 = 1.0 - ADAM_B2 ** ADAM_STEP
    g = p_ref[0].astype(F32)
    for s in range(1, p_ref.shape[0]):
        g = g + p_ref[s].astype(F32)
    nm = ADAM_B1 * m_ref[...] + (1.0 - ADAM_B1) * g
    nv = ADAM_B2 * v_ref[...] + (1.0 - ADAM_B2) * (g * g)
    g_ref[...] = g
    nm_ref[...] = nm
    nv_ref[...] = nv
    d_ref[...] = -ADAM_LR * ((nm / c1) / (jnp.sqrt(nv / c2) + ADAM_EPS) + ADAM_WD * w_ref[...])


def _adamw_vectors(parts, ws, ms, vs, loss_parts):
    k = len(ws)

    def body(*refs):
        for i in range(k):
            _adamw_math(refs[i], refs[k + i], refs[2 * k + i], refs[3 * k + i], *refs[4 * k + 1 + 4 * i:4 * k + 5 + 4 * i])
        lp, lo = refs[4 * k], refs[8 * k + 1]
        lo[...] = functools.reduce(jnp.add, [lp[s] for s in range(lp.shape[0])])

    return pl.pallas_call(
        body, name="adamw_vectors",
        out_shape=[jax.ShapeDtypeStruct(w.shape, F32) for w in ws for _ in range(4)] + [jax.ShapeDtypeStruct((1, 128), F32)],
        compiler_params=_params(),
    )(*parts, *ws, *ms, *vs, loss_parts)


def _adamw(name, parts, w, m, v, tb):
    n, R, Wd = parts.shape
    assert R % tb == 0
    body = functools.partial(_adamw_math)

    row = pl.BlockSpec((tb, Wd), lambda i: (i, 0))
    return pl.pallas_call(
        body, name=name, grid=(R // tb,),
        in_specs=[pl.BlockSpec((n, tb, Wd), lambda i: (0, i, 0)), row, row, row],
        out_specs=[row, row, row, row], out_shape=[jax.ShapeDtypeStruct((R, Wd), F32)] * 4,
        compiler_params=_params(("arbitrary",)),
    )(parts, w, m, v)


def _row_tile(R, unit, cap):
    best = unit
    for t in range(unit, cap + 1, unit):
        if R % t == 0:
            best = t
    return best


def kernel(x, ffn1_w_gate, ffn1_w_up, ffn1_w_down, ln1_g, ln1_b, w_in, b_in, ret_gn_g, conv_k, conv_b, conv_ln_g, conv_ln_b, w_ret_o, w_conv_o, w_out, ln2_g, ln2_b, ffn2_w_gate, ffn2_w_up, ffn2_w_down, ln3_g, ln3_b, loss_target, m_ffn1_w_gate, m_ffn1_w_up, m_ffn1_w_down, m_ln1_g, m_ln1_b, m_w_in, m_b_in, m_ret_gn_g, m_conv_k, m_conv_b, m_conv_ln_g, m_conv_ln_b, m_w_ret_o, m_w_conv_o, m_w_out, m_ln2_g, m_ln2_b, m_ffn2_w_gate, m_ffn2_w_up, m_ffn2_w_down, m_ln3_g, m_ln3_b, v_ffn1_w_gate, v_ffn1_w_up, v_ffn1_w_down, v_ln1_g, v_ln1_b, v_w_in, v_b_in, v_ret_gn_g, v_conv_k, v_conv_b, v_conv_ln_g, v_conv_ln_b, v_w_ret_o, v_w_conv_o, v_w_out, v_ln2_g, v_ln2_b, v_ffn2_w_gate, v_ffn2_w_up, v_ffn2_w_down, v_ln3_g, v_ln3_b):
    given = dict(locals())
    wts = {n: given[n] for n in WEIGHTS}
    mom = {n: given['m_' + n] for n in WEIGHTS}
    var = {n: given['v_' + n] for n in WEIGHTS}

    def shard2d(a):
        return a.reshape(a.shape[-3] * a.shape[-2] if a.ndim == 4 else a.shape[-2], a.shape[-1])

    comm = _Comm({n: shard2d(wts[n]).astype(BF16) for n in BIG})
    P = {n: wts[n].reshape(1, -1) for n in SMALL}
    loss, grad_x, _, S = _local_step(x[0], loss_target[0], None, P, comm=comm)

    parts = comm.parts
    res = {}
    for n in BIG:
        rows, cols = parts[n].shape[1:]
        tb = rows if rows % 16 else _row_tile(rows, 16, max(16, (256 * 1024) // cols))
        res[n] = _adamw("adamw_" + n, parts[n], shard2d(wts[n]), shard2d(mom[n]), shard2d(var[n]), tb)

    vec_parts = _run_rider("gather_vector_grads", _GatherRider([S[n] for n in SMALL] + [loss]))
    vec = _adamw_vectors(vec_parts[:-1], [P[n] for n in SMALL], [mom[n].reshape(1, -1) for n in SMALL],
                         [var[n].reshape(1, -1) for n in SMALL], vec_parts[-1])
    for i, n in enumerate(SMALL):
        res[n] = vec[4 * i:4 * i + 4]

    outs = [vec[-1][0, 0], grad_x[None]]
    for k in range(4):
        for n in WEIGHTS:
            outs.append(res[n][k].reshape(wts[n].shape))
    return tuple(outs)
```

```python
import functools
import math

import jax
import jax.numpy as jnp
from jax import lax
from jax.experimental import pallas as pl
from jax.experimental.pallas import tpu as pltpu

F32 = jnp.float32
BF16 = jnp.bfloat16

N_DEV = 8
LN_EPS = 1e-5
ALPHA = 2.0 ** 0.25
RET_DK = 128
RET_DV = 256
RET_CHUNK = 256
ROPE_BASE = 10000.0
CONV_WIDTH = 31
HALO = 32
ADAM_LR, ADAM_B1, ADAM_B2, ADAM_EPS, ADAM_WD, ADAM_STEP = 0.001, 0.9, 0.999, 1e-08, 0.01, 10
VMEM_LIMIT = 52 * 1024 * 1024
MESH = pl.DeviceIdType.MESH
ANY = pl.BlockSpec(memory_space=pl.ANY)

BIG = ['ffn1_w_gate', 'ffn1_w_up', 'ffn1_w_down', 'w_in', 'w_ret_o', 'w_conv_o', 'w_out',
       'ffn2_w_gate', 'ffn2_w_up', 'ffn2_w_down', 'conv_k']
COL_SHARDED = {'ffn1_w_gate', 'ffn1_w_up', 'w_in', 'ffn2_w_gate', 'ffn2_w_up', 'conv_k'}
SMALL = ['ln1_g', 'ln1_b', 'b_in', 'ret_gn_g', 'conv_b', 'conv_ln_g', 'conv_ln_b', 'ln2_g', 'ln2_b', 'ln3_g', 'ln3_b']
WEIGHTS = ['ffn1_w_gate', 'ffn1_w_up', 'ffn1_w_down', 'ln1_g', 'ln1_b', 'w_in', 'b_in', 'ret_gn_g', 'conv_k', 'conv_b',
           'conv_ln_g', 'conv_ln_b', 'w_ret_o', 'w_conv_o', 'w_out', 'ln2_g', 'ln2_b', 'ffn2_w_gate', 'ffn2_w_up',
           'ffn2_w_down', 'ln3_g', 'ln3_b']


def _params(sem=None):
    return pltpu.CompilerParams(dimension_semantics=sem, vmem_limit_bytes=VMEM_LIMIT)


def _sigmoid(x):
    return jax.nn.sigmoid(x)


def _dsilu(x, sg):
    return sg * (1.0 + x * (1.0 - sg))


def _fit(dim, want):
    if dim <= want:
        return dim
    return max(t for t in range(128, want + 1, 128) if dim % t == 0)


def _dot(a, b, ta=False, tb=False):
    dn = (((0,) if ta else (1,), (1,) if tb else (0,)), ((), ()))
    return lax.dot_general(a, b, dn, preferred_element_type=F32)


def _mm(name, As, Bs, prods, epi, out_dtypes, *, ta=False, tb=False, tm, tn, tk, extras=(), i_outer=True,
        b3=False, o3=False, rider=None, bsum=False, epi_rows=0):
    a0, b0 = As[0], Bs[0]
    M, K = (a0.shape[1], a0.shape[0]) if ta else a0.shape
    if b3:
        S, rows, cs = b0.shape
        N = rows if tb else S * cs
        assert K == (S * cs if tb else rows)
        tn, tk = (tn, cs) if tb else (cs, tk)
    else:
        N = b0.shape[0] if tb else b0.shape[1]
    tm, tn, tk = _fit(M, tm), _fit(N, tn), _fit(K, tk)
    assert M % tm == 0 and N % tn == 0 and K % tk == 0, (name, M, N, K, tm, tn, tk)
    gi, gj, gk = M // tm, N // tn, K // tk
    grid = (gi, gj, gk) if i_outer else (gj, gi, gk)

    def ij(g0, g1):
        return (g0, g1) if i_outer else (g1, g0)

    def amap(g0, g1, k):
        i, _ = ij(g0, g1)
        return (k, i) if ta else (i, k)

    def bmap(g0, g1, k):
        _, j = ij(g0, g1)
        return (j, k) if tb else (k, j)

    def bmap3(g0, g1, k):
        _, j = ij(g0, g1)
        return (k, j, 0) if tb else (j, k, 0)

    in_specs = [pl.BlockSpec((tk, tm) if ta else (tm, tk), amap) for _ in As]
    if b3:
        in_specs += [pl.BlockSpec((None, tn, tk) if tb else (None, tk, tn), bmap3) for _ in Bs]
    else:
        in_specs += [pl.BlockSpec((tn, tk) if tb else (tk, tn), bmap) for _ in Bs]
    args = list(As) + list(Bs)
    for arr, kind, coloff in extras:
        assert coloff % tn == 0
        off = coloff // tn
        if kind == 'mn':
            in_specs.append(pl.BlockSpec((tm, tn), lambda g0, g1, k, off=off: (ij(g0, g1)[0], ij(g0, g1)[1] + off)))
        else:
            in_specs.append(pl.BlockSpec((1, tn), lambda g0, g1, k, off=off: (0, ij(g0, g1)[1] + off)))
        args.append(arr)
    if o3:
        out_shape = [jax.ShapeDtypeStruct((gj, M, tn), dt) for dt in out_dtypes]
        out_specs = [pl.BlockSpec((None, tm, tn), lambda g0, g1, k: (ij(g0, g1)[1], ij(g0, g1)[0], 0))
                     for _ in out_dtypes]
    else:
        out_shape = [jax.ShapeDtypeStruct((M, N), dt) for dt in out_dtypes]
        out_specs = [pl.BlockSpec((tm, tn), lambda g0, g1, k: ij(g0, g1)) for _ in out_dtypes]
    if bsum:
        assert gi == 1 and not tb and not b3
        out_shape.append(jax.ShapeDtypeStruct((1, N), F32))
        out_specs.append(pl.BlockSpec((1, tn), lambda g0, g1, k: (0, ij(g0, g1)[1])))
    n_a, n_b, n_e, n_o = len(As), len(Bs), len(extras), len(out_shape)
    n_p = len(prods) if gk > 1 else 0
    scratch = [pltpu.VMEM((tm, tn), F32) for _ in range(n_p)]
    if rider is not None:
        in_specs, out_specs = in_specs + [ANY] * len(rider.ins), out_specs + [ANY] * len(rider.out_shape)
        args, out_shape, scratch = args + rider.ins, out_shape + rider.out_shape, scratch + rider.scratch
    n_in, n_out = len(args), len(out_shape)

    def body(*refs):
        a_refs = refs[:n_a]
        b_refs = refs[n_a:n_a + n_b]
        e_refs = refs[n_a + n_b:n_a + n_b + n_e]
        o_refs = refs[n_in:n_in + n_o]
        acc_refs = refs[n_in + n_out:n_in + n_out + n_p]
        k = pl.program_id(2)
        if rider is not None:
            step = (pl.program_id(0) * grid[1] + pl.program_id(1)) * gk + k
            ride = (step, grid[0] * grid[1] * gk, refs[n_a + n_b + n_e:n_in], refs[n_in + n_o:n_in + n_out],
                    refs[n_in + n_out + n_p:])
            rider.begin(*ride)

        def finish(accs, rows=slice(None)):
            ex = [(e[rows, :] if kind == 'mn' else e[...]).astype(F32) for e, (_, kind, _) in zip(e_refs, extras)]
            for o, r in zip(o_refs, epi(accs, *ex)):
                o[rows, :] = r.astype(o.dtype)

        if bsum:
            @pl.when(k == 0)
            def _():
                o_refs[-1][...] = jnp.zeros_like(o_refs[-1])

            o_refs[-1][...] += _colsum(b_refs[0][...].astype(F32))

        if gk == 1:
            sub = tm if ta or not epi_rows else _fit(tm, epi_rows)
            for r0 in range(0, tm, sub):
                rows = slice(None) if ta else slice(r0, r0 + sub)
                finish([functools.reduce(jnp.add, [_dot(a_refs[ai][...] if ta else a_refs[ai][rows, :],
                                                        b_refs[bi][...], ta, tb) for ai, bi in terms])
                        for terms in prods], rows)
        else:
            @pl.when(k == 0)
            def _():
                for acc in acc_refs:
                    acc[...] = jnp.zeros_like(acc)

            for p, terms in enumerate(prods):
                for ai, bi in terms:
                    acc_refs[p][...] += _dot(a_refs[ai][...], b_refs[bi][...], ta, tb)

            @pl.when(k == gk - 1)
            def _():
                finish([acc[...] for acc in acc_refs])

        if rider is not None:
            rider.end(*ride)

    aliases = {} if rider is None else {n_a + n_b + n_e + p: n_o + o for p, o in rider.aliases.items()}
    res = pl.pallas_call(
        body, name=name, grid=grid, in_specs=in_specs, out_specs=out_specs, out_shape=out_shape,
        scratch_shapes=scratch, input_output_aliases=aliases,
        compiler_params=_params(("arbitrary", "arbitrary", "arbitrary")),
    )(*args)
    if rider is not None:
        rider.results = res[n_o:]
    return res[:n_o]


def _rows(name, fn, ins, outs, *, T, tb, rider=None):
    tb = min(tb, T)
    assert T % tb == 0
    in_specs, args = [], []
    for arr, kind, width, cb in ins:
        if kind == 'r':
            in_specs.append(pl.BlockSpec((tb, width), lambda i, _, cb=cb: (i, cb)))
        else:
            in_specs.append(pl.BlockSpec((1, width), lambda i, _, cb=cb: (0, cb)))
        args.append(arr)
    out_shape, out_specs = [], []
    for kind, width, dtype in outs:
        if kind == 'r':
            out_shape.append(jax.ShapeDtypeStruct((T, width), dtype))
            out_specs.append(pl.BlockSpec((tb, width), lambda i, _: (i, 0)))
        elif kind == 'c':
            out_shape.append(jax.ShapeDtypeStruct((T, 1), dtype))
            out_specs.append(pl.BlockSpec((tb, 1), lambda i, _: (i, 0)))
        else:
            out_shape.append(jax.ShapeDtypeStruct((1, width), F32))
            out_specs.append(pl.BlockSpec((1, width), lambda i, _: (0, 0)))
    n_in = len(ins)

    def body(*refs):
        i = pl.program_id(0)
        vals = fn(*[r[...] for r in refs[:n_in]])
        for (kind, _, _), o, v in zip(outs, refs[n_in:], vals):
            if kind == 'a':
                @pl.when(i == 0)
                def _(o=o):
                    o[...] = jnp.zeros_like(o)

                o[...] += v
            else:
                o[...] = v.astype(o.dtype)

    return _hosted_call(body, rider, name=name, grid=(T // tb, 1), in_specs=in_specs, out_specs=out_specs,
                        out_shape=out_shape, scratch=[], args=args)


def _colsum(v):
    return jnp.sum(v, axis=0, keepdims=True)


def _ln_stats(z):
    mu = jnp.mean(z, axis=-1, keepdims=True)
    d = z - mu
    var = jnp.mean(d * d, axis=-1, keepdims=True)
    rstd = lax.rsqrt(var + LN_EPS)
    return d * rstd, rstd


def _ln_bwd_math(dy, xhat, rstd, g):
    dxh = dy * g
    m1 = jnp.mean(dxh, axis=-1, keepdims=True)
    m2 = jnp.mean(dxh * xhat, axis=-1, keepdims=True)
    return rstd * (dxh - m1 - xhat * m2)


def _ln_fwd(name, z, g, b, T, D, rider=None):
    def fn(z, g, b):
        xhat, rstd = _ln_stats(z)
        y = xhat * g + b
        return [y, y, xhat, rstd]

    return _rows(name, fn, [(z, 'r', D, 0), (g, 'v', D, 0), (b, 'v', D, 0)],
                 [('r', D, F32), ('r', D, BF16), ('r', D, F32), ('c', 1, F32)], T=T, tb=512, rider=rider)


def _ln_bwd(name, dy, xhat, rstd, g, scale, T, D, rider=None):
    def fn(dy, xhat, rstd, g):
        dz = _ln_bwd_math(dy, xhat, rstd, g)
        return [dz, dz * scale, _colsum(dy * xhat), _colsum(dy)]

    return _rows(name, fn, [(dy, 'r', D, 0), (xhat, 'r', D, 0), (rstd, 'r', 1, 0), (g, 'v', D, 0)],
                 [('r', D, F32), ('r', D, BF16), ('a', D, F32), ('a', D, F32)], T=T, tb=512, rider=rider)


def _ln_loss_bwd(name, z, g, b, tgt, T, D):
    def fn(z, g, b, tgt):
        xhat, rstd = _ln_stats(z)
        err = xhat * g + b - tgt
        row_loss = 0.5 * jnp.mean(err * err, axis=-1, keepdims=True)
        loss = jnp.broadcast_to(jnp.sum(row_loss, axis=0, keepdims=True), (1, 128))
        dy = err * (1.0 / D)
        dz = _ln_bwd_math(dy, xhat, rstd, g)
        return [dz, dz * 0.5, _colsum(dy * xhat), _colsum(dy), loss]

    return _rows(name, fn, [(z, 'r', D, 0), (g, 'v', D, 0), (b, 'v', D, 0), (tgt, 'r', D, 0)],
                 [('r', D, F32), ('r', D, BF16), ('a', D, F32), ('a', D, F32), ('a', 128, F32)], T=T, tb=512)


class _Net:
    def __init__(self, comm, G):
        self.comm, self.G = comm, G

    def gather(self, names, part=None):
        return self.comm.gather(names, part) if self.comm else None

    def exchange(self, names, part=None):
        return self.comm.exchange(names, self.G, part) if self.comm else None

    def done(self, rider):
        if rider is not None:
            self.comm.collect(rider)


def _ffn_fwd(tag, xb, x, W, names, net, rider=None, rider_down=None):
    def epi_gu(accs):
        a, b = accs
        return [a, b, a * _sigmoid(a) * b]

    ng, nu, nd = names
    a, b, s = _mm(tag + "_gate_up", [xb], [W[ng], W[nu]], [[(0, 0)], [(0, 1)]], epi_gu, [BF16, BF16, BF16],
                  tm=1024, tn=1408, tk=1024, rider=rider, epi_rows=256)
    net.done(rider)

    def epi_down(accs, xres):
        return [ALPHA * xres + 0.5 * accs[0]]

    rider_down = rider_down() if rider_down else None
    (z,) = _mm(tag + "_down", [s], [W[nd]], [[(0, 0)]], epi_down, [F32], tm=1024, tn=1024, tk=1408,
               extras=[(x, 'mn', 0)], rider=rider_down)
    net.done(rider_down)
    return a, b, s, z


def _ffn_bwd(tag, dzh, dz, xb, a, b, s, W, names, gdt, G, net, ride, pre=(None, None)):
    ng, nu, nd = names

    def epi_ds(accs, a, b):
        ds = accs[0]
        sg = _sigmoid(a)
        return [ds * b * _dsilu(a, sg), ds * a * sg]

    rider = pre[0]() if pre[0] else None
    da, db = _mm(tag + "_ds", [dzh], [W[nd]], [[(0, 0)]], epi_ds, [BF16, BF16], tb=True, tm=1024, tn=1408, tk=1024, epi_rows=256,
                 extras=[(a, 'mn', 0), (b, 'mn', 0)], rider=rider)
    net.done(rider)
    ident = lambda accs: accs
    rider = pre[1]() if pre[1] else None
    (G[nd],) = _mm(tag + "_dwd", [s], [dzh], [[(0, 0)]], ident, [gdt], ta=True, tm=1408, tn=1024, tk=1024,
                   rider=rider)
    net.done(rider)
    if ride == 'all':
        rider = net.exchange([nd])
        (G[ng],) = _mm(tag + "_dwg", [xb], [da], [[(0, 0)]], ident, [gdt], ta=True, tm=1024, tn=1408, tk=1024,
                       rider=rider)
        net.done(rider)
        rider = net.exchange([ng])
        (G[nu],) = _mm(tag + "_dwu", [xb], [db], [[(0, 0)]], ident, [gdt], ta=True, tm=1024, tn=1408, tk=1024,
                       rider=rider)
        net.done(rider)
    else:
        G[ng], G[nu] = _mm(tag + "_dwgu", [xb], [da, db], [[(0, 0)], [(0, 1)]], ident, [gdt, gdt], ta=True,
                           tm=1024, tn=1408, tk=1024)

    def epi_dx(accs, dzres):
        return [ALPHA * dzres + accs[0]]

    rider = net.exchange([nu] if ride == 'all' else [nd]) if ride else None
    (dx,) = _mm(tag + "_dx", [da, db], [W[ng], W[nu]], [[(0, 0), (1, 1)]], epi_dx, [F32], tb=True,
                tm=1024, tn=1024, tk=1408, extras=[(dz, 'mn', 0)], rider=rider)
    net.done(rider)
    return dx


def _ret_tables(H, T):
    C = RET_CHUNK
    log_g = jnp.log(1.0 - jnp.exp2(-5.0 - jnp.arange(H, dtype=F32)))
    idx = jnp.arange(C, dtype=F32)
    diff = idx[:, None] - idx[None, :]
    dm = jnp.where(diff[None] >= 0, jnp.exp(jnp.maximum(diff, 0.0)[None] * log_g[:, None, None]), 0.0)
    xi = jnp.exp((idx[None, :] + 1.0) * log_g[:, None])[:, :, None]
    zeta = jnp.exp((C - 1.0 - idx)[None, :] * log_g[:, None])[:, :, None]
    gc = jnp.broadcast_to(jnp.exp(C * log_g)[:, None, None], (H, 1, RET_DV))
    half = RET_DK // 2
    freqs = ROPE_BASE ** (-jnp.arange(half, dtype=F32) / half)
    ang = jnp.arange(T, dtype=F32)[:, None] * freqs[None, :]
    cos, sin = jnp.cos(ang), jnp.sin(ang)
    cosf = jnp.concatenate([cos, cos], axis=1)
    sins = jnp.concatenate([-sin, sin], axis=1)
    return dm, xi, zeta, gc, cosf, sins


def _rot(x, cosf, sins):
    return x * cosf + pltpu.roll(x, RET_DK // 2, 1) * sins


def _rot_bwd(dy, cosf, sins):
    return dy * cosf + pltpu.roll(dy * sins, RET_DK // 2, 1)


RET_HB = 8


def _ret_specs(H, HB, rev, NC):
    C, G = RET_CHUNK, H // HB
    nn = (lambda n: NC - 1 - n) if rev else (lambda n: n)
    return [
        pl.BlockSpec((C, HB * RET_DK), lambda h, n: (nn(n), h)),
        pl.BlockSpec((C, HB * RET_DK), lambda h, n: (nn(n), G + h)),
        pl.BlockSpec((C, HB * RET_DV), lambda h, n: (nn(n), G + h)),
        pl.BlockSpec((C, HB * RET_DV), lambda h, n: (nn(n), 2 * G + h)),
        pl.BlockSpec((C, RET_DK), lambda h, n: (nn(n), 0)),
        pl.BlockSpec((C, RET_DK), lambda h, n: (nn(n), 0)),
        pl.BlockSpec((1, HB * RET_DV), lambda h, n: (0, h)),
        pl.BlockSpec((HB, C, C), lambda h, n: (h, 0, 0)),
        pl.BlockSpec((HB, C, 1), lambda h, n: (h, 0, 0)),
        pl.BlockSpec((HB, C, 1), lambda h, n: (h, 0, 0)),
        pl.BlockSpec((HB, 1, RET_DV), lambda h, n: (h, 0, 0)),
    ]


def _ret_fwd(proj, gn_g, tabs, H, T, rider=None):
    C, NC = RET_CHUNK, T // RET_CHUNK
    HB = min(RET_HB, H)
    dm, xi, zeta, gc, cosf, sins = tabs
    scale = RET_DK ** -0.5

    def body(q_ref, k_ref, v_ref, g_ref, cos_ref, sin_ref, gn_ref, dm_ref, xi_ref, zt_ref, gc_ref,
             r_ref, ri_ref, st_ref, state):
        @pl.when(pl.program_id(1) == 0)
        def _():
            state[...] = jnp.zeros_like(state)

        cs, sn = cos_ref[...], sin_ref[...]
        hs = range(HB)
        qk = [slice(h * RET_DK, (h + 1) * RET_DK) for h in hs]
        vv = [slice(h * RET_DV, (h + 1) * RET_DV) for h in hs]
        kr = [_rot(k_ref[:, qk[h]].astype(F32), cs, sn) for h in hs]
        qb = [(_rot(q_ref[:, qk[h]].astype(F32), cs, sn) * scale).astype(BF16) for h in hs]
        kb = [kr[h].astype(BF16) for h in hs]
        kzb = [(kr[h] * zt_ref[h]).astype(BF16) for h in hs]
        vb = [v_ref[:, vv[h]].astype(BF16) for h in hs]
        st = [state[h] for h in hs]
        stb = [st[h].astype(BF16) for h in hs]
        sb = [(_dot(qb[h], kb[h], tb=True) * dm_ref[h]).astype(BF16) for h in hs]
        cross = [_dot(qb[h], stb[h]) for h in hs]
        kv = [_dot(kzb[h], vb[h], ta=True) for h in hs]
        intra = [_dot(sb[h], vb[h]) for h in hs]
        for h in hs:
            st_ref[h] = stb[h]
            state[h] = gc_ref[h] * st[h] + kv[h]
        for h in hs:
            r = intra[h] + cross[h] * xi_ref[h]
            rhat, _ = _ln_stats(r)
            g = g_ref[:, vv[h]].astype(F32)
            r_ref[:, vv[h]] = r
            ri_ref[:, vv[h]] = (g * _sigmoid(g) * (rhat * gn_ref[:, vv[h]])).astype(BF16)

    VW = H * RET_DV
    return _hosted_call(
        body, rider, name="ret_fwd", grid=(H // HB, NC), in_specs=_ret_specs(H, HB, False, NC),
        out_specs=[pl.BlockSpec((C, HB * RET_DV), lambda h, n: (n, h)),
                   pl.BlockSpec((C, HB * RET_DV), lambda h, n: (n, h)),
                   pl.BlockSpec((HB, None, RET_DK, RET_DV), lambda h, n: (h, n, 0, 0))],
        out_shape=[jax.ShapeDtypeStruct((T, VW), F32), jax.ShapeDtypeStruct((T, VW), BF16),
                   jax.ShapeDtypeStruct((H, NC, RET_DK, RET_DV), BF16)],
        scratch=[pltpu.VMEM((HB, RET_DK, RET_DV), F32)],
        args=[proj, proj, proj, proj, cosf, sins, gn_g, dm, xi, zeta, gc])


def _hosted_call(body, rider, *, name, grid, in_specs, out_specs, out_shape, scratch, args):
    n_in, n_out, n_scr = len(args), len(out_shape), len(scratch)
    if rider is None:
        hosted = body
    else:
        n_ri, n_ro = len(rider.ins), len(rider.out_shape)
        in_specs, out_specs = in_specs + [ANY] * n_ri, out_specs + [ANY] * n_ro
        args, out_shape, scratch = args + rider.ins, out_shape + rider.out_shape, scratch + rider.scratch

        def hosted(*refs):
            o0, s0 = n_in + n_ri, n_in + n_ri + n_out + n_ro
            step = pl.program_id(0) * grid[1] + pl.program_id(1)
            ride = (step, grid[0] * grid[1], refs[n_in:o0], refs[o0 + n_out:s0], refs[s0 + n_scr:])
            rider.begin(*ride)
            body(*refs[:n_in], *refs[o0:o0 + n_out], *refs[s0:s0 + n_scr])
            rider.end(*ride)

    aliases = {} if rider is None else {n_in + p: n_out + o for p, o in rider.aliases.items()}
    res = pl.pallas_call(
        hosted, name=name, grid=grid, in_specs=in_specs, out_specs=out_specs, out_shape=out_shape,
        scratch_shapes=scratch, input_output_aliases=aliases, compiler_params=_params(("arbitrary", "arbitrary")),
    )(*args)
    if rider is not None:
        rider.results = res[n_out:]
    return res[:n_out]


def _ret_bwd(dri, r, states, proj, gn_g, tabs, H, T, in_w, rider=None):
    C, NC = RET_CHUNK, T // RET_CHUNK
    HB = min(RET_HB, H)
    dm, xi, zeta, gc, cosf, sins = tabs
    scale = RET_DK ** -0.5

    def body(q_ref, k_ref, v_ref, g_ref, cos_ref, sin_ref, gn_ref, dm_ref, xi_ref, zt_ref, gc_ref,
             dri_ref, r_ref, st_ref, dp_ref, dgn_ref, dstate):
        @pl.when(pl.program_id(1) == 0)
        def _():
            dstate[...] = jnp.zeros_like(dstate)
            dgn_ref[...] = jnp.zeros_like(dgn_ref)

        cs, sn = cos_ref[...], sin_ref[...]
        hs = range(HB)
        qk = [slice(h * RET_DK, (h + 1) * RET_DK) for h in hs]
        vv = [slice(h * RET_DV, (h + 1) * RET_DV) for h in hs]
        qr = [_rot(q_ref[:, qk[h]].astype(F32), cs, sn) * scale for h in hs]
        kr = [_rot(k_ref[:, qk[h]].astype(F32), cs, sn) for h in hs]
        qb = [qr[h].astype(BF16) for h in hs]
        kb = [kr[h].astype(BF16) for h in hs]
        vb = [v_ref[:, vv[h]].astype(BF16) for h in hs]
        qxb = [(qr[h] * xi_ref[h]).astype(BF16) for h in hs]
        kzb = [(kr[h] * zt_ref[h]).astype(BF16) for h in hs]
        drb = []
        for h in hs:
            rhat, rstd = _ln_stats(r_ref[:, vv[h]])
            g, gn, dpre = g_ref[:, vv[h]].astype(F32), gn_ref[:, vv[h]], dri_ref[:, vv[h]]
            sg = _sigmoid(g)
            dp_ref[:, 2 * QW + VW + h * RET_DV:2 * QW + VW + (h + 1) * RET_DV] = (
                dpre * (rhat * gn) * _dsilu(g, sg)).astype(BF16)
            drn = dpre * (g * sg)
            dgn_ref[:, vv[h]] += _colsum(drn * rhat)
            drb.append(_ln_bwd_math(drn, rhat, rstd, gn).astype(BF16))
        ds1 = [dstate[h] for h in hs]
        ds1b = [ds1[h].astype(BF16) for h in hs]
        sb = [(_dot(qb[h], kb[h], tb=True) * dm_ref[h]).astype(BF16) for h in hs]
        dsb = [(_dot(drb[h], vb[h], tb=True) * dm_ref[h]).astype(BF16) for h in hs]
        dq_x = [_dot(drb[h], st_ref[h], tb=True) for h in hs]
        dk_x = [_dot(vb[h], ds1b[h], tb=True) for h in hs]
        dv_x = [_dot(kzb[h], ds1b[h]) for h in hs]
        dst = [_dot(qxb[h], drb[h], ta=True) for h in hs]
        for h in hs:
            dstate[h] = gc_ref[h] * ds1[h] + dst[h]
        dv_i = [_dot(sb[h], drb[h], ta=True) for h in hs]
        dq_i = [_dot(dsb[h], kb[h]) for h in hs]
        dk_i = [_dot(dsb[h], qb[h], ta=True) for h in hs]
        for h in hs:
            dp_ref[:, 2 * QW + h * RET_DV:2 * QW + (h + 1) * RET_DV] = (dv_i[h] + dv_x[h]).astype(BF16)
            dq = dq_i[h] + dq_x[h] * xi_ref[h]
            dk = dk_i[h] + dk_x[h] * zt_ref[h]
            dp_ref[:, qk[h]] = _rot_bwd(dq * scale, cs, sn).astype(BF16)
            dp_ref[:, QW + h * RET_DK:QW + (h + 1) * RET_DK] = _rot_bwd(dk, cs, sn).astype(BF16)

    VW, QW = H * RET_DV, H * RET_DK
    rv = lambda n: NC - 1 - n
    in_specs = _ret_specs(H, HB, True, NC) + [
        pl.BlockSpec((C, HB * RET_DV), lambda h, n: (rv(n), h)),
        pl.BlockSpec((C, HB * RET_DV), lambda h, n: (rv(n), h)),
        pl.BlockSpec((HB, None, RET_DK, RET_DV), lambda h, n: (h, rv(n), 0, 0)),
    ]
    assert HB == H
    return _hosted_call(
        body, rider, name="ret_bwd", grid=(1, NC), in_specs=in_specs,
        out_specs=[pl.BlockSpec((C, 2 * QW + 2 * VW), lambda h, n: (rv(n), 0)),
                   pl.BlockSpec((1, VW), lambda h, n: (0, 0))],
        out_shape=[jax.ShapeDtypeStruct((T, in_w), BF16), jax.ShapeDtypeStruct((1, VW), F32)],
        scratch=[pltpu.VMEM((HB, RET_DK, RET_DV), F32)],
        args=[proj, proj, proj, proj, cosf, sins, gn_g, dm, xi, zeta, gc, dri, r, states])


CONV_CW = 128
CONV_TB = 512


SUBLANES = 8


def _shift_copies(win, shifted, tb):
    n = tb + HALO - SUBLANES
    for s in range(1, SUBLANES):
        shifted[s - 1] = win[pl.ds(s, n), :]


def _tap(win, shifted, off, tb):
    s = off % SUBLANES
    if s == 0:
        return win[pl.ds(off, tb), :]
    return shifted[s - 1, pl.ds(off - s, tb), :]


def _conv_fwd(proj, kpad, bias, off_a, CC, T, rider=None):
    tb, cw = min(CONV_TB, T), CONV_CW
    hb = tb // HALO
    ca, cb = off_a // cw, (off_a + CC) // cw

    def body(a_ref, b_ref, ap_ref, bp_ref, k_ref, bias_ref, u1_ref, win, shifted):
        i = pl.program_id(0)
        keep = (i > 0).astype(F32)
        win[0:HALO, :] = ap_ref[...].astype(F32) * _sigmoid(bp_ref[...].astype(F32)) * keep
        win[HALO:, :] = a_ref[...].astype(F32) * _sigmoid(b_ref[...].astype(F32))
        _shift_copies(win, shifted, tb)
        acc = jnp.broadcast_to(bias_ref[...], (tb, cw))
        for w in range(CONV_WIDTH):
            acc = acc + k_ref[w:w + 1, :] * _tap(win, shifted, HALO - (CONV_WIDTH - 1) + w, tb)
        u1_ref[...] = acc

    prev = lambda i: jnp.maximum(i * hb - 1, 0)
    (u1,) = _hosted_call(
        body, rider, name="conv_fwd", grid=(T // tb, CC // cw),
        in_specs=[pl.BlockSpec((tb, cw), lambda i, c: (i, ca + c)),
                  pl.BlockSpec((tb, cw), lambda i, c: (i, cb + c)),
                  pl.BlockSpec((HALO, cw), lambda i, c: (prev(i), ca + c)),
                  pl.BlockSpec((HALO, cw), lambda i, c: (prev(i), cb + c)),
                  pl.BlockSpec((HALO, cw), lambda i, c: (0, c)),
                  pl.BlockSpec((1, cw), lambda i, c: (0, c))],
        out_specs=[pl.BlockSpec((tb, cw), lambda i, c: (i, c))],
        out_shape=[jax.ShapeDtypeStruct((T, CC), F32)],
        scratch=[pltpu.VMEM((tb + HALO, cw), F32), pltpu.VMEM((SUBLANES - 1, tb + HALO - SUBLANES, cw), F32)],
        args=[proj, proj, proj, proj, kpad, bias])
    return u1


def _conv_bwd(du1, proj, kpad, off_a, CC, T, rider=None):
    tb, cw = min(CONV_TB, T), CONV_CW
    hb = tb // HALO
    nt = T // tb
    ca, cb = off_a // cw, (off_a + CC) // cw

    def body(d_ref, dn_ref, a_ref, b_ref, ap_ref, bp_ref, k_ref, da_ref, db_ref, dk_ref, winu, wind, shu, shd):
        i = pl.program_id(1)
        a, b = a_ref[...].astype(F32), b_ref[...].astype(F32)
        sgb = _sigmoid(b)
        winu[0:HALO, :] = ap_ref[...].astype(F32) * _sigmoid(bp_ref[...].astype(F32)) * (i > 0).astype(F32)
        winu[HALO:, :] = a * sgb
        d = d_ref[...]
        wind[0:tb, :] = d
        wind[tb:, :] = dn_ref[...] * (i < nt - 1).astype(F32)

        @pl.when(i == 0)
        def _():
            dk_ref[...] = jnp.zeros_like(dk_ref)

        _shift_copies(winu, shu, tb)
        _shift_copies(wind, shd, tb)
        du0 = jnp.zeros((tb, cw), F32)
        for w in range(CONV_WIDTH):
            du0 = du0 + k_ref[w:w + 1, :] * _tap(wind, shd, CONV_WIDTH - 1 - w, tb)
            dk_ref[w:w + 1, :] += _colsum(_tap(winu, shu, HALO - (CONV_WIDTH - 1) + w, tb) * d)
        da_ref[...] = (du0 * sgb).astype(BF16)
        db_ref[...] = (du0 * a * sgb * (1.0 - sgb)).astype(BF16)

    prev = lambda i: jnp.maximum(i * hb - 1, 0)
    nxt = lambda i: jnp.minimum((i + 1) * hb, T // HALO - 1)
    return _hosted_call(
        body, rider, name="conv_bwd", grid=(CC // cw, nt),
        in_specs=[pl.BlockSpec((tb, cw), lambda c, i: (i, c)),
                  pl.BlockSpec((HALO, cw), lambda c, i: (nxt(i), c)),
                  pl.BlockSpec((tb, cw), lambda c, i: (i, ca + c)),
                  pl.BlockSpec((tb, cw), lambda c, i: (i, cb + c)),
                  pl.BlockSpec((HALO, cw), lambda c, i: (prev(i), ca + c)),
                  pl.BlockSpec((HALO, cw), lambda c, i: (prev(i), cb + c)),
                  pl.BlockSpec((HALO, cw), lambda c, i: (0, c))],
        out_specs=[pl.BlockSpec((tb, cw), lambda c, i: (i, c)),
                   pl.BlockSpec((tb, cw), lambda c, i: (i, c)),
                   pl.BlockSpec((HALO, cw), lambda c, i: (0, c))],
        out_shape=[jax.ShapeDtypeStruct((T, CC), BF16), jax.ShapeDtypeStruct((T, CC), BF16),
                   jax.ShapeDtypeStruct((HALO, CC), F32)],
        scratch=[pltpu.VMEM((tb + HALO, cw), F32), pltpu.VMEM((tb + HALO, cw), F32),
                 pltpu.VMEM((SUBLANES - 1, tb + HALO - SUBLANES, cw), F32),
                 pltpu.VMEM((SUBLANES - 1, tb + HALO - SUBLANES, cw), F32)],
        args=[du1, du1, proj, proj, proj, proj, kpad])


FFN1 = ('ffn1_w_gate', 'ffn1_w_up', 'ffn1_w_down')
FFN2 = ('ffn2_w_gate', 'ffn2_w_up', 'ffn2_w_down')


def _local_step(x, tgt, W, P, gdt=BF16, comm=None):
    T, D = x.shape
    G = {}
    net = _Net(comm, G)
    if comm is not None:
        W = comm.W
        first = net.gather(['ffn1_w_gate', 'ffn1_w_up'])
    (xb,) = _rows("x_to_bf16", lambda v: [v], [(x, 'r', D, 0)], [('r', D, BF16)], T=T, tb=512,
                  rider=first if comm is not None else None)
    if comm is not None:
        net.done(first)
    VW = P['ret_gn_g'].shape[1]
    H = VW // RET_DV
    QW = H * RET_DK
    CC = P['conv_b'].shape[1]
    off_glu = 2 * QW + 2 * VW
    off_gate = off_glu + 2 * CC
    ident = lambda accs: accs

    a1, b1, s1, z1 = _ffn_fwd("ffn1", xb, x, W, FFN1, net,
                              rider=net.gather(['ffn1_w_down', 'w_in'], {'w_in': (0, D // 4, False)}),
                              rider_down=lambda: net.gather(['w_in'], {'w_in': (D // 4, (3 * D) // 8, False)}))
    rider = net.gather(['w_in'], {'w_in': ((5 * D) // 8, (3 * D) // 8, True)})
    x1, x1b, xh1, rs1 = _ln_fwd("ln1", z1, P['ln1_g'], P['ln1_b'], T, D, rider=rider)
    net.done(rider)

    rest = net.gather(['conv_k', 'w_ret_o', 'w_conv_o', 'w_out'])
    (proj,) = _mm("w_in", [x1b], [W['w_in']], [[(0, 0)]], lambda accs, bias: [accs[0] + bias], [F32],
                  tm=2048, tn=0, tk=1024, extras=[(P['b_in'], 'n', 0)], i_outer=False, b3=True, rider=rest)
    net.done(rest)
    tabs = _ret_tables(H, T)
    rider = net.gather(['ffn2_w_gate', 'ffn2_w_up'])
    r, ret_in, states = _ret_fwd(proj, P['ret_gn_g'], tabs, H, T, rider=rider)
    net.done(rider)
    kpad = jnp.pad(W['conv_k'].astype(F32), ((0, HALO - CONV_WIDTH), (0, 0)))
    rider = net.gather(['ffn2_w_down'])
    u1 = _conv_fwd(proj, kpad, P['conv_b'], off_glu, CC, T, rider=rider)
    net.done(rider)

    def conv_ln(u1, g, b):
        xhat, rstd = _ln_stats(u1)
        u2 = xhat * g + b
        return [xhat, rstd, u2 * _sigmoid(u2)]

    xhc, rsc, u3 = _rows("conv_ln", conv_ln, [(u1, 'r', CC, 0), (P['conv_ln_g'], 'v', CC, 0), (P['conv_ln_b'], 'v', CC, 0)],
                         [('r', CC, F32), ('c', 1, F32), ('r', CC, BF16)], T=T, tb=512)
    (ret_out,) = _mm("ret_o", [ret_in], [W['w_ret_o']], [[(0, 0)]], ident, [F32], tm=1024, tn=1024, tk=2048)

    def epi_merge(accs, ret_out, gr, gc):
        conv_out = accs[0]
        return [conv_out, _sigmoid(gr) * ret_out + _sigmoid(gc) * conv_out]

    conv_out, merged = _mm("conv_o_merge", [u3], [W['w_conv_o']], [[(0, 0)]], epi_merge, [F32, BF16],
                           tm=512, tn=D, tk=1024, epi_rows=256,
                           extras=[(ret_out, 'mn', 0), (proj, 'mn', off_gate), (proj, 'mn', off_gate + D)])
    (z2,) = _mm("w_out", [merged], [W['w_out']], [[(0, 0)]], lambda accs, xr: [ALPHA * xr + accs[0]], [F32],
                tm=1024, tn=1024, tk=1024, extras=[(x1, 'mn', 0)])
    x2, x2b, xh2, rs2 = _ln_fwd("ln2", z2, P['ln2_g'], P['ln2_b'], T, D)
    a2, b2, s2, z3 = _ffn_fwd("ffn2", x2b, x2, W, FFN2, net)
    dz3, dz3h, g_ln3_g, g_ln3_b, loss = _ln_loss_bwd("ln3_loss", z3, P['ln3_g'], P['ln3_b'], tgt, T, D)

    S = {'ln3_g': g_ln3_g, 'ln3_b': g_ln3_b}
    dy2 = _ffn_bwd("ffn2b", dz3h, dz3, x2b, a2, b2, s2, W, FFN2, gdt, G, net, 'down')
    dz2, dz2b, S['ln2_g'], S['ln2_b'] = _ln_bwd("ln2b", dy2, xh2, rs2, P['ln2_g'], 1.0, T, D)

    (G['w_out'],) = _mm("d_w_out", [merged], [dz2b], [[(0, 0)]], ident, [gdt], ta=True, tm=1024, tn=1024, tk=1024)

    def epi_dmerge(accs, ret_out, conv_out, gr, gc):
        dm_ = accs[0]
        sr, sc = _sigmoid(gr), _sigmoid(gc)
        return [dm_ * sr, dm_ * sc, dm_ * ret_out * sr * (1.0 - sr), dm_ * conv_out * sc * (1.0 - sc)]

    dret_out, dconv_out, dgate_r, dgate_c = _mm(
        "d_merge", [dz2b], [W['w_out']], [[(0, 0)]], epi_dmerge, [BF16, BF16, BF16, BF16], tb=True,
        tm=512, tn=D, tk=1024, epi_rows=256,
        extras=[(ret_out, 'mn', 0), (conv_out, 'mn', 0), (proj, 'mn', off_gate), (proj, 'mn', off_gate + D)])
    (G['w_ret_o'],) = _mm("d_w_ret_o", [ret_in], [dret_out], [[(0, 0)]], ident, [gdt], ta=True, tm=1024, tn=1024, tk=1024)
    (G['w_conv_o'],) = _mm("d_w_conv_o", [u3], [dconv_out], [[(0, 0)]], ident, [gdt], ta=True, tm=1024, tn=1024, tk=1024)
    (dri,) = _mm("d_ret_in", [dret_out], [W['w_ret_o']], [[(0, 0)]], ident, [F32], tb=True, tm=1024, tn=1024, tk=1024)
    rider = net.exchange(['ffn2_w_gate', 'ffn2_w_up'])
    dproj, S['ret_gn_g'] = _ret_bwd(dri, r, states, proj, P['ret_gn_g'], tabs, H, T, proj.shape[1], rider=rider)
    net.done(rider)

    def epi_du2(accs, xhat, g, b):
        u2 = xhat * g + b
        return [accs[0] * _dsilu(u2, _sigmoid(u2))]

    (du2,) = _mm("d_u3", [dconv_out], [W['w_conv_o']], [[(0, 0)]], epi_du2, [F32], tb=True, tm=512, tn=CC, tk=1024, epi_rows=256,
                 extras=[(xhc, 'mn', 0), (P['conv_ln_g'], 'n', 0), (P['conv_ln_b'], 'n', 0)])

    def conv_ln_bwd(du2, xhat, rstd, g):
        du1 = _ln_bwd_math(du2, xhat, rstd, g)
        return [du1, _colsum(du2 * xhat), _colsum(du2), _colsum(du1)]

    du1, S['conv_ln_g'], S['conv_ln_b'], S['conv_b'] = _rows(
        "conv_ln_bwd", conv_ln_bwd, [(du2, 'r', CC, 0), (xhc, 'r', CC, 0), (rsc, 'r', 1, 0), (P['conv_ln_g'], 'v', CC, 0)],
        [('r', CC, F32), ('a', CC, F32), ('a', CC, F32), ('a', CC, F32)], T=T, tb=512)
    rider = net.exchange(['w_out', 'w_ret_o', 'w_conv_o'])
    dglu_a, dglu_b, dkpad = _conv_bwd(du1, proj, kpad, off_glu, CC, T, rider=rider)
    net.done(rider)
    G['conv_k'] = dkpad[:CONV_WIDTH].astype(gdt)

    for off, piece in ((off_glu, dglu_a), (off_glu + CC, dglu_b), (off_gate, dgate_r), (off_gate + D, dgate_c)):
        dproj = lax.dynamic_update_slice(dproj, piece, (0, off))
    IN_W = dproj.shape[1]
    G['w_in'], S['b_in'] = _mm("d_w_in", [x1b], [dproj], [[(0, 0)]], ident, [gdt], ta=True, o3=True, bsum=True,
                               tm=1024, tn=W['w_in'].shape[2], tk=1024)
    cuts = [0, (9 * D) // 16, (21 * D) // 32, (57 * D) // 64, D]
    w_in_rows = [{'w_in': (cuts[i], cuts[i + 1] - cuts[i], i == 3)} for i in range(4)]
    rider = net.exchange(['w_in', 'conv_k'], w_in_rows[0])
    (dy1,) = _mm("d_x1", [dproj], [W['w_in']], [[(0, 0)]], lambda accs, dzr: [ALPHA * dzr + accs[0]], [F32], tb=True,
                 b3=True, tm=1024, tn=1024, tk=0, extras=[(dz2, 'mn', 0)], rider=rider)
    net.done(rider)
    rider = net.exchange(['w_in'], w_in_rows[1])
    dz1, dz1h, S['ln1_g'], S['ln1_b'] = _ln_bwd("ln1b", dy1, xh1, rs1, P['ln1_g'], 0.5, T, D, rider=rider)
    net.done(rider)
    grad_x = _ffn_bwd("ffn1b", dz1h, dz1, xb, a1, b1, s1, W, FFN1, gdt, G, net, 'all',
                      pre=(lambda: net.exchange(['w_in'], w_in_rows[2]), lambda: net.exchange(['w_in'], w_in_rows[3])))
    return loss, grad_x, G, S


def _coords():
    return lax.axis_index("x"), lax.axis_index("y"), lax.axis_index("c")


def _flip(k, x, y, c):
    return (1 - x if k & 4 else x, 1 - y if k & 2 else y, 1 - c if k & 1 else c)


def _lin(p):
    return 4 * p[0] + 2 * p[1] + p[2]


class _Rider:
    def __init__(self, ins, out_shape, rows=None, fill=None):
        nb = len(ins)
        self.rows = rows or [None] * nb
        fill = fill or [None] * nb
        self.aliases = {nb + i: w for i, w in enumerate(w for w in range(nb) if fill[w] is not None)}
        self.ins = list(ins) + [f for f in fill if f is not None]
        self.out_shape, self.results = list(out_shape), None
        self.scratch = [pltpu.SemaphoreType.DMA((7 * nb,)), pltpu.SemaphoreType.DMA((7 * nb,)),
                        pltpu.SemaphoreType.DMA((nb,))]

    def span(self, w, ref, *slot):
        if self.rows[w] is None:
            return ref.at[slot] if slot else ref
        return ref.at[(*slot, pl.ds(*self.rows[w]))]

    def begin(self, step, n_steps, ins, outs, sems):
        @pl.when(step == 0)
        def _():
            self.start(ins, outs, sems)

        @pl.when(step == n_steps - 1)
        def _():
            self.mid(ins, outs, sems)

    def end(self, step, n_steps, ins, outs, sems):
        @pl.when(step == n_steps - 1)
        def _():
            self.finish(ins, outs, sems)

    def mid(self, ins, outs, sems):
        pass


class _GatherRider(_Rider):
    def __init__(self, blks, rows=None, fill=None):
        super().__init__(blks, [jax.ShapeDtypeStruct((N_DEV,) + b.shape, b.dtype) for b in blks], rows, fill)

    def _copies(self, x_refs, out_refs, sems):
        nb = len(self.out_shape)
        send_sems, recv_sems, local_sems = sems
        x, y, c = _coords()
        me, sibling = (x, y, c), (x, y, 1 - c)
        chips = [_flip(4, x, y, c), _flip(2, x, y, c), _flip(6, x, y, c)]
        own = [self.span(w, x_refs[w]) for w in range(nb)]

        def copy(k, w, block, to, src=None):
            slot = self.span(w, out_refs[w], _lin(block))
            return pltpu.make_async_remote_copy(
                src_ref=slot if src is None else src, dst_ref=slot, send_sem=send_sems.at[k * nb + w],
                recv_sem=recv_sems.at[k * nb + w], device_id=to, device_id_type=MESH)

        mines = [pltpu.make_async_copy(own[w], self.span(w, out_refs[w], _lin(me)), local_sems.at[w])
                 for w in range(nb)]
        first = [copy(0, w, me, sibling, src=own[w]) for w in range(nb)]
        first += [copy(1 + j, w, me, chip, src=own[w]) for w in range(nb) for j, chip in enumerate(chips)]
        landed = [(copy(1 + j, w, chip, me), copy(4 + j, w, chip, sibling)) for w in range(nb) for j, chip in enumerate(chips)]
        from_sibling = [copy(0, w, sibling, me) for w in range(nb)]
        from_sibling += [copy(4 + j, w, (chip[0], chip[1], 1 - c), me) for w in range(nb) for j, chip in enumerate(chips)]
        return mines, first, landed, from_sibling

    def start(self, ins, outs, sems):
        mines, first, _, _ = self._copies(ins, outs, sems)
        for cp in mines + first:
            cp.start()

    def mid(self, ins, outs, sems):
        for arrival, onward in self._copies(ins, outs, sems)[2]:
            arrival.wait_recv()
            onward.start()

    def finish(self, ins, outs, sems):
        mines, first, landed, from_sibling = self._copies(ins, outs, sems)
        for cp in from_sibling:
            cp.wait_recv()
        for cp in first + [onward for _, onward in landed]:
            cp.wait_send()
        for mine in mines:
            mine.wait()


class _ExchangeRider(_Rider):
    def __init__(self, gs, rows=None, fill=None):
        super().__init__(gs, [jax.ShapeDtypeStruct(g.shape, g.dtype) for g in gs], rows, fill)

    def _copies(self, g_refs, out_refs, sems):
        nb = len(self.out_shape)
        send_sems, recv_sems, local_sems = sems
        x, y, c = _coords()
        me = _lin((x, y, c))

        def copy(k, w, landing):
            peer = _flip(k, x, y, c)
            src, dst = (me, _lin(peer)) if landing else (_lin(peer), me)
            return pltpu.make_async_remote_copy(
                src_ref=self.span(w, g_refs[w], src), dst_ref=self.span(w, out_refs[w], dst),
                send_sem=send_sems.at[(k - 1) * nb + w], recv_sem=recv_sems.at[(k - 1) * nb + w],
                device_id=peer, device_id_type=MESH)

        mines = [pltpu.make_async_copy(self.span(w, g_refs[w], me), self.span(w, out_refs[w], me), local_sems.at[w])
                 for w in range(nb)]
        sends = [copy(k, w, False) for w in range(nb) for k in range(1, N_DEV)]
        landings = [copy(k, w, True) for w in range(nb) for k in range(1, N_DEV)]
        return mines, sends, landings

    def start(self, ins, outs, sems):
        mines, sends, _ = self._copies(ins, outs, sems)
        for cp in mines + sends:
            cp.start()

    def finish(self, ins, outs, sems):
        mines, sends, landings = self._copies(ins, outs, sems)
        for cp in landings:
            cp.wait_recv()
        for cp in sends:
            cp.wait_send()
        for mine in mines:
            mine.wait()


def _run_rider(name, rider):
    n_in, n_out = len(rider.ins), len(rider.out_shape)

    def body(*refs):
        ride = (refs[:n_in], refs[n_in:n_in + n_out], refs[n_in + n_out:])
        rider.start(*ride)
        rider.mid(*ride)
        rider.finish(*ride)

    rider.results = pl.pallas_call(
        body, name=name, out_shape=rider.out_shape, in_specs=[ANY] * n_in, out_specs=[ANY] * n_out,
        scratch_shapes=rider.scratch, input_output_aliases=dict(rider.aliases),
        compiler_params=pltpu.CompilerParams(has_side_effects=True),
    )(*rider.ins)
    return rider.results


def _as_matrix(name, g):
    if name == 'w_in':
        return g
    if name in COL_SHARDED:
        return jnp.transpose(g, (1, 0, 2)).reshape(g.shape[1], N_DEV * g.shape[2])
    return g.reshape(N_DEV * g.shape[1], g.shape[2])


def _by_owner(name, g):
    if name == 'w_in':
        return g
    if name in COL_SHARDED:
        return jnp.transpose(g.reshape(g.shape[0], N_DEV, g.shape[1] // N_DEV), (1, 0, 2))
    return g.reshape(N_DEV, g.shape[0] // N_DEV, g.shape[1])


class _Comm:
    def __init__(self, shards):
        self.shards, self.W, self.parts, self.partial, self.sent = shards, {}, {}, {}, {}

    def _ride(self, cls, names, srcs, part, sink):
        part = part or {}
        rider = cls(srcs, rows=[part[n][:2] if n in part else None for n in names],
                    fill=[self.partial.pop((sink, n), None) for n in names])
        rider.names, rider.sink = names, sink
        rider.unfinished = {n for n in names if n in part and not part[n][2]}
        return rider

    def gather(self, names, part=None):
        return self._ride(_GatherRider, names, [self.shards[n] for n in names], part, 'W')

    def exchange(self, names, G, part=None):
        for n in names:
            if n not in self.sent:
                self.sent[n] = _by_owner(n, G[n])
        return self._ride(_ExchangeRider, names, [self.sent[n] for n in names], part, 'parts')

    def collect(self, rider):
        for n, res in zip(rider.names, rider.results):
            if n in rider.unfinished:
                self.partial[(rider.sink, n)] = res
            elif rider.sink == 'W':
                self.W[n] = _as_matrix(n, res)
            else:
                self.parts[n] = res


def _adamw_math(p_ref, w_ref, m_ref, v_ref, g_ref, d_ref, nm_ref, nv_ref):
    c1 = 1.0 - ADAM_B1 ** ADAM_STEP
    c2 = 1.0 - ADAM_B2 ** ADAM_STEP
    g = p_ref[0].astype(F32)
    for s in range(1, p_ref.shape[0]):
        g = g + p_ref[s].astype(F32)
    nm = ADAM_B1 * m_ref[...] + (1.0 - ADAM_B1) * g
    nv = ADAM_B2 * v_ref[...] + (1.0 - ADAM_B2) * (g * g)
    g_ref[...] = g
    nm_ref[...] = nm
    nv_ref[...] = nv
    d_ref[...] = -ADAM_LR * ((nm / c1) / (jnp.sqrt(nv / c2) + ADAM_EPS) + ADAM_WD * w_ref[...])


def _adamw_vectors(parts, ws, ms, vs, loss_parts):
    k = len(ws)

    def body(*refs):
        for i in range(k):
            _adamw_math(refs[i], refs[k + i], refs[2 * k + i], refs[3 * k + i], *refs[4 * k + 1 + 4 * i:4 * k + 5 + 4 * i])
        lp, lo = refs[4 * k], refs[8 * k + 1]
        lo[...] = functools.reduce(jnp.add, [lp[s] for s in range(lp.shape[0])])

    return pl.pallas_call(
        body, name="adamw_vectors",
        out_shape=[jax.ShapeDtypeStruct(w.shape, F32) for w in ws for _ in range(4)] + [jax.ShapeDtypeStruct((1, 128), F32)],
        compiler_params=_params(),
    )(*parts, *ws, *ms, *vs, loss_parts)


def _adamw(name, parts, w, m, v, tb):
    n, R, Wd = parts.shape
    assert R % tb == 0
    body = functools.partial(_adamw_math)

    row = pl.BlockSpec((tb, Wd), lambda i: (i, 0))
    return pl.pallas_call(
        body, name=name, grid=(R // tb,),
        in_specs=[pl.BlockSpec((n, tb, Wd), lambda i: (0, i, 0)), row, row, row],
        out_specs=[row, row, row, row], out_shape=[jax.ShapeDtypeStruct((R, Wd), F32)] * 4,
        compiler_params=_params(("arbitrary",)),
    )(parts, w, m, v)


def _row_tile(R, unit, cap):
    best = unit
    for t in range(unit, cap + 1, unit):
        if R % t == 0:
            best = t
    return best


def kernel(x, ffn1_w_gate, ffn1_w_up, ffn1_w_down, ln1_g, ln1_b, w_in, b_in, ret_gn_g, conv_k, conv_b, conv_ln_g, conv_ln_b, w_ret_o, w_conv_o, w_out, ln2_g, ln2_b, ffn2_w_gate, ffn2_w_up, ffn2_w_down, ln3_g, ln3_b, loss_target, m_ffn1_w_gate, m_ffn1_w_up, m_ffn1_w_down, m_ln1_g, m_ln1_b, m_w_in, m_b_in, m_ret_gn_g, m_conv_k, m_conv_b, m_conv_ln_g, m_conv_ln_b, m_w_ret_o, m_w_conv_o, m_w_out, m_ln2_g, m_ln2_b, m_ffn2_w_gate, m_ffn2_w_up, m_ffn2_w_down, m_ln3_g, m_ln3_b, v_ffn1_w_gate, v_ffn1_w_up, v_ffn1_w_down, v_ln1_g, v_ln1_b, v_w_in, v_b_in, v_ret_gn_g, v_conv_k, v_conv_b, v_conv_ln_g, v_conv_ln_b, v_w_ret_o, v_w_conv_o, v_w_out, v_ln2_g, v_ln2_b, v_ffn2_w_gate, v_ffn2_w_up, v_ffn2_w_down, v_ln3_g, v_ln3_b):
    given = dict(locals())
    wts = {n: given[n] for n in WEIGHTS}
    mom = {n: given['m_' + n] for n in WEIGHTS}
    var = {n: given['v_' + n] for n in WEIGHTS}

    def shard2d(a):
        return a.reshape(a.shape[-3] * a.shape[-2] if a.ndim == 4 else a.shape[-2], a.shape[-1])

    comm = _Comm({n: shard2d(wts[n]).astype(BF16) for n in BIG})
    P = {n: wts[n].reshape(1, -1) for n in SMALL}
    loss, grad_x, _, S = _local_step(x[0], loss_target[0], None, P, comm=comm)

    parts = comm.parts
    res = {}
    for n in BIG:
        rows, cols = parts[n].shape[1:]
        tb = rows if rows % 16 else _row_tile(rows, 16, max(16, (256 * 1024) // cols))
        res[n] = _adamw("adamw_" + n, parts[n], shard2d(wts[n]), shard2d(mom[n]), shard2d(var[n]), tb)

    vec_parts = _run_rider("gather_vector_grads", _GatherRider([S[n] for n in SMALL] + [loss]))
    vec = _adamw_vectors(vec_parts[:-1], [P[n] for n in SMALL], [mom[n].reshape(1, -1) for n in SMALL],
                         [var[n].reshape(1, -1) for n in SMALL], vec_parts[-1])
    for i, n in enumerate(SMALL):
        res[n] = vec[4 * i:4 * i + 4]

    outs = [vec[-1][0, 0], grad_x[None]]
    for k in range(4):
        for n in WEIGHTS:
            outs.append(res[n][k].reshape(wts[n].shape))
    return tuple(outs)
```

```python
import functools
import math

import jax
import jax.numpy as jnp
from jax import lax
from jax.experimental import pallas as pl
from jax.experimental.pallas import tpu as pltpu

F32 = jnp.float32
BF16 = jnp.bfloat16

N_DEV = 8
LN_EPS = 1e-5
ALPHA = 2.0 ** 0.25
RET_DK = 128
RET_DV = 256
RET_CHUNK = 256
ROPE_BASE = 10000.0
CONV_WIDTH = 31
HALO = 32
ADAM_LR, ADAM_B1, ADAM_B2, ADAM_EPS, ADAM_WD, ADAM_STEP = 0.001, 0.9, 0.999, 1e-08, 0.01, 10
VMEM_LIMIT = 52 * 1024 * 1024
MESH = pl.DeviceIdType.MESH
ANY = pl.BlockSpec(memory_space=pl.ANY)

BIG = ['ffn1_w_gate', 'ffn1_w_up', 'ffn1_w_down', 'w_in', 'w_ret_o', 'w_conv_o', 'w_out',
       'ffn2_w_gate', 'ffn2_w_up', 'ffn2_w_down', 'conv_k']
COL_SHARDED = {'ffn1_w_gate', 'ffn1_w_up', 'w_in', 'ffn2_w_gate', 'ffn2_w_up', 'conv_k'}
SMALL = ['ln1_g', 'ln1_b', 'b_in', 'ret_gn_g', 'conv_b', 'conv_ln_g', 'conv_ln_b', 'ln2_g', 'ln2_b', 'ln3_g', 'ln3_b']
WEIGHTS = ['ffn1_w_gate', 'ffn1_w_up', 'ffn1_w_down', 'ln1_g', 'ln1_b', 'w_in', 'b_in', 'ret_gn_g', 'conv_k', 'conv_b',
           'conv_ln_g', 'conv_ln_b', 'w_ret_o', 'w_conv_o', 'w_out', 'ln2_g', 'ln2_b', 'ffn2_w_gate', 'ffn2_w_up',
           'ffn2_w_down', 'ln3_g', 'ln3_b']


def _params(sem=None):
    return pltpu.CompilerParams(dimension_semantics=sem, vmem_limit_bytes=VMEM_LIMIT)


def _sigmoid(x):
    return jax.nn.sigmoid(x)


def _dsilu(x, sg):
    return sg * (1.0 + x * (1.0 - sg))


def _fit(dim, want):
    if dim <= want:
        return dim
    return max(t for t in range(128, want + 1, 128) if dim % t == 0)


def _dot(a, b, ta=False, tb=False):
    dn = (((0,) if ta else (1,), (1,) if tb else (0,)), ((), ()))
    return lax.dot_general(a, b, dn, preferred_element_type=F32)


def _mm(name, As, Bs, prods, epi, out_dtypes, *, ta=False, tb=False, tm, tn, tk, extras=(), i_outer=True,
        b3=False, o3=False, rider=None, bsum=False, epi_rows=0):
    a0, b0 = As[0], Bs[0]
    M, K = (a0.shape[1], a0.shape[0]) if ta else a0.shape
    if b3:
        S, rows, cs = b0.shape
        N = rows if tb else S * cs
        assert K == (S * cs if tb else rows)
        tn, tk = (tn, cs) if tb else (cs, tk)
    else:
        N = b0.shape[0] if tb else b0.shape[1]
    tm, tn, tk = _fit(M, tm), _fit(N, tn), _fit(K, tk)
    assert M % tm == 0 and N % tn == 0 and K % tk == 0, (name, M, N, K, tm, tn, tk)
    gi, gj, gk = M // tm, N // tn, K // tk
    grid = (gi, gj, gk) if i_outer else (gj, gi, gk)

    def ij(g0, g1):
        return (g0, g1) if i_outer else (g1, g0)

    def amap(g0, g1, k):
        i, _ = ij(g0, g1)
        return (k, i) if ta else (i, k)

    def bmap(g0, g1, k):
        _, j = ij(g0, g1)
        return (j, k) if tb else (k, j)

    def bmap3(g0, g1, k):
        _, j = ij(g0, g1)
        return (k, j, 0) if tb else (j, k, 0)

    in_specs = [pl.BlockSpec((tk, tm) if ta else (tm, tk), amap) for _ in As]
    if b3:
        in_specs += [pl.BlockSpec((None, tn, tk) if tb else (None, tk, tn), bmap3) for _ in Bs]
    else:
        in_specs += [pl.BlockSpec((tn, tk) if tb else (tk, tn), bmap) for _ in Bs]
    args = list(As) + list(Bs)
    for arr, kind, coloff in extras:
        assert coloff % tn == 0
        off = coloff // tn
        if kind == 'mn':
            in_specs.append(pl.BlockSpec((tm, tn), lambda g0, g1, k, off=off: (ij(g0, g1)[0], ij(g0, g1)[1] + off)))
        else:
            in_specs.append(pl.BlockSpec((1, tn), lambda g0, g1, k, off=off: (0, ij(g0, g1)[1] + off)))
        args.append(arr)
    if o3:
        out_shape = [jax.ShapeDtypeStruct((gj, M, tn), dt) for dt in out_dtypes]
        out_specs = [pl.BlockSpec((None, tm, tn), lambda g0, g1, k: (ij(g0, g1)[1], ij(g0, g1)[0], 0))
                     for _ in out_dtypes]
    else:
        out_shape = [jax.ShapeDtypeStruct((M, N), dt) for dt in out_dtypes]
        out_specs = [pl.BlockSpec((tm, tn), lambda g0, g1, k: ij(g0, g1)) for _ in out_dtypes]
    if bsum:
        assert gi == 1 and not tb and not b3
        out_shape.append(jax.ShapeDtypeStruct((1, N), F32))
        out_specs.append(pl.BlockSpec((1, tn), lambda g0, g1, k: (0, ij(g0, g1)[1])))
    n_a, n_b, n_e, n_o = len(As), len(Bs), len(extras), len(out_shape)
    n_p = len(prods) if gk > 1 else 0
    scratch = [pltpu.VMEM((tm, tn), F32) for _ in range(n_p)]
    if rider is not None:
        in_specs, out_specs = in_specs + [ANY] * len(rider.ins), out_specs + [ANY] * len(rider.out_shape)
        args, out_shape, scratch = args + rider.ins, out_shape + rider.out_shape, scratch + rider.scratch
    n_in, n_out = len(args), len(out_shape)

    def body(*refs):
        a_refs = refs[:n_a]
        b_refs = refs[n_a:n_a + n_b]
        e_refs = refs[n_a + n_b:n_a + n_b + n_e]
        o_refs = refs[n_in:n_in + n_o]
        acc_refs = refs[n_in + n_out:n_in + n_out + n_p]
        k = pl.program_id(2)
        if rider is not None:
            step = (pl.program_id(0) * grid[1] + pl.program_id(1)) * gk + k
            ride = (step, grid[0] * grid[1] * gk, refs[n_a + n_b + n_e:n_in], refs[n_in + n_o:n_in + n_out],
                    refs[n_in + n_out + n_p:])
            rider.begin(*ride)

        def finish(accs, rows=slice(None)):
            ex = [(e[rows, :] if kind == 'mn' else e[...]).astype(F32) for e, (_, kind, _) in zip(e_refs, extras)]
            for o, r in zip(o_refs, epi(accs, *ex)):
                o[rows, :] = r.astype(o.dtype)

        if bsum:
            @pl.when(k == 0)
            def _():
                o_refs[-1][...] = jnp.zeros_like(o_refs[-1])

            o_refs[-1][...] += _colsum(b_refs[0][...].astype(F32))

        if gk == 1:
            sub = tm if ta or not epi_rows else _fit(tm, epi_rows)
            for r0 in range(0, tm, sub):
                rows = slice(None) if ta else slice(r0, r0 + sub)
                finish([functools.reduce(jnp.add, [_dot(a_refs[ai][...] if ta else a_refs[ai][rows, :],
                                                        b_refs[bi][...], ta, tb) for ai, bi in terms])
                        for terms in prods], rows)
        else:
            @pl.when(k == 0)
            def _():
                for acc in acc_refs:
                    acc[...] = jnp.zeros_like(acc)

            for p, terms in enumerate(prods):
                for ai, bi in terms:
                    acc_refs[p][...] += _dot(a_refs[ai][...], b_refs[bi][...], ta, tb)

            @pl.when(k == gk - 1)
            def _():
                finish([acc[...] for acc in acc_refs])

        if rider is not None:
            rider.end(*ride)

    aliases = {} if rider is None else {n_a + n_b + n_e + p: n_o + o for p, o in rider.aliases.items()}
    res = pl.pallas_call(
        body, name=name, grid=grid, in_specs=in_specs, out_specs=out_specs, out_shape=out_shape,
        scratch_shapes=scratch, input_output_aliases=aliases,
        compiler_params=_params(("arbitrary", "arbitrary", "arbitrary")),
    )(*args)
    if rider is not None:
        rider.results = res[n_o:]
    return res[:n_o]


def _rows(name, fn, ins, outs, *, T, tb, rider=None):
    tb = min(tb, T)
    assert T % tb == 0
    in_specs, args = [], []
    for arr, kind, width, cb in ins:
        if kind == 'r':
            in_specs.append(pl.BlockSpec((tb, width), lambda i, _, cb=cb: (i, cb)))
        else:
            in_specs.append(pl.BlockSpec((1, width), lambda i, _, cb=cb: (0, cb)))
        args.append(arr)
    out_shape, out_specs = [], []
    for kind, width, dtype in outs:
        if kind == 'r':
            out_shape.append(jax.ShapeDtypeStruct((T, width), dtype))
            out_specs.append(pl.BlockSpec((tb, width), lambda i, _: (i, 0)))
        elif kind == 'c':
            out_shape.append(jax.ShapeDtypeStruct((T, 1), dtype))
            out_specs.append(pl.BlockSpec((tb, 1), lambda i, _: (i, 0)))
        else:
            out_shape.append(jax.ShapeDtypeStruct((1, width), F32))
            out_specs.append(pl.BlockSpec((1, width), lambda i, _: (0, 0)))
    n_in = len(ins)

    def body(*refs):
        i = pl.program_id(0)
        vals = fn(*[r[...] for r in refs[:n_in]])
        for (kind, _, _), o, v in zip(outs, refs[n_in:], vals):
            if kind == 'a':
                @pl.when(i == 0)
                def _(o=o):
                    o[...] = jnp.zeros_like(o)

                o[...] += v
            else:
                o[...] = v.astype(o.dtype)

    return _hosted_call(body, rider, name=name, grid=(T // tb, 1), in_specs=in_specs, out_specs=out_specs,
                        out_shape=out_shape, scratch=[], args=args)


def _colsum(v):
    return jnp.sum(v, axis=0, keepdims=True)


def _ln_stats(z):
    mu = jnp.mean(z, axis=-1, keepdims=True)
    d = z - mu
    var = jnp.mean(d * d, axis=-1, keepdims=True)
    rstd = lax.rsqrt(var + LN_EPS)
    return d * rstd, rstd


def _ln_bwd_math(dy, xhat, rstd, g):
    dxh = dy * g
    m1 = jnp.mean(dxh, axis=-1, keepdims=True)
    m2 = jnp.mean(dxh * xhat, axis=-1, keepdims=True)
    return rstd * (dxh - m1 - xhat * m2)


def _ln_fwd(name, z, g, b, T, D, rider=None):
    def fn(z, g, b):
        xhat, rstd = _ln_stats(z)
        y = xhat * g + b
        return [y, y, xhat, rstd]

    return _rows(name, fn, [(z, 'r', D, 0), (g, 'v', D, 0), (b, 'v', D, 0)],
                 [('r', D, F32), ('r', D, BF16), ('r', D, F32), ('c', 1, F32)], T=T, tb=512, rider=rider)


def _ln_bwd(name, dy, xhat, rstd, g, scale, T, D, rider=None):
    def fn(dy, xhat, rstd, g):
        dz = _ln_bwd_math(dy, xhat, rstd, g)
        return [dz, dz * scale, _colsum(dy * xhat), _colsum(dy)]

    return _rows(name, fn, [(dy, 'r', D, 0), (xhat, 'r', D, 0), (rstd, 'r', 1, 0), (g, 'v', D, 0)],
                 [('r', D, F32), ('r', D, BF16), ('a', D, F32), ('a', D, F32)], T=T, tb=512, rider=rider)


def _ln_loss_bwd(name, z, g, b, tgt, T, D):
    def fn(z, g, b, tgt):
        xhat, rstd = _ln_stats(z)
        err = xhat * g + b - tgt
        row_loss = 0.5 * jnp.mean(err * err, axis=-1, keepdims=True)
        loss = jnp.broadcast_to(jnp.sum(row_loss, axis=0, keepdims=True), (1, 128))
        dy = err * (1.0 / D)
        dz = _ln_bwd_math(dy, xhat, rstd, g)
        return [dz, dz * 0.5, _colsum(dy * xhat), _colsum(dy), loss]

    return _rows(name, fn, [(z, 'r', D, 0), (g, 'v', D, 0), (b, 'v', D, 0), (tgt, 'r', D, 0)],
                 [('r', D, F32), ('r', D, BF16), ('a', D, F32), ('a', D, F32), ('a', 128, F32)], T=T, tb=512)


class _Net:
    def __init__(self, comm, G):
        self.comm, self.G = comm, G

    def gather(self, names, part=None):
        return self.comm.gather(names, part) if self.comm else None

    def exchange(self, names, part=None):
        return self.comm.exchange(names, self.G, part) if self.comm else None

    def done(self, rider):
        if rider is not None:
            self.comm.collect(rider)


def _ffn_fwd(tag, xb, x, W, names, net, rider=None, rider_down=None):
    def epi_gu(accs):
        a, b = accs
        return [a, b, a * _sigmoid(a) * b]

    ng, nu, nd = names
    a, b, s = _mm(tag + "_gate_up", [xb], [W[ng], W[nu]], [[(0, 0)], [(0, 1)]], epi_gu, [BF16, BF16, BF16],
                  tm=1024, tn=1408, tk=1024, rider=rider, epi_rows=256)
    net.done(rider)

    def epi_down(accs, xres):
        return [ALPHA * xres + 0.5 * accs[0]]

    rider_down = rider_down() if rider_down else None
    (z,) = _mm(tag + "_down", [s], [W[nd]], [[(0, 0)]], epi_down, [F32], tm=1024, tn=1024, tk=1408,
               extras=[(x, 'mn', 0)], rider=rider_down)
    net.done(rider_down)
    return a, b, s, z


def _ffn_bwd(tag, dzh, dz, xb, a, b, s, W, names, gdt, G, net, ride, pre=(None, None)):
    ng, nu, nd = names

    def epi_ds(accs, a, b):
        ds = accs[0]
        sg = _sigmoid(a)
        return [ds * b * _dsilu(a, sg), ds * a * sg]

    rider = pre[0]() if pre[0] else None
    da, db = _mm(tag + "_ds", [dzh], [W[nd]], [[(0, 0)]], epi_ds, [BF16, BF16], tb=True, tm=1024, tn=1408, tk=1024, epi_rows=256,
                 extras=[(a, 'mn', 0), (b, 'mn', 0)], rider=rider)
    net.done(rider)
    ident = lambda accs: accs
    rider = pre[1]() if pre[1] else None
    (G[nd],) = _mm(tag + "_dwd", [s], [dzh], [[(0, 0)]], ident, [gdt], ta=True, tm=1408, tn=1024, tk=1024,
                   rider=rider)
    net.done(rider)
    if ride == 'all':
        rider = net.exchange([nd])
        (G[ng],) = _mm(tag + "_dwg", [xb], [da], [[(0, 0)]], ident, [gdt], ta=True, tm=1024, tn=1408, tk=1024,
                       rider=rider)
        net.done(rider)
        rider = net.exchange([ng])
        (G[nu],) = _mm(tag + "_dwu", [xb], [db], [[(0, 0)]], ident, [gdt], ta=True, tm=1024, tn=1408, tk=1024,
                       rider=rider)
        net.done(rider)
    else:
        G[ng], G[nu] = _mm(tag + "_dwgu", [xb], [da, db], [[(0, 0)], [(0, 1)]], ident, [gdt, gdt], ta=True,
                           tm=1024, tn=1408, tk=1024)

    def epi_dx(accs, dzres):
        return [ALPHA * dzres + accs[0]]

    rider = net.exchange([nu] if ride == 'all' else [nd]) if ride else None
    (dx,) = _mm(tag + "_dx", [da, db], [W[ng], W[nu]], [[(0, 0), (1, 1)]], epi_dx, [F32], tb=True,
                tm=1024, tn=1024, tk=1408, extras=[(dz, 'mn', 0)], rider=rider)
    net.done(rider)
    return dx


def _ret_tables(H, T):
    C = RET_CHUNK
    log_g = jnp.log(1.0 - jnp.exp2(-5.0 - jnp.arange(H, dtype=F32)))
    idx = jnp.arange(C, dtype=F32)
    diff = idx[:, None] - idx[None, :]
    dm = jnp.where(diff[None] >= 0, jnp.exp(jnp.maximum(diff, 0.0)[None] * log_g[:, None, None]), 0.0)
    xi = jnp.exp((idx[None, :] + 1.0) * log_g[:, None])[:, :, None]
    zeta = jnp.exp((C - 1.0 - idx)[None, :] * log_g[:, None])[:, :, None]
    gc = jnp.broadcast_to(jnp.exp(C * log_g)[:, None, None], (H, 1, RET_DV))
    half = RET_DK // 2
    freqs = ROPE_BASE ** (-jnp.arange(half, dtype=F32) / half)
    ang = jnp.arange(T, dtype=F32)[:, None] * freqs[None, :]
    cos, sin = jnp.cos(ang), jnp.sin(ang)
    cosf = jnp.concatenate([cos, cos], axis=1)
    sins = jnp.concatenate([-sin, sin], axis=1)
    return dm, xi, zeta, gc, cosf, sins


def _rot(x, cosf, sins):
    return x * cosf + pltpu.roll(x, RET_DK // 2, 1) * sins


def _rot_bwd(dy, cosf, sins):
    return dy * cosf + pltpu.roll(dy * sins, RET_DK // 2, 1)


RET_HB = 8


def _ret_specs(H, HB, rev, NC):
    C, G = RET_CHUNK, H // HB
    nn = (lambda n: NC - 1 - n) if rev else (lambda n: n)
    return [
        pl.BlockSpec((C, HB * RET_DK), lambda h, n: (nn(n), h)),
        pl.BlockSpec((C, HB * RET_DK), lambda h, n: (nn(n), G + h)),
        pl.BlockSpec((C, HB * RET_DV), lambda h, n: (nn(n), G + h)),
        pl.BlockSpec((C, HB * RET_DV), lambda h, n: (nn(n), 2 * G + h)),
        pl.BlockSpec((C, RET_DK), lambda h, n: (nn(n), 0)),
        pl.BlockSpec((C, RET_DK), lambda h, n: (nn(n), 0)),
        pl.BlockSpec((1, HB * RET_DV), lambda h, n: (0, h)),
        pl.BlockSpec((HB, C, C), lambda h, n: (h, 0, 0)),
        pl.BlockSpec((HB, C, 1), lambda h, n: (h, 0, 0)),
        pl.BlockSpec((HB, C, 1), lambda h, n: (h, 0, 0)),
        pl.BlockSpec((HB, 1, RET_DV), lambda h, n: (h, 0, 0)),
    ]


def _ret_fwd(proj, gn_g, tabs, H, T, rider=None):
    C, NC = RET_CHUNK, T // RET_CHUNK
    HB = min(RET_HB, H)
    dm, xi, zeta, gc, cosf, sins = tabs
    scale = RET_DK ** -0.5

    def body(q_ref, k_ref, v_ref, g_ref, cos_ref, sin_ref, gn_ref, dm_ref, xi_ref, zt_ref, gc_ref,
             r_ref, ri_ref, st_ref, state):
        @pl.when(pl.program_id(1) == 0)
        def _():
            state[...] = jnp.zeros_like(state)

        cs, sn = cos_ref[...], sin_ref[...]
        hs = range(HB)
        qk = [slice(h * RET_DK, (h + 1) * RET_DK) for h in hs]
        vv = [slice(h * RET_DV, (h + 1) * RET_DV) for h in hs]
        kr = [_rot(k_ref[:, qk[h]].astype(F32), cs, sn) for h in hs]
        qb = [(_rot(q_ref[:, qk[h]].astype(F32), cs, sn) * scale).astype(BF16) for h in hs]
        kb = [kr[h].astype(BF16) for h in hs]
        kzb = [(kr[h] * zt_ref[h]).astype(BF16) for h in hs]
        vb = [v_ref[:, vv[h]].astype(BF16) for h in hs]
        st = [state[h] for h in hs]
        stb = [st[h].astype(BF16) for h in hs]
        sb = [(_dot(qb[h], kb[h], tb=True) * dm_ref[h]).astype(BF16) for h in hs]
        cross = [_dot(qb[h], stb[h]) for h in hs]
        kv = [_dot(kzb[h], vb[h], ta=True) for h in hs]
        intra = [_dot(sb[h], vb[h]) for h in hs]
        for h in hs:
            st_ref[h] = stb[h]
            state[h] = gc_ref[h] * st[h] + kv[h]
        for h in hs:
            r = intra[h] + cross[h] * xi_ref[h]
            rhat, _ = _ln_stats(r)
            g = g_ref[:, vv[h]].astype(F32)
            r_ref[:, vv[h]] = r
            ri_ref[:, vv[h]] = (g * _sigmoid(g) * (rhat * gn_ref[:, vv[h]])).astype(BF16)

    VW = H * RET_DV
    return _hosted_call(
        body, rider, name="ret_fwd", grid=(H // HB, NC), in_specs=_ret_specs(H, HB, False, NC),
        out_specs=[pl.BlockSpec((C, HB * RET_DV), lambda h, n: (n, h)),
                   pl.BlockSpec((C, HB * RET_DV), lambda h, n: (n, h)),
                   pl.BlockSpec((HB, None, RET_DK, RET_DV), lambda h, n: (h, n, 0, 0))],
        out_shape=[jax.ShapeDtypeStruct((T, VW), F32), jax.ShapeDtypeStruct((T, VW), BF16),
                   jax.ShapeDtypeStruct((H, NC, RET_DK, RET_DV), BF16)],
        scratch=[pltpu.VMEM((HB, RET_DK, RET_DV), F32)],
        args=[proj, proj, proj, proj, cosf, sins, gn_g, dm, xi, zeta, gc])


def _hosted_call(body, rider, *, name, grid, in_specs, out_specs, out_shape, scratch, args):
    n_in, n_out, n_scr = len(args), len(out_shape), len(scratch)
    if rider is None:
        hosted = body
    else:
        n_ri, n_ro = len(rider.ins), len(rider.out_shape)
        in_specs, out_specs = in_specs + [ANY] * n_ri, out_specs + [ANY] * n_ro
        args, out_shape, scratch = args + rider.ins, out_shape + rider.out_shape, scratch + rider.scratch

        def hosted(*refs):
            o0, s0 = n_in + n_ri, n_in + n_ri + n_out + n_ro
            step = pl.program_id(0) * grid[1] + pl.program_id(1)
            ride = (step, grid[0] * grid[1], refs[n_in:o0], refs[o0 + n_out:s0], refs[s0 + n_scr:])
            rider.begin(*ride)
            body(*refs[:n_in], *refs[o0:o0 + n_out], *refs[s0:s0 + n_scr])
            rider.end(*ride)

    aliases = {} if rider is None else {n_in + p: n_out + o for p, o in rider.aliases.items()}
    res = pl.pallas_call(
        hosted, name=name, grid=grid, in_specs=in_specs, out_specs=out_specs, out_shape=out_shape,
        scratch_shapes=scratch, input_output_aliases=aliases, compiler_params=_params(("arbitrary", "arbitrary")),
    )(*args)
    if rider is not None:
        rider.results = res[n_out:]
    return res[:n_out]


def _ret_bwd(dri, r, states, proj, gn_g, tabs, H, T, in_w, rider=None):
    C, NC = RET_CHUNK, T // RET_CHUNK
    HB = min(RET_HB, H)
    dm, xi, zeta, gc, cosf, sins = tabs
    scale = RET_DK ** -0.5

    def body(q_ref, k_ref, v_ref, g_ref, cos_ref, sin_ref, gn_ref, dm_ref, xi_ref, zt_ref, gc_ref,
             dri_ref, r_ref, st_ref, dp_ref, dgn_ref, dstate):
        @pl.when(pl.program_id(1) == 0)
        def _():
            dstate[...] = jnp.zeros_like(dstate)
            dgn_ref[...] = jnp.zeros_like(dgn_ref)

        cs, sn = cos_ref[...], sin_ref[...]
        hs = range(HB)
        qk = [slice(h * RET_DK, (h + 1) * RET_DK) for h in hs]
        vv = [slice(h * RET_DV, (h + 1) * RET_DV) for h in hs]
        qr = [_rot(q_ref[:, qk[h]].astype(F32), cs, sn) * scale for h in hs]
        kr = [_rot(k_ref[:, qk[h]].astype(F32), cs, sn) for h in hs]
        qb = [qr[h].astype(BF16) for h in hs]
        kb = [kr[h].astype(BF16) for h in hs]
        vb = [v_ref[:, vv[h]].astype(BF16) for h in hs]
        qxb = [(qr[h] * xi_ref[h]).astype(BF16) for h in hs]
        kzb = [(kr[h] * zt_ref[h]).astype(BF16) for h in hs]
        drb = []
        for h in hs:
            rhat, rstd = _ln_stats(r_ref[:, vv[h]])
            g, gn, dpre = g_ref[:, vv[h]].astype(F32), gn_ref[:, vv[h]], dri_ref[:, vv[h]]
            sg = _sigmoid(g)
            dp_ref[:, 2 * QW + VW + h * RET_DV:2 * QW + VW + (h + 1) * RET_DV] = (
                dpre * (rhat * gn) * _dsilu(g, sg)).astype(BF16)
            drn = dpre * (g * sg)
            dgn_ref[:, vv[h]] += _colsum(drn * rhat)
            drb.append(_ln_bwd_math(drn, rhat, rstd, gn).astype(BF16))
        ds1 = [dstate[h] for h in hs]
        ds1b = [ds1[h].astype(BF16) for h in hs]
        sb = [(_dot(qb[h], kb[h], tb=True) * dm_ref[h]).astype(BF16) for h in hs]
        dsb = [(_dot(drb[h], vb[h], tb=True) * dm_ref[h]).astype(BF16) for h in hs]
        dq_x = [_dot(drb[h], st_ref[h], tb=True) for h in hs]
        dk_x = [_dot(vb[h], ds1b[h], tb=True) for h in hs]
        dv_x = [_dot(kzb[h], ds1b[h]) for h in hs]
        dst = [_dot(qxb[h], drb[h], ta=True) for h in hs]
        for h in hs:
            dstate[h] = gc_ref[h] * ds1[h] + dst[h]
        dv_i = [_dot(sb[h], drb[h], ta=True) for h in hs]
        dq_i = [_dot(dsb[h], kb[h]) for h in hs]
        dk_i = [_dot(dsb[h], qb[h], ta=True) for h in hs]
        for h in hs:
            dp_ref[:, 2 * QW + h * RET_DV:2 * QW + (h + 1) * RET_DV] = (dv_i[h] + dv_x[h]).astype(BF16)
            dq = dq_i[h] + dq_x[h] * xi_ref[h]
            dk = dk_i[h] + dk_x[h] * zt_ref[h]
            dp_ref[:, qk[h]] = _rot_bwd(dq * scale, cs, sn).astype(BF16)
            dp_ref[:, QW + h * RET_DK:QW + (h + 1) * RET_DK] = _rot_bwd(dk, cs, sn).astype(BF16)

    VW, QW = H * RET_DV, H * RET_DK
    rv = lambda n: NC - 1 - n
    in_specs = _ret_specs(H, HB, True, NC) + [
        pl.BlockSpec((C, HB * RET_DV), lambda h, n: (rv(n), h)),
        pl.BlockSpec((C, HB * RET_DV), lambda h, n: (rv(n), h)),
        pl.BlockSpec((HB, None, RET_DK, RET_DV), lambda h, n: (h, rv(n), 0, 0)),
    ]
    assert HB == H
    return _hosted_call(
        body, rider, name="ret_bwd", grid=(1, NC), in_specs=in_specs,
        out_specs=[pl.BlockSpec((C, 2 * QW + 2 * VW), lambda h, n: (rv(n), 0)),
                   pl.BlockSpec((1, VW), lambda h, n: (0, 0))],
        out_shape=[jax.ShapeDtypeStruct((T, in_w), BF16), jax.ShapeDtypeStruct((1, VW), F32)],
        scratch=[pltpu.VMEM((HB, RET_DK, RET_DV), F32)],
        args=[proj, proj, proj, proj, cosf, sins, gn_g, dm, xi, zeta, gc, dri, r, states])


CONV_CW = 128
CONV_TB = 512


SUBLANES = 8


def _shift_copies(win, shifted, tb):
    n = tb + HALO - SUBLANES
    for s in range(1, SUBLANES):
        shifted[s - 1] = win[pl.ds(s, n), :]


def _tap(win, shifted, off, tb):
    s = off % SUBLANES
    if s == 0:
        return win[pl.ds(off, tb), :]
    return shifted[s - 1, pl.ds(off - s, tb), :]


def _conv_fwd(proj, kpad, bias, off_a, CC, T, rider=None):
    tb, cw = min(CONV_TB, T), CONV_CW
    hb = tb // HALO
    ca, cb = off_a // cw, (off_a + CC) // cw

    def body(a_ref, b_ref, ap_ref, bp_ref, k_ref, bias_ref, u1_ref, win, shifted):
        i = pl.program_id(0)
        keep = (i > 0).astype(F32)
        win[0:HALO, :] = ap_ref[...].astype(F32) * _sigmoid(bp_ref[...].astype(F32)) * keep
        win[HALO:, :] = a_ref[...].astype(F32) * _sigmoid(b_ref[...].astype(F32))
        _shift_copies(win, shifted, tb)
        acc = jnp.broadcast_to(bias_ref[...], (tb, cw))
        for w in range(CONV_WIDTH):
            acc = acc + k_ref[w:w + 1, :] * _tap(win, shifted, HALO - (CONV_WIDTH - 1) + w, tb)
        u1_ref[...] = acc

    prev = lambda i: jnp.maximum(i * hb - 1, 0)
    (u1,) = _hosted_call(
        body, rider, name="conv_fwd", grid=(T // tb, CC // cw),
        in_specs=[pl.BlockSpec((tb, cw), lambda i, c: (i, ca + c)),
                  pl.BlockSpec((tb, cw), lambda i, c: (i, cb + c)),
                  pl.BlockSpec((HALO, cw), lambda i, c: (prev(i), ca + c)),
                  pl.BlockSpec((HALO, cw), lambda i, c: (prev(i), cb + c)),
                  pl.BlockSpec((HALO, cw), lambda i, c: (0, c)),
                  pl.BlockSpec((1, cw), lambda i, c: (0, c))],
        out_specs=[pl.BlockSpec((tb, cw), lambda i, c: (i, c))],
        out_shape=[jax.ShapeDtypeStruct((T, CC), F32)],
        scratch=[pltpu.VMEM((tb + HALO, cw), F32), pltpu.VMEM((SUBLANES - 1, tb + HALO - SUBLANES, cw), F32)],
        args=[proj, proj, proj, proj, kpad, bias])
    return u1


def _conv_bwd(du1, proj, kpad, off_a, CC, T, rider=None):
    tb, cw = min(CONV_TB, T), CONV_CW
    hb = tb // HALO
    nt = T // tb
    ca, cb = off_a // cw, (off_a + CC) // cw

    def body(d_ref, dn_ref, a_ref, b_ref, ap_ref, bp_ref, k_ref, da_ref, db_ref, dk_ref, winu, wind, shu, shd):
        i = pl.program_id(1)
        a, b = a_ref[...].astype(F32), b_ref[...].astype(F32)
        sgb = _sigmoid(b)
        winu[0:HALO, :] = ap_ref[...].astype(F32) * _sigmoid(bp_ref[...].astype(F32)) * (i > 0).astype(F32)
        winu[HALO:, :] = a * sgb
        d = d_ref[...]
        wind[0:tb, :] = d
        wind[tb:, :] = dn_ref[...] * (i < nt - 1).astype(F32)

        @pl.when(i == 0)
        def _():
            dk_ref[...] = jnp.zeros_like(dk_ref)

        _shift_copies(winu, shu, tb)
        _shift_copies(wind, shd, tb)
        du0 = jnp.zeros((tb, cw), F32)
        for w in range(CONV_WIDTH):
            du0 = du0 + k_ref[w:w + 1, :] * _tap(wind, shd, CONV_WIDTH - 1 - w, tb)
            dk_ref[w:w + 1, :] += _colsum(_tap(winu, shu, HALO - (CONV_WIDTH - 1) + w, tb) * d)
        da_ref[...] = (du0 * sgb).astype(BF16)
        db_ref[...] = (du0 * a * sgb * (1.0 - sgb)).astype(BF16)

    prev = lambda i: jnp.maximum(i * hb - 1, 0)
    nxt = lambda i: jnp.minimum((i + 1) * hb, T // HALO - 1)
    return _hosted_call(
        body, rider, name="conv_bwd", grid=(CC // cw, nt),
        in_specs=[pl.BlockSpec((tb, cw), lambda c, i: (i, c)),
                  pl.BlockSpec((HALO, cw), lambda c, i: (nxt(i), c)),
                  pl.BlockSpec((tb, cw), lambda c, i: (i, ca + c)),
                  pl.BlockSpec((tb, cw), lambda c, i: (i, cb + c)),
                  pl.BlockSpec((HALO, cw), lambda c, i: (prev(i), ca + c)),
                  pl.BlockSpec((HALO, cw), lambda c, i: (prev(i), cb + c)),
                  pl.BlockSpec((HALO, cw), lambda c, i: (0, c))],
        out_specs=[pl.BlockSpec((tb, cw), lambda c, i: (i, c)),
                   pl.BlockSpec((tb, cw), lambda c, i: (i, c)),
                   pl.BlockSpec((HALO, cw), lambda c, i: (0, c))],
        out_shape=[jax.ShapeDtypeStruct((T, CC), BF16), jax.ShapeDtypeStruct((T, CC), BF16),
                   jax.ShapeDtypeStruct((HALO, CC), F32)],
        scratch=[pltpu.VMEM((tb + HALO, cw), F32), pltpu.VMEM((tb + HALO, cw), F32),
                 pltpu.VMEM((SUBLANES - 1, tb + HALO - SUBLANES, cw), F32),
                 pltpu.VMEM((SUBLANES - 1, tb + HALO - SUBLANES, cw), F32)],
        args=[du1, du1, proj, proj, proj, proj, kpad])


FFN1 = ('ffn1_w_gate', 'ffn1_w_up', 'ffn1_w_down')
FFN2 = ('ffn2_w_gate', 'ffn2_w_up', 'ffn2_w_down')


def _local_step(x, tgt, W, P, gdt=BF16, comm=None):
    T, D = x.shape
    G = {}
    net = _Net(comm, G)
    if comm is not None:
        W = comm.W
        first = net.gather(['ffn1_w_gate', 'ffn1_w_up'])
    (xb,) = _rows("x_to_bf16", lambda v: [v], [(x, 'r', D, 0)], [('r', D, BF16)], T=T, tb=512,
                  rider=first if comm is not None else None)
    if comm is not None:
        net.done(first)
    VW = P['ret_gn_g'].shape[1]
    H = VW // RET_DV
    QW = H * RET_DK
    CC = P['conv_b'].shape[1]
    off_glu = 2 * QW + 2 * VW
    off_gate = off_glu + 2 * CC
    ident = lambda accs: accs

    a1, b1, s1, z1 = _ffn_fwd("ffn1", xb, x, W, FFN1, net,
                              rider=net.gather(['ffn1_w_down', 'w_in'], {'w_in': (0, D // 4, False)}),
                              rider_down=lambda: net.gather(['w_in'], {'w_in': (D // 4, (3 * D) // 8, False)}))
    rider = net.gather(['w_in'], {'w_in': ((5 * D) // 8, (3 * D) // 8, True)})
    x1, x1b, xh1, rs1 = _ln_fwd("ln1", z1, P['ln1_g'], P['ln1_b'], T, D, rider=rider)
    net.done(rider)

    rest = net.gather(['conv_k', 'w_ret_o', 'w_conv_o', 'w_out'])
    (proj,) = _mm("w_in", [x1b], [W['w_in']], [[(0, 0)]], lambda accs, bias: [accs[0] + bias], [F32],
                  tm=2048, tn=0, tk=1024, extras=[(P['b_in'], 'n', 0)], i_outer=False, b3=True, rider=rest)
    net.done(rest)
    tabs = _ret_tables(H, T)
    rider = net.gather(['ffn2_w_gate', 'ffn2_w_up'])
    r, ret_in, states = _ret_fwd(proj, P['ret_gn_g'], tabs, H, T, rider=rider)
    net.done(rider)
    kpad = jnp.pad(W['conv_k'].astype(F32), ((0, HALO - CONV_WIDTH), (0, 0)))
    rider = net.gather(['ffn2_w_down'])
    u1 = _conv_fwd(proj, kpad, P['conv_b'], off_glu, CC, T, rider=rider)
    net.done(rider)

    def conv_ln(u1, g, b):
        xhat, rstd = _ln_stats(u1)
        u2 = xhat * g + b
        return [xhat, rstd, u2 * _sigmoid(u2)]

    xhc, rsc, u3 = _rows("conv_ln", conv_ln, [(u1, 'r', CC, 0), (P['conv_ln_g'], 'v', CC, 0), (P['conv_ln_b'], 'v', CC, 0)],
                         [('r', CC, F32), ('c', 1, F32), ('r', CC, BF16)], T=T, tb=512)
    (ret_out,) = _mm("ret_o", [ret_in], [W['w_ret_o']], [[(0, 0)]], ident, [F32], tm=1024, tn=1024, tk=2048)

    def epi_merge(accs, ret_out, gr, gc):
        conv_out = accs[0]
        return [conv_out, _sigmoid(gr) * ret_out + _sigmoid(gc) * conv_out]

    conv_out, merged = _mm("conv_o_merge", [u3], [W['w_conv_o']], [[(0, 0)]], epi_merge, [F32, BF16],
                           tm=512, tn=D, tk=1024, epi_rows=256,
                           extras=[(ret_out, 'mn', 0), (proj, 'mn', off_gate), (proj, 'mn', off_gate + D)])
    (z2,) = _mm("w_out", [merged], [W['w_out']], [[(0, 0)]], lambda accs, xr: [ALPHA * xr + accs[0]], [F32],
                tm=1024, tn=1024, tk=1024, extras=[(x1, 'mn', 0)])
    x2, x2b, xh2, rs2 = _ln_fwd("ln2", z2, P['ln2_g'], P['ln2_b'], T, D)
    a2, b2, s2, z3 = _ffn_fwd("ffn2", x2b, x2, W, FFN2, net)
    dz3, dz3h, g_ln3_g, g_ln3_b, loss = _ln_loss_bwd("ln3_loss", z3, P['ln3_g'], P['ln3_b'], tgt, T, D)

    S = {'ln3_g': g_ln3_g, 'ln3_b': g_ln3_b}
    dy2 = _ffn_bwd("ffn2b", dz3h, dz3, x2b, a2, b2, s2, W, FFN2, gdt, G, net, 'down')
    dz2, dz2b, S['ln2_g'], S['ln2_b'] = _ln_bwd("ln2b", dy2, xh2, rs2, P['ln2_g'], 1.0, T, D)

    (G['w_out'],) = _mm("d_w_out", [merged], [dz2b], [[(0, 0)]], ident, [gdt], ta=True, tm=1024, tn=1024, tk=1024)

    def epi_dmerge(accs, ret_out, conv_out, gr, gc):
        dm_ = accs[0]
        sr, sc = _sigmoid(gr), _sigmoid(gc)
        return [dm_ * sr, dm_ * sc, dm_ * ret_out * sr * (1.0 - sr), dm_ * conv_out * sc * (1.0 - sc)]

    dret_out, dconv_out, dgate_r, dgate_c = _mm(
        "d_merge", [dz2b], [W['w_out']], [[(0, 0)]], epi_dmerge, [BF16, BF16, BF16, BF16], tb=True,
        tm=512, tn=D, tk=1024, epi_rows=256,
        extras=[(ret_out, 'mn', 0), (conv_out, 'mn', 0), (proj, 'mn', off_gate), (proj, 'mn', off_gate + D)])
    (G['w_ret_o'],) = _mm("d_w_ret_o", [ret_in], [dret_out], [[(0, 0)]], ident, [gdt], ta=True, tm=1024, tn=1024, tk=1024)
    (G['w_conv_o'],) = _mm("d_w_conv_o", [u3], [dconv_out], [[(0, 0)]], ident, [gdt], ta=True, tm=1024, tn=1024, tk=1024)
    (dri,) = _mm("d_ret_in", [dret_out], [W['w_ret_o']], [[(0, 0)]], ident, [F32], tb=True, tm=1024, tn=1024, tk=1024)
    rider = net.exchange(['ffn2_w_gate', 'ffn2_w_up'])
    dproj, S['ret_gn_g'] = _ret_bwd(dri, r, states, proj, P['ret_gn_g'], tabs, H, T, proj.shape[1], rider=rider)
    net.done(rider)

    def epi_du2(accs, xhat, g, b):
        u2 = xhat * g + b
        return [accs[0] * _dsilu(u2, _sigmoid(u2))]

    (du2,) = _mm("d_u3", [dconv_out], [W['w_conv_o']], [[(0, 0)]], epi_du2, [F32], tb=True, tm=512, tn=CC, tk=1024, epi_rows=256,
                 extras=[(xhc, 'mn', 0), (P['conv_ln_g'], 'n', 0), (P['conv_ln_b'], 'n', 0)])

    def conv_ln_bwd(du2, xhat, rstd, g):
        du1 = _ln_bwd_math(du2, xhat, rstd, g)
        return [du1, _colsum(du2 * xhat), _colsum(du2), _colsum(du1)]

    du1, S['conv_ln_g'], S['conv_ln_b'], S['conv_b'] = _rows(
        "conv_ln_bwd", conv_ln_bwd, [(du2, 'r', CC, 0), (xhc, 'r', CC, 0), (rsc, 'r', 1, 0), (P['conv_ln_g'], 'v', CC, 0)],
        [('r', CC, F32), ('a', CC, F32), ('a', CC, F32), ('a', CC, F32)], T=T, tb=512)
    rider = net.exchange(['w_out', 'w_ret_o', 'w_conv_o'])
    dglu_a, dglu_b, dkpad = _conv_bwd(du1, proj, kpad, off_glu, CC, T, rider=rider)
    net.done(rider)
    G['conv_k'] = dkpad[:CONV_WIDTH].astype(gdt)

    for off, piece in ((off_glu, dglu_a), (off_glu + CC, dglu_b), (off_gate, dgate_r), (off_gate + D, dgate_c)):
        dproj = lax.dynamic_update_slice(dproj, piece, (0, off))
    IN_W = dproj.shape[1]
    G['w_in'], S['b_in'] = _mm("d_w_in", [x1b], [dproj], [[(0, 0)]], ident, [gdt], ta=True, o3=True, bsum=True,
                               tm=1024, tn=W['w_in'].shape[2], tk=1024)
    cuts = [0, (9 * D) // 16, (21 * D) // 32, (57 * D) // 64, D]
    w_in_rows = [{'w_in': (cuts[i], cuts[i + 1] - cuts[i], i == 3)} for i in range(4)]
    rider = net.exchange(['w_in', 'conv_k'], w_in_rows[0])
    (dy1,) = _mm("d_x1", [dproj], [W['w_in']], [[(0, 0)]], lambda accs, dzr: [ALPHA * dzr + accs[0]], [F32], tb=True,
                 b3=True, tm=1024, tn=1024, tk=0, extras=[(dz2, 'mn', 0)], rider=rider)
    net.done(rider)
    rider = net.exchange(['w_in'], w_in_rows[1])
    dz1, dz1h, S['ln1_g'], S['ln1_b'] = _ln_bwd("ln1b", dy1, xh1, rs1, P['ln1_g'], 0.5, T, D, rider=rider)
    net.done(rider)
    grad_x = _ffn_bwd("ffn1b", dz1h, dz1, xb, a1, b1, s1, W, FFN1, gdt, G, net, 'all',
                      pre=(lambda: net.exchange(['w_in'], w_in_rows[2]), lambda: net.exchange(['w_in'], w_in_rows[3])))
    return loss, grad_x, G, S


def _coords():
    return lax.axis_index("x"), lax.axis_index("y"), lax.axis_index("c")


def _flip(k, x, y, c):
    return (1 - x if k & 4 else x, 1 - y if k & 2 else y, 1 - c if k & 1 else c)


def _lin(p):
    return 4 * p[0] + 2 * p[1] + p[2]


class _Rider:
    def __init__(self, ins, out_shape, rows=None, fill=None):
        nb = len(ins)
        self.rows = rows or [None] * nb
        fill = fill or [None] * nb
        self.aliases = {nb + i: w for i, w in enumerate(w for w in range(nb) if fill[w] is not None)}
        self.ins = list(ins) + [f for f in fill if f is not None]
        self.out_shape, self.results = list(out_shape), None
        self.scratch = [pltpu.SemaphoreType.DMA((8 * nb,)), pltpu.SemaphoreType.DMA((8 * nb,)),
                        pltpu.SemaphoreType.DMA((nb,))]

    def span(self, w, ref, *slot, half=None):
        rows = self.rows[w]
        if half is not None:
            first, count = rows if rows is not None else (0, self.out_shape[w].shape[1])
            rows = (first + half * (count // 2), count // 2)
        if rows is None:
            return ref.at[slot] if slot else ref
        return ref.at[(*slot, pl.ds(*rows))]

    def begin(self, step, n_steps, ins, outs, sems):
        @pl.when(step == 0)
        def _():
            self.start(ins, outs, sems)

        @pl.when(step == min(n_steps - 1, (5 * n_steps) // 8))
        def _():
            self.relay(ins, outs, sems)

        @pl.when(step == n_steps - 1)
        def _():
            self.mid(ins, outs, sems)

    def end(self, step, n_steps, ins, outs, sems):
        @pl.when(step == n_steps - 1)
        def _():
            self.finish(ins, outs, sems)

    def relay(self, ins, outs, sems):
        pass

    def mid(self, ins, outs, sems):
        pass


class _GatherRider(_Rider):
    def __init__(self, blks, rows=None, fill=None):
        super().__init__(blks, [jax.ShapeDtypeStruct((N_DEV,) + b.shape, b.dtype) for b in blks], rows, fill)
        counts = [(r[1] if r is not None else b.shape[0]) for r, b in zip(self.rows, blks)]
        self.halves = [n % 32 == 0 for n in counts]

    def _copies(self, x_refs, out_refs, sems):
        nb = len(self.out_shape)
        send_sems, recv_sems, local_sems = sems
        x, y, c = _coords()
        me, sib = (x, y, c), (x, y, 1 - c)
        xn, yn, dg = _flip(4, x, y, c), _flip(2, x, y, c), _flip(6, x, y, c)
        plans = []
        for w in range(nb):
            own = self.span(w, x_refs[w])

            def copy(k, block, to, src=None, half=None, w=w):
                slot = self.span(w, out_refs[w], _lin(block), half=half)
                return pltpu.make_async_remote_copy(
                    src_ref=slot if src is None else src, dst_ref=slot, send_sem=send_sems.at[k * nb + w],
                    recv_sem=recv_sems.at[k * nb + w], device_id=to, device_id_type=MESH)

            mine = pltpu.make_async_copy(own, self.span(w, out_refs[w], _lin(me)), local_sems.at[w])
            first = [copy(0, me, sib, src=own), copy(1, me, xn, src=own), copy(2, me, yn, src=own)]
            if self.halves[w]:
                relay = [(copy(1, xn, me), [copy(3, xn, yn, half=0), copy(5, xn, sib)]),
                         (copy(2, yn, me), [copy(4, yn, xn, half=1), copy(6, yn, sib)])]
                last = [(copy(3, dg, me, half=0), []), (copy(4, dg, me, half=1), [copy(7, dg, sib)])]
            else:
                first.append(copy(3, me, dg, src=own))
                relay = [(copy(1, xn, me), [copy(5, xn, sib)]), (copy(2, yn, me), [copy(6, yn, sib)])]
                last = [(copy(3, dg, me), [copy(7, dg, sib)])]
            other = lambda p: (p[0], p[1], 1 - c)
            from_sib = [copy(0, sib, me), copy(5, other(xn), me), copy(6, other(yn), me), copy(7, other(dg), me)]
            plans.append((mine, first, relay, last, from_sib))
        return plans

    def start(self, ins, outs, sems):
        for mine, first, _, _, _ in self._copies(ins, outs, sems):
            for cp in [mine] + first:
                cp.start()

    def relay(self, ins, outs, sems):
        for _, _, relay, _, _ in self._copies(ins, outs, sems):
            for arrival, released in relay:
                arrival.wait_recv()
                for cp in released:
                    cp.start()

    def mid(self, ins, outs, sems):
        for _, _, _, last, _ in self._copies(ins, outs, sems):
            for arrival, released in last:
                arrival.wait_recv()
                for cp in released:
                    cp.start()

    def finish(self, ins, outs, sems):
        for mine, first, relay, last, from_sib in self._copies(ins, outs, sems):
            for cp in from_sib:
                cp.wait_recv()
            for cp in first + [cp for _, released in relay + last for cp in released]:
                cp.wait_send()
            mine.wait()


class _ExchangeRider(_Rider):
    def __init__(self, gs, rows=None, fill=None):
        super().__init__(gs, [jax.ShapeDtypeStruct(g.shape, g.dtype) for g in gs], rows, fill)

    def _copies(self, g_refs, out_refs, sems):
        nb = len(self.out_shape)
        send_sems, recv_sems, local_sems = sems
        x, y, c = _coords()
        me = _lin((x, y, c))

        def copy(k, w, landing):
            peer = _flip(k, x, y, c)
            src, dst = (me, _lin(peer)) if landing else (_lin(peer), me)
            return pltpu.make_async_remote_copy(
                src_ref=self.span(w, g_refs[w], src), dst_ref=self.span(w, out_refs[w], dst),
                send_sem=send_sems.at[(k - 1) * nb + w], recv_sem=recv_sems.at[(k - 1) * nb + w],
                device_id=peer, device_id_type=MESH)

        mines = [pltpu.make_async_copy(self.span(w, g_refs[w], me), self.span(w, out_refs[w], me), local_sems.at[w])
                 for w in range(nb)]
        sends = [copy(k, w, False) for w in range(nb) for k in range(1, N_DEV)]
        landings = [copy(k, w, True) for w in range(nb) for k in range(1, N_DEV)]
        return mines, sends, landings

    def start(self, ins, outs, sems):
        mines, sends, _ = self._copies(ins, outs, sems)
        for cp in mines + sends:
            cp.start()

    def finish(self, ins, outs, sems):
        mines, sends, landings = self._copies(ins, outs, sems)
        for cp in landings:
            cp.wait_recv()
        for cp in sends:
            cp.wait_send()
        for mine in mines:
            mine.wait()


def _run_rider(name, rider):
    n_in, n_out = len(rider.ins), len(rider.out_shape)

    def body(*refs):
        ride = (refs[:n_in], refs[n_in:n_in + n_out], refs[n_in + n_out:])
        rider.start(*ride)
        rider.relay(*ride)
        rider.mid(*ride)
        rider.finish(*ride)

    rider.results = pl.pallas_call(
        body, name=name, out_shape=rider.out_shape, in_specs=[ANY] * n_in, out_specs=[ANY] * n_out,
        scratch_shapes=rider.scratch, input_output_aliases=dict(rider.aliases),
        compiler_params=pltpu.CompilerParams(has_side_effects=True),
    )(*rider.ins)
    return rider.results


def _as_matrix(name, g):
    if name == 'w_in':
        return g
    if name in COL_SHARDED:
        return jnp.transpose(g, (1, 0, 2)).reshape(g.shape[1], N_DEV * g.shape[2])
    return g.reshape(N_DEV * g.shape[1], g.shape[2])


def _by_owner(name, g):
    if name == 'w_in':
        return g
    if name in COL_SHARDED:
        return jnp.transpose(g.reshape(g.shape[0], N_DEV, g.shape[1] // N_DEV), (1, 0, 2))
    return g.reshape(N_DEV, g.shape[0] // N_DEV, g.shape[1])


class _Comm:
    def __init__(self, shards):
        self.shards, self.W, self.parts, self.partial, self.sent = shards, {}, {}, {}, {}

    def _ride(self, cls, names, srcs, part, sink):
        part = part or {}
        rider = cls(srcs, rows=[part[n][:2] if n in part else None for n in names],
                    fill=[self.partial.pop((sink, n), None) for n in names])
        rider.names, rider.sink = names, sink
        rider.unfinished = {n for n in names if n in part and not part[n][2]}
        return rider

    def gather(self, names, part=None):
        return self._ride(_GatherRider, names, [self.shards[n] for n in names], part, 'W')

    def exchange(self, names, G, part=None):
        for n in names:
            if n not in self.sent:
                self.sent[n] = _by_owner(n, G[n])
        return self._ride(_ExchangeRider, names, [self.sent[n] for n in names], part, 'parts')

    def collect(self, rider):
        for n, res in zip(rider.names, rider.results):
            if n in rider.unfinished:
                self.partial[(rider.sink, n)] = res
            elif rider.sink == 'W':
                self.W[n] = _as_matrix(n, res)
            else:
                self.parts[n] = res


def _adamw_math(p_ref, w_ref, m_ref, v_ref, g_ref, d_ref, nm_ref, nv_ref):
    c1 = 1.0 - ADAM_B1 ** ADAM_STEP
    c2 = 1.0 - ADAM_B2 ** ADAM_STEP
    g = p_ref[0].astype(F32)
    for s in range(1, p_ref.shape[0]):
        g = g + p_ref[s].astype(F32)
    nm = ADAM_B1 * m_ref[...] + (1.0 - ADAM_B1) * g
    nv = ADAM_B2 * v_ref[...] + (1.0 - ADAM_B2) * (g * g)
    g_ref[...] = g
    nm_ref[...] = nm
    nv_ref[...] = nv
    d_ref[...] = -ADAM_LR * ((nm / c1) / (jnp.sqrt(nv / c2) + ADAM_EPS) + ADAM_WD * w_ref[...])


def _adamw_vectors(parts, ws, ms, vs, loss_parts):
    k = len(ws)

    def body(*refs):
        for i in range(k):
            _adamw_math(refs[i], refs[k + i], refs[2 * k + i], refs[3 * k + i], *refs[4 * k + 1 + 4 * i:4 * k + 5 + 4 * i])
        lp, lo = refs[4 * k], refs[8 * k + 1]
        lo[...] = functools.reduce(jnp.add, [lp[s] for s in range(lp.shape[0])])

    return pl.pallas_call(
        body, name="adamw_vectors",
        out_shape=[jax.ShapeDtypeStruct(w.shape, F32) for w in ws for _ in range(4)] + [jax.ShapeDtypeStruct((1, 128), F32)],
        compiler_params=_params(),
    )(*parts, *ws, *ms, *vs, loss_parts)


def _adamw(name, parts, w, m, v, tb):
    n, R, Wd = parts.shape
    assert R % tb == 0
    body = functools.partial(_adamw_math)

    row = pl.BlockSpec((tb, Wd), lambda i: (i, 0))
    return pl.pallas_call(
        body, name=name, grid=(R // tb,),
        in_specs=[pl.BlockSpec((n, tb, Wd), lambda i: (0, i, 0)), row, row, row],
        out_specs=[row, row, row, row], out_shape=[jax.ShapeDtypeStruct((R, Wd), F32)] * 4,
        compiler_params=_params(("arbitrary",)),
    )(parts, w, m, v)


def _row_tile(R, unit, cap):
    best = unit
    for t in range(unit, cap + 1, unit):
        if R % t == 0:
            best = t
    return best


def kernel(x, ffn1_w_gate, ffn1_w_up, ffn1_w_down, ln1_g, ln1_b, w_in, b_in, ret_gn_g, conv_k, conv_b, conv_ln_g, conv_ln_b, w_ret_o, w_conv_o, w_out, ln2_g, ln2_b, ffn2_w_gate, ffn2_w_up, ffn2_w_down, ln3_g, ln3_b, loss_target, m_ffn1_w_gate, m_ffn1_w_up, m_ffn1_w_down, m_ln1_g, m_ln1_b, m_w_in, m_b_in, m_ret_gn_g, m_conv_k, m_conv_b, m_conv_ln_g, m_conv_ln_b, m_w_ret_o, m_w_conv_o, m_w_out, m_ln2_g, m_ln2_b, m_ffn2_w_gate, m_ffn2_w_up, m_ffn2_w_down, m_ln3_g, m_ln3_b, v_ffn1_w_gate, v_ffn1_w_up, v_ffn1_w_down, v_ln1_g, v_ln1_b, v_w_in, v_b_in, v_ret_gn_g, v_conv_k, v_conv_b, v_conv_ln_g, v_conv_ln_b, v_w_ret_o, v_w_conv_o, v_w_out, v_ln2_g, v_ln2_b, v_ffn2_w_gate, v_ffn2_w_up, v_ffn2_w_down, v_ln3_g, v_ln3_b):
    given = dict(locals())
    wts = {n: given[n] for n in WEIGHTS}
    mom = {n: given['m_' + n] for n in WEIGHTS}
    var = {n: given['v_' + n] for n in WEIGHTS}

    def shard2d(a):
        return a.reshape(a.shape[-3] * a.shape[-2] if a.ndim == 4 else a.shape[-2], a.shape[-1])

    comm = _Comm({n: shard2d(wts[n]).astype(BF16) for n in BIG})
    P = {n: wts[n].reshape(1, -1) for n in SMALL}
    loss, grad_x, _, S = _local_step(x[0], loss_target[0], None, P, comm=comm)

    parts = comm.parts
    res = {}
    for n in BIG:
        rows, cols = parts[n].shape[1:]
        tb = rows if rows % 16 else _row_tile(rows, 16, max(16, (256 * 1024) // cols))
        res[n] = _adamw("adamw_" + n, parts[n], shard2d(wts[n]), shard2d(mom[n]), shard2d(var[n]), tb)

    vec_parts = _run_rider("gather_vector_grads", _GatherRider([S[n] for n in SMALL] + [loss]))
    vec = _adamw_vectors(vec_parts[:-1], [P[n] for n in SMALL], [mom[n].reshape(1, -1) for n in SMALL],
                         [var[n].reshape(1, -1) for n in SMALL], vec_parts[-1])
    for i, n in enumerate(SMALL):
        res[n] = vec[4 * i:4 * i + 4]

    outs = [vec[-1][0, 0], grad_x[None]]
    for k in range(4):
        for n in WEIGHTS:
            outs.append(res[n][k].reshape(wts[n].shape))
    return tuple(outs)
```

```python
import functools
import math

import jax
import jax.numpy as jnp
from jax import lax
from jax.experimental import pallas as pl
from jax.experimental.pallas import tpu as pltpu

F32 = jnp.float32
BF16 = jnp.bfloat16

N_DEV = 8
LN_EPS = 1e-5
ALPHA = 2.0 ** 0.25
RET_DK = 128
RET_DV = 256
RET_CHUNK = 256
ROPE_BASE = 10000.0
CONV_WIDTH = 31
HALO = 32
ADAM_LR, ADAM_B1, ADAM_B2, ADAM_EPS, ADAM_WD, ADAM_STEP = 0.001, 0.9, 0.999, 1e-08, 0.01, 10
VMEM_LIMIT = 52 * 1024 * 1024
MESH = pl.DeviceIdType.MESH
ANY = pl.BlockSpec(memory_space=pl.ANY)

BIG = ['ffn1_w_gate', 'ffn1_w_up', 'ffn1_w_down', 'w_in', 'w_ret_o', 'w_conv_o', 'w_out',
       'ffn2_w_gate', 'ffn2_w_up', 'ffn2_w_down', 'conv_k']
COL_SHARDED = {'ffn1_w_gate', 'ffn1_w_up', 'w_in', 'ffn2_w_gate', 'ffn2_w_up', 'conv_k'}
SMALL = ['ln1_g', 'ln1_b', 'b_in', 'ret_gn_g', 'conv_b', 'conv_ln_g', 'conv_ln_b', 'ln2_g', 'ln2_b', 'ln3_g', 'ln3_b']
WEIGHTS = ['ffn1_w_gate', 'ffn1_w_up', 'ffn1_w_down', 'ln1_g', 'ln1_b', 'w_in', 'b_in', 'ret_gn_g', 'conv_k', 'conv_b',
           'conv_ln_g', 'conv_ln_b', 'w_ret_o', 'w_conv_o', 'w_out', 'ln2_g', 'ln2_b', 'ffn2_w_gate', 'ffn2_w_up',
           'ffn2_w_down', 'ln3_g', 'ln3_b']


def _params(sem=None):
    return pltpu.CompilerParams(dimension_semantics=sem, vmem_limit_bytes=VMEM_LIMIT)


def _sigmoid(x):
    return jax.nn.sigmoid(x)


def _dsilu(x, sg):
    return sg * (1.0 + x * (1.0 - sg))


def _fit(dim, want):
    if dim <= want:
        return dim
    return max(t for t in range(128, want + 1, 128) if dim % t == 0)


def _dot(a, b, ta=False, tb=False):
    dn = (((0,) if ta else (1,), (1,) if tb else (0,)), ((), ()))
    return lax.dot_general(a, b, dn, preferred_element_type=F32)


def _mm(name, As, Bs, prods, epi, out_dtypes, *, ta=False, tb=False, tm, tn, tk, extras=(), i_outer=True,
        b3=False, o3=False, rider=None, bsum=False, epi_rows=0):
    a0, b0 = As[0], Bs[0]
    M, K = (a0.shape[1], a0.shape[0]) if ta else a0.shape
    if b3:
        S, rows, cs = b0.shape
        N = rows if tb else S * cs
        assert K == (S * cs if tb else rows)
        tn, tk = (tn, cs) if tb else (cs, tk)
    else:
        N = b0.shape[0] if tb else b0.shape[1]
    tm, tn, tk = _fit(M, tm), _fit(N, tn), _fit(K, tk)
    assert M % tm == 0 and N % tn == 0 and K % tk == 0, (name, M, N, K, tm, tn, tk)
    gi, gj, gk = M // tm, N // tn, K // tk
    grid = (gi, gj, gk) if i_outer else (gj, gi, gk)

    def ij(g0, g1):
        return (g0, g1) if i_outer else (g1, g0)

    def amap(g0, g1, k):
        i, _ = ij(g0, g1)
        return (k, i) if ta else (i, k)

    def bmap(g0, g1, k):
        _, j = ij(g0, g1)
        return (j, k) if tb else (k, j)

    def bmap3(g0, g1, k):
        _, j = ij(g0, g1)
        return (k, j, 0) if tb else (j, k, 0)

    in_specs = [pl.BlockSpec((tk, tm) if ta else (tm, tk), amap) for _ in As]
    if b3:
        in_specs += [pl.BlockSpec((None, tn, tk) if tb else (None, tk, tn), bmap3) for _ in Bs]
    else:
        in_specs += [pl.BlockSpec((tn, tk) if tb else (tk, tn), bmap) for _ in Bs]
    args = list(As) + list(Bs)
    for arr, kind, coloff in extras:
        assert coloff % tn == 0
        off = coloff // tn
        if kind == 'mn':
            in_specs.append(pl.BlockSpec((tm, tn), lambda g0, g1, k, off=off: (ij(g0, g1)[0], ij(g0, g1)[1] + off)))
        else:
            in_specs.append(pl.BlockSpec((1, tn), lambda g0, g1, k, off=off: (0, ij(g0, g1)[1] + off)))
        args.append(arr)
    if o3:
        out_shape = [jax.ShapeDtypeStruct((gj, M, tn), dt) for dt in out_dtypes]
        out_specs = [pl.BlockSpec((None, tm, tn), lambda g0, g1, k: (ij(g0, g1)[1], ij(g0, g1)[0], 0))
                     for _ in out_dtypes]
    else:
        out_shape = [jax.ShapeDtypeStruct((M, N), dt) for dt in out_dtypes]
        out_specs = [pl.BlockSpec((tm, tn), lambda g0, g1, k: ij(g0, g1)) for _ in out_dtypes]
    if bsum:
        assert gi == 1 and not tb and not b3
        out_shape.append(jax.ShapeDtypeStruct((1, N), F32))
        out_specs.append(pl.BlockSpec((1, tn), lambda g0, g1, k: (0, ij(g0, g1)[1])))
    n_a, n_b, n_e, n_o = len(As), len(Bs), len(extras), len(out_shape)
    n_p = len(prods) if gk > 1 else 0
    scratch = [pltpu.VMEM((tm, tn), F32) for _ in range(n_p)]
    if rider is not None:
        in_specs, out_specs = in_specs + [ANY] * len(rider.ins), out_specs + [ANY] * len(rider.out_shape)
        args, out_shape, scratch = args + rider.ins, out_shape + rider.out_shape, scratch + rider.scratch
    n_in, n_out = len(args), len(out_shape)

    def body(*refs):
        a_refs = refs[:n_a]
        b_refs = refs[n_a:n_a + n_b]
        e_refs = refs[n_a + n_b:n_a + n_b + n_e]
        o_refs = refs[n_in:n_in + n_o]
        acc_refs = refs[n_in + n_out:n_in + n_out + n_p]
        k = pl.program_id(2)
        if rider is not None:
            step = (pl.program_id(0) * grid[1] + pl.program_id(1)) * gk + k
            ride = (step, grid[0] * grid[1] * gk, refs[n_a + n_b + n_e:n_in], refs[n_in + n_o:n_in + n_out],
                    refs[n_in + n_out + n_p:])
            rider.begin(*ride)

        def finish(accs, rows=slice(None)):
            ex = [(e[rows, :] if kind == 'mn' else e[...]).astype(F32) for e, (_, kind, _) in zip(e_refs, extras)]
            for o, r in zip(o_refs, epi(accs, *ex)):
                o[rows, :] = r.astype(o.dtype)

        if bsum:
            @pl.when(k == 0)
            def _():
                o_refs[-1][...] = jnp.zeros_like(o_refs[-1])

            o_refs[-1][...] += _colsum(b_refs[0][...].astype(F32))

        if gk == 1:
            sub = tm if ta or not epi_rows else _fit(tm, epi_rows)
            for r0 in range(0, tm, sub):
                rows = slice(None) if ta else slice(r0, r0 + sub)
                finish([functools.reduce(jnp.add, [_dot(a_refs[ai][...] if ta else a_refs[ai][rows, :],
                                                        b_refs[bi][...], ta, tb) for ai, bi in terms])
                        for terms in prods], rows)
        else:
            @pl.when(k == 0)
            def _():
                for acc in acc_refs:
                    acc[...] = jnp.zeros_like(acc)

            for p, terms in enumerate(prods):
                for ai, bi in terms:
                    acc_refs[p][...] += _dot(a_refs[ai][...], b_refs[bi][...], ta, tb)

            @pl.when(k == gk - 1)
            def _():
                finish([acc[...] for acc in acc_refs])

        if rider is not None:
            rider.end(*ride)

    aliases = {} if rider is None else {n_a + n_b + n_e + p: n_o + o for p, o in rider.aliases.items()}
    res = pl.pallas_call(
        body, name=name, grid=grid, in_specs=in_specs, out_specs=out_specs, out_shape=out_shape,
        scratch_shapes=scratch, input_output_aliases=aliases,
        compiler_params=_params(("arbitrary", "arbitrary", "arbitrary")),
    )(*args)
    if rider is not None:
        rider.results = res[n_o:]
    return res[:n_o]


def _rows(name, fn, ins, outs, *, T, tb, rider=None):
    tb = min(tb, T)
    assert T % tb == 0
    in_specs, args = [], []
    for arr, kind, width, cb in ins:
        if kind == 'r':
            in_specs.append(pl.BlockSpec((tb, width), lambda i, _, cb=cb: (i, cb)))
        else:
            in_specs.append(pl.BlockSpec((1, width), lambda i, _, cb=cb: (0, cb)))
        args.append(arr)
    out_shape, out_specs = [], []
    for kind, width, dtype in outs:
        if kind == 'r':
            out_shape.append(jax.ShapeDtypeStruct((T, width), dtype))
            out_specs.append(pl.BlockSpec((tb, width), lambda i, _: (i, 0)))
        elif kind == 'c':
            out_shape.append(jax.ShapeDtypeStruct((T, 1), dtype))
            out_specs.append(pl.BlockSpec((tb, 1), lambda i, _: (i, 0)))
        else:
            out_shape.append(jax.ShapeDtypeStruct((1, width), F32))
            out_specs.append(pl.BlockSpec((1, width), lambda i, _: (0, 0)))
    n_in = len(ins)

    def body(*refs):
        i = pl.program_id(0)
        vals = fn(*[r[...] for r in refs[:n_in]])
        for (kind, _, _), o, v in zip(outs, refs[n_in:], vals):
            if kind == 'a':
                @pl.when(i == 0)
                def _(o=o):
                    o[...] = jnp.zeros_like(o)

                o[...] += v
            else:
                o[...] = v.astype(o.dtype)

    return _hosted_call(body, rider, name=name, grid=(T // tb, 1), in_specs=in_specs, out_specs=out_specs,
                        out_shape=out_shape, scratch=[], args=args)


def _colsum(v):
    return jnp.sum(v, axis=0, keepdims=True)


def _ln_stats(z):
    mu = jnp.mean(z, axis=-1, keepdims=True)
    d = z - mu
    var = jnp.mean(d * d, axis=-1, keepdims=True)
    rstd = lax.rsqrt(var + LN_EPS)
    return d * rstd, rstd


def _ln_bwd_math(dy, xhat, rstd, g):
    dxh = dy * g
    m1 = jnp.mean(dxh, axis=-1, keepdims=True)
    m2 = jnp.mean(dxh * xhat, axis=-1, keepdims=True)
    return rstd * (dxh - m1 - xhat * m2)


def _ln_fwd(name, z, g, b, T, D, rider=None):
    def fn(z, g, b):
        xhat, rstd = _ln_stats(z)
        y = xhat * g + b
        return [y, y, xhat, rstd]

    return _rows(name, fn, [(z, 'r', D, 0), (g, 'v', D, 0), (b, 'v', D, 0)],
                 [('r', D, F32), ('r', D, BF16), ('r', D, F32), ('c', 1, F32)], T=T, tb=512, rider=rider)


def _ln_bwd(name, dy, xhat, rstd, g, scale, T, D, rider=None):
    def fn(dy, xhat, rstd, g):
        dz = _ln_bwd_math(dy, xhat, rstd, g)
        return [dz, dz * scale, _colsum(dy * xhat), _colsum(dy)]

    return _rows(name, fn, [(dy, 'r', D, 0), (xhat, 'r', D, 0), (rstd, 'r', 1, 0), (g, 'v', D, 0)],
                 [('r', D, F32), ('r', D, BF16), ('a', D, F32), ('a', D, F32)], T=T, tb=512, rider=rider)


def _ln_loss_bwd(name, z, g, b, tgt, T, D):
    def fn(z, g, b, tgt):
        xhat, rstd = _ln_stats(z)
        err = xhat * g + b - tgt
        row_loss = 0.5 * jnp.mean(err * err, axis=-1, keepdims=True)
        loss = jnp.broadcast_to(jnp.sum(row_loss, axis=0, keepdims=True), (1, 128))
        dy = err * (1.0 / D)
        dz = _ln_bwd_math(dy, xhat, rstd, g)
        return [dz, dz * 0.5, _colsum(dy * xhat), _colsum(dy), loss]

    return _rows(name, fn, [(z, 'r', D, 0), (g, 'v', D, 0), (b, 'v', D, 0), (tgt, 'r', D, 0)],
                 [('r', D, F32), ('r', D, BF16), ('a', D, F32), ('a', D, F32), ('a', 128, F32)], T=T, tb=512)


class _Net:
    def __init__(self, comm, G):
        self.comm, self.G = comm, G

    def gather(self, names, part=None):
        return self.comm.gather(names, part) if self.comm else None

    def exchange(self, names, part=None):
        return self.comm.exchange(names, self.G, part) if self.comm else None

    def to_sibling(self, names):
        return self.comm.to_sibling(names, self.G) if self.comm else None

    def pairsum(self, names):
        if self.comm:
            self.comm.pairsum(names)

    def to_owner(self, names, part=None):
        return self.comm.to_owner(names, part) if self.comm else None

    def done(self, rider):
        if rider is not None:
            self.comm.collect(rider)


def _ffn_fwd(tag, xb, x, W, names, net, rider=None, rider_down=None):
    def epi_gu(accs):
        a, b = accs
        return [a, b, a * _sigmoid(a) * b]

    ng, nu, nd = names
    a, b, s = _mm(tag + "_gate_up", [xb], [W[ng], W[nu]], [[(0, 0)], [(0, 1)]], epi_gu, [BF16, BF16, BF16],
                  tm=1024, tn=1408, tk=1024, rider=rider, epi_rows=256)
    net.done(rider)

    def epi_down(accs, xres):
        return [ALPHA * xres + 0.5 * accs[0]]

    rider_down = rider_down() if rider_down else None
    (z,) = _mm(tag + "_down", [s], [W[nd]], [[(0, 0)]], epi_down, [F32], tm=1024, tn=1024, tk=1408,
               extras=[(x, 'mn', 0)], rider=rider_down)
    net.done(rider_down)
    return a, b, s, z


def _ffn_bwd(tag, dzh, dz, xb, a, b, s, W, names, gdt, G, net, ride, pre=(None, None)):
    ng, nu, nd = names

    def epi_ds(accs, a, b):
        ds = accs[0]
        sg = _sigmoid(a)
        return [ds * b * _dsilu(a, sg), ds * a * sg]

    rider = pre[0]() if pre[0] else None
    da, db = _mm(tag + "_ds", [dzh], [W[nd]], [[(0, 0)]], epi_ds, [BF16, BF16], tb=True, tm=1024, tn=1408, tk=1024, epi_rows=256,
                 extras=[(a, 'mn', 0), (b, 'mn', 0)], rider=rider)
    net.done(rider)
    ident = lambda accs: accs
    rider = pre[1]() if pre[1] else None
    (G[nd],) = _mm(tag + "_dwd", [s], [dzh], [[(0, 0)]], ident, [gdt], ta=True, tm=1408, tn=1024, tk=1024,
                   rider=rider)
    net.done(rider)
    if ride == 'all':
        rider = net.to_sibling([nd])
        (G[ng],) = _mm(tag + "_dwg", [xb], [da], [[(0, 0)]], ident, [gdt], ta=True, tm=1024, tn=1408, tk=1024,
                       rider=rider)
        net.done(rider)
        net.pairsum([nd])
        rider = net.to_owner([nd])
        (G[nu],) = _mm(tag + "_dwu", [xb], [db], [[(0, 0)]], ident, [gdt], ta=True, tm=1024, tn=1408, tk=1024,
                       rider=rider)
        net.done(rider)
        rider = net.to_sibling([ng, nu])
        if rider is not None:
            _run_rider(tag + "_to_sibling", rider)
            net.done(rider)
        net.pairsum([ng, nu])
        rider = net.to_owner([ng, nu])
    else:
        rider = net.to_sibling([nd]) if ride else None
        G[ng], G[nu] = _mm(tag + "_dwgu", [xb], [da, db], [[(0, 0)], [(0, 1)]], ident, [gdt, gdt], ta=True,
                           tm=1024, tn=1408, tk=1024, rider=rider)
        net.done(rider)
        if ride:
            net.pairsum([nd])
        rider = net.to_owner([nd]) if ride else None

    def epi_dx(accs, dzres):
        return [ALPHA * dzres + accs[0]]

    (dx,) = _mm(tag + "_dx", [da, db], [W[ng], W[nu]], [[(0, 0), (1, 1)]], epi_dx, [F32], tb=True,
                tm=1024, tn=1024, tk=1408, extras=[(dz, 'mn', 0)], rider=rider)
    net.done(rider)
    return dx


def _ret_tables(H, T):
    C = RET_CHUNK
    log_g = jnp.log(1.0 - jnp.exp2(-5.0 - jnp.arange(H, dtype=F32)))
    idx = jnp.arange(C, dtype=F32)
    diff = idx[:, None] - idx[None, :]
    dm = jnp.where(diff[None] >= 0, jnp.exp(jnp.maximum(diff, 0.0)[None] * log_g[:, None, None]), 0.0)
    xi = jnp.exp((idx[None, :] + 1.0) * log_g[:, None])[:, :, None]
    zeta = jnp.exp((C - 1.0 - idx)[None, :] * log_g[:, None])[:, :, None]
    gc = jnp.broadcast_to(jnp.exp(C * log_g)[:, None, None], (H, 1, RET_DV))
    half = RET_DK // 2
    freqs = ROPE_BASE ** (-jnp.arange(half, dtype=F32) / half)
    ang = jnp.arange(T, dtype=F32)[:, None] * freqs[None, :]
    cos, sin = jnp.cos(ang), jnp.sin(ang)
    cosf = jnp.concatenate([cos, cos], axis=1)
    sins = jnp.concatenate([-sin, sin], axis=1)
    return dm, xi, zeta, gc, cosf, sins


def _rot(x, cosf, sins):
    return x * cosf + pltpu.roll(x, RET_DK // 2, 1) * sins


def _rot_bwd(dy, cosf, sins):
    return dy * cosf + pltpu.roll(dy * sins, RET_DK // 2, 1)


RET_HB = 8


def _ret_specs(H, HB, rev, NC):
    C, G = RET_CHUNK, H // HB
    nn = (lambda n: NC - 1 - n) if rev else (lambda n: n)
    return [
        pl.BlockSpec((C, HB * RET_DK), lambda h, n: (nn(n), h)),
        pl.BlockSpec((C, HB * RET_DK), lambda h, n: (nn(n), G + h)),
        pl.BlockSpec((C, HB * RET_DV), lambda h, n: (nn(n), G + h)),
        pl.BlockSpec((C, HB * RET_DV), lambda h, n: (nn(n), 2 * G + h)),
        pl.BlockSpec((C, RET_DK), lambda h, n: (nn(n), 0)),
        pl.BlockSpec((C, RET_DK), lambda h, n: (nn(n), 0)),
        pl.BlockSpec((1, HB * RET_DV), lambda h, n: (0, h)),
        pl.BlockSpec((HB, C, C), lambda h, n: (h, 0, 0)),
        pl.BlockSpec((HB, C, 1), lambda h, n: (h, 0, 0)),
        pl.BlockSpec((HB, C, 1), lambda h, n: (h, 0, 0)),
        pl.BlockSpec((HB, 1, RET_DV), lambda h, n: (h, 0, 0)),
    ]


def _ret_fwd(proj, gn_g, tabs, H, T, rider=None):
    C, NC = RET_CHUNK, T // RET_CHUNK
    HB = min(RET_HB, H)
    dm, xi, zeta, gc, cosf, sins = tabs
    scale = RET_DK ** -0.5

    def body(q_ref, k_ref, v_ref, g_ref, cos_ref, sin_ref, gn_ref, dm_ref, xi_ref, zt_ref, gc_ref,
             r_ref, ri_ref, st_ref, state):
        @pl.when(pl.program_id(1) == 0)
        def _():
            state[...] = jnp.zeros_like(state)

        cs, sn = cos_ref[...], sin_ref[...]
        hs = range(HB)
        qk = [slice(h * RET_DK, (h + 1) * RET_DK) for h in hs]
        vv = [slice(h * RET_DV, (h + 1) * RET_DV) for h in hs]
        kr = [_rot(k_ref[:, qk[h]].astype(F32), cs, sn) for h in hs]
        qb = [(_rot(q_ref[:, qk[h]].astype(F32), cs, sn) * scale).astype(BF16) for h in hs]
        kb = [kr[h].astype(BF16) for h in hs]
        kzb = [(kr[h] * zt_ref[h]).astype(BF16) for h in hs]
        vb = [v_ref[:, vv[h]].astype(BF16) for h in hs]
        st = [state[h] for h in hs]
        stb = [st[h].astype(BF16) for h in hs]
        sb = [(_dot(qb[h], kb[h], tb=True) * dm_ref[h]).astype(BF16) for h in hs]
        cross = [_dot(qb[h], stb[h]) for h in hs]
        kv = [_dot(kzb[h], vb[h], ta=True) for h in hs]
        intra = [_dot(sb[h], vb[h]) for h in hs]
        for h in hs:
            st_ref[h] = stb[h]
            state[h] = gc_ref[h] * st[h] + kv[h]
        for h in hs:
            r = intra[h] + cross[h] * xi_ref[h]
            rhat, _ = _ln_stats(r)
            g = g_ref[:, vv[h]].astype(F32)
            r_ref[:, vv[h]] = r
            ri_ref[:, vv[h]] = (g * _sigmoid(g) * (rhat * gn_ref[:, vv[h]])).astype(BF16)

    VW = H * RET_DV
    return _hosted_call(
        body, rider, name="ret_fwd", grid=(H // HB, NC), in_specs=_ret_specs(H, HB, False, NC),
        out_specs=[pl.BlockSpec((C, HB * RET_DV), lambda h, n: (n, h)),
                   pl.BlockSpec((C, HB * RET_DV), lambda h, n: (n, h)),
                   pl.BlockSpec((HB, None, RET_DK, RET_DV), lambda h, n: (h, n, 0, 0))],
        out_shape=[jax.ShapeDtypeStruct((T, VW), F32), jax.ShapeDtypeStruct((T, VW), BF16),
                   jax.ShapeDtypeStruct((H, NC, RET_DK, RET_DV), BF16)],
        scratch=[pltpu.VMEM((HB, RET_DK, RET_DV), F32)],
        args=[proj, proj, proj, proj, cosf, sins, gn_g, dm, xi, zeta, gc])


def _hosted_call(body, rider, *, name, grid, in_specs, out_specs, out_shape, scratch, args):
    n_in, n_out, n_scr = len(args), len(out_shape), len(scratch)
    if rider is None:
        hosted = body
    else:
        n_ri, n_ro = len(rider.ins), len(rider.out_shape)
        in_specs, out_specs = in_specs + [ANY] * n_ri, out_specs + [ANY] * n_ro
        args, out_shape, scratch = args + rider.ins, out_shape + rider.out_shape, scratch + rider.scratch

        def hosted(*refs):
            o0, s0 = n_in + n_ri, n_in + n_ri + n_out + n_ro
            step = pl.program_id(0) * grid[1] + pl.program_id(1)
            ride = (step, grid[0] * grid[1], refs[n_in:o0], refs[o0 + n_out:s0], refs[s0 + n_scr:])
            rider.begin(*ride)
            body(*refs[:n_in], *refs[o0:o0 + n_out], *refs[s0:s0 + n_scr])
            rider.end(*ride)

    aliases = {} if rider is None else {n_in + p: n_out + o for p, o in rider.aliases.items()}
    res = pl.pallas_call(
        hosted, name=name, grid=grid, in_specs=in_specs, out_specs=out_specs, out_shape=out_shape,
        scratch_shapes=scratch, input_output_aliases=aliases, compiler_params=_params(("arbitrary", "arbitrary")),
    )(*args)
    if rider is not None:
        rider.results = res[n_out:]
    return res[:n_out]


def _ret_bwd(dri, r, states, proj, gn_g, tabs, H, T, in_w, rider=None):
    C, NC = RET_CHUNK, T // RET_CHUNK
    HB = min(RET_HB, H)
    dm, xi, zeta, gc, cosf, sins = tabs
    scale = RET_DK ** -0.5

    def body(q_ref, k_ref, v_ref, g_ref, cos_ref, sin_ref, gn_ref, dm_ref, xi_ref, zt_ref, gc_ref,
             dri_ref, r_ref, st_ref, dp_ref, dgn_ref, dstate):
        @pl.when(pl.program_id(1) == 0)
        def _():
            dstate[...] = jnp.zeros_like(dstate)
            dgn_ref[...] = jnp.zeros_like(dgn_ref)

        cs, sn = cos_ref[...], sin_ref[...]
        hs = range(HB)
        qk = [slice(h * RET_DK, (h + 1) * RET_DK) for h in hs]
        vv = [slice(h * RET_DV, (h + 1) * RET_DV) for h in hs]
        qr = [_rot(q_ref[:, qk[h]].astype(F32), cs, sn) * scale for h in hs]
        kr = [_rot(k_ref[:, qk[h]].astype(F32), cs, sn) for h in hs]
        qb = [qr[h].astype(BF16) for h in hs]
        kb = [kr[h].astype(BF16) for h in hs]
        vb = [v_ref[:, vv[h]].astype(BF16) for h in hs]
        qxb = [(qr[h] * xi_ref[h]).astype(BF16) for h in hs]
        kzb = [(kr[h] * zt_ref[h]).astype(BF16) for h in hs]
        drb = []
        for h in hs:
            rhat, rstd = _ln_stats(r_ref[:, vv[h]])
            g, gn, dpre = g_ref[:, vv[h]].astype(F32), gn_ref[:, vv[h]], dri_ref[:, vv[h]]
            sg = _sigmoid(g)
            dp_ref[:, 2 * QW + VW + h * RET_DV:2 * QW + VW + (h + 1) * RET_DV] = (
                dpre * (rhat * gn) * _dsilu(g, sg)).astype(BF16)
            drn = dpre * (g * sg)
            dgn_ref[:, vv[h]] += _colsum(drn * rhat)
            drb.append(_ln_bwd_math(drn, rhat, rstd, gn).astype(BF16))
        ds1 = [dstate[h] for h in hs]
        ds1b = [ds1[h].astype(BF16) for h in hs]
        sb = [(_dot(qb[h], kb[h], tb=True) * dm_ref[h]).astype(BF16) for h in hs]
        dsb = [(_dot(drb[h], vb[h], tb=True) * dm_ref[h]).astype(BF16) for h in hs]
        dq_x = [_dot(drb[h], st_ref[h], tb=True) for h in hs]
        dk_x = [_dot(vb[h], ds1b[h], tb=True) for h in hs]
        dv_x = [_dot(kzb[h], ds1b[h]) for h in hs]
        dst = [_dot(qxb[h], drb[h], ta=True) for h in hs]
        for h in hs:
            dstate[h] = gc_ref[h] * ds1[h] + dst[h]
        dv_i = [_dot(sb[h], drb[h], ta=True) for h in hs]
        dq_i = [_dot(dsb[h], kb[h]) for h in hs]
        dk_i = [_dot(dsb[h], qb[h], ta=True) for h in hs]
        for h in hs:
            dp_ref[:, 2 * QW + h * RET_DV:2 * QW + (h + 1) * RET_DV] = (dv_i[h] + dv_x[h]).astype(BF16)
            dq = dq_i[h] + dq_x[h] * xi_ref[h]
            dk = dk_i[h] + dk_x[h] * zt_ref[h]
            dp_ref[:, qk[h]] = _rot_bwd(dq * scale, cs, sn).astype(BF16)
            dp_ref[:, QW + h * RET_DK:QW + (h + 1) * RET_DK] = _rot_bwd(dk, cs, sn).astype(BF16)

    VW, QW = H * RET_DV, H * RET_DK
    rv = lambda n: NC - 1 - n
    in_specs = _ret_specs(H, HB, True, NC) + [
        pl.BlockSpec((C, HB * RET_DV), lambda h, n: (rv(n), h)),
        pl.BlockSpec((C, HB * RET_DV), lambda h, n: (rv(n), h)),
        pl.BlockSpec((HB, None, RET_DK, RET_DV), lambda h, n: (h, rv(n), 0, 0)),
    ]
    assert HB == H
    return _hosted_call(
        body, rider, name="ret_bwd", grid=(1, NC), in_specs=in_specs,
        out_specs=[pl.BlockSpec((C, 2 * QW + 2 * VW), lambda h, n: (rv(n), 0)),
                   pl.BlockSpec((1, VW), lambda h, n: (0, 0))],
        out_shape=[jax.ShapeDtypeStruct((T, in_w), BF16), jax.ShapeDtypeStruct((1, VW), F32)],
        scratch=[pltpu.VMEM((HB, RET_DK, RET_DV), F32)],
        args=[proj, proj, proj, proj, cosf, sins, gn_g, dm, xi, zeta, gc, dri, r, states])


CONV_CW = 128
CONV_TB = 512


SUBLANES = 8


def _shift_copies(win, shifted, tb):
    n = tb + HALO - SUBLANES
    for s in range(1, SUBLANES):
        shifted[s - 1] = win[pl.ds(s, n), :]


def _tap(win, shifted, off, tb):
    s = off % SUBLANES
    if s == 0:
        return win[pl.ds(off, tb), :]
    return shifted[s - 1, pl.ds(off - s, tb), :]


def _conv_fwd(proj, kpad, bias, off_a, CC, T, rider=None):
    tb, cw = min(CONV_TB, T), CONV_CW
    hb = tb // HALO
    ca, cb = off_a // cw, (off_a + CC) // cw

    def body(a_ref, b_ref, ap_ref, bp_ref, k_ref, bias_ref, u1_ref, win, shifted):
        i = pl.program_id(0)
        keep = (i > 0).astype(F32)
        win[0:HALO, :] = ap_ref[...].astype(F32) * _sigmoid(bp_ref[...].astype(F32)) * keep
        win[HALO:, :] = a_ref[...].astype(F32) * _sigmoid(b_ref[...].astype(F32))
        _shift_copies(win, shifted, tb)
        acc = jnp.broadcast_to(bias_ref[...], (tb, cw))
        for w in range(CONV_WIDTH):
            acc = acc + k_ref[w:w + 1, :] * _tap(win, shifted, HALO - (CONV_WIDTH - 1) + w, tb)
        u1_ref[...] = acc

    prev = lambda i: jnp.maximum(i * hb - 1, 0)
    (u1,) = _hosted_call(
        body, rider, name="conv_fwd", grid=(T // tb, CC // cw),
        in_specs=[pl.BlockSpec((tb, cw), lambda i, c: (i, ca + c)),
                  pl.BlockSpec((tb, cw), lambda i, c: (i, cb + c)),
                  pl.BlockSpec((HALO, cw), lambda i, c: (prev(i), ca + c)),
                  pl.BlockSpec((HALO, cw), lambda i, c: (prev(i), cb + c)),
                  pl.BlockSpec((HALO, cw), lambda i, c: (0, c)),
                  pl.BlockSpec((1, cw), lambda i, c: (0, c))],
        out_specs=[pl.BlockSpec((tb, cw), lambda i, c: (i, c))],
        out_shape=[jax.ShapeDtypeStruct((T, CC), F32)],
        scratch=[pltpu.VMEM((tb + HALO, cw), F32), pltpu.VMEM((SUBLANES - 1, tb + HALO - SUBLANES, cw), F32)],
        args=[proj, proj, proj, proj, kpad, bias])
    return u1


def _conv_bwd(du1, proj, kpad, off_a, CC, T, rider=None):
    tb, cw = min(CONV_TB, T), CONV_CW
    hb = tb // HALO
    nt = T // tb
    ca, cb = off_a // cw, (off_a + CC) // cw

    def body(d_ref, dn_ref, a_ref, b_ref, ap_ref, bp_ref, k_ref, da_ref, db_ref, dk_ref, winu, wind, shu, shd):
        i = pl.program_id(1)
        a, b = a_ref[...].astype(F32), b_ref[...].astype(F32)
        sgb = _sigmoid(b)
        winu[0:HALO, :] = ap_ref[...].astype(F32) * _sigmoid(bp_ref[...].astype(F32)) * (i > 0).astype(F32)
        winu[HALO:, :] = a * sgb
        d = d_ref[...]
        wind[0:tb, :] = d
        wind[tb:, :] = dn_ref[...] * (i < nt - 1).astype(F32)

        @pl.when(i == 0)
        def _():
            dk_ref[...] = jnp.zeros_like(dk_ref)

        _shift_copies(winu, shu, tb)
        _shift_copies(wind, shd, tb)
        du0 = jnp.zeros((tb, cw), F32)
        for w in range(CONV_WIDTH):
            du0 = du0 + k_ref[w:w + 1, :] * _tap(wind, shd, CONV_WIDTH - 1 - w, tb)
            dk_ref[w:w + 1, :] += _colsum(_tap(winu, shu, HALO - (CONV_WIDTH - 1) + w, tb) * d)
        da_ref[...] = (du0 * sgb).astype(BF16)
        db_ref[...] = (du0 * a * sgb * (1.0 - sgb)).astype(BF16)

    prev = lambda i: jnp.maximum(i * hb - 1, 0)
    nxt = lambda i: jnp.minimum((i + 1) * hb, T // HALO - 1)
    return _hosted_call(
        body, rider, name="conv_bwd", grid=(CC // cw, nt),
        in_specs=[pl.BlockSpec((tb, cw), lambda c, i: (i, c)),
                  pl.BlockSpec((HALO, cw), lambda c, i: (nxt(i), c)),
                  pl.BlockSpec((tb, cw), lambda c, i: (i, ca + c)),
                  pl.BlockSpec((tb, cw), lambda c, i: (i, cb + c)),
                  pl.BlockSpec((HALO, cw), lambda c, i: (prev(i), ca + c)),
                  pl.BlockSpec((HALO, cw), lambda c, i: (prev(i), cb + c)),
                  pl.BlockSpec((HALO, cw), lambda c, i: (0, c))],
        out_specs=[pl.BlockSpec((tb, cw), lambda c, i: (i, c)),
                   pl.BlockSpec((tb, cw), lambda c, i: (i, c)),
                   pl.BlockSpec((HALO, cw), lambda c, i: (0, c))],
        out_shape=[jax.ShapeDtypeStruct((T, CC), BF16), jax.ShapeDtypeStruct((T, CC), BF16),
                   jax.ShapeDtypeStruct((HALO, CC), F32)],
        scratch=[pltpu.VMEM((tb + HALO, cw), F32), pltpu.VMEM((tb + HALO, cw), F32),
                 pltpu.VMEM((SUBLANES - 1, tb + HALO - SUBLANES, cw), F32),
                 pltpu.VMEM((SUBLANES - 1, tb + HALO - SUBLANES, cw), F32)],
        args=[du1, du1, proj, proj, proj, proj, kpad])


FFN1 = ('ffn1_w_gate', 'ffn1_w_up', 'ffn1_w_down')
FFN2 = ('ffn2_w_gate', 'ffn2_w_up', 'ffn2_w_down')


def _local_step(x, tgt, W, P, gdt=BF16, comm=None):
    T, D = x.shape
    G = {}
    net = _Net(comm, G)
    if comm is not None:
        W = comm.W
        first = net.gather(['ffn1_w_gate', 'ffn1_w_up'])
    (xb,) = _rows("x_to_bf16", lambda v: [v], [(x, 'r', D, 0)], [('r', D, BF16)], T=T, tb=512,
                  rider=first if comm is not None else None)
    if comm is not None:
        net.done(first)
    VW = P['ret_gn_g'].shape[1]
    H = VW // RET_DV
    QW = H * RET_DK
    CC = P['conv_b'].shape[1]
    off_glu = 2 * QW + 2 * VW
    off_gate = off_glu + 2 * CC
    ident = lambda accs: accs

    a1, b1, s1, z1 = _ffn_fwd("ffn1", xb, x, W, FFN1, net,
                              rider=net.gather(['ffn1_w_down', 'w_in'], {'w_in': (0, D // 4, False)}),
                              rider_down=lambda: net.gather(['w_in'], {'w_in': (D // 4, (3 * D) // 8, False)}))
    rider = net.gather(['w_in'], {'w_in': ((5 * D) // 8, (3 * D) // 8, True)})
    x1, x1b, xh1, rs1 = _ln_fwd("ln1", z1, P['ln1_g'], P['ln1_b'], T, D, rider=rider)
    net.done(rider)

    rest = net.gather(['conv_k', 'w_ret_o', 'w_conv_o', 'w_out'])
    (proj,) = _mm("w_in", [x1b], [W['w_in']], [[(0, 0)]], lambda accs, bias: [accs[0] + bias], [F32],
                  tm=2048, tn=0, tk=1024, extras=[(P['b_in'], 'n', 0)], i_outer=False, b3=True, rider=rest)
    net.done(rest)
    tabs = _ret_tables(H, T)
    rider = net.gather(['ffn2_w_gate', 'ffn2_w_up'])
    r, ret_in, states = _ret_fwd(proj, P['ret_gn_g'], tabs, H, T, rider=rider)
    net.done(rider)
    kpad = jnp.pad(W['conv_k'].astype(F32), ((0, HALO - CONV_WIDTH), (0, 0)))
    rider = net.gather(['ffn2_w_down'])
    u1 = _conv_fwd(proj, kpad, P['conv_b'], off_glu, CC, T, rider=rider)
    net.done(rider)

    def conv_ln(u1, g, b):
        xhat, rstd = _ln_stats(u1)
        u2 = xhat * g + b
        return [xhat, rstd, u2 * _sigmoid(u2)]

    xhc, rsc, u3 = _rows("conv_ln", conv_ln, [(u1, 'r', CC, 0), (P['conv_ln_g'], 'v', CC, 0), (P['conv_ln_b'], 'v', CC, 0)],
                         [('r', CC, F32), ('c', 1, F32), ('r', CC, BF16)], T=T, tb=512)
    (ret_out,) = _mm("ret_o", [ret_in], [W['w_ret_o']], [[(0, 0)]], ident, [F32], tm=1024, tn=1024, tk=2048)

    def epi_merge(accs, ret_out, gr, gc):
        conv_out = accs[0]
        return [conv_out, _sigmoid(gr) * ret_out + _sigmoid(gc) * conv_out]

    conv_out, merged = _mm("conv_o_merge", [u3], [W['w_conv_o']], [[(0, 0)]], epi_merge, [F32, BF16],
                           tm=512, tn=D, tk=1024, epi_rows=256,
                           extras=[(ret_out, 'mn', 0), (proj, 'mn', off_gate), (proj, 'mn', off_gate + D)])
    (z2,) = _mm("w_out", [merged], [W['w_out']], [[(0, 0)]], lambda accs, xr: [ALPHA * xr + accs[0]], [F32],
                tm=1024, tn=1024, tk=1024, extras=[(x1, 'mn', 0)])
    x2, x2b, xh2, rs2 = _ln_fwd("ln2", z2, P['ln2_g'], P['ln2_b'], T, D)
    a2, b2, s2, z3 = _ffn_fwd("ffn2", x2b, x2, W, FFN2, net)
    dz3, dz3h, g_ln3_g, g_ln3_b, loss = _ln_loss_bwd("ln3_loss", z3, P['ln3_g'], P['ln3_b'], tgt, T, D)

    S = {'ln3_g': g_ln3_g, 'ln3_b': g_ln3_b}
    dy2 = _ffn_bwd("ffn2b", dz3h, dz3, x2b, a2, b2, s2, W, FFN2, gdt, G, net, 'down')
    rider = net.to_sibling(['ffn2_w_gate', 'ffn2_w_up'])
    dz2, dz2b, S['ln2_g'], S['ln2_b'] = _ln_bwd("ln2b", dy2, xh2, rs2, P['ln2_g'], 1.0, T, D, rider=rider)
    net.done(rider)
    net.pairsum(['ffn2_w_gate', 'ffn2_w_up'])

    (G['w_out'],) = _mm("d_w_out", [merged], [dz2b], [[(0, 0)]], ident, [gdt], ta=True, tm=1024, tn=1024, tk=1024)

    def epi_dmerge(accs, ret_out, conv_out, gr, gc):
        dm_ = accs[0]
        sr, sc = _sigmoid(gr), _sigmoid(gc)
        return [dm_ * sr, dm_ * sc, dm_ * ret_out * sr * (1.0 - sr), dm_ * conv_out * sc * (1.0 - sc)]

    rider = net.to_sibling(['w_out'])
    dret_out, dconv_out, dgate_r, dgate_c = _mm(
        "d_merge", [dz2b], [W['w_out']], [[(0, 0)]], epi_dmerge, [BF16, BF16, BF16, BF16], tb=True,
        tm=512, tn=D, tk=1024, epi_rows=256, rider=rider,
        extras=[(ret_out, 'mn', 0), (conv_out, 'mn', 0), (proj, 'mn', off_gate), (proj, 'mn', off_gate + D)])
    net.done(rider)
    (G['w_ret_o'],) = _mm("d_w_ret_o", [ret_in], [dret_out], [[(0, 0)]], ident, [gdt], ta=True, tm=1024, tn=1024, tk=1024)
    (G['w_conv_o'],) = _mm("d_w_conv_o", [u3], [dconv_out], [[(0, 0)]], ident, [gdt], ta=True, tm=1024, tn=1024, tk=1024)
    rider = net.to_sibling(['w_ret_o', 'w_conv_o'])
    (dri,) = _mm("d_ret_in", [dret_out], [W['w_ret_o']], [[(0, 0)]], ident, [F32], tb=True, tm=1024, tn=1024, tk=1024,
                 rider=rider)
    net.done(rider)
    net.pairsum(['w_out', 'w_ret_o', 'w_conv_o'])
    rider = net.to_owner(['ffn2_w_gate', 'ffn2_w_up'])
    dproj, S['ret_gn_g'] = _ret_bwd(dri, r, states, proj, P['ret_gn_g'], tabs, H, T, proj.shape[1], rider=rider)
    net.done(rider)

    def epi_du2(accs, xhat, g, b):
        u2 = xhat * g + b
        return [accs[0] * _dsilu(u2, _sigmoid(u2))]

    (du2,) = _mm("d_u3", [dconv_out], [W['w_conv_o']], [[(0, 0)]], epi_du2, [F32], tb=True, tm=512, tn=CC, tk=1024, epi_rows=256,
                 extras=[(xhc, 'mn', 0), (P['conv_ln_g'], 'n', 0), (P['conv_ln_b'], 'n', 0)])

    def conv_ln_bwd(du2, xhat, rstd, g):
        du1 = _ln_bwd_math(du2, xhat, rstd, g)
        return [du1, _colsum(du2 * xhat), _colsum(du2), _colsum(du1)]

    du1, S['conv_ln_g'], S['conv_ln_b'], S['conv_b'] = _rows(
        "conv_ln_bwd", conv_ln_bwd, [(du2, 'r', CC, 0), (xhc, 'r', CC, 0), (rsc, 'r', 1, 0), (P['conv_ln_g'], 'v', CC, 0)],
        [('r', CC, F32), ('a', CC, F32), ('a', CC, F32), ('a', CC, F32)], T=T, tb=512)
    rider = net.to_owner(['w_out', 'w_ret_o', 'w_conv_o'])
    dglu_a, dglu_b, dkpad = _conv_bwd(du1, proj, kpad, off_glu, CC, T, rider=rider)
    net.done(rider)
    G['conv_k'] = dkpad[:CONV_WIDTH].astype(gdt)

    for off, piece in ((off_glu, dglu_a), (off_glu + CC, dglu_b), (off_gate, dgate_r), (off_gate + D, dgate_c)):
        dproj = lax.dynamic_update_slice(dproj, piece, (0, off))
    IN_W = dproj.shape[1]
    rider = net.exchange(['conv_k'])
    G['w_in'], S['b_in'] = _mm("d_w_in", [x1b], [dproj], [[(0, 0)]], ident, [gdt], ta=True, o3=True, bsum=True,
                               tm=1024, tn=W['w_in'].shape[2], tk=1024, rider=rider)
    net.done(rider)
    cuts = [0, (3 * D) // 16, (43 * D) // 64, D]
    w_in_rows = [{'w_in': (cuts[i], cuts[i + 1] - cuts[i], i == 2)} for i in range(3)]
    rider = net.to_sibling(['w_in'])
    (dy1,) = _mm("d_x1", [dproj], [W['w_in']], [[(0, 0)]], lambda accs, dzr: [ALPHA * dzr + accs[0]], [F32], tb=True,
                 b3=True, tm=1024, tn=1024, tk=0, extras=[(dz2, 'mn', 0)], rider=rider)
    net.done(rider)
    net.pairsum(['w_in'])
    rider = net.to_owner(['w_in'], w_in_rows[0])
    dz1, dz1h, S['ln1_g'], S['ln1_b'] = _ln_bwd("ln1b", dy1, xh1, rs1, P['ln1_g'], 0.5, T, D, rider=rider)
    net.done(rider)
    grad_x = _ffn_bwd("ffn1b", dz1h, dz1, xb, a1, b1, s1, W, FFN1, gdt, G, net, 'all',
                      pre=(lambda: net.to_owner(['w_in'], w_in_rows[1]), lambda: net.to_owner(['w_in'], w_in_rows[2])))
    return loss, grad_x, G, S


def _coords():
    return lax.axis_index("x"), lax.axis_index("y"), lax.axis_index("c")


def _flip(k, x, y, c):
    return (1 - x if k & 4 else x, 1 - y if k & 2 else y, 1 - c if k & 1 else c)


def _lin(p):
    return 4 * p[0] + 2 * p[1] + p[2]


class _Rider:
    def __init__(self, ins, out_shape, rows=None, fill=None):
        nb = len(ins)
        self.rows = rows or [None] * nb
        fill = fill or [None] * nb
        self.aliases = {nb + i: w for i, w in enumerate(w for w in range(nb) if fill[w] is not None)}
        self.ins = list(ins) + [f for f in fill if f is not None]
        self.out_shape, self.results = list(out_shape), None
        self.scratch = [pltpu.SemaphoreType.DMA((8 * nb,)), pltpu.SemaphoreType.DMA((8 * nb,)),
                        pltpu.SemaphoreType.DMA((nb,))]

    def span(self, w, ref, *slot, half=None):
        rows = self.rows[w]
        if half is not None:
            first, count = rows if rows is not None else (0, self.out_shape[w].shape[1])
            rows = (first + half * (count // 2), count // 2)
        if rows is None:
            return ref.at[slot] if slot else ref
        return ref.at[(*slot, pl.ds(*rows))]

    def begin(self, step, n_steps, ins, outs, sems):
        @pl.when(step == 0)
        def _():
            self.start(ins, outs, sems)

        @pl.when(step == min(n_steps - 1, (5 * n_steps) // 8))
        def _():
            self.relay(ins, outs, sems)

        @pl.when(step == n_steps - 1)
        def _():
            self.mid(ins, outs, sems)

    def end(self, step, n_steps, ins, outs, sems):
        @pl.when(step == n_steps - 1)
        def _():
            self.finish(ins, outs, sems)

    def relay(self, ins, outs, sems):
        pass

    def mid(self, ins, outs, sems):
        pass


class _GatherRider(_Rider):
    def __init__(self, blks, rows=None, fill=None):
        super().__init__(blks, [jax.ShapeDtypeStruct((N_DEV,) + b.shape, b.dtype) for b in blks], rows, fill)
        counts = [(r[1] if r is not None else b.shape[0]) for r, b in zip(self.rows, blks)]
        self.halves = [n % 32 == 0 for n in counts]

    def _copies(self, x_refs, out_refs, sems):
        nb = len(self.out_shape)
        send_sems, recv_sems, local_sems = sems
        x, y, c = _coords()
        me, sib = (x, y, c), (x, y, 1 - c)
        xn, yn, dg = _flip(4, x, y, c), _flip(2, x, y, c), _flip(6, x, y, c)
        plans = []
        for w in range(nb):
            own = self.span(w, x_refs[w])

            def copy(k, block, to, src=None, half=None, w=w):
                slot = self.span(w, out_refs[w], _lin(block), half=half)
                return pltpu.make_async_remote_copy(
                    src_ref=slot if src is None else src, dst_ref=slot, send_sem=send_sems.at[k * nb + w],
                    recv_sem=recv_sems.at[k * nb + w], device_id=to, device_id_type=MESH)

            mine = pltpu.make_async_copy(own, self.span(w, out_refs[w], _lin(me)), local_sems.at[w])
            first = [copy(0, me, sib, src=own), copy(1, me, xn, src=own), copy(2, me, yn, src=own)]
            if self.halves[w]:
                relay = [(copy(1, xn, me), [copy(3, xn, yn, half=0), copy(5, xn, sib)]),
                         (copy(2, yn, me), [copy(4, yn, xn, half=1), copy(6, yn, sib)])]
                last = [(copy(3, dg, me, half=0), []), (copy(4, dg, me, half=1), [copy(7, dg, sib)])]
            else:
                first.append(copy(3, me, dg, src=own))
                relay = [(copy(1, xn, me), [copy(5, xn, sib)]), (copy(2, yn, me), [copy(6, yn, sib)])]
                last = [(copy(3, dg, me), [copy(7, dg, sib)])]
            other = lambda p: (p[0], p[1], 1 - c)
            from_sib = [copy(0, sib, me), copy(5, other(xn), me), copy(6, other(yn), me), copy(7, other(dg), me)]
            plans.append((mine, first, relay, last, from_sib))
        return plans

    def start(self, ins, outs, sems):
        for mine, first, _, _, _ in self._copies(ins, outs, sems):
            for cp in [mine] + first:
                cp.start()

    def relay(self, ins, outs, sems):
        for _, _, relay, _, _ in self._copies(ins, outs, sems):
            for arrival, released in relay:
                arrival.wait_recv()
                for cp in released:
                    cp.start()

    def mid(self, ins, outs, sems):
        for _, _, _, last, _ in self._copies(ins, outs, sems):
            for arrival, released in last:
                arrival.wait_recv()
                for cp in released:
                    cp.start()

    def finish(self, ins, outs, sems):
        for mine, first, relay, last, from_sib in self._copies(ins, outs, sems):
            for cp in from_sib:
                cp.wait_recv()
            for cp in first + [cp for _, released in relay + last for cp in released]:
                cp.wait_send()
            mine.wait()


class _ExchangeRider(_Rider):
    def __init__(self, gs, rows=None, fill=None):
        super().__init__(gs, [jax.ShapeDtypeStruct(g.shape, g.dtype) for g in gs], rows, fill)

    def _copies(self, g_refs, out_refs, sems):
        nb = len(self.out_shape)
        send_sems, recv_sems, local_sems = sems
        x, y, c = _coords()
        me = _lin((x, y, c))

        def copy(k, w, landing):
            peer = _flip(k, x, y, c)
            src, dst = (me, _lin(peer)) if landing else (_lin(peer), me)
            return pltpu.make_async_remote_copy(
                src_ref=self.span(w, g_refs[w], src), dst_ref=self.span(w, out_refs[w], dst),
                send_sem=send_sems.at[(k - 1) * nb + w], recv_sem=recv_sems.at[(k - 1) * nb + w],
                device_id=peer, device_id_type=MESH)

        mines = [pltpu.make_async_copy(self.span(w, g_refs[w], me), self.span(w, out_refs[w], me), local_sems.at[w])
                 for w in range(nb)]
        sends = [copy(k, w, False) for w in range(nb) for k in range(1, N_DEV)]
        landings = [copy(k, w, True) for w in range(nb) for k in range(1, N_DEV)]
        return mines, sends, landings

    def start(self, ins, outs, sems):
        mines, sends, _ = self._copies(ins, outs, sems)
        for cp in mines + sends:
            cp.start()

    def finish(self, ins, outs, sems):
        mines, sends, landings = self._copies(ins, outs, sems)
        for cp in landings:
            cp.wait_recv()
        for cp in sends:
            cp.wait_send()
        for mine in mines:
            mine.wait()


class _SiblingRider(_Rider):
    def __init__(self, gs):
        super().__init__(gs, [jax.ShapeDtypeStruct((4,) + g.shape[1:], g.dtype) for g in gs])

    def _copies(self, g_refs, out_refs, sems, landing):
        nb = len(self.out_shape)
        send_sems, recv_sems, _ = sems
        x, y, c = _coords()
        whose = c if landing else 1 - c
        return [pltpu.make_async_remote_copy(
            src_ref=g_refs[w].at[2 * q + whose], dst_ref=out_refs[w].at[q], send_sem=send_sems.at[q * nb + w],
            recv_sem=recv_sems.at[q * nb + w], device_id=(x, y, 1 - c), device_id_type=MESH)
            for w in range(nb) for q in range(4)]

    def start(self, ins, outs, sems):
        for cp in self._copies(ins, outs, sems, False):
            cp.start()

    def finish(self, ins, outs, sems):
        for cp in self._copies(ins, outs, sems, True):
            cp.wait_recv()
        for cp in self._copies(ins, outs, sems, False):
            cp.wait_send()


class _ChipRider(_Rider):
    FLIPS = (4, 2, 6)

    def __init__(self, ps, rows=None, fill=None):
        super().__init__(ps, [jax.ShapeDtypeStruct(p.shape, p.dtype) for p in ps], rows, fill)

    def _copies(self, p_refs, out_refs, sems):
        nb = len(self.out_shape)
        send_sems, recv_sems, local_sems = sems
        x, y, c = _coords()
        my_chip = 2 * x + y

        def copy(j, w, landing):
            peer = _flip(self.FLIPS[j], x, y, c)
            peer_chip = 2 * peer[0] + peer[1]
            src, dst = (my_chip, peer_chip) if landing else (peer_chip, my_chip)
            return pltpu.make_async_remote_copy(
                src_ref=self.span(w, p_refs[w], src), dst_ref=self.span(w, out_refs[w], dst),
                send_sem=send_sems.at[j * nb + w], recv_sem=recv_sems.at[j * nb + w],
                device_id=peer, device_id_type=MESH)

        mines = [pltpu.make_async_copy(self.span(w, p_refs[w], my_chip), self.span(w, out_refs[w], my_chip),
                                       local_sems.at[w]) for w in range(nb)]
        sends = [copy(j, w, False) for w in range(nb) for j in range(3)]
        landings = [copy(j, w, True) for w in range(nb) for j in range(3)]
        return mines, sends, landings

    def start(self, ins, outs, sems):
        mines, sends, _ = self._copies(ins, outs, sems)
        for cp in mines + sends:
            cp.start()

    def finish(self, ins, outs, sems):
        mines, sends, landings = self._copies(ins, outs, sems)
        for cp in landings:
            cp.wait_recv()
        for cp in sends:
            cp.wait_send()
        for mine in mines:
            mine.wait()


def _pairsum(name, g, land):
    _, r, cols = g.shape
    tb = r if r % 16 else _row_tile(r, 16, max(16, (256 * 1024) // cols))
    core = lax.axis_index("c").astype(jnp.int32).reshape(1)

    def body(core_ref, g_ref, l_ref, o_ref):
        o_ref[...] = (g_ref[...].astype(F32) + l_ref[...].astype(F32)).astype(o_ref.dtype)

    return pl.pallas_call(
        body, name=name, out_shape=jax.ShapeDtypeStruct((4, r, cols), g.dtype),
        grid_spec=pltpu.PrefetchScalarGridSpec(
            num_scalar_prefetch=1, grid=(4, r // tb),
            in_specs=[pl.BlockSpec((None, None, tb, cols), lambda q, i, core_ref: (q, core_ref[0], i, 0)),
                      pl.BlockSpec((None, tb, cols), lambda q, i, core_ref: (q, i, 0))],
            out_specs=pl.BlockSpec((None, tb, cols), lambda q, i, core_ref: (q, i, 0))),
        compiler_params=_params(("arbitrary", "arbitrary")),
    )(core, g.reshape(4, 2, r, cols), land)


def _run_rider(name, rider):
    n_in, n_out = len(rider.ins), len(rider.out_shape)

    def body(*refs):
        ride = (refs[:n_in], refs[n_in:n_in + n_out], refs[n_in + n_out:])
        rider.start(*ride)
        rider.relay(*ride)
        rider.mid(*ride)
        rider.finish(*ride)

    rider.results = pl.pallas_call(
        body, name=name, out_shape=rider.out_shape, in_specs=[ANY] * n_in, out_specs=[ANY] * n_out,
        scratch_shapes=rider.scratch, input_output_aliases=dict(rider.aliases),
        compiler_params=pltpu.CompilerParams(has_side_effects=True),
    )(*rider.ins)
    return rider.results


def _as_matrix(name, g):
    if name == 'w_in':
        return g
    if name in COL_SHARDED:
        return jnp.transpose(g, (1, 0, 2)).reshape(g.shape[1], N_DEV * g.shape[2])
    return g.reshape(N_DEV * g.shape[1], g.shape[2])


def _by_owner(name, g):
    if name == 'w_in':
        return g
    if name in COL_SHARDED:
        return jnp.transpose(g.reshape(g.shape[0], N_DEV, g.shape[1] // N_DEV), (1, 0, 2))
    return g.reshape(N_DEV, g.shape[0] // N_DEV, g.shape[1])


class _Comm:
    def __init__(self, shards):
        self.shards, self.W, self.parts, self.partial, self.sent = shards, {}, {}, {}, {}
        self.from_sibling, self.pairs = {}, {}

    def _ride(self, cls, names, srcs, part, sink):
        part = part or {}
        rider = cls(srcs, rows=[part[n][:2] if n in part else None for n in names],
                    fill=[self.partial.pop((sink, n), None) for n in names])
        rider.names, rider.sink = names, sink
        rider.unfinished = {n for n in names if n in part and not part[n][2]}
        return rider

    def gather(self, names, part=None):
        return self._ride(_GatherRider, names, [self.shards[n] for n in names], part, 'W')

    def exchange(self, names, G, part=None):
        for n in names:
            if n not in self.sent:
                self.sent[n] = _by_owner(n, G[n])
        return self._ride(_ExchangeRider, names, [self.sent[n] for n in names], part, 'parts')

    def to_sibling(self, names, G):
        for n in names:
            self.sent[n] = _by_owner(n, G[n])
        rider = _SiblingRider([self.sent[n] for n in names])
        rider.names, rider.sink, rider.unfinished = names, 'sibling', set()
        return rider

    def pairsum(self, names):
        for n in names:
            self.pairs[n] = _pairsum("pairsum_" + n, self.sent[n], self.from_sibling.pop(n))

    def to_owner(self, names, part=None):
        return self._ride(_ChipRider, names, [self.pairs[n] for n in names], part, 'parts')

    def collect(self, rider):
        for n, res in zip(rider.names, rider.results):
            if n in rider.unfinished:
                self.partial[(rider.sink, n)] = res
            elif rider.sink == 'W':
                self.W[n] = _as_matrix(n, res)
            elif rider.sink == 'sibling':
                self.from_sibling[n] = res
            else:
                self.parts[n] = res


def _adamw_math(p_ref, w_ref, m_ref, v_ref, g_ref, d_ref, nm_ref, nv_ref):
    c1 = 1.0 - ADAM_B1 ** ADAM_STEP
    c2 = 1.0 - ADAM_B2 ** ADAM_STEP
    g = p_ref[0].astype(F32)
    for s in range(1, p_ref.shape[0]):
        g = g + p_ref[s].astype(F32)
    nm = ADAM_B1 * m_ref[...] + (1.0 - ADAM_B1) * g
    nv = ADAM_B2 * v_ref[...] + (1.0 - ADAM_B2) * (g * g)
    g_ref[...] = g
    nm_ref[...] = nm
    nv_ref[...] = nv
    d_ref[...] = -ADAM_LR * ((nm / c1) / (jnp.sqrt(nv / c2) + ADAM_EPS) + ADAM_WD * w_ref[...])


def _adamw_vectors(parts, ws, ms, vs, loss_parts):
    k = len(ws)

    def body(*refs):
        for i in range(k):
            _adamw_math(refs[i], refs[k + i], refs[2 * k + i], refs[3 * k + i], *refs[4 * k + 1 + 4 * i:4 * k + 5 + 4 * i])
        lp, lo = refs[4 * k], refs[8 * k + 1]
        lo[...] = functools.reduce(jnp.add, [lp[s] for s in range(lp.shape[0])])

    return pl.pallas_call(
        body, name="adamw_vectors",
        out_shape=[jax.ShapeDtypeStruct(w.shape, F32) for w in ws for _ in range(4)] + [jax.ShapeDtypeStruct((1, 128), F32)],
        compiler_params=_params(),
    )(*parts, *ws, *ms, *vs, loss_parts)


def _adamw(name, parts, w, m, v, tb):
    n, R, Wd = parts.shape
    assert R % tb == 0
    body = functools.partial(_adamw_math)

    row = pl.BlockSpec((tb, Wd), lambda i: (i, 0))
    return pl.pallas_call(
        body, name=name, grid=(R // tb,),
        in_specs=[pl.BlockSpec((n, tb, Wd), lambda i: (0, i, 0)), row, row, row],
        out_specs=[row, row, row, row], out_shape=[jax.ShapeDtypeStruct((R, Wd), F32)] * 4,
        compiler_params=_params(("arbitrary",)),
    )(parts, w, m, v)


def _row_tile(R, unit, cap):
    best = unit
    for t in range(unit, cap + 1, unit):
        if R % t == 0:
            best = t
    return best


def kernel(x, ffn1_w_gate, ffn1_w_up, ffn1_w_down, ln1_g, ln1_b, w_in, b_in, ret_gn_g, conv_k, conv_b, conv_ln_g, conv_ln_b, w_ret_o, w_conv_o, w_out, ln2_g, ln2_b, ffn2_w_gate, ffn2_w_up, ffn2_w_down, ln3_g, ln3_b, loss_target, m_ffn1_w_gate, m_ffn1_w_up, m_ffn1_w_down, m_ln1_g, m_ln1_b, m_w_in, m_b_in, m_ret_gn_g, m_conv_k, m_conv_b, m_conv_ln_g, m_conv_ln_b, m_w_ret_o, m_w_conv_o, m_w_out, m_ln2_g, m_ln2_b, m_ffn2_w_gate, m_ffn2_w_up, m_ffn2_w_down, m_ln3_g, m_ln3_b, v_ffn1_w_gate, v_ffn1_w_up, v_ffn1_w_down, v_ln1_g, v_ln1_b, v_w_in, v_b_in, v_ret_gn_g, v_conv_k, v_conv_b, v_conv_ln_g, v_conv_ln_b, v_w_ret_o, v_w_conv_o, v_w_out, v_ln2_g, v_ln2_b, v_ffn2_w_gate, v_ffn2_w_up, v_ffn2_w_down, v_ln3_g, v_ln3_b):
    given = dict(locals())
    wts = {n: given[n] for n in WEIGHTS}
    mom = {n: given['m_' + n] for n in WEIGHTS}
    var = {n: given['v_' + n] for n in WEIGHTS}

    def shard2d(a):
        return a.reshape(a.shape[-3] * a.shape[-2] if a.ndim == 4 else a.shape[-2], a.shape[-1])

    comm = _Comm({n: shard2d(wts[n]).astype(BF16) for n in BIG})
    P = {n: wts[n].reshape(1, -1) for n in SMALL}
    loss, grad_x, _, S = _local_step(x[0], loss_target[0], None, P, comm=comm)

    parts = comm.parts
    res = {}
    for n in BIG:
        rows, cols = parts[n].shape[1:]
        tb = rows if rows % 16 else _row_tile(rows, 16, max(16, (256 * 1024) // cols))
        res[n] = _adamw("adamw_" + n, parts[n], shard2d(wts[n]), shard2d(mom[n]), shard2d(var[n]), tb)

    vec_parts = _run_rider("gather_vector_grads", _GatherRider([S[n] for n in SMALL] + [loss]))
    vec = _adamw_vectors(vec_parts[:-1], [P[n] for n in SMALL], [mom[n].reshape(1, -1) for n in SMALL],
                         [var[n].reshape(1, -1) for n in SMALL], vec_parts[-1])
    for i, n in enumerate(SMALL):
        res[n] = vec[4 * i:4 * i + 4]

    outs = [vec[-1][0, 0], grad_x[None]]
    for k in range(4):
        for n in WEIGHTS:
            outs.append(res[n][k].reshape(wts[n].shape))
    return tuple(outs)
```

```python
import functools
import math

import jax
import jax.numpy as jnp
from jax import lax
from jax.experimental import pallas as pl
from jax.experimental.pallas import tpu as pltpu

F32 = jnp.float32
BF16 = jnp.bfloat16

N_DEV = 8
LN_EPS = 1e-5
ALPHA = 2.0 ** 0.25
RET_DK = 128
RET_DV = 256
RET_CHUNK = 256
ROPE_BASE = 10000.0
CONV_WIDTH = 31
HALO = 32
ADAM_LR, ADAM_B1, ADAM_B2, ADAM_EPS, ADAM_WD, ADAM_STEP = 0.001, 0.9, 0.999, 1e-08, 0.01, 10
VMEM_LIMIT = 52 * 1024 * 1024
MESH = pl.DeviceIdType.MESH
ANY = pl.BlockSpec(memory_space=pl.ANY)

BIG = ['ffn1_w_gate', 'ffn1_w_up', 'ffn1_w_down', 'w_in', 'w_ret_o', 'w_conv_o', 'w_out',
       'ffn2_w_gate', 'ffn2_w_up', 'ffn2_w_down', 'conv_k']
COL_SHARDED = {'ffn1_w_gate', 'ffn1_w_up', 'w_in', 'ffn2_w_gate', 'ffn2_w_up', 'conv_k'}
SMALL = ['ln1_g', 'ln1_b', 'b_in', 'ret_gn_g', 'conv_b', 'conv_ln_g', 'conv_ln_b', 'ln2_g', 'ln2_b', 'ln3_g', 'ln3_b']
WEIGHTS = ['ffn1_w_gate', 'ffn1_w_up', 'ffn1_w_down', 'ln1_g', 'ln1_b', 'w_in', 'b_in', 'ret_gn_g', 'conv_k', 'conv_b',
           'conv_ln_g', 'conv_ln_b', 'w_ret_o', 'w_conv_o', 'w_out', 'ln2_g', 'ln2_b', 'ffn2_w_gate', 'ffn2_w_up',
           'ffn2_w_down', 'ln3_g', 'ln3_b']


def _params(sem=None):
    return pltpu.CompilerParams(dimension_semantics=sem, vmem_limit_bytes=VMEM_LIMIT)


def _sigmoid(x):
    return jax.nn.sigmoid(x)


def _dsilu(x, sg):
    return sg * (1.0 + x * (1.0 - sg))


def _fit(dim, want):
    if dim <= want:
        return dim
    return max(t for t in range(128, want + 1, 128) if dim % t == 0)


def _dot(a, b, ta=False, tb=False):
    dn = (((0,) if ta else (1,), (1,) if tb else (0,)), ((), ()))
    return lax.dot_general(a, b, dn, preferred_element_type=F32)


def _mm(name, As, Bs, prods, epi, out_dtypes, *, ta=False, tb=False, tm, tn, tk, extras=(), i_outer=True,
        b3=False, o3=False, rider=None, bsum=False, epi_rows=0):
    a0, b0 = As[0], Bs[0]
    M, K = (a0.shape[1], a0.shape[0]) if ta else a0.shape
    if b3:
        S, rows, cs = b0.shape
        N = rows if tb else S * cs
        assert K == (S * cs if tb else rows)
        tn, tk = (tn, cs) if tb else (cs, tk)
    else:
        N = b0.shape[0] if tb else b0.shape[1]
    tm, tn, tk = _fit(M, tm), _fit(N, tn), _fit(K, tk)
    assert M % tm == 0 and N % tn == 0 and K % tk == 0, (name, M, N, K, tm, tn, tk)
    gi, gj, gk = M // tm, N // tn, K // tk
    grid = (gi, gj, gk) if i_outer else (gj, gi, gk)

    def ij(g0, g1):
        return (g0, g1) if i_outer else (g1, g0)

    def amap(g0, g1, k):
        i, _ = ij(g0, g1)
        return (k, i) if ta else (i, k)

    def bmap(g0, g1, k):
        _, j = ij(g0, g1)
        return (j, k) if tb else (k, j)

    def bmap3(g0, g1, k):
        _, j = ij(g0, g1)
        return (k, j, 0) if tb else (j, k, 0)

    in_specs = [pl.BlockSpec((tk, tm) if ta else (tm, tk), amap) for _ in As]
    if b3:
        in_specs += [pl.BlockSpec((None, tn, tk) if tb else (None, tk, tn), bmap3) for _ in Bs]
    else:
        in_specs += [pl.BlockSpec((tn, tk) if tb else (tk, tn), bmap) for _ in Bs]
    args = list(As) + list(Bs)
    for arr, kind, coloff in extras:
        assert coloff % tn == 0
        off = coloff // tn
        if kind == 'mn':
            in_specs.append(pl.BlockSpec((tm, tn), lambda g0, g1, k, off=off: (ij(g0, g1)[0], ij(g0, g1)[1] + off)))
        else:
            in_specs.append(pl.BlockSpec((1, tn), lambda g0, g1, k, off=off: (0, ij(g0, g1)[1] + off)))
        args.append(arr)
    if o3:
        out_shape = [jax.ShapeDtypeStruct((gj, M, tn), dt) for dt in out_dtypes]
        out_specs = [pl.BlockSpec((None, tm, tn), lambda g0, g1, k: (ij(g0, g1)[1], ij(g0, g1)[0], 0))
                     for _ in out_dtypes]
    else:
        out_shape = [jax.ShapeDtypeStruct((M, N), dt) for dt in out_dtypes]
        out_specs = [pl.BlockSpec((tm, tn), lambda g0, g1, k: ij(g0, g1)) for _ in out_dtypes]
    if bsum:
        assert gi == 1 and not tb and not b3
        out_shape.append(jax.ShapeDtypeStruct((1, N), F32))
        out_specs.append(pl.BlockSpec((1, tn), lambda g0, g1, k: (0, ij(g0, g1)[1])))
    n_a, n_b, n_e, n_o = len(As), len(Bs), len(extras), len(out_shape)
    n_p = len(prods) if gk > 1 else 0
    scratch = [pltpu.VMEM((tm, tn), F32) for _ in range(n_p)]
    if rider is not None:
        in_specs, out_specs = in_specs + [ANY] * len(rider.ins), out_specs + [ANY] * len(rider.out_shape)
        args, out_shape, scratch = args + rider.ins, out_shape + rider.out_shape, scratch + rider.scratch
    n_in, n_out = len(args), len(out_shape)

    def body(*refs):
        a_refs = refs[:n_a]
        b_refs = refs[n_a:n_a + n_b]
        e_refs = refs[n_a + n_b:n_a + n_b + n_e]
        o_refs = refs[n_in:n_in + n_o]
        acc_refs = refs[n_in + n_out:n_in + n_out + n_p]
        k = pl.program_id(2)
        if rider is not None:
            step = (pl.program_id(0) * grid[1] + pl.program_id(1)) * gk + k
            ride = (step, grid[0] * grid[1] * gk, refs[n_a + n_b + n_e:n_in], refs[n_in + n_o:n_in + n_out],
                    refs[n_in + n_out + n_p:])
            rider.begin(*ride)

        def finish(accs, rows=slice(None)):
            ex = [(e[rows, :] if kind == 'mn' else e[...]).astype(F32) for e, (_, kind, _) in zip(e_refs, extras)]
            for o, r in zip(o_refs, epi(accs, *ex)):
                o[rows, :] = r.astype(o.dtype)

        if bsum:
            @pl.when(k == 0)
            def _():
                o_refs[-1][...] = jnp.zeros_like(o_refs[-1])

            o_refs[-1][...] += _colsum(b_refs[0][...].astype(F32))

        if gk == 1:
            sub = tm if ta or not epi_rows else _fit(tm, epi_rows)
            for r0 in range(0, tm, sub):
                rows = slice(None) if ta else slice(r0, r0 + sub)
                finish([functools.reduce(jnp.add, [_dot(a_refs[ai][...] if ta else a_refs[ai][rows, :],
                                                        b_refs[bi][...], ta, tb) for ai, bi in terms])
                        for terms in prods], rows)
        else:
            @pl.when(k == 0)
            def _():
                for acc in acc_refs:
                    acc[...] = jnp.zeros_like(acc)

            for p, terms in enumerate(prods):
                for ai, bi in terms:
                    acc_refs[p][...] += _dot(a_refs[ai][...], b_refs[bi][...], ta, tb)

            @pl.when(k == gk - 1)
            def _():
                finish([acc[...] for acc in acc_refs])

        if rider is not None:
            rider.end(*ride)

    aliases = {} if rider is None else {n_a + n_b + n_e + p: n_o + o for p, o in rider.aliases.items()}
    res = pl.pallas_call(
        body, name=name, grid=grid, in_specs=in_specs, out_specs=out_specs, out_shape=out_shape,
        scratch_shapes=scratch, input_output_aliases=aliases,
        compiler_params=_params(("arbitrary", "arbitrary", "arbitrary")),
    )(*args)
    if rider is not None:
        rider.results = res[n_o:]
    return res[:n_o]


def _rows(name, fn, ins, outs, *, T, tb, rider=None):
    tb = min(tb, T)
    assert T % tb == 0
    in_specs, args = [], []
    for arr, kind, width, cb in ins:
        if kind == 'r':
            in_specs.append(pl.BlockSpec((tb, width), lambda i, _, cb=cb: (i, cb)))
        else:
            in_specs.append(pl.BlockSpec((1, width), lambda i, _, cb=cb: (0, cb)))
        args.append(arr)
    out_shape, out_specs = [], []
    for kind, width, dtype in outs:
        if kind == 'r':
            out_shape.append(jax.ShapeDtypeStruct((T, width), dtype))
            out_specs.append(pl.BlockSpec((tb, width), lambda i, _: (i, 0)))
        elif kind == 'c':
            out_shape.append(jax.ShapeDtypeStruct((T, 1), dtype))
            out_specs.append(pl.BlockSpec((tb, 1), lambda i, _: (i, 0)))
        else:
            out_shape.append(jax.ShapeDtypeStruct((1, width), F32))
            out_specs.append(pl.BlockSpec((1, width), lambda i, _: (0, 0)))
    n_in = len(ins)

    def body(*refs):
        i = pl.program_id(0)
        vals = fn(*[r[...] for r in refs[:n_in]])
        for (kind, _, _), o, v in zip(outs, refs[n_in:], vals):
            if kind == 'a':
                @pl.when(i == 0)
                def _(o=o):
                    o[...] = jnp.zeros_like(o)

                o[...] += v
            else:
                o[...] = v.astype(o.dtype)

    return _hosted_call(body, rider, name=name, grid=(T // tb, 1), in_specs=in_specs, out_specs=out_specs,
                        out_shape=out_shape, scratch=[], args=args)


def _colsum(v):
    return jnp.sum(v, axis=0, keepdims=True)


def _ln_stats(z):
    mu = jnp.mean(z, axis=-1, keepdims=True)
    d = z - mu
    var = jnp.mean(d * d, axis=-1, keepdims=True)
    rstd = lax.rsqrt(var + LN_EPS)
    return d * rstd, rstd


def _ln_bwd_math(dy, xhat, rstd, g):
    dxh = dy * g
    m1 = jnp.mean(dxh, axis=-1, keepdims=True)
    m2 = jnp.mean(dxh * xhat, axis=-1, keepdims=True)
    return rstd * (dxh - m1 - xhat * m2)


def _ln_fwd(name, z, g, b, T, D, rider=None):
    def fn(z, g, b):
        xhat, rstd = _ln_stats(z)
        y = xhat * g + b
        return [y, y, xhat, rstd]

    return _rows(name, fn, [(z, 'r', D, 0), (g, 'v', D, 0), (b, 'v', D, 0)],
                 [('r', D, F32), ('r', D, BF16), ('r', D, F32), ('c', 1, F32)], T=T, tb=512, rider=rider)


def _ln_bwd(name, dy, xhat, rstd, g, scale, T, D, rider=None):
    def fn(dy, xhat, rstd, g):
        dz = _ln_bwd_math(dy, xhat, rstd, g)
        return [dz, dz * scale, _colsum(dy * xhat), _colsum(dy)]

    return _rows(name, fn, [(dy, 'r', D, 0), (xhat, 'r', D, 0), (rstd, 'r', 1, 0), (g, 'v', D, 0)],
                 [('r', D, F32), ('r', D, BF16), ('a', D, F32), ('a', D, F32)], T=T, tb=512, rider=rider)


def _ln_loss_bwd(name, z, g, b, tgt, T, D):
    def fn(z, g, b, tgt):
        xhat, rstd = _ln_stats(z)
        err = xhat * g + b - tgt
        row_loss = 0.5 * jnp.mean(err * err, axis=-1, keepdims=True)
        loss = jnp.broadcast_to(jnp.sum(row_loss, axis=0, keepdims=True), (1, 128))
        dy = err * (1.0 / D)
        dz = _ln_bwd_math(dy, xhat, rstd, g)
        return [dz, dz * 0.5, _colsum(dy * xhat), _colsum(dy), loss]

    return _rows(name, fn, [(z, 'r', D, 0), (g, 'v', D, 0), (b, 'v', D, 0), (tgt, 'r', D, 0)],
                 [('r', D, F32), ('r', D, BF16), ('a', D, F32), ('a', D, F32), ('a', 128, F32)], T=T, tb=512)


class _Net:
    def __init__(self, comm, G):
        self.comm, self.G = comm, G

    def gather(self, names, part=None):
        return self.comm.gather(names, part) if self.comm else None

    def exchange(self, names, part=None):
        return self.comm.exchange(names, self.G, part) if self.comm else None

    def to_sibling(self, names):
        return self.comm.to_sibling(names, self.G) if self.comm else None

    def pairsum(self, names):
        if self.comm:
            self.comm.pairsum(names)

    def to_owner(self, names, part=None):
        return self.comm.to_owner(names, part) if self.comm else None

    def done(self, rider):
        if rider is not None:
            self.comm.collect(rider)


def _ffn_fwd(tag, xb, x, W, names, net, rider=None, rider_down=None):
    def epi_gu(accs):
        a, b = accs
        return [a, b, a * _sigmoid(a) * b]

    ng, nu, nd = names
    a, b, s = _mm(tag + "_gate_up", [xb], [W[ng], W[nu]], [[(0, 0)], [(0, 1)]], epi_gu, [BF16, BF16, BF16],
                  tm=1024, tn=1408, tk=1024, rider=rider, epi_rows=256)
    net.done(rider)

    def epi_down(accs, xres):
        return [ALPHA * xres + 0.5 * accs[0]]

    rider_down = rider_down() if rider_down else None
    (z,) = _mm(tag + "_down", [s], [W[nd]], [[(0, 0)]], epi_down, [F32], tm=1024, tn=1024, tk=1408,
               extras=[(x, 'mn', 0)], rider=rider_down)
    net.done(rider_down)
    return a, b, s, z


def _ffn_bwd(tag, dzh, dz, xb, a, b, s, W, names, gdt, G, net, ride, pre=(None, None)):
    ng, nu, nd = names

    def epi_ds(accs, a, b):
        ds = accs[0]
        sg = _sigmoid(a)
        return [ds * b * _dsilu(a, sg), ds * a * sg]

    rider = pre[0]() if pre[0] else None
    da, db = _mm(tag + "_ds", [dzh], [W[nd]], [[(0, 0)]], epi_ds, [BF16, BF16], tb=True, tm=1024, tn=1408, tk=1024, epi_rows=256,
                 extras=[(a, 'mn', 0), (b, 'mn', 0)], rider=rider)
    net.done(rider)
    ident = lambda accs: accs
    rider = pre[1]() if pre[1] else None
    (G[nd],) = _mm(tag + "_dwd", [s], [dzh], [[(0, 0)]], ident, [gdt], ta=True, tm=1408, tn=1024, tk=1024,
                   rider=rider)
    net.done(rider)
    if ride == 'all':
        rider = net.to_sibling([nd])
        (G[ng],) = _mm(tag + "_dwg", [xb], [da], [[(0, 0)]], ident, [gdt], ta=True, tm=1024, tn=1408, tk=1024,
                       rider=rider)
        net.done(rider)
        net.pairsum([nd])
        rider = net.to_owner([nd])
        (G[nu],) = _mm(tag + "_dwu", [xb], [db], [[(0, 0)]], ident, [gdt], ta=True, tm=1024, tn=1408, tk=1024,
                       rider=rider)
        net.done(rider)
        rider = net.to_sibling([ng, nu])
        if rider is not None:
            _run_rider(tag + "_to_sibling", rider)
            net.done(rider)
        net.pairsum([ng, nu])
        rider = net.to_owner([ng, nu])
    else:
        rider = net.to_sibling([nd]) if ride else None
        G[ng], G[nu] = _mm(tag + "_dwgu", [xb], [da, db], [[(0, 0)], [(0, 1)]], ident, [gdt, gdt], ta=True,
                           tm=1024, tn=1408, tk=1024, rider=rider)
        net.done(rider)
        if ride:
            net.pairsum([nd])
        rider = net.to_owner([nd]) if ride else None

    def epi_dx(accs, dzres):
        return [ALPHA * dzres + accs[0]]

    (dx,) = _mm(tag + "_dx", [da, db], [W[ng], W[nu]], [[(0, 0), (1, 1)]], epi_dx, [F32], tb=True,
                tm=1024, tn=1024, tk=1408, extras=[(dz, 'mn', 0)], rider=rider)
    net.done(rider)
    return dx


def _ret_tables(H, T):
    C = RET_CHUNK
    log_g = jnp.log(1.0 - jnp.exp2(-5.0 - jnp.arange(H, dtype=F32)))
    idx = jnp.arange(C, dtype=F32)
    diff = idx[:, None] - idx[None, :]
    dm = jnp.where(diff[None] >= 0, jnp.exp(jnp.maximum(diff, 0.0)[None] * log_g[:, None, None]), 0.0)
    xi = jnp.exp((idx[None, :] + 1.0) * log_g[:, None])[:, :, None]
    zeta = jnp.exp((C - 1.0 - idx)[None, :] * log_g[:, None])[:, :, None]
    gc = jnp.broadcast_to(jnp.exp(C * log_g)[:, None, None], (H, 1, RET_DV))
    half = RET_DK // 2
    freqs = ROPE_BASE ** (-jnp.arange(half, dtype=F32) / half)
    ang = jnp.arange(T, dtype=F32)[:, None] * freqs[None, :]
    cos, sin = jnp.cos(ang), jnp.sin(ang)
    cosf = jnp.concatenate([cos, cos], axis=1)
    sins = jnp.concatenate([-sin, sin], axis=1)
    return dm, xi, zeta, gc, cosf, sins


def _rot(x, cosf, sins):
    return x * cosf + pltpu.roll(x, RET_DK // 2, 1) * sins


def _rot_bwd(dy, cosf, sins):
    return dy * cosf + pltpu.roll(dy * sins, RET_DK // 2, 1)


RET_HB = 8


def _ret_specs(H, HB, rev, NC):
    C, G = RET_CHUNK, H // HB
    nn = (lambda n: NC - 1 - n) if rev else (lambda n: n)
    return [
        pl.BlockSpec((C, HB * RET_DK), lambda h, n: (nn(n), h)),
        pl.BlockSpec((C, HB * RET_DK), lambda h, n: (nn(n), G + h)),
        pl.BlockSpec((C, HB * RET_DV), lambda h, n: (nn(n), G + h)),
        pl.BlockSpec((C, HB * RET_DV), lambda h, n: (nn(n), 2 * G + h)),
        pl.BlockSpec((C, RET_DK), lambda h, n: (nn(n), 0)),
        pl.BlockSpec((C, RET_DK), lambda h, n: (nn(n), 0)),
        pl.BlockSpec((1, HB * RET_DV), lambda h, n: (0, h)),
        pl.BlockSpec((HB, C, C), lambda h, n: (h, 0, 0)),
        pl.BlockSpec((HB, C, 1), lambda h, n: (h, 0, 0)),
        pl.BlockSpec((HB, C, 1), lambda h, n: (h, 0, 0)),
        pl.BlockSpec((HB, 1, RET_DV), lambda h, n: (h, 0, 0)),
    ]


def _ret_fwd(proj, gn_g, tabs, H, T, rider=None):
    C, NC = RET_CHUNK, T // RET_CHUNK
    HB = min(RET_HB, H)
    dm, xi, zeta, gc, cosf, sins = tabs
    scale = RET_DK ** -0.5

    def body(q_ref, k_ref, v_ref, g_ref, cos_ref, sin_ref, gn_ref, dm_ref, xi_ref, zt_ref, gc_ref,
             r_ref, ri_ref, st_ref, state):
        @pl.when(pl.program_id(1) == 0)
        def _():
            state[...] = jnp.zeros_like(state)

        cs, sn = cos_ref[...], sin_ref[...]
        hs = range(HB)
        qk = [slice(h * RET_DK, (h + 1) * RET_DK) for h in hs]
        vv = [slice(h * RET_DV, (h + 1) * RET_DV) for h in hs]
        kr = [_rot(k_ref[:, qk[h]].astype(F32), cs, sn) for h in hs]
        qb = [(_rot(q_ref[:, qk[h]].astype(F32), cs, sn) * scale).astype(BF16) for h in hs]
        kb = [kr[h].astype(BF16) for h in hs]
        kzb = [(kr[h] * zt_ref[h]).astype(BF16) for h in hs]
        vb = [v_ref[:, vv[h]].astype(BF16) for h in hs]
        st = [state[h] for h in hs]
        stb = [st[h].astype(BF16) for h in hs]
        sb = [(_dot(qb[h], kb[h], tb=True) * dm_ref[h]).astype(BF16) for h in hs]
        cross = [_dot(qb[h], stb[h]) for h in hs]
        kv = [_dot(kzb[h], vb[h], ta=True) for h in hs]
        intra = [_dot(sb[h], vb[h]) for h in hs]
        for h in hs:
            st_ref[h] = stb[h]
            state[h] = gc_ref[h] * st[h] + kv[h]
        for h in hs:
            r = intra[h] + cross[h] * xi_ref[h]
            rhat, _ = _ln_stats(r)
            g = g_ref[:, vv[h]].astype(F32)
            r_ref[:, vv[h]] = r
            ri_ref[:, vv[h]] = (g * _sigmoid(g) * (rhat * gn_ref[:, vv[h]])).astype(BF16)

    VW = H * RET_DV
    return _hosted_call(
        body, rider, name="ret_fwd", grid=(H // HB, NC), in_specs=_ret_specs(H, HB, False, NC),
        out_specs=[pl.BlockSpec((C, HB * RET_DV), lambda h, n: (n, h)),
                   pl.BlockSpec((C, HB * RET_DV), lambda h, n: (n, h)),
                   pl.BlockSpec((HB, None, RET_DK, RET_DV), lambda h, n: (h, n, 0, 0))],
        out_shape=[jax.ShapeDtypeStruct((T, VW), F32), jax.ShapeDtypeStruct((T, VW), BF16),
                   jax.ShapeDtypeStruct((H, NC, RET_DK, RET_DV), BF16)],
        scratch=[pltpu.VMEM((HB, RET_DK, RET_DV), F32)],
        args=[proj, proj, proj, proj, cosf, sins, gn_g, dm, xi, zeta, gc])


def _hosted_call(body, rider, *, name, grid, in_specs, out_specs, out_shape, scratch, args):
    n_in, n_out, n_scr = len(args), len(out_shape), len(scratch)
    if rider is None:
        hosted = body
    else:
        n_ri, n_ro = len(rider.ins), len(rider.out_shape)
        in_specs, out_specs = in_specs + [ANY] * n_ri, out_specs + [ANY] * n_ro
        args, out_shape, scratch = args + rider.ins, out_shape + rider.out_shape, scratch + rider.scratch

        def hosted(*refs):
            o0, s0 = n_in + n_ri, n_in + n_ri + n_out + n_ro
            step = pl.program_id(0) * grid[1] + pl.program_id(1)
            ride = (step, grid[0] * grid[1], refs[n_in:o0], refs[o0 + n_out:s0], refs[s0 + n_scr:])
            rider.begin(*ride)
            body(*refs[:n_in], *refs[o0:o0 + n_out], *refs[s0:s0 + n_scr])
            rider.end(*ride)

    aliases = {} if rider is None else {n_in + p: n_out + o for p, o in rider.aliases.items()}
    res = pl.pallas_call(
        hosted, name=name, grid=grid, in_specs=in_specs, out_specs=out_specs, out_shape=out_shape,
        scratch_shapes=scratch, input_output_aliases=aliases, compiler_params=_params(("arbitrary", "arbitrary")),
    )(*args)
    if rider is not None:
        rider.results = res[n_out:]
    return res[:n_out]


def _ret_bwd(dri, r, states, proj, gn_g, tabs, H, T, in_w, rider=None):
    C, NC = RET_CHUNK, T // RET_CHUNK
    HB = min(RET_HB, H)
    dm, xi, zeta, gc, cosf, sins = tabs
    scale = RET_DK ** -0.5

    def body(q_ref, k_ref, v_ref, g_ref, cos_ref, sin_ref, gn_ref, dm_ref, xi_ref, zt_ref, gc_ref,
             dri_ref, r_ref, st_ref, dp_ref, dgn_ref, dstate):
        @pl.when(pl.program_id(1) == 0)
        def _():
            dstate[...] = jnp.zeros_like(dstate)
            dgn_ref[...] = jnp.zeros_like(dgn_ref)

        cs, sn = cos_ref[...], sin_ref[...]
        hs = range(HB)
        qk = [slice(h * RET_DK, (h + 1) * RET_DK) for h in hs]
        vv = [slice(h * RET_DV, (h + 1) * RET_DV) for h in hs]
        qr = [_rot(q_ref[:, qk[h]].astype(F32), cs, sn) * scale for h in hs]
        kr = [_rot(k_ref[:, qk[h]].astype(F32), cs, sn) for h in hs]
        qb = [qr[h].astype(BF16) for h in hs]
        kb = [kr[h].astype(BF16) for h in hs]
        vb = [v_ref[:, vv[h]].astype(BF16) for h in hs]
        qxb = [(qr[h] * xi_ref[h]).astype(BF16) for h in hs]
        kzb = [(kr[h] * zt_ref[h]).astype(BF16) for h in hs]
        drb = []
        for h in hs:
            rhat, rstd = _ln_stats(r_ref[:, vv[h]])
            g, gn, dpre = g_ref[:, vv[h]].astype(F32), gn_ref[:, vv[h]], dri_ref[:, vv[h]]
            sg = _sigmoid(g)
            dp_ref[:, 2 * QW + VW + h * RET_DV:2 * QW + VW + (h + 1) * RET_DV] = (
                dpre * (rhat * gn) * _dsilu(g, sg)).astype(BF16)
            drn = dpre * (g * sg)
            dgn_ref[:, vv[h]] += _colsum(drn * rhat)
            drb.append(_ln_bwd_math(drn, rhat, rstd, gn).astype(BF16))
        ds1 = [dstate[h] for h in hs]
        ds1b = [ds1[h].astype(BF16) for h in hs]
        sb = [(_dot(qb[h], kb[h], tb=True) * dm_ref[h]).astype(BF16) for h in hs]
        dsb = [(_dot(drb[h], vb[h], tb=True) * dm_ref[h]).astype(BF16) for h in hs]
        dq_x = [_dot(drb[h], st_ref[h], tb=True) for h in hs]
        dk_x = [_dot(vb[h], ds1b[h], tb=True) for h in hs]
        dv_x = [_dot(kzb[h], ds1b[h]) for h in hs]
        dst = [_dot(qxb[h], drb[h], ta=True) for h in hs]
        for h in hs:
            dstate[h] = gc_ref[h] * ds1[h] + dst[h]
        dv_i = [_dot(sb[h], drb[h], ta=True) for h in hs]
        dq_i = [_dot(dsb[h], kb[h]) for h in hs]
        dk_i = [_dot(dsb[h], qb[h], ta=True) for h in hs]
        for h in hs:
            dp_ref[:, 2 * QW + h * RET_DV:2 * QW + (h + 1) * RET_DV] = (dv_i[h] + dv_x[h]).astype(BF16)
            dq = dq_i[h] + dq_x[h] * xi_ref[h]
            dk = dk_i[h] + dk_x[h] * zt_ref[h]
            dp_ref[:, qk[h]] = _rot_bwd(dq * scale, cs, sn).astype(BF16)
            dp_ref[:, QW + h * RET_DK:QW + (h + 1) * RET_DK] = _rot_bwd(dk, cs, sn).astype(BF16)

    VW, QW = H * RET_DV, H * RET_DK
    rv = lambda n: NC - 1 - n
    in_specs = _ret_specs(H, HB, True, NC) + [
        pl.BlockSpec((C, HB * RET_DV), lambda h, n: (rv(n), h)),
        pl.BlockSpec((C, HB * RET_DV), lambda h, n: (rv(n), h)),
        pl.BlockSpec((HB, None, RET_DK, RET_DV), lambda h, n: (h, rv(n), 0, 0)),
    ]
    assert HB == H
    return _hosted_call(
        body, rider, name="ret_bwd", grid=(1, NC), in_specs=in_specs,
        out_specs=[pl.BlockSpec((C, 2 * QW + 2 * VW), lambda h, n: (rv(n), 0)),
                   pl.BlockSpec((1, VW), lambda h, n: (0, 0))],
        out_shape=[jax.ShapeDtypeStruct((T, in_w), BF16), jax.ShapeDtypeStruct((1, VW), F32)],
        scratch=[pltpu.VMEM((HB, RET_DK, RET_DV), F32)],
        args=[proj, proj, proj, proj, cosf, sins, gn_g, dm, xi, zeta, gc, dri, r, states])


CONV_CW = 128
CONV_TB = 512


def _conv_fwd(proj, kpad, bias, off_a, CC, T, rider=None):
    tb, cw = min(CONV_TB, T), CONV_CW
    hb = tb // HALO
    ca, cb = off_a // cw, (off_a + CC) // cw

    def body(a_ref, b_ref, ap_ref, bp_ref, k_ref, bias_ref, u1_ref, win):
        i = pl.program_id(0)
        keep = (i > 0).astype(F32)
        win[0:HALO, :] = ap_ref[...].astype(F32) * _sigmoid(bp_ref[...].astype(F32)) * keep
        win[HALO:, :] = a_ref[...].astype(F32) * _sigmoid(b_ref[...].astype(F32))
        acc = jnp.broadcast_to(bias_ref[...], (tb, cw))
        for w in range(CONV_WIDTH):
            acc = acc + k_ref[w:w + 1, :] * win[pl.ds(HALO - (CONV_WIDTH - 1) + w, tb), :]
        u1_ref[...] = acc

    prev = lambda i: jnp.maximum(i * hb - 1, 0)
    (u1,) = _hosted_call(
        body, rider, name="conv_fwd", grid=(T // tb, CC // cw),
        in_specs=[pl.BlockSpec((tb, cw), lambda i, c: (i, ca + c)),
                  pl.BlockSpec((tb, cw), lambda i, c: (i, cb + c)),
                  pl.BlockSpec((HALO, cw), lambda i, c: (prev(i), ca + c)),
                  pl.BlockSpec((HALO, cw), lambda i, c: (prev(i), cb + c)),
                  pl.BlockSpec((HALO, cw), lambda i, c: (0, c)),
                  pl.BlockSpec((1, cw), lambda i, c: (0, c))],
        out_specs=[pl.BlockSpec((tb, cw), lambda i, c: (i, c))],
        out_shape=[jax.ShapeDtypeStruct((T, CC), F32)],
        scratch=[pltpu.VMEM((tb + HALO, cw), F32)],
        args=[proj, proj, proj, proj, kpad, bias])
    return u1


def _conv_bwd(du1, proj, kpad, off_a, CC, T, rider=None):
    tb, cw = min(CONV_TB, T), CONV_CW
    hb = tb // HALO
    nt = T // tb
    ca, cb = off_a // cw, (off_a + CC) // cw

    def body(d_ref, dn_ref, a_ref, b_ref, ap_ref, bp_ref, k_ref, da_ref, db_ref, dk_ref, winu, wind):
        i = pl.program_id(1)
        a, b = a_ref[...].astype(F32), b_ref[...].astype(F32)
        sgb = _sigmoid(b)
        winu[0:HALO, :] = ap_ref[...].astype(F32) * _sigmoid(bp_ref[...].astype(F32)) * (i > 0).astype(F32)
        winu[HALO:, :] = a * sgb
        d = d_ref[...]
        wind[0:tb, :] = d
        wind[tb:, :] = dn_ref[...] * (i < nt - 1).astype(F32)

        @pl.when(i == 0)
        def _():
            dk_ref[...] = jnp.zeros_like(dk_ref)

        du0 = jnp.zeros((tb, cw), F32)
        for w in range(CONV_WIDTH):
            du0 = du0 + k_ref[w:w + 1, :] * wind[pl.ds(CONV_WIDTH - 1 - w, tb), :]
            dk_ref[w:w + 1, :] += _colsum(winu[pl.ds(HALO - (CONV_WIDTH - 1) + w, tb), :] * d)
        da_ref[...] = (du0 * sgb).astype(BF16)
        db_ref[...] = (du0 * a * sgb * (1.0 - sgb)).astype(BF16)

    prev = lambda i: jnp.maximum(i * hb - 1, 0)
    nxt = lambda i: jnp.minimum((i + 1) * hb, T // HALO - 1)
    return _hosted_call(
        body, rider, name="conv_bwd", grid=(CC // cw, nt),
        in_specs=[pl.BlockSpec((tb, cw), lambda c, i: (i, c)),
                  pl.BlockSpec((HALO, cw), lambda c, i: (nxt(i), c)),
                  pl.BlockSpec((tb, cw), lambda c, i: (i, ca + c)),
                  pl.BlockSpec((tb, cw), lambda c, i: (i, cb + c)),
                  pl.BlockSpec((HALO, cw), lambda c, i: (prev(i), ca + c)),
                  pl.BlockSpec((HALO, cw), lambda c, i: (prev(i), cb + c)),
                  pl.BlockSpec((HALO, cw), lambda c, i: (0, c))],
        out_specs=[pl.BlockSpec((tb, cw), lambda c, i: (i, c)),
                   pl.BlockSpec((tb, cw), lambda c, i: (i, c)),
                   pl.BlockSpec((HALO, cw), lambda c, i: (0, c))],
        out_shape=[jax.ShapeDtypeStruct((T, CC), BF16), jax.ShapeDtypeStruct((T, CC), BF16),
                   jax.ShapeDtypeStruct((HALO, CC), F32)],
        scratch=[pltpu.VMEM((tb + HALO, cw), F32), pltpu.VMEM((tb + HALO, cw), F32)],
        args=[du1, du1, proj, proj, proj, proj, kpad])


FFN1 = ('ffn1_w_gate', 'ffn1_w_up', 'ffn1_w_down')
FFN2 = ('ffn2_w_gate', 'ffn2_w_up', 'ffn2_w_down')


def _local_step(x, tgt, W, P, gdt=BF16, comm=None):
    T, D = x.shape
    G = {}
    net = _Net(comm, G)
    if comm is not None:
        W = comm.W
        first = net.gather(['ffn1_w_gate', 'ffn1_w_up'])
    (xb,) = _rows("x_to_bf16", lambda v: [v], [(x, 'r', D, 0)], [('r', D, BF16)], T=T, tb=512,
                  rider=first if comm is not None else None)
    if comm is not None:
        net.done(first)
    VW = P['ret_gn_g'].shape[1]
    H = VW // RET_DV
    QW = H * RET_DK
    CC = P['conv_b'].shape[1]
    off_glu = 2 * QW + 2 * VW
    off_gate = off_glu + 2 * CC
    ident = lambda accs: accs

    a1, b1, s1, z1 = _ffn_fwd("ffn1", xb, x, W, FFN1, net,
                              rider=net.gather(['ffn1_w_down', 'w_in'], {'w_in': (0, D // 4, False)}),
                              rider_down=lambda: net.gather(['w_in'], {'w_in': (D // 4, (3 * D) // 8, False)}))
    rider = net.gather(['w_in'], {'w_in': ((5 * D) // 8, (3 * D) // 8, True)})
    x1, x1b, xh1, rs1 = _ln_fwd("ln1", z1, P['ln1_g'], P['ln1_b'], T, D, rider=rider)
    net.done(rider)

    rest = net.gather(['conv_k', 'w_ret_o', 'w_conv_o', 'w_out'])
    (proj,) = _mm("w_in", [x1b], [W['w_in']], [[(0, 0)]], lambda accs, bias: [accs[0] + bias], [F32],
                  tm=2048, tn=0, tk=1024, extras=[(P['b_in'], 'n', 0)], i_outer=False, b3=True, rider=rest)
    net.done(rest)
    tabs = _ret_tables(H, T)
    rider = net.gather(['ffn2_w_gate', 'ffn2_w_up'])
    r, ret_in, states = _ret_fwd(proj, P['ret_gn_g'], tabs, H, T, rider=rider)
    net.done(rider)
    kpad = jnp.pad(W['conv_k'].astype(F32), ((0, HALO - CONV_WIDTH), (0, 0)))
    rider = net.gather(['ffn2_w_down'])
    u1 = _conv_fwd(proj, kpad, P['conv_b'], off_glu, CC, T, rider=rider)
    net.done(rider)

    def conv_ln(u1, g, b):
        xhat, rstd = _ln_stats(u1)
        u2 = xhat * g + b
        return [xhat, rstd, u2 * _sigmoid(u2)]

    xhc, rsc, u3 = _rows("conv_ln", conv_ln, [(u1, 'r', CC, 0), (P['conv_ln_g'], 'v', CC, 0), (P['conv_ln_b'], 'v', CC, 0)],
                         [('r', CC, F32), ('c', 1, F32), ('r', CC, BF16)], T=T, tb=512)
    (ret_out,) = _mm("ret_o", [ret_in], [W['w_ret_o']], [[(0, 0)]], ident, [F32], tm=1024, tn=1024, tk=2048)

    def epi_merge(accs, ret_out, gr, gc):
        conv_out = accs[0]
        return [conv_out, _sigmoid(gr) * ret_out + _sigmoid(gc) * conv_out]

    conv_out, merged = _mm("conv_o_merge", [u3], [W['w_conv_o']], [[(0, 0)]], epi_merge, [F32, BF16],
                           tm=512, tn=D, tk=1024, epi_rows=256,
                           extras=[(ret_out, 'mn', 0), (proj, 'mn', off_gate), (proj, 'mn', off_gate + D)])
    (z2,) = _mm("w_out", [merged], [W['w_out']], [[(0, 0)]], lambda accs, xr: [ALPHA * xr + accs[0]], [F32],
                tm=1024, tn=1024, tk=1024, extras=[(x1, 'mn', 0)])
    x2, x2b, xh2, rs2 = _ln_fwd("ln2", z2, P['ln2_g'], P['ln2_b'], T, D)
    a2, b2, s2, z3 = _ffn_fwd("ffn2", x2b, x2, W, FFN2, net)
    dz3, dz3h, g_ln3_g, g_ln3_b, loss = _ln_loss_bwd("ln3_loss", z3, P['ln3_g'], P['ln3_b'], tgt, T, D)

    S = {'ln3_g': g_ln3_g, 'ln3_b': g_ln3_b}
    dy2 = _ffn_bwd("ffn2b", dz3h, dz3, x2b, a2, b2, s2, W, FFN2, gdt, G, net, 'down')
    rider = net.to_sibling(['ffn2_w_gate', 'ffn2_w_up'])
    dz2, dz2b, S['ln2_g'], S['ln2_b'] = _ln_bwd("ln2b", dy2, xh2, rs2, P['ln2_g'], 1.0, T, D, rider=rider)
    net.done(rider)
    net.pairsum(['ffn2_w_gate', 'ffn2_w_up'])

    (G['w_out'],) = _mm("d_w_out", [merged], [dz2b], [[(0, 0)]], ident, [gdt], ta=True, tm=1024, tn=1024, tk=1024)

    def epi_dmerge(accs, ret_out, conv_out, gr, gc):
        dm_ = accs[0]
        sr, sc = _sigmoid(gr), _sigmoid(gc)
        return [dm_ * sr, dm_ * sc, dm_ * ret_out * sr * (1.0 - sr), dm_ * conv_out * sc * (1.0 - sc)]

    rider = net.to_sibling(['w_out'])
    dret_out, dconv_out, dgate_r, dgate_c = _mm(
        "d_merge", [dz2b], [W['w_out']], [[(0, 0)]], epi_dmerge, [BF16, BF16, BF16, BF16], tb=True,
        tm=512, tn=D, tk=1024, epi_rows=256, rider=rider,
        extras=[(ret_out, 'mn', 0), (conv_out, 'mn', 0), (proj, 'mn', off_gate), (proj, 'mn', off_gate + D)])
    net.done(rider)
    (G['w_ret_o'],) = _mm("d_w_ret_o", [ret_in], [dret_out], [[(0, 0)]], ident, [gdt], ta=True, tm=1024, tn=1024, tk=1024)
    (G['w_conv_o'],) = _mm("d_w_conv_o", [u3], [dconv_out], [[(0, 0)]], ident, [gdt], ta=True, tm=1024, tn=1024, tk=1024)
    rider = net.to_sibling(['w_ret_o', 'w_conv_o'])
    (dri,) = _mm("d_ret_in", [dret_out], [W['w_ret_o']], [[(0, 0)]], ident, [F32], tb=True, tm=1024, tn=1024, tk=1024,
                 rider=rider)
    net.done(rider)
    net.pairsum(['w_out', 'w_ret_o', 'w_conv_o'])
    rider = net.to_owner(['ffn2_w_gate', 'ffn2_w_up'])
    dproj, S['ret_gn_g'] = _ret_bwd(dri, r, states, proj, P['ret_gn_g'], tabs, H, T, proj.shape[1], rider=rider)
    net.done(rider)

    def epi_du2(accs, xhat, g, b):
        u2 = xhat * g + b
        return [accs[0] * _dsilu(u2, _sigmoid(u2))]

    (du2,) = _mm("d_u3", [dconv_out], [W['w_conv_o']], [[(0, 0)]], epi_du2, [F32], tb=True, tm=512, tn=CC, tk=1024, epi_rows=256,
                 extras=[(xhc, 'mn', 0), (P['conv_ln_g'], 'n', 0), (P['conv_ln_b'], 'n', 0)])

    def conv_ln_bwd(du2, xhat, rstd, g):
        du1 = _ln_bwd_math(du2, xhat, rstd, g)
        return [du1, _colsum(du2 * xhat), _colsum(du2), _colsum(du1)]

    du1, S['conv_ln_g'], S['conv_ln_b'], S['conv_b'] = _rows(
        "conv_ln_bwd", conv_ln_bwd, [(du2, 'r', CC, 0), (xhc, 'r', CC, 0), (rsc, 'r', 1, 0), (P['conv_ln_g'], 'v', CC, 0)],
        [('r', CC, F32), ('a', CC, F32), ('a', CC, F32), ('a', CC, F32)], T=T, tb=512)
    rider = net.to_owner(['w_out', 'w_ret_o', 'w_conv_o'])
    dglu_a, dglu_b, dkpad = _conv_bwd(du1, proj, kpad, off_glu, CC, T, rider=rider)
    net.done(rider)
    G['conv_k'] = dkpad[:CONV_WIDTH].astype(gdt)

    for off, piece in ((off_glu, dglu_a), (off_glu + CC, dglu_b), (off_gate, dgate_r), (off_gate + D, dgate_c)):
        dproj = lax.dynamic_update_slice(dproj, piece, (0, off))
    IN_W = dproj.shape[1]
    rider = net.exchange(['conv_k'])
    G['w_in'], S['b_in'] = _mm("d_w_in", [x1b], [dproj], [[(0, 0)]], ident, [gdt], ta=True, o3=True, bsum=True,
                               tm=1024, tn=W['w_in'].shape[2], tk=1024, rider=rider)
    net.done(rider)
    cuts = [0, (3 * D) // 16, (43 * D) // 64, D]
    w_in_rows = [{'w_in': (cuts[i], cuts[i + 1] - cuts[i], i == 2)} for i in range(3)]
    rider = net.to_sibling(['w_in'])
    (dy1,) = _mm("d_x1", [dproj], [W['w_in']], [[(0, 0)]], lambda accs, dzr: [ALPHA * dzr + accs[0]], [F32], tb=True,
                 b3=True, tm=1024, tn=1024, tk=0, extras=[(dz2, 'mn', 0)], rider=rider)
    net.done(rider)
    net.pairsum(['w_in'])
    rider = net.to_owner(['w_in'], w_in_rows[0])
    dz1, dz1h, S['ln1_g'], S['ln1_b'] = _ln_bwd("ln1b", dy1, xh1, rs1, P['ln1_g'], 0.5, T, D, rider=rider)
    net.done(rider)
    grad_x = _ffn_bwd("ffn1b", dz1h, dz1, xb, a1, b1, s1, W, FFN1, gdt, G, net, 'all',
                      pre=(lambda: net.to_owner(['w_in'], w_in_rows[1]), lambda: net.to_owner(['w_in'], w_in_rows[2])))
    return loss, grad_x, G, S


def _coords():
    return lax.axis_index("x"), lax.axis_index("y"), lax.axis_index("c")


def _flip(k, x, y, c):
    return (1 - x if k & 4 else x, 1 - y if k & 2 else y, 1 - c if k & 1 else c)


def _lin(p):
    return 4 * p[0] + 2 * p[1] + p[2]


class _Rider:
    def __init__(self, ins, out_shape, rows=None, fill=None):
        nb = len(ins)
        self.rows = rows or [None] * nb
        fill = fill or [None] * nb
        self.aliases = {nb + i: w for i, w in enumerate(w for w in range(nb) if fill[w] is not None)}
        self.ins = list(ins) + [f for f in fill if f is not None]
        self.out_shape, self.results = list(out_shape), None
        self.scratch = [pltpu.SemaphoreType.DMA((8 * nb,)), pltpu.SemaphoreType.DMA((8 * nb,)),
                        pltpu.SemaphoreType.DMA((nb,))]

    def span(self, w, ref, *slot, half=None):
        rows = self.rows[w]
        if half is not None:
            first, count = rows if rows is not None else (0, self.out_shape[w].shape[1])
            rows = (first + half * (count // 2), count // 2)
        if rows is None:
            return ref.at[slot] if slot else ref
        return ref.at[(*slot, pl.ds(*rows))]

    def begin(self, step, n_steps, ins, outs, sems):
        @pl.when(step == 0)
        def _():
            self.start(ins, outs, sems)

        @pl.when(step == min(n_steps - 1, (5 * n_steps) // 8))
        def _():
            self.relay(ins, outs, sems)

        @pl.when(step == n_steps - 1)
        def _():
            self.mid(ins, outs, sems)

    def end(self, step, n_steps, ins, outs, sems):
        @pl.when(step == n_steps - 1)
        def _():
            self.finish(ins, outs, sems)

    def relay(self, ins, outs, sems):
        pass

    def mid(self, ins, outs, sems):
        pass


class _GatherRider(_Rider):
    def __init__(self, blks, rows=None, fill=None):
        super().__init__(blks, [jax.ShapeDtypeStruct((N_DEV,) + b.shape, b.dtype) for b in blks], rows, fill)
        counts = [(r[1] if r is not None else b.shape[0]) for r, b in zip(self.rows, blks)]
        self.halves = [n % 32 == 0 for n in counts]

    def _copies(self, x_refs, out_refs, sems):
        nb = len(self.out_shape)
        send_sems, recv_sems, local_sems = sems
        x, y, c = _coords()
        me, sib = (x, y, c), (x, y, 1 - c)
        xn, yn, dg = _flip(4, x, y, c), _flip(2, x, y, c), _flip(6, x, y, c)
        plans = []
        for w in range(nb):
            own = self.span(w, x_refs[w])

            def copy(k, block, to, src=None, half=None, w=w):
                slot = self.span(w, out_refs[w], _lin(block), half=half)
                return pltpu.make_async_remote_copy(
                    src_ref=slot if src is None else src, dst_ref=slot, send_sem=send_sems.at[k * nb + w],
                    recv_sem=recv_sems.at[k * nb + w], device_id=to, device_id_type=MESH)

            mine = pltpu.make_async_copy(own, self.span(w, out_refs[w], _lin(me)), local_sems.at[w])
            first = [copy(0, me, sib, src=own), copy(1, me, xn, src=own), copy(2, me, yn, src=own)]
            if self.halves[w]:
                relay = [(copy(1, xn, me), [copy(3, xn, yn, half=0), copy(5, xn, sib)]),
                         (copy(2, yn, me), [copy(4, yn, xn, half=1), copy(6, yn, sib)])]
                last = [(copy(3, dg, me, half=0), []), (copy(4, dg, me, half=1), [copy(7, dg, sib)])]
            else:
                first.append(copy(3, me, dg, src=own))
                relay = [(copy(1, xn, me), [copy(5, xn, sib)]), (copy(2, yn, me), [copy(6, yn, sib)])]
                last = [(copy(3, dg, me), [copy(7, dg, sib)])]
            other = lambda p: (p[0], p[1], 1 - c)
            from_sib = [copy(0, sib, me), copy(5, other(xn), me), copy(6, other(yn), me), copy(7, other(dg), me)]
            plans.append((mine, first, relay, last, from_sib))
        return plans

    def start(self, ins, outs, sems):
        for mine, first, _, _, _ in self._copies(ins, outs, sems):
            for cp in [mine] + first:
                cp.start()

    def relay(self, ins, outs, sems):
        for _, _, relay, _, _ in self._copies(ins, outs, sems):
            for arrival, released in relay:
                arrival.wait_recv()
                for cp in released:
                    cp.start()

    def mid(self, ins, outs, sems):
        for _, _, _, last, _ in self._copies(ins, outs, sems):
            for arrival, released in last:
                arrival.wait_recv()
                for cp in released:
                    cp.start()

    def finish(self, ins, outs, sems):
        for mine, first, relay, last, from_sib in self._copies(ins, outs, sems):
            for cp in from_sib:
                cp.wait_recv()
            for cp in first + [cp for _, released in relay + last for cp in released]:
                cp.wait_send()
            mine.wait()


class _ExchangeRider(_Rider):
    def __init__(self, gs, rows=None, fill=None):
        super().__init__(gs, [jax.ShapeDtypeStruct(g.shape, g.dtype) for g in gs], rows, fill)

    def _copies(self, g_refs, out_refs, sems):
        nb = len(self.out_shape)
        send_sems, recv_sems, local_sems = sems
        x, y, c = _coords()
        me = _lin((x, y, c))

        def copy(k, w, landing):
            peer = _flip(k, x, y, c)
            src, dst = (me, _lin(peer)) if landing else (_lin(peer), me)
            return pltpu.make_async_remote_copy(
                src_ref=self.span(w, g_refs[w], src), dst_ref=self.span(w, out_refs[w], dst),
                send_sem=send_sems.at[(k - 1) * nb + w], recv_sem=recv_sems.at[(k - 1) * nb + w],
                device_id=peer, device_id_type=MESH)

        mines = [pltpu.make_async_copy(self.span(w, g_refs[w], me), self.span(w, out_refs[w], me), local_sems.at[w])
                 for w in range(nb)]
        sends = [copy(k, w, False) for w in range(nb) for k in range(1, N_DEV)]
        landings = [copy(k, w, True) for w in range(nb) for k in range(1, N_DEV)]
        return mines, sends, landings

    def start(self, ins, outs, sems):
        mines, sends, _ = self._copies(ins, outs, sems)
        for cp in mines + sends:
            cp.start()

    def finish(self, ins, outs, sems):
        mines, sends, landings = self._copies(ins, outs, sems)
        for cp in landings:
            cp.wait_recv()
        for cp in sends:
            cp.wait_send()
        for mine in mines:
            mine.wait()


class _SiblingRider(_Rider):
    def __init__(self, gs):
        super().__init__(gs, [jax.ShapeDtypeStruct((4,) + g.shape[1:], g.dtype) for g in gs])

    def _copies(self, g_refs, out_refs, sems, landing):
        nb = len(self.out_shape)
        send_sems, recv_sems, _ = sems
        x, y, c = _coords()
        whose = c if landing else 1 - c
        return [pltpu.make_async_remote_copy(
            src_ref=g_refs[w].at[2 * q + whose], dst_ref=out_refs[w].at[q], send_sem=send_sems.at[q * nb + w],
            recv_sem=recv_sems.at[q * nb + w], device_id=(x, y, 1 - c), device_id_type=MESH)
            for w in range(nb) for q in range(4)]

    def start(self, ins, outs, sems):
        for cp in self._copies(ins, outs, sems, False):
            cp.start()

    def finish(self, ins, outs, sems):
        for cp in self._copies(ins, outs, sems, True):
            cp.wait_recv()
        for cp in self._copies(ins, outs, sems, False):
            cp.wait_send()


class _ChipRider(_Rider):
    FLIPS = (4, 2, 6)

    def __init__(self, ps, rows=None, fill=None):
        super().__init__(ps, [jax.ShapeDtypeStruct(p.shape, p.dtype) for p in ps], rows, fill)

    def _copies(self, p_refs, out_refs, sems):
        nb = len(self.out_shape)
        send_sems, recv_sems, local_sems = sems
        x, y, c = _coords()
        my_chip = 2 * x + y

        def copy(j, w, landing):
            peer = _flip(self.FLIPS[j], x, y, c)
            peer_chip = 2 * peer[0] + peer[1]
            src, dst = (my_chip, peer_chip) if landing else (peer_chip, my_chip)
            return pltpu.make_async_remote_copy(
                src_ref=self.span(w, p_refs[w], src), dst_ref=self.span(w, out_refs[w], dst),
                send_sem=send_sems.at[j * nb + w], recv_sem=recv_sems.at[j * nb + w],
                device_id=peer, device_id_type=MESH)

        mines = [pltpu.make_async_copy(self.span(w, p_refs[w], my_chip), self.span(w, out_refs[w], my_chip),
                                       local_sems.at[w]) for w in range(nb)]
        sends = [copy(j, w, False) for w in range(nb) for j in range(3)]
        landings = [copy(j, w, True) for w in range(nb) for j in range(3)]
        return mines, sends, landings

    def start(self, ins, outs, sems):
        mines, sends, _ = self._copies(ins, outs, sems)
        for cp in mines + sends:
            cp.start()

    def finish(self, ins, outs, sems):
        mines, sends, landings = self._copies(ins, outs, sems)
        for cp in landings:
            cp.wait_recv()
        for cp in sends:
            cp.wait_send()
        for mine in mines:
            mine.wait()


def _pairsum(name, g, land):
    _, r, cols = g.shape
    tb = r if r % 16 else _row_tile(r, 16, max(16, (1024 * 1024) // cols))
    core = lax.axis_index("c").astype(jnp.int32).reshape(1)

    def body(core_ref, g_ref, l_ref, o_ref):
        o_ref[...] = (g_ref[...].astype(F32) + l_ref[...].astype(F32)).astype(o_ref.dtype)

    return pl.pallas_call(
        body, name=name, out_shape=jax.ShapeDtypeStruct((4, r, cols), g.dtype),
        grid_spec=pltpu.PrefetchScalarGridSpec(
            num_scalar_prefetch=1, grid=(4, r // tb),
            in_specs=[pl.BlockSpec((None, None, tb, cols), lambda q, i, core_ref: (q, core_ref[0], i, 0)),
                      pl.BlockSpec((None, tb, cols), lambda q, i, core_ref: (q, i, 0))],
            out_specs=pl.BlockSpec((None, tb, cols), lambda q, i, core_ref: (q, i, 0))),
        compiler_params=_params(("arbitrary", "arbitrary")),
    )(core, g.reshape(4, 2, r, cols), land)


def _run_rider(name, rider):
    n_in, n_out = len(rider.ins), len(rider.out_shape)

    def body(*refs):
        ride = (refs[:n_in], refs[n_in:n_in + n_out], refs[n_in + n_out:])
        rider.start(*ride)
        rider.relay(*ride)
        rider.mid(*ride)
        rider.finish(*ride)

    rider.results = pl.pallas_call(
        body, name=name, out_shape=rider.out_shape, in_specs=[ANY] * n_in, out_specs=[ANY] * n_out,
        scratch_shapes=rider.scratch, input_output_aliases=dict(rider.aliases),
        compiler_params=pltpu.CompilerParams(has_side_effects=True),
    )(*rider.ins)
    return rider.results


def _as_matrix(name, g):
    if name == 'w_in':
        return g
    if name in COL_SHARDED:
        return jnp.transpose(g, (1, 0, 2)).reshape(g.shape[1], N_DEV * g.shape[2])
    return g.reshape(N_DEV * g.shape[1], g.shape[2])


def _by_owner(name, g):
    if name == 'w_in':
        return g
    if name in COL_SHARDED:
        return jnp.transpose(g.reshape(g.shape[0], N_DEV, g.shape[1] // N_DEV), (1, 0, 2))
    return g.reshape(N_DEV, g.shape[0] // N_DEV, g.shape[1])


class _Comm:
    def __init__(self, shards):
        self.shards, self.W, self.parts, self.partial, self.sent = shards, {}, {}, {}, {}
        self.from_sibling, self.pairs = {}, {}

    def _ride(self, cls, names, srcs, part, sink):
        part = part or {}
        rider = cls(srcs, rows=[part[n][:2] if n in part else None for n in names],
                    fill=[self.partial.pop((sink, n), None) for n in names])
        rider.names, rider.sink = names, sink
        rider.unfinished = {n for n in names if n in part and not part[n][2]}
        return rider

    def gather(self, names, part=None):
        return self._ride(_GatherRider, names, [self.shards[n] for n in names], part, 'W')

    def exchange(self, names, G, part=None):
        for n in names:
            if n not in self.sent:
                self.sent[n] = _by_owner(n, G[n])
        return self._ride(_ExchangeRider, names, [self.sent[n] for n in names], part, 'parts')

    def to_sibling(self, names, G):
        for n in names:
            self.sent[n] = _by_owner(n, G[n])
        rider = _SiblingRider([self.sent[n] for n in names])
        rider.names, rider.sink, rider.unfinished = names, 'sibling', set()
        return rider

    def pairsum(self, names):
        for n in names:
            self.pairs[n] = _pairsum("pairsum_" + n, self.sent[n], self.from_sibling.pop(n))

    def to_owner(self, names, part=None):
        return self._ride(_ChipRider, names, [self.pairs[n] for n in names], part, 'parts')

    def collect(self, rider):
        for n, res in zip(rider.names, rider.results):
            if n in rider.unfinished:
                self.partial[(rider.sink, n)] = res
            elif rider.sink == 'W':
                self.W[n] = _as_matrix(n, res)
            elif rider.sink == 'sibling':
                self.from_sibling[n] = res
            else:
                self.parts[n] = res


def _adamw_math(p_ref, w_ref, m_ref, v_ref, g_ref, d_ref, nm_ref, nv_ref):
    c1 = 1.0 - ADAM_B1 ** ADAM_STEP
    c2 = 1.0 - ADAM_B2 ** ADAM_STEP
    g = p_ref[0].astype(F32)
    for s in range(1, p_ref.shape[0]):
        g = g + p_ref[s].astype(F32)
    nm = ADAM_B1 * m_ref[...] + (1.0 - ADAM_B1) * g
    nv = ADAM_B2 * v_ref[...] + (1.0 - ADAM_B2) * (g * g)
    g_ref[...] = g
    nm_ref[...] = nm
    nv_ref[...] = nv
    d_ref[...] = -ADAM_LR * ((nm / c1) / (jnp.sqrt(nv / c2) + ADAM_EPS) + ADAM_WD * w_ref[...])


def _adamw_vectors(parts, ws, ms, vs, loss_parts):
    k = len(ws)

    def body(*refs):
        for i in range(k):
            _adamw_math(refs[i], refs[k + i], refs[2 * k + i], refs[3 * k + i], *refs[4 * k + 1 + 4 * i:4 * k + 5 + 4 * i])
        lp, lo = refs[4 * k], refs[8 * k + 1]
        lo[...] = functools.reduce(jnp.add, [lp[s] for s in range(lp.shape[0])])

    return pl.pallas_call(
        body, name="adamw_vectors",
        out_shape=[jax.ShapeDtypeStruct(w.shape, F32) for w in ws for _ in range(4)] + [jax.ShapeDtypeStruct((1, 128), F32)],
        compiler_params=_params(),
    )(*parts, *ws, *ms, *vs, loss_parts)


def _adamw(name, parts, w, m, v, tb):
    n, R, Wd = parts.shape
    assert R % tb == 0
    body = functools.partial(_adamw_math)

    row = pl.BlockSpec((tb, Wd), lambda i: (i, 0))
    return pl.pallas_call(
        body, name=name, grid=(R // tb,),
        in_specs=[pl.BlockSpec((n, tb, Wd), lambda i: (0, i, 0)), row, row, row],
        out_specs=[row, row, row, row], out_shape=[jax.ShapeDtypeStruct((R, Wd), F32)] * 4,
        compiler_params=_params(("arbitrary",)),
    )(parts, w, m, v)


def _row_tile(R, unit, cap):
    best = unit
    for t in range(unit, cap + 1, unit):
        if R % t == 0:
            best = t
    return best


def kernel(x, ffn1_w_gate, ffn1_w_up, ffn1_w_down, ln1_g, ln1_b, w_in, b_in, ret_gn_g, conv_k, conv_b, conv_ln_g, conv_ln_b, w_ret_o, w_conv_o, w_out, ln2_g, ln2_b, ffn2_w_gate, ffn2_w_up, ffn2_w_down, ln3_g, ln3_b, loss_target, m_ffn1_w_gate, m_ffn1_w_up, m_ffn1_w_down, m_ln1_g, m_ln1_b, m_w_in, m_b_in, m_ret_gn_g, m_conv_k, m_conv_b, m_conv_ln_g, m_conv_ln_b, m_w_ret_o, m_w_conv_o, m_w_out, m_ln2_g, m_ln2_b, m_ffn2_w_gate, m_ffn2_w_up, m_ffn2_w_down, m_ln3_g, m_ln3_b, v_ffn1_w_gate, v_ffn1_w_up, v_ffn1_w_down, v_ln1_g, v_ln1_b, v_w_in, v_b_in, v_ret_gn_g, v_conv_k, v_conv_b, v_conv_ln_g, v_conv_ln_b, v_w_ret_o, v_w_conv_o, v_w_out, v_ln2_g, v_ln2_b, v_ffn2_w_gate, v_ffn2_w_up, v_ffn2_w_down, v_ln3_g, v_ln3_b):
    given = dict(locals())
    wts = {n: given[n] for n in WEIGHTS}
    mom = {n: given['m_' + n] for n in WEIGHTS}
    var = {n: given['v_' + n] for n in WEIGHTS}

    def shard2d(a):
        return a.reshape(a.shape[-3] * a.shape[-2] if a.ndim == 4 else a.shape[-2], a.shape[-1])

    comm = _Comm({n: shard2d(wts[n]).astype(BF16) for n in BIG})
    P = {n: wts[n].reshape(1, -1) for n in SMALL}
    loss, grad_x, _, S = _local_step(x[0], loss_target[0], None, P, comm=comm)

    parts = comm.parts
    res = {}
    for n in BIG:
        rows, cols = parts[n].shape[1:]
        tb = rows if rows % 16 else _row_tile(rows, 16, max(16, (256 * 1024) // cols))
        res[n] = _adamw("adamw_" + n, parts[n], shard2d(wts[n]), shard2d(mom[n]), shard2d(var[n]), tb)

    vec_parts = _run_rider("gather_vector_grads", _GatherRider([S[n] for n in SMALL] + [loss]))
    vec = _adamw_vectors(vec_parts[:-1], [P[n] for n in SMALL], [mom[n].reshape(1, -1) for n in SMALL],
                         [var[n].reshape(1, -1) for n in SMALL], vec_parts[-1])
    for i, n in enumerate(SMALL):
        res[n] = vec[4 * i:4 * i + 4]

    outs = [vec[-1][0, 0], grad_x[None]]
    for k in range(4):
        for n in WEIGHTS:
            outs.append(res[n][k].reshape(wts[n].shape))
    return tuple(outs)
```

```python
import functools
import math

import jax
import jax.numpy as jnp
from jax import lax
from jax.experimental import pallas as pl
from jax.experimental.pallas import tpu as pltpu

F32 = jnp.float32
BF16 = jnp.bfloat16

N_DEV = 8
LN_EPS = 1e-5
ALPHA = 2.0 ** 0.25
RET_DK = 128
RET_DV = 256
RET_CHUNK = 256
ROPE_BASE = 10000.0
CONV_WIDTH = 31
HALO = 32
ADAM_LR, ADAM_B1, ADAM_B2, ADAM_EPS, ADAM_WD, ADAM_STEP = 0.001, 0.9, 0.999, 1e-08, 0.01, 10
VMEM_LIMIT = 52 * 1024 * 1024
MESH = pl.DeviceIdType.MESH
ANY = pl.BlockSpec(memory_space=pl.ANY)

BIG = ['ffn1_w_gate', 'ffn1_w_up', 'ffn1_w_down', 'w_in', 'w_ret_o', 'w_conv_o', 'w_out',
       'ffn2_w_gate', 'ffn2_w_up', 'ffn2_w_down', 'conv_k']
COL_SHARDED = {'ffn1_w_gate', 'ffn1_w_up', 'w_in', 'ffn2_w_gate', 'ffn2_w_up', 'conv_k'}
SMALL = ['ln1_g', 'ln1_b', 'b_in', 'ret_gn_g', 'conv_b', 'conv_ln_g', 'conv_ln_b', 'ln2_g', 'ln2_b', 'ln3_g', 'ln3_b']
WEIGHTS = ['ffn1_w_gate', 'ffn1_w_up', 'ffn1_w_down', 'ln1_g', 'ln1_b', 'w_in', 'b_in', 'ret_gn_g', 'conv_k', 'conv_b',
           'conv_ln_g', 'conv_ln_b', 'w_ret_o', 'w_conv_o', 'w_out', 'ln2_g', 'ln2_b', 'ffn2_w_gate', 'ffn2_w_up',
           'ffn2_w_down', 'ln3_g', 'ln3_b']


def _params(sem=None):
    return pltpu.CompilerParams(dimension_semantics=sem, vmem_limit_bytes=VMEM_LIMIT)


def _sigmoid(x):
    return jax.nn.sigmoid(x)


def _dsilu(x, sg):
    return sg * (1.0 + x * (1.0 - sg))


def _fit(dim, want):
    if dim <= want:
        return dim
    return max(t for t in range(128, want + 1, 128) if dim % t == 0)


def _dot(a, b, ta=False, tb=False):
    dn = (((0,) if ta else (1,), (1,) if tb else (0,)), ((), ()))
    return lax.dot_general(a, b, dn, preferred_element_type=F32)


def _mm(name, As, Bs, prods, epi, out_dtypes, *, ta=False, tb=False, tm, tn, tk, extras=(), i_outer=True,
        b3=False, o3=False, rider=None, bsum=False, epi_rows=0):
    a0, b0 = As[0], Bs[0]
    M, K = (a0.shape[1], a0.shape[0]) if ta else a0.shape
    if b3:
        S, rows, cs = b0.shape
        N = rows if tb else S * cs
        assert K == (S * cs if tb else rows)
        tn, tk = (tn, cs) if tb else (cs, tk)
    else:
        N = b0.shape[0] if tb else b0.shape[1]
    tm, tn, tk = _fit(M, tm), _fit(N, tn), _fit(K, tk)
    assert M % tm == 0 and N % tn == 0 and K % tk == 0, (name, M, N, K, tm, tn, tk)
    gi, gj, gk = M // tm, N // tn, K // tk
    grid = (gi, gj, gk) if i_outer else (gj, gi, gk)

    def ij(g0, g1):
        return (g0, g1) if i_outer else (g1, g0)

    def amap(g0, g1, k):
        i, _ = ij(g0, g1)
        return (k, i) if ta else (i, k)

    def bmap(g0, g1, k):
        _, j = ij(g0, g1)
        return (j, k) if tb else (k, j)

    def bmap3(g0, g1, k):
        _, j = ij(g0, g1)
        return (k, j, 0) if tb else (j, k, 0)

    in_specs = [pl.BlockSpec((tk, tm) if ta else (tm, tk), amap) for _ in As]
    if b3:
        in_specs += [pl.BlockSpec((None, tn, tk) if tb else (None, tk, tn), bmap3) for _ in Bs]
    else:
        in_specs += [pl.BlockSpec((tn, tk) if tb else (tk, tn), bmap) for _ in Bs]
    args = list(As) + list(Bs)
    for arr, kind, coloff in extras:
        assert coloff % tn == 0
        off = coloff // tn
        if kind == 'mn':
            in_specs.append(pl.BlockSpec((tm, tn), lambda g0, g1, k, off=off: (ij(g0, g1)[0], ij(g0, g1)[1] + off)))
        else:
            in_specs.append(pl.BlockSpec((1, tn), lambda g0, g1, k, off=off: (0, ij(g0, g1)[1] + off)))
        args.append(arr)
    if o3:
        out_shape = [jax.ShapeDtypeStruct((gj, M, tn), dt) for dt in out_dtypes]
        out_specs = [pl.BlockSpec((None, tm, tn), lambda g0, g1, k: (ij(g0, g1)[1], ij(g0, g1)[0], 0))
                     for _ in out_dtypes]
    else:
        out_shape = [jax.ShapeDtypeStruct((M, N), dt) for dt in out_dtypes]
        out_specs = [pl.BlockSpec((tm, tn), lambda g0, g1, k: ij(g0, g1)) for _ in out_dtypes]
    if bsum:
        assert gi == 1 and not tb and not b3
        out_shape.append(jax.ShapeDtypeStruct((1, N), F32))
        out_specs.append(pl.BlockSpec((1, tn), lambda g0, g1, k: (0, ij(g0, g1)[1])))
    n_a, n_b, n_e, n_o = len(As), len(Bs), len(extras), len(out_shape)
    n_p = len(prods) if gk > 1 else 0
    scratch = [pltpu.VMEM((tm, tn), F32) for _ in range(n_p)]
    if rider is not None:
        in_specs, out_specs = in_specs + [ANY] * len(rider.ins), out_specs + [ANY] * len(rider.out_shape)
        args, out_shape, scratch = args + rider.ins, out_shape + rider.out_shape, scratch + rider.scratch
    n_in, n_out = len(args), len(out_shape)

    def body(*refs):
        a_refs = refs[:n_a]
        b_refs = refs[n_a:n_a + n_b]
        e_refs = refs[n_a + n_b:n_a + n_b + n_e]
        o_refs = refs[n_in:n_in + n_o]
        acc_refs = refs[n_in + n_out:n_in + n_out + n_p]
        k = pl.program_id(2)
        if rider is not None:
            step = (pl.program_id(0) * grid[1] + pl.program_id(1)) * gk + k
            ride = (step, grid[0] * grid[1] * gk, refs[n_a + n_b + n_e:n_in], refs[n_in + n_o:n_in + n_out],
                    refs[n_in + n_out + n_p:])
            rider.begin(*ride)

        def finish(accs, rows=slice(None)):
            ex = [(e[rows, :] if kind == 'mn' else e[...]).astype(F32) for e, (_, kind, _) in zip(e_refs, extras)]
            for o, r in zip(o_refs, epi(accs, *ex)):
                o[rows, :] = r.astype(o.dtype)

        if bsum:
            @pl.when(k == 0)
            def _():
                o_refs[-1][...] = jnp.zeros_like(o_refs[-1])

            o_refs[-1][...] += _colsum(b_refs[0][...].astype(F32))

        if gk == 1:
            sub = tm if ta or not epi_rows else _fit(tm, epi_rows)
            for r0 in range(0, tm, sub):
                rows = slice(None) if ta else slice(r0, r0 + sub)
                finish([functools.reduce(jnp.add, [_dot(a_refs[ai][...] if ta else a_refs[ai][rows, :],
                                                        b_refs[bi][...], ta, tb) for ai, bi in terms])
                        for terms in prods], rows)
        else:
            @pl.when(k == 0)
            def _():
                for acc in acc_refs:
                    acc[...] = jnp.zeros_like(acc)

            for p, terms in enumerate(prods):
                for ai, bi in terms:
                    acc_refs[p][...] += _dot(a_refs[ai][...], b_refs[bi][...], ta, tb)

            @pl.when(k == gk - 1)
            def _():
                finish([acc[...] for acc in acc_refs])

        if rider is not None:
            rider.end(*ride)

    aliases = {} if rider is None else {n_a + n_b + n_e + p: n_o + o for p, o in rider.aliases.items()}
    res = pl.pallas_call(
        body, name=name, grid=grid, in_specs=in_specs, out_specs=out_specs, out_shape=out_shape,
        scratch_shapes=scratch, input_output_aliases=aliases,
        compiler_params=_params(("arbitrary", "arbitrary", "arbitrary")),
    )(*args)
    if rider is not None:
        rider.results = res[n_o:]
    return res[:n_o]


def _rows(name, fn, ins, outs, *, T, tb, rider=None):
    tb = min(tb, T)
    assert T % tb == 0
    in_specs, args = [], []
    for arr, kind, width, cb in ins:
        if kind == 'r':
            in_specs.append(pl.BlockSpec((tb, width), lambda i, _, cb=cb: (i, cb)))
        else:
            in_specs.append(pl.BlockSpec((1, width), lambda i, _, cb=cb: (0, cb)))
        args.append(arr)
    out_shape, out_specs = [], []
    for kind, width, dtype in outs:
        if kind == 'r':
            out_shape.append(jax.ShapeDtypeStruct((T, width), dtype))
            out_specs.append(pl.BlockSpec((tb, width), lambda i, _: (i, 0)))
        elif kind == 'c':
            out_shape.append(jax.ShapeDtypeStruct((T, 1), dtype))
            out_specs.append(pl.BlockSpec((tb, 1), lambda i, _: (i, 0)))
        else:
            out_shape.append(jax.ShapeDtypeStruct((1, width), F32))
            out_specs.append(pl.BlockSpec((1, width), lambda i, _: (0, 0)))
    n_in = len(ins)

    def body(*refs):
        i = pl.program_id(0)
        vals = fn(*[r[...] for r in refs[:n_in]])
        for (kind, _, _), o, v in zip(outs, refs[n_in:], vals):
            if kind == 'a':
                @pl.when(i == 0)
                def _(o=o):
                    o[...] = jnp.zeros_like(o)

                o[...] += v
            else:
                o[...] = v.astype(o.dtype)

    return _hosted_call(body, rider, name=name, grid=(T // tb, 1), in_specs=in_specs, out_specs=out_specs,
                        out_shape=out_shape, scratch=[], args=args)


def _colsum(v):
    return jnp.sum(v, axis=0, keepdims=True)


def _ln_stats(z):
    mu = jnp.mean(z, axis=-1, keepdims=True)
    d = z - mu
    var = jnp.mean(d * d, axis=-1, keepdims=True)
    rstd = lax.rsqrt(var + LN_EPS)
    return d * rstd, rstd


def _ln_bwd_math(dy, xhat, rstd, g):
    dxh = dy * g
    m1 = jnp.mean(dxh, axis=-1, keepdims=True)
    m2 = jnp.mean(dxh * xhat, axis=-1, keepdims=True)
    return rstd * (dxh - m1 - xhat * m2)


def _ln_fwd(name, z, g, b, T, D, rider=None):
    def fn(z, g, b):
        xhat, rstd = _ln_stats(z)
        y = xhat * g + b
        return [y, y, xhat, rstd]

    return _rows(name, fn, [(z, 'r', D, 0), (g, 'v', D, 0), (b, 'v', D, 0)],
                 [('r', D, F32), ('r', D, BF16), ('r', D, F32), ('c', 1, F32)], T=T, tb=512, rider=rider)


def _ln_bwd(name, dy, xhat, rstd, g, scale, T, D, rider=None):
    def fn(dy, xhat, rstd, g):
        dz = _ln_bwd_math(dy, xhat, rstd, g)
        return [dz, dz * scale, _colsum(dy * xhat), _colsum(dy)]

    return _rows(name, fn, [(dy, 'r', D, 0), (xhat, 'r', D, 0), (rstd, 'r', 1, 0), (g, 'v', D, 0)],
                 [('r', D, F32), ('r', D, BF16), ('a', D, F32), ('a', D, F32)], T=T, tb=512, rider=rider)


def _ln_loss_bwd(name, z, g, b, tgt, T, D):
    def fn(z, g, b, tgt):
        xhat, rstd = _ln_stats(z)
        err = xhat * g + b - tgt
        row_loss = 0.5 * jnp.mean(err * err, axis=-1, keepdims=True)
        loss = jnp.broadcast_to(jnp.sum(row_loss, axis=0, keepdims=True), (1, 128))
        dy = err * (1.0 / D)
        dz = _ln_bwd_math(dy, xhat, rstd, g)
        return [dz, dz * 0.5, _colsum(dy * xhat), _colsum(dy), loss]

    return _rows(name, fn, [(z, 'r', D, 0), (g, 'v', D, 0), (b, 'v', D, 0), (tgt, 'r', D, 0)],
                 [('r', D, F32), ('r', D, BF16), ('a', D, F32), ('a', D, F32), ('a', 128, F32)], T=T, tb=512)


class _Net:
    def __init__(self, comm, G):
        self.comm, self.G = comm, G

    def gather(self, names, part=None):
        return self.comm.gather(names, part) if self.comm else None

    def exchange(self, names, part=None):
        return self.comm.exchange(names, self.G, part) if self.comm else None

    def to_sibling(self, names):
        return self.comm.to_sibling(names, self.G) if self.comm else None

    def pairsum(self, names):
        if self.comm:
            self.comm.pairsum(names)

    def to_owner(self, names, part=None):
        return self.comm.to_owner(names, part) if self.comm else None

    def done(self, rider):
        if rider is not None:
            self.comm.collect(rider)


def _ffn_fwd(tag, xb, x, W, names, net, rider=None, rider_down=None):
    def epi_gu(accs):
        a, b = accs
        return [a, b, a * _sigmoid(a) * b]

    ng, nu, nd = names
    a, b, s = _mm(tag + "_gate_up", [xb], [W[ng], W[nu]], [[(0, 0)], [(0, 1)]], epi_gu, [BF16, BF16, BF16],
                  tm=1024, tn=1408, tk=1024, rider=rider, epi_rows=256)
    net.done(rider)

    def epi_down(accs, xres):
        return [ALPHA * xres + 0.5 * accs[0]]

    rider_down = rider_down() if rider_down else None
    (z,) = _mm(tag + "_down", [s], [W[nd]], [[(0, 0)]], epi_down, [F32], tm=1024, tn=1024, tk=1408,
               extras=[(x, 'mn', 0)], rider=rider_down)
    net.done(rider_down)
    return a, b, s, z


def _ffn_bwd(tag, dzh, dz, xb, a, b, s, W, names, gdt, G, net, ride, pre=(None, None)):
    ng, nu, nd = names

    def epi_ds(accs, a, b):
        ds = accs[0]
        sg = _sigmoid(a)
        return [ds * b * _dsilu(a, sg), ds * a * sg]

    rider = pre[0]() if pre[0] else None
    da, db = _mm(tag + "_ds", [dzh], [W[nd]], [[(0, 0)]], epi_ds, [BF16, BF16], tb=True, tm=1024, tn=1408, tk=1024, epi_rows=256,
                 extras=[(a, 'mn', 0), (b, 'mn', 0)], rider=rider)
    net.done(rider)
    ident = lambda accs: accs
    rider = pre[1]() if pre[1] else None
    (G[nd],) = _mm(tag + "_dwd", [s], [dzh], [[(0, 0)]], ident, [gdt], ta=True, tm=1408, tn=1024, tk=1024,
                   rider=rider)
    net.done(rider)
    if ride == 'all':
        rider = net.to_sibling([nd])
        (G[ng],) = _mm(tag + "_dwg", [xb], [da], [[(0, 0)]], ident, [gdt], ta=True, tm=1024, tn=1408, tk=1024,
                       rider=rider)
        net.done(rider)
        net.pairsum([nd])
        rider = net.to_owner([nd])
        (G[nu],) = _mm(tag + "_dwu", [xb], [db], [[(0, 0)]], ident, [gdt], ta=True, tm=1024, tn=1408, tk=1024,
                       rider=rider)
        net.done(rider)
        rider = net.to_sibling([ng, nu])
        if rider is not None:
            _run_rider(tag + "_to_sibling", rider)
            net.done(rider)
        net.pairsum([ng, nu])
        rider = net.to_owner([ng, nu])
    else:
        rider = net.to_sibling([nd]) if ride else None
        G[ng], G[nu] = _mm(tag + "_dwgu", [xb], [da, db], [[(0, 0)], [(0, 1)]], ident, [gdt, gdt], ta=True,
                           tm=1024, tn=1408, tk=1024, rider=rider)
        net.done(rider)
        if ride:
            net.pairsum([nd])
        rider = net.to_owner([nd]) if ride else None

    def epi_dx(accs, dzres):
        return [ALPHA * dzres + accs[0]]

    (dx,) = _mm(tag + "_dx", [da, db], [W[ng], W[nu]], [[(0, 0), (1, 1)]], epi_dx, [F32], tb=True,
                tm=1024, tn=1024, tk=1408, extras=[(dz, 'mn', 0)], rider=rider)
    net.done(rider)
    return dx


def _ret_tables(H, T):
    C = RET_CHUNK
    log_g = jnp.log(1.0 - jnp.exp2(-5.0 - jnp.arange(H, dtype=F32)))
    idx = jnp.arange(C, dtype=F32)
    diff = idx[:, None] - idx[None, :]
    dm = jnp.where(diff[None] >= 0, jnp.exp(jnp.maximum(diff, 0.0)[None] * log_g[:, None, None]), 0.0)
    xi = jnp.exp((idx[None, :] + 1.0) * log_g[:, None])[:, :, None]
    zeta = jnp.exp((C - 1.0 - idx)[None, :] * log_g[:, None])[:, :, None]
    gc = jnp.broadcast_to(jnp.exp(C * log_g)[:, None, None], (H, 1, RET_DV))
    half = RET_DK // 2
    freqs = ROPE_BASE ** (-jnp.arange(half, dtype=F32) / half)
    ang = jnp.arange(T, dtype=F32)[:, None] * freqs[None, :]
    cos, sin = jnp.cos(ang), jnp.sin(ang)
    cosf = jnp.concatenate([cos, cos], axis=1)
    sins = jnp.concatenate([-sin, sin], axis=1)
    return dm, xi, zeta, gc, cosf, sins


def _rot(x, cosf, sins):
    return x * cosf + pltpu.roll(x, RET_DK // 2, 1) * sins


def _rot_bwd(dy, cosf, sins):
    return dy * cosf + pltpu.roll(dy * sins, RET_DK // 2, 1)


RET_HB = 8


def _ret_specs(H, HB, rev, NC):
    C, G = RET_CHUNK, H // HB
    nn = (lambda n: NC - 1 - n) if rev else (lambda n: n)
    return [
        pl.BlockSpec((C, HB * RET_DK), lambda h, n: (nn(n), h)),
        pl.BlockSpec((C, HB * RET_DK), lambda h, n: (nn(n), G + h)),
        pl.BlockSpec((C, HB * RET_DV), lambda h, n: (nn(n), G + h)),
        pl.BlockSpec((C, HB * RET_DV), lambda h, n: (nn(n), 2 * G + h)),
        pl.BlockSpec((C, RET_DK), lambda h, n: (nn(n), 0)),
        pl.BlockSpec((C, RET_DK), lambda h, n: (nn(n), 0)),
        pl.BlockSpec((1, HB * RET_DV), lambda h, n: (0, h)),
        pl.BlockSpec((HB, C, C), lambda h, n: (h, 0, 0)),
        pl.BlockSpec((HB, C, 1), lambda h, n: (h, 0, 0)),
        pl.BlockSpec((HB, C, 1), lambda h, n: (h, 0, 0)),
        pl.BlockSpec((HB, 1, RET_DV), lambda h, n: (h, 0, 0)),
    ]


def _ret_fwd(proj, gn_g, tabs, H, T, rider=None):
    C, NC = RET_CHUNK, T // RET_CHUNK
    HB = min(RET_HB, H)
    dm, xi, zeta, gc, cosf, sins = tabs
    scale = RET_DK ** -0.5

    def body(q_ref, k_ref, v_ref, g_ref, cos_ref, sin_ref, gn_ref, dm_ref, xi_ref, zt_ref, gc_ref,
             r_ref, ri_ref, st_ref, state):
        @pl.when(pl.program_id(1) == 0)
        def _():
            state[...] = jnp.zeros_like(state)

        cs, sn = cos_ref[...], sin_ref[...]
        hs = range(HB)
        qk = [slice(h * RET_DK, (h + 1) * RET_DK) for h in hs]
        vv = [slice(h * RET_DV, (h + 1) * RET_DV) for h in hs]
        kr = [_rot(k_ref[:, qk[h]].astype(F32), cs, sn) for h in hs]
        qb = [(_rot(q_ref[:, qk[h]].astype(F32), cs, sn) * scale).astype(BF16) for h in hs]
        kb = [kr[h].astype(BF16) for h in hs]
        kzb = [(kr[h] * zt_ref[h]).astype(BF16) for h in hs]
        vb = [v_ref[:, vv[h]].astype(BF16) for h in hs]
        st = [state[h] for h in hs]
        stb = [st[h].astype(BF16) for h in hs]
        sb = [(_dot(qb[h], kb[h], tb=True) * dm_ref[h]).astype(BF16) for h in hs]
        cross = [_dot(qb[h], stb[h]) for h in hs]
        kv = [_dot(kzb[h], vb[h], ta=True) for h in hs]
        intra = [_dot(sb[h], vb[h]) for h in hs]
        for h in hs:
            st_ref[h] = stb[h]
            state[h] = gc_ref[h] * st[h] + kv[h]
        for h in hs:
            r = intra[h] + cross[h] * xi_ref[h]
            rhat, _ = _ln_stats(r)
            g = g_ref[:, vv[h]].astype(F32)
            r_ref[:, vv[h]] = r
            ri_ref[:, vv[h]] = (g * _sigmoid(g) * (rhat * gn_ref[:, vv[h]])).astype(BF16)

    VW = H * RET_DV
    return _hosted_call(
        body, rider, name="ret_fwd", grid=(H // HB, NC), in_specs=_ret_specs(H, HB, False, NC),
        out_specs=[pl.BlockSpec((C, HB * RET_DV), lambda h, n: (n, h)),
                   pl.BlockSpec((C, HB * RET_DV), lambda h, n: (n, h)),
                   pl.BlockSpec((HB, None, RET_DK, RET_DV), lambda h, n: (h, n, 0, 0))],
        out_shape=[jax.ShapeDtypeStruct((T, VW), F32), jax.ShapeDtypeStruct((T, VW), BF16),
                   jax.ShapeDtypeStruct((H, NC, RET_DK, RET_DV), BF16)],
        scratch=[pltpu.VMEM((HB, RET_DK, RET_DV), F32)],
        args=[proj, proj, proj, proj, cosf, sins, gn_g, dm, xi, zeta, gc])


def _hosted_call(body, rider, *, name, grid, in_specs, out_specs, out_shape, scratch, args):
    n_in, n_out, n_scr = len(args), len(out_shape), len(scratch)
    if rider is None:
        hosted = body
    else:
        n_ri, n_ro = len(rider.ins), len(rider.out_shape)
        in_specs, out_specs = in_specs + [ANY] * n_ri, out_specs + [ANY] * n_ro
        args, out_shape, scratch = args + rider.ins, out_shape + rider.out_shape, scratch + rider.scratch

        def hosted(*refs):
            o0, s0 = n_in + n_ri, n_in + n_ri + n_out + n_ro
            step = pl.program_id(0) * grid[1] + pl.program_id(1)
            ride = (step, grid[0] * grid[1], refs[n_in:o0], refs[o0 + n_out:s0], refs[s0 + n_scr:])
            rider.begin(*ride)
            body(*refs[:n_in], *refs[o0:o0 + n_out], *refs[s0:s0 + n_scr])
            rider.end(*ride)

    aliases = {} if rider is None else {n_in + p: n_out + o for p, o in rider.aliases.items()}
    res = pl.pallas_call(
        hosted, name=name, grid=grid, in_specs=in_specs, out_specs=out_specs, out_shape=out_shape,
        scratch_shapes=scratch, input_output_aliases=aliases, compiler_params=_params(("arbitrary", "arbitrary")),
    )(*args)
    if rider is not None:
        rider.results = res[n_out:]
    return res[:n_out]


def _ret_bwd(dri, r, states, proj, gn_g, tabs, H, T, in_w, rider=None):
    C, NC = RET_CHUNK, T // RET_CHUNK
    HB = min(RET_HB, H)
    dm, xi, zeta, gc, cosf, sins = tabs
    scale = RET_DK ** -0.5

    def body(q_ref, k_ref, v_ref, g_ref, cos_ref, sin_ref, gn_ref, dm_ref, xi_ref, zt_ref, gc_ref,
             dri_ref, r_ref, st_ref, dp_ref, dgn_ref, dstate):
        @pl.when(pl.program_id(1) == 0)
        def _():
            dstate[...] = jnp.zeros_like(dstate)
            dgn_ref[...] = jnp.zeros_like(dgn_ref)

        cs, sn = cos_ref[...], sin_ref[...]
        hs = range(HB)
        qk = [slice(h * RET_DK, (h + 1) * RET_DK) for h in hs]
        vv = [slice(h * RET_DV, (h + 1) * RET_DV) for h in hs]
        qr = [_rot(q_ref[:, qk[h]].astype(F32), cs, sn) * scale for h in hs]
        kr = [_rot(k_ref[:, qk[h]].astype(F32), cs, sn) for h in hs]
        qb = [qr[h].astype(BF16) for h in hs]
        kb = [kr[h].astype(BF16) for h in hs]
        vb = [v_ref[:, vv[h]].astype(BF16) for h in hs]
        qxb = [(qr[h] * xi_ref[h]).astype(BF16) for h in hs]
        kzb = [(kr[h] * zt_ref[h]).astype(BF16) for h in hs]
        drb = []
        for h in hs:
            rhat, rstd = _ln_stats(r_ref[:, vv[h]])
            g, gn, dpre = g_ref[:, vv[h]].astype(F32), gn_ref[:, vv[h]], dri_ref[:, vv[h]]
            sg = _sigmoid(g)
            dp_ref[:, 2 * QW + VW + h * RET_DV:2 * QW + VW + (h + 1) * RET_DV] = (
                dpre * (rhat * gn) * _dsilu(g, sg)).astype(BF16)
            drn = dpre * (g * sg)
            dgn_ref[:, vv[h]] += _colsum(drn * rhat)
            drb.append(_ln_bwd_math(drn, rhat, rstd, gn).astype(BF16))
        ds1 = [dstate[h] for h in hs]
        ds1b = [ds1[h].astype(BF16) for h in hs]
        sb = [(_dot(qb[h], kb[h], tb=True) * dm_ref[h]).astype(BF16) for h in hs]
        dsb = [(_dot(drb[h], vb[h], tb=True) * dm_ref[h]).astype(BF16) for h in hs]
        dq_x = [_dot(drb[h], st_ref[h], tb=True) for h in hs]
        dk_x = [_dot(vb[h], ds1b[h], tb=True) for h in hs]
        dv_x = [_dot(kzb[h], ds1b[h]) for h in hs]
        dst = [_dot(qxb[h], drb[h], ta=True) for h in hs]
        for h in hs:
            dstate[h] = gc_ref[h] * ds1[h] + dst[h]
        dv_i = [_dot(sb[h], drb[h], ta=True) for h in hs]
        dq_i = [_dot(dsb[h], kb[h]) for h in hs]
        dk_i = [_dot(dsb[h], qb[h], ta=True) for h in hs]
        for h in hs:
            dp_ref[:, 2 * QW + h * RET_DV:2 * QW + (h + 1) * RET_DV] = (dv_i[h] + dv_x[h]).astype(BF16)
            dq = dq_i[h] + dq_x[h] * xi_ref[h]
            dk = dk_i[h] + dk_x[h] * zt_ref[h]
            dp_ref[:, qk[h]] = _rot_bwd(dq * scale, cs, sn).astype(BF16)
            dp_ref[:, QW + h * RET_DK:QW + (h + 1) * RET_DK] = _rot_bwd(dk, cs, sn).astype(BF16)

    VW, QW = H * RET_DV, H * RET_DK
    rv = lambda n: NC - 1 - n
    in_specs = _ret_specs(H, HB, True, NC) + [
        pl.BlockSpec((C, HB * RET_DV), lambda h, n: (rv(n), h)),
        pl.BlockSpec((C, HB * RET_DV), lambda h, n: (rv(n), h)),
        pl.BlockSpec((HB, None, RET_DK, RET_DV), lambda h, n: (h, rv(n), 0, 0)),
    ]
    assert HB == H
    return _hosted_call(
        body, rider, name="ret_bwd", grid=(1, NC), in_specs=in_specs,
        out_specs=[pl.BlockSpec((C, 2 * QW + 2 * VW), lambda h, n: (rv(n), 0)),
                   pl.BlockSpec((1, VW), lambda h, n: (0, 0))],
        out_shape=[jax.ShapeDtypeStruct((T, in_w), BF16), jax.ShapeDtypeStruct((1, VW), F32)],
        scratch=[pltpu.VMEM((HB, RET_DK, RET_DV), F32)],
        args=[proj, proj, proj, proj, cosf, sins, gn_g, dm, xi, zeta, gc, dri, r, states])


CONV_CW = 128
CONV_TB = 512


def _conv_fwd(proj, kpad, bias, off_a, CC, T, rider=None):
    tb, cw = min(CONV_TB, T), CONV_CW
    hb = tb // HALO
    ca, cb = off_a // cw, (off_a + CC) // cw

    def body(a_ref, b_ref, ap_ref, bp_ref, k_ref, bias_ref, u1_ref, win):
        i = pl.program_id(0)
        keep = (i > 0).astype(F32)
        win[0:HALO, :] = ap_ref[...].astype(F32) * _sigmoid(bp_ref[...].astype(F32)) * keep
        win[HALO:, :] = a_ref[...].astype(F32) * _sigmoid(b_ref[...].astype(F32))
        acc = jnp.broadcast_to(bias_ref[...], (tb, cw))
        for w in range(CONV_WIDTH):
            acc = acc + k_ref[w:w + 1, :] * win[pl.ds(HALO - (CONV_WIDTH - 1) + w, tb), :]
        u1_ref[...] = acc

    prev = lambda i: jnp.maximum(i * hb - 1, 0)
    (u1,) = _hosted_call(
        body, rider, name="conv_fwd", grid=(T // tb, CC // cw),
        in_specs=[pl.BlockSpec((tb, cw), lambda i, c: (i, ca + c)),
                  pl.BlockSpec((tb, cw), lambda i, c: (i, cb + c)),
                  pl.BlockSpec((HALO, cw), lambda i, c: (prev(i), ca + c)),
                  pl.BlockSpec((HALO, cw), lambda i, c: (prev(i), cb + c)),
                  pl.BlockSpec((HALO, cw), lambda i, c: (0, c)),
                  pl.BlockSpec((1, cw), lambda i, c: (0, c))],
        out_specs=[pl.BlockSpec((tb, cw), lambda i, c: (i, c))],
        out_shape=[jax.ShapeDtypeStruct((T, CC), F32)],
        scratch=[pltpu.VMEM((tb + HALO, cw), F32)],
        args=[proj, proj, proj, proj, kpad, bias])
    return u1


def _conv_bwd(du1, proj, kpad, off_a, CC, T, rider=None):
    tb, cw = min(CONV_TB, T), CONV_CW
    hb = tb // HALO
    nt = T // tb
    ca, cb = off_a // cw, (off_a + CC) // cw

    def body(d_ref, dn_ref, a_ref, b_ref, ap_ref, bp_ref, k_ref, da_ref, db_ref, dk_ref, winu, wind):
        i = pl.program_id(1)
        a, b = a_ref[...].astype(F32), b_ref[...].astype(F32)
        sgb = _sigmoid(b)
        winu[0:HALO, :] = ap_ref[...].astype(F32) * _sigmoid(bp_ref[...].astype(F32)) * (i > 0).astype(F32)
        winu[HALO:, :] = a * sgb
        d = d_ref[...]
        wind[0:tb, :] = d
        wind[tb:, :] = dn_ref[...] * (i < nt - 1).astype(F32)

        @pl.when(i == 0)
        def _():
            dk_ref[...] = jnp.zeros_like(dk_ref)

        du0 = jnp.zeros((tb, cw), F32)
        for w in range(CONV_WIDTH):
            du0 = du0 + k_ref[w:w + 1, :] * wind[pl.ds(CONV_WIDTH - 1 - w, tb), :]
            dk_ref[w:w + 1, :] += _colsum(winu[pl.ds(HALO - (CONV_WIDTH - 1) + w, tb), :] * d)
        da_ref[...] = (du0 * sgb).astype(BF16)
        db_ref[...] = (du0 * a * sgb * (1.0 - sgb)).astype(BF16)

    prev = lambda i: jnp.maximum(i * hb - 1, 0)
    nxt = lambda i: jnp.minimum((i + 1) * hb, T // HALO - 1)
    return _hosted_call(
        body, rider, name="conv_bwd", grid=(CC // cw, nt),
        in_specs=[pl.BlockSpec((tb, cw), lambda c, i: (i, c)),
                  pl.BlockSpec((HALO, cw), lambda c, i: (nxt(i), c)),
                  pl.BlockSpec((tb, cw), lambda c, i: (i, ca + c)),
                  pl.BlockSpec((tb, cw), lambda c, i: (i, cb + c)),
                  pl.BlockSpec((HALO, cw), lambda c, i: (prev(i), ca + c)),
                  pl.BlockSpec((HALO, cw), lambda c, i: (prev(i), cb + c)),
                  pl.BlockSpec((HALO, cw), lambda c, i: (0, c))],
        out_specs=[pl.BlockSpec((tb, cw), lambda c, i: (i, c)),
                   pl.BlockSpec((tb, cw), lambda c, i: (i, c)),
                   pl.BlockSpec((HALO, cw), lambda c, i: (0, c))],
        out_shape=[jax.ShapeDtypeStruct((T, CC), BF16), jax.ShapeDtypeStruct((T, CC), BF16),
                   jax.ShapeDtypeStruct((HALO, CC), F32)],
        scratch=[pltpu.VMEM((tb + HALO, cw), F32), pltpu.VMEM((tb + HALO, cw), F32)],
        args=[du1, du1, proj, proj, proj, proj, kpad])


FFN1 = ('ffn1_w_gate', 'ffn1_w_up', 'ffn1_w_down')
FFN2 = ('ffn2_w_gate', 'ffn2_w_up', 'ffn2_w_down')


def _local_step(x, tgt, W, P, gdt=BF16, comm=None):
    T, D = x.shape
    G = {}
    net = _Net(comm, G)
    if comm is not None:
        W = comm.W
        first = net.gather(['ffn1_w_gate', 'ffn1_w_up'])
    (xb,) = _rows("x_to_bf16", lambda v: [v], [(x, 'r', D, 0)], [('r', D, BF16)], T=T, tb=512,
                  rider=first if comm is not None else None)
    if comm is not None:
        net.done(first)
    VW = P['ret_gn_g'].shape[1]
    H = VW // RET_DV
    QW = H * RET_DK
    CC = P['conv_b'].shape[1]
    off_glu = 2 * QW + 2 * VW
    off_gate = off_glu + 2 * CC
    ident = lambda accs: accs

    a1, b1, s1, z1 = _ffn_fwd("ffn1", xb, x, W, FFN1, net,
                              rider=net.gather(['ffn1_w_down', 'w_in'], {'w_in': (0, D // 4, False)}),
                              rider_down=lambda: net.gather(['w_in'], {'w_in': (D // 4, (3 * D) // 8, False)}))
    rider = net.gather(['w_in'], {'w_in': ((5 * D) // 8, (3 * D) // 8, True)})
    x1, x1b, xh1, rs1 = _ln_fwd("ln1", z1, P['ln1_g'], P['ln1_b'], T, D, rider=rider)
    net.done(rider)

    rest = net.gather(['conv_k', 'w_ret_o', 'w_conv_o', 'w_out'])
    (proj,) = _mm("w_in", [x1b], [W['w_in']], [[(0, 0)]], lambda accs, bias: [accs[0] + bias], [F32],
                  tm=2048, tn=0, tk=1024, extras=[(P['b_in'], 'n', 0)], i_outer=False, b3=True, rider=rest)
    net.done(rest)
    tabs = _ret_tables(H, T)
    rider = net.gather(['ffn2_w_gate', 'ffn2_w_up'])
    r, ret_in, states = _ret_fwd(proj, P['ret_gn_g'], tabs, H, T, rider=rider)
    net.done(rider)
    kpad = jnp.pad(W['conv_k'].astype(F32), ((0, HALO - CONV_WIDTH), (0, 0)))
    u1 = _conv_fwd(proj, kpad, P['conv_b'], off_glu, CC, T)

    def conv_ln(u1, g, b):
        xhat, rstd = _ln_stats(u1)
        u2 = xhat * g + b
        return [xhat, rstd, u2 * _sigmoid(u2)]

    xhc, rsc, u3 = _rows("conv_ln", conv_ln, [(u1, 'r', CC, 0), (P['conv_ln_g'], 'v', CC, 0), (P['conv_ln_b'], 'v', CC, 0)],
                         [('r', CC, F32), ('c', 1, F32), ('r', CC, BF16)], T=T, tb=512)
    rider = net.gather(['ffn2_w_down'])
    (ret_out,) = _mm("ret_o", [ret_in], [W['w_ret_o']], [[(0, 0)]], ident, [F32], tm=1024, tn=1024, tk=2048,
                     rider=rider)
    net.done(rider)

    def epi_merge(accs, ret_out, gr, gc):
        conv_out = accs[0]
        return [conv_out, _sigmoid(gr) * ret_out + _sigmoid(gc) * conv_out]

    conv_out, merged = _mm("conv_o_merge", [u3], [W['w_conv_o']], [[(0, 0)]], epi_merge, [F32, BF16],
                           tm=512, tn=D, tk=1024, epi_rows=256,
                           extras=[(ret_out, 'mn', 0), (proj, 'mn', off_gate), (proj, 'mn', off_gate + D)])
    (z2,) = _mm("w_out", [merged], [W['w_out']], [[(0, 0)]], lambda accs, xr: [ALPHA * xr + accs[0]], [F32],
                tm=1024, tn=1024, tk=1024, extras=[(x1, 'mn', 0)])
    x2, x2b, xh2, rs2 = _ln_fwd("ln2", z2, P['ln2_g'], P['ln2_b'], T, D)
    a2, b2, s2, z3 = _ffn_fwd("ffn2", x2b, x2, W, FFN2, net)
    dz3, dz3h, g_ln3_g, g_ln3_b, loss = _ln_loss_bwd("ln3_loss", z3, P['ln3_g'], P['ln3_b'], tgt, T, D)

    S = {'ln3_g': g_ln3_g, 'ln3_b': g_ln3_b}
    dy2 = _ffn_bwd("ffn2b", dz3h, dz3, x2b, a2, b2, s2, W, FFN2, gdt, G, net, 'down')
    rider = net.to_sibling(['ffn2_w_gate', 'ffn2_w_up'])
    dz2, dz2b, S['ln2_g'], S['ln2_b'] = _ln_bwd("ln2b", dy2, xh2, rs2, P['ln2_g'], 1.0, T, D, rider=rider)
    net.done(rider)
    net.pairsum(['ffn2_w_gate', 'ffn2_w_up'])

    (G['w_out'],) = _mm("d_w_out", [merged], [dz2b], [[(0, 0)]], ident, [gdt], ta=True, tm=1024, tn=1024, tk=1024)

    def epi_dmerge(accs, ret_out, conv_out, gr, gc):
        dm_ = accs[0]
        sr, sc = _sigmoid(gr), _sigmoid(gc)
        return [dm_ * sr, dm_ * sc, dm_ * ret_out * sr * (1.0 - sr), dm_ * conv_out * sc * (1.0 - sc)]

    rider = net.to_sibling(['w_out'])
    dret_out, dconv_out, dgate_r, dgate_c = _mm(
        "d_merge", [dz2b], [W['w_out']], [[(0, 0)]], epi_dmerge, [BF16, BF16, BF16, BF16], tb=True,
        tm=512, tn=D, tk=1024, epi_rows=256, rider=rider,
        extras=[(ret_out, 'mn', 0), (conv_out, 'mn', 0), (proj, 'mn', off_gate), (proj, 'mn', off_gate + D)])
    net.done(rider)
    (G['w_ret_o'],) = _mm("d_w_ret_o", [ret_in], [dret_out], [[(0, 0)]], ident, [gdt], ta=True, tm=1024, tn=1024, tk=1024)
    (G['w_conv_o'],) = _mm("d_w_conv_o", [u3], [dconv_out], [[(0, 0)]], ident, [gdt], ta=True, tm=1024, tn=1024, tk=1024)
    rider = net.to_sibling(['w_ret_o', 'w_conv_o'])
    (dri,) = _mm("d_ret_in", [dret_out], [W['w_ret_o']], [[(0, 0)]], ident, [F32], tb=True, tm=1024, tn=1024, tk=1024,
                 rider=rider)
    net.done(rider)
    net.pairsum(['w_out', 'w_ret_o', 'w_conv_o'])
    rider = net.to_owner(['ffn2_w_gate', 'ffn2_w_up', 'w_out', 'w_ret_o', 'w_conv_o'])
    dproj, S['ret_gn_g'] = _ret_bwd(dri, r, states, proj, P['ret_gn_g'], tabs, H, T, proj.shape[1], rider=rider)
    net.done(rider)

    def epi_du2(accs, xhat, g, b):
        u2 = xhat * g + b
        return [accs[0] * _dsilu(u2, _sigmoid(u2))]

    (du2,) = _mm("d_u3", [dconv_out], [W['w_conv_o']], [[(0, 0)]], epi_du2, [F32], tb=True, tm=512, tn=CC, tk=1024, epi_rows=256,
                 extras=[(xhc, 'mn', 0), (P['conv_ln_g'], 'n', 0), (P['conv_ln_b'], 'n', 0)])

    def conv_ln_bwd(du2, xhat, rstd, g):
        du1 = _ln_bwd_math(du2, xhat, rstd, g)
        return [du1, _colsum(du2 * xhat), _colsum(du2), _colsum(du1)]

    du1, S['conv_ln_g'], S['conv_ln_b'], S['conv_b'] = _rows(
        "conv_ln_bwd", conv_ln_bwd, [(du2, 'r', CC, 0), (xhc, 'r', CC, 0), (rsc, 'r', 1, 0), (P['conv_ln_g'], 'v', CC, 0)],
        [('r', CC, F32), ('a', CC, F32), ('a', CC, F32), ('a', CC, F32)], T=T, tb=512)
    dglu_a, dglu_b, dkpad = _conv_bwd(du1, proj, kpad, off_glu, CC, T)
    G['conv_k'] = dkpad[:CONV_WIDTH].astype(gdt)

    for off, piece in ((off_glu, dglu_a), (off_glu + CC, dglu_b), (off_gate, dgate_r), (off_gate + D, dgate_c)):
        dproj = lax.dynamic_update_slice(dproj, piece, (0, off))
    IN_W = dproj.shape[1]
    rider = net.exchange(['conv_k'])
    G['w_in'], S['b_in'] = _mm("d_w_in", [x1b], [dproj], [[(0, 0)]], ident, [gdt], ta=True, o3=True, bsum=True,
                               tm=1024, tn=W['w_in'].shape[2], tk=1024, rider=rider)
    net.done(rider)
    cuts = [0, (3 * D) // 16, (43 * D) // 64, D]
    w_in_rows = [{'w_in': (cuts[i], cuts[i + 1] - cuts[i], i == 2)} for i in range(3)]
    rider = net.to_sibling(['w_in'])
    (dy1,) = _mm("d_x1", [dproj], [W['w_in']], [[(0, 0)]], lambda accs, dzr: [ALPHA * dzr + accs[0]], [F32], tb=True,
                 b3=True, tm=1024, tn=1024, tk=0, extras=[(dz2, 'mn', 0)], rider=rider)
    net.done(rider)
    net.pairsum(['w_in'])
    rider = net.to_owner(['w_in'], w_in_rows[0])
    dz1, dz1h, S['ln1_g'], S['ln1_b'] = _ln_bwd("ln1b", dy1, xh1, rs1, P['ln1_g'], 0.5, T, D, rider=rider)
    net.done(rider)
    grad_x = _ffn_bwd("ffn1b", dz1h, dz1, xb, a1, b1, s1, W, FFN1, gdt, G, net, 'all',
                      pre=(lambda: net.to_owner(['w_in'], w_in_rows[1]), lambda: net.to_owner(['w_in'], w_in_rows[2])))
    return loss, grad_x, G, S


def _coords():
    return lax.axis_index("x"), lax.axis_index("y"), lax.axis_index("c")


def _flip(k, x, y, c):
    return (1 - x if k & 4 else x, 1 - y if k & 2 else y, 1 - c if k & 1 else c)


def _lin(p):
    return 4 * p[0] + 2 * p[1] + p[2]


class _Rider:
    def __init__(self, ins, out_shape, rows=None, fill=None):
        nb = len(ins)
        self.rows = rows or [None] * nb
        fill = fill or [None] * nb
        self.aliases = {nb + i: w for i, w in enumerate(w for w in range(nb) if fill[w] is not None)}
        self.ins = list(ins) + [f for f in fill if f is not None]
        self.out_shape, self.results = list(out_shape), None
        self.scratch = [pltpu.SemaphoreType.DMA((8 * nb,)), pltpu.SemaphoreType.DMA((8 * nb,)),
                        pltpu.SemaphoreType.DMA((nb,))]

    def span(self, w, ref, *slot, half=None):
        rows = self.rows[w]
        if half is not None:
            first, count = rows if rows is not None else (0, self.out_shape[w].shape[1])
            rows = (first + half * (count // 2), count // 2)
        if rows is None:
            return ref.at[slot] if slot else ref
        return ref.at[(*slot, pl.ds(*rows))]

    def begin(self, step, n_steps, ins, outs, sems):
        @pl.when(step == 0)
        def _():
            self.start(ins, outs, sems)

        @pl.when(step == min(n_steps - 1, (5 * n_steps) // 8))
        def _():
            self.relay(ins, outs, sems)

        @pl.when(step == n_steps - 1)
        def _():
            self.mid(ins, outs, sems)

    def end(self, step, n_steps, ins, outs, sems):
        @pl.when(step == n_steps - 1)
        def _():
            self.finish(ins, outs, sems)

    def relay(self, ins, outs, sems):
        pass

    def mid(self, ins, outs, sems):
        pass


class _GatherRider(_Rider):
    def __init__(self, blks, rows=None, fill=None):
        super().__init__(blks, [jax.ShapeDtypeStruct((N_DEV,) + b.shape, b.dtype) for b in blks], rows, fill)
        counts = [(r[1] if r is not None else b.shape[0]) for r, b in zip(self.rows, blks)]
        self.halves = [n % 32 == 0 for n in counts]

    def _copies(self, x_refs, out_refs, sems):
        nb = len(self.out_shape)
        send_sems, recv_sems, local_sems = sems
        x, y, c = _coords()
        me, sib = (x, y, c), (x, y, 1 - c)
        xn, yn, dg = _flip(4, x, y, c), _flip(2, x, y, c), _flip(6, x, y, c)
        plans = []
        for w in range(nb):
            own = self.span(w, x_refs[w])

            def copy(k, block, to, src=None, half=None, w=w):
                slot = self.span(w, out_refs[w], _lin(block), half=half)
                return pltpu.make_async_remote_copy(
                    src_ref=slot if src is None else src, dst_ref=slot, send_sem=send_sems.at[k * nb + w],
                    recv_sem=recv_sems.at[k * nb + w], device_id=to, device_id_type=MESH)

            mine = pltpu.make_async_copy(own, self.span(w, out_refs[w], _lin(me)), local_sems.at[w])
            first = [copy(0, me, sib, src=own), copy(1, me, xn, src=own), copy(2, me, yn, src=own)]
            if self.halves[w]:
                relay = [(copy(1, xn, me), [copy(3, xn, yn, half=0), copy(5, xn, sib)]),
                         (copy(2, yn, me), [copy(4, yn, xn, half=1), copy(6, yn, sib)])]
                last = [(copy(3, dg, me, half=0), []), (copy(4, dg, me, half=1), [copy(7, dg, sib)])]
            else:
                first.append(copy(3, me, dg, src=own))
                relay = [(copy(1, xn, me), [copy(5, xn, sib)]), (copy(2, yn, me), [copy(6, yn, sib)])]
                last = [(copy(3, dg, me), [copy(7, dg, sib)])]
            other = lambda p: (p[0], p[1], 1 - c)
            from_sib = [copy(0, sib, me), copy(5, other(xn), me), copy(6, other(yn), me), copy(7, other(dg), me)]
            plans.append((mine, first, relay, last, from_sib))
        return plans

    def start(self, ins, outs, sems):
        for mine, first, _, _, _ in self._copies(ins, outs, sems):
            for cp in [mine] + first:
                cp.start()

    def relay(self, ins, outs, sems):
        for _, _, relay, _, _ in self._copies(ins, outs, sems):
            for arrival, released in relay:
                arrival.wait_recv()
                for cp in released:
                    cp.start()

    def mid(self, ins, outs, sems):
        for _, _, _, last, _ in self._copies(ins, outs, sems):
            for arrival, released in last:
                arrival.wait_recv()
                for cp in released:
                    cp.start()

    def finish(self, ins, outs, sems):
        for mine, first, relay, last, from_sib in self._copies(ins, outs, sems):
            for cp in from_sib:
                cp.wait_recv()
            for cp in first + [cp for _, released in relay + last for cp in released]:
                cp.wait_send()
            mine.wait()


class _ExchangeRider(_Rider):
    def __init__(self, gs, rows=None, fill=None):
        super().__init__(gs, [jax.ShapeDtypeStruct(g.shape, g.dtype) for g in gs], rows, fill)

    def _copies(self, g_refs, out_refs, sems):
        nb = len(self.out_shape)
        send_sems, recv_sems, local_sems = sems
        x, y, c = _coords()
        me = _lin((x, y, c))

        def copy(k, w, landing):
            peer = _flip(k, x, y, c)
            src, dst = (me, _lin(peer)) if landing else (_lin(peer), me)
            return pltpu.make_async_remote_copy(
                src_ref=self.span(w, g_refs[w], src), dst_ref=self.span(w, out_refs[w], dst),
                send_sem=send_sems.at[(k - 1) * nb + w], recv_sem=recv_sems.at[(k - 1) * nb + w],
                device_id=peer, device_id_type=MESH)

        mines = [pltpu.make_async_copy(self.span(w, g_refs[w], me), self.span(w, out_refs[w], me), local_sems.at[w])
                 for w in range(nb)]
        sends = [copy(k, w, False) for w in range(nb) for k in range(1, N_DEV)]
        landings = [copy(k, w, True) for w in range(nb) for k in range(1, N_DEV)]
        return mines, sends, landings

    def start(self, ins, outs, sems):
        mines, sends, _ = self._copies(ins, outs, sems)
        for cp in mines + sends:
            cp.start()

    def finish(self, ins, outs, sems):
        mines, sends, landings = self._copies(ins, outs, sems)
        for cp in landings:
            cp.wait_recv()
        for cp in sends:
            cp.wait_send()
        for mine in mines:
            mine.wait()


class _SiblingRider(_Rider):
    def __init__(self, gs):
        super().__init__(gs, [jax.ShapeDtypeStruct((4,) + g.shape[1:], g.dtype) for g in gs])

    def _copies(self, g_refs, out_refs, sems, landing):
        nb = len(self.out_shape)
        send_sems, recv_sems, _ = sems
        x, y, c = _coords()
        whose = c if landing else 1 - c
        return [pltpu.make_async_remote_copy(
            src_ref=g_refs[w].at[2 * q + whose], dst_ref=out_refs[w].at[q], send_sem=send_sems.at[q * nb + w],
            recv_sem=recv_sems.at[q * nb + w], device_id=(x, y, 1 - c), device_id_type=MESH)
            for w in range(nb) for q in range(4)]

    def start(self, ins, outs, sems):
        for cp in self._copies(ins, outs, sems, False):
            cp.start()

    def finish(self, ins, outs, sems):
        for cp in self._copies(ins, outs, sems, True):
            cp.wait_recv()
        for cp in self._copies(ins, outs, sems, False):
            cp.wait_send()


class _ChipRider(_Rider):
    FLIPS = (4, 2, 6)

    def __init__(self, ps, rows=None, fill=None):
        super().__init__(ps, [jax.ShapeDtypeStruct(p.shape, p.dtype) for p in ps], rows, fill)

    def _copies(self, p_refs, out_refs, sems):
        nb = len(self.out_shape)
        send_sems, recv_sems, local_sems = sems
        x, y, c = _coords()
        my_chip = 2 * x + y

        def copy(j, w, landing):
            peer = _flip(self.FLIPS[j], x, y, c)
            peer_chip = 2 * peer[0] + peer[1]
            src, dst = (my_chip, peer_chip) if landing else (peer_chip, my_chip)
            return pltpu.make_async_remote_copy(
                src_ref=self.span(w, p_refs[w], src), dst_ref=self.span(w, out_refs[w], dst),
                send_sem=send_sems.at[j * nb + w], recv_sem=recv_sems.at[j * nb + w],
                device_id=peer, device_id_type=MESH)

        mines = [pltpu.make_async_copy(self.span(w, p_refs[w], my_chip), self.span(w, out_refs[w], my_chip),
                                       local_sems.at[w]) for w in range(nb)]
        sends = [copy(j, w, False) for w in range(nb) for j in range(3)]
        landings = [copy(j, w, True) for w in range(nb) for j in range(3)]
        return mines, sends, landings

    def start(self, ins, outs, sems):
        mines, sends, _ = self._copies(ins, outs, sems)
        for cp in mines + sends:
            cp.start()

    def finish(self, ins, outs, sems):
        mines, sends, landings = self._copies(ins, outs, sems)
        for cp in landings:
            cp.wait_recv()
        for cp in sends:
            cp.wait_send()
        for mine in mines:
            mine.wait()


def _pairsum(name, g, land):
    _, r, cols = g.shape
    tb = r if r % 16 else _row_tile(r, 16, max(16, (1024 * 1024) // cols))
    core = lax.axis_index("c").astype(jnp.int32).reshape(1)

    def body(core_ref, g_ref, l_ref, o_ref):
        o_ref[...] = (g_ref[...].astype(F32) + l_ref[...].astype(F32)).astype(o_ref.dtype)

    return pl.pallas_call(
        body, name=name, out_shape=jax.ShapeDtypeStruct((4, r, cols), g.dtype),
        grid_spec=pltpu.PrefetchScalarGridSpec(
            num_scalar_prefetch=1, grid=(4, r // tb),
            in_specs=[pl.BlockSpec((None, None, tb, cols), lambda q, i, core_ref: (q, core_ref[0], i, 0)),
                      pl.BlockSpec((None, tb, cols), lambda q, i, core_ref: (q, i, 0))],
            out_specs=pl.BlockSpec((None, tb, cols), lambda q, i, core_ref: (q, i, 0))),
        compiler_params=_params(("arbitrary", "arbitrary")),
    )(core, g.reshape(4, 2, r, cols), land)


def _run_rider(name, rider):
    n_in, n_out = len(rider.ins), len(rider.out_shape)

    def body(*refs):
        ride = (refs[:n_in], refs[n_in:n_in + n_out], refs[n_in + n_out:])
        rider.start(*ride)
        rider.relay(*ride)
        rider.mid(*ride)
        rider.finish(*ride)

    rider.results = pl.pallas_call(
        body, name=name, out_shape=rider.out_shape, in_specs=[ANY] * n_in, out_specs=[ANY] * n_out,
        scratch_shapes=rider.scratch, input_output_aliases=dict(rider.aliases),
        compiler_params=pltpu.CompilerParams(has_side_effects=True),
    )(*rider.ins)
    return rider.results


def _as_matrix(name, g):
    if name == 'w_in':
        return g
    if name in COL_SHARDED:
        return jnp.transpose(g, (1, 0, 2)).reshape(g.shape[1], N_DEV * g.shape[2])
    return g.reshape(N_DEV * g.shape[1], g.shape[2])


def _by_owner(name, g):
    if name == 'w_in':
        return g
    if name in COL_SHARDED:
        return jnp.transpose(g.reshape(g.shape[0], N_DEV, g.shape[1] // N_DEV), (1, 0, 2))
    return g.reshape(N_DEV, g.shape[0] // N_DEV, g.shape[1])


class _Comm:
    def __init__(self, shards):
        self.shards, self.W, self.parts, self.partial, self.sent = shards, {}, {}, {}, {}
        self.from_sibling, self.pairs = {}, {}

    def _ride(self, cls, names, srcs, part, sink):
        part = part or {}
        rider = cls(srcs, rows=[part[n][:2] if n in part else None for n in names],
                    fill=[self.partial.pop((sink, n), None) for n in names])
        rider.names, rider.sink = names, sink
        rider.unfinished = {n for n in names if n in part and not part[n][2]}
        return rider

    def gather(self, names, part=None):
        return self._ride(_GatherRider, names, [self.shards[n] for n in names], part, 'W')

    def exchange(self, names, G, part=None):
        for n in names:
            if n not in self.sent:
                self.sent[n] = _by_owner(n, G[n])
        return self._ride(_ExchangeRider, names, [self.sent[n] for n in names], part, 'parts')

    def to_sibling(self, names, G):
        for n in names:
            self.sent[n] = _by_owner(n, G[n])
        rider = _SiblingRider([self.sent[n] for n in names])
        rider.names, rider.sink, rider.unfinished = names, 'sibling', set()
        return rider

    def pairsum(self, names):
        for n in names:
            self.pairs[n] = _pairsum("pairsum_" + n, self.sent[n], self.from_sibling.pop(n))

    def to_owner(self, names, part=None):
        return self._ride(_ChipRider, names, [self.pairs[n] for n in names], part, 'parts')

    def collect(self, rider):
        for n, res in zip(rider.names, rider.results):
            if n in rider.unfinished:
                self.partial[(rider.sink, n)] = res
            elif rider.sink == 'W':
                self.W[n] = _as_matrix(n, res)
            elif rider.sink == 'sibling':
                self.from_sibling[n] = res
            else:
                self.parts[n] = res


def _adamw_math(p_ref, w_ref, m_ref, v_ref, g_ref, d_ref, nm_ref, nv_ref):
    c1 = 1.0 - ADAM_B1 ** ADAM_STEP
    c2 = 1.0 - ADAM_B2 ** ADAM_STEP
    g = p_ref[0].astype(F32)
    for s in range(1, p_ref.shape[0]):
        g = g + p_ref[s].astype(F32)
    nm = ADAM_B1 * m_ref[...] + (1.0 - ADAM_B1) * g
    nv = ADAM_B2 * v_ref[...] + (1.0 - ADAM_B2) * (g * g)
    g_ref[...] = g
    nm_ref[...] = nm
    nv_ref[...] = nv
    d_ref[...] = -ADAM_LR * ((nm / c1) / (jnp.sqrt(nv / c2) + ADAM_EPS) + ADAM_WD * w_ref[...])


def _adamw_vectors(parts, ws, ms, vs, loss_parts):
    k = len(ws)

    def body(*refs):
        for i in range(k):
            _adamw_math(refs[i], refs[k + i], refs[2 * k + i], refs[3 * k + i], *refs[4 * k + 1 + 4 * i:4 * k + 5 + 4 * i])
        lp, lo = refs[4 * k], refs[8 * k + 1]
        lo[...] = functools.reduce(jnp.add, [lp[s] for s in range(lp.shape[0])])

    return pl.pallas_call(
        body, name="adamw_vectors",
        out_shape=[jax.ShapeDtypeStruct(w.shape, F32) for w in ws for _ in range(4)] + [jax.ShapeDtypeStruct((1, 128), F32)],
        compiler_params=_params(),
    )(*parts, *ws, *ms, *vs, loss_parts)


def _adamw(name, parts, w, m, v, tb):
    n, R, Wd = parts.shape
    assert R % tb == 0
    body = functools.partial(_adamw_math)

    row = pl.BlockSpec((tb, Wd), lambda i: (i, 0))
    return pl.pallas_call(
        body, name=name, grid=(R // tb,),
        in_specs=[pl.BlockSpec((n, tb, Wd), lambda i: (0, i, 0)), row, row, row],
        out_specs=[row, row, row, row], out_shape=[jax.ShapeDtypeStruct((R, Wd), F32)] * 4,
        compiler_params=_params(("arbitrary",)),
    )(parts, w, m, v)


def _row_tile(R, unit, cap):
    best = unit
    for t in range(unit, cap + 1, unit):
        if R % t == 0:
            best = t
    return best


def kernel(x, ffn1_w_gate, ffn1_w_up, ffn1_w_down, ln1_g, ln1_b, w_in, b_in, ret_gn_g, conv_k, conv_b, conv_ln_g, conv_ln_b, w_ret_o, w_conv_o, w_out, ln2_g, ln2_b, ffn2_w_gate, ffn2_w_up, ffn2_w_down, ln3_g, ln3_b, loss_target, m_ffn1_w_gate, m_ffn1_w_up, m_ffn1_w_down, m_ln1_g, m_ln1_b, m_w_in, m_b_in, m_ret_gn_g, m_conv_k, m_conv_b, m_conv_ln_g, m_conv_ln_b, m_w_ret_o, m_w_conv_o, m_w_out, m_ln2_g, m_ln2_b, m_ffn2_w_gate, m_ffn2_w_up, m_ffn2_w_down, m_ln3_g, m_ln3_b, v_ffn1_w_gate, v_ffn1_w_up, v_ffn1_w_down, v_ln1_g, v_ln1_b, v_w_in, v_b_in, v_ret_gn_g, v_conv_k, v_conv_b, v_conv_ln_g, v_conv_ln_b, v_w_ret_o, v_w_conv_o, v_w_out, v_ln2_g, v_ln2_b, v_ffn2_w_gate, v_ffn2_w_up, v_ffn2_w_down, v_ln3_g, v_ln3_b):
    given = dict(locals())
    wts = {n: given[n] for n in WEIGHTS}
    mom = {n: given['m_' + n] for n in WEIGHTS}
    var = {n: given['v_' + n] for n in WEIGHTS}

    def shard2d(a):
        return a.reshape(a.shape[-3] * a.shape[-2] if a.ndim == 4 else a.shape[-2], a.shape[-1])

    comm = _Comm({n: shard2d(wts[n]).astype(BF16) for n in BIG})
    P = {n: wts[n].reshape(1, -1) for n in SMALL}
    loss, grad_x, _, S = _local_step(x[0], loss_target[0], None, P, comm=comm)

    parts = comm.parts
    res = {}
    for n in BIG:
        rows, cols = parts[n].shape[1:]
        tb = rows if rows % 16 else _row_tile(rows, 16, max(16, (256 * 1024) // cols))
        res[n] = _adamw("adamw_" + n, parts[n], shard2d(wts[n]), shard2d(mom[n]), shard2d(var[n]), tb)

    vec_parts = _run_rider("gather_vector_grads", _GatherRider([S[n] for n in SMALL] + [loss]))
    vec = _adamw_vectors(vec_parts[:-1], [P[n] for n in SMALL], [mom[n].reshape(1, -1) for n in SMALL],
                         [var[n].reshape(1, -1) for n in SMALL], vec_parts[-1])
    for i, n in enumerate(SMALL):
        res[n] = vec[4 * i:4 * i + 4]

    outs = [vec[-1][0, 0], grad_x[None]]
    for k in range(4):
        for n in WEIGHTS:
            outs.append(res[n][k].reshape(wts[n].shape))
    return tuple(outs)
```

```python
import functools
import math

import jax
import jax.numpy as jnp
from jax import lax
from jax.experimental import pallas as pl
from jax.experimental.pallas import tpu as pltpu

F32 = jnp.float32
BF16 = jnp.bfloat16

N_DEV = 8
LN_EPS = 1e-5
ALPHA = 2.0 ** 0.25
RET_DK = 128
RET_DV = 256
RET_CHUNK = 256
ROPE_BASE = 10000.0
CONV_WIDTH = 31
HALO = 32
ADAM_LR, ADAM_B1, ADAM_B2, ADAM_EPS, ADAM_WD, ADAM_STEP = 0.001, 0.9, 0.999, 1e-08, 0.01, 10
VMEM_LIMIT = 52 * 1024 * 1024
MESH = pl.DeviceIdType.MESH
ANY = pl.BlockSpec(memory_space=pl.ANY)

BIG = ['ffn1_w_gate', 'ffn1_w_up', 'ffn1_w_down', 'w_in', 'w_ret_o', 'w_conv_o', 'w_out',
       'ffn2_w_gate', 'ffn2_w_up', 'ffn2_w_down', 'conv_k']
COL_SHARDED = {'ffn1_w_gate', 'ffn1_w_up', 'w_in', 'ffn2_w_gate', 'ffn2_w_up', 'conv_k'}
SMALL = ['ln1_g', 'ln1_b', 'b_in', 'ret_gn_g', 'conv_b', 'conv_ln_g', 'conv_ln_b', 'ln2_g', 'ln2_b', 'ln3_g', 'ln3_b']
WEIGHTS = ['ffn1_w_gate', 'ffn1_w_up', 'ffn1_w_down', 'ln1_g', 'ln1_b', 'w_in', 'b_in', 'ret_gn_g', 'conv_k', 'conv_b',
           'conv_ln_g', 'conv_ln_b', 'w_ret_o', 'w_conv_o', 'w_out', 'ln2_g', 'ln2_b', 'ffn2_w_gate', 'ffn2_w_up',
           'ffn2_w_down', 'ln3_g', 'ln3_b']


def _params(sem=None):
    return pltpu.CompilerParams(dimension_semantics=sem, vmem_limit_bytes=VMEM_LIMIT)


def _sigmoid(x):
    return jax.nn.sigmoid(x)


def _dsilu(x, sg):
    return sg * (1.0 + x * (1.0 - sg))


def _fit(dim, want):
    if dim <= want:
        return dim
    return max(t for t in range(128, want + 1, 128) if dim % t == 0)


def _dot(a, b, ta=False, tb=False):
    dn = (((0,) if ta else (1,), (1,) if tb else (0,)), ((), ()))
    return lax.dot_general(a, b, dn, preferred_element_type=F32)


def _mm(name, As, Bs, prods, epi, out_dtypes, *, ta=False, tb=False, tm, tn, tk, extras=(), i_outer=True,
        b3=False, o3=False, rider=None, bsum=False, epi_rows=0):
    a0, b0 = As[0], Bs[0]
    M, K = (a0.shape[1], a0.shape[0]) if ta else a0.shape
    if b3:
        S, rows, cs = b0.shape
        N = rows if tb else S * cs
        assert K == (S * cs if tb else rows)
        tn, tk = (tn, cs) if tb else (cs, tk)
    else:
        N = b0.shape[0] if tb else b0.shape[1]
    tm, tn, tk = _fit(M, tm), _fit(N, tn), _fit(K, tk)
    assert M % tm == 0 and N % tn == 0 and K % tk == 0, (name, M, N, K, tm, tn, tk)
    gi, gj, gk = M // tm, N // tn, K // tk
    grid = (gi, gj, gk) if i_outer else (gj, gi, gk)

    def ij(g0, g1):
        return (g0, g1) if i_outer else (g1, g0)

    def amap(g0, g1, k):
        i, _ = ij(g0, g1)
        return (k, i) if ta else (i, k)

    def bmap(g0, g1, k):
        _, j = ij(g0, g1)
        return (j, k) if tb else (k, j)

    def bmap3(g0, g1, k):
        _, j = ij(g0, g1)
        return (k, j, 0) if tb else (j, k, 0)

    in_specs = [pl.BlockSpec((tk, tm) if ta else (tm, tk), amap) for _ in As]
    if b3:
        in_specs += [pl.BlockSpec((None, tn, tk) if tb else (None, tk, tn), bmap3) for _ in Bs]
    else:
        in_specs += [pl.BlockSpec((tn, tk) if tb else (tk, tn), bmap) for _ in Bs]
    args = list(As) + list(Bs)
    for arr, kind, coloff in extras:
        assert coloff % tn == 0
        off = coloff // tn
        if kind == 'mn':
            in_specs.append(pl.BlockSpec((tm, tn), lambda g0, g1, k, off=off: (ij(g0, g1)[0], ij(g0, g1)[1] + off)))
        else:
            in_specs.append(pl.BlockSpec((1, tn), lambda g0, g1, k, off=off: (0, ij(g0, g1)[1] + off)))
        args.append(arr)
    if o3:
        out_shape = [jax.ShapeDtypeStruct((gj, M, tn), dt) for dt in out_dtypes]
        out_specs = [pl.BlockSpec((None, tm, tn), lambda g0, g1, k: (ij(g0, g1)[1], ij(g0, g1)[0], 0))
                     for _ in out_dtypes]
    else:
        out_shape = [jax.ShapeDtypeStruct((M, N), dt) for dt in out_dtypes]
        out_specs = [pl.BlockSpec((tm, tn), lambda g0, g1, k: ij(g0, g1)) for _ in out_dtypes]
    if bsum:
        assert gi == 1 and not tb and not b3
        out_shape.append(jax.ShapeDtypeStruct((1, N), F32))
        out_specs.append(pl.BlockSpec((1, tn), lambda g0, g1, k: (0, ij(g0, g1)[1])))
    n_a, n_b, n_e, n_o = len(As), len(Bs), len(extras), len(out_shape)
    n_p = len(prods) if gk > 1 else 0
    scratch = [pltpu.VMEM((tm, tn), F32) for _ in range(n_p)]
    if rider is not None:
        in_specs, out_specs = in_specs + [ANY] * len(rider.ins), out_specs + [ANY] * len(rider.out_shape)
        args, out_shape, scratch = args + rider.ins, out_shape + rider.out_shape, scratch + rider.scratch
    n_in, n_out = len(args), len(out_shape)

    def body(*refs):
        a_refs = refs[:n_a]
        b_refs = refs[n_a:n_a + n_b]
        e_refs = refs[n_a + n_b:n_a + n_b + n_e]
        o_refs = refs[n_in:n_in + n_o]
        acc_refs = refs[n_in + n_out:n_in + n_out + n_p]
        k = pl.program_id(2)
        if rider is not None:
            step = (pl.program_id(0) * grid[1] + pl.program_id(1)) * gk + k
            ride = (step, grid[0] * grid[1] * gk, refs[n_a + n_b + n_e:n_in], refs[n_in + n_o:n_in + n_out],
                    refs[n_in + n_out + n_p:])
            rider.begin(*ride)

        def finish(accs, rows=slice(None)):
            ex = [(e[rows, :] if kind == 'mn' else e[...]).astype(F32) for e, (_, kind, _) in zip(e_refs, extras)]
            for o, r in zip(o_refs, epi(accs, *ex)):
                o[rows, :] = r.astype(o.dtype)

        if bsum:
            @pl.when(k == 0)
            def _():
                o_refs[-1][...] = jnp.zeros_like(o_refs[-1])

            o_refs[-1][...] += _colsum(b_refs[0][...].astype(F32))

        if gk == 1:
            sub = tm if ta or not epi_rows else _fit(tm, epi_rows)
            for r0 in range(0, tm, sub):
                rows = slice(None) if ta else slice(r0, r0 + sub)
                finish([functools.reduce(jnp.add, [_dot(a_refs[ai][...] if ta else a_refs[ai][rows, :],
                                                        b_refs[bi][...], ta, tb) for ai, bi in terms])
                        for terms in prods], rows)
        else:
            @pl.when(k == 0)
            def _():
                for acc in acc_refs:
                    acc[...] = jnp.zeros_like(acc)

            for p, terms in enumerate(prods):
                for ai, bi in terms:
                    acc_refs[p][...] += _dot(a_refs[ai][...], b_refs[bi][...], ta, tb)

            @pl.when(k == gk - 1)
            def _():
                finish([acc[...] for acc in acc_refs])

        if rider is not None:
            rider.end(*ride)

    aliases = {} if rider is None else {n_a + n_b + n_e + p: n_o + o for p, o in rider.aliases.items()}
    res = pl.pallas_call(
        body, name=name, grid=grid, in_specs=in_specs, out_specs=out_specs, out_shape=out_shape,
        scratch_shapes=scratch, input_output_aliases=aliases,
        compiler_params=_params(("arbitrary", "arbitrary", "arbitrary")),
    )(*args)
    if rider is not None:
        rider.results = res[n_o:]
    return res[:n_o]


def _rows(name, fn, ins, outs, *, T, tb, rider=None):
    tb = min(tb, T)
    assert T % tb == 0
    in_specs, args = [], []
    for arr, kind, width, cb in ins:
        if kind == 'r':
            in_specs.append(pl.BlockSpec((tb, width), lambda i, _, cb=cb: (i, cb)))
        else:
            in_specs.append(pl.BlockSpec((1, width), lambda i, _, cb=cb: (0, cb)))
        args.append(arr)
    out_shape, out_specs = [], []
    for kind, width, dtype in outs:
        if kind == 'r':
            out_shape.append(jax.ShapeDtypeStruct((T, width), dtype))
            out_specs.append(pl.BlockSpec((tb, width), lambda i, _: (i, 0)))
        elif kind == 'c':
            out_shape.append(jax.ShapeDtypeStruct((T, 1), dtype))
            out_specs.append(pl.BlockSpec((tb, 1), lambda i, _: (i, 0)))
        else:
            out_shape.append(jax.ShapeDtypeStruct((1, width), F32))
            out_specs.append(pl.BlockSpec((1, width), lambda i, _: (0, 0)))
    n_in = len(ins)

    def body(*refs):
        i = pl.program_id(0)
        vals = fn(*[r[...] for r in refs[:n_in]])
        for (kind, _, _), o, v in zip(outs, refs[n_in:], vals):
            if kind == 'a':
                @pl.when(i == 0)
                def _(o=o):
                    o[...] = jnp.zeros_like(o)

                o[...] += v
            else:
                o[...] = v.astype(o.dtype)

    return _hosted_call(body, rider, name=name, grid=(T // tb, 1), in_specs=in_specs, out_specs=out_specs,
                        out_shape=out_shape, scratch=[], args=args)


def _colsum(v):
    return jnp.sum(v, axis=0, keepdims=True)


def _ln_stats(z):
    mu = jnp.mean(z, axis=-1, keepdims=True)
    d = z - mu
    var = jnp.mean(d * d, axis=-1, keepdims=True)
    rstd = lax.rsqrt(var + LN_EPS)
    return d * rstd, rstd


def _ln_bwd_math(dy, xhat, rstd, g):
    dxh = dy * g
    m1 = jnp.mean(dxh, axis=-1, keepdims=True)
    m2 = jnp.mean(dxh * xhat, axis=-1, keepdims=True)
    return rstd * (dxh - m1 - xhat * m2)


def _ln_fwd(name, z, g, b, T, D, rider=None):
    def fn(z, g, b):
        xhat, rstd = _ln_stats(z)
        y = xhat * g + b
        return [y, y, xhat, rstd]

    return _rows(name, fn, [(z, 'r', D, 0), (g, 'v', D, 0), (b, 'v', D, 0)],
                 [('r', D, F32), ('r', D, BF16), ('r', D, F32), ('c', 1, F32)], T=T, tb=512, rider=rider)


def _ln_bwd(name, dy, xhat, rstd, g, scale, T, D, rider=None):
    def fn(dy, xhat, rstd, g):
        dz = _ln_bwd_math(dy, xhat, rstd, g)
        return [dz, dz * scale, _colsum(dy * xhat), _colsum(dy)]

    return _rows(name, fn, [(dy, 'r', D, 0), (xhat, 'r', D, 0), (rstd, 'r', 1, 0), (g, 'v', D, 0)],
                 [('r', D, F32), ('r', D, BF16), ('a', D, F32), ('a', D, F32)], T=T, tb=512, rider=rider)


def _ln_loss_bwd(name, z, g, b, tgt, T, D):
    def fn(z, g, b, tgt):
        xhat, rstd = _ln_stats(z)
        err = xhat * g + b - tgt
        row_loss = 0.5 * jnp.mean(err * err, axis=-1, keepdims=True)
        loss = jnp.broadcast_to(jnp.sum(row_loss, axis=0, keepdims=True), (1, 128))
        dy = err * (1.0 / D)
        dz = _ln_bwd_math(dy, xhat, rstd, g)
        return [dz, dz * 0.5, _colsum(dy * xhat), _colsum(dy), loss]

    return _rows(name, fn, [(z, 'r', D, 0), (g, 'v', D, 0), (b, 'v', D, 0), (tgt, 'r', D, 0)],
                 [('r', D, F32), ('r', D, BF16), ('a', D, F32), ('a', D, F32), ('a', 128, F32)], T=T, tb=512)


class _Net:
    def __init__(self, comm, G):
        self.comm, self.G = comm, G

    def gather(self, names, part=None):
        return self.comm.gather(names, part) if self.comm else None

    def exchange(self, names, part=None):
        return self.comm.exchange(names, self.G, part) if self.comm else None

    def to_sibling(self, names):
        return self.comm.to_sibling(names, self.G) if self.comm else None

    def pairsum(self, names):
        if self.comm:
            self.comm.pairsum(names)

    def to_owner(self, names, part=None):
        return self.comm.to_owner(names, part) if self.comm else None

    def done(self, rider):
        if rider is not None:
            self.comm.collect(rider)


def _ffn_fwd(tag, xb, x, W, names, net, rider=None, rider_down=None, rider_up=None):
    def epi_gu(accs):
        a, b = accs
        return [a, b, a * _sigmoid(a) * b]

    ng, nu, nd = names
    if rider_up is None:
        a, b, s = _mm(tag + "_gate_up", [xb], [W[ng], W[nu]], [[(0, 0)], [(0, 1)]], epi_gu, [BF16, BF16, BF16],
                      tm=1024, tn=1408, tk=1024, rider=rider, epi_rows=256)
        net.done(rider)
    else:
        (a,) = _mm(tag + "_gate", [xb], [W[ng]], [[(0, 0)]], lambda accs: accs, [BF16], tm=1024, tn=1408, tk=1024,
                   rider=rider)
        net.done(rider)
        rider_up = rider_up()
        b, s = _mm(tag + "_up", [xb], [W[nu]], [[(0, 0)]], lambda accs, a_: [accs[0], a_ * _sigmoid(a_) * accs[0]],
                   [BF16, BF16], tm=1024, tn=1408, tk=1024, rider=rider_up, epi_rows=256, extras=[(a, 'mn', 0)])
        net.done(rider_up)

    def epi_down(accs, xres):
        return [ALPHA * xres + 0.5 * accs[0]]

    rider_down = rider_down() if rider_down else None
    (z,) = _mm(tag + "_down", [s], [W[nd]], [[(0, 0)]], epi_down, [F32], tm=1024, tn=1024, tk=1408,
               extras=[(x, 'mn', 0)], rider=rider_down)
    net.done(rider_down)
    return a, b, s, z


def _ffn_bwd(tag, dzh, dz, xb, a, b, s, W, names, gdt, G, net, ride, pre=(None, None)):
    ng, nu, nd = names

    def epi_ds(accs, a, b):
        ds = accs[0]
        sg = _sigmoid(a)
        return [ds * b * _dsilu(a, sg), ds * a * sg]

    rider = pre[0]() if pre[0] else None
    da, db = _mm(tag + "_ds", [dzh], [W[nd]], [[(0, 0)]], epi_ds, [BF16, BF16], tb=True, tm=1024, tn=1408, tk=1024, epi_rows=256,
                 extras=[(a, 'mn', 0), (b, 'mn', 0)], rider=rider)
    net.done(rider)
    ident = lambda accs: accs
    rider = pre[1]() if pre[1] else None
    (G[nd],) = _mm(tag + "_dwd", [s], [dzh], [[(0, 0)]], ident, [gdt], ta=True, tm=1408, tn=1024, tk=1024,
                   rider=rider)
    net.done(rider)
    if ride == 'all':
        rider = net.to_sibling([nd])
        (G[ng],) = _mm(tag + "_dwg", [xb], [da], [[(0, 0)]], ident, [gdt], ta=True, tm=1024, tn=1408, tk=1024,
                       rider=rider)
        net.done(rider)
        net.pairsum([nd])
        rider = net.to_owner([nd])
        (G[nu],) = _mm(tag + "_dwu", [xb], [db], [[(0, 0)]], ident, [gdt], ta=True, tm=1024, tn=1408, tk=1024,
                       rider=rider)
        net.done(rider)
        rider = net.to_sibling([ng, nu])
        if rider is not None:
            _run_rider(tag + "_to_sibling", rider)
            net.done(rider)
        net.pairsum([ng, nu])
        rider = net.to_owner([ng, nu])
    else:
        rider = net.to_sibling([nd]) if ride else None
        G[ng], G[nu] = _mm(tag + "_dwgu", [xb], [da, db], [[(0, 0)], [(0, 1)]], ident, [gdt, gdt], ta=True,
                           tm=1024, tn=1408, tk=1024, rider=rider)
        net.done(rider)
        if ride:
            net.pairsum([nd])
        rider = net.to_owner([nd]) if ride else None

    def epi_dx(accs, dzres):
        return [ALPHA * dzres + accs[0]]

    (dx,) = _mm(tag + "_dx", [da, db], [W[ng], W[nu]], [[(0, 0), (1, 1)]], epi_dx, [F32], tb=True,
                tm=1024, tn=1024, tk=1408, extras=[(dz, 'mn', 0)], rider=rider)
    net.done(rider)
    return dx


def _ret_tables(H, T):
    C = RET_CHUNK
    log_g = jnp.log(1.0 - jnp.exp2(-5.0 - jnp.arange(H, dtype=F32)))
    idx = jnp.arange(C, dtype=F32)
    diff = idx[:, None] - idx[None, :]
    dm = jnp.where(diff[None] >= 0, jnp.exp(jnp.maximum(diff, 0.0)[None] * log_g[:, None, None]), 0.0)
    xi = jnp.exp((idx[None, :] + 1.0) * log_g[:, None])[:, :, None]
    zeta = jnp.exp((C - 1.0 - idx)[None, :] * log_g[:, None])[:, :, None]
    gc = jnp.broadcast_to(jnp.exp(C * log_g)[:, None, None], (H, 1, RET_DV))
    half = RET_DK // 2
    freqs = ROPE_BASE ** (-jnp.arange(half, dtype=F32) / half)
    ang = jnp.arange(T, dtype=F32)[:, None] * freqs[None, :]
    cos, sin = jnp.cos(ang), jnp.sin(ang)
    cosf = jnp.concatenate([cos, cos], axis=1)
    sins = jnp.concatenate([-sin, sin], axis=1)
    return dm, xi, zeta, gc, cosf, sins


def _rot(x, cosf, sins):
    return x * cosf + pltpu.roll(x, RET_DK // 2, 1) * sins


def _rot_bwd(dy, cosf, sins):
    return dy * cosf + pltpu.roll(dy * sins, RET_DK // 2, 1)


RET_HB = 8


def _ret_specs(H, HB, rev, NC):
    C, G = RET_CHUNK, H // HB
    nn = (lambda n: NC - 1 - n) if rev else (lambda n: n)
    return [
        pl.BlockSpec((C, HB * RET_DK), lambda h, n: (nn(n), h)),
        pl.BlockSpec((C, HB * RET_DK), lambda h, n: (nn(n), G + h)),
        pl.BlockSpec((C, HB * RET_DV), lambda h, n: (nn(n), G + h)),
        pl.BlockSpec((C, HB * RET_DV), lambda h, n: (nn(n), 2 * G + h)),
        pl.BlockSpec((C, RET_DK), lambda h, n: (nn(n), 0)),
        pl.BlockSpec((C, RET_DK), lambda h, n: (nn(n), 0)),
        pl.BlockSpec((1, HB * RET_DV), lambda h, n: (0, h)),
        pl.BlockSpec((HB, C, C), lambda h, n: (h, 0, 0)),
        pl.BlockSpec((HB, C, 1), lambda h, n: (h, 0, 0)),
        pl.BlockSpec((HB, C, 1), lambda h, n: (h, 0, 0)),
        pl.BlockSpec((HB, 1, RET_DV), lambda h, n: (h, 0, 0)),
    ]


def _ret_fwd(proj, gn_g, tabs, H, T, rider=None):
    C, NC = RET_CHUNK, T // RET_CHUNK
    HB = min(RET_HB, H)
    dm, xi, zeta, gc, cosf, sins = tabs
    scale = RET_DK ** -0.5

    def body(q_ref, k_ref, v_ref, g_ref, cos_ref, sin_ref, gn_ref, dm_ref, xi_ref, zt_ref, gc_ref,
             r_ref, ri_ref, st_ref, state):
        @pl.when(pl.program_id(1) == 0)
        def _():
            state[...] = jnp.zeros_like(state)

        cs, sn = cos_ref[...], sin_ref[...]
        hs = range(HB)
        qk = [slice(h * RET_DK, (h + 1) * RET_DK) for h in hs]
        vv = [slice(h * RET_DV, (h + 1) * RET_DV) for h in hs]
        kr = [_rot(k_ref[:, qk[h]].astype(F32), cs, sn) for h in hs]
        qb = [(_rot(q_ref[:, qk[h]].astype(F32), cs, sn) * scale).astype(BF16) for h in hs]
        kb = [kr[h].astype(BF16) for h in hs]
        kzb = [(kr[h] * zt_ref[h]).astype(BF16) for h in hs]
        vb = [v_ref[:, vv[h]].astype(BF16) for h in hs]
        st = [state[h] for h in hs]
        stb = [st[h].astype(BF16) for h in hs]
        sb = [(_dot(qb[h], kb[h], tb=True) * dm_ref[h]).astype(BF16) for h in hs]
        cross = [_dot(qb[h], stb[h]) for h in hs]
        kv = [_dot(kzb[h], vb[h], ta=True) for h in hs]
        intra = [_dot(sb[h], vb[h]) for h in hs]
        for h in hs:
            st_ref[h] = stb[h]
            state[h] = gc_ref[h] * st[h] + kv[h]
        for h in hs:
            r = intra[h] + cross[h] * xi_ref[h]
            rhat, _ = _ln_stats(r)
            g = g_ref[:, vv[h]].astype(F32)
            r_ref[:, vv[h]] = r
            ri_ref[:, vv[h]] = (g * _sigmoid(g) * (rhat * gn_ref[:, vv[h]])).astype(BF16)

    VW = H * RET_DV
    return _hosted_call(
        body, rider, name="ret_fwd", grid=(H // HB, NC), in_specs=_ret_specs(H, HB, False, NC),
        out_specs=[pl.BlockSpec((C, HB * RET_DV), lambda h, n: (n, h)),
                   pl.BlockSpec((C, HB * RET_DV), lambda h, n: (n, h)),
                   pl.BlockSpec((HB, None, RET_DK, RET_DV), lambda h, n: (h, n, 0, 0))],
        out_shape=[jax.ShapeDtypeStruct((T, VW), F32), jax.ShapeDtypeStruct((T, VW), BF16),
                   jax.ShapeDtypeStruct((H, NC, RET_DK, RET_DV), BF16)],
        scratch=[pltpu.VMEM((HB, RET_DK, RET_DV), F32)],
        args=[proj, proj, proj, proj, cosf, sins, gn_g, dm, xi, zeta, gc])


def _hosted_call(body, rider, *, name, grid, in_specs, out_specs, out_shape, scratch, args):
    n_in, n_out, n_scr = len(args), len(out_shape), len(scratch)
    if rider is None:
        hosted = body
    else:
        n_ri, n_ro = len(rider.ins), len(rider.out_shape)
        in_specs, out_specs = in_specs + [ANY] * n_ri, out_specs + [ANY] * n_ro
        args, out_shape, scratch = args + rider.ins, out_shape + rider.out_shape, scratch + rider.scratch

        def hosted(*refs):
            o0, s0 = n_in + n_ri, n_in + n_ri + n_out + n_ro
            step = pl.program_id(0) * grid[1] + pl.program_id(1)
            ride = (step, grid[0] * grid[1], refs[n_in:o0], refs[o0 + n_out:s0], refs[s0 + n_scr:])
            rider.begin(*ride)
            body(*refs[:n_in], *refs[o0:o0 + n_out], *refs[s0:s0 + n_scr])
            rider.end(*ride)

    aliases = {} if rider is None else {n_in + p: n_out + o for p, o in rider.aliases.items()}
    res = pl.pallas_call(
        hosted, name=name, grid=grid, in_specs=in_specs, out_specs=out_specs, out_shape=out_shape,
        scratch_shapes=scratch, input_output_aliases=aliases, compiler_params=_params(("arbitrary", "arbitrary")),
    )(*args)
    if rider is not None:
        rider.results = res[n_out:]
    return res[:n_out]


def _ret_bwd(dri, r, states, proj, gn_g, tabs, H, T, in_w, rider=None):
    C, NC = RET_CHUNK, T // RET_CHUNK
    HB = min(RET_HB, H)
    dm, xi, zeta, gc, cosf, sins = tabs
    scale = RET_DK ** -0.5

    def body(q_ref, k_ref, v_ref, g_ref, cos_ref, sin_ref, gn_ref, dm_ref, xi_ref, zt_ref, gc_ref,
             dri_ref, r_ref, st_ref, dp_ref, dgn_ref, dstate):
        @pl.when(pl.program_id(1) == 0)
        def _():
            dstate[...] = jnp.zeros_like(dstate)
            dgn_ref[...] = jnp.zeros_like(dgn_ref)

        cs, sn = cos_ref[...], sin_ref[...]
        hs = range(HB)
        qk = [slice(h * RET_DK, (h + 1) * RET_DK) for h in hs]
        vv = [slice(h * RET_DV, (h + 1) * RET_DV) for h in hs]
        qr = [_rot(q_ref[:, qk[h]].astype(F32), cs, sn) * scale for h in hs]
        kr = [_rot(k_ref[:, qk[h]].astype(F32), cs, sn) for h in hs]
        qb = [qr[h].astype(BF16) for h in hs]
        kb = [kr[h].astype(BF16) for h in hs]
        vb = [v_ref[:, vv[h]].astype(BF16) for h in hs]
        qxb = [(qr[h] * xi_ref[h]).astype(BF16) for h in hs]
        kzb = [(kr[h] * zt_ref[h]).astype(BF16) for h in hs]
        drb = []
        for h in hs:
            rhat, rstd = _ln_stats(r_ref[:, vv[h]])
            g, gn, dpre = g_ref[:, vv[h]].astype(F32), gn_ref[:, vv[h]], dri_ref[:, vv[h]]
            sg = _sigmoid(g)
            dp_ref[:, 2 * QW + VW + h * RET_DV:2 * QW + VW + (h + 1) * RET_DV] = (
                dpre * (rhat * gn) * _dsilu(g, sg)).astype(BF16)
            drn = dpre * (g * sg)
            dgn_ref[:, vv[h]] += _colsum(drn * rhat)
            drb.append(_ln_bwd_math(drn, rhat, rstd, gn).astype(BF16))
        ds1 = [dstate[h] for h in hs]
        ds1b = [ds1[h].astype(BF16) for h in hs]
        sb = [(_dot(qb[h], kb[h], tb=True) * dm_ref[h]).astype(BF16) for h in hs]
        dsb = [(_dot(drb[h], vb[h], tb=True) * dm_ref[h]).astype(BF16) for h in hs]
        dq_x = [_dot(drb[h], st_ref[h], tb=True) for h in hs]
        dk_x = [_dot(vb[h], ds1b[h], tb=True) for h in hs]
        dv_x = [_dot(kzb[h], ds1b[h]) for h in hs]
        dst = [_dot(qxb[h], drb[h], ta=True) for h in hs]
        for h in hs:
            dstate[h] = gc_ref[h] * ds1[h] + dst[h]
        dv_i = [_dot(sb[h], drb[h], ta=True) for h in hs]
        dq_i = [_dot(dsb[h], kb[h]) for h in hs]
        dk_i = [_dot(dsb[h], qb[h], ta=True) for h in hs]
        for h in hs:
            dp_ref[:, 2 * QW + h * RET_DV:2 * QW + (h + 1) * RET_DV] = (dv_i[h] + dv_x[h]).astype(BF16)
            dq = dq_i[h] + dq_x[h] * xi_ref[h]
            dk = dk_i[h] + dk_x[h] * zt_ref[h]
            dp_ref[:, qk[h]] = _rot_bwd(dq * scale, cs, sn).astype(BF16)
            dp_ref[:, QW + h * RET_DK:QW + (h + 1) * RET_DK] = _rot_bwd(dk, cs, sn).astype(BF16)

    VW, QW = H * RET_DV, H * RET_DK
    rv = lambda n: NC - 1 - n
    in_specs = _ret_specs(H, HB, True, NC) + [
        pl.BlockSpec((C, HB * RET_DV), lambda h, n: (rv(n), h)),
        pl.BlockSpec((C, HB * RET_DV), lambda h, n: (rv(n), h)),
        pl.BlockSpec((HB, None, RET_DK, RET_DV), lambda h, n: (h, rv(n), 0, 0)),
    ]
    assert HB == H
    return _hosted_call(
        body, rider, name="ret_bwd", grid=(1, NC), in_specs=in_specs,
        out_specs=[pl.BlockSpec((C, 2 * QW + 2 * VW), lambda h, n: (rv(n), 0)),
                   pl.BlockSpec((1, VW), lambda h, n: (0, 0))],
        out_shape=[jax.ShapeDtypeStruct((T, in_w), BF16), jax.ShapeDtypeStruct((1, VW), F32)],
        scratch=[pltpu.VMEM((HB, RET_DK, RET_DV), F32)],
        args=[proj, proj, proj, proj, cosf, sins, gn_g, dm, xi, zeta, gc, dri, r, states])


CONV_CW = 128
CONV_TB = 512


def _conv_fwd(proj, kpad, bias, off_a, CC, T, rider=None):
    tb, cw = min(CONV_TB, T), CONV_CW
    hb = tb // HALO
    ca, cb = off_a // cw, (off_a + CC) // cw

    def body(a_ref, b_ref, ap_ref, bp_ref, k_ref, bias_ref, u1_ref, win):
        i = pl.program_id(0)
        keep = (i > 0).astype(F32)
        win[0:HALO, :] = ap_ref[...].astype(F32) * _sigmoid(bp_ref[...].astype(F32)) * keep
        win[HALO:, :] = a_ref[...].astype(F32) * _sigmoid(b_ref[...].astype(F32))
        acc = jnp.broadcast_to(bias_ref[...], (tb, cw))
        for w in range(CONV_WIDTH):
            acc = acc + k_ref[w:w + 1, :] * win[pl.ds(HALO - (CONV_WIDTH - 1) + w, tb), :]
        u1_ref[...] = acc

    prev = lambda i: jnp.maximum(i * hb - 1, 0)
    (u1,) = _hosted_call(
        body, rider, name="conv_fwd", grid=(T // tb, CC // cw),
        in_specs=[pl.BlockSpec((tb, cw), lambda i, c: (i, ca + c)),
                  pl.BlockSpec((tb, cw), lambda i, c: (i, cb + c)),
                  pl.BlockSpec((HALO, cw), lambda i, c: (prev(i), ca + c)),
                  pl.BlockSpec((HALO, cw), lambda i, c: (prev(i), cb + c)),
                  pl.BlockSpec((HALO, cw), lambda i, c: (0, c)),
                  pl.BlockSpec((1, cw), lambda i, c: (0, c))],
        out_specs=[pl.BlockSpec((tb, cw), lambda i, c: (i, c))],
        out_shape=[jax.ShapeDtypeStruct((T, CC), F32)],
        scratch=[pltpu.VMEM((tb + HALO, cw), F32)],
        args=[proj, proj, proj, proj, kpad, bias])
    return u1


def _conv_bwd(du1, proj, kpad, off_a, CC, T, rider=None):
    tb, cw = min(CONV_TB, T), CONV_CW
    hb = tb // HALO
    nt = T // tb
    ca, cb = off_a // cw, (off_a + CC) // cw

    def body(d_ref, dn_ref, a_ref, b_ref, ap_ref, bp_ref, k_ref, da_ref, db_ref, dk_ref, winu, wind):
        i = pl.program_id(1)
        a, b = a_ref[...].astype(F32), b_ref[...].astype(F32)
        sgb = _sigmoid(b)
        winu[0:HALO, :] = ap_ref[...].astype(F32) * _sigmoid(bp_ref[...].astype(F32)) * (i > 0).astype(F32)
        winu[HALO:, :] = a * sgb
        d = d_ref[...]
        wind[0:tb, :] = d
        wind[tb:, :] = dn_ref[...] * (i < nt - 1).astype(F32)

        @pl.when(i == 0)
        def _():
            dk_ref[...] = jnp.zeros_like(dk_ref)

        du0 = jnp.zeros((tb, cw), F32)
        for w in range(CONV_WIDTH):
            du0 = du0 + k_ref[w:w + 1, :] * wind[pl.ds(CONV_WIDTH - 1 - w, tb), :]
            dk_ref[w:w + 1, :] += _colsum(winu[pl.ds(HALO - (CONV_WIDTH - 1) + w, tb), :] * d)
        da_ref[...] = (du0 * sgb).astype(BF16)
        db_ref[...] = (du0 * a * sgb * (1.0 - sgb)).astype(BF16)

    prev = lambda i: jnp.maximum(i * hb - 1, 0)
    nxt = lambda i: jnp.minimum((i + 1) * hb, T // HALO - 1)
    return _hosted_call(
        body, rider, name="conv_bwd", grid=(CC // cw, nt),
        in_specs=[pl.BlockSpec((tb, cw), lambda c, i: (i, c)),
                  pl.BlockSpec((HALO, cw), lambda c, i: (nxt(i), c)),
                  pl.BlockSpec((tb, cw), lambda c, i: (i, ca + c)),
                  pl.BlockSpec((tb, cw), lambda c, i: (i, cb + c)),
                  pl.BlockSpec((HALO, cw), lambda c, i: (prev(i), ca + c)),
                  pl.BlockSpec((HALO, cw), lambda c, i: (prev(i), cb + c)),
                  pl.BlockSpec((HALO, cw), lambda c, i: (0, c))],
        out_specs=[pl.BlockSpec((tb, cw), lambda c, i: (i, c)),
                   pl.BlockSpec((tb, cw), lambda c, i: (i, c)),
                   pl.BlockSpec((HALO, cw), lambda c, i: (0, c))],
        out_shape=[jax.ShapeDtypeStruct((T, CC), BF16), jax.ShapeDtypeStruct((T, CC), BF16),
                   jax.ShapeDtypeStruct((HALO, CC), F32)],
        scratch=[pltpu.VMEM((tb + HALO, cw), F32), pltpu.VMEM((tb + HALO, cw), F32)],
        args=[du1, du1, proj, proj, proj, proj, kpad])


FFN1 = ('ffn1_w_gate', 'ffn1_w_up', 'ffn1_w_down')
FFN2 = ('ffn2_w_gate', 'ffn2_w_up', 'ffn2_w_down')


def _local_step(x, tgt, W, P, gdt=BF16, comm=None):
    T, D = x.shape
    G = {}
    net = _Net(comm, G)
    if comm is not None:
        W = comm.W
        first = net.gather(['ffn1_w_gate'])
    (xb,) = _rows("x_to_bf16", lambda v: [v], [(x, 'r', D, 0)], [('r', D, BF16)], T=T, tb=512,
                  rider=first if comm is not None else None)
    if comm is not None:
        net.done(first)
    VW = P['ret_gn_g'].shape[1]
    H = VW // RET_DV
    QW = H * RET_DK
    CC = P['conv_b'].shape[1]
    off_glu = 2 * QW + 2 * VW
    off_gate = off_glu + 2 * CC
    ident = lambda accs: accs

    a1, b1, s1, z1 = _ffn_fwd("ffn1", xb, x, W, FFN1, net, rider=net.gather(['ffn1_w_up', 'ffn1_w_down']),
                              rider_up=lambda: net.gather(['w_in'], {'w_in': (0, D // 4, False)}),
                              rider_down=lambda: net.gather(['w_in'], {'w_in': (D // 4, (3 * D) // 8, False)}))
    rider = net.gather(['w_in'], {'w_in': ((5 * D) // 8, (3 * D) // 8, True)})
    x1, x1b, xh1, rs1 = _ln_fwd("ln1", z1, P['ln1_g'], P['ln1_b'], T, D, rider=rider)
    net.done(rider)

    rest = net.gather(['conv_k', 'w_ret_o', 'w_conv_o', 'w_out'])
    (proj,) = _mm("w_in", [x1b], [W['w_in']], [[(0, 0)]], lambda accs, bias: [accs[0] + bias], [F32],
                  tm=2048, tn=0, tk=1024, extras=[(P['b_in'], 'n', 0)], i_outer=False, b3=True, rider=rest)
    net.done(rest)
    tabs = _ret_tables(H, T)
    rider = net.gather(['ffn2_w_gate', 'ffn2_w_up'])
    r, ret_in, states = _ret_fwd(proj, P['ret_gn_g'], tabs, H, T, rider=rider)
    net.done(rider)
    kpad = jnp.pad(W['conv_k'].astype(F32), ((0, HALO - CONV_WIDTH), (0, 0)))
    u1 = _conv_fwd(proj, kpad, P['conv_b'], off_glu, CC, T)

    def conv_ln(u1, g, b):
        xhat, rstd = _ln_stats(u1)
        u2 = xhat * g + b
        return [xhat, rstd, u2 * _sigmoid(u2)]

    xhc, rsc, u3 = _rows("conv_ln", conv_ln, [(u1, 'r', CC, 0), (P['conv_ln_g'], 'v', CC, 0), (P['conv_ln_b'], 'v', CC, 0)],
                         [('r', CC, F32), ('c', 1, F32), ('r', CC, BF16)], T=T, tb=512)
    rider = net.gather(['ffn2_w_down'])
    (ret_out,) = _mm("ret_o", [ret_in], [W['w_ret_o']], [[(0, 0)]], ident, [F32], tm=1024, tn=1024, tk=2048,
                     rider=rider)
    net.done(rider)

    def epi_merge(accs, ret_out, gr, gc):
        conv_out = accs[0]
        return [conv_out, _sigmoid(gr) * ret_out + _sigmoid(gc) * conv_out]

    conv_out, merged = _mm("conv_o_merge", [u3], [W['w_conv_o']], [[(0, 0)]], epi_merge, [F32, BF16],
                           tm=512, tn=D, tk=1024, epi_rows=256,
                           extras=[(ret_out, 'mn', 0), (proj, 'mn', off_gate), (proj, 'mn', off_gate + D)])
    (z2,) = _mm("w_out", [merged], [W['w_out']], [[(0, 0)]], lambda accs, xr: [ALPHA * xr + accs[0]], [F32],
                tm=1024, tn=1024, tk=1024, extras=[(x1, 'mn', 0)])
    x2, x2b, xh2, rs2 = _ln_fwd("ln2", z2, P['ln2_g'], P['ln2_b'], T, D)
    a2, b2, s2, z3 = _ffn_fwd("ffn2", x2b, x2, W, FFN2, net)
    dz3, dz3h, g_ln3_g, g_ln3_b, loss = _ln_loss_bwd("ln3_loss", z3, P['ln3_g'], P['ln3_b'], tgt, T, D)

    S = {'ln3_g': g_ln3_g, 'ln3_b': g_ln3_b}
    dy2 = _ffn_bwd("ffn2b", dz3h, dz3, x2b, a2, b2, s2, W, FFN2, gdt, G, net, 'down')
    rider = net.to_sibling(['ffn2_w_gate', 'ffn2_w_up'])
    dz2, dz2b, S['ln2_g'], S['ln2_b'] = _ln_bwd("ln2b", dy2, xh2, rs2, P['ln2_g'], 1.0, T, D, rider=rider)
    net.done(rider)
    net.pairsum(['ffn2_w_gate', 'ffn2_w_up'])

    (G['w_out'],) = _mm("d_w_out", [merged], [dz2b], [[(0, 0)]], ident, [gdt], ta=True, tm=1024, tn=1024, tk=1024)

    def epi_dmerge(accs, ret_out, conv_out, gr, gc):
        dm_ = accs[0]
        sr, sc = _sigmoid(gr), _sigmoid(gc)
        return [dm_ * sr, dm_ * sc, dm_ * ret_out * sr * (1.0 - sr), dm_ * conv_out * sc * (1.0 - sc)]

    rider = net.to_sibling(['w_out'])
    dret_out, dconv_out, dgate_r, dgate_c = _mm(
        "d_merge", [dz2b], [W['w_out']], [[(0, 0)]], epi_dmerge, [BF16, BF16, BF16, BF16], tb=True,
        tm=512, tn=D, tk=1024, epi_rows=256, rider=rider,
        extras=[(ret_out, 'mn', 0), (conv_out, 'mn', 0), (proj, 'mn', off_gate), (proj, 'mn', off_gate + D)])
    net.done(rider)
    (G['w_ret_o'],) = _mm("d_w_ret_o", [ret_in], [dret_out], [[(0, 0)]], ident, [gdt], ta=True, tm=1024, tn=1024, tk=1024)
    (G['w_conv_o'],) = _mm("d_w_conv_o", [u3], [dconv_out], [[(0, 0)]], ident, [gdt], ta=True, tm=1024, tn=1024, tk=1024)
    rider = net.to_sibling(['w_ret_o', 'w_conv_o'])
    (dri,) = _mm("d_ret_in", [dret_out], [W['w_ret_o']], [[(0, 0)]], ident, [F32], tb=True, tm=1024, tn=1024, tk=1024,
                 rider=rider)
    net.done(rider)
    net.pairsum(['w_out', 'w_ret_o', 'w_conv_o'])
    rider = net.to_owner(['ffn2_w_gate', 'ffn2_w_up', 'w_out', 'w_ret_o', 'w_conv_o'])
    dproj, S['ret_gn_g'] = _ret_bwd(dri, r, states, proj, P['ret_gn_g'], tabs, H, T, proj.shape[1], rider=rider)
    net.done(rider)

    def epi_du2(accs, xhat, g, b):
        u2 = xhat * g + b
        return [accs[0] * _dsilu(u2, _sigmoid(u2))]

    (du2,) = _mm("d_u3", [dconv_out], [W['w_conv_o']], [[(0, 0)]], epi_du2, [F32], tb=True, tm=512, tn=CC, tk=1024, epi_rows=256,
                 extras=[(xhc, 'mn', 0), (P['conv_ln_g'], 'n', 0), (P['conv_ln_b'], 'n', 0)])

    def conv_ln_bwd(du2, xhat, rstd, g):
        du1 = _ln_bwd_math(du2, xhat, rstd, g)
        return [du1, _colsum(du2 * xhat), _colsum(du2), _colsum(du1)]

    du1, S['conv_ln_g'], S['conv_ln_b'], S['conv_b'] = _rows(
        "conv_ln_bwd", conv_ln_bwd, [(du2, 'r', CC, 0), (xhc, 'r', CC, 0), (rsc, 'r', 1, 0), (P['conv_ln_g'], 'v', CC, 0)],
        [('r', CC, F32), ('a', CC, F32), ('a', CC, F32), ('a', CC, F32)], T=T, tb=512)
    dglu_a, dglu_b, dkpad = _conv_bwd(du1, proj, kpad, off_glu, CC, T)
    G['conv_k'] = dkpad[:CONV_WIDTH].astype(gdt)

    for off, piece in ((off_glu, dglu_a), (off_glu + CC, dglu_b), (off_gate, dgate_r), (off_gate + D, dgate_c)):
        dproj = lax.dynamic_update_slice(dproj, piece, (0, off))
    IN_W = dproj.shape[1]
    rider = net.exchange(['conv_k'])
    G['w_in'], S['b_in'] = _mm("d_w_in", [x1b], [dproj], [[(0, 0)]], ident, [gdt], ta=True, o3=True, bsum=True,
                               tm=1024, tn=W['w_in'].shape[2], tk=1024, rider=rider)
    net.done(rider)
    cuts = [0, (3 * D) // 16, (43 * D) // 64, D]
    w_in_rows = [{'w_in': (cuts[i], cuts[i + 1] - cuts[i], i == 2)} for i in range(3)]
    rider = net.to_sibling(['w_in'])
    (dy1,) = _mm("d_x1", [dproj], [W['w_in']], [[(0, 0)]], lambda accs, dzr: [ALPHA * dzr + accs[0]], [F32], tb=True,
                 b3=True, tm=1024, tn=1024, tk=0, extras=[(dz2, 'mn', 0)], rider=rider)
    net.done(rider)
    net.pairsum(['w_in'])
    rider = net.to_owner(['w_in'], w_in_rows[0])
    dz1, dz1h, S['ln1_g'], S['ln1_b'] = _ln_bwd("ln1b", dy1, xh1, rs1, P['ln1_g'], 0.5, T, D, rider=rider)
    net.done(rider)
    grad_x = _ffn_bwd("ffn1b", dz1h, dz1, xb, a1, b1, s1, W, FFN1, gdt, G, net, 'all',
                      pre=(lambda: net.to_owner(['w_in'], w_in_rows[1]), lambda: net.to_owner(['w_in'], w_in_rows[2])))
    return loss, grad_x, G, S


def _coords():
    return lax.axis_index("x"), lax.axis_index("y"), lax.axis_index("c")


def _flip(k, x, y, c):
    return (1 - x if k & 4 else x, 1 - y if k & 2 else y, 1 - c if k & 1 else c)


def _lin(p):
    return 4 * p[0] + 2 * p[1] + p[2]


class _Rider:
    def __init__(self, ins, out_shape, rows=None, fill=None):
        nb = len(ins)
        self.rows = rows or [None] * nb
        fill = fill or [None] * nb
        self.aliases = {nb + i: w for i, w in enumerate(w for w in range(nb) if fill[w] is not None)}
        self.ins = list(ins) + [f for f in fill if f is not None]
        self.out_shape, self.results = list(out_shape), None
        self.scratch = [pltpu.SemaphoreType.DMA((8 * nb,)), pltpu.SemaphoreType.DMA((8 * nb,)),
                        pltpu.SemaphoreType.DMA((nb,))]

    def span(self, w, ref, *slot, half=None):
        rows = self.rows[w]
        if half is not None:
            first, count = rows if rows is not None else (0, self.out_shape[w].shape[1])
            rows = (first + half * (count // 2), count // 2)
        if rows is None:
            return ref.at[slot] if slot else ref
        return ref.at[(*slot, pl.ds(*rows))]

    def begin(self, step, n_steps, ins, outs, sems):
        @pl.when(step == 0)
        def _():
            self.start(ins, outs, sems)

        @pl.when(step == min(n_steps - 1, (5 * n_steps) // 8))
        def _():
            self.relay(ins, outs, sems)

        @pl.when(step == n_steps - 1)
        def _():
            self.mid(ins, outs, sems)

    def end(self, step, n_steps, ins, outs, sems):
        @pl.when(step == n_steps - 1)
        def _():
            self.finish(ins, outs, sems)

    def relay(self, ins, outs, sems):
        pass

    def mid(self, ins, outs, sems):
        pass


class _GatherRider(_Rider):
    def __init__(self, blks, rows=None, fill=None):
        super().__init__(blks, [jax.ShapeDtypeStruct((N_DEV,) + b.shape, b.dtype) for b in blks], rows, fill)
        counts = [(r[1] if r is not None else b.shape[0]) for r, b in zip(self.rows, blks)]
        self.halves = [n % 32 == 0 for n in counts]

    def _copies(self, x_refs, out_refs, sems):
        nb = len(self.out_shape)
        send_sems, recv_sems, local_sems = sems
        x, y, c = _coords()
        me, sib = (x, y, c), (x, y, 1 - c)
        xn, yn, dg = _flip(4, x, y, c), _flip(2, x, y, c), _flip(6, x, y, c)
        plans = []
        for w in range(nb):
            own = self.span(w, x_refs[w])

            def copy(k, block, to, src=None, half=None, w=w):
                slot = self.span(w, out_refs[w], _lin(block), half=half)
                return pltpu.make_async_remote_copy(
                    src_ref=slot if src is None else src, dst_ref=slot, send_sem=send_sems.at[k * nb + w],
                    recv_sem=recv_sems.at[k * nb + w], device_id=to, device_id_type=MESH)

            mine = pltpu.make_async_copy(own, self.span(w, out_refs[w], _lin(me)), local_sems.at[w])
            first = [copy(0, me, sib, src=own), copy(1, me, xn, src=own), copy(2, me, yn, src=own)]
            if self.halves[w]:
                relay = [(copy(1, xn, me), [copy(3, xn, yn, half=0), copy(5, xn, sib)]),
                         (copy(2, yn, me), [copy(4, yn, xn, half=1), copy(6, yn, sib)])]
                last = [(copy(3, dg, me, half=0), []), (copy(4, dg, me, half=1), [copy(7, dg, sib)])]
            else:
                first.append(copy(3, me, dg, src=own))
                relay = [(copy(1, xn, me), [copy(5, xn, sib)]), (copy(2, yn, me), [copy(6, yn, sib)])]
                last = [(copy(3, dg, me), [copy(7, dg, sib)])]
            other = lambda p: (p[0], p[1], 1 - c)
            from_sib = [copy(0, sib, me), copy(5, other(xn), me), copy(6, other(yn), me), copy(7, other(dg), me)]
            plans.append((mine, first, relay, last, from_sib))
        return plans

    def start(self, ins, outs, sems):
        for mine, first, _, _, _ in self._copies(ins, outs, sems):
            for cp in [mine] + first:
                cp.start()

    def relay(self, ins, outs, sems):
        for _, _, relay, _, _ in self._copies(ins, outs, sems):
            for arrival, released in relay:
                arrival.wait_recv()
                for cp in released:
                    cp.start()

    def mid(self, ins, outs, sems):
        for _, _, _, last, _ in self._copies(ins, outs, sems):
            for arrival, released in last:
                arrival.wait_recv()
                for cp in released:
                    cp.start()

    def finish(self, ins, outs, sems):
        for mine, first, relay, last, from_sib in self._copies(ins, outs, sems):
            for cp in from_sib:
                cp.wait_recv()
            for cp in first + [cp for _, released in relay + last for cp in released]:
                cp.wait_send()
            mine.wait()


class _ExchangeRider(_Rider):
    def __init__(self, gs, rows=None, fill=None):
        super().__init__(gs, [jax.ShapeDtypeStruct(g.shape, g.dtype) for g in gs], rows, fill)

    def _copies(self, g_refs, out_refs, sems):
        nb = len(self.out_shape)
        send_sems, recv_sems, local_sems = sems
        x, y, c = _coords()
        me = _lin((x, y, c))

        def copy(k, w, landing):
            peer = _flip(k, x, y, c)
            src, dst = (me, _lin(peer)) if landing else (_lin(peer), me)
            return pltpu.make_async_remote_copy(
                src_ref=self.span(w, g_refs[w], src), dst_ref=self.span(w, out_refs[w], dst),
                send_sem=send_sems.at[(k - 1) * nb + w], recv_sem=recv_sems.at[(k - 1) * nb + w],
                device_id=peer, device_id_type=MESH)

        mines = [pltpu.make_async_copy(self.span(w, g_refs[w], me), self.span(w, out_refs[w], me), local_sems.at[w])
                 for w in range(nb)]
        sends = [copy(k, w, False) for w in range(nb) for k in range(1, N_DEV)]
        landings = [copy(k, w, True) for w in range(nb) for k in range(1, N_DEV)]
        return mines, sends, landings

    def start(self, ins, outs, sems):
        mines, sends, _ = self._copies(ins, outs, sems)
        for cp in mines + sends:
            cp.start()

    def finish(self, ins, outs, sems):
        mines, sends, landings = self._copies(ins, outs, sems)
        for cp in landings:
            cp.wait_recv()
        for cp in sends:
            cp.wait_send()
        for mine in mines:
            mine.wait()


class _SiblingRider(_Rider):
    def __init__(self, gs):
        super().__init__(gs, [jax.ShapeDtypeStruct((4,) + g.shape[1:], g.dtype) for g in gs])

    def _copies(self, g_refs, out_refs, sems, landing):
        nb = len(self.out_shape)
        send_sems, recv_sems, _ = sems
        x, y, c = _coords()
        whose = c if landing else 1 - c
        return [pltpu.make_async_remote_copy(
            src_ref=g_refs[w].at[2 * q + whose], dst_ref=out_refs[w].at[q], send_sem=send_sems.at[q * nb + w],
            recv_sem=recv_sems.at[q * nb + w], device_id=(x, y, 1 - c), device_id_type=MESH)
            for w in range(nb) for q in range(4)]

    def start(self, ins, outs, sems):
        for cp in self._copies(ins, outs, sems, False):
            cp.start()

    def finish(self, ins, outs, sems):
        for cp in self._copies(ins, outs, sems, True):
            cp.wait_recv()
        for cp in self._copies(ins, outs, sems, False):
            cp.wait_send()


class _ChipRider(_Rider):
    FLIPS = (4, 2, 6)

    def __init__(self, ps, rows=None, fill=None):
        super().__init__(ps, [jax.ShapeDtypeStruct(p.shape, p.dtype) for p in ps], rows, fill)

    def _copies(self, p_refs, out_refs, sems):
        nb = len(self.out_shape)
        send_sems, recv_sems, local_sems = sems
        x, y, c = _coords()
        my_chip = 2 * x + y

        def copy(j, w, landing):
            peer = _flip(self.FLIPS[j], x, y, c)
            peer_chip = 2 * peer[0] + peer[1]
            src, dst = (my_chip, peer_chip) if landing else (peer_chip, my_chip)
            return pltpu.make_async_remote_copy(
                src_ref=self.span(w, p_refs[w], src), dst_ref=self.span(w, out_refs[w], dst),
                send_sem=send_sems.at[j * nb + w], recv_sem=recv_sems.at[j * nb + w],
                device_id=peer, device_id_type=MESH)

        mines = [pltpu.make_async_copy(self.span(w, p_refs[w], my_chip), self.span(w, out_refs[w], my_chip),
                                       local_sems.at[w]) for w in range(nb)]
        sends = [copy(j, w, False) for w in range(nb) for j in range(3)]
        landings = [copy(j, w, True) for w in range(nb) for j in range(3)]
        return mines, sends, landings

    def start(self, ins, outs, sems):
        mines, sends, _ = self._copies(ins, outs, sems)
        for cp in mines + sends:
            cp.start()

    def finish(self, ins, outs, sems):
        mines, sends, landings = self._copies(ins, outs, sems)
        for cp in landings:
            cp.wait_recv()
        for cp in sends:
            cp.wait_send()
        for mine in mines:
            mine.wait()


def _pairsum(name, g, land):
    _, r, cols = g.shape
    tb = r if r % 16 else _row_tile(r, 16, max(16, (1024 * 1024) // cols))
    core = lax.axis_index("c").astype(jnp.int32).reshape(1)

    def body(core_ref, g_ref, l_ref, o_ref):
        o_ref[...] = (g_ref[...].astype(F32) + l_ref[...].astype(F32)).astype(o_ref.dtype)

    return pl.pallas_call(
        body, name=name, out_shape=jax.ShapeDtypeStruct((4, r, cols), g.dtype),
        grid_spec=pltpu.PrefetchScalarGridSpec(
            num_scalar_prefetch=1, grid=(4, r // tb),
            in_specs=[pl.BlockSpec((None, None, tb, cols), lambda q, i, core_ref: (q, core_ref[0], i, 0)),
                      pl.BlockSpec((None, tb, cols), lambda q, i, core_ref: (q, i, 0))],
            out_specs=pl.BlockSpec((None, tb, cols), lambda q, i, core_ref: (q, i, 0))),
        compiler_params=_params(("arbitrary", "arbitrary")),
    )(core, g.reshape(4, 2, r, cols), land)


def _run_rider(name, rider):
    n_in, n_out = len(rider.ins), len(rider.out_shape)

    def body(*refs):
        ride = (refs[:n_in], refs[n_in:n_in + n_out], refs[n_in + n_out:])
        rider.start(*ride)
        rider.relay(*ride)
        rider.mid(*ride)
        rider.finish(*ride)

    rider.results = pl.pallas_call(
        body, name=name, out_shape=rider.out_shape, in_specs=[ANY] * n_in, out_specs=[ANY] * n_out,
        scratch_shapes=rider.scratch, input_output_aliases=dict(rider.aliases),
        compiler_params=pltpu.CompilerParams(has_side_effects=True),
    )(*rider.ins)
    return rider.results


def _as_matrix(name, g):
    if name == 'w_in':
        return g
    if name in COL_SHARDED:
        return jnp.transpose(g, (1, 0, 2)).reshape(g.shape[1], N_DEV * g.shape[2])
    return g.reshape(N_DEV * g.shape[1], g.shape[2])


def _by_owner(name, g):
    if name == 'w_in':
        return g
    if name in COL_SHARDED:
        return jnp.transpose(g.reshape(g.shape[0], N_DEV, g.shape[1] // N_DEV), (1, 0, 2))
    return g.reshape(N_DEV, g.shape[0] // N_DEV, g.shape[1])


class _Comm:
    def __init__(self, shards):
        self.shards, self.W, self.parts, self.partial, self.sent = shards, {}, {}, {}, {}
        self.from_sibling, self.pairs = {}, {}

    def _ride(self, cls, names, srcs, part, sink):
        part = part or {}
        rider = cls(srcs, rows=[part[n][:2] if n in part else None for n in names],
                    fill=[self.partial.pop((sink, n), None) for n in names])
        rider.names, rider.sink = names, sink
        rider.unfinished = {n for n in names if n in part and not part[n][2]}
        return rider

    def gather(self, names, part=None):
        return self._ride(_GatherRider, names, [self.shards[n] for n in names], part, 'W')

    def exchange(self, names, G, part=None):
        for n in names:
            if n not in self.sent:
                self.sent[n] = _by_owner(n, G[n])
        return self._ride(_ExchangeRider, names, [self.sent[n] for n in names], part, 'parts')

    def to_sibling(self, names, G):
        for n in names:
            self.sent[n] = _by_owner(n, G[n])
        rider = _SiblingRider([self.sent[n] for n in names])
        rider.names, rider.sink, rider.unfinished = names, 'sibling', set()
        return rider

    def pairsum(self, names):
        for n in names:
            self.pairs[n] = _pairsum("pairsum_" + n, self.sent[n], self.from_sibling.pop(n))

    def to_owner(self, names, part=None):
        return self._ride(_ChipRider, names, [self.pairs[n] for n in names], part, 'parts')

    def collect(self, rider):
        for n, res in zip(rider.names, rider.results):
            if n in rider.unfinished:
                self.partial[(rider.sink, n)] = res
            elif rider.sink == 'W':
                self.W[n] = _as_matrix(n, res)
            elif rider.sink == 'sibling':
                self.from_sibling[n] = res
            else:
                self.parts[n] = res


def _adamw_math(p_ref, w_ref, m_ref, v_ref, g_ref, d_ref, nm_ref, nv_ref):
    c1 = 1.0 - ADAM_B1 ** ADAM_STEP
    c2 = 1.0 - ADAM_B2 ** ADAM_STEP
    g = p_ref[0].astype(F32)
    for s in range(1, p_ref.shape[0]):
        g = g + p_ref[s].astype(F32)
    nm = ADAM_B1 * m_ref[...] + (1.0 - ADAM_B1) * g
    nv = ADAM_B2 * v_ref[...] + (1.0 - ADAM_B2) * (g * g)
    g_ref[...] = g
    nm_ref[...] = nm
    nv_ref[...] = nv
    d_ref[...] = -ADAM_LR * ((nm / c1) / (jnp.sqrt(nv / c2) + ADAM_EPS) + ADAM_WD * w_ref[...])


def _adamw_vectors(parts, ws, ms, vs, loss_parts):
    k = len(ws)

    def body(*refs):
        for i in range(k):
            _adamw_math(refs[i], refs[k + i], refs[2 * k + i], refs[3 * k + i], *refs[4 * k + 1 + 4 * i:4 * k + 5 + 4 * i])
        lp, lo = refs[4 * k], refs[8 * k + 1]
        lo[...] = functools.reduce(jnp.add, [lp[s] for s in range(lp.shape[0])])

    return pl.pallas_call(
        body, name="adamw_vectors",
        out_shape=[jax.ShapeDtypeStruct(w.shape, F32) for w in ws for _ in range(4)] + [jax.ShapeDtypeStruct((1, 128), F32)],
        compiler_params=_params(),
    )(*parts, *ws, *ms, *vs, loss_parts)


def _adamw(name, parts, w, m, v, tb):
    n, R, Wd = parts.shape
    assert R % tb == 0
    body = functools.partial(_adamw_math)

    row = pl.BlockSpec((tb, Wd), lambda i: (i, 0))
    return pl.pallas_call(
        body, name=name, grid=(R // tb,),
        in_specs=[pl.BlockSpec((n, tb, Wd), lambda i: (0, i, 0)), row, row, row],
        out_specs=[row, row, row, row], out_shape=[jax.ShapeDtypeStruct((R, Wd), F32)] * 4,
        compiler_params=_params(("arbitrary",)),
    )(parts, w, m, v)


def _row_tile(R, unit, cap):
    best = unit
    for t in range(unit, cap + 1, unit):
        if R % t == 0:
            best = t
    return best


def kernel(x, ffn1_w_gate, ffn1_w_up, ffn1_w_down, ln1_g, ln1_b, w_in, b_in, ret_gn_g, conv_k, conv_b, conv_ln_g, conv_ln_b, w_ret_o, w_conv_o, w_out, ln2_g, ln2_b, ffn2_w_gate, ffn2_w_up, ffn2_w_down, ln3_g, ln3_b, loss_target, m_ffn1_w_gate, m_ffn1_w_up, m_ffn1_w_down, m_ln1_g, m_ln1_b, m_w_in, m_b_in, m_ret_gn_g, m_conv_k, m_conv_b, m_conv_ln_g, m_conv_ln_b, m_w_ret_o, m_w_conv_o, m_w_out, m_ln2_g, m_ln2_b, m_ffn2_w_gate, m_ffn2_w_up, m_ffn2_w_down, m_ln3_g, m_ln3_b, v_ffn1_w_gate, v_ffn1_w_up, v_ffn1_w_down, v_ln1_g, v_ln1_b, v_w_in, v_b_in, v_ret_gn_g, v_conv_k, v_conv_b, v_conv_ln_g, v_conv_ln_b, v_w_ret_o, v_w_conv_o, v_w_out, v_ln2_g, v_ln2_b, v_ffn2_w_gate, v_ffn2_w_up, v_ffn2_w_down, v_ln3_g, v_ln3_b):
    given = dict(locals())
    wts = {n: given[n] for n in WEIGHTS}
    mom = {n: given['m_' + n] for n in WEIGHTS}
    var = {n: given['v_' + n] for n in WEIGHTS}

    def shard2d(a):
        return a.reshape(a.shape[-3] * a.shape[-2] if a.ndim == 4 else a.shape[-2], a.shape[-1])

    comm = _Comm({n: shard2d(wts[n]).astype(BF16) for n in BIG})
    P = {n: wts[n].reshape(1, -1) for n in SMALL}
    loss, grad_x, _, S = _local_step(x[0], loss_target[0], None, P, comm=comm)

    parts = comm.parts
    res = {}
    for n in BIG:
        rows, cols = parts[n].shape[1:]
        tb = rows if rows % 16 else _row_tile(rows, 16, max(16, (256 * 1024) // cols))
        res[n] = _adamw("adamw_" + n, parts[n], shard2d(wts[n]), shard2d(mom[n]), shard2d(var[n]), tb)

    vec_parts = _run_rider("gather_vector_grads", _GatherRider([S[n] for n in SMALL] + [loss]))
    vec = _adamw_vectors(vec_parts[:-1], [P[n] for n in SMALL], [mom[n].reshape(1, -1) for n in SMALL],
                         [var[n].reshape(1, -1) for n in SMALL], vec_parts[-1])
    for i, n in enumerate(SMALL):
        res[n] = vec[4 * i:4 * i + 4]

    outs = [vec[-1][0, 0], grad_x[None]]
    for k in range(4):
        for n in WEIGHTS:
            outs.append(res[n][k].reshape(wts[n].shape))
    return tuple(outs)
```

```python
import functools
import math

import jax
import jax.numpy as jnp
from jax import lax
from jax.experimental import pallas as pl
from jax.experimental.pallas import tpu as pltpu

F32 = jnp.float32
BF16 = jnp.bfloat16

N_DEV = 8
LN_EPS = 1e-5
ALPHA = 2.0 ** 0.25
RET_DK = 128
RET_DV = 256
RET_CHUNK = 256
ROPE_BASE = 10000.0
CONV_WIDTH = 31
HALO = 32
ADAM_LR, ADAM_B1, ADAM_B2, ADAM_EPS, ADAM_WD, ADAM_STEP = 0.001, 0.9, 0.999, 1e-08, 0.01, 10
VMEM_LIMIT = 52 * 1024 * 1024
MESH = pl.DeviceIdType.MESH
ANY = pl.BlockSpec(memory_space=pl.ANY)

BIG = ['ffn1_w_gate', 'ffn1_w_up', 'ffn1_w_down', 'w_in', 'w_ret_o', 'w_conv_o', 'w_out',
       'ffn2_w_gate', 'ffn2_w_up', 'ffn2_w_down', 'conv_k']
COL_SHARDED = {'ffn1_w_gate', 'ffn1_w_up', 'w_in', 'ffn2_w_gate', 'ffn2_w_up', 'conv_k'}
SMALL = ['ln1_g', 'ln1_b', 'b_in', 'ret_gn_g', 'conv_b', 'conv_ln_g', 'conv_ln_b', 'ln2_g', 'ln2_b', 'ln3_g', 'ln3_b']
WEIGHTS = ['ffn1_w_gate', 'ffn1_w_up', 'ffn1_w_down', 'ln1_g', 'ln1_b', 'w_in', 'b_in', 'ret_gn_g', 'conv_k', 'conv_b',
           'conv_ln_g', 'conv_ln_b', 'w_ret_o', 'w_conv_o', 'w_out', 'ln2_g', 'ln2_b', 'ffn2_w_gate', 'ffn2_w_up',
           'ffn2_w_down', 'ln3_g', 'ln3_b']


def _params(sem=None):
    return pltpu.CompilerParams(dimension_semantics=sem, vmem_limit_bytes=VMEM_LIMIT)


def _sigmoid(x):
    return jax.nn.sigmoid(x)


def _dsilu(x, sg):
    return sg * (1.0 + x * (1.0 - sg))


def _fit(dim, want):
    if dim <= want:
        return dim
    return max(t for t in range(128, want + 1, 128) if dim % t == 0)


def _dot(a, b, ta=False, tb=False):
    dn = (((0,) if ta else (1,), (1,) if tb else (0,)), ((), ()))
    return lax.dot_general(a, b, dn, preferred_element_type=F32)


def _mm(name, As, Bs, prods, epi, out_dtypes, *, ta=False, tb=False, tm, tn, tk, extras=(), i_outer=True,
        b3=False, o3=False, rider=None, bsum=False, epi_rows=0):
    a0, b0 = As[0], Bs[0]
    M, K = (a0.shape[1], a0.shape[0]) if ta else a0.shape
    if b3:
        S, rows, cs = b0.shape
        N = rows if tb else S * cs
        assert K == (S * cs if tb else rows)
        tn, tk = (tn, cs) if tb else (cs, tk)
    else:
        N = b0.shape[0] if tb else b0.shape[1]
    tm, tn, tk = _fit(M, tm), _fit(N, tn), _fit(K, tk)
    assert M % tm == 0 and N % tn == 0 and K % tk == 0, (name, M, N, K, tm, tn, tk)
    gi, gj, gk = M // tm, N // tn, K // tk
    grid = (gi, gj, gk) if i_outer else (gj, gi, gk)

    def ij(g0, g1):
        return (g0, g1) if i_outer else (g1, g0)

    def amap(g0, g1, k):
        i, _ = ij(g0, g1)
        return (k, i) if ta else (i, k)

    def bmap(g0, g1, k):
        _, j = ij(g0, g1)
        return (j, k) if tb else (k, j)

    def bmap3(g0, g1, k):
        _, j = ij(g0, g1)
        return (k, j, 0) if tb else (j, k, 0)

    in_specs = [pl.BlockSpec((tk, tm) if ta else (tm, tk), amap) for _ in As]
    if b3:
        in_specs += [pl.BlockSpec((None, tn, tk) if tb else (None, tk, tn), bmap3) for _ in Bs]
    else:
        in_specs += [pl.BlockSpec((tn, tk) if tb else (tk, tn), bmap) for _ in Bs]
    args = list(As) + list(Bs)
    for arr, kind, coloff in extras:
        assert coloff % tn == 0
        off = coloff // tn
        if kind == 'mn':
            in_specs.append(pl.BlockSpec((tm, tn), lambda g0, g1, k, off=off: (ij(g0, g1)[0], ij(g0, g1)[1] + off)))
        else:
            in_specs.append(pl.BlockSpec((1, tn), lambda g0, g1, k, off=off: (0, ij(g0, g1)[1] + off)))
        args.append(arr)
    if o3:
        out_shape = [jax.ShapeDtypeStruct((gj, M, tn), dt) for dt in out_dtypes]
        out_specs = [pl.BlockSpec((None, tm, tn), lambda g0, g1, k: (ij(g0, g1)[1], ij(g0, g1)[0], 0))
                     for _ in out_dtypes]
    else:
        out_shape = [jax.ShapeDtypeStruct((M, N), dt) for dt in out_dtypes]
        out_specs = [pl.BlockSpec((tm, tn), lambda g0, g1, k: ij(g0, g1)) for _ in out_dtypes]
    if bsum:
        assert gi == 1 and not tb and not b3
        out_shape.append(jax.ShapeDtypeStruct((1, N), F32))
        out_specs.append(pl.BlockSpec((1, tn), lambda g0, g1, k: (0, ij(g0, g1)[1])))
    n_a, n_b, n_e, n_o = len(As), len(Bs), len(extras), len(out_shape)
    n_p = len(prods) if gk > 1 else 0
    scratch = [pltpu.VMEM((tm, tn), F32) for _ in range(n_p)]
    if rider is not None:
        in_specs, out_specs = in_specs + [ANY] * len(rider.ins), out_specs + [ANY] * len(rider.out_shape)
        args, out_shape, scratch = args + rider.ins, out_shape + rider.out_shape, scratch + rider.scratch
    n_in, n_out = len(args), len(out_shape)

    def body(*refs):
        a_refs = refs[:n_a]
        b_refs = refs[n_a:n_a + n_b]
        e_refs = refs[n_a + n_b:n_a + n_b + n_e]
        o_refs = refs[n_in:n_in + n_o]
        acc_refs = refs[n_in + n_out:n_in + n_out + n_p]
        k = pl.program_id(2)
        if rider is not None:
            step = (pl.program_id(0) * grid[1] + pl.program_id(1)) * gk + k
            ride = (step, grid[0] * grid[1] * gk, refs[n_a + n_b + n_e:n_in], refs[n_in + n_o:n_in + n_out],
                    refs[n_in + n_out + n_p:])
            rider.begin(*ride)

        def finish(accs, rows=slice(None)):
            ex = [(e[rows, :] if kind == 'mn' else e[...]).astype(F32) for e, (_, kind, _) in zip(e_refs, extras)]
            for o, r in zip(o_refs, epi(accs, *ex)):
                o[rows, :] = r.astype(o.dtype)

        if bsum:
            @pl.when(k == 0)
            def _():
                o_refs[-1][...] = jnp.zeros_like(o_refs[-1])

            o_refs[-1][...] += _colsum(b_refs[0][...].astype(F32))

        if gk == 1:
            sub = tm if ta or not epi_rows else _fit(tm, epi_rows)
            for r0 in range(0, tm, sub):
                rows = slice(None) if ta else slice(r0, r0 + sub)
                finish([functools.reduce(jnp.add, [_dot(a_refs[ai][...] if ta else a_refs[ai][rows, :],
                                                        b_refs[bi][...], ta, tb) for ai, bi in terms])
                        for terms in prods], rows)
        else:
            @pl.when(k == 0)
            def _():
                for acc in acc_refs:
                    acc[...] = jnp.zeros_like(acc)

            for p, terms in enumerate(prods):
                for ai, bi in terms:
                    acc_refs[p][...] += _dot(a_refs[ai][...], b_refs[bi][...], ta, tb)

            @pl.when(k == gk - 1)
            def _():
                finish([acc[...] for acc in acc_refs])

        if rider is not None:
            rider.end(*ride)

    aliases = {} if rider is None else {n_a + n_b + n_e + p: n_o + o for p, o in rider.aliases.items()}
    res = pl.pallas_call(
        body, name=name, grid=grid, in_specs=in_specs, out_specs=out_specs, out_shape=out_shape,
        scratch_shapes=scratch, input_output_aliases=aliases,
        compiler_params=_params(("arbitrary", "arbitrary", "arbitrary")),
    )(*args)
    if rider is not None:
        rider.results = res[n_o:]
    return res[:n_o]


def _rows(name, fn, ins, outs, *, T, tb, rider=None):
    tb = min(tb, T)
    assert T % tb == 0
    in_specs, args = [], []
    for arr, kind, width, cb in ins:
        if kind == 'r':
            in_specs.append(pl.BlockSpec((tb, width), lambda i, _, cb=cb: (i, cb)))
        else:
            in_specs.append(pl.BlockSpec((1, width), lambda i, _, cb=cb: (0, cb)))
        args.append(arr)
    out_shape, out_specs = [], []
    for kind, width, dtype in outs:
        if kind == 'r':
            out_shape.append(jax.ShapeDtypeStruct((T, width), dtype))
            out_specs.append(pl.BlockSpec((tb, width), lambda i, _: (i, 0)))
        elif kind == 'c':
            out_shape.append(jax.ShapeDtypeStruct((T, 1), dtype))
            out_specs.append(pl.BlockSpec((tb, 1), lambda i, _: (i, 0)))
        else:
            out_shape.append(jax.ShapeDtypeStruct((1, width), F32))
            out_specs.append(pl.BlockSpec((1, width), lambda i, _: (0, 0)))
    n_in = len(ins)

    def body(*refs):
        i = pl.program_id(0)
        vals = fn(*[r[...] for r in refs[:n_in]])
        for (kind, _, _), o, v in zip(outs, refs[n_in:], vals):
            if kind == 'a':
                @pl.when(i == 0)
                def _(o=o):
                    o[...] = jnp.zeros_like(o)

                o[...] += v
            else:
                o[...] = v.astype(o.dtype)

    return _hosted_call(body, rider, name=name, grid=(T // tb, 1), in_specs=in_specs, out_specs=out_specs,
                        out_shape=out_shape, scratch=[], args=args)


def _colsum(v):
    return jnp.sum(v, axis=0, keepdims=True)


def _ln_stats(z):
    mu = jnp.mean(z, axis=-1, keepdims=True)
    d = z - mu
    var = jnp.mean(d * d, axis=-1, keepdims=True)
    rstd = lax.rsqrt(var + LN_EPS)
    return d * rstd, rstd


def _ln_bwd_math(dy, xhat, rstd, g):
    dxh = dy * g
    m1 = jnp.mean(dxh, axis=-1, keepdims=True)
    m2 = jnp.mean(dxh * xhat, axis=-1, keepdims=True)
    return rstd * (dxh - m1 - xhat * m2)


def _ln_fwd(name, z, g, b, T, D, rider=None):
    def fn(z, g, b):
        xhat, rstd = _ln_stats(z)
        y = xhat * g + b
        return [y, y, xhat, rstd]

    return _rows(name, fn, [(z, 'r', D, 0), (g, 'v', D, 0), (b, 'v', D, 0)],
                 [('r', D, F32), ('r', D, BF16), ('r', D, F32), ('c', 1, F32)], T=T, tb=512, rider=rider)


def _ln_bwd(name, dy, xhat, rstd, g, scale, T, D, rider=None):
    def fn(dy, xhat, rstd, g):
        dz = _ln_bwd_math(dy, xhat, rstd, g)
        return [dz, dz * scale, _colsum(dy * xhat), _colsum(dy)]

    return _rows(name, fn, [(dy, 'r', D, 0), (xhat, 'r', D, 0), (rstd, 'r', 1, 0), (g, 'v', D, 0)],
                 [('r', D, F32), ('r', D, BF16), ('a', D, F32), ('a', D, F32)], T=T, tb=512, rider=rider)


def _ln_loss_bwd(name, z, g, b, tgt, T, D):
    def fn(z, g, b, tgt):
        xhat, rstd = _ln_stats(z)
        err = xhat * g + b - tgt
        row_loss = 0.5 * jnp.mean(err * err, axis=-1, keepdims=True)
        loss = jnp.broadcast_to(jnp.sum(row_loss, axis=0, keepdims=True), (1, 128))
        dy = err * (1.0 / D)
        dz = _ln_bwd_math(dy, xhat, rstd, g)
        return [dz, dz * 0.5, _colsum(dy * xhat), _colsum(dy), loss]

    return _rows(name, fn, [(z, 'r', D, 0), (g, 'v', D, 0), (b, 'v', D, 0), (tgt, 'r', D, 0)],
                 [('r', D, F32), ('r', D, BF16), ('a', D, F32), ('a', D, F32), ('a', 128, F32)], T=T, tb=512)


class _Net:
    def __init__(self, comm, G):
        self.comm, self.G = comm, G

    def gather(self, names, part=None):
        return self.comm.gather(names, part) if self.comm else None

    def exchange(self, names, part=None):
        return self.comm.exchange(names, self.G, part) if self.comm else None

    def to_sibling(self, names):
        return self.comm.to_sibling(names, self.G) if self.comm else None

    def pairsum(self, names):
        if self.comm:
            self.comm.pairsum(names)

    def to_owner(self, names, part=None):
        return self.comm.to_owner(names, part) if self.comm else None

    def done(self, rider):
        if rider is not None:
            for one in getattr(rider, 'riders', [rider]):
                self.comm.collect(one)

    def both(self, *riders):
        return _Riders(riders) if self.comm else None


def _ffn_fwd(tag, xb, x, W, names, net, rider=None, rider_down=None):
    def epi_gu(accs):
        a, b = accs
        return [a, b, a * _sigmoid(a) * b]

    ng, nu, nd = names
    a, b, s = _mm(tag + "_gate_up", [xb], [W[ng], W[nu]], [[(0, 0)], [(0, 1)]], epi_gu, [BF16, BF16, BF16],
                  tm=1024, tn=1408, tk=1024, rider=rider, epi_rows=256)
    net.done(rider)

    def epi_down(accs, xres):
        return [ALPHA * xres + 0.5 * accs[0]]

    rider_down = rider_down() if rider_down else None
    (z,) = _mm(tag + "_down", [s], [W[nd]], [[(0, 0)]], epi_down, [F32], tm=1024, tn=1024, tk=1408,
               extras=[(x, 'mn', 0)], rider=rider_down)
    net.done(rider_down)
    return a, b, s, z


def _ffn_bwd(tag, dzh, dz, xb, a, b, s, W, names, gdt, G, net, ride, pre=(None, None)):
    ng, nu, nd = names

    def epi_ds(accs, a, b):
        ds = accs[0]
        sg = _sigmoid(a)
        return [ds * b * _dsilu(a, sg), ds * a * sg]

    rider = pre[0]() if pre[0] else None
    da, db = _mm(tag + "_ds", [dzh], [W[nd]], [[(0, 0)]], epi_ds, [BF16, BF16], tb=True, tm=1024, tn=1408, tk=1024, epi_rows=256,
                 extras=[(a, 'mn', 0), (b, 'mn', 0)], rider=rider)
    net.done(rider)
    ident = lambda accs: accs
    rider = pre[1]() if pre[1] else None
    (G[nd],) = _mm(tag + "_dwd", [s], [dzh], [[(0, 0)]], ident, [gdt], ta=True, tm=1408, tn=1024, tk=1024,
                   rider=rider)
    net.done(rider)
    if ride == 'all':
        rider = net.to_sibling([nd])
        (G[ng],) = _mm(tag + "_dwg", [xb], [da], [[(0, 0)]], ident, [gdt], ta=True, tm=1024, tn=1408, tk=1024,
                       rider=rider)
        net.done(rider)
        net.pairsum([nd])
        rider = net.to_owner([nd])
        (G[nu],) = _mm(tag + "_dwu", [xb], [db], [[(0, 0)]], ident, [gdt], ta=True, tm=1024, tn=1408, tk=1024,
                       rider=rider)
        net.done(rider)
        rider = net.to_sibling([ng, nu])
        if rider is not None:
            _run_rider(tag + "_to_sibling", rider)
            net.done(rider)
        net.pairsum([ng, nu])
        rider = net.to_owner([ng, nu])
    else:
        rider = net.to_sibling([nd]) if ride else None
        G[ng], G[nu] = _mm(tag + "_dwgu", [xb], [da, db], [[(0, 0)], [(0, 1)]], ident, [gdt, gdt], ta=True,
                           tm=1024, tn=1408, tk=1024, rider=rider)
        net.done(rider)
        if ride:
            net.pairsum([nd])
        rider = net.to_owner([nd]) if ride else None

    def epi_dx(accs, dzres):
        return [ALPHA * dzres + accs[0]]

    (dx,) = _mm(tag + "_dx", [da, db], [W[ng], W[nu]], [[(0, 0), (1, 1)]], epi_dx, [F32], tb=True,
                tm=1024, tn=1024, tk=1408, extras=[(dz, 'mn', 0)], rider=rider)
    net.done(rider)
    return dx


def _ret_tables(H, T):
    C = RET_CHUNK
    log_g = jnp.log(1.0 - jnp.exp2(-5.0 - jnp.arange(H, dtype=F32)))
    idx = jnp.arange(C, dtype=F32)
    diff = idx[:, None] - idx[None, :]
    dm = jnp.where(diff[None] >= 0, jnp.exp(jnp.maximum(diff, 0.0)[None] * log_g[:, None, None]), 0.0)
    xi = jnp.exp((idx[None, :] + 1.0) * log_g[:, None])[:, :, None]
    zeta = jnp.exp((C - 1.0 - idx)[None, :] * log_g[:, None])[:, :, None]
    gc = jnp.broadcast_to(jnp.exp(C * log_g)[:, None, None], (H, 1, RET_DV))
    half = RET_DK // 2
    freqs = ROPE_BASE ** (-jnp.arange(half, dtype=F32) / half)
    ang = jnp.arange(T, dtype=F32)[:, None] * freqs[None, :]
    cos, sin = jnp.cos(ang), jnp.sin(ang)
    cosf = jnp.concatenate([cos, cos], axis=1)
    sins = jnp.concatenate([-sin, sin], axis=1)
    return dm, xi, zeta, gc, cosf, sins


def _rot(x, cosf, sins):
    return x * cosf + pltpu.roll(x, RET_DK // 2, 1) * sins


def _rot_bwd(dy, cosf, sins):
    return dy * cosf + pltpu.roll(dy * sins, RET_DK // 2, 1)


RET_HB = 8


def _ret_specs(H, HB, rev, NC):
    C, G = RET_CHUNK, H // HB
    nn = (lambda n: NC - 1 - n) if rev else (lambda n: n)
    return [
        pl.BlockSpec((C, HB * RET_DK), lambda h, n: (nn(n), h)),
        pl.BlockSpec((C, HB * RET_DK), lambda h, n: (nn(n), G + h)),
        pl.BlockSpec((C, HB * RET_DV), lambda h, n: (nn(n), G + h)),
        pl.BlockSpec((C, HB * RET_DV), lambda h, n: (nn(n), 2 * G + h)),
        pl.BlockSpec((C, RET_DK), lambda h, n: (nn(n), 0)),
        pl.BlockSpec((C, RET_DK), lambda h, n: (nn(n), 0)),
        pl.BlockSpec((1, HB * RET_DV), lambda h, n: (0, h)),
        pl.BlockSpec((HB, C, C), lambda h, n: (h, 0, 0)),
        pl.BlockSpec((HB, C, 1), lambda h, n: (h, 0, 0)),
        pl.BlockSpec((HB, C, 1), lambda h, n: (h, 0, 0)),
        pl.BlockSpec((HB, 1, RET_DV), lambda h, n: (h, 0, 0)),
    ]


def _ret_fwd(proj, gn_g, tabs, H, T, rider=None):
    C, NC = RET_CHUNK, T // RET_CHUNK
    HB = min(RET_HB, H)
    dm, xi, zeta, gc, cosf, sins = tabs
    scale = RET_DK ** -0.5

    def body(q_ref, k_ref, v_ref, g_ref, cos_ref, sin_ref, gn_ref, dm_ref, xi_ref, zt_ref, gc_ref,
             r_ref, ri_ref, st_ref, state):
        @pl.when(pl.program_id(1) == 0)
        def _():
            state[...] = jnp.zeros_like(state)

        cs, sn = cos_ref[...], sin_ref[...]
        hs = range(HB)
        qk = [slice(h * RET_DK, (h + 1) * RET_DK) for h in hs]
        vv = [slice(h * RET_DV, (h + 1) * RET_DV) for h in hs]
        kr = [_rot(k_ref[:, qk[h]].astype(F32), cs, sn) for h in hs]
        qb = [(_rot(q_ref[:, qk[h]].astype(F32), cs, sn) * scale).astype(BF16) for h in hs]
        kb = [kr[h].astype(BF16) for h in hs]
        kzb = [(kr[h] * zt_ref[h]).astype(BF16) for h in hs]
        vb = [v_ref[:, vv[h]].astype(BF16) for h in hs]
        st = [state[h] for h in hs]
        stb = [st[h].astype(BF16) for h in hs]
        sb = [(_dot(qb[h], kb[h], tb=True) * dm_ref[h]).astype(BF16) for h in hs]
        cross = [_dot(qb[h], stb[h]) for h in hs]
        kv = [_dot(kzb[h], vb[h], ta=True) for h in hs]
        intra = [_dot(sb[h], vb[h]) for h in hs]
        for h in hs:
            st_ref[h] = stb[h]
            state[h] = gc_ref[h] * st[h] + kv[h]
        for h in hs:
            r = intra[h] + cross[h] * xi_ref[h]
            rhat, _ = _ln_stats(r)
            g = g_ref[:, vv[h]].astype(F32)
            r_ref[:, vv[h]] = r
            ri_ref[:, vv[h]] = (g * _sigmoid(g) * (rhat * gn_ref[:, vv[h]])).astype(BF16)

    VW = H * RET_DV
    return _hosted_call(
        body, rider, name="ret_fwd", grid=(H // HB, NC), in_specs=_ret_specs(H, HB, False, NC),
        out_specs=[pl.BlockSpec((C, HB * RET_DV), lambda h, n: (n, h)),
                   pl.BlockSpec((C, HB * RET_DV), lambda h, n: (n, h)),
                   pl.BlockSpec((HB, None, RET_DK, RET_DV), lambda h, n: (h, n, 0, 0))],
        out_shape=[jax.ShapeDtypeStruct((T, VW), F32), jax.ShapeDtypeStruct((T, VW), BF16),
                   jax.ShapeDtypeStruct((H, NC, RET_DK, RET_DV), BF16)],
        scratch=[pltpu.VMEM((HB, RET_DK, RET_DV), F32)],
        args=[proj, proj, proj, proj, cosf, sins, gn_g, dm, xi, zeta, gc])


def _hosted_call(body, rider, *, name, grid, in_specs, out_specs, out_shape, scratch, args):
    n_in, n_out, n_scr = len(args), len(out_shape), len(scratch)
    if rider is None:
        hosted = body
    else:
        n_ri, n_ro = len(rider.ins), len(rider.out_shape)
        in_specs, out_specs = in_specs + [ANY] * n_ri, out_specs + [ANY] * n_ro
        args, out_shape, scratch = args + rider.ins, out_shape + rider.out_shape, scratch + rider.scratch

        def hosted(*refs):
            o0, s0 = n_in + n_ri, n_in + n_ri + n_out + n_ro
            step = pl.program_id(0) * grid[1] + pl.program_id(1)
            ride = (step, grid[0] * grid[1], refs[n_in:o0], refs[o0 + n_out:s0], refs[s0 + n_scr:])
            rider.begin(*ride)
            body(*refs[:n_in], *refs[o0:o0 + n_out], *refs[s0:s0 + n_scr])
            rider.end(*ride)

    aliases = {} if rider is None else {n_in + p: n_out + o for p, o in rider.aliases.items()}
    res = pl.pallas_call(
        hosted, name=name, grid=grid, in_specs=in_specs, out_specs=out_specs, out_shape=out_shape,
        scratch_shapes=scratch, input_output_aliases=aliases, compiler_params=_params(("arbitrary", "arbitrary")),
    )(*args)
    if rider is not None:
        rider.results = res[n_out:]
    return res[:n_out]


def _ret_bwd(dri, r, states, proj, gn_g, tabs, H, T, in_w, rider=None):
    C, NC = RET_CHUNK, T // RET_CHUNK
    HB = min(RET_HB, H)
    dm, xi, zeta, gc, cosf, sins = tabs
    scale = RET_DK ** -0.5

    def body(q_ref, k_ref, v_ref, g_ref, cos_ref, sin_ref, gn_ref, dm_ref, xi_ref, zt_ref, gc_ref,
             dri_ref, r_ref, st_ref, dp_ref, dgn_ref, dstate):
        @pl.when(pl.program_id(1) == 0)
        def _():
            dstate[...] = jnp.zeros_like(dstate)
            dgn_ref[...] = jnp.zeros_like(dgn_ref)

        cs, sn = cos_ref[...], sin_ref[...]
        hs = range(HB)
        qk = [slice(h * RET_DK, (h + 1) * RET_DK) for h in hs]
        vv = [slice(h * RET_DV, (h + 1) * RET_DV) for h in hs]
        qr = [_rot(q_ref[:, qk[h]].astype(F32), cs, sn) * scale for h in hs]
        kr = [_rot(k_ref[:, qk[h]].astype(F32), cs, sn) for h in hs]
        qb = [qr[h].astype(BF16) for h in hs]
        kb = [kr[h].astype(BF16) for h in hs]
        vb = [v_ref[:, vv[h]].astype(BF16) for h in hs]
        qxb = [(qr[h] * xi_ref[h]).astype(BF16) for h in hs]
        kzb = [(kr[h] * zt_ref[h]).astype(BF16) for h in hs]
        drb = []
        for h in hs:
            rhat, rstd = _ln_stats(r_ref[:, vv[h]])
            g, gn, dpre = g_ref[:, vv[h]].astype(F32), gn_ref[:, vv[h]], dri_ref[:, vv[h]]
            sg = _sigmoid(g)
            dp_ref[:, 2 * QW + VW + h * RET_DV:2 * QW + VW + (h + 1) * RET_DV] = (
                dpre * (rhat * gn) * _dsilu(g, sg)).astype(BF16)
            drn = dpre * (g * sg)
            dgn_ref[:, vv[h]] += _colsum(drn * rhat)
            drb.append(_ln_bwd_math(drn, rhat, rstd, gn).astype(BF16))
        ds1 = [dstate[h] for h in hs]
        ds1b = [ds1[h].astype(BF16) for h in hs]
        sb = [(_dot(qb[h], kb[h], tb=True) * dm_ref[h]).astype(BF16) for h in hs]
        dsb = [(_dot(drb[h], vb[h], tb=True) * dm_ref[h]).astype(BF16) for h in hs]
        dq_x = [_dot(drb[h], st_ref[h], tb=True) for h in hs]
        dk_x = [_dot(vb[h], ds1b[h], tb=True) for h in hs]
        dv_x = [_dot(kzb[h], ds1b[h]) for h in hs]
        dst = [_dot(qxb[h], drb[h], ta=True) for h in hs]
        for h in hs:
            dstate[h] = gc_ref[h] * ds1[h] + dst[h]
        dv_i = [_dot(sb[h], drb[h], ta=True) for h in hs]
        dq_i = [_dot(dsb[h], kb[h]) for h in hs]
        dk_i = [_dot(dsb[h], qb[h], ta=True) for h in hs]
        for h in hs:
            dp_ref[:, 2 * QW + h * RET_DV:2 * QW + (h + 1) * RET_DV] = (dv_i[h] + dv_x[h]).astype(BF16)
            dq = dq_i[h] + dq_x[h] * xi_ref[h]
            dk = dk_i[h] + dk_x[h] * zt_ref[h]
            dp_ref[:, qk[h]] = _rot_bwd(dq * scale, cs, sn).astype(BF16)
            dp_ref[:, QW + h * RET_DK:QW + (h + 1) * RET_DK] = _rot_bwd(dk, cs, sn).astype(BF16)

    VW, QW = H * RET_DV, H * RET_DK
    rv = lambda n: NC - 1 - n
    in_specs = _ret_specs(H, HB, True, NC) + [
        pl.BlockSpec((C, HB * RET_DV), lambda h, n: (rv(n), h)),
        pl.BlockSpec((C, HB * RET_DV), lambda h, n: (rv(n), h)),
        pl.BlockSpec((HB, None, RET_DK, RET_DV), lambda h, n: (h, rv(n), 0, 0)),
    ]
    assert HB == H
    return _hosted_call(
        body, rider, name="ret_bwd", grid=(1, NC), in_specs=in_specs,
        out_specs=[pl.BlockSpec((C, 2 * QW + 2 * VW), lambda h, n: (rv(n), 0)),
                   pl.BlockSpec((1, VW), lambda h, n: (0, 0))],
        out_shape=[jax.ShapeDtypeStruct((T, in_w), BF16), jax.ShapeDtypeStruct((1, VW), F32)],
        scratch=[pltpu.VMEM((HB, RET_DK, RET_DV), F32)],
        args=[proj, proj, proj, proj, cosf, sins, gn_g, dm, xi, zeta, gc, dri, r, states])


CONV_CW = 128
CONV_TB = 1024


def _conv_fwd(proj, kpad, bias, off_a, CC, T, rider=None):
    tb, cw = min(CONV_TB, T), CONV_CW
    hb = tb // HALO
    ca, cb = off_a // cw, (off_a + CC) // cw

    def body(a_ref, b_ref, ap_ref, bp_ref, k_ref, bias_ref, u1_ref, win):
        i = pl.program_id(0)
        keep = (i > 0).astype(F32)
        win[0:HALO, :] = ap_ref[...].astype(F32) * _sigmoid(bp_ref[...].astype(F32)) * keep
        win[HALO:, :] = a_ref[...].astype(F32) * _sigmoid(b_ref[...].astype(F32))
        acc = jnp.broadcast_to(bias_ref[...], (tb, cw))
        for w in range(CONV_WIDTH):
            acc = acc + k_ref[w:w + 1, :] * win[pl.ds(HALO - (CONV_WIDTH - 1) + w, tb), :]
        u1_ref[...] = acc

    prev = lambda i: jnp.maximum(i * hb - 1, 0)
    (u1,) = _hosted_call(
        body, rider, name="conv_fwd", grid=(T // tb, CC // cw),
        in_specs=[pl.BlockSpec((tb, cw), lambda i, c: (i, ca + c)),
                  pl.BlockSpec((tb, cw), lambda i, c: (i, cb + c)),
                  pl.BlockSpec((HALO, cw), lambda i, c: (prev(i), ca + c)),
                  pl.BlockSpec((HALO, cw), lambda i, c: (prev(i), cb + c)),
                  pl.BlockSpec((HALO, cw), lambda i, c: (0, c)),
                  pl.BlockSpec((1, cw), lambda i, c: (0, c))],
        out_specs=[pl.BlockSpec((tb, cw), lambda i, c: (i, c))],
        out_shape=[jax.ShapeDtypeStruct((T, CC), F32)],
        scratch=[pltpu.VMEM((tb + HALO, cw), F32)],
        args=[proj, proj, proj, proj, kpad, bias])
    return u1


def _conv_bwd(du1, proj, kpad, off_a, CC, T, rider=None):
    tb, cw = min(CONV_TB, T), CONV_CW
    hb = tb // HALO
    nt = T // tb
    ca, cb = off_a // cw, (off_a + CC) // cw

    def body(d_ref, dn_ref, a_ref, b_ref, ap_ref, bp_ref, k_ref, da_ref, db_ref, dk_ref, winu, wind):
        i = pl.program_id(1)
        a, b = a_ref[...].astype(F32), b_ref[...].astype(F32)
        sgb = _sigmoid(b)
        winu[0:HALO, :] = ap_ref[...].astype(F32) * _sigmoid(bp_ref[...].astype(F32)) * (i > 0).astype(F32)
        winu[HALO:, :] = a * sgb
        d = d_ref[...]
        wind[0:tb, :] = d
        wind[tb:, :] = dn_ref[...] * (i < nt - 1).astype(F32)

        @pl.when(i == 0)
        def _():
            dk_ref[...] = jnp.zeros_like(dk_ref)

        du0 = jnp.zeros((tb, cw), F32)
        for w in range(CONV_WIDTH):
            du0 = du0 + k_ref[w:w + 1, :] * wind[pl.ds(CONV_WIDTH - 1 - w, tb), :]
            dk_ref[w:w + 1, :] += _colsum(winu[pl.ds(HALO - (CONV_WIDTH - 1) + w, tb), :] * d)
        da_ref[...] = (du0 * sgb).astype(BF16)
        db_ref[...] = (du0 * a * sgb * (1.0 - sgb)).astype(BF16)

    prev = lambda i: jnp.maximum(i * hb - 1, 0)
    nxt = lambda i: jnp.minimum((i + 1) * hb, T // HALO - 1)
    return _hosted_call(
        body, rider, name="conv_bwd", grid=(CC // cw, nt),
        in_specs=[pl.BlockSpec((tb, cw), lambda c, i: (i, c)),
                  pl.BlockSpec((HALO, cw), lambda c, i: (nxt(i), c)),
                  pl.BlockSpec((tb, cw), lambda c, i: (i, ca + c)),
                  pl.BlockSpec((tb, cw), lambda c, i: (i, cb + c)),
                  pl.BlockSpec((HALO, cw), lambda c, i: (prev(i), ca + c)),
                  pl.BlockSpec((HALO, cw), lambda c, i: (prev(i), cb + c)),
                  pl.BlockSpec((HALO, cw), lambda c, i: (0, c))],
        out_specs=[pl.BlockSpec((tb, cw), lambda c, i: (i, c)),
                   pl.BlockSpec((tb, cw), lambda c, i: (i, c)),
                   pl.BlockSpec((HALO, cw), lambda c, i: (0, c))],
        out_shape=[jax.ShapeDtypeStruct((T, CC), BF16), jax.ShapeDtypeStruct((T, CC), BF16),
                   jax.ShapeDtypeStruct((HALO, CC), F32)],
        scratch=[pltpu.VMEM((tb + HALO, cw), F32), pltpu.VMEM((tb + HALO, cw), F32)],
        args=[du1, du1, proj, proj, proj, proj, kpad])


FFN1 = ('ffn1_w_gate', 'ffn1_w_up', 'ffn1_w_down')
FFN2 = ('ffn2_w_gate', 'ffn2_w_up', 'ffn2_w_down')


def _local_step(x, tgt, W, P, gdt=BF16, comm=None):
    T, D = x.shape
    G = {}
    net = _Net(comm, G)
    if comm is not None:
        W = comm.W
        first = net.gather(['ffn1_w_gate', 'ffn1_w_up'])
    (xb,) = _rows("x_to_bf16", lambda v: [v], [(x, 'r', D, 0)], [('r', D, BF16)], T=T, tb=512,
                  rider=first if comm is not None else None)
    if comm is not None:
        net.done(first)
    VW = P['ret_gn_g'].shape[1]
    H = VW // RET_DV
    QW = H * RET_DK
    CC = P['conv_b'].shape[1]
    off_glu = 2 * QW + 2 * VW
    off_gate = off_glu + 2 * CC
    ident = lambda accs: accs

    a1, b1, s1, z1 = _ffn_fwd("ffn1", xb, x, W, FFN1, net,
                              rider=net.gather(['ffn1_w_down', 'w_in'], {'w_in': (0, D // 4, False)}),
                              rider_down=lambda: net.gather(['w_in'], {'w_in': (D // 4, (3 * D) // 8, False)}))
    rider = net.gather(['w_in'], {'w_in': ((5 * D) // 8, (3 * D) // 8, True)})
    x1, x1b, xh1, rs1 = _ln_fwd("ln1", z1, P['ln1_g'], P['ln1_b'], T, D, rider=rider)
    net.done(rider)

    rest = net.gather(['conv_k', 'w_ret_o', 'w_conv_o', 'w_out'])
    (proj,) = _mm("w_in", [x1b], [W['w_in']], [[(0, 0)]], lambda accs, bias: [accs[0] + bias], [F32],
                  tm=2048, tn=0, tk=1024, extras=[(P['b_in'], 'n', 0)], i_outer=False, b3=True, rider=rest)
    net.done(rest)
    tabs = _ret_tables(H, T)
    rider = net.gather(['ffn2_w_gate', 'ffn2_w_up'])
    r, ret_in, states = _ret_fwd(proj, P['ret_gn_g'], tabs, H, T, rider=rider)
    net.done(rider)
    kpad = jnp.pad(W['conv_k'].astype(F32), ((0, HALO - CONV_WIDTH), (0, 0)))
    u1 = _conv_fwd(proj, kpad, P['conv_b'], off_glu, CC, T)

    def conv_ln(u1, g, b):
        xhat, rstd = _ln_stats(u1)
        u2 = xhat * g + b
        return [xhat, rstd, u2 * _sigmoid(u2)]

    xhc, rsc, u3 = _rows("conv_ln", conv_ln, [(u1, 'r', CC, 0), (P['conv_ln_g'], 'v', CC, 0), (P['conv_ln_b'], 'v', CC, 0)],
                         [('r', CC, F32), ('c', 1, F32), ('r', CC, BF16)], T=T, tb=512)
    rider = net.gather(['ffn2_w_down'])
    (ret_out,) = _mm("ret_o", [ret_in], [W['w_ret_o']], [[(0, 0)]], ident, [F32], tm=1024, tn=1024, tk=2048,
                     rider=rider)
    net.done(rider)

    def epi_merge(accs, ret_out, gr, gc):
        conv_out = accs[0]
        return [conv_out, _sigmoid(gr) * ret_out + _sigmoid(gc) * conv_out]

    conv_out, merged = _mm("conv_o_merge", [u3], [W['w_conv_o']], [[(0, 0)]], epi_merge, [F32, BF16],
                           tm=512, tn=D, tk=1024, epi_rows=256,
                           extras=[(ret_out, 'mn', 0), (proj, 'mn', off_gate), (proj, 'mn', off_gate + D)])
    (z2,) = _mm("w_out", [merged], [W['w_out']], [[(0, 0)]], lambda accs, xr: [ALPHA * xr + accs[0]], [F32],
                tm=1024, tn=1024, tk=1024, extras=[(x1, 'mn', 0)])
    x2, x2b, xh2, rs2 = _ln_fwd("ln2", z2, P['ln2_g'], P['ln2_b'], T, D)
    a2, b2, s2, z3 = _ffn_fwd("ffn2", x2b, x2, W, FFN2, net)
    dz3, dz3h, g_ln3_g, g_ln3_b, loss = _ln_loss_bwd("ln3_loss", z3, P['ln3_g'], P['ln3_b'], tgt, T, D)

    S = {'ln3_g': g_ln3_g, 'ln3_b': g_ln3_b}
    dy2 = _ffn_bwd("ffn2b", dz3h, dz3, x2b, a2, b2, s2, W, FFN2, gdt, G, net, 'down')
    rider = net.to_sibling(['ffn2_w_gate', 'ffn2_w_up'])
    dz2, dz2b, S['ln2_g'], S['ln2_b'] = _ln_bwd("ln2b", dy2, xh2, rs2, P['ln2_g'], 1.0, T, D, rider=rider)
    net.done(rider)
    net.pairsum(['ffn2_w_gate', 'ffn2_w_up'])

    (G['w_out'],) = _mm("d_w_out", [merged], [dz2b], [[(0, 0)]], ident, [gdt], ta=True, tm=1024, tn=1024, tk=1024)

    def epi_dmerge(accs, ret_out, conv_out, gr, gc):
        dm_ = accs[0]
        sr, sc = _sigmoid(gr), _sigmoid(gc)
        return [dm_ * sr, dm_ * sc, dm_ * ret_out * sr * (1.0 - sr), dm_ * conv_out * sc * (1.0 - sc)]

    rider = net.to_sibling(['w_out'])
    dret_out, dconv_out, dgate_r, dgate_c = _mm(
        "d_merge", [dz2b], [W['w_out']], [[(0, 0)]], epi_dmerge, [BF16, BF16, BF16, BF16], tb=True,
        tm=512, tn=D, tk=1024, epi_rows=256, rider=rider,
        extras=[(ret_out, 'mn', 0), (conv_out, 'mn', 0), (proj, 'mn', off_gate), (proj, 'mn', off_gate + D)])
    net.done(rider)
    (G['w_ret_o'],) = _mm("d_w_ret_o", [ret_in], [dret_out], [[(0, 0)]], ident, [gdt], ta=True, tm=1024, tn=1024, tk=1024)
    (G['w_conv_o'],) = _mm("d_w_conv_o", [u3], [dconv_out], [[(0, 0)]], ident, [gdt], ta=True, tm=1024, tn=1024, tk=1024)
    rider = net.to_sibling(['w_ret_o', 'w_conv_o'])
    (dri,) = _mm("d_ret_in", [dret_out], [W['w_ret_o']], [[(0, 0)]], ident, [F32], tb=True, tm=1024, tn=1024, tk=1024,
                 rider=rider)
    net.done(rider)
    net.pairsum(['w_out', 'w_ret_o', 'w_conv_o'])
    rider = net.to_owner(['ffn2_w_gate', 'ffn2_w_up'])
    dproj, S['ret_gn_g'] = _ret_bwd(dri, r, states, proj, P['ret_gn_g'], tabs, H, T, proj.shape[1], rider=rider)
    net.done(rider)

    def epi_du2(accs, xhat, g, b):
        u2 = xhat * g + b
        return [accs[0] * _dsilu(u2, _sigmoid(u2))]

    (du2,) = _mm("d_u3", [dconv_out], [W['w_conv_o']], [[(0, 0)]], epi_du2, [F32], tb=True, tm=512, tn=CC, tk=1024, epi_rows=256,
                 extras=[(xhc, 'mn', 0), (P['conv_ln_g'], 'n', 0), (P['conv_ln_b'], 'n', 0)])

    def conv_ln_bwd(du2, xhat, rstd, g):
        du1 = _ln_bwd_math(du2, xhat, rstd, g)
        return [du1, _colsum(du2 * xhat), _colsum(du2), _colsum(du1)]

    du1, S['conv_ln_g'], S['conv_ln_b'], S['conv_b'] = _rows(
        "conv_ln_bwd", conv_ln_bwd, [(du2, 'r', CC, 0), (xhc, 'r', CC, 0), (rsc, 'r', 1, 0), (P['conv_ln_g'], 'v', CC, 0)],
        [('r', CC, F32), ('a', CC, F32), ('a', CC, F32), ('a', CC, F32)], T=T, tb=512)
    dglu_a, dglu_b, dkpad = _conv_bwd(du1, proj, kpad, off_glu, CC, T)
    G['conv_k'] = dkpad[:CONV_WIDTH].astype(gdt)

    for off, piece in ((off_glu, dglu_a), (off_glu + CC, dglu_b), (off_gate, dgate_r), (off_gate + D, dgate_c)):
        dproj = lax.dynamic_update_slice(dproj, piece, (0, off))
    IN_W = dproj.shape[1]
    rider = net.both(net.exchange(['conv_k']),
                     net.to_owner(['w_out', 'w_ret_o', 'w_conv_o']))
    G['w_in'], S['b_in'] = _mm("d_w_in", [x1b], [dproj], [[(0, 0)]], ident, [gdt], ta=True, o3=True, bsum=True,
                               tm=1024, tn=W['w_in'].shape[2], tk=1024, rider=rider)
    net.done(rider)
    cuts = [0, (3 * D) // 16, (43 * D) // 64, D]
    w_in_rows = [{'w_in': (cuts[i], cuts[i + 1] - cuts[i], i == 2)} for i in range(3)]
    rider = net.to_sibling(['w_in'])
    (dy1,) = _mm("d_x1", [dproj], [W['w_in']], [[(0, 0)]], lambda accs, dzr: [ALPHA * dzr + accs[0]], [F32], tb=True,
                 b3=True, tm=1024, tn=1024, tk=0, extras=[(dz2, 'mn', 0)], rider=rider)
    net.done(rider)
    net.pairsum(['w_in'])
    rider = net.to_owner(['w_in'], w_in_rows[0])
    dz1, dz1h, S['ln1_g'], S['ln1_b'] = _ln_bwd("ln1b", dy1, xh1, rs1, P['ln1_g'], 0.5, T, D, rider=rider)
    net.done(rider)
    grad_x = _ffn_bwd("ffn1b", dz1h, dz1, xb, a1, b1, s1, W, FFN1, gdt, G, net, 'all',
                      pre=(lambda: net.to_owner(['w_in'], w_in_rows[1]), lambda: net.to_owner(['w_in'], w_in_rows[2])))
    return loss, grad_x, G, S


def _coords():
    return lax.axis_index("x"), lax.axis_index("y"), lax.axis_index("c")


def _flip(k, x, y, c):
    return (1 - x if k & 4 else x, 1 - y if k & 2 else y, 1 - c if k & 1 else c)


def _lin(p):
    return 4 * p[0] + 2 * p[1] + p[2]


class _Rider:
    def __init__(self, ins, out_shape, rows=None, fill=None):
        nb = len(ins)
        self.rows = rows or [None] * nb
        fill = fill or [None] * nb
        self.aliases = {nb + i: w for i, w in enumerate(w for w in range(nb) if fill[w] is not None)}
        self.ins = list(ins) + [f for f in fill if f is not None]
        self.out_shape, self.results = list(out_shape), None
        self.scratch = [pltpu.SemaphoreType.DMA((8 * nb,)), pltpu.SemaphoreType.DMA((8 * nb,)),
                        pltpu.SemaphoreType.DMA((nb,))]

    def span(self, w, ref, *slot, half=None):
        rows = self.rows[w]
        if half is not None:
            first, count = rows if rows is not None else (0, self.out_shape[w].shape[1])
            rows = (first + half * (count // 2), count // 2)
        if rows is None:
            return ref.at[slot] if slot else ref
        return ref.at[(*slot, pl.ds(*rows))]

    def begin(self, step, n_steps, ins, outs, sems):
        @pl.when(step == 0)
        def _():
            self.start(ins, outs, sems)

        @pl.when(step == min(n_steps - 1, (5 * n_steps) // 8))
        def _():
            self.relay(ins, outs, sems)

        @pl.when(step == n_steps - 1)
        def _():
            self.mid(ins, outs, sems)

    def end(self, step, n_steps, ins, outs, sems):
        @pl.when(step == n_steps - 1)
        def _():
            self.finish(ins, outs, sems)

    def relay(self, ins, outs, sems):
        pass

    def mid(self, ins, outs, sems):
        pass


class _Riders:
    def __init__(self, riders):
        self.riders = list(riders)
        self.ins = [a for r in self.riders for a in r.ins]
        self.out_shape = [o for r in self.riders for o in r.out_shape]
        self.scratch = [c for r in self.riders for c in r.scratch]
        self.aliases, n_in, n_out = {}, 0, 0
        for r in self.riders:
            self.aliases.update({n_in + p: n_out + o for p, o in r.aliases.items()})
            n_in, n_out = n_in + len(r.ins), n_out + len(r.out_shape)

    def _each(self, ins, outs, sems):
        i = o = c = 0
        for r in self.riders:
            yield r, ins[i:i + len(r.ins)], outs[o:o + len(r.out_shape)], sems[c:c + len(r.scratch)]
            i, o, c = i + len(r.ins), o + len(r.out_shape), c + len(r.scratch)

    def begin(self, step, n_steps, ins, outs, sems):
        for r, i, o, c in self._each(ins, outs, sems):
            r.begin(step, n_steps, i, o, c)

    def end(self, step, n_steps, ins, outs, sems):
        for r, i, o, c in self._each(ins, outs, sems):
            r.end(step, n_steps, i, o, c)

    @property
    def results(self):
        return [x for r in self.riders for x in r.results]

    @results.setter
    def results(self, res):
        for r, _, o, _ in self._each([], list(res), []):
            r.results = o


class _GatherRider(_Rider):
    def __init__(self, blks, rows=None, fill=None):
        super().__init__(blks, [jax.ShapeDtypeStruct((N_DEV,) + b.shape, b.dtype) for b in blks], rows, fill)
        counts = [(r[1] if r is not None else b.shape[0]) for r, b in zip(self.rows, blks)]
        self.halves = [n % 32 == 0 for n in counts]

    def _copies(self, x_refs, out_refs, sems):
        nb = len(self.out_shape)
        send_sems, recv_sems, local_sems = sems
        x, y, c = _coords()
        me, sib = (x, y, c), (x, y, 1 - c)
        xn, yn, dg = _flip(4, x, y, c), _flip(2, x, y, c), _flip(6, x, y, c)
        plans = []
        for w in range(nb):
            own = self.span(w, x_refs[w])

            def copy(k, block, to, src=None, half=None, w=w):
                slot = self.span(w, out_refs[w], _lin(block), half=half)
                return pltpu.make_async_remote_copy(
                    src_ref=slot if src is None else src, dst_ref=slot, send_sem=send_sems.at[k * nb + w],
                    recv_sem=recv_sems.at[k * nb + w], device_id=to, device_id_type=MESH)

            mine = pltpu.make_async_copy(own, self.span(w, out_refs[w], _lin(me)), local_sems.at[w])
            first = [copy(0, me, sib, src=own), copy(1, me, xn, src=own), copy(2, me, yn, src=own)]
            if self.halves[w]:
                relay = [(copy(1, xn, me), [copy(3, xn, yn, half=0), copy(5, xn, sib)]),
                         (copy(2, yn, me), [copy(4, yn, xn, half=1), copy(6, yn, sib)])]
                last = [(copy(3, dg, me, half=0), []), (copy(4, dg, me, half=1), [copy(7, dg, sib)])]
            else:
                first.append(copy(3, me, dg, src=own))
                relay = [(copy(1, xn, me), [copy(5, xn, sib)]), (copy(2, yn, me), [copy(6, yn, sib)])]
                last = [(copy(3, dg, me), [copy(7, dg, sib)])]
            other = lambda p: (p[0], p[1], 1 - c)
            from_sib = [copy(0, sib, me), copy(5, other(xn), me), copy(6, other(yn), me), copy(7, other(dg), me)]
            plans.append((mine, first, relay, last, from_sib))
        return plans

    def start(self, ins, outs, sems):
        for mine, first, _, _, _ in self._copies(ins, outs, sems):
            for cp in [mine] + first:
                cp.start()

    def relay(self, ins, outs, sems):
        for _, _, relay, _, _ in self._copies(ins, outs, sems):
            for arrival, released in relay:
                arrival.wait_recv()
                for cp in released:
                    cp.start()

    def mid(self, ins, outs, sems):
        for _, _, _, last, _ in self._copies(ins, outs, sems):
            for arrival, released in last:
                arrival.wait_recv()
                for cp in released:
                    cp.start()

    def finish(self, ins, outs, sems):
        for mine, first, relay, last, from_sib in self._copies(ins, outs, sems):
            for cp in from_sib:
                cp.wait_recv()
            for cp in first + [cp for _, released in relay + last for cp in released]:
                cp.wait_send()
            mine.wait()


class _ExchangeRider(_Rider):
    def __init__(self, gs, rows=None, fill=None):
        super().__init__(gs, [jax.ShapeDtypeStruct(g.shape, g.dtype) for g in gs], rows, fill)

    def _copies(self, g_refs, out_refs, sems):
        nb = len(self.out_shape)
        send_sems, recv_sems, local_sems = sems
        x, y, c = _coords()
        me = _lin((x, y, c))

        def copy(k, w, landing):
            peer = _flip(k, x, y, c)
            src, dst = (me, _lin(peer)) if landing else (_lin(peer), me)
            return pltpu.make_async_remote_copy(
                src_ref=self.span(w, g_refs[w], src), dst_ref=self.span(w, out_refs[w], dst),
                send_sem=send_sems.at[(k - 1) * nb + w], recv_sem=recv_sems.at[(k - 1) * nb + w],
                device_id=peer, device_id_type=MESH)

        mines = [pltpu.make_async_copy(self.span(w, g_refs[w], me), self.span(w, out_refs[w], me), local_sems.at[w])
                 for w in range(nb)]
        sends = [copy(k, w, False) for w in range(nb) for k in range(1, N_DEV)]
        landings = [copy(k, w, True) for w in range(nb) for k in range(1, N_DEV)]
        return mines, sends, landings

    def start(self, ins, outs, sems):
        mines, sends, _ = self._copies(ins, outs, sems)
        for cp in mines + sends:
            cp.start()

    def finish(self, ins, outs, sems):
        mines, sends, landings = self._copies(ins, outs, sems)
        for cp in landings:
            cp.wait_recv()
        for cp in sends:
            cp.wait_send()
        for mine in mines:
            mine.wait()


class _SiblingRider(_Rider):
    def __init__(self, gs):
        super().__init__(gs, [jax.ShapeDtypeStruct((4,) + g.shape[1:], g.dtype) for g in gs])

    def _copies(self, g_refs, out_refs, sems, landing):
        nb = len(self.out_shape)
        send_sems, recv_sems, _ = sems
        x, y, c = _coords()
        whose = c if landing else 1 - c
        return [pltpu.make_async_remote_copy(
            src_ref=g_refs[w].at[2 * q + whose], dst_ref=out_refs[w].at[q], send_sem=send_sems.at[q * nb + w],
            recv_sem=recv_sems.at[q * nb + w], device_id=(x, y, 1 - c), device_id_type=MESH)
            for w in range(nb) for q in range(4)]

    def start(self, ins, outs, sems):
        for cp in self._copies(ins, outs, sems, False):
            cp.start()

    def finish(self, ins, outs, sems):
        for cp in self._copies(ins, outs, sems, True):
            cp.wait_recv()
        for cp in self._copies(ins, outs, sems, False):
            cp.wait_send()


class _ChipRider(_Rider):
    FLIPS = (4, 2, 6)

    def __init__(self, ps, rows=None, fill=None):
        super().__init__(ps, [jax.ShapeDtypeStruct(p.shape, p.dtype) for p in ps], rows, fill)

    def _copies(self, p_refs, out_refs, sems):
        nb = len(self.out_shape)
        send_sems, recv_sems, local_sems = sems
        x, y, c = _coords()
        my_chip = 2 * x + y

        def copy(j, w, landing):
            peer = _flip(self.FLIPS[j], x, y, c)
            peer_chip = 2 * peer[0] + peer[1]
            src, dst = (my_chip, peer_chip) if landing else (peer_chip, my_chip)
            return pltpu.make_async_remote_copy(
                src_ref=self.span(w, p_refs[w], src), dst_ref=self.span(w, out_refs[w], dst),
                send_sem=send_sems.at[j * nb + w], recv_sem=recv_sems.at[j * nb + w],
                device_id=peer, device_id_type=MESH)

        mines = [pltpu.make_async_copy(self.span(w, p_refs[w], my_chip), self.span(w, out_refs[w], my_chip),
                                       local_sems.at[w]) for w in range(nb)]
        sends = [copy(j, w, False) for w in range(nb) for j in range(3)]
        landings = [copy(j, w, True) for w in range(nb) for j in range(3)]
        return mines, sends, landings

    def start(self, ins, outs, sems):
        mines, sends, _ = self._copies(ins, outs, sems)
        for cp in mines + sends:
            cp.start()

    def finish(self, ins, outs, sems):
        mines, sends, landings = self._copies(ins, outs, sems)
        for cp in landings:
            cp.wait_recv()
        for cp in sends:
            cp.wait_send()
        for mine in mines:
            mine.wait()


def _pairsum(name, g, land):
    _, r, cols = g.shape
    tb = r if r % 16 else _row_tile(r, 16, max(16, (1024 * 1024) // cols))
    core = lax.axis_index("c").astype(jnp.int32).reshape(1)

    def body(core_ref, g_ref, l_ref, o_ref):
        o_ref[...] = (g_ref[...].astype(F32) + l_ref[...].astype(F32)).astype(o_ref.dtype)

    return pl.pallas_call(
        body, name=name, out_shape=jax.ShapeDtypeStruct((4, r, cols), g.dtype),
        grid_spec=pltpu.PrefetchScalarGridSpec(
            num_scalar_prefetch=1, grid=(4, r // tb),
            in_specs=[pl.BlockSpec((None, None, tb, cols), lambda q, i, core_ref: (q, core_ref[0], i, 0)),
                      pl.BlockSpec((None, tb, cols), lambda q, i, core_ref: (q, i, 0))],
            out_specs=pl.BlockSpec((None, tb, cols), lambda q, i, core_ref: (q, i, 0))),
        compiler_params=_params(("arbitrary", "arbitrary")),
    )(core, g.reshape(4, 2, r, cols), land)


def _run_rider(name, rider):
    n_in, n_out = len(rider.ins), len(rider.out_shape)

    def body(*refs):
        ride = (refs[:n_in], refs[n_in:n_in + n_out], refs[n_in + n_out:])
        rider.start(*ride)
        rider.relay(*ride)
        rider.mid(*ride)
        rider.finish(*ride)

    rider.results = pl.pallas_call(
        body, name=name, out_shape=rider.out_shape, in_specs=[ANY] * n_in, out_specs=[ANY] * n_out,
        scratch_shapes=rider.scratch, input_output_aliases=dict(rider.aliases),
        compiler_params=pltpu.CompilerParams(has_side_effects=True),
    )(*rider.ins)
    return rider.results


def _as_matrix(name, g):
    if name == 'w_in':
        return g
    if name in COL_SHARDED:
        return jnp.transpose(g, (1, 0, 2)).reshape(g.shape[1], N_DEV * g.shape[2])
    return g.reshape(N_DEV * g.shape[1], g.shape[2])


def _by_owner(name, g):
    if name == 'w_in':
        return g
    if name in COL_SHARDED:
        return jnp.transpose(g.reshape(g.shape[0], N_DEV, g.shape[1] // N_DEV), (1, 0, 2))
    return g.reshape(N_DEV, g.shape[0] // N_DEV, g.shape[1])


class _Comm:
    def __init__(self, shards):
        self.shards, self.W, self.parts, self.partial, self.sent = shards, {}, {}, {}, {}
        self.from_sibling, self.pairs = {}, {}

    def _ride(self, cls, names, srcs, part, sink):
        part = part or {}
        rider = cls(srcs, rows=[part[n][:2] if n in part else None for n in names],
                    fill=[self.partial.pop((sink, n), None) for n in names])
        rider.names, rider.sink = names, sink
        rider.unfinished = {n for n in names if n in part and not part[n][2]}
        return rider

    def gather(self, names, part=None):
        return self._ride(_GatherRider, names, [self.shards[n] for n in names], part, 'W')

    def exchange(self, names, G, part=None):
        for n in names:
            if n not in self.sent:
                self.sent[n] = _by_owner(n, G[n])
        return self._ride(_ExchangeRider, names, [self.sent[n] for n in names], part, 'parts')

    def to_sibling(self, names, G):
        for n in names:
            self.sent[n] = _by_owner(n, G[n])
        rider = _SiblingRider([self.sent[n] for n in names])
        rider.names, rider.sink, rider.unfinished = names, 'sibling', set()
        return rider

    def pairsum(self, names):
        for n in names:
            self.pairs[n] = _pairsum("pairsum_" + n, self.sent[n], self.from_sibling.pop(n))

    def to_owner(self, names, part=None):
        return self._ride(_ChipRider, names, [self.pairs[n] for n in names], part, 'parts')

    def collect(self, rider):
        for n, res in zip(rider.names, rider.results):
            if n in rider.unfinished:
                self.partial[(rider.sink, n)] = res
            elif rider.sink == 'W':
                self.W[n] = _as_matrix(n, res)
            elif rider.sink == 'sibling':
                self.from_sibling[n] = res
            else:
                self.parts[n] = res


def _adamw_math(p_ref, w_ref, m_ref, v_ref, g_ref, d_ref, nm_ref, nv_ref):
    c1 = 1.0 - ADAM_B1 ** ADAM_STEP
    c2 = 1.0 - ADAM_B2 ** ADAM_STEP
    g = p_ref[0].astype(F32)
    for s in range(1, p_ref.shape[0]):
        g = g + p_ref[s].astype(F32)
    nm = ADAM_B1 * m_ref[...] + (1.0 - ADAM_B1) * g
    nv = ADAM_B2 * v_ref[...] + (1.0 - ADAM_B2) * (g * g)
    g_ref[...] = g
    nm_ref[...] = nm
    nv_ref[...] = nv
    d_ref[...] = -ADAM_LR * ((nm / c1) / (jnp.sqrt(nv / c2) + ADAM_EPS) + ADAM_WD * w_ref[...])


def _adamw_vectors(parts, ws, ms, vs, loss_parts):
    k = len(ws)

    def body(*refs):
        for i in range(k):
            _adamw_math(refs[i], refs[k + i], refs[2 * k + i], refs[3 * k + i], *refs[4 * k + 1 + 4 * i:4 * k + 5 + 4 * i])
        lp, lo = refs[4 * k], refs[8 * k + 1]
        lo[...] = functools.reduce(jnp.add, [lp[s] for s in range(lp.shape[0])])

    return pl.pallas_call(
        body, name="adamw_vectors",
        out_shape=[jax.ShapeDtypeStruct(w.shape, F32) for w in ws for _ in range(4)] + [jax.ShapeDtypeStruct((1, 128), F32)],
        compiler_params=_params(),
    )(*parts, *ws, *ms, *vs, loss_parts)


def _adamw(name, parts, w, m, v, tb):
    n, R, Wd = parts.shape
    assert R % tb == 0
    body = functools.partial(_adamw_math)

    row = pl.BlockSpec((tb, Wd), lambda i: (i, 0))
    return pl.pallas_call(
        body, name=name, grid=(R // tb,),
        in_specs=[pl.BlockSpec((n, tb, Wd), lambda i: (0, i, 0)), row, row, row],
        out_specs=[row, row, row, row], out_shape=[jax.ShapeDtypeStruct((R, Wd), F32)] * 4,
        compiler_params=_params(("arbitrary",)),
    )(parts, w, m, v)


def _row_tile(R, unit, cap):
    best = unit
    for t in range(unit, cap + 1, unit):
        if R % t == 0:
            best = t
    return best


def kernel(x, ffn1_w_gate, ffn1_w_up, ffn1_w_down, ln1_g, ln1_b, w_in, b_in, ret_gn_g, conv_k, conv_b, conv_ln_g, conv_ln_b, w_ret_o, w_conv_o, w_out, ln2_g, ln2_b, ffn2_w_gate, ffn2_w_up, ffn2_w_down, ln3_g, ln3_b, loss_target, m_ffn1_w_gate, m_ffn1_w_up, m_ffn1_w_down, m_ln1_g, m_ln1_b, m_w_in, m_b_in, m_ret_gn_g, m_conv_k, m_conv_b, m_conv_ln_g, m_conv_ln_b, m_w_ret_o, m_w_conv_o, m_w_out, m_ln2_g, m_ln2_b, m_ffn2_w_gate, m_ffn2_w_up, m_ffn2_w_down, m_ln3_g, m_ln3_b, v_ffn1_w_gate, v_ffn1_w_up, v_ffn1_w_down, v_ln1_g, v_ln1_b, v_w_in, v_b_in, v_ret_gn_g, v_conv_k, v_conv_b, v_conv_ln_g, v_conv_ln_b, v_w_ret_o, v_w_conv_o, v_w_out, v_ln2_g, v_ln2_b, v_ffn2_w_gate, v_ffn2_w_up, v_ffn2_w_down, v_ln3_g, v_ln3_b):
    given = dict(locals())
    wts = {n: given[n] for n in WEIGHTS}
    mom = {n: given['m_' + n] for n in WEIGHTS}
    var = {n: given['v_' + n] for n in WEIGHTS}

    def shard2d(a):
        return a.reshape(a.shape[-3] * a.shape[-2] if a.ndim == 4 else a.shape[-2], a.shape[-1])

    comm = _Comm({n: shard2d(wts[n]).astype(BF16) for n in BIG})
    P = {n: wts[n].reshape(1, -1) for n in SMALL}
    loss, grad_x, _, S = _local_step(x[0], loss_target[0], None, P, comm=comm)

    parts = comm.parts
    res = {}
    for n in BIG:
        rows, cols = parts[n].shape[1:]
        tb = rows if rows % 16 else _row_tile(rows, 16, max(16, (256 * 1024) // cols))
        res[n] = _adamw("adamw_" + n, parts[n], shard2d(wts[n]), shard2d(mom[n]), shard2d(var[n]), tb)

    vec_parts = _run_rider("gather_vector_grads", _GatherRider([S[n] for n in SMALL] + [loss]))
    vec = _adamw_vectors(vec_parts[:-1], [P[n] for n in SMALL], [mom[n].reshape(1, -1) for n in SMALL],
                         [var[n].reshape(1, -1) for n in SMALL], vec_parts[-1])
    for i, n in enumerate(SMALL):
        res[n] = vec[4 * i:4 * i + 4]

    outs = [vec[-1][0, 0], grad_x[None]]
    for k in range(4):
        for n in WEIGHTS:
            outs.append(res[n][k].reshape(wts[n].shape))
    return tuple(outs)
```

```python
import functools
import math

import jax
import jax.numpy as jnp
from jax import lax
from jax.experimental import pallas as pl
from jax.experimental.pallas import tpu as pltpu

F32 = jnp.float32
BF16 = jnp.bfloat16

N_DEV = 8
LN_EPS = 1e-5
ALPHA = 2.0 ** 0.25
RET_DK = 128
RET_DV = 256
RET_CHUNK = 256
ROPE_BASE = 10000.0
CONV_WIDTH = 31
HALO = 32
ADAM_LR, ADAM_B1, ADAM_B2, ADAM_EPS, ADAM_WD, ADAM_STEP = 0.001, 0.9, 0.999, 1e-08, 0.01, 10
VMEM_LIMIT = 52 * 1024 * 1024
MESH = pl.DeviceIdType.MESH
ANY = pl.BlockSpec(memory_space=pl.ANY)

BIG = ['ffn1_w_gate', 'ffn1_w_up', 'ffn1_w_down', 'w_in', 'w_ret_o', 'w_conv_o', 'w_out',
       'ffn2_w_gate', 'ffn2_w_up', 'ffn2_w_down', 'conv_k']
COL_SHARDED = {'ffn1_w_gate', 'ffn1_w_up', 'w_in', 'ffn2_w_gate', 'ffn2_w_up', 'conv_k'}
SMALL = ['ln1_g', 'ln1_b', 'b_in', 'ret_gn_g', 'conv_b', 'conv_ln_g', 'conv_ln_b', 'ln2_g', 'ln2_b', 'ln3_g', 'ln3_b']
WEIGHTS = ['ffn1_w_gate', 'ffn1_w_up', 'ffn1_w_down', 'ln1_g', 'ln1_b', 'w_in', 'b_in', 'ret_gn_g', 'conv_k', 'conv_b',
           'conv_ln_g', 'conv_ln_b', 'w_ret_o', 'w_conv_o', 'w_out', 'ln2_g', 'ln2_b', 'ffn2_w_gate', 'ffn2_w_up',
           'ffn2_w_down', 'ln3_g', 'ln3_b']


def _params(sem=None):
    return pltpu.CompilerParams(dimension_semantics=sem, vmem_limit_bytes=VMEM_LIMIT)


def _sigmoid(x):
    return jax.nn.sigmoid(x)


def _dsilu(x, sg):
    return sg * (1.0 + x * (1.0 - sg))


def _fit(dim, want):
    if dim <= want:
        return dim
    return max(t for t in range(128, want + 1, 128) if dim % t == 0)


def _dot(a, b, ta=False, tb=False):
    dn = (((0,) if ta else (1,), (1,) if tb else (0,)), ((), ()))
    return lax.dot_general(a, b, dn, preferred_element_type=F32)


def _mm(name, As, Bs, prods, epi, out_dtypes, *, ta=False, tb=False, tm, tn, tk, extras=(), i_outer=True,
        b3=False, o3=False, rider=None, bsum=False, epi_rows=0):
    a0, b0 = As[0], Bs[0]
    M, K = (a0.shape[1], a0.shape[0]) if ta else a0.shape
    if b3:
        S, rows, cs = b0.shape
        N = rows if tb else S * cs
        assert K == (S * cs if tb else rows)
        tn, tk = (tn, cs) if tb else (cs, tk)
    else:
        N = b0.shape[0] if tb else b0.shape[1]
    tm, tn, tk = _fit(M, tm), _fit(N, tn), _fit(K, tk)
    assert M % tm == 0 and N % tn == 0 and K % tk == 0, (name, M, N, K, tm, tn, tk)
    gi, gj, gk = M // tm, N // tn, K // tk
    grid = (gi, gj, gk) if i_outer else (gj, gi, gk)

    def ij(g0, g1):
        return (g0, g1) if i_outer else (g1, g0)

    def amap(g0, g1, k):
        i, _ = ij(g0, g1)
        return (k, i) if ta else (i, k)

    def bmap(g0, g1, k):
        _, j = ij(g0, g1)
        return (j, k) if tb else (k, j)

    def bmap3(g0, g1, k):
        _, j = ij(g0, g1)
        return (k, j, 0) if tb else (j, k, 0)

    in_specs = [pl.BlockSpec((tk, tm) if ta else (tm, tk), amap) for _ in As]
    if b3:
        in_specs += [pl.BlockSpec((None, tn, tk) if tb else (None, tk, tn), bmap3) for _ in Bs]
    else:
        in_specs += [pl.BlockSpec((tn, tk) if tb else (tk, tn), bmap) for _ in Bs]
    args = list(As) + list(Bs)
    for arr, kind, coloff in extras:
        assert coloff % tn == 0
        off = coloff // tn
        if kind == 'mn':
            in_specs.append(pl.BlockSpec((tm, tn), lambda g0, g1, k, off=off: (ij(g0, g1)[0], ij(g0, g1)[1] + off)))
        else:
            in_specs.append(pl.BlockSpec((1, tn), lambda g0, g1, k, off=off: (0, ij(g0, g1)[1] + off)))
        args.append(arr)
    if o3:
        out_shape = [jax.ShapeDtypeStruct((gj, M, tn), dt) for dt in out_dtypes]
        out_specs = [pl.BlockSpec((None, tm, tn), lambda g0, g1, k: (ij(g0, g1)[1], ij(g0, g1)[0], 0))
                     for _ in out_dtypes]
    else:
        out_shape = [jax.ShapeDtypeStruct((M, N), dt) for dt in out_dtypes]
        out_specs = [pl.BlockSpec((tm, tn), lambda g0, g1, k: ij(g0, g1)) for _ in out_dtypes]
    if bsum:
        assert gi == 1 and not tb and not b3
        out_shape.append(jax.ShapeDtypeStruct((1, N), F32))
        out_specs.append(pl.BlockSpec((1, tn), lambda g0, g1, k: (0, ij(g0, g1)[1])))
    n_a, n_b, n_e, n_o = len(As), len(Bs), len(extras), len(out_shape)
    n_p = len(prods) if gk > 1 else 0
    scratch = [pltpu.VMEM((tm, tn), F32) for _ in range(n_p)]
    if rider is not None:
        in_specs, out_specs = in_specs + [ANY] * len(rider.ins), out_specs + [ANY] * len(rider.out_shape)
        args, out_shape, scratch = args + rider.ins, out_shape + rider.out_shape, scratch + rider.scratch
    n_in, n_out = len(args), len(out_shape)

    def body(*refs):
        a_refs = refs[:n_a]
        b_refs = refs[n_a:n_a + n_b]
        e_refs = refs[n_a + n_b:n_a + n_b + n_e]
        o_refs = refs[n_in:n_in + n_o]
        acc_refs = refs[n_in + n_out:n_in + n_out + n_p]
        k = pl.program_id(2)
        if rider is not None:
            step = (pl.program_id(0) * grid[1] + pl.program_id(1)) * gk + k
            ride = (step, grid[0] * grid[1] * gk, refs[n_a + n_b + n_e:n_in], refs[n_in + n_o:n_in + n_out],
                    refs[n_in + n_out + n_p:])
            rider.begin(*ride)

        def finish(accs, rows=slice(None)):
            ex = [(e[rows, :] if kind == 'mn' else e[...]).astype(F32) for e, (_, kind, _) in zip(e_refs, extras)]
            for o, r in zip(o_refs, epi(accs, *ex)):
                o[rows, :] = r.astype(o.dtype)

        if bsum:
            @pl.when(k == 0)
            def _():
                o_refs[-1][...] = jnp.zeros_like(o_refs[-1])

            o_refs[-1][...] += _colsum(b_refs[0][...].astype(F32))

        if gk == 1:
            sub = tm if ta or not epi_rows else _fit(tm, epi_rows)
            for r0 in range(0, tm, sub):
                rows = slice(None) if ta else slice(r0, r0 + sub)
                finish([functools.reduce(jnp.add, [_dot(a_refs[ai][...] if ta else a_refs[ai][rows, :],
                                                        b_refs[bi][...], ta, tb) for ai, bi in terms])
                        for terms in prods], rows)
        else:
            @pl.when(k == 0)
            def _():
                for acc in acc_refs:
                    acc[...] = jnp.zeros_like(acc)

            for p, terms in enumerate(prods):
                for ai, bi in terms:
                    acc_refs[p][...] += _dot(a_refs[ai][...], b_refs[bi][...], ta, tb)

            @pl.when(k == gk - 1)
            def _():
                finish([acc[...] for acc in acc_refs])

        if rider is not None:
            rider.end(*ride)

    aliases = {} if rider is None else {n_a + n_b + n_e + p: n_o + o for p, o in rider.aliases.items()}
    res = pl.pallas_call(
        body, name=name, grid=grid, in_specs=in_specs, out_specs=out_specs, out_shape=out_shape,
        scratch_shapes=scratch, input_output_aliases=aliases,
        compiler_params=_params(("arbitrary", "arbitrary", "arbitrary")),
    )(*args)
    if rider is not None:
        rider.results = res[n_o:]
    return res[:n_o]


def _rows(name, fn, ins, outs, *, T, tb, rider=None):
    tb = min(tb, T)
    assert T % tb == 0
    in_specs, args = [], []
    for arr, kind, width, cb in ins:
        if kind == 'r':
            in_specs.append(pl.BlockSpec((tb, width), lambda i, _, cb=cb: (i, cb)))
        else:
            in_specs.append(pl.BlockSpec((1, width), lambda i, _, cb=cb: (0, cb)))
        args.append(arr)
    out_shape, out_specs = [], []
    for kind, width, dtype in outs:
        if kind == 'r':
            out_shape.append(jax.ShapeDtypeStruct((T, width), dtype))
            out_specs.append(pl.BlockSpec((tb, width), lambda i, _: (i, 0)))
        elif kind == 'c':
            out_shape.append(jax.ShapeDtypeStruct((T, 1), dtype))
            out_specs.append(pl.BlockSpec((tb, 1), lambda i, _: (i, 0)))
        else:
            out_shape.append(jax.ShapeDtypeStruct((1, width), F32))
            out_specs.append(pl.BlockSpec((1, width), lambda i, _: (0, 0)))
    n_in = len(ins)

    def body(*refs):
        i = pl.program_id(0)
        vals = fn(*[r[...] for r in refs[:n_in]])
        for (kind, _, _), o, v in zip(outs, refs[n_in:], vals):
            if kind == 'a':
                @pl.when(i == 0)
                def _(o=o):
                    o[...] = jnp.zeros_like(o)

                o[...] += v
            else:
                o[...] = v.astype(o.dtype)

    return _hosted_call(body, rider, name=name, grid=(T // tb, 1), in_specs=in_specs, out_specs=out_specs,
                        out_shape=out_shape, scratch=[], args=args)


def _colsum(v):
    return jnp.sum(v, axis=0, keepdims=True)


def _ln_stats(z):
    mu = jnp.mean(z, axis=-1, keepdims=True)
    d = z - mu
    var = jnp.mean(d * d, axis=-1, keepdims=True)
    rstd = lax.rsqrt(var + LN_EPS)
    return d * rstd, rstd


def _ln_bwd_math(dy, xhat, rstd, g):
    dxh = dy * g
    m1 = jnp.mean(dxh, axis=-1, keepdims=True)
    m2 = jnp.mean(dxh * xhat, axis=-1, keepdims=True)
    return rstd * (dxh - m1 - xhat * m2)


def _ln_fwd(name, z, g, b, T, D, rider=None):
    def fn(z, g, b):
        xhat, rstd = _ln_stats(z)
        y = xhat * g + b
        return [y, y, xhat, rstd]

    return _rows(name, fn, [(z, 'r', D, 0), (g, 'v', D, 0), (b, 'v', D, 0)],
                 [('r', D, F32), ('r', D, BF16), ('r', D, F32), ('c', 1, F32)], T=T, tb=512, rider=rider)


def _ln_bwd(name, dy, xhat, rstd, g, scale, T, D, rider=None):
    def fn(dy, xhat, rstd, g):
        dz = _ln_bwd_math(dy, xhat, rstd, g)
        return [dz, dz * scale, _colsum(dy * xhat), _colsum(dy)]

    return _rows(name, fn, [(dy, 'r', D, 0), (xhat, 'r', D, 0), (rstd, 'r', 1, 0), (g, 'v', D, 0)],
                 [('r', D, F32), ('r', D, BF16), ('a', D, F32), ('a', D, F32)], T=T, tb=512, rider=rider)


def _ln_loss_bwd(name, z, g, b, tgt, T, D):
    def fn(z, g, b, tgt):
        xhat, rstd = _ln_stats(z)
        err = xhat * g + b - tgt
        row_loss = 0.5 * jnp.mean(err * err, axis=-1, keepdims=True)
        loss = jnp.broadcast_to(jnp.sum(row_loss, axis=0, keepdims=True), (1, 128))
        dy = err * (1.0 / D)
        dz = _ln_bwd_math(dy, xhat, rstd, g)
        return [dz, dz * 0.5, _colsum(dy * xhat), _colsum(dy), loss]

    return _rows(name, fn, [(z, 'r', D, 0), (g, 'v', D, 0), (b, 'v', D, 0), (tgt, 'r', D, 0)],
                 [('r', D, F32), ('r', D, BF16), ('a', D, F32), ('a', D, F32), ('a', 128, F32)], T=T, tb=512)


class _Net:
    def __init__(self, comm, G):
        self.comm, self.G = comm, G

    def gather(self, names, part=None):
        return self.comm.gather(names, part) if self.comm else None

    def exchange(self, names, part=None):
        return self.comm.exchange(names, self.G, part) if self.comm else None

    def to_sibling(self, names):
        return self.comm.to_sibling(names, self.G) if self.comm else None

    def pairsum(self, names):
        if self.comm:
            self.comm.pairsum(names)

    def to_owner(self, names, part=None):
        return self.comm.to_owner(names, part) if self.comm else None

    def done(self, rider):
        if rider is not None:
            for one in getattr(rider, 'riders', [rider]):
                self.comm.collect(one)

    def both(self, *riders):
        return _Riders(riders) if self.comm else None


def _ffn_fwd(tag, xb, x, W, names, net, rider=None, rider_down=None):
    def epi_gu(accs):
        a, b = accs
        return [a, b, a * _sigmoid(a) * b]

    ng, nu, nd = names
    a, b, s = _mm(tag + "_gate_up", [xb], [W[ng], W[nu]], [[(0, 0)], [(0, 1)]], epi_gu, [BF16, BF16, BF16],
                  tm=1024, tn=1408, tk=1024, rider=rider, epi_rows=256)
    net.done(rider)

    def epi_down(accs, xres):
        return [ALPHA * xres + 0.5 * accs[0]]

    rider_down = rider_down() if rider_down else None
    (z,) = _mm(tag + "_down", [s], [W[nd]], [[(0, 0)]], epi_down, [F32], tm=1024, tn=1024, tk=1408,
               extras=[(x, 'mn', 0)], rider=rider_down)
    net.done(rider_down)
    return a, b, s, z


def _ffn_bwd(tag, dzh, dz, xb, a, b, s, W, names, gdt, G, net, ride, pre=(None, None)):
    ng, nu, nd = names

    def epi_ds(accs, a, b):
        ds = accs[0]
        sg = _sigmoid(a)
        return [ds * b * _dsilu(a, sg), ds * a * sg]

    rider = pre[0]() if pre[0] else None
    da, db = _mm(tag + "_ds", [dzh], [W[nd]], [[(0, 0)]], epi_ds, [BF16, BF16], tb=True, tm=1024, tn=1408, tk=1024, epi_rows=256,
                 extras=[(a, 'mn', 0), (b, 'mn', 0)], rider=rider)
    net.done(rider)
    ident = lambda accs: accs
    rider = pre[1]() if pre[1] else None
    (G[nd],) = _mm(tag + "_dwd", [s], [dzh], [[(0, 0)]], ident, [gdt], ta=True, tm=1408, tn=1024, tk=1024,
                   rider=rider)
    net.done(rider)
    if ride == 'all':
        rider = net.to_sibling([nd])
        (G[ng],) = _mm(tag + "_dwg", [xb], [da], [[(0, 0)]], ident, [gdt], ta=True, tm=1024, tn=1408, tk=1024,
                       rider=rider)
        net.done(rider)
        net.pairsum([nd])
        rider = net.to_owner([nd])
        (G[nu],) = _mm(tag + "_dwu", [xb], [db], [[(0, 0)]], ident, [gdt], ta=True, tm=1024, tn=1408, tk=1024,
                       rider=rider)
        net.done(rider)
        rider = net.to_sibling([ng, nu])
        if rider is not None:
            _run_rider(tag + "_to_sibling", rider)
            net.done(rider)
        net.pairsum([ng, nu])
        rider = net.to_owner([ng, nu])
    else:
        rider = net.to_sibling([nd]) if ride else None
        G[ng], G[nu] = _mm(tag + "_dwgu", [xb], [da, db], [[(0, 0)], [(0, 1)]], ident, [gdt, gdt], ta=True,
                           tm=1024, tn=1408, tk=1024, rider=rider)
        net.done(rider)
        if ride:
            net.pairsum([nd])
        rider = net.to_owner([nd]) if ride else None

    def epi_dx(accs, dzres):
        return [ALPHA * dzres + accs[0]]

    (dx,) = _mm(tag + "_dx", [da, db], [W[ng], W[nu]], [[(0, 0), (1, 1)]], epi_dx, [F32], tb=True,
                tm=1024, tn=1024, tk=1408, extras=[(dz, 'mn', 0)], rider=rider)
    net.done(rider)
    return dx


def _ret_tables(H, T):
    C = RET_CHUNK
    log_g = jnp.log(1.0 - jnp.exp2(-5.0 - jnp.arange(H, dtype=F32)))
    idx = jnp.arange(C, dtype=F32)
    diff = idx[:, None] - idx[None, :]
    dm = jnp.where(diff[None] >= 0, jnp.exp(jnp.maximum(diff, 0.0)[None] * log_g[:, None, None]), 0.0)
    xi = jnp.exp((idx[None, :] + 1.0) * log_g[:, None])[:, :, None]
    zeta = jnp.exp((C - 1.0 - idx)[None, :] * log_g[:, None])[:, :, None]
    gc = jnp.broadcast_to(jnp.exp(C * log_g)[:, None, None], (H, 1, RET_DV))
    half = RET_DK // 2
    freqs = ROPE_BASE ** (-jnp.arange(half, dtype=F32) / half)
    ang = jnp.arange(T, dtype=F32)[:, None] * freqs[None, :]
    cos, sin = jnp.cos(ang), jnp.sin(ang)
    cosf = jnp.concatenate([cos, cos], axis=1)
    sins = jnp.concatenate([-sin, sin], axis=1)
    return dm, xi, zeta, gc, cosf, sins


def _rot(x, cosf, sins):
    return x * cosf + pltpu.roll(x, RET_DK // 2, 1) * sins


def _rot_bwd(dy, cosf, sins):
    return dy * cosf + pltpu.roll(dy * sins, RET_DK // 2, 1)


RET_HB = 8


def _ret_specs(H, HB, rev, NC):
    C, G = RET_CHUNK, H // HB
    nn = (lambda n: NC - 1 - n) if rev else (lambda n: n)
    return [
        pl.BlockSpec((C, HB * RET_DK), lambda h, n: (nn(n), h)),
        pl.BlockSpec((C, HB * RET_DK), lambda h, n: (nn(n), G + h)),
        pl.BlockSpec((C, HB * RET_DV), lambda h, n: (nn(n), G + h)),
        pl.BlockSpec((C, HB * RET_DV), lambda h, n: (nn(n), 2 * G + h)),
        pl.BlockSpec((C, RET_DK), lambda h, n: (nn(n), 0)),
        pl.BlockSpec((C, RET_DK), lambda h, n: (nn(n), 0)),
        pl.BlockSpec((1, HB * RET_DV), lambda h, n: (0, h)),
        pl.BlockSpec((HB, C, C), lambda h, n: (h, 0, 0)),
        pl.BlockSpec((HB, C, 1), lambda h, n: (h, 0, 0)),
        pl.BlockSpec((HB, C, 1), lambda h, n: (h, 0, 0)),
        pl.BlockSpec((HB, 1, RET_DV), lambda h, n: (h, 0, 0)),
    ]


def _ret_fwd(proj, gn_g, tabs, H, T, rider=None):
    C, NC = RET_CHUNK, T // RET_CHUNK
    HB = min(RET_HB, H)
    dm, xi, zeta, gc, cosf, sins = tabs
    scale = RET_DK ** -0.5

    def body(q_ref, k_ref, v_ref, g_ref, cos_ref, sin_ref, gn_ref, dm_ref, xi_ref, zt_ref, gc_ref,
             r_ref, ri_ref, st_ref, state):
        @pl.when(pl.program_id(1) == 0)
        def _():
            state[...] = jnp.zeros_like(state)

        cs, sn = cos_ref[...], sin_ref[...]
        hs = range(HB)
        qk = [slice(h * RET_DK, (h + 1) * RET_DK) for h in hs]
        vv = [slice(h * RET_DV, (h + 1) * RET_DV) for h in hs]
        kr = [_rot(k_ref[:, qk[h]].astype(F32), cs, sn) for h in hs]
        qb = [(_rot(q_ref[:, qk[h]].astype(F32), cs, sn) * scale).astype(BF16) for h in hs]
        kb = [kr[h].astype(BF16) for h in hs]
        kzb = [(kr[h] * zt_ref[h]).astype(BF16) for h in hs]
        vb = [v_ref[:, vv[h]].astype(BF16) for h in hs]
        st = [state[h] for h in hs]
        stb = [st[h].astype(BF16) for h in hs]
        sb = [(_dot(qb[h], kb[h], tb=True) * dm_ref[h]).astype(BF16) for h in hs]
        cross = [_dot(qb[h], stb[h]) for h in hs]
        kv = [_dot(kzb[h], vb[h], ta=True) for h in hs]
        intra = [_dot(sb[h], vb[h]) for h in hs]
        for h in hs:
            st_ref[h] = stb[h]
            state[h] = gc_ref[h] * st[h] + kv[h]
        for h in hs:
            r = intra[h] + cross[h] * xi_ref[h]
            rhat, _ = _ln_stats(r)
            g = g_ref[:, vv[h]].astype(F32)
            r_ref[:, vv[h]] = r
            ri_ref[:, vv[h]] = (g * _sigmoid(g) * (rhat * gn_ref[:, vv[h]])).astype(BF16)

    VW = H * RET_DV
    return _hosted_call(
        body, rider, name="ret_fwd", grid=(H // HB, NC), in_specs=_ret_specs(H, HB, False, NC),
        out_specs=[pl.BlockSpec((C, HB * RET_DV), lambda h, n: (n, h)),
                   pl.BlockSpec((C, HB * RET_DV), lambda h, n: (n, h)),
                   pl.BlockSpec((HB, None, RET_DK, RET_DV), lambda h, n: (h, n, 0, 0))],
        out_shape=[jax.ShapeDtypeStruct((T, VW), F32), jax.ShapeDtypeStruct((T, VW), BF16),
                   jax.ShapeDtypeStruct((H, NC, RET_DK, RET_DV), BF16)],
        scratch=[pltpu.VMEM((HB, RET_DK, RET_DV), F32)],
        args=[proj, proj, proj, proj, cosf, sins, gn_g, dm, xi, zeta, gc])


def _hosted_call(body, rider, *, name, grid, in_specs, out_specs, out_shape, scratch, args):
    n_in, n_out, n_scr = len(args), len(out_shape), len(scratch)
    if rider is None:
        hosted = body
    else:
        n_ri, n_ro = len(rider.ins), len(rider.out_shape)
        in_specs, out_specs = in_specs + [ANY] * n_ri, out_specs + [ANY] * n_ro
        args, out_shape, scratch = args + rider.ins, out_shape + rider.out_shape, scratch + rider.scratch

        def hosted(*refs):
            o0, s0 = n_in + n_ri, n_in + n_ri + n_out + n_ro
            step = pl.program_id(0) * grid[1] + pl.program_id(1)
            ride = (step, grid[0] * grid[1], refs[n_in:o0], refs[o0 + n_out:s0], refs[s0 + n_scr:])
            rider.begin(*ride)
            body(*refs[:n_in], *refs[o0:o0 + n_out], *refs[s0:s0 + n_scr])
            rider.end(*ride)

    aliases = {} if rider is None else {n_in + p: n_out + o for p, o in rider.aliases.items()}
    res = pl.pallas_call(
        hosted, name=name, grid=grid, in_specs=in_specs, out_specs=out_specs, out_shape=out_shape,
        scratch_shapes=scratch, input_output_aliases=aliases, compiler_params=_params(("arbitrary", "arbitrary")),
    )(*args)
    if rider is not None:
        rider.results = res[n_out:]
    return res[:n_out]


def _ret_bwd(dri, r, states, proj, gn_g, tabs, H, T, in_w, rider=None):
    C, NC = RET_CHUNK, T // RET_CHUNK
    HB = min(RET_HB, H)
    dm, xi, zeta, gc, cosf, sins = tabs
    scale = RET_DK ** -0.5

    def body(q_ref, k_ref, v_ref, g_ref, cos_ref, sin_ref, gn_ref, dm_ref, xi_ref, zt_ref, gc_ref,
             dri_ref, r_ref, st_ref, dp_ref, dgn_ref, dstate):
        @pl.when(pl.program_id(1) == 0)
        def _():
            dstate[...] = jnp.zeros_like(dstate)
            dgn_ref[...] = jnp.zeros_like(dgn_ref)

        cs, sn = cos_ref[...], sin_ref[...]
        hs = range(HB)
        qk = [slice(h * RET_DK, (h + 1) * RET_DK) for h in hs]
        vv = [slice(h * RET_DV, (h + 1) * RET_DV) for h in hs]
        qr = [_rot(q_ref[:, qk[h]].astype(F32), cs, sn) * scale for h in hs]
        kr = [_rot(k_ref[:, qk[h]].astype(F32), cs, sn) for h in hs]
        qb = [qr[h].astype(BF16) for h in hs]
        kb = [kr[h].astype(BF16) for h in hs]
        vb = [v_ref[:, vv[h]].astype(BF16) for h in hs]
        qxb = [(qr[h] * xi_ref[h]).astype(BF16) for h in hs]
        kzb = [(kr[h] * zt_ref[h]).astype(BF16) for h in hs]
        drb = []
        for h in hs:
            rhat, rstd = _ln_stats(r_ref[:, vv[h]])
            g, gn, dpre = g_ref[:, vv[h]].astype(F32), gn_ref[:, vv[h]], dri_ref[:, vv[h]]
            sg = _sigmoid(g)
            dp_ref[:, 2 * QW + VW + h * RET_DV:2 * QW + VW + (h + 1) * RET_DV] = (
                dpre * (rhat * gn) * _dsilu(g, sg)).astype(BF16)
            drn = dpre * (g * sg)
            dgn_ref[:, vv[h]] += _colsum(drn * rhat)
            drb.append(_ln_bwd_math(drn, rhat, rstd, gn).astype(BF16))
        ds1 = [dstate[h] for h in hs]
        ds1b = [ds1[h].astype(BF16) for h in hs]
        sb = [(_dot(qb[h], kb[h], tb=True) * dm_ref[h]).astype(BF16) for h in hs]
        dsb = [(_dot(drb[h], vb[h], tb=True) * dm_ref[h]).astype(BF16) for h in hs]
        dq_x = [_dot(drb[h], st_ref[h], tb=True) for h in hs]
        dk_x = [_dot(vb[h], ds1b[h], tb=True) for h in hs]
        dv_x = [_dot(kzb[h], ds1b[h]) for h in hs]
        dst = [_dot(qxb[h], drb[h], ta=True) for h in hs]
        for h in hs:
            dstate[h] = gc_ref[h] * ds1[h] + dst[h]
        dv_i = [_dot(sb[h], drb[h], ta=True) for h in hs]
        dq_i = [_dot(dsb[h], kb[h]) for h in hs]
        dk_i = [_dot(dsb[h], qb[h], ta=True) for h in hs]
        for h in hs:
            dp_ref[:, 2 * QW + h * RET_DV:2 * QW + (h + 1) * RET_DV] = (dv_i[h] + dv_x[h]).astype(BF16)
            dq = dq_i[h] + dq_x[h] * xi_ref[h]
            dk = dk_i[h] + dk_x[h] * zt_ref[h]
            dp_ref[:, qk[h]] = _rot_bwd(dq * scale, cs, sn).astype(BF16)
            dp_ref[:, QW + h * RET_DK:QW + (h + 1) * RET_DK] = _rot_bwd(dk, cs, sn).astype(BF16)

    VW, QW = H * RET_DV, H * RET_DK
    rv = lambda n: NC - 1 - n
    in_specs = _ret_specs(H, HB, True, NC) + [
        pl.BlockSpec((C, HB * RET_DV), lambda h, n: (rv(n), h)),
        pl.BlockSpec((C, HB * RET_DV), lambda h, n: (rv(n), h)),
        pl.BlockSpec((HB, None, RET_DK, RET_DV), lambda h, n: (h, rv(n), 0, 0)),
    ]
    assert HB == H
    return _hosted_call(
        body, rider, name="ret_bwd", grid=(1, NC), in_specs=in_specs,
        out_specs=[pl.BlockSpec((C, 2 * QW + 2 * VW), lambda h, n: (rv(n), 0)),
                   pl.BlockSpec((1, VW), lambda h, n: (0, 0))],
        out_shape=[jax.ShapeDtypeStruct((T, in_w), BF16), jax.ShapeDtypeStruct((1, VW), F32)],
        scratch=[pltpu.VMEM((HB, RET_DK, RET_DV), F32)],
        args=[proj, proj, proj, proj, cosf, sins, gn_g, dm, xi, zeta, gc, dri, r, states])


CONV_CW = 128
CONV_TB = 512


def _conv_fwd(proj, kpad, bias, off_a, CC, T, rider=None):
    tb, cw = min(CONV_TB, T), CONV_CW
    hb = tb // HALO
    ca, cb = off_a // cw, (off_a + CC) // cw

    def body(a_ref, b_ref, ap_ref, bp_ref, k_ref, bias_ref, u1_ref, win):
        i = pl.program_id(0)
        keep = (i > 0).astype(F32)
        win[0:HALO, :] = ap_ref[...].astype(F32) * _sigmoid(bp_ref[...].astype(F32)) * keep
        win[HALO:, :] = a_ref[...].astype(F32) * _sigmoid(b_ref[...].astype(F32))
        acc = jnp.broadcast_to(bias_ref[...], (tb, cw))
        for w in range(CONV_WIDTH):
            acc = acc + k_ref[w:w + 1, :] * win[pl.ds(HALO - (CONV_WIDTH - 1) + w, tb), :]
        u1_ref[...] = acc

    prev = lambda i: jnp.maximum(i * hb - 1, 0)
    (u1,) = _hosted_call(
        body, rider, name="conv_fwd", grid=(T // tb, CC // cw),
        in_specs=[pl.BlockSpec((tb, cw), lambda i, c: (i, ca + c)),
                  pl.BlockSpec((tb, cw), lambda i, c: (i, cb + c)),
                  pl.BlockSpec((HALO, cw), lambda i, c: (prev(i), ca + c)),
                  pl.BlockSpec((HALO, cw), lambda i, c: (prev(i), cb + c)),
                  pl.BlockSpec((HALO, cw), lambda i, c: (0, c)),
                  pl.BlockSpec((1, cw), lambda i, c: (0, c))],
        out_specs=[pl.BlockSpec((tb, cw), lambda i, c: (i, c))],
        out_shape=[jax.ShapeDtypeStruct((T, CC), F32)],
        scratch=[pltpu.VMEM((tb + HALO, cw), F32)],
        args=[proj, proj, proj, proj, kpad, bias])
    return u1


def _conv_bwd(du1, proj, kpad, off_a, CC, T, rider=None):
    tb, cw = min(CONV_TB, T), CONV_CW
    hb = tb // HALO
    nt = T // tb
    ca, cb = off_a // cw, (off_a + CC) // cw

    def body(d_ref, dn_ref, a_ref, b_ref, ap_ref, bp_ref, k_ref, da_ref, db_ref, dk_ref, winu, wind):
        i = pl.program_id(1)
        a, b = a_ref[...].astype(F32), b_ref[...].astype(F32)
        sgb = _sigmoid(b)
        winu[0:HALO, :] = ap_ref[...].astype(F32) * _sigmoid(bp_ref[...].astype(F32)) * (i > 0).astype(F32)
        winu[HALO:, :] = a * sgb
        d = d_ref[...]
        wind[0:tb, :] = d
        wind[tb:, :] = dn_ref[...] * (i < nt - 1).astype(F32)

        @pl.when(i == 0)
        def _():
            dk_ref[...] = jnp.zeros_like(dk_ref)

        du0 = jnp.zeros((tb, cw), F32)
        for w in range(CONV_WIDTH):
            du0 = du0 + k_ref[w:w + 1, :] * wind[pl.ds(CONV_WIDTH - 1 - w, tb), :]
            dk_ref[w:w + 1, :] += _colsum(winu[pl.ds(HALO - (CONV_WIDTH - 1) + w, tb), :] * d)
        da_ref[...] = (du0 * sgb).astype(BF16)
        db_ref[...] = (du0 * a * sgb * (1.0 - sgb)).astype(BF16)

    prev = lambda i: jnp.maximum(i * hb - 1, 0)
    nxt = lambda i: jnp.minimum((i + 1) * hb, T // HALO - 1)
    return _hosted_call(
        body, rider, name="conv_bwd", grid=(CC // cw, nt),
        in_specs=[pl.BlockSpec((tb, cw), lambda c, i: (i, c)),
                  pl.BlockSpec((HALO, cw), lambda c, i: (nxt(i), c)),
                  pl.BlockSpec((tb, cw), lambda c, i: (i, ca + c)),
                  pl.BlockSpec((tb, cw), lambda c, i: (i, cb + c)),
                  pl.BlockSpec((HALO, cw), lambda c, i: (prev(i), ca + c)),
                  pl.BlockSpec((HALO, cw), lambda c, i: (prev(i), cb + c)),
                  pl.BlockSpec((HALO, cw), lambda c, i: (0, c))],
        out_specs=[pl.BlockSpec((tb, cw), lambda c, i: (i, c)),
                   pl.BlockSpec((tb, cw), lambda c, i: (i, c)),
                   pl.BlockSpec((HALO, cw), lambda c, i: (0, c))],
        out_shape=[jax.ShapeDtypeStruct((T, CC), BF16), jax.ShapeDtypeStruct((T, CC), BF16),
                   jax.ShapeDtypeStruct((HALO, CC), F32)],
        scratch=[pltpu.VMEM((tb + HALO, cw), F32), pltpu.VMEM((tb + HALO, cw), F32)],
        args=[du1, du1, proj, proj, proj, proj, kpad])


FFN1 = ('ffn1_w_gate', 'ffn1_w_up', 'ffn1_w_down')
FFN2 = ('ffn2_w_gate', 'ffn2_w_up', 'ffn2_w_down')


def _local_step(x, tgt, W, P, gdt=BF16, comm=None):
    T, D = x.shape
    G = {}
    net = _Net(comm, G)
    if comm is not None:
        W = comm.W
        first = net.gather(['ffn1_w_gate', 'ffn1_w_up'])
    (xb,) = _rows("x_to_bf16", lambda v: [v], [(x, 'r', D, 0)], [('r', D, BF16)], T=T, tb=512,
                  rider=first if comm is not None else None)
    if comm is not None:
        net.done(first)
    VW = P['ret_gn_g'].shape[1]
    H = VW // RET_DV
    QW = H * RET_DK
    CC = P['conv_b'].shape[1]
    off_glu = 2 * QW + 2 * VW
    off_gate = off_glu + 2 * CC
    ident = lambda accs: accs

    a1, b1, s1, z1 = _ffn_fwd("ffn1", xb, x, W, FFN1, net,
                              rider=net.gather(['ffn1_w_down', 'w_in'], {'w_in': (0, (3 * D) // 8, False)}),
                              rider_down=lambda: net.gather(['w_in'], {'w_in': ((3 * D) // 8, (3 * D) // 8, False)}))
    rider = net.gather(['w_in'], {'w_in': ((3 * D) // 4, D // 4, True)})
    x1, x1b, xh1, rs1 = _ln_fwd("ln1", z1, P['ln1_g'], P['ln1_b'], T, D, rider=rider)
    net.done(rider)

    rest = net.gather(['conv_k', 'w_ret_o', 'w_conv_o', 'w_out', 'ffn2_w_down'])
    (proj,) = _mm("w_in", [x1b], [W['w_in']], [[(0, 0)]], lambda accs, bias: [accs[0] + bias], [F32],
                  tm=2048, tn=0, tk=1024, extras=[(P['b_in'], 'n', 0)], i_outer=False, b3=True, rider=rest)
    net.done(rest)
    tabs = _ret_tables(H, T)
    rider = net.gather(['ffn2_w_gate', 'ffn2_w_up'])
    r, ret_in, states = _ret_fwd(proj, P['ret_gn_g'], tabs, H, T, rider=rider)
    net.done(rider)
    kpad = jnp.pad(W['conv_k'].astype(F32), ((0, HALO - CONV_WIDTH), (0, 0)))
    u1 = _conv_fwd(proj, kpad, P['conv_b'], off_glu, CC, T)

    def conv_ln(u1, g, b):
        xhat, rstd = _ln_stats(u1)
        u2 = xhat * g + b
        return [xhat, rstd, u2 * _sigmoid(u2)]

    xhc, rsc, u3 = _rows("conv_ln", conv_ln, [(u1, 'r', CC, 0), (P['conv_ln_g'], 'v', CC, 0), (P['conv_ln_b'], 'v', CC, 0)],
                         [('r', CC, F32), ('c', 1, F32), ('r', CC, BF16)], T=T, tb=512)
    (ret_out,) = _mm("ret_o", [ret_in], [W['w_ret_o']], [[(0, 0)]], ident, [F32], tm=1024, tn=1024, tk=2048)

    def epi_merge(accs, ret_out, gr, gc):
        conv_out = accs[0]
        return [conv_out, _sigmoid(gr) * ret_out + _sigmoid(gc) * conv_out]

    conv_out, merged = _mm("conv_o_merge", [u3], [W['w_conv_o']], [[(0, 0)]], epi_merge, [F32, BF16],
                           tm=512, tn=D, tk=1024, epi_rows=256,
                           extras=[(ret_out, 'mn', 0), (proj, 'mn', off_gate), (proj, 'mn', off_gate + D)])
    (z2,) = _mm("w_out", [merged], [W['w_out']], [[(0, 0)]], lambda accs, xr: [ALPHA * xr + accs[0]], [F32],
                tm=1024, tn=1024, tk=1024, extras=[(x1, 'mn', 0)])
    x2, x2b, xh2, rs2 = _ln_fwd("ln2", z2, P['ln2_g'], P['ln2_b'], T, D)
    a2, b2, s2, z3 = _ffn_fwd("ffn2", x2b, x2, W, FFN2, net)
    dz3, dz3h, g_ln3_g, g_ln3_b, loss = _ln_loss_bwd("ln3_loss", z3, P['ln3_g'], P['ln3_b'], tgt, T, D)

    S = {'ln3_g': g_ln3_g, 'ln3_b': g_ln3_b}
    dy2 = _ffn_bwd("ffn2b", dz3h, dz3, x2b, a2, b2, s2, W, FFN2, gdt, G, net, 'down')
    rider = net.to_sibling(['ffn2_w_gate', 'ffn2_w_up'])
    dz2, dz2b, S['ln2_g'], S['ln2_b'] = _ln_bwd("ln2b", dy2, xh2, rs2, P['ln2_g'], 1.0, T, D, rider=rider)
    net.done(rider)
    net.pairsum(['ffn2_w_gate', 'ffn2_w_up'])

    (G['w_out'],) = _mm("d_w_out", [merged], [dz2b], [[(0, 0)]], ident, [gdt], ta=True, tm=1024, tn=1024, tk=1024)

    def epi_dmerge(accs, ret_out, conv_out, gr, gc):
        dm_ = accs[0]
        sr, sc = _sigmoid(gr), _sigmoid(gc)
        return [dm_ * sr, dm_ * sc, dm_ * ret_out * sr * (1.0 - sr), dm_ * conv_out * sc * (1.0 - sc)]

    rider = net.to_sibling(['w_out'])
    dret_out, dconv_out, dgate_r, dgate_c = _mm(
        "d_merge", [dz2b], [W['w_out']], [[(0, 0)]], epi_dmerge, [BF16, BF16, BF16, BF16], tb=True,
        tm=512, tn=D, tk=1024, epi_rows=256, rider=rider,
        extras=[(ret_out, 'mn', 0), (conv_out, 'mn', 0), (proj, 'mn', off_gate), (proj, 'mn', off_gate + D)])
    net.done(rider)
    (G['w_ret_o'],) = _mm("d_w_ret_o", [ret_in], [dret_out], [[(0, 0)]], ident, [gdt], ta=True, tm=1024, tn=1024, tk=1024)
    (G['w_conv_o'],) = _mm("d_w_conv_o", [u3], [dconv_out], [[(0, 0)]], ident, [gdt], ta=True, tm=1024, tn=1024, tk=1024)
    rider = net.to_sibling(['w_ret_o', 'w_conv_o'])
    (dri,) = _mm("d_ret_in", [dret_out], [W['w_ret_o']], [[(0, 0)]], ident, [F32], tb=True, tm=1024, tn=1024, tk=1024,
                 rider=rider)
    net.done(rider)
    net.pairsum(['w_out', 'w_ret_o', 'w_conv_o'])
    rider = net.to_owner(['ffn2_w_gate', 'ffn2_w_up'])
    dproj, S['ret_gn_g'] = _ret_bwd(dri, r, states, proj, P['ret_gn_g'], tabs, H, T, proj.shape[1], rider=rider)
    net.done(rider)

    def epi_du2(accs, xhat, g, b):
        u2 = xhat * g + b
        return [accs[0] * _dsilu(u2, _sigmoid(u2))]

    (du2,) = _mm("d_u3", [dconv_out], [W['w_conv_o']], [[(0, 0)]], epi_du2, [F32], tb=True, tm=512, tn=CC, tk=1024, epi_rows=256,
                 extras=[(xhc, 'mn', 0), (P['conv_ln_g'], 'n', 0), (P['conv_ln_b'], 'n', 0)])

    def conv_ln_bwd(du2, xhat, rstd, g):
        du1 = _ln_bwd_math(du2, xhat, rstd, g)
        return [du1, _colsum(du2 * xhat), _colsum(du2), _colsum(du1)]

    du1, S['conv_ln_g'], S['conv_ln_b'], S['conv_b'] = _rows(
        "conv_ln_bwd", conv_ln_bwd, [(du2, 'r', CC, 0), (xhc, 'r', CC, 0), (rsc, 'r', 1, 0), (P['conv_ln_g'], 'v', CC, 0)],
        [('r', CC, F32), ('a', CC, F32), ('a', CC, F32), ('a', CC, F32)], T=T, tb=512)
    dglu_a, dglu_b, dkpad = _conv_bwd(du1, proj, kpad, off_glu, CC, T)
    G['conv_k'] = dkpad[:CONV_WIDTH].astype(gdt)

    for off, piece in ((off_glu, dglu_a), (off_glu + CC, dglu_b), (off_gate, dgate_r), (off_gate + D, dgate_c)):
        dproj = lax.dynamic_update_slice(dproj, piece, (0, off))
    IN_W = dproj.shape[1]
    rider = net.both(net.exchange(['conv_k']),
                     net.to_owner(['w_out', 'w_ret_o', 'w_conv_o']))
    G['w_in'], S['b_in'] = _mm("d_w_in", [x1b], [dproj], [[(0, 0)]], ident, [gdt], ta=True, o3=True, bsum=True,
                               tm=1024, tn=W['w_in'].shape[2], tk=1024, rider=rider)
    net.done(rider)
    cuts = [0, (3 * D) // 16, (43 * D) // 64, D]
    w_in_rows = [{'w_in': (cuts[i], cuts[i + 1] - cuts[i], i == 2)} for i in range(3)]
    rider = net.to_sibling(['w_in'])
    (dy1,) = _mm("d_x1", [dproj], [W['w_in']], [[(0, 0)]], lambda accs, dzr: [ALPHA * dzr + accs[0]], [F32], tb=True,
                 b3=True, tm=1024, tn=1024, tk=0, extras=[(dz2, 'mn', 0)], rider=rider)
    net.done(rider)
    net.pairsum(['w_in'])
    rider = net.to_owner(['w_in'], w_in_rows[0])
    dz1, dz1h, S['ln1_g'], S['ln1_b'] = _ln_bwd("ln1b", dy1, xh1, rs1, P['ln1_g'], 0.5, T, D, rider=rider)
    net.done(rider)
    grad_x = _ffn_bwd("ffn1b", dz1h, dz1, xb, a1, b1, s1, W, FFN1, gdt, G, net, 'all',
                      pre=(lambda: net.to_owner(['w_in'], w_in_rows[1]), lambda: net.to_owner(['w_in'], w_in_rows[2])))
    return loss, grad_x, G, S


def _coords():
    return lax.axis_index("x"), lax.axis_index("y"), lax.axis_index("c")


def _flip(k, x, y, c):
    return (1 - x if k & 4 else x, 1 - y if k & 2 else y, 1 - c if k & 1 else c)


def _lin(p):
    return 4 * p[0] + 2 * p[1] + p[2]


class _Rider:
    def __init__(self, ins, out_shape, rows=None, fill=None):
        nb = len(ins)
        self.rows = rows or [None] * nb
        fill = fill or [None] * nb
        self.aliases = {nb + i: w for i, w in enumerate(w for w in range(nb) if fill[w] is not None)}
        self.ins = list(ins) + [f for f in fill if f is not None]
        self.out_shape, self.results = list(out_shape), None
        self.scratch = [pltpu.SemaphoreType.DMA((8 * nb,)), pltpu.SemaphoreType.DMA((8 * nb,)),
                        pltpu.SemaphoreType.DMA((nb,))]

    def span(self, w, ref, *slot, half=None):
        rows = self.rows[w]
        if half is not None:
            first, count = rows if rows is not None else (0, self.out_shape[w].shape[1])
            rows = (first + half * (count // 2), count // 2)
        if rows is None:
            return ref.at[slot] if slot else ref
        return ref.at[(*slot, pl.ds(*rows))]

    def begin(self, step, n_steps, ins, outs, sems):
        @pl.when(step == 0)
        def _():
            self.start(ins, outs, sems)

        @pl.when(step == min(n_steps - 1, (5 * n_steps) // 8))
        def _():
            self.relay(ins, outs, sems)

        @pl.when(step == n_steps - 1)
        def _():
            self.mid(ins, outs, sems)

    def end(self, step, n_steps, ins, outs, sems):
        @pl.when(step == n_steps - 1)
        def _():
            self.finish(ins, outs, sems)

    def relay(self, ins, outs, sems):
        pass

    def mid(self, ins, outs, sems):
        pass


class _Riders:
    def __init__(self, riders):
        self.riders = list(riders)
        self.ins = [a for r in self.riders for a in r.ins]
        self.out_shape = [o for r in self.riders for o in r.out_shape]
        self.scratch = [c for r in self.riders for c in r.scratch]
        self.aliases, n_in, n_out = {}, 0, 0
        for r in self.riders:
            self.aliases.update({n_in + p: n_out + o for p, o in r.aliases.items()})
            n_in, n_out = n_in + len(r.ins), n_out + len(r.out_shape)

    def _each(self, ins, outs, sems):
        i = o = c = 0
        for r in self.riders:
            yield r, ins[i:i + len(r.ins)], outs[o:o + len(r.out_shape)], sems[c:c + len(r.scratch)]
            i, o, c = i + len(r.ins), o + len(r.out_shape), c + len(r.scratch)

    def begin(self, step, n_steps, ins, outs, sems):
        for r, i, o, c in self._each(ins, outs, sems):
            r.begin(step, n_steps, i, o, c)

    def end(self, step, n_steps, ins, outs, sems):
        for r, i, o, c in self._each(ins, outs, sems):
            r.end(step, n_steps, i, o, c)

    @property
    def results(self):
        return [x for r in self.riders for x in r.results]

    @results.setter
    def results(self, res):
        for r, _, o, _ in self._each([], list(res), []):
            r.results = o


class _GatherRider(_Rider):
    def __init__(self, blks, rows=None, fill=None):
        super().__init__(blks, [jax.ShapeDtypeStruct((N_DEV,) + b.shape, b.dtype) for b in blks], rows, fill)
        counts = [(r[1] if r is not None else b.shape[0]) for r, b in zip(self.rows, blks)]
        self.halves = [n % 32 == 0 for n in counts]

    def _copies(self, x_refs, out_refs, sems):
        nb = len(self.out_shape)
        send_sems, recv_sems, local_sems = sems
        x, y, c = _coords()
        me, sib = (x, y, c), (x, y, 1 - c)
        xn, yn, dg = _flip(4, x, y, c), _flip(2, x, y, c), _flip(6, x, y, c)
        plans = []
        for w in range(nb):
            own = self.span(w, x_refs[w])

            def copy(k, block, to, src=None, half=None, w=w):
                slot = self.span(w, out_refs[w], _lin(block), half=half)
                return pltpu.make_async_remote_copy(
                    src_ref=slot if src is None else src, dst_ref=slot, send_sem=send_sems.at[k * nb + w],
                    recv_sem=recv_sems.at[k * nb + w], device_id=to, device_id_type=MESH)

            mine = pltpu.make_async_copy(own, self.span(w, out_refs[w], _lin(me)), local_sems.at[w])
            first = [copy(0, me, sib, src=own), copy(1, me, xn, src=own), copy(2, me, yn, src=own)]
            if self.halves[w]:
                relay = [(copy(1, xn, me), [copy(3, xn, yn, half=0), copy(5, xn, sib)]),
                         (copy(2, yn, me), [copy(4, yn, xn, half=1), copy(6, yn, sib)])]
                last = [(copy(3, dg, me, half=0), []), (copy(4, dg, me, half=1), [copy(7, dg, sib)])]
            else:
                first.append(copy(3, me, dg, src=own))
                relay = [(copy(1, xn, me), [copy(5, xn, sib)]), (copy(2, yn, me), [copy(6, yn, sib)])]
                last = [(copy(3, dg, me), [copy(7, dg, sib)])]
            other = lambda p: (p[0], p[1], 1 - c)
            from_sib = [copy(0, sib, me), copy(5, other(xn), me), copy(6, other(yn), me), copy(7, other(dg), me)]
            plans.append((mine, first, relay, last, from_sib))
        return plans

    def start(self, ins, outs, sems):
        for mine, first, _, _, _ in self._copies(ins, outs, sems):
            for cp in [mine] + first:
                cp.start()

    def relay(self, ins, outs, sems):
        for _, _, relay, _, _ in self._copies(ins, outs, sems):
            for arrival, released in relay:
                arrival.wait_recv()
                for cp in released:
                    cp.start()

    def mid(self, ins, outs, sems):
        for _, _, _, last, _ in self._copies(ins, outs, sems):
            for arrival, released in last:
                arrival.wait_recv()
                for cp in released:
                    cp.start()

    def finish(self, ins, outs, sems):
        for mine, first, relay, last, from_sib in self._copies(ins, outs, sems):
            for cp in from_sib:
                cp.wait_recv()
            for cp in first + [cp for _, released in relay + last for cp in released]:
                cp.wait_send()
            mine.wait()


class _ExchangeRider(_Rider):
    def __init__(self, gs, rows=None, fill=None):
        super().__init__(gs, [jax.ShapeDtypeStruct(g.shape, g.dtype) for g in gs], rows, fill)

    def _copies(self, g_refs, out_refs, sems):
        nb = len(self.out_shape)
        send_sems, recv_sems, local_sems = sems
        x, y, c = _coords()
        me = _lin((x, y, c))

        def copy(k, w, landing):
            peer = _flip(k, x, y, c)
            src, dst = (me, _lin(peer)) if landing else (_lin(peer), me)
            return pltpu.make_async_remote_copy(
                src_ref=self.span(w, g_refs[w], src), dst_ref=self.span(w, out_refs[w], dst),
                send_sem=send_sems.at[(k - 1) * nb + w], recv_sem=recv_sems.at[(k - 1) * nb + w],
                device_id=peer, device_id_type=MESH)

        mines = [pltpu.make_async_copy(self.span(w, g_refs[w], me), self.span(w, out_refs[w], me), local_sems.at[w])
                 for w in range(nb)]
        sends = [copy(k, w, False) for w in range(nb) for k in range(1, N_DEV)]
        landings = [copy(k, w, True) for w in range(nb) for k in range(1, N_DEV)]
        return mines, sends, landings

    def start(self, ins, outs, sems):
        mines, sends, _ = self._copies(ins, outs, sems)
        for cp in mines + sends:
            cp.start()

    def finish(self, ins, outs, sems):
        mines, sends, landings = self._copies(ins, outs, sems)
        for cp in landings:
            cp.wait_recv()
        for cp in sends:
            cp.wait_send()
        for mine in mines:
            mine.wait()


class _SiblingRider(_Rider):
    def __init__(self, gs):
        super().__init__(gs, [jax.ShapeDtypeStruct((4,) + g.shape[1:], g.dtype) for g in gs])

    def _copies(self, g_refs, out_refs, sems, landing):
        nb = len(self.out_shape)
        send_sems, recv_sems, _ = sems
        x, y, c = _coords()
        whose = c if landing else 1 - c
        return [pltpu.make_async_remote_copy(
            src_ref=g_refs[w].at[2 * q + whose], dst_ref=out_refs[w].at[q], send_sem=send_sems.at[q * nb + w],
            recv_sem=recv_sems.at[q * nb + w], device_id=(x, y, 1 - c), device_id_type=MESH)
            for w in range(nb) for q in range(4)]

    def start(self, ins, outs, sems):
        for cp in self._copies(ins, outs, sems, False):
            cp.start()

    def finish(self, ins, outs, sems):
        for cp in self._copies(ins, outs, sems, True):
            cp.wait_recv()
        for cp in self._copies(ins, outs, sems, False):
            cp.wait_send()


class _ChipRider(_Rider):
    FLIPS = (4, 2, 6)

    def __init__(self, ps, rows=None, fill=None):
        super().__init__(ps, [jax.ShapeDtypeStruct(p.shape, p.dtype) for p in ps], rows, fill)

    def _copies(self, p_refs, out_refs, sems):
        nb = len(self.out_shape)
        send_sems, recv_sems, local_sems = sems
        x, y, c = _coords()
        my_chip = 2 * x + y

        def copy(j, w, landing):
            peer = _flip(self.FLIPS[j], x, y, c)
            peer_chip = 2 * peer[0] + peer[1]
            src, dst = (my_chip, peer_chip) if landing else (peer_chip, my_chip)
            return pltpu.make_async_remote_copy(
                src_ref=self.span(w, p_refs[w], src), dst_ref=self.span(w, out_refs[w], dst),
                send_sem=send_sems.at[j * nb + w], recv_sem=recv_sems.at[j * nb + w],
                device_id=peer, device_id_type=MESH)

        mines = [pltpu.make_async_copy(self.span(w, p_refs[w], my_chip), self.span(w, out_refs[w], my_chip),
                                       local_sems.at[w]) for w in range(nb)]
        sends = [copy(j, w, False) for w in range(nb) for j in range(3)]
        landings = [copy(j, w, True) for w in range(nb) for j in range(3)]
        return mines, sends, landings

    def start(self, ins, outs, sems):
        mines, sends, _ = self._copies(ins, outs, sems)
        for cp in mines + sends:
            cp.start()

    def finish(self, ins, outs, sems):
        mines, sends, landings = self._copies(ins, outs, sems)
        for cp in landings:
            cp.wait_recv()
        for cp in sends:
            cp.wait_send()
        for mine in mines:
            mine.wait()


def _pairsum(name, g, land):
    _, r, cols = g.shape
    tb = r if r % 16 else _row_tile(r, 16, max(16, (1024 * 1024) // cols))
    core = lax.axis_index("c").astype(jnp.int32).reshape(1)

    def body(core_ref, g_ref, l_ref, o_ref):
        o_ref[...] = (g_ref[...].astype(F32) + l_ref[...].astype(F32)).astype(o_ref.dtype)

    return pl.pallas_call(
        body, name=name, out_shape=jax.ShapeDtypeStruct((4, r, cols), g.dtype),
        grid_spec=pltpu.PrefetchScalarGridSpec(
            num_scalar_prefetch=1, grid=(4, r // tb),
            in_specs=[pl.BlockSpec((None, None, tb, cols), lambda q, i, core_ref: (q, core_ref[0], i, 0)),
                      pl.BlockSpec((None, tb, cols), lambda q, i, core_ref: (q, i, 0))],
            out_specs=pl.BlockSpec((None, tb, cols), lambda q, i, core_ref: (q, i, 0))),
        compiler_params=_params(("arbitrary", "arbitrary")),
    )(core, g.reshape(4, 2, r, cols), land)


def _run_rider(name, rider):
    n_in, n_out = len(rider.ins), len(rider.out_shape)

    def body(*refs):
        ride = (refs[:n_in], refs[n_in:n_in + n_out], refs[n_in + n_out:])
        rider.start(*ride)
        rider.relay(*ride)
        rider.mid(*ride)
        rider.finish(*ride)

    rider.results = pl.pallas_call(
        body, name=name, out_shape=rider.out_shape, in_specs=[ANY] * n_in, out_specs=[ANY] * n_out,
        scratch_shapes=rider.scratch, input_output_aliases=dict(rider.aliases),
        compiler_params=pltpu.CompilerParams(has_side_effects=True),
    )(*rider.ins)
    return rider.results


def _as_matrix(name, g):
    if name == 'w_in':
        return g
    if name in COL_SHARDED:
        return jnp.transpose(g, (1, 0, 2)).reshape(g.shape[1], N_DEV * g.shape[2])
    return g.reshape(N_DEV * g.shape[1], g.shape[2])


def _by_owner(name, g):
    if name == 'w_in':
        return g
    if name in COL_SHARDED:
        return jnp.transpose(g.reshape(g.shape[0], N_DEV, g.shape[1] // N_DEV), (1, 0, 2))
    return g.reshape(N_DEV, g.shape[0] // N_DEV, g.shape[1])


class _Comm:
    def __init__(self, shards):
        self.shards, self.W, self.parts, self.partial, self.sent = shards, {}, {}, {}, {}
        self.from_sibling, self.pairs = {}, {}

    def _ride(self, cls, names, srcs, part, sink):
        part = part or {}
        rider = cls(srcs, rows=[part[n][:2] if n in part else None for n in names],
                    fill=[self.partial.pop((sink, n), None) for n in names])
        rider.names, rider.sink = names, sink
        rider.unfinished = {n for n in names if n in part and not part[n][2]}
        return rider

    def gather(self, names, part=None):
        return self._ride(_GatherRider, names, [self.shards[n] for n in names], part, 'W')

    def exchange(self, names, G, part=None):
        for n in names:
            if n not in self.sent:
                self.sent[n] = _by_owner(n, G[n])
        return self._ride(_ExchangeRider, names, [self.sent[n] for n in names], part, 'parts')

    def to_sibling(self, names, G):
        for n in names:
            self.sent[n] = _by_owner(n, G[n])
        rider = _SiblingRider([self.sent[n] for n in names])
        rider.names, rider.sink, rider.unfinished = names, 'sibling', set()
        return rider

    def pairsum(self, names):
        for n in names:
            self.pairs[n] = _pairsum("pairsum_" + n, self.sent[n], self.from_sibling.pop(n))

    def to_owner(self, names, part=None):
        return self._ride(_ChipRider, names, [self.pairs[n] for n in names], part, 'parts')

    def collect(self, rider):
        for n, res in zip(rider.names, rider.results):
            if n in rider.unfinished:
                self.partial[(rider.sink, n)] = res
            elif rider.sink == 'W':
                self.W[n] = _as_matrix(n, res)
            elif rider.sink == 'sibling':
                self.from_sibling[n] = res
            else:
                self.parts[n] = res


def _adamw_math(p_ref, w_ref, m_ref, v_ref, g_ref, d_ref, nm_ref, nv_ref):
    c1 = 1.0 - ADAM_B1 ** ADAM_STEP
    c2 = 1.0 - ADAM_B2 ** ADAM_STEP
    g = p_ref[0].astype(F32)
    for s in range(1, p_ref.shape[0]):
        g = g + p_ref[s].astype(F32)
    nm = ADAM_B1 * m_ref[...] + (1.0 - ADAM_B1) * g
    nv = ADAM_B2 * v_ref[...] + (1.0 - ADAM_B2) * (g * g)
    g_ref[...] = g
    nm_ref[...] = nm
    nv_ref[...] = nv
    d_ref[...] = -ADAM_LR * ((nm / c1) / (jnp.sqrt(nv / c2) + ADAM_EPS) + ADAM_WD * w_ref[...])


def _adamw_vectors(parts, ws, ms, vs, loss_parts):
    k = len(ws)

    def body(*refs):
        for i in range(k):
            _adamw_math(refs[i], refs[k + i], refs[2 * k + i], refs[3 * k + i], *refs[4 * k + 1 + 4 * i:4 * k + 5 + 4 * i])
        lp, lo = refs[4 * k], refs[8 * k + 1]
        lo[...] = functools.reduce(jnp.add, [lp[s] for s in range(lp.shape[0])])

    return pl.pallas_call(
        body, name="adamw_vectors",
        out_shape=[jax.ShapeDtypeStruct(w.shape, F32) for w in ws for _ in range(4)] + [jax.ShapeDtypeStruct((1, 128), F32)],
        compiler_params=_params(),
    )(*parts, *ws, *ms, *vs, loss_parts)


def _adamw(name, parts, w, m, v, tb):
    n, R, Wd = parts.shape
    assert R % tb == 0
    body = functools.partial(_adamw_math)

    row = pl.BlockSpec((tb, Wd), lambda i: (i, 0))
    return pl.pallas_call(
        body, name=name, grid=(R // tb,),
        in_specs=[pl.BlockSpec((n, tb, Wd), lambda i: (0, i, 0)), row, row, row],
        out_specs=[row, row, row, row], out_shape=[jax.ShapeDtypeStruct((R, Wd), F32)] * 4,
        compiler_params=_params(("arbitrary",)),
    )(parts, w, m, v)


def _row_tile(R, unit, cap):
    best = unit
    for t in range(unit, cap + 1, unit):
        if R % t == 0:
            best = t
    return best


def kernel(x, ffn1_w_gate, ffn1_w_up, ffn1_w_down, ln1_g, ln1_b, w_in, b_in, ret_gn_g, conv_k, conv_b, conv_ln_g, conv_ln_b, w_ret_o, w_conv_o, w_out, ln2_g, ln2_b, ffn2_w_gate, ffn2_w_up, ffn2_w_down, ln3_g, ln3_b, loss_target, m_ffn1_w_gate, m_ffn1_w_up, m_ffn1_w_down, m_ln1_g, m_ln1_b, m_w_in, m_b_in, m_ret_gn_g, m_conv_k, m_conv_b, m_conv_ln_g, m_conv_ln_b, m_w_ret_o, m_w_conv_o, m_w_out, m_ln2_g, m_ln2_b, m_ffn2_w_gate, m_ffn2_w_up, m_ffn2_w_down, m_ln3_g, m_ln3_b, v_ffn1_w_gate, v_ffn1_w_up, v_ffn1_w_down, v_ln1_g, v_ln1_b, v_w_in, v_b_in, v_ret_gn_g, v_conv_k, v_conv_b, v_conv_ln_g, v_conv_ln_b, v_w_ret_o, v_w_conv_o, v_w_out, v_ln2_g, v_ln2_b, v_ffn2_w_gate, v_ffn2_w_up, v_ffn2_w_down, v_ln3_g, v_ln3_b):
    given = dict(locals())
    wts = {n: given[n] for n in WEIGHTS}
    mom = {n: given['m_' + n] for n in WEIGHTS}
    var = {n: given['v_' + n] for n in WEIGHTS}

    def shard2d(a):
        return a.reshape(a.shape[-3] * a.shape[-2] if a.ndim == 4 else a.shape[-2], a.shape[-1])

    comm = _Comm({n: shard2d(wts[n]).astype(BF16) for n in BIG})
    P = {n: wts[n].reshape(1, -1) for n in SMALL}
    loss, grad_x, _, S = _local_step(x[0], loss_target[0], None, P, comm=comm)

    parts = comm.parts
    res = {}
    for n in BIG:
        rows, cols = parts[n].shape[1:]
        tb = rows if rows % 16 else _row_tile(rows, 16, max(16, (256 * 1024) // cols))
        res[n] = _adamw("adamw_" + n, parts[n], shard2d(wts[n]), shard2d(mom[n]), shard2d(var[n]), tb)

    vec_parts = _run_rider("gather_vector_grads", _GatherRider([S[n] for n in SMALL] + [loss]))
    vec = _adamw_vectors(vec_parts[:-1], [P[n] for n in SMALL], [mom[n].reshape(1, -1) for n in SMALL],
                         [var[n].reshape(1, -1) for n in SMALL], vec_parts[-1])
    for i, n in enumerate(SMALL):
        res[n] = vec[4 * i:4 * i + 4]

    outs = [vec[-1][0, 0], grad_x[None]]
    for k in range(4):
        for n in WEIGHTS:
            outs.append(res[n][k].reshape(wts[n].shape))
    return tuple(outs)
```

```python
import functools
import math

import jax
import jax.numpy as jnp
from jax import lax
from jax.experimental import pallas as pl
from jax.experimental.pallas import tpu as pltpu

F32 = jnp.float32
BF16 = jnp.bfloat16

N_DEV = 8
LN_EPS = 1e-5
ALPHA = 2.0 ** 0.25
RET_DK = 128
RET_DV = 256
RET_CHUNK = 256
ROPE_BASE = 10000.0
CONV_WIDTH = 31
HALO = 32
ADAM_LR, ADAM_B1, ADAM_B2, ADAM_EPS, ADAM_WD, ADAM_STEP = 0.001, 0.9, 0.999, 1e-08, 0.01, 10
VMEM_LIMIT = 52 * 1024 * 1024
MESH = pl.DeviceIdType.MESH
ANY = pl.BlockSpec(memory_space=pl.ANY)

BIG = ['ffn1_w_gate', 'ffn1_w_up', 'ffn1_w_down', 'w_in', 'w_ret_o', 'w_conv_o', 'w_out',
       'ffn2_w_gate', 'ffn2_w_up', 'ffn2_w_down', 'conv_k']
COL_SHARDED = {'ffn1_w_gate', 'ffn1_w_up', 'w_in', 'ffn2_w_gate', 'ffn2_w_up', 'conv_k'}
SMALL = ['ln1_g', 'ln1_b', 'b_in', 'ret_gn_g', 'conv_b', 'conv_ln_g', 'conv_ln_b', 'ln2_g', 'ln2_b', 'ln3_g', 'ln3_b']
WEIGHTS = ['ffn1_w_gate', 'ffn1_w_up', 'ffn1_w_down', 'ln1_g', 'ln1_b', 'w_in', 'b_in', 'ret_gn_g', 'conv_k', 'conv_b',
           'conv_ln_g', 'conv_ln_b', 'w_ret_o', 'w_conv_o', 'w_out', 'ln2_g', 'ln2_b', 'ffn2_w_gate', 'ffn2_w_up',
           'ffn2_w_down', 'ln3_g', 'ln3_b']


def _params(sem=None):
    return pltpu.CompilerParams(dimension_semantics=sem, vmem_limit_bytes=VMEM_LIMIT)


def _sigmoid(x):
    return jax.nn.sigmoid(x)


def _dsilu(x, sg):
    return sg * (1.0 + x * (1.0 - sg))


def _fit(dim, want):
    if dim <= want:
        return dim
    return max(t for t in range(128, want + 1, 128) if dim % t == 0)


def _dot(a, b, ta=False, tb=False):
    dn = (((0,) if ta else (1,), (1,) if tb else (0,)), ((), ()))
    return lax.dot_general(a, b, dn, preferred_element_type=F32)


def _mm(name, As, Bs, prods, epi, out_dtypes, *, ta=False, tb=False, tm, tn, tk, extras=(), i_outer=True,
        b3=False, o3=False, rider=None, bsum=False, epi_rows=0):
    a0, b0 = As[0], Bs[0]
    M, K = (a0.shape[1], a0.shape[0]) if ta else a0.shape
    if b3:
        S, rows, cs = b0.shape
        N = rows if tb else S * cs
        assert K == (S * cs if tb else rows)
        tn, tk = (tn, cs) if tb else (cs, tk)
    else:
        N = b0.shape[0] if tb else b0.shape[1]
    tm, tn, tk = _fit(M, tm), _fit(N, tn), _fit(K, tk)
    assert M % tm == 0 and N % tn == 0 and K % tk == 0, (name, M, N, K, tm, tn, tk)
    gi, gj, gk = M // tm, N // tn, K // tk
    grid = (gi, gj, gk) if i_outer else (gj, gi, gk)

    def ij(g0, g1):
        return (g0, g1) if i_outer else (g1, g0)

    def amap(g0, g1, k):
        i, _ = ij(g0, g1)
        return (k, i) if ta else (i, k)

    def bmap(g0, g1, k):
        _, j = ij(g0, g1)
        return (j, k) if tb else (k, j)

    def bmap3(g0, g1, k):
        _, j = ij(g0, g1)
        return (k, j, 0) if tb else (j, k, 0)

    in_specs = [pl.BlockSpec((tk, tm) if ta else (tm, tk), amap) for _ in As]
    if b3:
        in_specs += [pl.BlockSpec((None, tn, tk) if tb else (None, tk, tn), bmap3) for _ in Bs]
    else:
        in_specs += [pl.BlockSpec((tn, tk) if tb else (tk, tn), bmap) for _ in Bs]
    args = list(As) + list(Bs)
    for arr, kind, coloff in extras:
        assert coloff % tn == 0
        off = coloff // tn
        if kind == 'mn':
            in_specs.append(pl.BlockSpec((tm, tn), lambda g0, g1, k, off=off: (ij(g0, g1)[0], ij(g0, g1)[1] + off)))
        else:
            in_specs.append(pl.BlockSpec((1, tn), lambda g0, g1, k, off=off: (0, ij(g0, g1)[1] + off)))
        args.append(arr)
    if o3:
        out_shape = [jax.ShapeDtypeStruct((gj, M, tn), dt) for dt in out_dtypes]
        out_specs = [pl.BlockSpec((None, tm, tn), lambda g0, g1, k: (ij(g0, g1)[1], ij(g0, g1)[0], 0))
                     for _ in out_dtypes]
    else:
        out_shape = [jax.ShapeDtypeStruct((M, N), dt) for dt in out_dtypes]
        out_specs = [pl.BlockSpec((tm, tn), lambda g0, g1, k: ij(g0, g1)) for _ in out_dtypes]
    if bsum:
        assert gi == 1 and not tb and not b3
        out_shape.append(jax.ShapeDtypeStruct((1, N), F32))
        out_specs.append(pl.BlockSpec((1, tn), lambda g0, g1, k: (0, ij(g0, g1)[1])))
    n_a, n_b, n_e, n_o = len(As), len(Bs), len(extras), len(out_shape)
    n_p = len(prods) if gk > 1 else 0
    scratch = [pltpu.VMEM((tm, tn), F32) for _ in range(n_p)]
    if rider is not None:
        in_specs, out_specs = in_specs + [ANY] * len(rider.ins), out_specs + [ANY] * len(rider.out_shape)
        args, out_shape, scratch = args + rider.ins, out_shape + rider.out_shape, scratch + rider.scratch
    n_in, n_out = len(args), len(out_shape)

    def body(*refs):
        a_refs = refs[:n_a]
        b_refs = refs[n_a:n_a + n_b]
        e_refs = refs[n_a + n_b:n_a + n_b + n_e]
        o_refs = refs[n_in:n_in + n_o]
        acc_refs = refs[n_in + n_out:n_in + n_out + n_p]
        k = pl.program_id(2)
        if rider is not None:
            step = (pl.program_id(0) * grid[1] + pl.program_id(1)) * gk + k
            ride = (step, grid[0] * grid[1] * gk, refs[n_a + n_b + n_e:n_in], refs[n_in + n_o:n_in + n_out],
                    refs[n_in + n_out + n_p:])
            rider.begin(*ride)

        def finish(accs, rows=slice(None)):
            ex = [(e[rows, :] if kind == 'mn' else e[...]).astype(F32) for e, (_, kind, _) in zip(e_refs, extras)]
            for o, r in zip(o_refs, epi(accs, *ex)):
                o[rows, :] = r.astype(o.dtype)

        if bsum:
            @pl.when(k == 0)
            def _():
                o_refs[-1][...] = jnp.zeros_like(o_refs[-1])

            o_refs[-1][...] += _colsum(b_refs[0][...].astype(F32))

        if gk == 1:
            sub = tm if ta or not epi_rows else _fit(tm, epi_rows)
            for r0 in range(0, tm, sub):
                rows = slice(None) if ta else slice(r0, r0 + sub)
                finish([functools.reduce(jnp.add, [_dot(a_refs[ai][...] if ta else a_refs[ai][rows, :],
                                                        b_refs[bi][...], ta, tb) for ai, bi in terms])
                        for terms in prods], rows)
        else:
            @pl.when(k == 0)
            def _():
                for acc in acc_refs:
                    acc[...] = jnp.zeros_like(acc)

            for p, terms in enumerate(prods):
                for ai, bi in terms:
                    acc_refs[p][...] += _dot(a_refs[ai][...], b_refs[bi][...], ta, tb)

            @pl.when(k == gk - 1)
            def _():
                finish([acc[...] for acc in acc_refs])

        if rider is not None:
            rider.end(*ride)

    aliases = {} if rider is None else {n_a + n_b + n_e + p: n_o + o for p, o in rider.aliases.items()}
    res = pl.pallas_call(
        body, name=name, grid=grid, in_specs=in_specs, out_specs=out_specs, out_shape=out_shape,
        scratch_shapes=scratch, input_output_aliases=aliases,
        compiler_params=_params(("arbitrary", "arbitrary", "arbitrary")),
    )(*args)
    if rider is not None:
        rider.results = res[n_o:]
    return res[:n_o]


def _rows(name, fn, ins, outs, *, T, tb, rider=None):
    tb = min(tb, T)
    assert T % tb == 0
    in_specs, args = [], []
    for arr, kind, width, cb in ins:
        if kind == 'r':
            in_specs.append(pl.BlockSpec((tb, width), lambda i, _, cb=cb: (i, cb)))
        else:
            in_specs.append(pl.BlockSpec((1, width), lambda i, _, cb=cb: (0, cb)))
        args.append(arr)
    out_shape, out_specs = [], []
    for kind, width, dtype in outs:
        if kind == 'r':
            out_shape.append(jax.ShapeDtypeStruct((T, width), dtype))
            out_specs.append(pl.BlockSpec((tb, width), lambda i, _: (i, 0)))
        elif kind == 'c':
            out_shape.append(jax.ShapeDtypeStruct((T, 1), dtype))
            out_specs.append(pl.BlockSpec((tb, 1), lambda i, _: (i, 0)))
        else:
            out_shape.append(jax.ShapeDtypeStruct((1, width), F32))
            out_specs.append(pl.BlockSpec((1, width), lambda i, _: (0, 0)))
    n_in = len(ins)

    def body(*refs):
        i = pl.program_id(0)
        vals = fn(*[r[...] for r in refs[:n_in]])
        for (kind, _, _), o, v in zip(outs, refs[n_in:], vals):
            if kind == 'a':
                @pl.when(i == 0)
                def _(o=o):
                    o[...] = jnp.zeros_like(o)

                o[...] += v
            else:
                o[...] = v.astype(o.dtype)

    return _hosted_call(body, rider, name=name, grid=(T // tb, 1), in_specs=in_specs, out_specs=out_specs,
                        out_shape=out_shape, scratch=[], args=args)


def _colsum(v):
    return jnp.sum(v, axis=0, keepdims=True)


def _ln_stats(z):
    mu = jnp.mean(z, axis=-1, keepdims=True)
    d = z - mu
    var = jnp.mean(d * d, axis=-1, keepdims=True)
    rstd = lax.rsqrt(var + LN_EPS)
    return d * rstd, rstd


def _ln_bwd_math(dy, xhat, rstd, g):
    dxh = dy * g
    m1 = jnp.mean(dxh, axis=-1, keepdims=True)
    m2 = jnp.mean(dxh * xhat, axis=-1, keepdims=True)
    return rstd * (dxh - m1 - xhat * m2)


def _ln_fwd(name, z, g, b, T, D, rider=None):
    def fn(z, g, b):
        xhat, rstd = _ln_stats(z)
        y = xhat * g + b
        return [y, y, xhat, rstd]

    return _rows(name, fn, [(z, 'r', D, 0), (g, 'v', D, 0), (b, 'v', D, 0)],
                 [('r', D, F32), ('r', D, BF16), ('r', D, F32), ('c', 1, F32)], T=T, tb=512, rider=rider)


def _ln_bwd(name, dy, xhat, rstd, g, scale, T, D, rider=None):
    def fn(dy, xhat, rstd, g):
        dz = _ln_bwd_math(dy, xhat, rstd, g)
        return [dz, dz * scale, _colsum(dy * xhat), _colsum(dy)]

    return _rows(name, fn, [(dy, 'r', D, 0), (xhat, 'r', D, 0), (rstd, 'r', 1, 0), (g, 'v', D, 0)],
                 [('r', D, F32), ('r', D, BF16), ('a', D, F32), ('a', D, F32)], T=T, tb=512, rider=rider)


def _ln_loss_bwd(name, z, g, b, tgt, T, D):
    def fn(z, g, b, tgt):
        xhat, rstd = _ln_stats(z)
        err = xhat * g + b - tgt
        row_loss = 0.5 * jnp.mean(err * err, axis=-1, keepdims=True)
        loss = jnp.broadcast_to(jnp.sum(row_loss, axis=0, keepdims=True), (1, 128))
        dy = err * (1.0 / D)
        dz = _ln_bwd_math(dy, xhat, rstd, g)
        return [dz, dz * 0.5, _colsum(dy * xhat), _colsum(dy), loss]

    return _rows(name, fn, [(z, 'r', D, 0), (g, 'v', D, 0), (b, 'v', D, 0), (tgt, 'r', D, 0)],
                 [('r', D, F32), ('r', D, BF16), ('a', D, F32), ('a', D, F32), ('a', 128, F32)], T=T, tb=512)


class _Net:
    def __init__(self, comm, G):
        self.comm, self.G = comm, G

    def gather(self, names, part=None):
        return self.comm.gather(names, part) if self.comm else None

    def exchange(self, names, part=None):
        return self.comm.exchange(names, self.G, part) if self.comm else None

    def to_sibling(self, names):
        return self.comm.to_sibling(names, self.G) if self.comm else None

    def pairsum(self, names):
        if self.comm:
            self.comm.pairsum(names)

    def to_owner(self, names, part=None):
        return self.comm.to_owner(names, part) if self.comm else None

    def done(self, rider):
        if rider is not None:
            for one in getattr(rider, 'riders', [rider]):
                self.comm.collect(one)

    def both(self, *riders):
        return _Riders(riders) if self.comm else None


def _ffn_fwd(tag, xb, x, W, names, net, rider=None, rider_down=None):
    def epi_gu(accs):
        a, b = accs
        return [a, b, a * _sigmoid(a) * b]

    ng, nu, nd = names
    a, b, s = _mm(tag + "_gate_up", [xb], [W[ng], W[nu]], [[(0, 0)], [(0, 1)]], epi_gu, [BF16, BF16, BF16],
                  tm=1024, tn=1408, tk=1024, rider=rider, epi_rows=256)
    net.done(rider)

    def epi_down(accs, xres):
        return [ALPHA * xres + 0.5 * accs[0]]

    rider_down = rider_down() if rider_down else None
    (z,) = _mm(tag + "_down", [s], [W[nd]], [[(0, 0)]], epi_down, [F32], tm=1024, tn=1024, tk=1408,
               extras=[(x, 'mn', 0)], rider=rider_down)
    net.done(rider_down)
    return a, b, s, z


def _ffn_bwd(tag, dzh, dz, xb, a, b, s, W, names, gdt, G, net, ride, pre=(None, None)):
    ng, nu, nd = names

    def epi_ds(accs, a, b):
        ds = accs[0]
        sg = _sigmoid(a)
        return [ds * b * _dsilu(a, sg), ds * a * sg]

    rider = pre[0]() if pre[0] else None
    da, db = _mm(tag + "_ds", [dzh], [W[nd]], [[(0, 0)]], epi_ds, [BF16, BF16], tb=True, tm=1024, tn=1408, tk=1024, epi_rows=256,
                 extras=[(a, 'mn', 0), (b, 'mn', 0)], rider=rider)
    net.done(rider)
    ident = lambda accs: accs
    rider = pre[1]() if pre[1] else None
    (G[nd],) = _mm(tag + "_dwd", [s], [dzh], [[(0, 0)]], ident, [gdt], ta=True, tm=1408, tn=1024, tk=2048,
                   rider=rider)
    net.done(rider)
    if ride == 'all':
        rider = net.to_sibling([nd])
        (G[ng],) = _mm(tag + "_dwg", [xb], [da], [[(0, 0)]], ident, [gdt], ta=True, tm=1024, tn=1408, tk=2048,
                       rider=rider)
        net.done(rider)
        net.pairsum([nd])
        rider = net.to_owner([nd])
        (G[nu],) = _mm(tag + "_dwu", [xb], [db], [[(0, 0)]], ident, [gdt], ta=True, tm=1024, tn=1408, tk=2048,
                       rider=rider)
        net.done(rider)
        rider = net.to_sibling([ng, nu])
        if rider is not None:
            _run_rider(tag + "_to_sibling", rider)
            net.done(rider)
        net.pairsum([ng, nu])
        rider = net.to_owner([ng, nu])
    else:
        rider = net.to_sibling([nd]) if ride else None
        G[ng], G[nu] = _mm(tag + "_dwgu", [xb], [da, db], [[(0, 0)], [(0, 1)]], ident, [gdt, gdt], ta=True,
                           tm=1024, tn=1408, tk=1024, rider=rider)
        net.done(rider)
        if ride:
            net.pairsum([nd])
        rider = net.to_owner([nd]) if ride else None

    def epi_dx(accs, dzres):
        return [ALPHA * dzres + accs[0]]

    (dx,) = _mm(tag + "_dx", [da, db], [W[ng], W[nu]], [[(0, 0), (1, 1)]], epi_dx, [F32], tb=True,
                tm=1024, tn=1024, tk=1408, extras=[(dz, 'mn', 0)], rider=rider)
    net.done(rider)
    return dx


def _ret_tables(H, T):
    C = RET_CHUNK
    log_g = jnp.log(1.0 - jnp.exp2(-5.0 - jnp.arange(H, dtype=F32)))
    idx = jnp.arange(C, dtype=F32)
    diff = idx[:, None] - idx[None, :]
    dm = jnp.where(diff[None] >= 0, jnp.exp(jnp.maximum(diff, 0.0)[None] * log_g[:, None, None]), 0.0)
    xi = jnp.exp((idx[None, :] + 1.0) * log_g[:, None])[:, :, None]
    zeta = jnp.exp((C - 1.0 - idx)[None, :] * log_g[:, None])[:, :, None]
    gc = jnp.broadcast_to(jnp.exp(C * log_g)[:, None, None], (H, 1, RET_DV))
    half = RET_DK // 2
    freqs = ROPE_BASE ** (-jnp.arange(half, dtype=F32) / half)
    ang = jnp.arange(T, dtype=F32)[:, None] * freqs[None, :]
    cos, sin = jnp.cos(ang), jnp.sin(ang)
    cosf = jnp.concatenate([cos, cos], axis=1)
    sins = jnp.concatenate([-sin, sin], axis=1)
    return dm, xi, zeta, gc, cosf, sins


def _rot(x, cosf, sins):
    return x * cosf + pltpu.roll(x, RET_DK // 2, 1) * sins


def _rot_bwd(dy, cosf, sins):
    return dy * cosf + pltpu.roll(dy * sins, RET_DK // 2, 1)


RET_HB = 8


def _ret_specs(H, HB, rev, NC):
    C, G = RET_CHUNK, H // HB
    nn = (lambda n: NC - 1 - n) if rev else (lambda n: n)
    return [
        pl.BlockSpec((C, HB * RET_DK), lambda h, n: (nn(n), h)),
        pl.BlockSpec((C, HB * RET_DK), lambda h, n: (nn(n), G + h)),
        pl.BlockSpec((C, HB * RET_DV), lambda h, n: (nn(n), G + h)),
        pl.BlockSpec((C, HB * RET_DV), lambda h, n: (nn(n), 2 * G + h)),
        pl.BlockSpec((C, RET_DK), lambda h, n: (nn(n), 0)),
        pl.BlockSpec((C, RET_DK), lambda h, n: (nn(n), 0)),
        pl.BlockSpec((1, HB * RET_DV), lambda h, n: (0, h)),
        pl.BlockSpec((HB, C, C), lambda h, n: (h, 0, 0)),
        pl.BlockSpec((HB, C, 1), lambda h, n: (h, 0, 0)),
        pl.BlockSpec((HB, C, 1), lambda h, n: (h, 0, 0)),
        pl.BlockSpec((HB, 1, RET_DV), lambda h, n: (h, 0, 0)),
    ]


def _ret_fwd(proj, gn_g, tabs, H, T, rider=None):
    C, NC = RET_CHUNK, T // RET_CHUNK
    HB = min(RET_HB, H)
    dm, xi, zeta, gc, cosf, sins = tabs
    scale = RET_DK ** -0.5

    def body(q_ref, k_ref, v_ref, g_ref, cos_ref, sin_ref, gn_ref, dm_ref, xi_ref, zt_ref, gc_ref,
             r_ref, ri_ref, st_ref, state):
        @pl.when(pl.program_id(1) == 0)
        def _():
            state[...] = jnp.zeros_like(state)

        cs, sn = cos_ref[...], sin_ref[...]
        hs = range(HB)
        qk = [slice(h * RET_DK, (h + 1) * RET_DK) for h in hs]
        vv = [slice(h * RET_DV, (h + 1) * RET_DV) for h in hs]
        kr = [_rot(k_ref[:, qk[h]].astype(F32), cs, sn) for h in hs]
        qb = [(_rot(q_ref[:, qk[h]].astype(F32), cs, sn) * scale).astype(BF16) for h in hs]
        kb = [kr[h].astype(BF16) for h in hs]
        kzb = [(kr[h] * zt_ref[h]).astype(BF16) for h in hs]
        vb = [v_ref[:, vv[h]].astype(BF16) for h in hs]
        st = [state[h] for h in hs]
        stb = [st[h].astype(BF16) for h in hs]
        sb = [(_dot(qb[h], kb[h], tb=True) * dm_ref[h]).astype(BF16) for h in hs]
        cross = [_dot(qb[h], stb[h]) for h in hs]
        kv = [_dot(kzb[h], vb[h], ta=True) for h in hs]
        intra = [_dot(sb[h], vb[h]) for h in hs]
        for h in hs:
            st_ref[h] = stb[h]
            state[h] = gc_ref[h] * st[h] + kv[h]
        for h in hs:
            r = intra[h] + cross[h] * xi_ref[h]
            rhat, _ = _ln_stats(r)
            g = g_ref[:, vv[h]].astype(F32)
            r_ref[:, vv[h]] = r
            ri_ref[:, vv[h]] = (g * _sigmoid(g) * (rhat * gn_ref[:, vv[h]])).astype(BF16)

    VW = H * RET_DV
    return _hosted_call(
        body, rider, name="ret_fwd", grid=(H // HB, NC), in_specs=_ret_specs(H, HB, False, NC),
        out_specs=[pl.BlockSpec((C, HB * RET_DV), lambda h, n: (n, h)),
                   pl.BlockSpec((C, HB * RET_DV), lambda h, n: (n, h)),
                   pl.BlockSpec((HB, None, RET_DK, RET_DV), lambda h, n: (h, n, 0, 0))],
        out_shape=[jax.ShapeDtypeStruct((T, VW), F32), jax.ShapeDtypeStruct((T, VW), BF16),
                   jax.ShapeDtypeStruct((H, NC, RET_DK, RET_DV), BF16)],
        scratch=[pltpu.VMEM((HB, RET_DK, RET_DV), F32)],
        args=[proj, proj, proj, proj, cosf, sins, gn_g, dm, xi, zeta, gc])


def _hosted_call(body, rider, *, name, grid, in_specs, out_specs, out_shape, scratch, args):
    n_in, n_out, n_scr = len(args), len(out_shape), len(scratch)
    if rider is None:
        hosted = body
    else:
        n_ri, n_ro = len(rider.ins), len(rider.out_shape)
        in_specs, out_specs = in_specs + [ANY] * n_ri, out_specs + [ANY] * n_ro
        args, out_shape, scratch = args + rider.ins, out_shape + rider.out_shape, scratch + rider.scratch

        def hosted(*refs):
            o0, s0 = n_in + n_ri, n_in + n_ri + n_out + n_ro
            step = pl.program_id(0) * grid[1] + pl.program_id(1)
            ride = (step, grid[0] * grid[1], refs[n_in:o0], refs[o0 + n_out:s0], refs[s0 + n_scr:])
            rider.begin(*ride)
            body(*refs[:n_in], *refs[o0:o0 + n_out], *refs[s0:s0 + n_scr])
            rider.end(*ride)

    aliases = {} if rider is None else {n_in + p: n_out + o for p, o in rider.aliases.items()}
    res = pl.pallas_call(
        hosted, name=name, grid=grid, in_specs=in_specs, out_specs=out_specs, out_shape=out_shape,
        scratch_shapes=scratch, input_output_aliases=aliases, compiler_params=_params(("arbitrary", "arbitrary")),
    )(*args)
    if rider is not None:
        rider.results = res[n_out:]
    return res[:n_out]


def _ret_bwd(dri, r, states, proj, gn_g, tabs, H, T, in_w, rider=None):
    C, NC = RET_CHUNK, T // RET_CHUNK
    HB = min(RET_HB, H)
    dm, xi, zeta, gc, cosf, sins = tabs
    scale = RET_DK ** -0.5

    def body(q_ref, k_ref, v_ref, g_ref, cos_ref, sin_ref, gn_ref, dm_ref, xi_ref, zt_ref, gc_ref,
             dri_ref, r_ref, st_ref, dp_ref, dgn_ref, dstate):
        @pl.when(pl.program_id(1) == 0)
        def _():
            dstate[...] = jnp.zeros_like(dstate)
            dgn_ref[...] = jnp.zeros_like(dgn_ref)

        cs, sn = cos_ref[...], sin_ref[...]
        hs = range(HB)
        qk = [slice(h * RET_DK, (h + 1) * RET_DK) for h in hs]
        vv = [slice(h * RET_DV, (h + 1) * RET_DV) for h in hs]
        qr = [_rot(q_ref[:, qk[h]].astype(F32), cs, sn) * scale for h in hs]
        kr = [_rot(k_ref[:, qk[h]].astype(F32), cs, sn) for h in hs]
        qb = [qr[h].astype(BF16) for h in hs]
        kb = [kr[h].astype(BF16) for h in hs]
        vb = [v_ref[:, vv[h]].astype(BF16) for h in hs]
        qxb = [(qr[h] * xi_ref[h]).astype(BF16) for h in hs]
        kzb = [(kr[h] * zt_ref[h]).astype(BF16) for h in hs]
        drb = []
        for h in hs:
            rhat, rstd = _ln_stats(r_ref[:, vv[h]])
            g, gn, dpre = g_ref[:, vv[h]].astype(F32), gn_ref[:, vv[h]], dri_ref[:, vv[h]]
            sg = _sigmoid(g)
            dp_ref[:, 2 * QW + VW + h * RET_DV:2 * QW + VW + (h + 1) * RET_DV] = (
                dpre * (rhat * gn) * _dsilu(g, sg)).astype(BF16)
            drn = dpre * (g * sg)
            dgn_ref[:, vv[h]] += _colsum(drn * rhat)
            drb.append(_ln_bwd_math(drn, rhat, rstd, gn).astype(BF16))
        ds1 = [dstate[h] for h in hs]
        ds1b = [ds1[h].astype(BF16) for h in hs]
        sb = [(_dot(qb[h], kb[h], tb=True) * dm_ref[h]).astype(BF16) for h in hs]
        dsb = [(_dot(drb[h], vb[h], tb=True) * dm_ref[h]).astype(BF16) for h in hs]
        dq_x = [_dot(drb[h], st_ref[h], tb=True) for h in hs]
        dk_x = [_dot(vb[h], ds1b[h], tb=True) for h in hs]
        dv_x = [_dot(kzb[h], ds1b[h]) for h in hs]
        dst = [_dot(qxb[h], drb[h], ta=True) for h in hs]
        for h in hs:
            dstate[h] = gc_ref[h] * ds1[h] + dst[h]
        dv_i = [_dot(sb[h], drb[h], ta=True) for h in hs]
        dq_i = [_dot(dsb[h], kb[h]) for h in hs]
        dk_i = [_dot(dsb[h], qb[h], ta=True) for h in hs]
        for h in hs:
            dp_ref[:, 2 * QW + h * RET_DV:2 * QW + (h + 1) * RET_DV] = (dv_i[h] + dv_x[h]).astype(BF16)
            dq = dq_i[h] + dq_x[h] * xi_ref[h]
            dk = dk_i[h] + dk_x[h] * zt_ref[h]
            dp_ref[:, qk[h]] = _rot_bwd(dq * scale, cs, sn).astype(BF16)
            dp_ref[:, QW + h * RET_DK:QW + (h + 1) * RET_DK] = _rot_bwd(dk, cs, sn).astype(BF16)

    VW, QW = H * RET_DV, H * RET_DK
    rv = lambda n: NC - 1 - n
    in_specs = _ret_specs(H, HB, True, NC) + [
        pl.BlockSpec((C, HB * RET_DV), lambda h, n: (rv(n), h)),
        pl.BlockSpec((C, HB * RET_DV), lambda h, n: (rv(n), h)),
        pl.BlockSpec((HB, None, RET_DK, RET_DV), lambda h, n: (h, rv(n), 0, 0)),
    ]
    assert HB == H
    return _hosted_call(
        body, rider, name="ret_bwd", grid=(1, NC), in_specs=in_specs,
        out_specs=[pl.BlockSpec((C, 2 * QW + 2 * VW), lambda h, n: (rv(n), 0)),
                   pl.BlockSpec((1, VW), lambda h, n: (0, 0))],
        out_shape=[jax.ShapeDtypeStruct((T, in_w), BF16), jax.ShapeDtypeStruct((1, VW), F32)],
        scratch=[pltpu.VMEM((HB, RET_DK, RET_DV), F32)],
        args=[proj, proj, proj, proj, cosf, sins, gn_g, dm, xi, zeta, gc, dri, r, states])


CONV_CW = 128
CONV_TB = 512


def _conv_fwd(proj, kpad, bias, off_a, CC, T, rider=None):
    tb, cw = min(CONV_TB, T), CONV_CW
    hb = tb // HALO
    ca, cb = off_a // cw, (off_a + CC) // cw

    def body(a_ref, b_ref, ap_ref, bp_ref, k_ref, bias_ref, u1_ref, win):
        i = pl.program_id(0)
        keep = (i > 0).astype(F32)
        win[0:HALO, :] = ap_ref[...].astype(F32) * _sigmoid(bp_ref[...].astype(F32)) * keep
        win[HALO:, :] = a_ref[...].astype(F32) * _sigmoid(b_ref[...].astype(F32))
        acc = jnp.broadcast_to(bias_ref[...], (tb, cw))
        for w in range(CONV_WIDTH):
            acc = acc + k_ref[w:w + 1, :] * win[pl.ds(HALO - (CONV_WIDTH - 1) + w, tb), :]
        u1_ref[...] = acc

    prev = lambda i: jnp.maximum(i * hb - 1, 0)
    (u1,) = _hosted_call(
        body, rider, name="conv_fwd", grid=(T // tb, CC // cw),
        in_specs=[pl.BlockSpec((tb, cw), lambda i, c: (i, ca + c)),
                  pl.BlockSpec((tb, cw), lambda i, c: (i, cb + c)),
                  pl.BlockSpec((HALO, cw), lambda i, c: (prev(i), ca + c)),
                  pl.BlockSpec((HALO, cw), lambda i, c: (prev(i), cb + c)),
                  pl.BlockSpec((HALO, cw), lambda i, c: (0, c)),
                  pl.BlockSpec((1, cw), lambda i, c: (0, c))],
        out_specs=[pl.BlockSpec((tb, cw), lambda i, c: (i, c))],
        out_shape=[jax.ShapeDtypeStruct((T, CC), F32)],
        scratch=[pltpu.VMEM((tb + HALO, cw), F32)],
        args=[proj, proj, proj, proj, kpad, bias])
    return u1


def _conv_bwd(du1, proj, kpad, off_a, CC, T, rider=None):
    tb, cw = min(CONV_TB, T), CONV_CW
    hb = tb // HALO
    nt = T // tb
    ca, cb = off_a // cw, (off_a + CC) // cw

    def body(d_ref, dn_ref, a_ref, b_ref, ap_ref, bp_ref, k_ref, da_ref, db_ref, dk_ref, winu, wind):
        i = pl.program_id(1)
        a, b = a_ref[...].astype(F32), b_ref[...].astype(F32)
        sgb = _sigmoid(b)
        winu[0:HALO, :] = ap_ref[...].astype(F32) * _sigmoid(bp_ref[...].astype(F32)) * (i > 0).astype(F32)
        winu[HALO:, :] = a * sgb
        d = d_ref[...]
        wind[0:tb, :] = d
        wind[tb:, :] = dn_ref[...] * (i < nt - 1).astype(F32)

        @pl.when(i == 0)
        def _():
            dk_ref[...] = jnp.zeros_like(dk_ref)

        du0 = jnp.zeros((tb, cw), F32)
        for w in range(CONV_WIDTH):
            du0 = du0 + k_ref[w:w + 1, :] * wind[pl.ds(CONV_WIDTH - 1 - w, tb), :]
            dk_ref[w:w + 1, :] += _colsum(winu[pl.ds(HALO - (CONV_WIDTH - 1) + w, tb), :] * d)
        da_ref[...] = (du0 * sgb).astype(BF16)
        db_ref[...] = (du0 * a * sgb * (1.0 - sgb)).astype(BF16)

    prev = lambda i: jnp.maximum(i * hb - 1, 0)
    nxt = lambda i: jnp.minimum((i + 1) * hb, T // HALO - 1)
    return _hosted_call(
        body, rider, name="conv_bwd", grid=(CC // cw, nt),
        in_specs=[pl.BlockSpec((tb, cw), lambda c, i: (i, c)),
                  pl.BlockSpec((HALO, cw), lambda c, i: (nxt(i), c)),
                  pl.BlockSpec((tb, cw), lambda c, i: (i, ca + c)),
                  pl.BlockSpec((tb, cw), lambda c, i: (i, cb + c)),
                  pl.BlockSpec((HALO, cw), lambda c, i: (prev(i), ca + c)),
                  pl.BlockSpec((HALO, cw), lambda c, i: (prev(i), cb + c)),
                  pl.BlockSpec((HALO, cw), lambda c, i: (0, c))],
        out_specs=[pl.BlockSpec((tb, cw), lambda c, i: (i, c)),
                   pl.BlockSpec((tb, cw), lambda c, i: (i, c)),
                   pl.BlockSpec((HALO, cw), lambda c, i: (0, c))],
        out_shape=[jax.ShapeDtypeStruct((T, CC), BF16), jax.ShapeDtypeStruct((T, CC), BF16),
                   jax.ShapeDtypeStruct((HALO, CC), F32)],
        scratch=[pltpu.VMEM((tb + HALO, cw), F32), pltpu.VMEM((tb + HALO, cw), F32)],
        args=[du1, du1, proj, proj, proj, proj, kpad])


FFN1 = ('ffn1_w_gate', 'ffn1_w_up', 'ffn1_w_down')
FFN2 = ('ffn2_w_gate', 'ffn2_w_up', 'ffn2_w_down')


def _local_step(x, tgt, W, P, gdt=BF16, comm=None):
    T, D = x.shape
    G = {}
    net = _Net(comm, G)
    if comm is not None:
        W = comm.W
        first = net.gather(['ffn1_w_gate', 'ffn1_w_up'])
    (xb,) = _rows("x_to_bf16", lambda v: [v], [(x, 'r', D, 0)], [('r', D, BF16)], T=T, tb=512,
                  rider=first if comm is not None else None)
    if comm is not None:
        net.done(first)
    VW = P['ret_gn_g'].shape[1]
    H = VW // RET_DV
    QW = H * RET_DK
    CC = P['conv_b'].shape[1]
    off_glu = 2 * QW + 2 * VW
    off_gate = off_glu + 2 * CC
    ident = lambda accs: accs

    a1, b1, s1, z1 = _ffn_fwd("ffn1", xb, x, W, FFN1, net,
                              rider=net.gather(['ffn1_w_down', 'w_in'], {'w_in': (0, (3 * D) // 8, False)}),
                              rider_down=lambda: net.gather(['w_in'], {'w_in': ((3 * D) // 8, (3 * D) // 8, False)}))
    rider = net.gather(['w_in'], {'w_in': ((3 * D) // 4, D // 4, True)})
    x1, x1b, xh1, rs1 = _ln_fwd("ln1", z1, P['ln1_g'], P['ln1_b'], T, D, rider=rider)
    net.done(rider)

    rest = net.gather(['conv_k', 'w_ret_o', 'w_conv_o', 'w_out', 'ffn2_w_down'])
    (proj,) = _mm("w_in", [x1b], [W['w_in']], [[(0, 0)]], lambda accs, bias: [accs[0] + bias], [F32],
                  tm=2048, tn=0, tk=1024, extras=[(P['b_in'], 'n', 0)], i_outer=False, b3=True, rider=rest)
    net.done(rest)
    tabs = _ret_tables(H, T)
    rider = net.gather(['ffn2_w_gate', 'ffn2_w_up'])
    r, ret_in, states = _ret_fwd(proj, P['ret_gn_g'], tabs, H, T, rider=rider)
    net.done(rider)
    kpad = jnp.pad(W['conv_k'].astype(F32), ((0, HALO - CONV_WIDTH), (0, 0)))
    u1 = _conv_fwd(proj, kpad, P['conv_b'], off_glu, CC, T)

    def conv_ln(u1, g, b):
        xhat, rstd = _ln_stats(u1)
        u2 = xhat * g + b
        return [xhat, rstd, u2 * _sigmoid(u2)]

    xhc, rsc, u3 = _rows("conv_ln", conv_ln, [(u1, 'r', CC, 0), (P['conv_ln_g'], 'v', CC, 0), (P['conv_ln_b'], 'v', CC, 0)],
                         [('r', CC, F32), ('c', 1, F32), ('r', CC, BF16)], T=T, tb=512)
    (ret_out,) = _mm("ret_o", [ret_in], [W['w_ret_o']], [[(0, 0)]], ident, [F32], tm=1024, tn=1024, tk=2048)

    def epi_merge(accs, ret_out, gr, gc):
        conv_out = accs[0]
        return [conv_out, _sigmoid(gr) * ret_out + _sigmoid(gc) * conv_out]

    conv_out, merged = _mm("conv_o_merge", [u3], [W['w_conv_o']], [[(0, 0)]], epi_merge, [F32, BF16],
                           tm=512, tn=D, tk=1024, epi_rows=256,
                           extras=[(ret_out, 'mn', 0), (proj, 'mn', off_gate), (proj, 'mn', off_gate + D)])
    (z2,) = _mm("w_out", [merged], [W['w_out']], [[(0, 0)]], lambda accs, xr: [ALPHA * xr + accs[0]], [F32],
                tm=1024, tn=1024, tk=1024, extras=[(x1, 'mn', 0)])
    x2, x2b, xh2, rs2 = _ln_fwd("ln2", z2, P['ln2_g'], P['ln2_b'], T, D)
    a2, b2, s2, z3 = _ffn_fwd("ffn2", x2b, x2, W, FFN2, net)
    dz3, dz3h, g_ln3_g, g_ln3_b, loss = _ln_loss_bwd("ln3_loss", z3, P['ln3_g'], P['ln3_b'], tgt, T, D)

    S = {'ln3_g': g_ln3_g, 'ln3_b': g_ln3_b}
    dy2 = _ffn_bwd("ffn2b", dz3h, dz3, x2b, a2, b2, s2, W, FFN2, gdt, G, net, 'down')
    rider = net.to_sibling(['ffn2_w_gate', 'ffn2_w_up'])
    dz2, dz2b, S['ln2_g'], S['ln2_b'] = _ln_bwd("ln2b", dy2, xh2, rs2, P['ln2_g'], 1.0, T, D, rider=rider)
    net.done(rider)
    net.pairsum(['ffn2_w_gate', 'ffn2_w_up'])

    (G['w_out'],) = _mm("d_w_out", [merged], [dz2b], [[(0, 0)]], ident, [gdt], ta=True, tm=1024, tn=1024, tk=2048)

    def epi_dmerge(accs, ret_out, conv_out, gr, gc):
        dm_ = accs[0]
        sr, sc = _sigmoid(gr), _sigmoid(gc)
        return [dm_ * sr, dm_ * sc, dm_ * ret_out * sr * (1.0 - sr), dm_ * conv_out * sc * (1.0 - sc)]

    rider = net.to_sibling(['w_out'])
    dret_out, dconv_out, dgate_r, dgate_c = _mm(
        "d_merge", [dz2b], [W['w_out']], [[(0, 0)]], epi_dmerge, [BF16, BF16, BF16, BF16], tb=True,
        tm=512, tn=D, tk=1024, epi_rows=256, rider=rider,
        extras=[(ret_out, 'mn', 0), (conv_out, 'mn', 0), (proj, 'mn', off_gate), (proj, 'mn', off_gate + D)])
    net.done(rider)
    (G['w_ret_o'],) = _mm("d_w_ret_o", [ret_in], [dret_out], [[(0, 0)]], ident, [gdt], ta=True, tm=1024, tn=1024, tk=2048)
    (G['w_conv_o'],) = _mm("d_w_conv_o", [u3], [dconv_out], [[(0, 0)]], ident, [gdt], ta=True, tm=1024, tn=1024, tk=2048)
    rider = net.to_sibling(['w_ret_o', 'w_conv_o'])
    (dri,) = _mm("d_ret_in", [dret_out], [W['w_ret_o']], [[(0, 0)]], ident, [F32], tb=True, tm=1024, tn=1024, tk=1024,
                 rider=rider)
    net.done(rider)
    net.pairsum(['w_out', 'w_ret_o', 'w_conv_o'])
    rider = net.to_owner(['ffn2_w_gate', 'ffn2_w_up'])
    dproj, S['ret_gn_g'] = _ret_bwd(dri, r, states, proj, P['ret_gn_g'], tabs, H, T, proj.shape[1], rider=rider)
    net.done(rider)

    def epi_du2(accs, xhat, g, b):
        u2 = xhat * g + b
        return [accs[0] * _dsilu(u2, _sigmoid(u2))]

    (du2,) = _mm("d_u3", [dconv_out], [W['w_conv_o']], [[(0, 0)]], epi_du2, [F32], tb=True, tm=512, tn=CC, tk=1024, epi_rows=256,
                 extras=[(xhc, 'mn', 0), (P['conv_ln_g'], 'n', 0), (P['conv_ln_b'], 'n', 0)])

    def conv_ln_bwd(du2, xhat, rstd, g):
        du1 = _ln_bwd_math(du2, xhat, rstd, g)
        return [du1, _colsum(du2 * xhat), _colsum(du2), _colsum(du1)]

    du1, S['conv_ln_g'], S['conv_ln_b'], S['conv_b'] = _rows(
        "conv_ln_bwd", conv_ln_bwd, [(du2, 'r', CC, 0), (xhc, 'r', CC, 0), (rsc, 'r', 1, 0), (P['conv_ln_g'], 'v', CC, 0)],
        [('r', CC, F32), ('a', CC, F32), ('a', CC, F32), ('a', CC, F32)], T=T, tb=512)
    dglu_a, dglu_b, dkpad = _conv_bwd(du1, proj, kpad, off_glu, CC, T)
    G['conv_k'] = dkpad[:CONV_WIDTH].astype(gdt)

    for off, piece in ((off_glu, dglu_a), (off_glu + CC, dglu_b), (off_gate, dgate_r), (off_gate + D, dgate_c)):
        dproj = lax.dynamic_update_slice(dproj, piece, (0, off))
    IN_W = dproj.shape[1]
    rider = net.both(net.exchange(['conv_k']),
                     net.to_owner(['w_out', 'w_ret_o', 'w_conv_o']))
    G['w_in'], S['b_in'] = _mm("d_w_in", [x1b], [dproj], [[(0, 0)]], ident, [gdt], ta=True, o3=True, bsum=True,
                               tm=1024, tn=W['w_in'].shape[2], tk=2048, rider=rider)
    net.done(rider)
    cuts = [0, (3 * D) // 16, (43 * D) // 64, D]
    w_in_rows = [{'w_in': (cuts[i], cuts[i + 1] - cuts[i], i == 2)} for i in range(3)]
    rider = net.to_sibling(['w_in'])
    (dy1,) = _mm("d_x1", [dproj], [W['w_in']], [[(0, 0)]], lambda accs, dzr: [ALPHA * dzr + accs[0]], [F32], tb=True,
                 b3=True, tm=1024, tn=1024, tk=0, extras=[(dz2, 'mn', 0)], rider=rider)
    net.done(rider)
    net.pairsum(['w_in'])
    rider = net.to_owner(['w_in'], w_in_rows[0])
    dz1, dz1h, S['ln1_g'], S['ln1_b'] = _ln_bwd("ln1b", dy1, xh1, rs1, P['ln1_g'], 0.5, T, D, rider=rider)
    net.done(rider)
    grad_x = _ffn_bwd("ffn1b", dz1h, dz1, xb, a1, b1, s1, W, FFN1, gdt, G, net, 'all',
                      pre=(lambda: net.to_owner(['w_in'], w_in_rows[1]), lambda: net.to_owner(['w_in'], w_in_rows[2])))
    return loss, grad_x, G, S


def _coords():
    return lax.axis_index("x"), lax.axis_index("y"), lax.axis_index("c")


def _flip(k, x, y, c):
    return (1 - x if k & 4 else x, 1 - y if k & 2 else y, 1 - c if k & 1 else c)


def _lin(p):
    return 4 * p[0] + 2 * p[1] + p[2]


class _Rider:
    def __init__(self, ins, out_shape, rows=None, fill=None):
        nb = len(ins)
        self.rows = rows or [None] * nb
        fill = fill or [None] * nb
        self.aliases = {nb + i: w for i, w in enumerate(w for w in range(nb) if fill[w] is not None)}
        self.ins = list(ins) + [f for f in fill if f is not None]
        self.out_shape, self.results = list(out_shape), None
        self.scratch = [pltpu.SemaphoreType.DMA((8 * nb,)), pltpu.SemaphoreType.DMA((8 * nb,)),
                        pltpu.SemaphoreType.DMA((nb,))]

    def span(self, w, ref, *slot, half=None):
        rows = self.rows[w]
        if half is not None:
            first, count = rows if rows is not None else (0, self.out_shape[w].shape[1])
            rows = (first + half * (count // 2), count // 2)
        if rows is None:
            return ref.at[slot] if slot else ref
        return ref.at[(*slot, pl.ds(*rows))]

    def begin(self, step, n_steps, ins, outs, sems):
        @pl.when(step == 0)
        def _():
            self.start(ins, outs, sems)

        @pl.when(step == min(n_steps - 1, (5 * n_steps) // 8))
        def _():
            self.relay(ins, outs, sems)

        @pl.when(step == n_steps - 1)
        def _():
            self.mid(ins, outs, sems)

    def end(self, step, n_steps, ins, outs, sems):
        @pl.when(step == n_steps - 1)
        def _():
            self.finish(ins, outs, sems)

    def relay(self, ins, outs, sems):
        pass

    def mid(self, ins, outs, sems):
        pass


class _Riders:
    def __init__(self, riders):
        self.riders = list(riders)
        self.ins = [a for r in self.riders for a in r.ins]
        self.out_shape = [o for r in self.riders for o in r.out_shape]
        self.scratch = [c for r in self.riders for c in r.scratch]
        self.aliases, n_in, n_out = {}, 0, 0
        for r in self.riders:
            self.aliases.update({n_in + p: n_out + o for p, o in r.aliases.items()})
            n_in, n_out = n_in + len(r.ins), n_out + len(r.out_shape)

    def _each(self, ins, outs, sems):
        i = o = c = 0
        for r in self.riders:
            yield r, ins[i:i + len(r.ins)], outs[o:o + len(r.out_shape)], sems[c:c + len(r.scratch)]
            i, o, c = i + len(r.ins), o + len(r.out_shape), c + len(r.scratch)

    def begin(self, step, n_steps, ins, outs, sems):
        for r, i, o, c in self._each(ins, outs, sems):
            r.begin(step, n_steps, i, o, c)

    def end(self, step, n_steps, ins, outs, sems):
        for r, i, o, c in self._each(ins, outs, sems):
            r.end(step, n_steps, i, o, c)

    @property
    def results(self):
        return [x for r in self.riders for x in r.results]

    @results.setter
    def results(self, res):
        for r, _, o, _ in self._each([], list(res), []):
            r.results = o


class _GatherRider(_Rider):
    def __init__(self, blks, rows=None, fill=None):
        super().__init__(blks, [jax.ShapeDtypeStruct((N_DEV,) + b.shape, b.dtype) for b in blks], rows, fill)
        counts = [(r[1] if r is not None else b.shape[0]) for r, b in zip(self.rows, blks)]
        self.halves = [n % 32 == 0 for n in counts]

    def _copies(self, x_refs, out_refs, sems):
        nb = len(self.out_shape)
        send_sems, recv_sems, local_sems = sems
        x, y, c = _coords()
        me, sib = (x, y, c), (x, y, 1 - c)
        xn, yn, dg = _flip(4, x, y, c), _flip(2, x, y, c), _flip(6, x, y, c)
        plans = []
        for w in range(nb):
            own = self.span(w, x_refs[w])

            def copy(k, block, to, src=None, half=None, w=w):
                slot = self.span(w, out_refs[w], _lin(block), half=half)
                return pltpu.make_async_remote_copy(
                    src_ref=slot if src is None else src, dst_ref=slot, send_sem=send_sems.at[k * nb + w],
                    recv_sem=recv_sems.at[k * nb + w], device_id=to, device_id_type=MESH)

            mine = pltpu.make_async_copy(own, self.span(w, out_refs[w], _lin(me)), local_sems.at[w])
            first = [copy(0, me, sib, src=own), copy(1, me, xn, src=own), copy(2, me, yn, src=own)]
            if self.halves[w]:
                relay = [(copy(1, xn, me), [copy(3, xn, yn, half=0), copy(5, xn, sib)]),
                         (copy(2, yn, me), [copy(4, yn, xn, half=1), copy(6, yn, sib)])]
                last = [(copy(3, dg, me, half=0), []), (copy(4, dg, me, half=1), [copy(7, dg, sib)])]
            else:
                first.append(copy(3, me, dg, src=own))
                relay = [(copy(1, xn, me), [copy(5, xn, sib)]), (copy(2, yn, me), [copy(6, yn, sib)])]
                last = [(copy(3, dg, me), [copy(7, dg, sib)])]
            other = lambda p: (p[0], p[1], 1 - c)
            from_sib = [copy(0, sib, me), copy(5, other(xn), me), copy(6, other(yn), me), copy(7, other(dg), me)]
            plans.append((mine, first, relay, last, from_sib))
        return plans

    def start(self, ins, outs, sems):
        for mine, first, _, _, _ in self._copies(ins, outs, sems):
            for cp in [mine] + first:
                cp.start()

    def relay(self, ins, outs, sems):
        for _, _, relay, _, _ in self._copies(ins, outs, sems):
            for arrival, released in relay:
                arrival.wait_recv()
                for cp in released:
                    cp.start()

    def mid(self, ins, outs, sems):
        for _, _, _, last, _ in self._copies(ins, outs, sems):
            for arrival, released in last:
                arrival.wait_recv()
                for cp in released:
                    cp.start()

    def finish(self, ins, outs, sems):
        for mine, first, relay, last, from_sib in self._copies(ins, outs, sems):
            for cp in from_sib:
                cp.wait_recv()
            for cp in first + [cp for _, released in relay + last for cp in released]:
                cp.wait_send()
            mine.wait()


class _ExchangeRider(_Rider):
    def __init__(self, gs, rows=None, fill=None):
        super().__init__(gs, [jax.ShapeDtypeStruct(g.shape, g.dtype) for g in gs], rows, fill)

    def _copies(self, g_refs, out_refs, sems):
        nb = len(self.out_shape)
        send_sems, recv_sems, local_sems = sems
        x, y, c = _coords()
        me = _lin((x, y, c))

        def copy(k, w, landing):
            peer = _flip(k, x, y, c)
            src, dst = (me, _lin(peer)) if landing else (_lin(peer), me)
            return pltpu.make_async_remote_copy(
                src_ref=self.span(w, g_refs[w], src), dst_ref=self.span(w, out_refs[w], dst),
                send_sem=send_sems.at[(k - 1) * nb + w], recv_sem=recv_sems.at[(k - 1) * nb + w],
                device_id=peer, device_id_type=MESH)

        mines = [pltpu.make_async_copy(self.span(w, g_refs[w], me), self.span(w, out_refs[w], me), local_sems.at[w])
                 for w in range(nb)]
        sends = [copy(k, w, False) for w in range(nb) for k in range(1, N_DEV)]
        landings = [copy(k, w, True) for w in range(nb) for k in range(1, N_DEV)]
        return mines, sends, landings

    def start(self, ins, outs, sems):
        mines, sends, _ = self._copies(ins, outs, sems)
        for cp in mines + sends:
            cp.start()

    def finish(self, ins, outs, sems):
        mines, sends, landings = self._copies(ins, outs, sems)
        for cp in landings:
            cp.wait_recv()
        for cp in sends:
            cp.wait_send()
        for mine in mines:
            mine.wait()


class _SiblingRider(_Rider):
    def __init__(self, gs):
        super().__init__(gs, [jax.ShapeDtypeStruct((4,) + g.shape[1:], g.dtype) for g in gs])

    def _copies(self, g_refs, out_refs, sems, landing):
        nb = len(self.out_shape)
        send_sems, recv_sems, _ = sems
        x, y, c = _coords()
        whose = c if landing else 1 - c
        return [pltpu.make_async_remote_copy(
            src_ref=g_refs[w].at[2 * q + whose], dst_ref=out_refs[w].at[q], send_sem=send_sems.at[q * nb + w],
            recv_sem=recv_sems.at[q * nb + w], device_id=(x, y, 1 - c), device_id_type=MESH)
            for w in range(nb) for q in range(4)]

    def start(self, ins, outs, sems):
        for cp in self._copies(ins, outs, sems, False):
            cp.start()

    def finish(self, ins, outs, sems):
        for cp in self._copies(ins, outs, sems, True):
            cp.wait_recv()
        for cp in self._copies(ins, outs, sems, False):
            cp.wait_send()


class _ChipRider(_Rider):
    FLIPS = (4, 2, 6)

    def __init__(self, ps, rows=None, fill=None):
        super().__init__(ps, [jax.ShapeDtypeStruct(p.shape, p.dtype) for p in ps], rows, fill)

    def _copies(self, p_refs, out_refs, sems):
        nb = len(self.out_shape)
        send_sems, recv_sems, local_sems = sems
        x, y, c = _coords()
        my_chip = 2 * x + y

        def copy(j, w, landing):
            peer = _flip(self.FLIPS[j], x, y, c)
            peer_chip = 2 * peer[0] + peer[1]
            src, dst = (my_chip, peer_chip) if landing else (peer_chip, my_chip)
            return pltpu.make_async_remote_copy(
                src_ref=self.span(w, p_refs[w], src), dst_ref=self.span(w, out_refs[w], dst),
                send_sem=send_sems.at[j * nb + w], recv_sem=recv_sems.at[j * nb + w],
                device_id=peer, device_id_type=MESH)

        mines = [pltpu.make_async_copy(self.span(w, p_refs[w], my_chip), self.span(w, out_refs[w], my_chip),
                                       local_sems.at[w]) for w in range(nb)]
        sends = [copy(j, w, False) for w in range(nb) for j in range(3)]
        landings = [copy(j, w, True) for w in range(nb) for j in range(3)]
        return mines, sends, landings

    def start(self, ins, outs, sems):
        mines, sends, _ = self._copies(ins, outs, sems)
        for cp in mines + sends:
            cp.start()

    def finish(self, ins, outs, sems):
        mines, sends, landings = self._copies(ins, outs, sems)
        for cp in landings:
            cp.wait_recv()
        for cp in sends:
            cp.wait_send()
        for mine in mines:
            mine.wait()


def _pairsum(name, g, land):
    _, r, cols = g.shape
    tb = r if r % 16 else _row_tile(r, 16, max(16, (1024 * 1024) // cols))
    core = lax.axis_index("c").astype(jnp.int32).reshape(1)

    def body(core_ref, g_ref, l_ref, o_ref):
        o_ref[...] = (g_ref[...].astype(F32) + l_ref[...].astype(F32)).astype(o_ref.dtype)

    return pl.pallas_call(
        body, name=name, out_shape=jax.ShapeDtypeStruct((4, r, cols), g.dtype),
        grid_spec=pltpu.PrefetchScalarGridSpec(
            num_scalar_prefetch=1, grid=(4, r // tb),
            in_specs=[pl.BlockSpec((None, None, tb, cols), lambda q, i, core_ref: (q, core_ref[0], i, 0)),
                      pl.BlockSpec((None, tb, cols), lambda q, i, core_ref: (q, i, 0))],
            out_specs=pl.BlockSpec((None, tb, cols), lambda q, i, core_ref: (q, i, 0))),
        compiler_params=_params(("arbitrary", "arbitrary")),
    )(core, g.reshape(4, 2, r, cols), land)


def _run_rider(name, rider):
    n_in, n_out = len(rider.ins), len(rider.out_shape)

    def body(*refs):
        ride = (refs[:n_in], refs[n_in:n_in + n_out], refs[n_in + n_out:])
        rider.start(*ride)
        rider.relay(*ride)
        rider.mid(*ride)
        rider.finish(*ride)

    rider.results = pl.pallas_call(
        body, name=name, out_shape=rider.out_shape, in_specs=[ANY] * n_in, out_specs=[ANY] * n_out,
        scratch_shapes=rider.scratch, input_output_aliases=dict(rider.aliases),
        compiler_params=pltpu.CompilerParams(has_side_effects=True),
    )(*rider.ins)
    return rider.results


def _as_matrix(name, g):
    if name == 'w_in':
        return g
    if name in COL_SHARDED:
        return jnp.transpose(g, (1, 0, 2)).reshape(g.shape[1], N_DEV * g.shape[2])
    return g.reshape(N_DEV * g.shape[1], g.shape[2])


def _by_owner(name, g):
    if name == 'w_in':
        return g
    if name in COL_SHARDED:
        return jnp.transpose(g.reshape(g.shape[0], N_DEV, g.shape[1] // N_DEV), (1, 0, 2))
    return g.reshape(N_DEV, g.shape[0] // N_DEV, g.shape[1])


class _Comm:
    def __init__(self, shards):
        self.shards, self.W, self.parts, self.partial, self.sent = shards, {}, {}, {}, {}
        self.from_sibling, self.pairs = {}, {}

    def _ride(self, cls, names, srcs, part, sink):
        part = part or {}
        rider = cls(srcs, rows=[part[n][:2] if n in part else None for n in names],
                    fill=[self.partial.pop((sink, n), None) for n in names])
        rider.names, rider.sink = names, sink
        rider.unfinished = {n for n in names if n in part and not part[n][2]}
        return rider

    def gather(self, names, part=None):
        return self._ride(_GatherRider, names, [self.shards[n] for n in names], part, 'W')

    def exchange(self, names, G, part=None):
        for n in names:
            if n not in self.sent:
                self.sent[n] = _by_owner(n, G[n])
        return self._ride(_ExchangeRider, names, [self.sent[n] for n in names], part, 'parts')

    def to_sibling(self, names, G):
        for n in names:
            self.sent[n] = _by_owner(n, G[n])
        rider = _SiblingRider([self.sent[n] for n in names])
        rider.names, rider.sink, rider.unfinished = names, 'sibling', set()
        return rider

    def pairsum(self, names):
        for n in names:
            self.pairs[n] = _pairsum("pairsum_" + n, self.sent[n], self.from_sibling.pop(n))

    def to_owner(self, names, part=None):
        return self._ride(_ChipRider, names, [self.pairs[n] for n in names], part, 'parts')

    def collect(self, rider):
        for n, res in zip(rider.names, rider.results):
            if n in rider.unfinished:
                self.partial[(rider.sink, n)] = res
            elif rider.sink == 'W':
                self.W[n] = _as_matrix(n, res)
            elif rider.sink == 'sibling':
                self.from_sibling[n] = res
            else:
                self.parts[n] = res


def _adamw_math(p_ref, w_ref, m_ref, v_ref, g_ref, d_ref, nm_ref, nv_ref):
    c1 = 1.0 - ADAM_B1 ** ADAM_STEP
    c2 = 1.0 - ADAM_B2 ** ADAM_STEP
    g = p_ref[0].astype(F32)
    for s in range(1, p_ref.shape[0]):
        g = g + p_ref[s].astype(F32)
    nm = ADAM_B1 * m_ref[...] + (1.0 - ADAM_B1) * g
    nv = ADAM_B2 * v_ref[...] + (1.0 - ADAM_B2) * (g * g)
    g_ref[...] = g
    nm_ref[...] = nm
    nv_ref[...] = nv
    d_ref[...] = -ADAM_LR * ((nm / c1) / (jnp.sqrt(nv / c2) + ADAM_EPS) + ADAM_WD * w_ref[...])


def _adamw_vectors(parts, ws, ms, vs, loss_parts):
    k = len(ws)

    def body(*refs):
        for i in range(k):
            _adamw_math(refs[i], refs[k + i], refs[2 * k + i], refs[3 * k + i], *refs[4 * k + 1 + 4 * i:4 * k + 5 + 4 * i])
        lp, lo = refs[4 * k], refs[8 * k + 1]
        lo[...] = functools.reduce(jnp.add, [lp[s] for s in range(lp.shape[0])])

    return pl.pallas_call(
        body, name="adamw_vectors",
        out_shape=[jax.ShapeDtypeStruct(w.shape, F32) for w in ws for _ in range(4)] + [jax.ShapeDtypeStruct((1, 128), F32)],
        compiler_params=_params(),
    )(*parts, *ws, *ms, *vs, loss_parts)


def _adamw(name, parts, w, m, v, tb):
    n, R, Wd = parts.shape
    assert R % tb == 0
    body = functools.partial(_adamw_math)

    row = pl.BlockSpec((tb, Wd), lambda i: (i, 0))
    return pl.pallas_call(
        body, name=name, grid=(R // tb,),
        in_specs=[pl.BlockSpec((n, tb, Wd), lambda i: (0, i, 0)), row, row, row],
        out_specs=[row, row, row, row], out_shape=[jax.ShapeDtypeStruct((R, Wd), F32)] * 4,
        compiler_params=_params(("arbitrary",)),
    )(parts, w, m, v)


def _row_tile(R, unit, cap):
    best = unit
    for t in range(unit, cap + 1, unit):
        if R % t == 0:
            best = t
    return best


def kernel(x, ffn1_w_gate, ffn1_w_up, ffn1_w_down, ln1_g, ln1_b, w_in, b_in, ret_gn_g, conv_k, conv_b, conv_ln_g, conv_ln_b, w_ret_o, w_conv_o, w_out, ln2_g, ln2_b, ffn2_w_gate, ffn2_w_up, ffn2_w_down, ln3_g, ln3_b, loss_target, m_ffn1_w_gate, m_ffn1_w_up, m_ffn1_w_down, m_ln1_g, m_ln1_b, m_w_in, m_b_in, m_ret_gn_g, m_conv_k, m_conv_b, m_conv_ln_g, m_conv_ln_b, m_w_ret_o, m_w_conv_o, m_w_out, m_ln2_g, m_ln2_b, m_ffn2_w_gate, m_ffn2_w_up, m_ffn2_w_down, m_ln3_g, m_ln3_b, v_ffn1_w_gate, v_ffn1_w_up, v_ffn1_w_down, v_ln1_g, v_ln1_b, v_w_in, v_b_in, v_ret_gn_g, v_conv_k, v_conv_b, v_conv_ln_g, v_conv_ln_b, v_w_ret_o, v_w_conv_o, v_w_out, v_ln2_g, v_ln2_b, v_ffn2_w_gate, v_ffn2_w_up, v_ffn2_w_down, v_ln3_g, v_ln3_b):
    given = dict(locals())
    wts = {n: given[n] for n in WEIGHTS}
    mom = {n: given['m_' + n] for n in WEIGHTS}
    var = {n: given['v_' + n] for n in WEIGHTS}

    def shard2d(a):
        return a.reshape(a.shape[-3] * a.shape[-2] if a.ndim == 4 else a.shape[-2], a.shape[-1])

    comm = _Comm({n: shard2d(wts[n]).astype(BF16) for n in BIG})
    P = {n: wts[n].reshape(1, -1) for n in SMALL}
    loss, grad_x, _, S = _local_step(x[0], loss_target[0], None, P, comm=comm)

    parts = comm.parts
    res = {}
    for n in BIG:
        rows, cols = parts[n].shape[1:]
        tb = rows if rows % 16 else _row_tile(rows, 16, max(16, (256 * 1024) // cols))
        res[n] = _adamw("adamw_" + n, parts[n], shard2d(wts[n]), shard2d(mom[n]), shard2d(var[n]), tb)

    vec_parts = _run_rider("gather_vector_grads", _GatherRider([S[n] for n in SMALL] + [loss]))
    vec = _adamw_vectors(vec_parts[:-1], [P[n] for n in SMALL], [mom[n].reshape(1, -1) for n in SMALL],
                         [var[n].reshape(1, -1) for n in SMALL], vec_parts[-1])
    for i, n in enumerate(SMALL):
        res[n] = vec[4 * i:4 * i + 4]

    outs = [vec[-1][0, 0], grad_x[None]]
    for k in range(4):
        for n in WEIGHTS:
            outs.append(res[n][k].reshape(wts[n].shape))
    return tuple(outs)
```

```python
import functools
import math

import jax
import jax.numpy as jnp
from jax import lax
from jax.experimental import pallas as pl
from jax.experimental.pallas import tpu as pltpu

F32 = jnp.float32
BF16 = jnp.bfloat16

N_DEV = 8
LN_EPS = 1e-5
ALPHA = 2.0 ** 0.25
RET_DK = 128
RET_DV = 256
RET_CHUNK = 256
ROPE_BASE = 10000.0
CONV_WIDTH = 31
HALO = 32
ADAM_LR, ADAM_B1, ADAM_B2, ADAM_EPS, ADAM_WD, ADAM_STEP = 0.001, 0.9, 0.999, 1e-08, 0.01, 10
VMEM_LIMIT = 52 * 1024 * 1024
MESH = pl.DeviceIdType.MESH
ANY = pl.BlockSpec(memory_space=pl.ANY)

BIG = ['ffn1_w_gate', 'ffn1_w_up', 'ffn1_w_down', 'w_in', 'w_ret_o', 'w_conv_o', 'w_out',
       'ffn2_w_gate', 'ffn2_w_up', 'ffn2_w_down', 'conv_k']
COL_SHARDED = {'ffn1_w_gate', 'ffn1_w_up', 'w_in', 'ffn2_w_gate', 'ffn2_w_up', 'conv_k'}
SMALL = ['ln1_g', 'ln1_b', 'b_in', 'ret_gn_g', 'conv_b', 'conv_ln_g', 'conv_ln_b', 'ln2_g', 'ln2_b', 'ln3_g', 'ln3_b']
WEIGHTS = ['ffn1_w_gate', 'ffn1_w_up', 'ffn1_w_down', 'ln1_g', 'ln1_b', 'w_in', 'b_in', 'ret_gn_g', 'conv_k', 'conv_b',
           'conv_ln_g', 'conv_ln_b', 'w_ret_o', 'w_conv_o', 'w_out', 'ln2_g', 'ln2_b', 'ffn2_w_gate', 'ffn2_w_up',
           'ffn2_w_down', 'ln3_g', 'ln3_b']


def _params(sem=None):
    return pltpu.CompilerParams(dimension_semantics=sem, vmem_limit_bytes=VMEM_LIMIT)


def _sigmoid(x):
    return jax.nn.sigmoid(x)


def _dsilu(x, sg):
    return sg * (1.0 + x * (1.0 - sg))


def _fit(dim, want):
    if dim <= want:
        return dim
    return max(t for t in range(128, want + 1, 128) if dim % t == 0)


def _dot(a, b, ta=False, tb=False):
    dn = (((0,) if ta else (1,), (1,) if tb else (0,)), ((), ()))
    return lax.dot_general(a, b, dn, preferred_element_type=F32)


def _mm(name, As, Bs, prods, epi, out_dtypes, *, ta=False, tb=False, tm, tn, tk, extras=(), i_outer=True,
        b3=False, o3=False, rider=None, bsum=False, epi_rows=0):
    a0, b0 = As[0], Bs[0]
    M, K = (a0.shape[1], a0.shape[0]) if ta else a0.shape
    if b3:
        S, rows, cs = b0.shape
        N = rows if tb else S * cs
        assert K == (S * cs if tb else rows)
        tn, tk = (tn, cs) if tb else (cs, tk)
    else:
        N = b0.shape[0] if tb else b0.shape[1]
    tm, tn, tk = _fit(M, tm), _fit(N, tn), _fit(K, tk)
    assert M % tm == 0 and N % tn == 0 and K % tk == 0, (name, M, N, K, tm, tn, tk)
    gi, gj, gk = M // tm, N // tn, K // tk
    grid = (gi, gj, gk) if i_outer else (gj, gi, gk)

    def ij(g0, g1):
        return (g0, g1) if i_outer else (g1, g0)

    def amap(g0, g1, k):
        i, _ = ij(g0, g1)
        return (k, i) if ta else (i, k)

    def bmap(g0, g1, k):
        _, j = ij(g0, g1)
        return (j, k) if tb else (k, j)

    def bmap3(g0, g1, k):
        _, j = ij(g0, g1)
        return (k, j, 0) if tb else (j, k, 0)

    in_specs = [pl.BlockSpec((tk, tm) if ta else (tm, tk), amap) for _ in As]
    if b3:
        in_specs += [pl.BlockSpec((None, tn, tk) if tb else (None, tk, tn), bmap3) for _ in Bs]
    else:
        in_specs += [pl.BlockSpec((tn, tk) if tb else (tk, tn), bmap) for _ in Bs]
    args = list(As) + list(Bs)
    for arr, kind, coloff in extras:
        assert coloff % tn == 0
        off = coloff // tn
        if kind == 'mn':
            in_specs.append(pl.BlockSpec((tm, tn), lambda g0, g1, k, off=off: (ij(g0, g1)[0], ij(g0, g1)[1] + off)))
        else:
            in_specs.append(pl.BlockSpec((1, tn), lambda g0, g1, k, off=off: (0, ij(g0, g1)[1] + off)))
        args.append(arr)
    if o3:
        out_shape = [jax.ShapeDtypeStruct((gj, M, tn), dt) for dt in out_dtypes]
        out_specs = [pl.BlockSpec((None, tm, tn), lambda g0, g1, k: (ij(g0, g1)[1], ij(g0, g1)[0], 0))
                     for _ in out_dtypes]
    else:
        out_shape = [jax.ShapeDtypeStruct((M, N), dt) for dt in out_dtypes]
        out_specs = [pl.BlockSpec((tm, tn), lambda g0, g1, k: ij(g0, g1)) for _ in out_dtypes]
    if bsum:
        assert gi == 1 and not tb and not b3
        out_shape.append(jax.ShapeDtypeStruct((1, N), F32))
        out_specs.append(pl.BlockSpec((1, tn), lambda g0, g1, k: (0, ij(g0, g1)[1])))
    n_a, n_b, n_e, n_o = len(As), len(Bs), len(extras), len(out_shape)
    n_p = len(prods) if gk > 1 else 0
    scratch = [pltpu.VMEM((tm, tn), F32) for _ in range(n_p)]
    if rider is not None:
        in_specs, out_specs = in_specs + [ANY] * len(rider.ins), out_specs + [ANY] * len(rider.out_shape)
        args, out_shape, scratch = args + rider.ins, out_shape + rider.out_shape, scratch + rider.scratch
    n_in, n_out = len(args), len(out_shape)

    def body(*refs):
        a_refs = refs[:n_a]
        b_refs = refs[n_a:n_a + n_b]
        e_refs = refs[n_a + n_b:n_a + n_b + n_e]
        o_refs = refs[n_in:n_in + n_o]
        acc_refs = refs[n_in + n_out:n_in + n_out + n_p]
        k = pl.program_id(2)
        if rider is not None:
            step = (pl.program_id(0) * grid[1] + pl.program_id(1)) * gk + k
            ride = (step, grid[0] * grid[1] * gk, refs[n_a + n_b + n_e:n_in], refs[n_in + n_o:n_in + n_out],
                    refs[n_in + n_out + n_p:])
            rider.begin(*ride)

        def finish(accs, rows=slice(None)):
            ex = [(e[rows, :] if kind == 'mn' else e[...]).astype(F32) for e, (_, kind, _) in zip(e_refs, extras)]
            for o, r in zip(o_refs, epi(accs, *ex)):
                o[rows, :] = r.astype(o.dtype)

        if bsum:
            @pl.when(k == 0)
            def _():
                o_refs[-1][...] = jnp.zeros_like(o_refs[-1])

            o_refs[-1][...] += _colsum(b_refs[0][...].astype(F32))

        if gk == 1:
            sub = tm if ta or not epi_rows else _fit(tm, epi_rows)
            for r0 in range(0, tm, sub):
                rows = slice(None) if ta else slice(r0, r0 + sub)
                finish([functools.reduce(jnp.add, [_dot(a_refs[ai][...] if ta else a_refs[ai][rows, :],
                                                        b_refs[bi][...], ta, tb) for ai, bi in terms])
                        for terms in prods], rows)
        else:
            @pl.when(k == 0)
            def _():
                for acc in acc_refs:
                    acc[...] = jnp.zeros_like(acc)

            for p, terms in enumerate(prods):
                for ai, bi in terms:
                    acc_refs[p][...] += _dot(a_refs[ai][...], b_refs[bi][...], ta, tb)

            @pl.when(k == gk - 1)
            def _():
                finish([acc[...] for acc in acc_refs])

        if rider is not None:
            rider.end(*ride)

    aliases = {} if rider is None else {n_a + n_b + n_e + p: n_o + o for p, o in rider.aliases.items()}
    res = pl.pallas_call(
        body, name=name, grid=grid, in_specs=in_specs, out_specs=out_specs, out_shape=out_shape,
        scratch_shapes=scratch, input_output_aliases=aliases,
        compiler_params=_params(("arbitrary", "arbitrary", "arbitrary")),
    )(*args)
    if rider is not None:
        rider.results = res[n_o:]
    return res[:n_o]


def _rows(name, fn, ins, outs, *, T, tb, rider=None):
    tb = min(tb, T)
    assert T % tb == 0
    in_specs, args = [], []
    for arr, kind, width, cb in ins:
        if kind == 'r':
            in_specs.append(pl.BlockSpec((tb, width), lambda i, _, cb=cb: (i, cb)))
        else:
            in_specs.append(pl.BlockSpec((1, width), lambda i, _, cb=cb: (0, cb)))
        args.append(arr)
    out_shape, out_specs = [], []
    for kind, width, dtype in outs:
        if kind == 'r':
            out_shape.append(jax.ShapeDtypeStruct((T, width), dtype))
            out_specs.append(pl.BlockSpec((tb, width), lambda i, _: (i, 0)))
        elif kind == 'c':
            out_shape.append(jax.ShapeDtypeStruct((T, 1), dtype))
            out_specs.append(pl.BlockSpec((tb, 1), lambda i, _: (i, 0)))
        else:
            out_shape.append(jax.ShapeDtypeStruct((1, width), F32))
            out_specs.append(pl.BlockSpec((1, width), lambda i, _: (0, 0)))
    n_in = len(ins)

    def body(*refs):
        i = pl.program_id(0)
        vals = fn(*[r[...] for r in refs[:n_in]])
        for (kind, _, _), o, v in zip(outs, refs[n_in:], vals):
            if kind == 'a':
                @pl.when(i == 0)
                def _(o=o):
                    o[...] = jnp.zeros_like(o)

                o[...] += v
            else:
                o[...] = v.astype(o.dtype)

    return _hosted_call(body, rider, name=name, grid=(T // tb, 1), in_specs=in_specs, out_specs=out_specs,
                        out_shape=out_shape, scratch=[], args=args)


def _colsum(v):
    return jnp.sum(v, axis=0, keepdims=True)


def _ln_stats(z):
    mu = jnp.mean(z, axis=-1, keepdims=True)
    d = z - mu
    var = jnp.mean(d * d, axis=-1, keepdims=True)
    rstd = lax.rsqrt(var + LN_EPS)
    return d * rstd, rstd


def _ln_bwd_math(dy, xhat, rstd, g):
    dxh = dy * g
    m1 = jnp.mean(dxh, axis=-1, keepdims=True)
    m2 = jnp.mean(dxh * xhat, axis=-1, keepdims=True)
    return rstd * (dxh - m1 - xhat * m2)


def _ln_fwd(name, z, g, b, T, D, rider=None):
    def fn(z, g, b):
        xhat, rstd = _ln_stats(z)
        y = xhat * g + b
        return [y, y, xhat, rstd]

    return _rows(name, fn, [(z, 'r', D, 0), (g, 'v', D, 0), (b, 'v', D, 0)],
                 [('r', D, F32), ('r', D, BF16), ('r', D, F32), ('c', 1, F32)], T=T, tb=512, rider=rider)


def _ln_bwd(name, dy, xhat, rstd, g, scale, T, D, rider=None):
    def fn(dy, xhat, rstd, g):
        dz = _ln_bwd_math(dy, xhat, rstd, g)
        return [dz, dz * scale, _colsum(dy * xhat), _colsum(dy)]

    return _rows(name, fn, [(dy, 'r', D, 0), (xhat, 'r', D, 0), (rstd, 'r', 1, 0), (g, 'v', D, 0)],
                 [('r', D, F32), ('r', D, BF16), ('a', D, F32), ('a', D, F32)], T=T, tb=512, rider=rider)


def _ln_loss_bwd(name, z, g, b, tgt, T, D):
    def fn(z, g, b, tgt):
        xhat, rstd = _ln_stats(z)
        err = xhat * g + b - tgt
        row_loss = 0.5 * jnp.mean(err * err, axis=-1, keepdims=True)
        loss = jnp.broadcast_to(jnp.sum(row_loss, axis=0, keepdims=True), (1, 128))
        dy = err * (1.0 / D)
        dz = _ln_bwd_math(dy, xhat, rstd, g)
        return [dz, dz * 0.5, _colsum(dy * xhat), _colsum(dy), loss]

    return _rows(name, fn, [(z, 'r', D, 0), (g, 'v', D, 0), (b, 'v', D, 0), (tgt, 'r', D, 0)],
                 [('r', D, F32), ('r', D, BF16), ('a', D, F32), ('a', D, F32), ('a', 128, F32)], T=T, tb=512)


class _Net:
    def __init__(self, comm, G):
        self.comm, self.G = comm, G

    def gather(self, names, part=None):
        return self.comm.gather(names, part) if self.comm else None

    def exchange(self, names, part=None):
        return self.comm.exchange(names, self.G, part) if self.comm else None

    def to_sibling(self, names):
        return self.comm.to_sibling(names, self.G) if self.comm else None

    def pairsum(self, names):
        if self.comm:
            self.comm.pairsum(names)

    def to_owner(self, names, part=None):
        return self.comm.to_owner(names, part) if self.comm else None

    def done(self, rider):
        if rider is not None:
            for one in getattr(rider, 'riders', [rider]):
                self.comm.collect(one)

    def both(self, *riders):
        return _Riders(riders) if self.comm else None


def _ffn_fwd(tag, xb, x, W, names, net, rider=None, rider_down=None):
    def epi_gu(accs):
        a, b = accs
        return [a, b, a * _sigmoid(a) * b]

    ng, nu, nd = names
    a, b, s = _mm(tag + "_gate_up", [xb], [W[ng], W[nu]], [[(0, 0)], [(0, 1)]], epi_gu, [BF16, BF16, BF16],
                  tm=1024, tn=1408, tk=1024, rider=rider, epi_rows=256)
    net.done(rider)

    def epi_down(accs, xres):
        return [ALPHA * xres + 0.5 * accs[0]]

    rider_down = rider_down() if rider_down else None
    (z,) = _mm(tag + "_down", [s], [W[nd]], [[(0, 0)]], epi_down, [F32], tm=1024, tn=1024, tk=1408,
               extras=[(x, 'mn', 0)], rider=rider_down)
    net.done(rider_down)
    return a, b, s, z


def _ffn_bwd(tag, dzh, dz, xb, a, b, s, W, names, gdt, G, net, ride, pre=(None, None)):
    ng, nu, nd = names

    def epi_ds(accs, a, b):
        ds = accs[0]
        sg = _sigmoid(a)
        return [ds * b * _dsilu(a, sg), ds * a * sg]

    rider = pre[0]() if pre[0] else None
    da, db = _mm(tag + "_ds", [dzh], [W[nd]], [[(0, 0)]], epi_ds, [BF16, BF16], tb=True, tm=1024, tn=1408, tk=1024, epi_rows=256,
                 extras=[(a, 'mn', 0), (b, 'mn', 0)], rider=rider)
    net.done(rider)
    ident = lambda accs: accs
    rider = pre[1]() if pre[1] else None
    (G[nd],) = _mm(tag + "_dwd", [s], [dzh], [[(0, 0)]], ident, [gdt], ta=True, tm=1408, tn=1024, tk=1024,
                   rider=rider)
    net.done(rider)
    if ride == 'all':
        rider = net.to_sibling([nd])
        (G[ng],) = _mm(tag + "_dwg", [xb], [da], [[(0, 0)]], ident, [gdt], ta=True, tm=1024, tn=1408, tk=1024,
                       rider=rider)
        net.done(rider)
        net.pairsum([nd])
        rider = net.to_owner([nd])
        (G[nu],) = _mm(tag + "_dwu", [xb], [db], [[(0, 0)]], ident, [gdt], ta=True, tm=1024, tn=1408, tk=1024,
                       rider=rider)
        net.done(rider)
        rider = net.to_sibling([ng, nu])
        if rider is not None:
            _run_rider(tag + "_to_sibling", rider)
            net.done(rider)
        net.pairsum([ng, nu])
        rider = net.to_owner([ng, nu])
    else:
        rider = net.to_sibling([nd]) if ride else None
        G[ng], G[nu] = _mm(tag + "_dwgu", [xb], [da, db], [[(0, 0)], [(0, 1)]], ident, [gdt, gdt], ta=True,
                           tm=1024, tn=1408, tk=1024, rider=rider)
        net.done(rider)
        if ride:
            net.pairsum([nd])
        rider = net.to_owner([nd]) if ride else None

    def epi_dx(accs, dzres):
        return [ALPHA * dzres + accs[0]]

    (dx,) = _mm(tag + "_dx", [da, db], [W[ng], W[nu]], [[(0, 0), (1, 1)]], epi_dx, [F32], tb=True,
                tm=1024, tn=1024, tk=1408, extras=[(dz, 'mn', 0)], rider=rider)
    net.done(rider)
    return dx


def _ret_tables(H, T):
    C = RET_CHUNK
    log_g = jnp.log(1.0 - jnp.exp2(-5.0 - jnp.arange(H, dtype=F32)))
    idx = jnp.arange(C, dtype=F32)
    diff = idx[:, None] - idx[None, :]
    dm = jnp.where(diff[None] >= 0, jnp.exp(jnp.maximum(diff, 0.0)[None] * log_g[:, None, None]), 0.0)
    xi = jnp.broadcast_to(jnp.exp((idx[None, :] + 1.0) * log_g[:, None])[:, :, None], (H, C, RET_DV))
    zeta = jnp.broadcast_to(jnp.exp((C - 1.0 - idx)[None, :] * log_g[:, None])[:, :, None], (H, C, RET_DK))
    gc = jnp.broadcast_to(jnp.exp(C * log_g)[:, None, None], (H, 1, RET_DV))
    half = RET_DK // 2
    freqs = ROPE_BASE ** (-jnp.arange(half, dtype=F32) / half)
    ang = jnp.arange(T, dtype=F32)[:, None] * freqs[None, :]
    cos, sin = jnp.cos(ang), jnp.sin(ang)
    cosf = jnp.concatenate([cos, cos], axis=1)
    sins = jnp.concatenate([-sin, sin], axis=1)
    return dm, xi, zeta, gc, cosf, sins


def _rot(x, cosf, sins):
    return x * cosf + pltpu.roll(x, RET_DK // 2, 1) * sins


def _rot_bwd(dy, cosf, sins):
    return dy * cosf + pltpu.roll(dy * sins, RET_DK // 2, 1)


RET_HB = 8


def _ret_specs(H, HB, rev, NC):
    C, G = RET_CHUNK, H // HB
    nn = (lambda n: NC - 1 - n) if rev else (lambda n: n)
    return [
        pl.BlockSpec((C, HB * RET_DK), lambda h, n: (nn(n), h)),
        pl.BlockSpec((C, HB * RET_DK), lambda h, n: (nn(n), G + h)),
        pl.BlockSpec((C, HB * RET_DV), lambda h, n: (nn(n), G + h)),
        pl.BlockSpec((C, HB * RET_DV), lambda h, n: (nn(n), 2 * G + h)),
        pl.BlockSpec((C, RET_DK), lambda h, n: (nn(n), 0)),
        pl.BlockSpec((C, RET_DK), lambda h, n: (nn(n), 0)),
        pl.BlockSpec((1, HB * RET_DV), lambda h, n: (0, h)),
        pl.BlockSpec((HB, C, C), lambda h, n: (h, 0, 0)),
        pl.BlockSpec((HB, C, RET_DV), lambda h, n: (h, 0, 0)),
        pl.BlockSpec((HB, C, RET_DK), lambda h, n: (h, 0, 0)),
        pl.BlockSpec((HB, 1, RET_DV), lambda h, n: (h, 0, 0)),
    ]


def _ret_fwd(proj, gn_g, tabs, H, T, rider=None):
    C, NC = RET_CHUNK, T // RET_CHUNK
    HB = min(RET_HB, H)
    dm, xi, zeta, gc, cosf, sins = tabs
    scale = RET_DK ** -0.5

    def body(q_ref, k_ref, v_ref, g_ref, cos_ref, sin_ref, gn_ref, dm_ref, xi_ref, zt_ref, gc_ref,
             r_ref, ri_ref, st_ref, state):
        @pl.when(pl.program_id(1) == 0)
        def _():
            state[...] = jnp.zeros_like(state)

        cs, sn = cos_ref[...], sin_ref[...]
        hs = range(HB)
        qk = [slice(h * RET_DK, (h + 1) * RET_DK) for h in hs]
        vv = [slice(h * RET_DV, (h + 1) * RET_DV) for h in hs]
        kr = [_rot(k_ref[:, qk[h]].astype(F32), cs, sn) for h in hs]
        qb = [(_rot(q_ref[:, qk[h]].astype(F32), cs, sn) * scale).astype(BF16) for h in hs]
        kb = [kr[h].astype(BF16) for h in hs]
        kzb = [(kr[h] * zt_ref[h]).astype(BF16) for h in hs]
        vb = [v_ref[:, vv[h]].astype(BF16) for h in hs]
        st = [state[h] for h in hs]
        stb = [st[h].astype(BF16) for h in hs]
        sb = [(_dot(qb[h], kb[h], tb=True) * dm_ref[h]).astype(BF16) for h in hs]
        cross = [_dot(qb[h], stb[h]) for h in hs]
        kv = [_dot(kzb[h], vb[h], ta=True) for h in hs]
        intra = [_dot(sb[h], vb[h]) for h in hs]
        for h in hs:
            st_ref[h] = stb[h]
            state[h] = gc_ref[h] * st[h] + kv[h]
        for h in hs:
            r = intra[h] + cross[h] * xi_ref[h]
            rhat, _ = _ln_stats(r)
            g = g_ref[:, vv[h]].astype(F32)
            r_ref[:, vv[h]] = r
            ri_ref[:, vv[h]] = (g * _sigmoid(g) * (rhat * gn_ref[:, vv[h]])).astype(BF16)

    VW = H * RET_DV
    return _hosted_call(
        body, rider, name="ret_fwd", grid=(H // HB, NC), in_specs=_ret_specs(H, HB, False, NC),
        out_specs=[pl.BlockSpec((C, HB * RET_DV), lambda h, n: (n, h)),
                   pl.BlockSpec((C, HB * RET_DV), lambda h, n: (n, h)),
                   pl.BlockSpec((HB, None, RET_DK, RET_DV), lambda h, n: (h, n, 0, 0))],
        out_shape=[jax.ShapeDtypeStruct((T, VW), F32), jax.ShapeDtypeStruct((T, VW), BF16),
                   jax.ShapeDtypeStruct((H, NC, RET_DK, RET_DV), BF16)],
        scratch=[pltpu.VMEM((HB, RET_DK, RET_DV), F32)],
        args=[proj, proj, proj, proj, cosf, sins, gn_g, dm, xi, zeta, gc])


def _hosted_call(body, rider, *, name, grid, in_specs, out_specs, out_shape, scratch, args):
    n_in, n_out, n_scr = len(args), len(out_shape), len(scratch)
    if rider is None:
        hosted = body
    else:
        n_ri, n_ro = len(rider.ins), len(rider.out_shape)
        in_specs, out_specs = in_specs + [ANY] * n_ri, out_specs + [ANY] * n_ro
        args, out_shape, scratch = args + rider.ins, out_shape + rider.out_shape, scratch + rider.scratch

        def hosted(*refs):
            o0, s0 = n_in + n_ri, n_in + n_ri + n_out + n_ro
            step = pl.program_id(0) * grid[1] + pl.program_id(1)
            ride = (step, grid[0] * grid[1], refs[n_in:o0], refs[o0 + n_out:s0], refs[s0 + n_scr:])
            rider.begin(*ride)
            body(*refs[:n_in], *refs[o0:o0 + n_out], *refs[s0:s0 + n_scr])
            rider.end(*ride)

    aliases = {} if rider is None else {n_in + p: n_out + o for p, o in rider.aliases.items()}
    res = pl.pallas_call(
        hosted, name=name, grid=grid, in_specs=in_specs, out_specs=out_specs, out_shape=out_shape,
        scratch_shapes=scratch, input_output_aliases=aliases, compiler_params=_params(("arbitrary", "arbitrary")),
    )(*args)
    if rider is not None:
        rider.results = res[n_out:]
    return res[:n_out]


def _ret_bwd(dri, r, states, proj, gn_g, tabs, H, T, in_w, rider=None):
    C, NC = RET_CHUNK, T // RET_CHUNK
    HB = min(RET_HB, H)
    dm, xi, zeta, gc, cosf, sins = tabs
    scale = RET_DK ** -0.5

    def body(q_ref, k_ref, v_ref, g_ref, cos_ref, sin_ref, gn_ref, dm_ref, xi_ref, zt_ref, gc_ref,
             dri_ref, r_ref, st_ref, dp_ref, dgn_ref, dstate):
        @pl.when(pl.program_id(1) == 0)
        def _():
            dstate[...] = jnp.zeros_like(dstate)
            dgn_ref[...] = jnp.zeros_like(dgn_ref)

        cs, sn = cos_ref[...], sin_ref[...]
        hs = range(HB)
        qk = [slice(h * RET_DK, (h + 1) * RET_DK) for h in hs]
        vv = [slice(h * RET_DV, (h + 1) * RET_DV) for h in hs]
        qr = [_rot(q_ref[:, qk[h]].astype(F32), cs, sn) * scale for h in hs]
        kr = [_rot(k_ref[:, qk[h]].astype(F32), cs, sn) for h in hs]
        qb = [qr[h].astype(BF16) for h in hs]
        kb = [kr[h].astype(BF16) for h in hs]
        vb = [v_ref[:, vv[h]].astype(BF16) for h in hs]
        qxb = [(qr[h] * xi_ref[h, :, :RET_DK]).astype(BF16) for h in hs]
        kzb = [(kr[h] * zt_ref[h]).astype(BF16) for h in hs]
        drb = []
        for h in hs:
            rhat, rstd = _ln_stats(r_ref[:, vv[h]])
            g, gn, dpre = g_ref[:, vv[h]].astype(F32), gn_ref[:, vv[h]], dri_ref[:, vv[h]]
            sg = _sigmoid(g)
            dp_ref[:, 2 * QW + VW + h * RET_DV:2 * QW + VW + (h + 1) * RET_DV] = (
                dpre * (rhat * gn) * _dsilu(g, sg)).astype(BF16)
            drn = dpre * (g * sg)
            dgn_ref[:, vv[h]] += _colsum(drn * rhat)
            drb.append(_ln_bwd_math(drn, rhat, rstd, gn).astype(BF16))
        ds1 = [dstate[h] for h in hs]
        ds1b = [ds1[h].astype(BF16) for h in hs]
        sb = [(_dot(qb[h], kb[h], tb=True) * dm_ref[h]).astype(BF16) for h in hs]
        dsb = [(_dot(drb[h], vb[h], tb=True) * dm_ref[h]).astype(BF16) for h in hs]
        dq_x = [_dot(drb[h], st_ref[h], tb=True) for h in hs]
        dk_x = [_dot(vb[h], ds1b[h], tb=True) for h in hs]
        dv_x = [_dot(kzb[h], ds1b[h]) for h in hs]
        dst = [_dot(qxb[h], drb[h], ta=True) for h in hs]
        for h in hs:
            dstate[h] = gc_ref[h] * ds1[h] + dst[h]
        dv_i = [_dot(sb[h], drb[h], ta=True) for h in hs]
        dq_i = [_dot(dsb[h], kb[h]) for h in hs]
        dk_i = [_dot(dsb[h], qb[h], ta=True) for h in hs]
        for h in hs:
            dp_ref[:, 2 * QW + h * RET_DV:2 * QW + (h + 1) * RET_DV] = (dv_i[h] + dv_x[h]).astype(BF16)
            dq = dq_i[h] + dq_x[h] * xi_ref[h, :, :RET_DK]
            dk = dk_i[h] + dk_x[h] * zt_ref[h]
            dp_ref[:, qk[h]] = _rot_bwd(dq * scale, cs, sn).astype(BF16)
            dp_ref[:, QW + h * RET_DK:QW + (h + 1) * RET_DK] = _rot_bwd(dk, cs, sn).astype(BF16)

    VW, QW = H * RET_DV, H * RET_DK
    rv = lambda n: NC - 1 - n
    in_specs = _ret_specs(H, HB, True, NC) + [
        pl.BlockSpec((C, HB * RET_DV), lambda h, n: (rv(n), h)),
        pl.BlockSpec((C, HB * RET_DV), lambda h, n: (rv(n), h)),
        pl.BlockSpec((HB, None, RET_DK, RET_DV), lambda h, n: (h, rv(n), 0, 0)),
    ]
    assert HB == H
    return _hosted_call(
        body, rider, name="ret_bwd", grid=(1, NC), in_specs=in_specs,
        out_specs=[pl.BlockSpec((C, 2 * QW + 2 * VW), lambda h, n: (rv(n), 0)),
                   pl.BlockSpec((1, VW), lambda h, n: (0, 0))],
        out_shape=[jax.ShapeDtypeStruct((T, in_w), BF16), jax.ShapeDtypeStruct((1, VW), F32)],
        scratch=[pltpu.VMEM((HB, RET_DK, RET_DV), F32)],
        args=[proj, proj, proj, proj, cosf, sins, gn_g, dm, xi, zeta, gc, dri, r, states])


CONV_CW = 128
CONV_TB = 512


def _conv_fwd(proj, kpad, bias, off_a, CC, T, rider=None):
    tb, cw = min(CONV_TB, T), CONV_CW
    hb = tb // HALO
    ca, cb = off_a // cw, (off_a + CC) // cw

    def body(a_ref, b_ref, ap_ref, bp_ref, k_ref, bias_ref, u1_ref, win):
        i = pl.program_id(0)
        keep = (i > 0).astype(F32)
        win[0:HALO, :] = ap_ref[...].astype(F32) * _sigmoid(bp_ref[...].astype(F32)) * keep
        win[HALO:, :] = a_ref[...].astype(F32) * _sigmoid(b_ref[...].astype(F32))
        acc = jnp.broadcast_to(bias_ref[...], (tb, cw))
        for w in range(CONV_WIDTH):
            acc = acc + k_ref[w:w + 1, :] * win[pl.ds(HALO - (CONV_WIDTH - 1) + w, tb), :]
        u1_ref[...] = acc

    prev = lambda i: jnp.maximum(i * hb - 1, 0)
    (u1,) = _hosted_call(
        body, rider, name="conv_fwd", grid=(T // tb, CC // cw),
        in_specs=[pl.BlockSpec((tb, cw), lambda i, c: (i, ca + c)),
                  pl.BlockSpec((tb, cw), lambda i, c: (i, cb + c)),
                  pl.BlockSpec((HALO, cw), lambda i, c: (prev(i), ca + c)),
                  pl.BlockSpec((HALO, cw), lambda i, c: (prev(i), cb + c)),
                  pl.BlockSpec((HALO, cw), lambda i, c: (0, c)),
                  pl.BlockSpec((1, cw), lambda i, c: (0, c))],
        out_specs=[pl.BlockSpec((tb, cw), lambda i, c: (i, c))],
        out_shape=[jax.ShapeDtypeStruct((T, CC), F32)],
        scratch=[pltpu.VMEM((tb + HALO, cw), F32)],
        args=[proj, proj, proj, proj, kpad, bias])
    return u1


def _conv_bwd(du1, proj, kpad, off_a, CC, T, rider=None):
    tb, cw = min(CONV_TB, T), CONV_CW
    hb = tb // HALO
    nt = T // tb
    ca, cb = off_a // cw, (off_a + CC) // cw

    def body(d_ref, dn_ref, a_ref, b_ref, ap_ref, bp_ref, k_ref, da_ref, db_ref, dk_ref, winu, wind):
        i = pl.program_id(1)
        a, b = a_ref[...].astype(F32), b_ref[...].astype(F32)
        sgb = _sigmoid(b)
        winu[0:HALO, :] = ap_ref[...].astype(F32) * _sigmoid(bp_ref[...].astype(F32)) * (i > 0).astype(F32)
        winu[HALO:, :] = a * sgb
        d = d_ref[...]
        wind[0:tb, :] = d
        wind[tb:, :] = dn_ref[...] * (i < nt - 1).astype(F32)

        @pl.when(i == 0)
        def _():
            dk_ref[...] = jnp.zeros_like(dk_ref)

        du0 = jnp.zeros((tb, cw), F32)
        for w in range(CONV_WIDTH):
            du0 = du0 + k_ref[w:w + 1, :] * wind[pl.ds(CONV_WIDTH - 1 - w, tb), :]
            dk_ref[w:w + 1, :] += _colsum(winu[pl.ds(HALO - (CONV_WIDTH - 1) + w, tb), :] * d)
        da_ref[...] = (du0 * sgb).astype(BF16)
        db_ref[...] = (du0 * a * sgb * (1.0 - sgb)).astype(BF16)

    prev = lambda i: jnp.maximum(i * hb - 1, 0)
    nxt = lambda i: jnp.minimum((i + 1) * hb, T // HALO - 1)
    return _hosted_call(
        body, rider, name="conv_bwd", grid=(CC // cw, nt),
        in_specs=[pl.BlockSpec((tb, cw), lambda c, i: (i, c)),
                  pl.BlockSpec((HALO, cw), lambda c, i: (nxt(i), c)),
                  pl.BlockSpec((tb, cw), lambda c, i: (i, ca + c)),
                  pl.BlockSpec((tb, cw), lambda c, i: (i, cb + c)),
                  pl.BlockSpec((HALO, cw), lambda c, i: (prev(i), ca + c)),
                  pl.BlockSpec((HALO, cw), lambda c, i: (prev(i), cb + c)),
                  pl.BlockSpec((HALO, cw), lambda c, i: (0, c))],
        out_specs=[pl.BlockSpec((tb, cw), lambda c, i: (i, c)),
                   pl.BlockSpec((tb, cw), lambda c, i: (i, c)),
                   pl.BlockSpec((HALO, cw), lambda c, i: (0, c))],
        out_shape=[jax.ShapeDtypeStruct((T, CC), BF16), jax.ShapeDtypeStruct((T, CC), BF16),
                   jax.ShapeDtypeStruct((HALO, CC), F32)],
        scratch=[pltpu.VMEM((tb + HALO, cw), F32), pltpu.VMEM((tb + HALO, cw), F32)],
        args=[du1, du1, proj, proj, proj, proj, kpad])


FFN1 = ('ffn1_w_gate', 'ffn1_w_up', 'ffn1_w_down')
FFN2 = ('ffn2_w_gate', 'ffn2_w_up', 'ffn2_w_down')


def _local_step(x, tgt, W, P, gdt=BF16, comm=None):
    T, D = x.shape
    G = {}
    net = _Net(comm, G)
    if comm is not None:
        W = comm.W
        first = net.gather(['ffn1_w_gate', 'ffn1_w_up'])
    (xb,) = _rows("x_to_bf16", lambda v: [v], [(x, 'r', D, 0)], [('r', D, BF16)], T=T, tb=512,
                  rider=first if comm is not None else None)
    if comm is not None:
        net.done(first)
    VW = P['ret_gn_g'].shape[1]
    H = VW // RET_DV
    QW = H * RET_DK
    CC = P['conv_b'].shape[1]
    off_glu = 2 * QW + 2 * VW
    off_gate = off_glu + 2 * CC
    ident = lambda accs: accs

    a1, b1, s1, z1 = _ffn_fwd("ffn1", xb, x, W, FFN1, net,
                              rider=net.gather(['ffn1_w_down', 'w_in'], {'w_in': (0, (3 * D) // 8, False)}),
                              rider_down=lambda: net.gather(['w_in'], {'w_in': ((3 * D) // 8, (3 * D) // 8, False)}))
    rider = net.gather(['w_in'], {'w_in': ((3 * D) // 4, D // 4, True)})
    x1, x1b, xh1, rs1 = _ln_fwd("ln1", z1, P['ln1_g'], P['ln1_b'], T, D, rider=rider)
    net.done(rider)

    rest = net.gather(['conv_k', 'w_ret_o', 'w_conv_o', 'w_out', 'ffn2_w_down'])
    (proj,) = _mm("w_in", [x1b], [W['w_in']], [[(0, 0)]], lambda accs, bias: [accs[0] + bias], [F32],
                  tm=2048, tn=0, tk=1024, extras=[(P['b_in'], 'n', 0)], i_outer=False, b3=True, rider=rest)
    net.done(rest)
    tabs = _ret_tables(H, T)
    rider = net.gather(['ffn2_w_gate', 'ffn2_w_up'])
    r, ret_in, states = _ret_fwd(proj, P['ret_gn_g'], tabs, H, T, rider=rider)
    net.done(rider)
    kpad = jnp.pad(W['conv_k'].astype(F32), ((0, HALO - CONV_WIDTH), (0, 0)))
    u1 = _conv_fwd(proj, kpad, P['conv_b'], off_glu, CC, T)

    def conv_ln(u1, g, b):
        xhat, rstd = _ln_stats(u1)
        u2 = xhat * g + b
        return [xhat, rstd, u2 * _sigmoid(u2)]

    xhc, rsc, u3 = _rows("conv_ln", conv_ln, [(u1, 'r', CC, 0), (P['conv_ln_g'], 'v', CC, 0), (P['conv_ln_b'], 'v', CC, 0)],
                         [('r', CC, F32), ('c', 1, F32), ('r', CC, BF16)], T=T, tb=512)
    (ret_out,) = _mm("ret_o", [ret_in], [W['w_ret_o']], [[(0, 0)]], ident, [F32], tm=1024, tn=1024, tk=2048)

    def epi_merge(accs, ret_out, gr, gc):
        conv_out = accs[0]
        return [conv_out, _sigmoid(gr) * ret_out + _sigmoid(gc) * conv_out]

    conv_out, merged = _mm("conv_o_merge", [u3], [W['w_conv_o']], [[(0, 0)]], epi_merge, [F32, BF16],
                           tm=512, tn=D, tk=1024, epi_rows=256,
                           extras=[(ret_out, 'mn', 0), (proj, 'mn', off_gate), (proj, 'mn', off_gate + D)])
    (z2,) = _mm("w_out", [merged], [W['w_out']], [[(0, 0)]], lambda accs, xr: [ALPHA * xr + accs[0]], [F32],
                tm=1024, tn=1024, tk=1024, extras=[(x1, 'mn', 0)])
    x2, x2b, xh2, rs2 = _ln_fwd("ln2", z2, P['ln2_g'], P['ln2_b'], T, D)
    a2, b2, s2, z3 = _ffn_fwd("ffn2", x2b, x2, W, FFN2, net)
    dz3, dz3h, g_ln3_g, g_ln3_b, loss = _ln_loss_bwd("ln3_loss", z3, P['ln3_g'], P['ln3_b'], tgt, T, D)

    S = {'ln3_g': g_ln3_g, 'ln3_b': g_ln3_b}
    dy2 = _ffn_bwd("ffn2b", dz3h, dz3, x2b, a2, b2, s2, W, FFN2, gdt, G, net, 'down')
    rider = net.to_sibling(['ffn2_w_gate', 'ffn2_w_up'])
    dz2, dz2b, S['ln2_g'], S['ln2_b'] = _ln_bwd("ln2b", dy2, xh2, rs2, P['ln2_g'], 1.0, T, D, rider=rider)
    net.done(rider)
    net.pairsum(['ffn2_w_gate', 'ffn2_w_up'])

    (G['w_out'],) = _mm("d_w_out", [merged], [dz2b], [[(0, 0)]], ident, [gdt], ta=True, tm=1024, tn=1024, tk=1024)

    def epi_dmerge(accs, ret_out, conv_out, gr, gc):
        dm_ = accs[0]
        sr, sc = _sigmoid(gr), _sigmoid(gc)
        return [dm_ * sr, dm_ * sc, dm_ * ret_out * sr * (1.0 - sr), dm_ * conv_out * sc * (1.0 - sc)]

    rider = net.to_sibling(['w_out'])
    dret_out, dconv_out, dgate_r, dgate_c = _mm(
        "d_merge", [dz2b], [W['w_out']], [[(0, 0)]], epi_dmerge, [BF16, BF16, BF16, BF16], tb=True,
        tm=512, tn=D, tk=1024, epi_rows=256, rider=rider,
        extras=[(ret_out, 'mn', 0), (conv_out, 'mn', 0), (proj, 'mn', off_gate), (proj, 'mn', off_gate + D)])
    net.done(rider)
    (G['w_ret_o'],) = _mm("d_w_ret_o", [ret_in], [dret_out], [[(0, 0)]], ident, [gdt], ta=True, tm=1024, tn=1024, tk=1024)
    (G['w_conv_o'],) = _mm("d_w_conv_o", [u3], [dconv_out], [[(0, 0)]], ident, [gdt], ta=True, tm=1024, tn=1024, tk=1024)
    rider = net.to_sibling(['w_ret_o', 'w_conv_o'])
    (dri,) = _mm("d_ret_in", [dret_out], [W['w_ret_o']], [[(0, 0)]], ident, [F32], tb=True, tm=1024, tn=1024, tk=1024,
                 rider=rider)
    net.done(rider)
    net.pairsum(['w_out', 'w_ret_o', 'w_conv_o'])
    rider = net.to_owner(['ffn2_w_gate', 'ffn2_w_up'])
    dproj, S['ret_gn_g'] = _ret_bwd(dri, r, states, proj, P['ret_gn_g'], tabs, H, T, proj.shape[1], rider=rider)
    net.done(rider)

    def epi_du2(accs, xhat, g, b):
        u2 = xhat * g + b
        return [accs[0] * _dsilu(u2, _sigmoid(u2))]

    (du2,) = _mm("d_u3", [dconv_out], [W['w_conv_o']], [[(0, 0)]], epi_du2, [F32], tb=True, tm=512, tn=CC, tk=1024, epi_rows=256,
                 extras=[(xhc, 'mn', 0), (P['conv_ln_g'], 'n', 0), (P['conv_ln_b'], 'n', 0)])

    def conv_ln_bwd(du2, xhat, rstd, g):
        du1 = _ln_bwd_math(du2, xhat, rstd, g)
        return [du1, _colsum(du2 * xhat), _colsum(du2), _colsum(du1)]

    du1, S['conv_ln_g'], S['conv_ln_b'], S['conv_b'] = _rows(
        "conv_ln_bwd", conv_ln_bwd, [(du2, 'r', CC, 0), (xhc, 'r', CC, 0), (rsc, 'r', 1, 0), (P['conv_ln_g'], 'v', CC, 0)],
        [('r', CC, F32), ('a', CC, F32), ('a', CC, F32), ('a', CC, F32)], T=T, tb=512)
    dglu_a, dglu_b, dkpad = _conv_bwd(du1, proj, kpad, off_glu, CC, T)
    G['conv_k'] = dkpad[:CONV_WIDTH].astype(gdt)

    for off, piece in ((off_glu, dglu_a), (off_glu + CC, dglu_b), (off_gate, dgate_r), (off_gate + D, dgate_c)):
        dproj = lax.dynamic_update_slice(dproj, piece, (0, off))
    IN_W = dproj.shape[1]
    rider = net.both(net.exchange(['conv_k']),
                     net.to_owner(['w_out', 'w_ret_o', 'w_conv_o']))
    G['w_in'], S['b_in'] = _mm("d_w_in", [x1b], [dproj], [[(0, 0)]], ident, [gdt], ta=True, o3=True, bsum=True,
                               tm=1024, tn=W['w_in'].shape[2], tk=1024, rider=rider)
    net.done(rider)
    cuts = [0, (3 * D) // 16, (43 * D) // 64, D]
    w_in_rows = [{'w_in': (cuts[i], cuts[i + 1] - cuts[i], i == 2)} for i in range(3)]
    rider = net.to_sibling(['w_in'])
    (dy1,) = _mm("d_x1", [dproj], [W['w_in']], [[(0, 0)]], lambda accs, dzr: [ALPHA * dzr + accs[0]], [F32], tb=True,
                 b3=True, tm=1024, tn=1024, tk=0, extras=[(dz2, 'mn', 0)], rider=rider)
    net.done(rider)
    net.pairsum(['w_in'])
    rider = net.to_owner(['w_in'], w_in_rows[0])
    dz1, dz1h, S['ln1_g'], S['ln1_b'] = _ln_bwd("ln1b", dy1, xh1, rs1, P['ln1_g'], 0.5, T, D, rider=rider)
    net.done(rider)
    grad_x = _ffn_bwd("ffn1b", dz1h, dz1, xb, a1, b1, s1, W, FFN1, gdt, G, net, 'all',
                      pre=(lambda: net.to_owner(['w_in'], w_in_rows[1]), lambda: net.to_owner(['w_in'], w_in_rows[2])))
    return loss, grad_x, G, S


def _coords():
    return lax.axis_index("x"), lax.axis_index("y"), lax.axis_index("c")


def _flip(k, x, y, c):
    return (1 - x if k & 4 else x, 1 - y if k & 2 else y, 1 - c if k & 1 else c)


def _lin(p):
    return 4 * p[0] + 2 * p[1] + p[2]


class _Rider:
    def __init__(self, ins, out_shape, rows=None, fill=None):
        nb = len(ins)
        self.rows = rows or [None] * nb
        fill = fill or [None] * nb
        self.aliases = {nb + i: w for i, w in enumerate(w for w in range(nb) if fill[w] is not None)}
        self.ins = list(ins) + [f for f in fill if f is not None]
        self.out_shape, self.results = list(out_shape), None
        self.scratch = [pltpu.SemaphoreType.DMA((8 * nb,)), pltpu.SemaphoreType.DMA((8 * nb,)),
                        pltpu.SemaphoreType.DMA((nb,))]

    def span(self, w, ref, *slot, half=None):
        rows = self.rows[w]
        if half is not None:
            first, count = rows if rows is not None else (0, self.out_shape[w].shape[1])
            rows = (first + half * (count // 2), count // 2)
        if rows is None:
            return ref.at[slot] if slot else ref
        return ref.at[(*slot, pl.ds(*rows))]

    def begin(self, step, n_steps, ins, outs, sems):
        @pl.when(step == 0)
        def _():
            self.start(ins, outs, sems)

        @pl.when(step == min(n_steps - 1, (5 * n_steps) // 8))
        def _():
            self.relay(ins, outs, sems)

        @pl.when(step == n_steps - 1)
        def _():
            self.mid(ins, outs, sems)

    def end(self, step, n_steps, ins, outs, sems):
        @pl.when(step == n_steps - 1)
        def _():
            self.finish(ins, outs, sems)

    def relay(self, ins, outs, sems):
        pass

    def mid(self, ins, outs, sems):
        pass


class _Riders:
    def __init__(self, riders):
        self.riders = list(riders)
        self.ins = [a for r in self.riders for a in r.ins]
        self.out_shape = [o for r in self.riders for o in r.out_shape]
        self.scratch = [c for r in self.riders for c in r.scratch]
        self.aliases, n_in, n_out = {}, 0, 0
        for r in self.riders:
            self.aliases.update({n_in + p: n_out + o for p, o in r.aliases.items()})
            n_in, n_out = n_in + len(r.ins), n_out + len(r.out_shape)

    def _each(self, ins, outs, sems):
        i = o = c = 0
        for r in self.riders:
            yield r, ins[i:i + len(r.ins)], outs[o:o + len(r.out_shape)], sems[c:c + len(r.scratch)]
            i, o, c = i + len(r.ins), o + len(r.out_shape), c + len(r.scratch)

    def begin(self, step, n_steps, ins, outs, sems):
        for r, i, o, c in self._each(ins, outs, sems):
            r.begin(step, n_steps, i, o, c)

    def end(self, step, n_steps, ins, outs, sems):
        for r, i, o, c in self._each(ins, outs, sems):
            r.end(step, n_steps, i, o, c)

    @property
    def results(self):
        return [x for r in self.riders for x in r.results]

    @results.setter
    def results(self, res):
        for r, _, o, _ in self._each([], list(res), []):
            r.results = o


class _GatherRider(_Rider):
    def __init__(self, blks, rows=None, fill=None):
        super().__init__(blks, [jax.ShapeDtypeStruct((N_DEV,) + b.shape, b.dtype) for b in blks], rows, fill)
        counts = [(r[1] if r is not None else b.shape[0]) for r, b in zip(self.rows, blks)]
        self.halves = [n % 32 == 0 for n in counts]

    def _copies(self, x_refs, out_refs, sems):
        nb = len(self.out_shape)
        send_sems, recv_sems, local_sems = sems
        x, y, c = _coords()
        me, sib = (x, y, c), (x, y, 1 - c)
        xn, yn, dg = _flip(4, x, y, c), _flip(2, x, y, c), _flip(6, x, y, c)
        plans = []
        for w in range(nb):
            own = self.span(w, x_refs[w])

            def copy(k, block, to, src=None, half=None, w=w):
                slot = self.span(w, out_refs[w], _lin(block), half=half)
                return pltpu.make_async_remote_copy(
                    src_ref=slot if src is None else src, dst_ref=slot, send_sem=send_sems.at[k * nb + w],
                    recv_sem=recv_sems.at[k * nb + w], device_id=to, device_id_type=MESH)

            mine = pltpu.make_async_copy(own, self.span(w, out_refs[w], _lin(me)), local_sems.at[w])
            first = [copy(0, me, sib, src=own), copy(1, me, xn, src=own), copy(2, me, yn, src=own)]
            if self.halves[w]:
                relay = [(copy(1, xn, me), [copy(3, xn, yn, half=0), copy(5, xn, sib)]),
                         (copy(2, yn, me), [copy(4, yn, xn, half=1), copy(6, yn, sib)])]
                last = [(copy(3, dg, me, half=0), []), (copy(4, dg, me, half=1), [copy(7, dg, sib)])]
            else:
                first.append(copy(3, me, dg, src=own))
                relay = [(copy(1, xn, me), [copy(5, xn, sib)]), (copy(2, yn, me), [copy(6, yn, sib)])]
                last = [(copy(3, dg, me), [copy(7, dg, sib)])]
            other = lambda p: (p[0], p[1], 1 - c)
            from_sib = [copy(0, sib, me), copy(5, other(xn), me), copy(6, other(yn), me), copy(7, other(dg), me)]
            plans.append((mine, first, relay, last, from_sib))
        return plans

    def start(self, ins, outs, sems):
        for mine, first, _, _, _ in self._copies(ins, outs, sems):
            for cp in [mine] + first:
                cp.start()

    def relay(self, ins, outs, sems):
        for _, _, relay, _, _ in self._copies(ins, outs, sems):
            for arrival, released in relay:
                arrival.wait_recv()
                for cp in released:
                    cp.start()

    def mid(self, ins, outs, sems):
        for _, _, _, last, _ in self._copies(ins, outs, sems):
            for arrival, released in last:
                arrival.wait_recv()
                for cp in released:
                    cp.start()

    def finish(self, ins, outs, sems):
        for mine, first, relay, last, from_sib in self._copies(ins, outs, sems):
            for cp in from_sib:
                cp.wait_recv()
            for cp in first + [cp for _, released in relay + last for cp in released]:
                cp.wait_send()
            mine.wait()


class _ExchangeRider(_Rider):
    def __init__(self, gs, rows=None, fill=None):
        super().__init__(gs, [jax.ShapeDtypeStruct(g.shape, g.dtype) for g in gs], rows, fill)

    def _copies(self, g_refs, out_refs, sems):
        nb = len(self.out_shape)
        send_sems, recv_sems, local_sems = sems
        x, y, c = _coords()
        me = _lin((x, y, c))

        def copy(k, w, landing):
            peer = _flip(k, x, y, c)
            src, dst = (me, _lin(peer)) if landing else (_lin(peer), me)
            return pltpu.make_async_remote_copy(
                src_ref=self.span(w, g_refs[w], src), dst_ref=self.span(w, out_refs[w], dst),
                send_sem=send_sems.at[(k - 1) * nb + w], recv_sem=recv_sems.at[(k - 1) * nb + w],
                device_id=peer, device_id_type=MESH)

        mines = [pltpu.make_async_copy(self.span(w, g_refs[w], me), self.span(w, out_refs[w], me), local_sems.at[w])
                 for w in range(nb)]
        sends = [copy(k, w, False) for w in range(nb) for k in range(1, N_DEV)]
        landings = [copy(k, w, True) for w in range(nb) for k in range(1, N_DEV)]
        return mines, sends, landings

    def start(self, ins, outs, sems):
        mines, sends, _ = self._copies(ins, outs, sems)
        for cp in mines + sends:
            cp.start()

    def finish(self, ins, outs, sems):
        mines, sends, landings = self._copies(ins, outs, sems)
        for cp in landings:
            cp.wait_recv()
        for cp in sends:
            cp.wait_send()
        for mine in mines:
            mine.wait()


class _SiblingRider(_Rider):
    def __init__(self, gs):
        super().__init__(gs, [jax.ShapeDtypeStruct((4,) + g.shape[1:], g.dtype) for g in gs])

    def _copies(self, g_refs, out_refs, sems, landing):
        nb = len(self.out_shape)
        send_sems, recv_sems, _ = sems
        x, y, c = _coords()
        whose = c if landing else 1 - c
        return [pltpu.make_async_remote_copy(
            src_ref=g_refs[w].at[2 * q + whose], dst_ref=out_refs[w].at[q], send_sem=send_sems.at[q * nb + w],
            recv_sem=recv_sems.at[q * nb + w], device_id=(x, y, 1 - c), device_id_type=MESH)
            for w in range(nb) for q in range(4)]

    def start(self, ins, outs, sems):
        for cp in self._copies(ins, outs, sems, False):
            cp.start()

    def finish(self, ins, outs, sems):
        for cp in self._copies(ins, outs, sems, True):
            cp.wait_recv()
        for cp in self._copies(ins, outs, sems, False):
            cp.wait_send()


class _ChipRider(_Rider):
    FLIPS = (4, 2, 6)

    def __init__(self, ps, rows=None, fill=None):
        super().__init__(ps, [jax.ShapeDtypeStruct(p.shape, p.dtype) for p in ps], rows, fill)

    def _copies(self, p_refs, out_refs, sems):
        nb = len(self.out_shape)
        send_sems, recv_sems, local_sems = sems
        x, y, c = _coords()
        my_chip = 2 * x + y

        def copy(j, w, landing):
            peer = _flip(self.FLIPS[j], x, y, c)
            peer_chip = 2 * peer[0] + peer[1]
            src, dst = (my_chip, peer_chip) if landing else (peer_chip, my_chip)
            return pltpu.make_async_remote_copy(
                src_ref=self.span(w, p_refs[w], src), dst_ref=self.span(w, out_refs[w], dst),
                send_sem=send_sems.at[j * nb + w], recv_sem=recv_sems.at[j * nb + w],
                device_id=peer, device_id_type=MESH)

        mines = [pltpu.make_async_copy(self.span(w, p_refs[w], my_chip), self.span(w, out_refs[w], my_chip),
                                       local_sems.at[w]) for w in range(nb)]
        sends = [copy(j, w, False) for w in range(nb) for j in range(3)]
        landings = [copy(j, w, True) for w in range(nb) for j in range(3)]
        return mines, sends, landings

    def start(self, ins, outs, sems):
        mines, sends, _ = self._copies(ins, outs, sems)
        for cp in mines + sends:
            cp.start()

    def finish(self, ins, outs, sems):
        mines, sends, landings = self._copies(ins, outs, sems)
        for cp in landings:
            cp.wait_recv()
        for cp in sends:
            cp.wait_send()
        for mine in mines:
            mine.wait()


def _pairsum(name, g, land):
    _, r, cols = g.shape
    tb = r if r % 16 else _row_tile(r, 16, max(16, (1024 * 1024) // cols))
    core = lax.axis_index("c").astype(jnp.int32).reshape(1)

    def body(core_ref, g_ref, l_ref, o_ref):
        o_ref[...] = (g_ref[...].astype(F32) + l_ref[...].astype(F32)).astype(o_ref.dtype)

    return pl.pallas_call(
        body, name=name, out_shape=jax.ShapeDtypeStruct((4, r, cols), g.dtype),
        grid_spec=pltpu.PrefetchScalarGridSpec(
            num_scalar_prefetch=1, grid=(4, r // tb),
            in_specs=[pl.BlockSpec((None, None, tb, cols), lambda q, i, core_ref: (q, core_ref[0], i, 0)),
                      pl.BlockSpec((None, tb, cols), lambda q, i, core_ref: (q, i, 0))],
            out_specs=pl.BlockSpec((None, tb, cols), lambda q, i, core_ref: (q, i, 0))),
        compiler_params=_params(("arbitrary", "arbitrary")),
    )(core, g.reshape(4, 2, r, cols), land)


def _run_rider(name, rider):
    n_in, n_out = len(rider.ins), len(rider.out_shape)

    def body(*refs):
        ride = (refs[:n_in], refs[n_in:n_in + n_out], refs[n_in + n_out:])
        rider.start(*ride)
        rider.relay(*ride)
        rider.mid(*ride)
        rider.finish(*ride)

    rider.results = pl.pallas_call(
        body, name=name, out_shape=rider.out_shape, in_specs=[ANY] * n_in, out_specs=[ANY] * n_out,
        scratch_shapes=rider.scratch, input_output_aliases=dict(rider.aliases),
        compiler_params=pltpu.CompilerParams(has_side_effects=True),
    )(*rider.ins)
    return rider.results


def _as_matrix(name, g):
    if name == 'w_in':
        return g
    if name in COL_SHARDED:
        return jnp.transpose(g, (1, 0, 2)).reshape(g.shape[1], N_DEV * g.shape[2])
    return g.reshape(N_DEV * g.shape[1], g.shape[2])


def _by_owner(name, g):
    if name == 'w_in':
        return g
    if name in COL_SHARDED:
        return jnp.transpose(g.reshape(g.shape[0], N_DEV, g.shape[1] // N_DEV), (1, 0, 2))
    return g.reshape(N_DEV, g.shape[0] // N_DEV, g.shape[1])


class _Comm:
    def __init__(self, shards):
        self.shards, self.W, self.parts, self.partial, self.sent = shards, {}, {}, {}, {}
        self.from_sibling, self.pairs = {}, {}

    def _ride(self, cls, names, srcs, part, sink):
        part = part or {}
        rider = cls(srcs, rows=[part[n][:2] if n in part else None for n in names],
                    fill=[self.partial.pop((sink, n), None) for n in names])
        rider.names, rider.sink = names, sink
        rider.unfinished = {n for n in names if n in part and not part[n][2]}
        return rider

    def gather(self, names, part=None):
        return self._ride(_GatherRider, names, [self.shards[n] for n in names], part, 'W')

    def exchange(self, names, G, part=None):
        for n in names:
            if n not in self.sent:
                self.sent[n] = _by_owner(n, G[n])
        return self._ride(_ExchangeRider, names, [self.sent[n] for n in names], part, 'parts')

    def to_sibling(self, names, G):
        for n in names:
            self.sent[n] = _by_owner(n, G[n])
        rider = _SiblingRider([self.sent[n] for n in names])
        rider.names, rider.sink, rider.unfinished = names, 'sibling', set()
        return rider

    def pairsum(self, names):
        for n in names:
            self.pairs[n] = _pairsum("pairsum_" + n, self.sent[n], self.from_sibling.pop(n))

    def to_owner(self, names, part=None):
        return self._ride(_ChipRider, names, [self.pairs[n] for n in names], part, 'parts')

    def collect(self, rider):
        for n, res in zip(rider.names, rider.results):
            if n in rider.unfinished:
                self.partial[(rider.sink, n)] = res
            elif rider.sink == 'W':
                self.W[n] = _as_matrix(n, res)
            elif rider.sink == 'sibling':
                self.from_sibling[n] = res
            else:
                self.parts[n] = res


def _adamw_math(p_ref, w_ref, m_ref, v_ref, g_ref, d_ref, nm_ref, nv_ref):
    c1 = 1.0 - ADAM_B1 ** ADAM_STEP
    c2 = 1.0 - ADAM_B2 ** ADAM_STEP
    g = p_ref[0].astype(F32)
    for s in range(1, p_ref.shape[0]):
        g = g + p_ref[s].astype(F32)
    nm = ADAM_B1 * m_ref[...] + (1.0 - ADAM_B1) * g
    nv = ADAM_B2 * v_ref[...] + (1.0 - ADAM_B2) * (g * g)
    g_ref[...] = g
    nm_ref[...] = nm
    nv_ref[...] = nv
    d_ref[...] = -ADAM_LR * ((nm / c1) / (jnp.sqrt(nv / c2) + ADAM_EPS) + ADAM_WD * w_ref[...])


def _adamw_vectors(parts, ws, ms, vs, loss_parts):
    k = len(ws)

    def body(*refs):
        for i in range(k):
            _adamw_math(refs[i], refs[k + i], refs[2 * k + i], refs[3 * k + i], *refs[4 * k + 1 + 4 * i:4 * k + 5 + 4 * i])
        lp, lo = refs[4 * k], refs[8 * k + 1]
        lo[...] = functools.reduce(jnp.add, [lp[s] for s in range(lp.shape[0])])

    return pl.pallas_call(
        body, name="adamw_vectors",
        out_shape=[jax.ShapeDtypeStruct(w.shape, F32) for w in ws for _ in range(4)] + [jax.ShapeDtypeStruct((1, 128), F32)],
        compiler_params=_params(),
    )(*parts, *ws, *ms, *vs, loss_parts)


def _adamw(name, parts, w, m, v, tb):
    n, R, Wd = parts.shape
    assert R % tb == 0
    body = functools.partial(_adamw_math)

    row = pl.BlockSpec((tb, Wd), lambda i: (i, 0))
    return pl.pallas_call(
        body, name=name, grid=(R // tb,),
        in_specs=[pl.BlockSpec((n, tb, Wd), lambda i: (0, i, 0)), row, row, row],
        out_specs=[row, row, row, row], out_shape=[jax.ShapeDtypeStruct((R, Wd), F32)] * 4,
        compiler_params=_params(("arbitrary",)),
    )(parts, w, m, v)


def _row_tile(R, unit, cap):
    best = unit
    for t in range(unit, cap + 1, unit):
        if R % t == 0:
            best = t
    return best


def kernel(x, ffn1_w_gate, ffn1_w_up, ffn1_w_down, ln1_g, ln1_b, w_in, b_in, ret_gn_g, conv_k, conv_b, conv_ln_g, conv_ln_b, w_ret_o, w_conv_o, w_out, ln2_g, ln2_b, ffn2_w_gate, ffn2_w_up, ffn2_w_down, ln3_g, ln3_b, loss_target, m_ffn1_w_gate, m_ffn1_w_up, m_ffn1_w_down, m_ln1_g, m_ln1_b, m_w_in, m_b_in, m_ret_gn_g, m_conv_k, m_conv_b, m_conv_ln_g, m_conv_ln_b, m_w_ret_o, m_w_conv_o, m_w_out, m_ln2_g, m_ln2_b, m_ffn2_w_gate, m_ffn2_w_up, m_ffn2_w_down, m_ln3_g, m_ln3_b, v_ffn1_w_gate, v_ffn1_w_up, v_ffn1_w_down, v_ln1_g, v_ln1_b, v_w_in, v_b_in, v_ret_gn_g, v_conv_k, v_conv_b, v_conv_ln_g, v_conv_ln_b, v_w_ret_o, v_w_conv_o, v_w_out, v_ln2_g, v_ln2_b, v_ffn2_w_gate, v_ffn2_w_up, v_ffn2_w_down, v_ln3_g, v_ln3_b):
    given = dict(locals())
    wts = {n: given[n] for n in WEIGHTS}
    mom = {n: given['m_' + n] for n in WEIGHTS}
    var = {n: given['v_' + n] for n in WEIGHTS}

    def shard2d(a):
        return a.reshape(a.shape[-3] * a.shape[-2] if a.ndim == 4 else a.shape[-2], a.shape[-1])

    comm = _Comm({n: shard2d(wts[n]).astype(BF16) for n in BIG})
    P = {n: wts[n].reshape(1, -1) for n in SMALL}
    loss, grad_x, _, S = _local_step(x[0], loss_target[0], None, P, comm=comm)

    parts = comm.parts
    res = {}
    for n in BIG:
        rows, cols = parts[n].shape[1:]
        tb = rows if rows % 16 else _row_tile(rows, 16, max(16, (256 * 1024) // cols))
        res[n] = _adamw("adamw_" + n, parts[n], shard2d(wts[n]), shard2d(mom[n]), shard2d(var[n]), tb)

    vec_parts = _run_rider("gather_vector_grads", _GatherRider([S[n] for n in SMALL] + [loss]))
    vec = _adamw_vectors(vec_parts[:-1], [P[n] for n in SMALL], [mom[n].reshape(1, -1) for n in SMALL],
                         [var[n].reshape(1, -1) for n in SMALL], vec_parts[-1])
    for i, n in enumerate(SMALL):
        res[n] = vec[4 * i:4 * i + 4]

    outs = [vec[-1][0, 0], grad_x[None]]
    for k in range(4):
        for n in WEIGHTS:
            outs.append(res[n][k].reshape(wts[n].shape))
    return tuple(outs)
```

```python
import functools
import math

import jax
import jax.numpy as jnp
from jax import lax
from jax.experimental import pallas as pl
from jax.experimental.pallas import tpu as pltpu

F32 = jnp.float32
BF16 = jnp.bfloat16

N_DEV = 8
LN_EPS = 1e-5
ALPHA = 2.0 ** 0.25
RET_DK = 128
RET_DV = 256
RET_CHUNK = 256
ROPE_BASE = 10000.0
CONV_WIDTH = 31
HALO = 32
ADAM_LR, ADAM_B1, ADAM_B2, ADAM_EPS, ADAM_WD, ADAM_STEP = 0.001, 0.9, 0.999, 1e-08, 0.01, 10
VMEM_LIMIT = 52 * 1024 * 1024
MESH = pl.DeviceIdType.MESH
ANY = pl.BlockSpec(memory_space=pl.ANY)

BIG = ['ffn1_w_gate', 'ffn1_w_up', 'ffn1_w_down', 'w_in', 'w_ret_o', 'w_conv_o', 'w_out',
       'ffn2_w_gate', 'ffn2_w_up', 'ffn2_w_down', 'conv_k']
COL_SHARDED = {'ffn1_w_gate', 'ffn1_w_up', 'w_in', 'ffn2_w_gate', 'ffn2_w_up', 'conv_k'}
SMALL = ['ln1_g', 'ln1_b', 'b_in', 'ret_gn_g', 'conv_b', 'conv_ln_g', 'conv_ln_b', 'ln2_g', 'ln2_b', 'ln3_g', 'ln3_b']
WEIGHTS = ['ffn1_w_gate', 'ffn1_w_up', 'ffn1_w_down', 'ln1_g', 'ln1_b', 'w_in', 'b_in', 'ret_gn_g', 'conv_k', 'conv_b',
           'conv_ln_g', 'conv_ln_b', 'w_ret_o', 'w_conv_o', 'w_out', 'ln2_g', 'ln2_b', 'ffn2_w_gate', 'ffn2_w_up',
           'ffn2_w_down', 'ln3_g', 'ln3_b']


def _params(sem=None):
    return pltpu.CompilerParams(dimension_semantics=sem, vmem_limit_bytes=VMEM_LIMIT)


def _sigmoid(x):
    return jax.nn.sigmoid(x)


def _dsilu(x, sg):
    return sg * (1.0 + x * (1.0 - sg))


def _fit(dim, want):
    if dim <= want:
        return dim
    return max(t for t in range(128, want + 1, 128) if dim % t == 0)


def _dot(a, b, ta=False, tb=False):
    dn = (((0,) if ta else (1,), (1,) if tb else (0,)), ((), ()))
    return lax.dot_general(a, b, dn, preferred_element_type=F32)


def _mm(name, As, Bs, prods, epi, out_dtypes, *, ta=False, tb=False, tm, tn, tk, extras=(), i_outer=True,
        b3=False, o3=False, rider=None, bsum=False, epi_rows=0):
    a0, b0 = As[0], Bs[0]
    M, K = (a0.shape[1], a0.shape[0]) if ta else a0.shape
    if b3:
        S, rows, cs = b0.shape
        N = rows if tb else S * cs
        assert K == (S * cs if tb else rows)
        tn, tk = (tn, cs) if tb else (cs, tk)
    else:
        N = b0.shape[0] if tb else b0.shape[1]
    tm, tn, tk = _fit(M, tm), _fit(N, tn), _fit(K, tk)
    assert M % tm == 0 and N % tn == 0 and K % tk == 0, (name, M, N, K, tm, tn, tk)
    gi, gj, gk = M // tm, N // tn, K // tk
    grid = (gi, gj, gk) if i_outer else (gj, gi, gk)

    def ij(g0, g1):
        return (g0, g1) if i_outer else (g1, g0)

    def amap(g0, g1, k):
        i, _ = ij(g0, g1)
        return (k, i) if ta else (i, k)

    def bmap(g0, g1, k):
        _, j = ij(g0, g1)
        return (j, k) if tb else (k, j)

    def bmap3(g0, g1, k):
        _, j = ij(g0, g1)
        return (k, j, 0) if tb else (j, k, 0)

    in_specs = [pl.BlockSpec((tk, tm) if ta else (tm, tk), amap) for _ in As]
    if b3:
        in_specs += [pl.BlockSpec((None, tn, tk) if tb else (None, tk, tn), bmap3) for _ in Bs]
    else:
        in_specs += [pl.BlockSpec((tn, tk) if tb else (tk, tn), bmap) for _ in Bs]
    args = list(As) + list(Bs)
    for arr, kind, coloff in extras:
        assert coloff % tn == 0
        off = coloff // tn
        if kind == 'mn':
            in_specs.append(pl.BlockSpec((tm, tn), lambda g0, g1, k, off=off: (ij(g0, g1)[0], ij(g0, g1)[1] + off)))
        else:
            in_specs.append(pl.BlockSpec((1, tn), lambda g0, g1, k, off=off: (0, ij(g0, g1)[1] + off)))
        args.append(arr)
    if o3:
        out_shape = [jax.ShapeDtypeStruct((gj, M, tn), dt) for dt in out_dtypes]
        out_specs = [pl.BlockSpec((None, tm, tn), lambda g0, g1, k: (ij(g0, g1)[1], ij(g0, g1)[0], 0))
                     for _ in out_dtypes]
    else:
        out_shape = [jax.ShapeDtypeStruct((M, N), dt) for dt in out_dtypes]
        out_specs = [pl.BlockSpec((tm, tn), lambda g0, g1, k: ij(g0, g1)) for _ in out_dtypes]
    if bsum:
        assert gi == 1 and not tb and not b3
        out_shape.append(jax.ShapeDtypeStruct((1, N), F32))
        out_specs.append(pl.BlockSpec((1, tn), lambda g0, g1, k: (0, ij(g0, g1)[1])))
    n_a, n_b, n_e, n_o = len(As), len(Bs), len(extras), len(out_shape)
    n_p = len(prods) if gk > 1 else 0
    scratch = [pltpu.VMEM((tm, tn), F32) for _ in range(n_p)]
    if rider is not None:
        in_specs, out_specs = in_specs + [ANY] * len(rider.ins), out_specs + [ANY] * len(rider.out_shape)
        args, out_shape, scratch = args + rider.ins, out_shape + rider.out_shape, scratch + rider.scratch
    n_in, n_out = len(args), len(out_shape)

    def body(*refs):
        a_refs = refs[:n_a]
        b_refs = refs[n_a:n_a + n_b]
        e_refs = refs[n_a + n_b:n_a + n_b + n_e]
        o_refs = refs[n_in:n_in + n_o]
        acc_refs = refs[n_in + n_out:n_in + n_out + n_p]
        k = pl.program_id(2)
        if rider is not None:
            step = (pl.program_id(0) * grid[1] + pl.program_id(1)) * gk + k
            ride = (step, grid[0] * grid[1] * gk, refs[n_a + n_b + n_e:n_in], refs[n_in + n_o:n_in + n_out],
                    refs[n_in + n_out + n_p:])
            rider.begin(*ride)

        def finish(accs, rows=slice(None)):
            ex = [(e[rows, :] if kind == 'mn' else e[...]).astype(F32) for e, (_, kind, _) in zip(e_refs, extras)]
            for o, r in zip(o_refs, epi(accs, *ex)):
                o[rows, :] = r.astype(o.dtype)

        if bsum:
            @pl.when(k == 0)
            def _():
                o_refs[-1][...] = jnp.zeros_like(o_refs[-1])

            o_refs[-1][...] += _colsum(b_refs[0][...].astype(F32))

        if gk == 1:
            sub = tm if ta or not epi_rows else _fit(tm, epi_rows)
            for r0 in range(0, tm, sub):
                rows = slice(None) if ta else slice(r0, r0 + sub)
                finish([functools.reduce(jnp.add, [_dot(a_refs[ai][...] if ta else a_refs[ai][rows, :],
                                                        b_refs[bi][...], ta, tb) for ai, bi in terms])
                        for terms in prods], rows)
        else:
            @pl.when(k == 0)
            def _():
                for acc in acc_refs:
                    acc[...] = jnp.zeros_like(acc)

            for p, terms in enumerate(prods):
                for ai, bi in terms:
                    acc_refs[p][...] += _dot(a_refs[ai][...], b_refs[bi][...], ta, tb)

            @pl.when(k == gk - 1)
            def _():
                finish([acc[...] for acc in acc_refs])

        if rider is not None:
            rider.end(*ride)

    aliases = {} if rider is None else {n_a + n_b + n_e + p: n_o + o for p, o in rider.aliases.items()}
    res = pl.pallas_call(
        body, name=name, grid=grid, in_specs=in_specs, out_specs=out_specs, out_shape=out_shape,
        scratch_shapes=scratch, input_output_aliases=aliases,
        compiler_params=_params(("arbitrary", "arbitrary", "arbitrary")),
    )(*args)
    if rider is not None:
        rider.results = res[n_o:]
    return res[:n_o]


def _rows(name, fn, ins, outs, *, T, tb, rider=None):
    tb = min(tb, T)
    assert T % tb == 0
    in_specs, args = [], []
    for arr, kind, width, cb in ins:
        if kind == 'r':
            in_specs.append(pl.BlockSpec((tb, width), lambda i, _, cb=cb: (i, cb)))
        else:
            in_specs.append(pl.BlockSpec((1, width), lambda i, _, cb=cb: (0, cb)))
        args.append(arr)
    out_shape, out_specs = [], []
    for kind, width, dtype in outs:
        if kind == 'r':
            out_shape.append(jax.ShapeDtypeStruct((T, width), dtype))
            out_specs.append(pl.BlockSpec((tb, width), lambda i, _: (i, 0)))
        elif kind == 'c':
            out_shape.append(jax.ShapeDtypeStruct((T, 1), dtype))
            out_specs.append(pl.BlockSpec((tb, 1), lambda i, _: (i, 0)))
        else:
            out_shape.append(jax.ShapeDtypeStruct((1, width), F32))
            out_specs.append(pl.BlockSpec((1, width), lambda i, _: (0, 0)))
    n_in = len(ins)

    def body(*refs):
        i = pl.program_id(0)
        vals = fn(*[r[...] for r in refs[:n_in]])
        for (kind, _, _), o, v in zip(outs, refs[n_in:], vals):
            if kind == 'a':
                @pl.when(i == 0)
                def _(o=o):
                    o[...] = jnp.zeros_like(o)

                o[...] += v
            else:
                o[...] = v.astype(o.dtype)

    return _hosted_call(body, rider, name=name, grid=(T // tb, 1), in_specs=in_specs, out_specs=out_specs,
                        out_shape=out_shape, scratch=[], args=args)


def _colsum(v):
    return jnp.sum(v, axis=0, keepdims=True)


def _ln_stats(z):
    mu = jnp.mean(z, axis=-1, keepdims=True)
    d = z - mu
    var = jnp.mean(d * d, axis=-1, keepdims=True)
    rstd = lax.rsqrt(var + LN_EPS)
    return d * rstd, rstd


def _ln_bwd_math(dy, xhat, rstd, g):
    dxh = dy * g
    m1 = jnp.mean(dxh, axis=-1, keepdims=True)
    m2 = jnp.mean(dxh * xhat, axis=-1, keepdims=True)
    return rstd * (dxh - m1 - xhat * m2)


def _ln_fwd(name, z, g, b, T, D, rider=None):
    def fn(z, g, b):
        xhat, rstd = _ln_stats(z)
        y = xhat * g + b
        return [y, y, xhat, rstd]

    return _rows(name, fn, [(z, 'r', D, 0), (g, 'v', D, 0), (b, 'v', D, 0)],
                 [('r', D, F32), ('r', D, BF16), ('r', D, F32), ('c', 1, F32)], T=T, tb=512, rider=rider)


def _ln_bwd(name, dy, xhat, rstd, g, scale, T, D, rider=None):
    def fn(dy, xhat, rstd, g):
        dz = _ln_bwd_math(dy, xhat, rstd, g)
        return [dz, dz * scale, _colsum(dy * xhat), _colsum(dy)]

    return _rows(name, fn, [(dy, 'r', D, 0), (xhat, 'r', D, 0), (rstd, 'r', 1, 0), (g, 'v', D, 0)],
                 [('r', D, F32), ('r', D, BF16), ('a', D, F32), ('a', D, F32)], T=T, tb=512, rider=rider)


def _ln_loss_bwd(name, z, g, b, tgt, T, D):
    def fn(z, g, b, tgt):
        xhat, rstd = _ln_stats(z)
        err = xhat * g + b - tgt
        row_loss = 0.5 * jnp.mean(err * err, axis=-1, keepdims=True)
        loss = jnp.broadcast_to(jnp.sum(row_loss, axis=0, keepdims=True), (1, 128))
        dy = err * (1.0 / D)
        dz = _ln_bwd_math(dy, xhat, rstd, g)
        return [dz, dz * 0.5, _colsum(dy * xhat), _colsum(dy), loss]

    return _rows(name, fn, [(z, 'r', D, 0), (g, 'v', D, 0), (b, 'v', D, 0), (tgt, 'r', D, 0)],
                 [('r', D, F32), ('r', D, BF16), ('a', D, F32), ('a', D, F32), ('a', 128, F32)], T=T, tb=512)


class _Net:
    def __init__(self, comm, G):
        self.comm, self.G = comm, G

    def gather(self, names, part=None):
        return self.comm.gather(names, part) if self.comm else None

    def exchange(self, names, part=None):
        return self.comm.exchange(names, self.G, part) if self.comm else None

    def to_sibling(self, names):
        return self.comm.to_sibling(names, self.G) if self.comm else None

    def pairsum(self, names):
        if self.comm:
            self.comm.pairsum(names)

    def to_owner(self, names, part=None):
        return self.comm.to_owner(names, part) if self.comm else None

    def done(self, rider):
        if rider is not None:
            for one in getattr(rider, 'riders', [rider]):
                self.comm.collect(one)

    def both(self, *riders):
        return _Riders(riders) if self.comm else None


def _ffn_fwd(tag, xb, x, W, names, net, rider=None, rider_down=None):
    def epi_gu(accs):
        a, b = accs
        return [a, b, a * _sigmoid(a) * b]

    ng, nu, nd = names
    a, b, s = _mm(tag + "_gate_up", [xb], [W[ng], W[nu]], [[(0, 0)], [(0, 1)]], epi_gu, [BF16, BF16, BF16],
                  tm=1024, tn=1408, tk=1024, rider=rider, epi_rows=256)
    net.done(rider)

    def epi_down(accs, xres):
        return [ALPHA * xres + 0.5 * accs[0]]

    rider_down = rider_down() if rider_down else None
    (z,) = _mm(tag + "_down", [s], [W[nd]], [[(0, 0)]], epi_down, [F32], tm=1024, tn=1024, tk=1408,
               extras=[(x, 'mn', 0)], rider=rider_down)
    net.done(rider_down)
    return a, b, s, z


def _ffn_bwd(tag, dzh, dz, xb, a, b, s, W, names, gdt, G, net, ride, pre=(None, None)):
    ng, nu, nd = names

    def epi_ds(accs, a, b):
        ds = accs[0]
        sg = _sigmoid(a)
        return [ds * b * _dsilu(a, sg), ds * a * sg]

    rider = pre[0]() if pre[0] else None
    da, db = _mm(tag + "_ds", [dzh], [W[nd]], [[(0, 0)]], epi_ds, [BF16, BF16], tb=True, tm=1024, tn=1408, tk=1024, epi_rows=256,
                 extras=[(a, 'mn', 0), (b, 'mn', 0)], rider=rider)
    net.done(rider)
    ident = lambda accs: accs
    rider = pre[1]() if pre[1] else None
    (G[nd],) = _mm(tag + "_dwd", [s], [dzh], [[(0, 0)]], ident, [gdt], ta=True, tm=1408, tn=1024, tk=1024,
                   rider=rider)
    net.done(rider)
    if ride == 'all':
        rider = net.to_sibling([nd])
        (G[ng],) = _mm(tag + "_dwg", [xb], [da], [[(0, 0)]], ident, [gdt], ta=True, tm=1024, tn=1408, tk=1024,
                       rider=rider)
        net.done(rider)
        net.pairsum([nd])
        rider = net.to_owner([nd])
        (G[nu],) = _mm(tag + "_dwu", [xb], [db], [[(0, 0)]], ident, [gdt], ta=True, tm=1024, tn=1408, tk=1024,
                       rider=rider)
        net.done(rider)
        rider = net.to_sibling([ng, nu])
        if rider is not None:
            _run_rider(tag + "_to_sibling", rider)
            net.done(rider)
        net.pairsum([ng, nu])
        rider = net.to_owner([ng, nu])
    else:
        rider = net.to_sibling([nd]) if ride else None
        G[ng], G[nu] = _mm(tag + "_dwgu", [xb], [da, db], [[(0, 0)], [(0, 1)]], ident, [gdt, gdt], ta=True,
                           tm=1024, tn=1408, tk=1024, rider=rider)
        net.done(rider)
        if ride:
            net.pairsum([nd])
        rider = net.to_owner([nd]) if ride else None

    def epi_dx(accs, dzres):
        return [ALPHA * dzres + accs[0]]

    (dx,) = _mm(tag + "_dx", [da, db], [W[ng], W[nu]], [[(0, 0), (1, 1)]], epi_dx, [F32], tb=True,
                tm=1024, tn=1024, tk=1408, extras=[(dz, 'mn', 0)], rider=rider)
    net.done(rider)
    return dx


def _ret_tables(H, T):
    C = RET_CHUNK
    log_g = jnp.log(1.0 - jnp.exp2(-5.0 - jnp.arange(H, dtype=F32)))
    idx = jnp.arange(C, dtype=F32)
    diff = idx[:, None] - idx[None, :]
    dm = jnp.where(diff[None] >= 0, jnp.exp(jnp.maximum(diff, 0.0)[None] * log_g[:, None, None]), 0.0)
    xi = jnp.broadcast_to(jnp.exp((idx[None, :] + 1.0) * log_g[:, None])[:, :, None], (H, C, RET_DV))
    zeta = jnp.broadcast_to(jnp.exp((C - 1.0 - idx)[None, :] * log_g[:, None])[:, :, None], (H, C, RET_DK))
    gc = jnp.broadcast_to(jnp.exp(C * log_g)[:, None, None], (H, 1, RET_DV))
    half = RET_DK // 2
    freqs = ROPE_BASE ** (-jnp.arange(half, dtype=F32) / half)
    ang = jnp.arange(T, dtype=F32)[:, None] * freqs[None, :]
    cos, sin = jnp.cos(ang), jnp.sin(ang)
    cosf = jnp.concatenate([cos, cos], axis=1)
    sins = jnp.concatenate([-sin, sin], axis=1)
    return dm, xi, zeta, gc, cosf, sins


def _rot(x, cosf, sins):
    return x * cosf + pltpu.roll(x, RET_DK // 2, 1) * sins


def _rot_bwd(dy, cosf, sins):
    return dy * cosf + pltpu.roll(dy * sins, RET_DK // 2, 1)


RET_HB = 8


def _ret_specs(H, HB, rev, NC):
    C, G = RET_CHUNK, H // HB
    nn = (lambda n: NC - 1 - n) if rev else (lambda n: n)
    return [
        pl.BlockSpec((C, HB * RET_DK), lambda h, n: (nn(n), h)),
        pl.BlockSpec((C, HB * RET_DK), lambda h, n: (nn(n), G + h)),
        pl.BlockSpec((C, HB * RET_DV), lambda h, n: (nn(n), G + h)),
        pl.BlockSpec((C, HB * RET_DV), lambda h, n: (nn(n), 2 * G + h)),
        pl.BlockSpec((C, RET_DK), lambda h, n: (nn(n), 0)),
        pl.BlockSpec((C, RET_DK), lambda h, n: (nn(n), 0)),
        pl.BlockSpec((1, HB * RET_DV), lambda h, n: (0, h)),
        pl.BlockSpec((HB, C, C), lambda h, n: (h, 0, 0)),
        pl.BlockSpec((HB, C, RET_DV), lambda h, n: (h, 0, 0)),
        pl.BlockSpec((HB, C, RET_DK), lambda h, n: (h, 0, 0)),
        pl.BlockSpec((HB, 1, RET_DV), lambda h, n: (h, 0, 0)),
    ]


def _ret_fwd(proj, gn_g, tabs, H, T, rider=None):
    C, NC = RET_CHUNK, T // RET_CHUNK
    HB = min(RET_HB, H)
    dm, xi, zeta, gc, cosf, sins = tabs
    scale = RET_DK ** -0.5

    def body(q_ref, k_ref, v_ref, g_ref, cos_ref, sin_ref, gn_ref, dm_ref, xi_ref, zt_ref, gc_ref,
             r_ref, ri_ref, st_ref, state):
        @pl.when(pl.program_id(1) == 0)
        def _():
            state[...] = jnp.zeros_like(state)

        cs, sn = cos_ref[...], sin_ref[...]
        hs = range(HB)
        qk = [slice(h * RET_DK, (h + 1) * RET_DK) for h in hs]
        vv = [slice(h * RET_DV, (h + 1) * RET_DV) for h in hs]
        kr = [_rot(k_ref[:, qk[h]].astype(F32), cs, sn) for h in hs]
        qb = [(_rot(q_ref[:, qk[h]].astype(F32), cs, sn) * scale).astype(BF16) for h in hs]
        kb = [kr[h].astype(BF16) for h in hs]
        kzb = [(kr[h] * zt_ref[h]).astype(BF16) for h in hs]
        vb = [v_ref[:, vv[h]].astype(BF16) for h in hs]
        st = [state[h] for h in hs]
        stb = [st[h].astype(BF16) for h in hs]
        sb = [(_dot(qb[h], kb[h], tb=True) * dm_ref[h]).astype(BF16) for h in hs]
        cross = [_dot(qb[h], stb[h]) for h in hs]
        kv = [_dot(kzb[h], vb[h], ta=True) for h in hs]
        intra = [_dot(sb[h], vb[h]) for h in hs]
        for h in hs:
            st_ref[h] = stb[h]
            state[h] = gc_ref[h] * st[h] + kv[h]
        for h in hs:
            r = intra[h] + cross[h] * xi_ref[h]
            rhat, _ = _ln_stats(r)
            g = g_ref[:, vv[h]].astype(F32)
            r_ref[:, vv[h]] = r
            ri_ref[:, vv[h]] = (g * _sigmoid(g) * (rhat * gn_ref[:, vv[h]])).astype(BF16)

    VW = H * RET_DV
    return _hosted_call(
        body, rider, name="ret_fwd", grid=(H // HB, NC), in_specs=_ret_specs(H, HB, False, NC),
        out_specs=[pl.BlockSpec((C, HB * RET_DV), lambda h, n: (n, h)),
                   pl.BlockSpec((C, HB * RET_DV), lambda h, n: (n, h)),
                   pl.BlockSpec((HB, None, RET_DK, RET_DV), lambda h, n: (h, n, 0, 0))],
        out_shape=[jax.ShapeDtypeStruct((T, VW), F32), jax.ShapeDtypeStruct((T, VW), BF16),
                   jax.ShapeDtypeStruct((H, NC, RET_DK, RET_DV), BF16)],
        scratch=[pltpu.VMEM((HB, RET_DK, RET_DV), F32)],
        args=[proj, proj, proj, proj, cosf, sins, gn_g, dm, xi, zeta, gc])


def _hosted_call(body, rider, *, name, grid, in_specs, out_specs, out_shape, scratch, args):
    n_in, n_out, n_scr = len(args), len(out_shape), len(scratch)
    if rider is None:
        hosted = body
    else:
        n_ri, n_ro = len(rider.ins), len(rider.out_shape)
        in_specs, out_specs = in_specs + [ANY] * n_ri, out_specs + [ANY] * n_ro
        args, out_shape, scratch = args + rider.ins, out_shape + rider.out_shape, scratch + rider.scratch

        def hosted(*refs):
            o0, s0 = n_in + n_ri, n_in + n_ri + n_out + n_ro
            step = pl.program_id(0) * grid[1] + pl.program_id(1)
            ride = (step, grid[0] * grid[1], refs[n_in:o0], refs[o0 + n_out:s0], refs[s0 + n_scr:])
            rider.begin(*ride)
            body(*refs[:n_in], *refs[o0:o0 + n_out], *refs[s0:s0 + n_scr])
            rider.end(*ride)

    aliases = {} if rider is None else {n_in + p: n_out + o for p, o in rider.aliases.items()}
    res = pl.pallas_call(
        hosted, name=name, grid=grid, in_specs=in_specs, out_specs=out_specs, out_shape=out_shape,
        scratch_shapes=scratch, input_output_aliases=aliases, compiler_params=_params(("arbitrary", "arbitrary")),
    )(*args)
    if rider is not None:
        rider.results = res[n_out:]
    return res[:n_out]


def _ret_bwd(dri, r, states, proj, gn_g, tabs, H, T, in_w, rider=None):
    C, NC = RET_CHUNK, T // RET_CHUNK
    HB = min(RET_HB, H)
    dm, xi, zeta, gc, cosf, sins = tabs
    scale = RET_DK ** -0.5

    def body(q_ref, k_ref, v_ref, g_ref, cos_ref, sin_ref, gn_ref, dm_ref, xi_ref, zt_ref, gc_ref,
             dri_ref, r_ref, st_ref, dp_ref, dgn_ref, dstate):
        @pl.when(pl.program_id(1) == 0)
        def _():
            dstate[...] = jnp.zeros_like(dstate)
            dgn_ref[...] = jnp.zeros_like(dgn_ref)

        cs, sn = cos_ref[...], sin_ref[...]
        hs = range(HB)
        qk = [slice(h * RET_DK, (h + 1) * RET_DK) for h in hs]
        vv = [slice(h * RET_DV, (h + 1) * RET_DV) for h in hs]
        qr = [_rot(q_ref[:, qk[h]].astype(F32), cs, sn) * scale for h in hs]
        kr = [_rot(k_ref[:, qk[h]].astype(F32), cs, sn) for h in hs]
        qb = [qr[h].astype(BF16) for h in hs]
        kb = [kr[h].astype(BF16) for h in hs]
        vb = [v_ref[:, vv[h]].astype(BF16) for h in hs]
        qxb = [(qr[h] * xi_ref[h, :, :RET_DK]).astype(BF16) for h in hs]
        kzb = [(kr[h] * zt_ref[h]).astype(BF16) for h in hs]
        drb = []
        for h in hs:
            rhat, rstd = _ln_stats(r_ref[:, vv[h]])
            g, gn, dpre = g_ref[:, vv[h]].astype(F32), gn_ref[:, vv[h]], dri_ref[:, vv[h]]
            sg = _sigmoid(g)
            dp_ref[:, 2 * QW + VW + h * RET_DV:2 * QW + VW + (h + 1) * RET_DV] = (
                dpre * (rhat * gn) * _dsilu(g, sg)).astype(BF16)
            drn = dpre * (g * sg)
            dgn_ref[:, vv[h]] += _colsum(drn * rhat)
            drb.append(_ln_bwd_math(drn, rhat, rstd, gn).astype(BF16))
        ds1 = [dstate[h] for h in hs]
        ds1b = [ds1[h].astype(BF16) for h in hs]
        sb = [(_dot(qb[h], kb[h], tb=True) * dm_ref[h]).astype(BF16) for h in hs]
        dsb = [(_dot(drb[h], vb[h], tb=True) * dm_ref[h]).astype(BF16) for h in hs]
        dq_x = [_dot(drb[h], st_ref[h], tb=True) for h in hs]
        dk_x = [_dot(vb[h], ds1b[h], tb=True) for h in hs]
        dv_x = [_dot(kzb[h], ds1b[h]) for h in hs]
        dst = [_dot(qxb[h], drb[h], ta=True) for h in hs]
        for h in hs:
            dstate[h] = gc_ref[h] * ds1[h] + dst[h]
        dv_i = [_dot(sb[h], drb[h], ta=True) for h in hs]
        dq_i = [_dot(dsb[h], kb[h]) for h in hs]
        dk_i = [_dot(dsb[h], qb[h], ta=True) for h in hs]
        for h in hs:
            dp_ref[:, 2 * QW + h * RET_DV:2 * QW + (h + 1) * RET_DV] = (dv_i[h] + dv_x[h]).astype(BF16)
            dq = dq_i[h] + dq_x[h] * xi_ref[h, :, :RET_DK]
            dk = dk_i[h] + dk_x[h] * zt_ref[h]
            dp_ref[:, qk[h]] = _rot_bwd(dq * scale, cs, sn).astype(BF16)
            dp_ref[:, QW + h * RET_DK:QW + (h + 1) * RET_DK] = _rot_bwd(dk, cs, sn).astype(BF16)

    VW, QW = H * RET_DV, H * RET_DK
    rv = lambda n: NC - 1 - n
    in_specs = _ret_specs(H, HB, True, NC) + [
        pl.BlockSpec((C, HB * RET_DV), lambda h, n: (rv(n), h)),
        pl.BlockSpec((C, HB * RET_DV), lambda h, n: (rv(n), h)),
        pl.BlockSpec((HB, None, RET_DK, RET_DV), lambda h, n: (h, rv(n), 0, 0)),
    ]
    assert HB == H
    return _hosted_call(
        body, rider, name="ret_bwd", grid=(1, NC), in_specs=in_specs,
        out_specs=[pl.BlockSpec((C, 2 * QW + 2 * VW), lambda h, n: (rv(n), 0)),
                   pl.BlockSpec((1, VW), lambda h, n: (0, 0))],
        out_shape=[jax.ShapeDtypeStruct((T, in_w), BF16), jax.ShapeDtypeStruct((1, VW), F32)],
        scratch=[pltpu.VMEM((HB, RET_DK, RET_DV), F32)],
        args=[proj, proj, proj, proj, cosf, sins, gn_g, dm, xi, zeta, gc, dri, r, states])


CONV_CW = 128
CONV_TB = 512


def _conv_fwd(proj, kpad, bias, off_a, CC, T, rider=None):
    tb, cw = min(CONV_TB, T), CONV_CW
    hb = tb // HALO
    ca, cb = off_a // cw, (off_a + CC) // cw

    def body(a_ref, b_ref, ap_ref, bp_ref, k_ref, bias_ref, u1_ref, win):
        i = pl.program_id(0)
        keep = (i > 0).astype(F32)
        win[0:HALO, :] = ap_ref[...].astype(F32) * _sigmoid(bp_ref[...].astype(F32)) * keep
        win[HALO:, :] = a_ref[...].astype(F32) * _sigmoid(b_ref[...].astype(F32))
        acc = jnp.broadcast_to(bias_ref[...], (tb, cw))
        for w in range(CONV_WIDTH):
            acc = acc + k_ref[w:w + 1, :] * win[pl.ds(HALO - (CONV_WIDTH - 1) + w, tb), :]
        u1_ref[...] = acc

    prev = lambda i: jnp.maximum(i * hb - 1, 0)
    (u1,) = _hosted_call(
        body, rider, name="conv_fwd", grid=(T // tb, CC // cw),
        in_specs=[pl.BlockSpec((tb, cw), lambda i, c: (i, ca + c)),
                  pl.BlockSpec((tb, cw), lambda i, c: (i, cb + c)),
                  pl.BlockSpec((HALO, cw), lambda i, c: (prev(i), ca + c)),
                  pl.BlockSpec((HALO, cw), lambda i, c: (prev(i), cb + c)),
                  pl.BlockSpec((HALO, cw), lambda i, c: (0, c)),
                  pl.BlockSpec((1, cw), lambda i, c: (0, c))],
        out_specs=[pl.BlockSpec((tb, cw), lambda i, c: (i, c))],
        out_shape=[jax.ShapeDtypeStruct((T, CC), F32)],
        scratch=[pltpu.VMEM((tb + HALO, cw), F32)],
        args=[proj, proj, proj, proj, kpad, bias])
    return u1


def _conv_bwd(du1, proj, kpad, off_a, CC, T, rider=None):
    tb, cw = min(CONV_TB, T), CONV_CW
    hb = tb // HALO
    nt = T // tb
    ca, cb = off_a // cw, (off_a + CC) // cw

    def body(d_ref, dn_ref, a_ref, b_ref, ap_ref, bp_ref, k_ref, da_ref, db_ref, dk_ref, winu, wind):
        i = pl.program_id(1)
        a, b = a_ref[...].astype(F32), b_ref[...].astype(F32)
        sgb = _sigmoid(b)
        winu[0:HALO, :] = ap_ref[...].astype(F32) * _sigmoid(bp_ref[...].astype(F32)) * (i > 0).astype(F32)
        winu[HALO:, :] = a * sgb
        d = d_ref[...]
        wind[0:tb, :] = d
        wind[tb:, :] = dn_ref[...] * (i < nt - 1).astype(F32)

        @pl.when(i == 0)
        def _():
            dk_ref[...] = jnp.zeros_like(dk_ref)

        du0 = jnp.zeros((tb, cw), F32)
        for w in range(CONV_WIDTH):
            du0 = du0 + k_ref[w:w + 1, :] * wind[pl.ds(CONV_WIDTH - 1 - w, tb), :]
            dk_ref[w:w + 1, :] += _colsum(winu[pl.ds(HALO - (CONV_WIDTH - 1) + w, tb), :] * d)
        da_ref[...] = (du0 * sgb).astype(BF16)
        db_ref[...] = (du0 * a * sgb * (1.0 - sgb)).astype(BF16)

    prev = lambda i: jnp.maximum(i * hb - 1, 0)
    nxt = lambda i: jnp.minimum((i + 1) * hb, T // HALO - 1)
    return _hosted_call(
        body, rider, name="conv_bwd", grid=(CC // cw, nt),
        in_specs=[pl.BlockSpec((tb, cw), lambda c, i: (i, c)),
                  pl.BlockSpec((HALO, cw), lambda c, i: (nxt(i), c)),
                  pl.BlockSpec((tb, cw), lambda c, i: (i, ca + c)),
                  pl.BlockSpec((tb, cw), lambda c, i: (i, cb + c)),
                  pl.BlockSpec((HALO, cw), lambda c, i: (prev(i), ca + c)),
                  pl.BlockSpec((HALO, cw), lambda c, i: (prev(i), cb + c)),
                  pl.BlockSpec((HALO, cw), lambda c, i: (0, c))],
        out_specs=[pl.BlockSpec((tb, cw), lambda c, i: (i, c)),
                   pl.BlockSpec((tb, cw), lambda c, i: (i, c)),
                   pl.BlockSpec((HALO, cw), lambda c, i: (0, c))],
        out_shape=[jax.ShapeDtypeStruct((T, CC), BF16), jax.ShapeDtypeStruct((T, CC), BF16),
                   jax.ShapeDtypeStruct((HALO, CC), F32)],
        scratch=[pltpu.VMEM((tb + HALO, cw), F32), pltpu.VMEM((tb + HALO, cw), F32)],
        args=[du1, du1, proj, proj, proj, proj, kpad])


FFN1 = ('ffn1_w_gate', 'ffn1_w_up', 'ffn1_w_down')
FFN2 = ('ffn2_w_gate', 'ffn2_w_up', 'ffn2_w_down')


def _local_step(x, tgt, W, P, gdt=BF16, comm=None):
    T, D = x.shape
    G = {}
    net = _Net(comm, G)
    if comm is not None:
        W = comm.W
        first = net.gather(['ffn1_w_gate', 'ffn1_w_up'])
    (xb,) = _rows("x_to_bf16", lambda v: [v], [(x, 'r', D, 0)], [('r', D, BF16)], T=T, tb=512,
                  rider=first if comm is not None else None)
    if comm is not None:
        net.done(first)
    VW = P['ret_gn_g'].shape[1]
    H = VW // RET_DV
    QW = H * RET_DK
    CC = P['conv_b'].shape[1]
    off_glu = 2 * QW + 2 * VW
    off_gate = off_glu + 2 * CC
    ident = lambda accs: accs

    a1, b1, s1, z1 = _ffn_fwd("ffn1", xb, x, W, FFN1, net,
                              rider=net.gather(['ffn1_w_down', 'w_in'], {'w_in': (0, (3 * D) // 8, False)}),
                              rider_down=lambda: net.gather(['w_in'], {'w_in': ((3 * D) // 8, (3 * D) // 8, False)}))
    rider = net.gather(['w_in'], {'w_in': ((3 * D) // 4, D // 4, True)})
    x1, x1b, xh1, rs1 = _ln_fwd("ln1", z1, P['ln1_g'], P['ln1_b'], T, D, rider=rider)
    net.done(rider)

    rest = net.gather(['conv_k', 'w_ret_o', 'w_conv_o', 'w_out', 'ffn2_w_down'])
    (proj,) = _mm("w_in", [x1b], [W['w_in']], [[(0, 0)]], lambda accs, bias: [accs[0] + bias], [F32],
                  tm=2048, tn=0, tk=1024, extras=[(P['b_in'], 'n', 0)], i_outer=False, b3=True, rider=rest)
    net.done(rest)
    tabs = _ret_tables(H, T)
    rider = net.gather(['ffn2_w_gate'])
    r, ret_in, states = _ret_fwd(proj, P['ret_gn_g'], tabs, H, T, rider=rider)
    net.done(rider)
    kpad = jnp.pad(W['conv_k'].astype(F32), ((0, HALO - CONV_WIDTH), (0, 0)))
    u1 = _conv_fwd(proj, kpad, P['conv_b'], off_glu, CC, T)

    def conv_ln(u1, g, b):
        xhat, rstd = _ln_stats(u1)
        u2 = xhat * g + b
        return [xhat, rstd, u2 * _sigmoid(u2)]

    xhc, rsc, u3 = _rows("conv_ln", conv_ln, [(u1, 'r', CC, 0), (P['conv_ln_g'], 'v', CC, 0), (P['conv_ln_b'], 'v', CC, 0)],
                         [('r', CC, F32), ('c', 1, F32), ('r', CC, BF16)], T=T, tb=512)
    rider = net.gather(['ffn2_w_up'])
    (ret_out,) = _mm("ret_o", [ret_in], [W['w_ret_o']], [[(0, 0)]], ident, [F32], tm=1024, tn=1024, tk=2048,
                     rider=rider)
    net.done(rider)

    def epi_merge(accs, ret_out, gr, gc):
        conv_out = accs[0]
        return [conv_out, _sigmoid(gr) * ret_out + _sigmoid(gc) * conv_out]

    conv_out, merged = _mm("conv_o_merge", [u3], [W['w_conv_o']], [[(0, 0)]], epi_merge, [F32, BF16],
                           tm=512, tn=D, tk=1024, epi_rows=256,
                           extras=[(ret_out, 'mn', 0), (proj, 'mn', off_gate), (proj, 'mn', off_gate + D)])
    (z2,) = _mm("w_out", [merged], [W['w_out']], [[(0, 0)]], lambda accs, xr: [ALPHA * xr + accs[0]], [F32],
                tm=1024, tn=1024, tk=1024, extras=[(x1, 'mn', 0)])
    x2, x2b, xh2, rs2 = _ln_fwd("ln2", z2, P['ln2_g'], P['ln2_b'], T, D)
    a2, b2, s2, z3 = _ffn_fwd("ffn2", x2b, x2, W, FFN2, net)
    dz3, dz3h, g_ln3_g, g_ln3_b, loss = _ln_loss_bwd("ln3_loss", z3, P['ln3_g'], P['ln3_b'], tgt, T, D)

    S = {'ln3_g': g_ln3_g, 'ln3_b': g_ln3_b}
    dy2 = _ffn_bwd("ffn2b", dz3h, dz3, x2b, a2, b2, s2, W, FFN2, gdt, G, net, 'down')
    rider = net.to_sibling(['ffn2_w_gate', 'ffn2_w_up'])
    dz2, dz2b, S['ln2_g'], S['ln2_b'] = _ln_bwd("ln2b", dy2, xh2, rs2, P['ln2_g'], 1.0, T, D, rider=rider)
    net.done(rider)
    net.pairsum(['ffn2_w_gate', 'ffn2_w_up'])

    (G['w_out'],) = _mm("d_w_out", [merged], [dz2b], [[(0, 0)]], ident, [gdt], ta=True, tm=1024, tn=1024, tk=1024)

    def epi_dmerge(accs, ret_out, conv_out, gr, gc):
        dm_ = accs[0]
        sr, sc = _sigmoid(gr), _sigmoid(gc)
        return [dm_ * sr, dm_ * sc, dm_ * ret_out * sr * (1.0 - sr), dm_ * conv_out * sc * (1.0 - sc)]

    rider = net.to_sibling(['w_out'])
    dret_out, dconv_out, dgate_r, dgate_c = _mm(
        "d_merge", [dz2b], [W['w_out']], [[(0, 0)]], epi_dmerge, [BF16, BF16, BF16, BF16], tb=True,
        tm=512, tn=D, tk=1024, epi_rows=256, rider=rider,
        extras=[(ret_out, 'mn', 0), (conv_out, 'mn', 0), (proj, 'mn', off_gate), (proj, 'mn', off_gate + D)])
    net.done(rider)
    (G['w_ret_o'],) = _mm("d_w_ret_o", [ret_in], [dret_out], [[(0, 0)]], ident, [gdt], ta=True, tm=1024, tn=1024, tk=1024)
    (G['w_conv_o'],) = _mm("d_w_conv_o", [u3], [dconv_out], [[(0, 0)]], ident, [gdt], ta=True, tm=1024, tn=1024, tk=1024)
    rider = net.to_sibling(['w_ret_o', 'w_conv_o'])
    (dri,) = _mm("d_ret_in", [dret_out], [W['w_ret_o']], [[(0, 0)]], ident, [F32], tb=True, tm=1024, tn=1024, tk=1024,
                 rider=rider)
    net.done(rider)
    net.pairsum(['w_out', 'w_ret_o', 'w_conv_o'])
    rider = net.to_owner(['ffn2_w_gate', 'ffn2_w_up'])
    dproj, S['ret_gn_g'] = _ret_bwd(dri, r, states, proj, P['ret_gn_g'], tabs, H, T, proj.shape[1], rider=rider)
    net.done(rider)

    def epi_du2(accs, xhat, g, b):
        u2 = xhat * g + b
        return [accs[0] * _dsilu(u2, _sigmoid(u2))]

    (du2,) = _mm("d_u3", [dconv_out], [W['w_conv_o']], [[(0, 0)]], epi_du2, [F32], tb=True, tm=512, tn=CC, tk=1024, epi_rows=256,
                 extras=[(xhc, 'mn', 0), (P['conv_ln_g'], 'n', 0), (P['conv_ln_b'], 'n', 0)])

    def conv_ln_bwd(du2, xhat, rstd, g):
        du1 = _ln_bwd_math(du2, xhat, rstd, g)
        return [du1, _colsum(du2 * xhat), _colsum(du2), _colsum(du1)]

    du1, S['conv_ln_g'], S['conv_ln_b'], S['conv_b'] = _rows(
        "conv_ln_bwd", conv_ln_bwd, [(du2, 'r', CC, 0), (xhc, 'r', CC, 0), (rsc, 'r', 1, 0), (P['conv_ln_g'], 'v', CC, 0)],
        [('r', CC, F32), ('a', CC, F32), ('a', CC, F32), ('a', CC, F32)], T=T, tb=512)
    dglu_a, dglu_b, dkpad = _conv_bwd(du1, proj, kpad, off_glu, CC, T)
    G['conv_k'] = dkpad[:CONV_WIDTH].astype(gdt)

    for off, piece in ((off_glu, dglu_a), (off_glu + CC, dglu_b), (off_gate, dgate_r), (off_gate + D, dgate_c)):
        dproj = lax.dynamic_update_slice(dproj, piece, (0, off))
    IN_W = dproj.shape[1]
    rider = net.both(net.exchange(['conv_k']),
                     net.to_owner(['w_out', 'w_ret_o', 'w_conv_o']))
    G['w_in'], S['b_in'] = _mm("d_w_in", [x1b], [dproj], [[(0, 0)]], ident, [gdt], ta=True, o3=True, bsum=True,
                               tm=1024, tn=W['w_in'].shape[2], tk=1024, rider=rider)
    net.done(rider)
    cuts = [0, (3 * D) // 16, (43 * D) // 64, D]
    w_in_rows = [{'w_in': (cuts[i], cuts[i + 1] - cuts[i], i == 2)} for i in range(3)]
    rider = net.to_sibling(['w_in'])
    (dy1,) = _mm("d_x1", [dproj], [W['w_in']], [[(0, 0)]], lambda accs, dzr: [ALPHA * dzr + accs[0]], [F32], tb=True,
                 b3=True, tm=1024, tn=1024, tk=0, extras=[(dz2, 'mn', 0)], rider=rider)
    net.done(rider)
    net.pairsum(['w_in'])
    rider = net.to_owner(['w_in'], w_in_rows[0])
    dz1, dz1h, S['ln1_g'], S['ln1_b'] = _ln_bwd("ln1b", dy1, xh1, rs1, P['ln1_g'], 0.5, T, D, rider=rider)
    net.done(rider)
    grad_x = _ffn_bwd("ffn1b", dz1h, dz1, xb, a1, b1, s1, W, FFN1, gdt, G, net, 'all',
                      pre=(lambda: net.to_owner(['w_in'], w_in_rows[1]), lambda: net.to_owner(['w_in'], w_in_rows[2])))
    return loss, grad_x, G, S


def _coords():
    return lax.axis_index("x"), lax.axis_index("y"), lax.axis_index("c")


def _flip(k, x, y, c):
    return (1 - x if k & 4 else x, 1 - y if k & 2 else y, 1 - c if k & 1 else c)


def _lin(p):
    return 4 * p[0] + 2 * p[1] + p[2]


class _Rider:
    def __init__(self, ins, out_shape, rows=None, fill=None):
        nb = len(ins)
        self.rows = rows or [None] * nb
        fill = fill or [None] * nb
        self.aliases = {nb + i: w for i, w in enumerate(w for w in range(nb) if fill[w] is not None)}
        self.ins = list(ins) + [f for f in fill if f is not None]
        self.out_shape, self.results = list(out_shape), None
        self.scratch = [pltpu.SemaphoreType.DMA((8 * nb,)), pltpu.SemaphoreType.DMA((8 * nb,)),
                        pltpu.SemaphoreType.DMA((nb,))]

    def span(self, w, ref, *slot, half=None):
        rows = self.rows[w]
        if half is not None:
            first, count = rows if rows is not None else (0, self.out_shape[w].shape[1])
            rows = (first + half * (count // 2), count // 2)
        if rows is None:
            return ref.at[slot] if slot else ref
        return ref.at[(*slot, pl.ds(*rows))]

    def begin(self, step, n_steps, ins, outs, sems):
        @pl.when(step == 0)
        def _():
            self.start(ins, outs, sems)

        @pl.when(step == min(n_steps - 1, (5 * n_steps) // 8))
        def _():
            self.relay(ins, outs, sems)

        @pl.when(step == n_steps - 1)
        def _():
            self.mid(ins, outs, sems)

    def end(self, step, n_steps, ins, outs, sems):
        @pl.when(step == n_steps - 1)
        def _():
            self.finish(ins, outs, sems)

    def relay(self, ins, outs, sems):
        pass

    def mid(self, ins, outs, sems):
        pass


class _Riders:
    def __init__(self, riders):
        self.riders = list(riders)
        self.ins = [a for r in self.riders for a in r.ins]
        self.out_shape = [o for r in self.riders for o in r.out_shape]
        self.scratch = [c for r in self.riders for c in r.scratch]
        self.aliases, n_in, n_out = {}, 0, 0
        for r in self.riders:
            self.aliases.update({n_in + p: n_out + o for p, o in r.aliases.items()})
            n_in, n_out = n_in + len(r.ins), n_out + len(r.out_shape)

    def _each(self, ins, outs, sems):
        i = o = c = 0
        for r in self.riders:
            yield r, ins[i:i + len(r.ins)], outs[o:o + len(r.out_shape)], sems[c:c + len(r.scratch)]
            i, o, c = i + len(r.ins), o + len(r.out_shape), c + len(r.scratch)

    def begin(self, step, n_steps, ins, outs, sems):
        for r, i, o, c in self._each(ins, outs, sems):
            r.begin(step, n_steps, i, o, c)

    def end(self, step, n_steps, ins, outs, sems):
        for r, i, o, c in self._each(ins, outs, sems):
            r.end(step, n_steps, i, o, c)

    @property
    def results(self):
        return [x for r in self.riders for x in r.results]

    @results.setter
    def results(self, res):
        for r, _, o, _ in self._each([], list(res), []):
            r.results = o


class _GatherRider(_Rider):
    def __init__(self, blks, rows=None, fill=None):
        super().__init__(blks, [jax.ShapeDtypeStruct((N_DEV,) + b.shape, b.dtype) for b in blks], rows, fill)
        counts = [(r[1] if r is not None else b.shape[0]) for r, b in zip(self.rows, blks)]
        self.halves = [n % 32 == 0 for n in counts]

    def _copies(self, x_refs, out_refs, sems):
        nb = len(self.out_shape)
        send_sems, recv_sems, local_sems = sems
        x, y, c = _coords()
        me, sib = (x, y, c), (x, y, 1 - c)
        xn, yn, dg = _flip(4, x, y, c), _flip(2, x, y, c), _flip(6, x, y, c)
        plans = []
        for w in range(nb):
            own = self.span(w, x_refs[w])

            def copy(k, block, to, src=None, half=None, w=w):
                slot = self.span(w, out_refs[w], _lin(block), half=half)
                return pltpu.make_async_remote_copy(
                    src_ref=slot if src is None else src, dst_ref=slot, send_sem=send_sems.at[k * nb + w],
                    recv_sem=recv_sems.at[k * nb + w], device_id=to, device_id_type=MESH)

            mine = pltpu.make_async_copy(own, self.span(w, out_refs[w], _lin(me)), local_sems.at[w])
            first = [copy(0, me, sib, src=own), copy(1, me, xn, src=own), copy(2, me, yn, src=own)]
            if self.halves[w]:
                relay = [(copy(1, xn, me), [copy(3, xn, yn, half=0), copy(5, xn, sib)]),
                         (copy(2, yn, me), [copy(4, yn, xn, half=1), copy(6, yn, sib)])]
                last = [(copy(3, dg, me, half=0), []), (copy(4, dg, me, half=1), [copy(7, dg, sib)])]
            else:
                first.append(copy(3, me, dg, src=own))
                relay = [(copy(1, xn, me), [copy(5, xn, sib)]), (copy(2, yn, me), [copy(6, yn, sib)])]
                last = [(copy(3, dg, me), [copy(7, dg, sib)])]
            other = lambda p: (p[0], p[1], 1 - c)
            from_sib = [copy(0, sib, me), copy(5, other(xn), me), copy(6, other(yn), me), copy(7, other(dg), me)]
            plans.append((mine, first, relay, last, from_sib))
        return plans

    def start(self, ins, outs, sems):
        for mine, first, _, _, _ in self._copies(ins, outs, sems):
            for cp in [mine] + first:
                cp.start()

    def relay(self, ins, outs, sems):
        for _, _, relay, _, _ in self._copies(ins, outs, sems):
            for arrival, released in relay:
                arrival.wait_recv()
                for cp in released:
                    cp.start()

    def mid(self, ins, outs, sems):
        for _, _, _, last, _ in self._copies(ins, outs, sems):
            for arrival, released in last:
                arrival.wait_recv()
                for cp in released:
                    cp.start()

    def finish(self, ins, outs, sems):
        for mine, first, relay, last, from_sib in self._copies(ins, outs, sems):
            for cp in from_sib:
                cp.wait_recv()
            for cp in first + [cp for _, released in relay + last for cp in released]:
                cp.wait_send()
            mine.wait()


class _ExchangeRider(_Rider):
    def __init__(self, gs, rows=None, fill=None):
        super().__init__(gs, [jax.ShapeDtypeStruct(g.shape, g.dtype) for g in gs], rows, fill)

    def _copies(self, g_refs, out_refs, sems):
        nb = len(self.out_shape)
        send_sems, recv_sems, local_sems = sems
        x, y, c = _coords()
        me = _lin((x, y, c))

        def copy(k, w, landing):
            peer = _flip(k, x, y, c)
            src, dst = (me, _lin(peer)) if landing else (_lin(peer), me)
            return pltpu.make_async_remote_copy(
                src_ref=self.span(w, g_refs[w], src), dst_ref=self.span(w, out_refs[w], dst),
                send_sem=send_sems.at[(k - 1) * nb + w], recv_sem=recv_sems.at[(k - 1) * nb + w],
                device_id=peer, device_id_type=MESH)

        mines = [pltpu.make_async_copy(self.span(w, g_refs[w], me), self.span(w, out_refs[w], me), local_sems.at[w])
                 for w in range(nb)]
        sends = [copy(k, w, False) for w in range(nb) for k in range(1, N_DEV)]
        landings = [copy(k, w, True) for w in range(nb) for k in range(1, N_DEV)]
        return mines, sends, landings

    def start(self, ins, outs, sems):
        mines, sends, _ = self._copies(ins, outs, sems)
        for cp in mines + sends:
            cp.start()

    def finish(self, ins, outs, sems):
        mines, sends, landings = self._copies(ins, outs, sems)
        for cp in landings:
            cp.wait_recv()
        for cp in sends:
            cp.wait_send()
        for mine in mines:
            mine.wait()


class _SiblingRider(_Rider):
    def __init__(self, gs):
        super().__init__(gs, [jax.ShapeDtypeStruct((4,) + g.shape[1:], g.dtype) for g in gs])

    def _copies(self, g_refs, out_refs, sems, landing):
        nb = len(self.out_shape)
        send_sems, recv_sems, _ = sems
        x, y, c = _coords()
        whose = c if landing else 1 - c
        return [pltpu.make_async_remote_copy(
            src_ref=g_refs[w].at[2 * q + whose], dst_ref=out_refs[w].at[q], send_sem=send_sems.at[q * nb + w],
            recv_sem=recv_sems.at[q * nb + w], device_id=(x, y, 1 - c), device_id_type=MESH)
            for w in range(nb) for q in range(4)]

    def start(self, ins, outs, sems):
        for cp in self._copies(ins, outs, sems, False):
            cp.start()

    def finish(self, ins, outs, sems):
        for cp in self._copies(ins, outs, sems, True):
            cp.wait_recv()
        for cp in self._copies(ins, outs, sems, False):
            cp.wait_send()


class _ChipRider(_Rider):
    FLIPS = (4, 2, 6)

    def __init__(self, ps, rows=None, fill=None):
        super().__init__(ps, [jax.ShapeDtypeStruct(p.shape, p.dtype) for p in ps], rows, fill)

    def _copies(self, p_refs, out_refs, sems):
        nb = len(self.out_shape)
        send_sems, recv_sems, local_sems = sems
        x, y, c = _coords()
        my_chip = 2 * x + y

        def copy(j, w, landing):
            peer = _flip(self.FLIPS[j], x, y, c)
            peer_chip = 2 * peer[0] + peer[1]
            src, dst = (my_chip, peer_chip) if landing else (peer_chip, my_chip)
            return pltpu.make_async_remote_copy(
                src_ref=self.span(w, p_refs[w], src), dst_ref=self.span(w, out_refs[w], dst),
                send_sem=send_sems.at[j * nb + w], recv_sem=recv_sems.at[j * nb + w],
                device_id=peer, device_id_type=MESH)

        mines = [pltpu.make_async_copy(self.span(w, p_refs[w], my_chip), self.span(w, out_refs[w], my_chip),
                                       local_sems.at[w]) for w in range(nb)]
        sends = [copy(j, w, False) for w in range(nb) for j in range(3)]
        landings = [copy(j, w, True) for w in range(nb) for j in range(3)]
        return mines, sends, landings

    def start(self, ins, outs, sems):
        mines, sends, _ = self._copies(ins, outs, sems)
        for cp in mines + sends:
            cp.start()

    def finish(self, ins, outs, sems):
        mines, sends, landings = self._copies(ins, outs, sems)
        for cp in landings:
            cp.wait_recv()
        for cp in sends:
            cp.wait_send()
        for mine in mines:
            mine.wait()


def _pairsum(name, g, land):
    _, r, cols = g.shape
    tb = r if r % 16 else _row_tile(r, 16, max(16, (1024 * 1024) // cols))
    core = lax.axis_index("c").astype(jnp.int32).reshape(1)

    def body(core_ref, g_ref, l_ref, o_ref):
        o_ref[...] = (g_ref[...].astype(F32) + l_ref[...].astype(F32)).astype(o_ref.dtype)

    return pl.pallas_call(
        body, name=name, out_shape=jax.ShapeDtypeStruct((4, r, cols), g.dtype),
        grid_spec=pltpu.PrefetchScalarGridSpec(
            num_scalar_prefetch=1, grid=(4, r // tb),
            in_specs=[pl.BlockSpec((None, None, tb, cols), lambda q, i, core_ref: (q, core_ref[0], i, 0)),
                      pl.BlockSpec((None, tb, cols), lambda q, i, core_ref: (q, i, 0))],
            out_specs=pl.BlockSpec((None, tb, cols), lambda q, i, core_ref: (q, i, 0))),
        compiler_params=_params(("arbitrary", "arbitrary")),
    )(core, g.reshape(4, 2, r, cols), land)


def _run_rider(name, rider):
    n_in, n_out = len(rider.ins), len(rider.out_shape)

    def body(*refs):
        ride = (refs[:n_in], refs[n_in:n_in + n_out], refs[n_in + n_out:])
        rider.start(*ride)
        rider.relay(*ride)
        rider.mid(*ride)
        rider.finish(*ride)

    rider.results = pl.pallas_call(
        body, name=name, out_shape=rider.out_shape, in_specs=[ANY] * n_in, out_specs=[ANY] * n_out,
        scratch_shapes=rider.scratch, input_output_aliases=dict(rider.aliases),
        compiler_params=pltpu.CompilerParams(has_side_effects=True),
    )(*rider.ins)
    return rider.results


def _as_matrix(name, g):
    if name == 'w_in':
        return g
    if name in COL_SHARDED:
        return jnp.transpose(g, (1, 0, 2)).reshape(g.shape[1], N_DEV * g.shape[2])
    return g.reshape(N_DEV * g.shape[1], g.shape[2])


def _by_owner(name, g):
    if name == 'w_in':
        return g
    if name in COL_SHARDED:
        return jnp.transpose(g.reshape(g.shape[0], N_DEV, g.shape[1] // N_DEV), (1, 0, 2))
    return g.reshape(N_DEV, g.shape[0] // N_DEV, g.shape[1])


class _Comm:
    def __init__(self, shards):
        self.shards, self.W, self.parts, self.partial, self.sent = shards, {}, {}, {}, {}
        self.from_sibling, self.pairs = {}, {}

    def _ride(self, cls, names, srcs, part, sink):
        part = part or {}
        rider = cls(srcs, rows=[part[n][:2] if n in part else None for n in names],
                    fill=[self.partial.pop((sink, n), None) for n in names])
        rider.names, rider.sink = names, sink
        rider.unfinished = {n for n in names if n in part and not part[n][2]}
        return rider

    def gather(self, names, part=None):
        return self._ride(_GatherRider, names, [self.shards[n] for n in names], part, 'W')

    def exchange(self, names, G, part=None):
        for n in names:
            if n not in self.sent:
                self.sent[n] = _by_owner(n, G[n])
        return self._ride(_ExchangeRider, names, [self.sent[n] for n in names], part, 'parts')

    def to_sibling(self, names, G):
        for n in names:
            self.sent[n] = _by_owner(n, G[n])
        rider = _SiblingRider([self.sent[n] for n in names])
        rider.names, rider.sink, rider.unfinished = names, 'sibling', set()
        return rider

    def pairsum(self, names):
        for n in names:
            self.pairs[n] = _pairsum("pairsum_" + n, self.sent[n], self.from_sibling.pop(n))

    def to_owner(self, names, part=None):
        return self._ride(_ChipRider, names, [self.pairs[n] for n in names], part, 'parts')

    def collect(self, rider):
        for n, res in zip(rider.names, rider.results):
            if n in rider.unfinished:
                self.partial[(rider.sink, n)] = res
            elif rider.sink == 'W':
                self.W[n] = _as_matrix(n, res)
            elif rider.sink == 'sibling':
                self.from_sibling[n] = res
            else:
                self.parts[n] = res


def _adamw_math(p_ref, w_ref, m_ref, v_ref, g_ref, d_ref, nm_ref, nv_ref):
    c1 = 1.0 - ADAM_B1 ** ADAM_STEP
    c2 = 1.0 - ADAM_B2 ** ADAM_STEP
    g = p_ref[0].astype(F32)
    for s in range(1, p_ref.shape[0]):
        g = g + p_ref[s].astype(F32)
    nm = ADAM_B1 * m_ref[...] + (1.0 - ADAM_B1) * g
    nv = ADAM_B2 * v_ref[...] + (1.0 - ADAM_B2) * (g * g)
    g_ref[...] = g
    nm_ref[...] = nm
    nv_ref[...] = nv
    d_ref[...] = -ADAM_LR * ((nm / c1) / (jnp.sqrt(nv / c2) + ADAM_EPS) + ADAM_WD * w_ref[...])


def _adamw_vectors(parts, ws, ms, vs, loss_parts):
    k = len(ws)

    def body(*refs):
        for i in range(k):
            _adamw_math(refs[i], refs[k + i], refs[2 * k + i], refs[3 * k + i], *refs[4 * k + 1 + 4 * i:4 * k + 5 + 4 * i])
        lp, lo = refs[4 * k], refs[8 * k + 1]
        lo[...] = functools.reduce(jnp.add, [lp[s] for s in range(lp.shape[0])])

    return pl.pallas_call(
        body, name="adamw_vectors",
        out_shape=[jax.ShapeDtypeStruct(w.shape, F32) for w in ws for _ in range(4)] + [jax.ShapeDtypeStruct((1, 128), F32)],
        compiler_params=_params(),
    )(*parts, *ws, *ms, *vs, loss_parts)


def _adamw(name, parts, w, m, v, tb):
    n, R, Wd = parts.shape
    assert R % tb == 0
    body = functools.partial(_adamw_math)

    row = pl.BlockSpec((tb, Wd), lambda i: (i, 0))
    return pl.pallas_call(
        body, name=name, grid=(R // tb,),
        in_specs=[pl.BlockSpec((n, tb, Wd), lambda i: (0, i, 0)), row, row, row],
        out_specs=[row, row, row, row], out_shape=[jax.ShapeDtypeStruct((R, Wd), F32)] * 4,
        compiler_params=_params(("arbitrary",)),
    )(parts, w, m, v)


def _row_tile(R, unit, cap):
    best = unit
    for t in range(unit, cap + 1, unit):
        if R % t == 0:
            best = t
    return best


def kernel(x, ffn1_w_gate, ffn1_w_up, ffn1_w_down, ln1_g, ln1_b, w_in, b_in, ret_gn_g, conv_k, conv_b, conv_ln_g, conv_ln_b, w_ret_o, w_conv_o, w_out, ln2_g, ln2_b, ffn2_w_gate, ffn2_w_up, ffn2_w_down, ln3_g, ln3_b, loss_target, m_ffn1_w_gate, m_ffn1_w_up, m_ffn1_w_down, m_ln1_g, m_ln1_b, m_w_in, m_b_in, m_ret_gn_g, m_conv_k, m_conv_b, m_conv_ln_g, m_conv_ln_b, m_w_ret_o, m_w_conv_o, m_w_out, m_ln2_g, m_ln2_b, m_ffn2_w_gate, m_ffn2_w_up, m_ffn2_w_down, m_ln3_g, m_ln3_b, v_ffn1_w_gate, v_ffn1_w_up, v_ffn1_w_down, v_ln1_g, v_ln1_b, v_w_in, v_b_in, v_ret_gn_g, v_conv_k, v_conv_b, v_conv_ln_g, v_conv_ln_b, v_w_ret_o, v_w_conv_o, v_w_out, v_ln2_g, v_ln2_b, v_ffn2_w_gate, v_ffn2_w_up, v_ffn2_w_down, v_ln3_g, v_ln3_b):
    given = dict(locals())
    wts = {n: given[n] for n in WEIGHTS}
    mom = {n: given['m_' + n] for n in WEIGHTS}
    var = {n: given['v_' + n] for n in WEIGHTS}

    def shard2d(a):
        return a.reshape(a.shape[-3] * a.shape[-2] if a.ndim == 4 else a.shape[-2], a.shape[-1])

    comm = _Comm({n: shard2d(wts[n]).astype(BF16) for n in BIG})
    P = {n: wts[n].reshape(1, -1) for n in SMALL}
    loss, grad_x, _, S = _local_step(x[0], loss_target[0], None, P, comm=comm)

    parts = comm.parts
    res = {}
    for n in BIG:
        rows, cols = parts[n].shape[1:]
        tb = rows if rows % 16 else _row_tile(rows, 16, max(16, (256 * 1024) // cols))
        res[n] = _adamw("adamw_" + n, parts[n], shard2d(wts[n]), shard2d(mom[n]), shard2d(var[n]), tb)

    vec_parts = _run_rider("gather_vector_grads", _GatherRider([S[n] for n in SMALL] + [loss]))
    vec = _adamw_vectors(vec_parts[:-1], [P[n] for n in SMALL], [mom[n].reshape(1, -1) for n in SMALL],
                         [var[n].reshape(1, -1) for n in SMALL], vec_parts[-1])
    for i, n in enumerate(SMALL):
        res[n] = vec[4 * i:4 * i + 4]

    outs = [vec[-1][0, 0], grad_x[None]]
    for k in range(4):
        for n in WEIGHTS:
            outs.append(res[n][k].reshape(wts[n].shape))
    return tuple(outs)
```

```python
import functools
import math

import jax
import jax.numpy as jnp
from jax import lax
from jax.experimental import pallas as pl
from jax.experimental.pallas import tpu as pltpu

F32 = jnp.float32
BF16 = jnp.bfloat16

N_DEV = 8
LN_EPS = 1e-5
ALPHA = 2.0 ** 0.25
RET_DK = 128
RET_DV = 256
RET_CHUNK = 256
ROPE_BASE = 10000.0
CONV_WIDTH = 31
HALO = 32
ADAM_LR, ADAM_B1, ADAM_B2, ADAM_EPS, ADAM_WD, ADAM_STEP = 0.001, 0.9, 0.999, 1e-08, 0.01, 10
VMEM_LIMIT = 52 * 1024 * 1024
MESH = pl.DeviceIdType.MESH
ANY = pl.BlockSpec(memory_space=pl.ANY)

BIG = ['ffn1_w_gate', 'ffn1_w_up', 'ffn1_w_down', 'w_in', 'w_ret_o', 'w_conv_o', 'w_out',
       'ffn2_w_gate', 'ffn2_w_up', 'ffn2_w_down', 'conv_k']
COL_SHARDED = {'ffn1_w_gate', 'ffn1_w_up', 'w_in', 'ffn2_w_gate', 'ffn2_w_up', 'conv_k'}
SMALL = ['ln1_g', 'ln1_b', 'b_in', 'ret_gn_g', 'conv_b', 'conv_ln_g', 'conv_ln_b', 'ln2_g', 'ln2_b', 'ln3_g', 'ln3_b']
WEIGHTS = ['ffn1_w_gate', 'ffn1_w_up', 'ffn1_w_down', 'ln1_g', 'ln1_b', 'w_in', 'b_in', 'ret_gn_g', 'conv_k', 'conv_b',
           'conv_ln_g', 'conv_ln_b', 'w_ret_o', 'w_conv_o', 'w_out', 'ln2_g', 'ln2_b', 'ffn2_w_gate', 'ffn2_w_up',
           'ffn2_w_down', 'ln3_g', 'ln3_b']


def _params(sem=None):
    return pltpu.CompilerParams(dimension_semantics=sem, vmem_limit_bytes=VMEM_LIMIT)


def _sigmoid(x):
    return jax.nn.sigmoid(x)


def _dsilu(x, sg):
    return sg * (1.0 + x * (1.0 - sg))


def _fit(dim, want):
    if dim <= want:
        return dim
    return max(t for t in range(128, want + 1, 128) if dim % t == 0)


def _dot(a, b, ta=False, tb=False):
    dn = (((0,) if ta else (1,), (1,) if tb else (0,)), ((), ()))
    return lax.dot_general(a, b, dn, preferred_element_type=F32)


def _mm(name, As, Bs, prods, epi, out_dtypes, *, ta=False, tb=False, tm, tn, tk, extras=(), i_outer=True,
        b3=False, o3=False, rider=None, bsum=False, epi_rows=0):
    a0, b0 = As[0], Bs[0]
    M, K = (a0.shape[1], a0.shape[0]) if ta else a0.shape
    if b3:
        S, rows, cs = b0.shape
        N = rows if tb else S * cs
        assert K == (S * cs if tb else rows)
        tn, tk = (tn, cs) if tb else (cs, tk)
    else:
        N = b0.shape[0] if tb else b0.shape[1]
    tm, tn, tk = _fit(M, tm), _fit(N, tn), _fit(K, tk)
    assert M % tm == 0 and N % tn == 0 and K % tk == 0, (name, M, N, K, tm, tn, tk)
    gi, gj, gk = M // tm, N // tn, K // tk
    grid = (gi, gj, gk) if i_outer else (gj, gi, gk)

    def ij(g0, g1):
        return (g0, g1) if i_outer else (g1, g0)

    def amap(g0, g1, k):
        i, _ = ij(g0, g1)
        return (k, i) if ta else (i, k)

    def bmap(g0, g1, k):
        _, j = ij(g0, g1)
        return (j, k) if tb else (k, j)

    def bmap3(g0, g1, k):
        _, j = ij(g0, g1)
        return (k, j, 0) if tb else (j, k, 0)

    in_specs = [pl.BlockSpec((tk, tm) if ta else (tm, tk), amap) for _ in As]
    if b3:
        in_specs += [pl.BlockSpec((None, tn, tk) if tb else (None, tk, tn), bmap3) for _ in Bs]
    else:
        in_specs += [pl.BlockSpec((tn, tk) if tb else (tk, tn), bmap) for _ in Bs]
    args = list(As) + list(Bs)
    for arr, kind, coloff in extras:
        assert coloff % tn == 0
        off = coloff // tn
        if kind == 'mn':
            in_specs.append(pl.BlockSpec((tm, tn), lambda g0, g1, k, off=off: (ij(g0, g1)[0], ij(g0, g1)[1] + off)))
        else:
            in_specs.append(pl.BlockSpec((1, tn), lambda g0, g1, k, off=off: (0, ij(g0, g1)[1] + off)))
        args.append(arr)
    if o3:
        out_shape = [jax.ShapeDtypeStruct((gj, M, tn), dt) for dt in out_dtypes]
        out_specs = [pl.BlockSpec((None, tm, tn), lambda g0, g1, k: (ij(g0, g1)[1], ij(g0, g1)[0], 0))
                     for _ in out_dtypes]
    else:
        out_shape = [jax.ShapeDtypeStruct((M, N), dt) for dt in out_dtypes]
        out_specs = [pl.BlockSpec((tm, tn), lambda g0, g1, k: ij(g0, g1)) for _ in out_dtypes]
    if bsum:
        assert gi == 1 and not tb and not b3
        out_shape.append(jax.ShapeDtypeStruct((1, N), F32))
        out_specs.append(pl.BlockSpec((1, tn), lambda g0, g1, k: (0, ij(g0, g1)[1])))
    n_a, n_b, n_e, n_o = len(As), len(Bs), len(extras), len(out_shape)
    n_p = len(prods) if gk > 1 else 0
    scratch = [pltpu.VMEM((tm, tn), F32) for _ in range(n_p)]
    if rider is not None:
        in_specs, out_specs = in_specs + [ANY] * len(rider.ins), out_specs + [ANY] * len(rider.out_shape)
        args, out_shape, scratch = args + rider.ins, out_shape + rider.out_shape, scratch + rider.scratch
    n_in, n_out = len(args), len(out_shape)

    def body(*refs):
        a_refs = refs[:n_a]
        b_refs = refs[n_a:n_a + n_b]
        e_refs = refs[n_a + n_b:n_a + n_b + n_e]
        o_refs = refs[n_in:n_in + n_o]
        acc_refs = refs[n_in + n_out:n_in + n_out + n_p]
        k = pl.program_id(2)
        if rider is not None:
            step = (pl.program_id(0) * grid[1] + pl.program_id(1)) * gk + k
            ride = (step, grid[0] * grid[1] * gk, refs[n_a + n_b + n_e:n_in], refs[n_in + n_o:n_in + n_out],
                    refs[n_in + n_out + n_p:])
            rider.begin(*ride)

        def finish(accs, rows=slice(None)):
            ex = [(e[rows, :] if kind == 'mn' else e[...]).astype(F32) for e, (_, kind, _) in zip(e_refs, extras)]
            for o, r in zip(o_refs, epi(accs, *ex)):
                o[rows, :] = r.astype(o.dtype)

        if bsum:
            @pl.when(k == 0)
            def _():
                o_refs[-1][...] = jnp.zeros_like(o_refs[-1])

            o_refs[-1][...] += _colsum(b_refs[0][...].astype(F32))

        if gk == 1:
            sub = tm if ta or not epi_rows else _fit(tm, epi_rows)
            for r0 in range(0, tm, sub):
                rows = slice(None) if ta else slice(r0, r0 + sub)
                finish([functools.reduce(jnp.add, [_dot(a_refs[ai][...] if ta else a_refs[ai][rows, :],
                                                        b_refs[bi][...], ta, tb) for ai, bi in terms])
                        for terms in prods], rows)
        else:
            @pl.when(k == 0)
            def _():
                for acc in acc_refs:
                    acc[...] = jnp.zeros_like(acc)

            for p, terms in enumerate(prods):
                for ai, bi in terms:
                    acc_refs[p][...] += _dot(a_refs[ai][...], b_refs[bi][...], ta, tb)

            @pl.when(k == gk - 1)
            def _():
                finish([acc[...] for acc in acc_refs])

        if rider is not None:
            rider.end(*ride)

    aliases = {} if rider is None else {n_a + n_b + n_e + p: n_o + o for p, o in rider.aliases.items()}
    res = pl.pallas_call(
        body, name=name, grid=grid, in_specs=in_specs, out_specs=out_specs, out_shape=out_shape,
        scratch_shapes=scratch, input_output_aliases=aliases,
        compiler_params=_params(("arbitrary", "arbitrary", "arbitrary")),
    )(*args)
    if rider is not None:
        rider.results = res[n_o:]
    return res[:n_o]


def _rows(name, fn, ins, outs, *, T, tb, rider=None):
    tb = min(tb, T)
    assert T % tb == 0
    in_specs, args = [], []
    for arr, kind, width, cb in ins:
        if kind == 'r':
            in_specs.append(pl.BlockSpec((tb, width), lambda i, _, cb=cb: (i, cb)))
        else:
            in_specs.append(pl.BlockSpec((1, width), lambda i, _, cb=cb: (0, cb)))
        args.append(arr)
    out_shape, out_specs = [], []
    for kind, width, dtype in outs:
        if kind == 'r':
            out_shape.append(jax.ShapeDtypeStruct((T, width), dtype))
            out_specs.append(pl.BlockSpec((tb, width), lambda i, _: (i, 0)))
        elif kind == 'c':
            out_shape.append(jax.ShapeDtypeStruct((T, 1), dtype))
            out_specs.append(pl.BlockSpec((tb, 1), lambda i, _: (i, 0)))
        else:
            out_shape.append(jax.ShapeDtypeStruct((1, width), F32))
            out_specs.append(pl.BlockSpec((1, width), lambda i, _: (0, 0)))
    n_in = len(ins)

    def body(*refs):
        i = pl.program_id(0)
        vals = fn(*[r[...] for r in refs[:n_in]])
        for (kind, _, _), o, v in zip(outs, refs[n_in:], vals):
            if kind == 'a':
                @pl.when(i == 0)
                def _(o=o):
                    o[...] = jnp.zeros_like(o)

                o[...] += v
            else:
                o[...] = v.astype(o.dtype)

    return _hosted_call(body, rider, name=name, grid=(T // tb, 1), in_specs=in_specs, out_specs=out_specs,
                        out_shape=out_shape, scratch=[], args=args)


def _colsum(v):
    return jnp.sum(v, axis=0, keepdims=True)


def _ln_stats(z):
    mu = jnp.mean(z, axis=-1, keepdims=True)
    d = z - mu
    var = jnp.mean(d * d, axis=-1, keepdims=True)
    rstd = lax.rsqrt(var + LN_EPS)
    return d * rstd, rstd


def _ln_bwd_math(dy, xhat, rstd, g):
    dxh = dy * g
    m1 = jnp.mean(dxh, axis=-1, keepdims=True)
    m2 = jnp.mean(dxh * xhat, axis=-1, keepdims=True)
    return rstd * (dxh - m1 - xhat * m2)


def _ln_fwd(name, z, g, b, T, D, rider=None):
    def fn(z, g, b):
        xhat, rstd = _ln_stats(z)
        y = xhat * g + b
        return [y, y, xhat, rstd]

    return _rows(name, fn, [(z, 'r', D, 0), (g, 'v', D, 0), (b, 'v', D, 0)],
                 [('r', D, F32), ('r', D, BF16), ('r', D, F32), ('c', 1, F32)], T=T, tb=512, rider=rider)


def _ln_bwd(name, dy, xhat, rstd, g, scale, T, D, rider=None):
    def fn(dy, xhat, rstd, g):
        dz = _ln_bwd_math(dy, xhat, rstd, g)
        return [dz, dz * scale, _colsum(dy * xhat), _colsum(dy)]

    return _rows(name, fn, [(dy, 'r', D, 0), (xhat, 'r', D, 0), (rstd, 'r', 1, 0), (g, 'v', D, 0)],
                 [('r', D, F32), ('r', D, BF16), ('a', D, F32), ('a', D, F32)], T=T, tb=512, rider=rider)


def _ln_loss_bwd(name, z, g, b, tgt, T, D):
    def fn(z, g, b, tgt):
        xhat, rstd = _ln_stats(z)
        err = xhat * g + b - tgt
        row_loss = 0.5 * jnp.mean(err * err, axis=-1, keepdims=True)
        loss = jnp.broadcast_to(jnp.sum(row_loss, axis=0, keepdims=True), (1, 128))
        dy = err * (1.0 / D)
        dz = _ln_bwd_math(dy, xhat, rstd, g)
        return [dz, dz * 0.5, _colsum(dy * xhat), _colsum(dy), loss]

    return _rows(name, fn, [(z, 'r', D, 0), (g, 'v', D, 0), (b, 'v', D, 0), (tgt, 'r', D, 0)],
                 [('r', D, F32), ('r', D, BF16), ('a', D, F32), ('a', D, F32), ('a', 128, F32)], T=T, tb=512)


class _Net:
    def __init__(self, comm, G):
        self.comm, self.G = comm, G

    def gather(self, names, part=None):
        return self.comm.gather(names, part) if self.comm else None

    def exchange(self, names, part=None):
        return self.comm.exchange(names, self.G, part) if self.comm else None

    def to_sibling(self, names):
        return self.comm.to_sibling(names, self.G) if self.comm else None

    def pairsum(self, names):
        if self.comm:
            self.comm.pairsum(names)

    def to_owner(self, names, part=None):
        return self.comm.to_owner(names, part) if self.comm else None

    def done(self, rider):
        if rider is not None:
            for one in getattr(rider, 'riders', [rider]):
                self.comm.collect(one)

    def both(self, *riders):
        return _Riders(riders) if self.comm else None


def _ffn_fwd(tag, xb, x, W, names, net, rider=None, rider_down=None):
    def epi_gu(accs):
        a, b = accs
        return [a, b, a * _sigmoid(a) * b]

    ng, nu, nd = names
    a, b, s = _mm(tag + "_gate_up", [xb], [W[ng], W[nu]], [[(0, 0)], [(0, 1)]], epi_gu, [BF16, BF16, BF16],
                  tm=1024, tn=1408, tk=1024, rider=rider, epi_rows=256)
    net.done(rider)

    def epi_down(accs, xres):
        return [ALPHA * xres + 0.5 * accs[0]]

    rider_down = rider_down() if rider_down else None
    (z,) = _mm(tag + "_down", [s], [W[nd]], [[(0, 0)]], epi_down, [F32], tm=1024, tn=1024, tk=1408,
               extras=[(x, 'mn', 0)], rider=rider_down)
    net.done(rider_down)
    return a, b, s, z


def _ffn_bwd(tag, dzh, dz, xb, a, b, s, W, names, gdt, G, net, ride, pre=(None, None)):
    ng, nu, nd = names

    def epi_ds(accs, a, b):
        ds = accs[0]
        sg = _sigmoid(a)
        return [ds * b * _dsilu(a, sg), ds * a * sg]

    rider = pre[0]() if pre[0] else None
    da, db = _mm(tag + "_ds", [dzh], [W[nd]], [[(0, 0)]], epi_ds, [BF16, BF16], tb=True, tm=1024, tn=1408, tk=1024, epi_rows=256,
                 extras=[(a, 'mn', 0), (b, 'mn', 0)], rider=rider)
    net.done(rider)
    ident = lambda accs: accs
    rider = pre[1]() if pre[1] else None
    (G[nd],) = _mm(tag + "_dwd", [s], [dzh], [[(0, 0)]], ident, [gdt], ta=True, tm=1408, tn=1024, tk=1024,
                   rider=rider)
    net.done(rider)
    if ride == 'all':
        rider = net.to_sibling([nd])
        (G[ng],) = _mm(tag + "_dwg", [xb], [da], [[(0, 0)]], ident, [gdt], ta=True, tm=1024, tn=1408, tk=1024,
                       rider=rider)
        net.done(rider)
        net.pairsum([nd])
        rider = net.to_owner([nd])
        (G[nu],) = _mm(tag + "_dwu", [xb], [db], [[(0, 0)]], ident, [gdt], ta=True, tm=1024, tn=1408, tk=1024,
                       rider=rider)
        net.done(rider)
        rider = net.to_sibling([ng, nu])
        if rider is not None:
            _run_rider(tag + "_to_sibling", rider)
            net.done(rider)
        net.pairsum([ng, nu])
        rider = net.to_owner([ng, nu])
    else:
        rider = net.to_sibling([nd]) if ride else None
        G[ng], G[nu] = _mm(tag + "_dwgu", [xb], [da, db], [[(0, 0)], [(0, 1)]], ident, [gdt, gdt], ta=True,
                           tm=1024, tn=1408, tk=1024, rider=rider)
        net.done(rider)
        if ride:
            net.pairsum([nd])
        rider = net.to_owner([nd]) if ride else None

    def epi_dx(accs, dzres):
        return [ALPHA * dzres + accs[0]]

    (dx,) = _mm(tag + "_dx", [da, db], [W[ng], W[nu]], [[(0, 0), (1, 1)]], epi_dx, [F32], tb=True,
                tm=1024, tn=1024, tk=1408, extras=[(dz, 'mn', 0)], rider=rider)
    net.done(rider)
    return dx


def _ret_tables(H, T):
    C = RET_CHUNK
    log_g = jnp.log(1.0 - jnp.exp2(-5.0 - jnp.arange(H, dtype=F32)))
    idx = jnp.arange(C, dtype=F32)
    diff = idx[:, None] - idx[None, :]
    dm = jnp.where(diff[None] >= 0, jnp.exp(jnp.maximum(diff, 0.0)[None] * log_g[:, None, None]), 0.0)
    xi = jnp.broadcast_to(jnp.exp((idx[None, :] + 1.0) * log_g[:, None])[:, :, None], (H, C, RET_DV))
    zeta = jnp.broadcast_to(jnp.exp((C - 1.0 - idx)[None, :] * log_g[:, None])[:, :, None], (H, C, RET_DK))
    gc = jnp.broadcast_to(jnp.exp(C * log_g)[:, None, None], (H, 1, RET_DV))
    half = RET_DK // 2
    freqs = ROPE_BASE ** (-jnp.arange(half, dtype=F32) / half)
    ang = jnp.arange(T, dtype=F32)[:, None] * freqs[None, :]
    cos, sin = jnp.cos(ang), jnp.sin(ang)
    cosf = jnp.concatenate([cos, cos], axis=1)
    sins = jnp.concatenate([-sin, sin], axis=1)
    return dm, xi, zeta, gc, cosf, sins


def _rot(x, cosf, sins):
    return x * cosf + pltpu.roll(x, RET_DK // 2, 1) * sins


def _rot_bwd(dy, cosf, sins):
    return dy * cosf + pltpu.roll(dy * sins, RET_DK // 2, 1)


RET_HB = 8


def _ret_specs(H, HB, rev, NC):
    C, G = RET_CHUNK, H // HB
    nn = (lambda n: NC - 1 - n) if rev else (lambda n: n)
    return [
        pl.BlockSpec((C, HB * RET_DK), lambda h, n: (nn(n), h)),
        pl.BlockSpec((C, HB * RET_DK), lambda h, n: (nn(n), G + h)),
        pl.BlockSpec((C, HB * RET_DV), lambda h, n: (nn(n), G + h)),
        pl.BlockSpec((C, HB * RET_DV), lambda h, n: (nn(n), 2 * G + h)),
        pl.BlockSpec((C, RET_DK), lambda h, n: (nn(n), 0)),
        pl.BlockSpec((C, RET_DK), lambda h, n: (nn(n), 0)),
        pl.BlockSpec((1, HB * RET_DV), lambda h, n: (0, h)),
        pl.BlockSpec((HB, C, C), lambda h, n: (h, 0, 0)),
        pl.BlockSpec((HB, C, RET_DV), lambda h, n: (h, 0, 0)),
        pl.BlockSpec((HB, C, RET_DK), lambda h, n: (h, 0, 0)),
        pl.BlockSpec((HB, 1, RET_DV), lambda h, n: (h, 0, 0)),
    ]


def _ret_fwd(proj, gn_g, tabs, H, T, rider=None):
    C, NC = RET_CHUNK, T // RET_CHUNK
    HB = min(RET_HB, H)
    dm, xi, zeta, gc, cosf, sins = tabs
    scale = RET_DK ** -0.5

    def body(q_ref, k_ref, v_ref, g_ref, cos_ref, sin_ref, gn_ref, dm_ref, xi_ref, zt_ref, gc_ref,
             r_ref, ri_ref, st_ref, state):
        @pl.when(pl.program_id(1) == 0)
        def _():
            state[...] = jnp.zeros_like(state)

        cs, sn = cos_ref[...], sin_ref[...]
        hs = range(HB)
        qk = [slice(h * RET_DK, (h + 1) * RET_DK) for h in hs]
        vv = [slice(h * RET_DV, (h + 1) * RET_DV) for h in hs]
        kr = [_rot(k_ref[:, qk[h]].astype(F32), cs, sn) for h in hs]
        qb = [(_rot(q_ref[:, qk[h]].astype(F32), cs, sn) * scale).astype(BF16) for h in hs]
        kb = [kr[h].astype(BF16) for h in hs]
        kzb = [(kr[h] * zt_ref[h]).astype(BF16) for h in hs]
        vb = [v_ref[:, vv[h]].astype(BF16) for h in hs]
        st = [state[h] for h in hs]
        stb = [st[h].astype(BF16) for h in hs]
        sb = [(_dot(qb[h], kb[h], tb=True) * dm_ref[h]).astype(BF16) for h in hs]
        cross = [_dot(qb[h], stb[h]) for h in hs]
        kv = [_dot(kzb[h], vb[h], ta=True) for h in hs]
        intra = [_dot(sb[h], vb[h]) for h in hs]
        for h in hs:
            st_ref[h] = stb[h]
            state[h] = gc_ref[h] * st[h] + kv[h]
        for h in hs:
            r = intra[h] + cross[h] * xi_ref[h]
            rhat, _ = _ln_stats(r)
            g = g_ref[:, vv[h]].astype(F32)
            r_ref[:, vv[h]] = r
            ri_ref[:, vv[h]] = (g * _sigmoid(g) * (rhat * gn_ref[:, vv[h]])).astype(BF16)

    VW = H * RET_DV
    return _hosted_call(
        body, rider, name="ret_fwd", grid=(H // HB, NC), in_specs=_ret_specs(H, HB, False, NC),
        out_specs=[pl.BlockSpec((C, HB * RET_DV), lambda h, n: (n, h)),
                   pl.BlockSpec((C, HB * RET_DV), lambda h, n: (n, h)),
                   pl.BlockSpec((HB, None, RET_DK, RET_DV), lambda h, n: (h, n, 0, 0))],
        out_shape=[jax.ShapeDtypeStruct((T, VW), F32), jax.ShapeDtypeStruct((T, VW), BF16),
                   jax.ShapeDtypeStruct((H, NC, RET_DK, RET_DV), BF16)],
        scratch=[pltpu.VMEM((HB, RET_DK, RET_DV), F32)],
        args=[proj, proj, proj, proj, cosf, sins, gn_g, dm, xi, zeta, gc])


def _hosted_call(body, rider, *, name, grid, in_specs, out_specs, out_shape, scratch, args):
    n_in, n_out, n_scr = len(args), len(out_shape), len(scratch)
    if rider is None:
        hosted = body
    else:
        n_ri, n_ro = len(rider.ins), len(rider.out_shape)
        in_specs, out_specs = in_specs + [ANY] * n_ri, out_specs + [ANY] * n_ro
        args, out_shape, scratch = args + rider.ins, out_shape + rider.out_shape, scratch + rider.scratch

        def hosted(*refs):
            o0, s0 = n_in + n_ri, n_in + n_ri + n_out + n_ro
            step = pl.program_id(0) * grid[1] + pl.program_id(1)
            ride = (step, grid[0] * grid[1], refs[n_in:o0], refs[o0 + n_out:s0], refs[s0 + n_scr:])
            rider.begin(*ride)
            body(*refs[:n_in], *refs[o0:o0 + n_out], *refs[s0:s0 + n_scr])
            rider.end(*ride)

    aliases = {} if rider is None else {n_in + p: n_out + o for p, o in rider.aliases.items()}
    res = pl.pallas_call(
        hosted, name=name, grid=grid, in_specs=in_specs, out_specs=out_specs, out_shape=out_shape,
        scratch_shapes=scratch, input_output_aliases=aliases, compiler_params=_params(("arbitrary", "arbitrary")),
    )(*args)
    if rider is not None:
        rider.results = res[n_out:]
    return res[:n_out]


def _ret_bwd(dri, r, states, proj, gn_g, tabs, H, T, in_w, rider=None):
    C, NC = RET_CHUNK, T // RET_CHUNK
    HB = min(RET_HB, H)
    dm, xi, zeta, gc, cosf, sins = tabs
    scale = RET_DK ** -0.5

    def body(q_ref, k_ref, v_ref, g_ref, cos_ref, sin_ref, gn_ref, dm_ref, xi_ref, zt_ref, gc_ref,
             dri_ref, r_ref, st_ref, dp_ref, dgn_ref, dstate):
        @pl.when(pl.program_id(1) == 0)
        def _():
            dstate[...] = jnp.zeros_like(dstate)
            dgn_ref[...] = jnp.zeros_like(dgn_ref)

        cs, sn = cos_ref[...], sin_ref[...]
        hs = range(HB)
        qk = [slice(h * RET_DK, (h + 1) * RET_DK) for h in hs]
        vv = [slice(h * RET_DV, (h + 1) * RET_DV) for h in hs]
        qr = [_rot(q_ref[:, qk[h]].astype(F32), cs, sn) * scale for h in hs]
        kr = [_rot(k_ref[:, qk[h]].astype(F32), cs, sn) for h in hs]
        qb = [qr[h].astype(BF16) for h in hs]
        kb = [kr[h].astype(BF16) for h in hs]
        vb = [v_ref[:, vv[h]].astype(BF16) for h in hs]
        qxb = [(qr[h] * xi_ref[h, :, :RET_DK]).astype(BF16) for h in hs]
        kzb = [(kr[h] * zt_ref[h]).astype(BF16) for h in hs]
        drb = []
        for h in hs:
            rhat, rstd = _ln_stats(r_ref[:, vv[h]])
            g, gn, dpre = g_ref[:, vv[h]].astype(F32), gn_ref[:, vv[h]], dri_ref[:, vv[h]]
            sg = _sigmoid(g)
            dp_ref[:, 2 * QW + VW + h * RET_DV:2 * QW + VW + (h + 1) * RET_DV] = (
                dpre * (rhat * gn) * _dsilu(g, sg)).astype(BF16)
            drn = dpre * (g * sg)
            dgn_ref[:, vv[h]] += _colsum(drn * rhat)
            drb.append(_ln_bwd_math(drn, rhat, rstd, gn).astype(BF16))
        ds1 = [dstate[h] for h in hs]
        ds1b = [ds1[h].astype(BF16) for h in hs]
        sb = [(_dot(qb[h], kb[h], tb=True) * dm_ref[h]).astype(BF16) for h in hs]
        dsb = [(_dot(drb[h], vb[h], tb=True) * dm_ref[h]).astype(BF16) for h in hs]
        dq_x = [_dot(drb[h], st_ref[h], tb=True) for h in hs]
        dk_x = [_dot(vb[h], ds1b[h], tb=True) for h in hs]
        dv_x = [_dot(kzb[h], ds1b[h]) for h in hs]
        dst = [_dot(qxb[h], drb[h], ta=True) for h in hs]
        for h in hs:
            dstate[h] = gc_ref[h] * ds1[h] + dst[h]
        dv_i = [_dot(sb[h], drb[h], ta=True) for h in hs]
        dq_i = [_dot(dsb[h], kb[h]) for h in hs]
        dk_i = [_dot(dsb[h], qb[h], ta=True) for h in hs]
        for h in hs:
            dp_ref[:, 2 * QW + h * RET_DV:2 * QW + (h + 1) * RET_DV] = (dv_i[h] + dv_x[h]).astype(BF16)
            dq = dq_i[h] + dq_x[h] * xi_ref[h, :, :RET_DK]
            dk = dk_i[h] + dk_x[h] * zt_ref[h]
            dp_ref[:, qk[h]] = _rot_bwd(dq * scale, cs, sn).astype(BF16)
            dp_ref[:, QW + h * RET_DK:QW + (h + 1) * RET_DK] = _rot_bwd(dk, cs, sn).astype(BF16)

    VW, QW = H * RET_DV, H * RET_DK
    rv = lambda n: NC - 1 - n
    in_specs = _ret_specs(H, HB, True, NC) + [
        pl.BlockSpec((C, HB * RET_DV), lambda h, n: (rv(n), h)),
        pl.BlockSpec((C, HB * RET_DV), lambda h, n: (rv(n), h)),
        pl.BlockSpec((HB, None, RET_DK, RET_DV), lambda h, n: (h, rv(n), 0, 0)),
    ]
    assert HB == H
    return _hosted_call(
        body, rider, name="ret_bwd", grid=(1, NC), in_specs=in_specs,
        out_specs=[pl.BlockSpec((C, 2 * QW + 2 * VW), lambda h, n: (rv(n), 0)),
                   pl.BlockSpec((1, VW), lambda h, n: (0, 0))],
        out_shape=[jax.ShapeDtypeStruct((T, in_w), BF16), jax.ShapeDtypeStruct((1, VW), F32)],
        scratch=[pltpu.VMEM((HB, RET_DK, RET_DV), F32)],
        args=[proj, proj, proj, proj, cosf, sins, gn_g, dm, xi, zeta, gc, dri, r, states])


CONV_CW = 128
CONV_TB = 512


def _conv_fwd(proj, kpad, bias, off_a, CC, T, rider=None):
    tb, cw = min(CONV_TB, T), CONV_CW
    hb = tb // HALO
    ca, cb = off_a // cw, (off_a + CC) // cw

    def body(a_ref, b_ref, ap_ref, bp_ref, k_ref, bias_ref, u1_ref, win):
        i = pl.program_id(0)
        keep = (i > 0).astype(F32)
        win[0:HALO, :] = ap_ref[...].astype(F32) * _sigmoid(bp_ref[...].astype(F32)) * keep
        win[HALO:, :] = a_ref[...].astype(F32) * _sigmoid(b_ref[...].astype(F32))
        acc = jnp.broadcast_to(bias_ref[...], (tb, cw))
        for w in range(CONV_WIDTH):
            acc = acc + k_ref[w:w + 1, :] * win[pl.ds(HALO - (CONV_WIDTH - 1) + w, tb), :]
        u1_ref[...] = acc

    prev = lambda i: jnp.maximum(i * hb - 1, 0)
    (u1,) = _hosted_call(
        body, rider, name="conv_fwd", grid=(T // tb, CC // cw),
        in_specs=[pl.BlockSpec((tb, cw), lambda i, c: (i, ca + c)),
                  pl.BlockSpec((tb, cw), lambda i, c: (i, cb + c)),
                  pl.BlockSpec((HALO, cw), lambda i, c: (prev(i), ca + c)),
                  pl.BlockSpec((HALO, cw), lambda i, c: (prev(i), cb + c)),
                  pl.BlockSpec((HALO, cw), lambda i, c: (0, c)),
                  pl.BlockSpec((1, cw), lambda i, c: (0, c))],
        out_specs=[pl.BlockSpec((tb, cw), lambda i, c: (i, c))],
        out_shape=[jax.ShapeDtypeStruct((T, CC), F32)],
        scratch=[pltpu.VMEM((tb + HALO, cw), F32)],
        args=[proj, proj, proj, proj, kpad, bias])
    return u1


def _conv_bwd(du1, proj, kpad, off_a, CC, T, rider=None):
    tb, cw = min(CONV_TB, T), CONV_CW
    hb = tb // HALO
    nt = T // tb
    ca, cb = off_a // cw, (off_a + CC) // cw

    def body(d_ref, dn_ref, a_ref, b_ref, ap_ref, bp_ref, k_ref, da_ref, db_ref, dk_ref, winu, wind):
        i = pl.program_id(1)
        a, b = a_ref[...].astype(F32), b_ref[...].astype(F32)
        sgb = _sigmoid(b)
        winu[0:HALO, :] = ap_ref[...].astype(F32) * _sigmoid(bp_ref[...].astype(F32)) * (i > 0).astype(F32)
        winu[HALO:, :] = a * sgb
        d = d_ref[...]
        wind[0:tb, :] = d
        wind[tb:, :] = dn_ref[...] * (i < nt - 1).astype(F32)

        @pl.when(i == 0)
        def _():
            dk_ref[...] = jnp.zeros_like(dk_ref)

        du0 = jnp.zeros((tb, cw), F32)
        for w in range(CONV_WIDTH):
            du0 = du0 + k_ref[w:w + 1, :] * wind[pl.ds(CONV_WIDTH - 1 - w, tb), :]
            dk_ref[w:w + 1, :] += _colsum(winu[pl.ds(HALO - (CONV_WIDTH - 1) + w, tb), :] * d)
        da_ref[...] = (du0 * sgb).astype(BF16)
        db_ref[...] = (du0 * a * sgb * (1.0 - sgb)).astype(BF16)

    prev = lambda i: jnp.maximum(i * hb - 1, 0)
    nxt = lambda i: jnp.minimum((i + 1) * hb, T // HALO - 1)
    return _hosted_call(
        body, rider, name="conv_bwd", grid=(CC // cw, nt),
        in_specs=[pl.BlockSpec((tb, cw), lambda c, i: (i, c)),
                  pl.BlockSpec((HALO, cw), lambda c, i: (nxt(i), c)),
                  pl.BlockSpec((tb, cw), lambda c, i: (i, ca + c)),
                  pl.BlockSpec((tb, cw), lambda c, i: (i, cb + c)),
                  pl.BlockSpec((HALO, cw), lambda c, i: (prev(i), ca + c)),
                  pl.BlockSpec((HALO, cw), lambda c, i: (prev(i), cb + c)),
                  pl.BlockSpec((HALO, cw), lambda c, i: (0, c))],
        out_specs=[pl.BlockSpec((tb, cw), lambda c, i: (i, c)),
                   pl.BlockSpec((tb, cw), lambda c, i: (i, c)),
                   pl.BlockSpec((HALO, cw), lambda c, i: (0, c))],
        out_shape=[jax.ShapeDtypeStruct((T, CC), BF16), jax.ShapeDtypeStruct((T, CC), BF16),
                   jax.ShapeDtypeStruct((HALO, CC), F32)],
        scratch=[pltpu.VMEM((tb + HALO, cw), F32), pltpu.VMEM((tb + HALO, cw), F32)],
        args=[du1, du1, proj, proj, proj, proj, kpad])


FFN1 = ('ffn1_w_gate', 'ffn1_w_up', 'ffn1_w_down')
FFN2 = ('ffn2_w_gate', 'ffn2_w_up', 'ffn2_w_down')


def _local_step(x, tgt, W, P, gdt=BF16, comm=None):
    T, D = x.shape
    G = {}
    net = _Net(comm, G)
    if comm is not None:
        W = comm.W
        first = net.gather(['ffn1_w_gate', 'ffn1_w_up'])
    (xb,) = _rows("x_to_bf16", lambda v: [v], [(x, 'r', D, 0)], [('r', D, BF16)], T=T, tb=512,
                  rider=first if comm is not None else None)
    if comm is not None:
        net.done(first)
    VW = P['ret_gn_g'].shape[1]
    H = VW // RET_DV
    QW = H * RET_DK
    CC = P['conv_b'].shape[1]
    off_glu = 2 * QW + 2 * VW
    off_gate = off_glu + 2 * CC
    ident = lambda accs: accs

    a1, b1, s1, z1 = _ffn_fwd("ffn1", xb, x, W, FFN1, net,
                              rider=net.gather(['ffn1_w_down', 'w_in'], {'w_in': (0, (3 * D) // 8, False)}),
                              rider_down=lambda: net.gather(['w_in'], {'w_in': ((3 * D) // 8, (3 * D) // 8, False)}))
    rider = net.gather(['w_in'], {'w_in': ((3 * D) // 4, D // 4, True)})
    x1, x1b, xh1, rs1 = _ln_fwd("ln1", z1, P['ln1_g'], P['ln1_b'], T, D, rider=rider)
    net.done(rider)

    rest = net.gather(['conv_k', 'w_ret_o', 'w_conv_o', 'w_out', 'ffn2_w_down'])
    (proj,) = _mm("w_in", [x1b], [W['w_in']], [[(0, 0)]], lambda accs, bias: [accs[0] + bias], [F32],
                  tm=2048, tn=0, tk=1024, extras=[(P['b_in'], 'n', 0)], i_outer=False, b3=True, rider=rest)
    net.done(rest)
    tabs = _ret_tables(H, T)
    rider = net.gather(['ffn2_w_gate', 'ffn2_w_up'])
    r, ret_in, states = _ret_fwd(proj, P['ret_gn_g'], tabs, H, T, rider=rider)
    net.done(rider)
    kpad = jnp.pad(W['conv_k'].astype(F32), ((0, HALO - CONV_WIDTH), (0, 0)))
    u1 = _conv_fwd(proj, kpad, P['conv_b'], off_glu, CC, T)

    def conv_ln(u1, g, b):
        xhat, rstd = _ln_stats(u1)
        u2 = xhat * g + b
        return [xhat, rstd, u2 * _sigmoid(u2)]

    xhc, rsc, u3 = _rows("conv_ln", conv_ln, [(u1, 'r', CC, 0), (P['conv_ln_g'], 'v', CC, 0), (P['conv_ln_b'], 'v', CC, 0)],
                         [('r', CC, F32), ('c', 1, F32), ('r', CC, BF16)], T=T, tb=512)
    (ret_out,) = _mm("ret_o", [ret_in], [W['w_ret_o']], [[(0, 0)]], ident, [F32], tm=1024, tn=1024, tk=2048)

    def epi_merge(accs, ret_out, gr, gc):
        conv_out = accs[0]
        return [conv_out, _sigmoid(gr) * ret_out + _sigmoid(gc) * conv_out]

    conv_out, merged = _mm("conv_o_merge", [u3], [W['w_conv_o']], [[(0, 0)]], epi_merge, [F32, BF16],
                           tm=512, tn=D, tk=1024, epi_rows=256,
                           extras=[(ret_out, 'mn', 0), (proj, 'mn', off_gate), (proj, 'mn', off_gate + D)])
    (z2,) = _mm("w_out", [merged], [W['w_out']], [[(0, 0)]], lambda accs, xr: [ALPHA * xr + accs[0]], [F32],
                tm=1024, tn=1024, tk=1024, extras=[(x1, 'mn', 0)])
    x2, x2b, xh2, rs2 = _ln_fwd("ln2", z2, P['ln2_g'], P['ln2_b'], T, D)
    a2, b2, s2, z3 = _ffn_fwd("ffn2", x2b, x2, W, FFN2, net)
    dz3, dz3h, g_ln3_g, g_ln3_b, loss = _ln_loss_bwd("ln3_loss", z3, P['ln3_g'], P['ln3_b'], tgt, T, D)

    S = {'ln3_g': g_ln3_g, 'ln3_b': g_ln3_b}
    dy2 = _ffn_bwd("ffn2b", dz3h, dz3, x2b, a2, b2, s2, W, FFN2, gdt, G, net, 'down')
    rider = net.to_sibling(['ffn2_w_gate', 'ffn2_w_up'])
    dz2, dz2b, S['ln2_g'], S['ln2_b'] = _ln_bwd("ln2b", dy2, xh2, rs2, P['ln2_g'], 1.0, T, D, rider=rider)
    net.done(rider)
    net.pairsum(['ffn2_w_gate', 'ffn2_w_up'])

    (G['w_out'],) = _mm("d_w_out", [merged], [dz2b], [[(0, 0)]], ident, [gdt], ta=True, tm=1024, tn=1024, tk=1024)

    def epi_dmerge(accs, ret_out, conv_out, gr, gc):
        dm_ = accs[0]
        sr, sc = _sigmoid(gr), _sigmoid(gc)
        return [dm_ * sr, dm_ * sc, dm_ * ret_out * sr * (1.0 - sr), dm_ * conv_out * sc * (1.0 - sc)]

    rider = net.to_sibling(['w_out'])
    dret_out, dconv_out, dgate_r, dgate_c = _mm(
        "d_merge", [dz2b], [W['w_out']], [[(0, 0)]], epi_dmerge, [BF16, BF16, BF16, BF16], tb=True,
        tm=512, tn=D, tk=1024, epi_rows=256, rider=rider,
        extras=[(ret_out, 'mn', 0), (conv_out, 'mn', 0), (proj, 'mn', off_gate), (proj, 'mn', off_gate + D)])
    net.done(rider)
    (G['w_ret_o'],) = _mm("d_w_ret_o", [ret_in], [dret_out], [[(0, 0)]], ident, [gdt], ta=True, tm=1024, tn=1024, tk=1024)
    (G['w_conv_o'],) = _mm("d_w_conv_o", [u3], [dconv_out], [[(0, 0)]], ident, [gdt], ta=True, tm=1024, tn=1024, tk=1024)
    rider = net.to_sibling(['w_ret_o', 'w_conv_o'])
    (dri,) = _mm("d_ret_in", [dret_out], [W['w_ret_o']], [[(0, 0)]], ident, [F32], tb=True, tm=1024, tn=1024, tk=1024,
                 rider=rider)
    net.done(rider)
    net.pairsum(['w_out', 'w_ret_o', 'w_conv_o'])
    rider = net.to_owner(['ffn2_w_gate', 'ffn2_w_up'])
    dproj, S['ret_gn_g'] = _ret_bwd(dri, r, states, proj, P['ret_gn_g'], tabs, H, T, proj.shape[1], rider=rider)
    net.done(rider)

    def epi_du2(accs, xhat, g, b):
        u2 = xhat * g + b
        return [accs[0] * _dsilu(u2, _sigmoid(u2))]

    (du2,) = _mm("d_u3", [dconv_out], [W['w_conv_o']], [[(0, 0)]], epi_du2, [F32], tb=True, tm=512, tn=CC, tk=1024, epi_rows=256,
                 extras=[(xhc, 'mn', 0), (P['conv_ln_g'], 'n', 0), (P['conv_ln_b'], 'n', 0)])

    def conv_ln_bwd(du2, xhat, rstd, g):
        du1 = _ln_bwd_math(du2, xhat, rstd, g)
        return [du1, _colsum(du2 * xhat), _colsum(du2), _colsum(du1)]

    du1, S['conv_ln_g'], S['conv_ln_b'], S['conv_b'] = _rows(
        "conv_ln_bwd", conv_ln_bwd, [(du2, 'r', CC, 0), (xhc, 'r', CC, 0), (rsc, 'r', 1, 0), (P['conv_ln_g'], 'v', CC, 0)],
        [('r', CC, F32), ('a', CC, F32), ('a', CC, F32), ('a', CC, F32)], T=T, tb=512)
    dglu_a, dglu_b, dkpad = _conv_bwd(du1, proj, kpad, off_glu, CC, T)
    G['conv_k'] = dkpad[:CONV_WIDTH].astype(gdt)

    dproj = lax.dynamic_update_slice(dproj, jnp.concatenate([dglu_a, dglu_b, dgate_r, dgate_c], axis=1), (0, off_glu))
    IN_W = dproj.shape[1]
    rider = net.both(net.exchange(['conv_k']),
                     net.to_owner(['w_out', 'w_ret_o', 'w_conv_o']))
    G['w_in'], S['b_in'] = _mm("d_w_in", [x1b], [dproj], [[(0, 0)]], ident, [gdt], ta=True, o3=True, bsum=True,
                               tm=1024, tn=W['w_in'].shape[2], tk=1024, rider=rider)
    net.done(rider)
    cuts = [0, (3 * D) // 16, (43 * D) // 64, D]
    w_in_rows = [{'w_in': (cuts[i], cuts[i + 1] - cuts[i], i == 2)} for i in range(3)]
    rider = net.to_sibling(['w_in'])
    (dy1,) = _mm("d_x1", [dproj], [W['w_in']], [[(0, 0)]], lambda accs, dzr: [ALPHA * dzr + accs[0]], [F32], tb=True,
                 b3=True, tm=1024, tn=1024, tk=0, extras=[(dz2, 'mn', 0)], rider=rider)
    net.done(rider)
    net.pairsum(['w_in'])
    rider = net.to_owner(['w_in'], w_in_rows[0])
    dz1, dz1h, S['ln1_g'], S['ln1_b'] = _ln_bwd("ln1b", dy1, xh1, rs1, P['ln1_g'], 0.5, T, D, rider=rider)
    net.done(rider)
    grad_x = _ffn_bwd("ffn1b", dz1h, dz1, xb, a1, b1, s1, W, FFN1, gdt, G, net, 'all',
                      pre=(lambda: net.to_owner(['w_in'], w_in_rows[1]), lambda: net.to_owner(['w_in'], w_in_rows[2])))
    return loss, grad_x, G, S


def _coords():
    return lax.axis_index("x"), lax.axis_index("y"), lax.axis_index("c")


def _flip(k, x, y, c):
    return (1 - x if k & 4 else x, 1 - y if k & 2 else y, 1 - c if k & 1 else c)


def _lin(p):
    return 4 * p[0] + 2 * p[1] + p[2]


class _Rider:
    def __init__(self, ins, out_shape, rows=None, fill=None):
        nb = len(ins)
        self.rows = rows or [None] * nb
        fill = fill or [None] * nb
        self.aliases = {nb + i: w for i, w in enumerate(w for w in range(nb) if fill[w] is not None)}
        self.ins = list(ins) + [f for f in fill if f is not None]
        self.out_shape, self.results = list(out_shape), None
        self.scratch = [pltpu.SemaphoreType.DMA((8 * nb,)), pltpu.SemaphoreType.DMA((8 * nb,)),
                        pltpu.SemaphoreType.DMA((nb,))]

    def span(self, w, ref, *slot, half=None):
        rows = self.rows[w]
        if half is not None:
            first, count = rows if rows is not None else (0, self.out_shape[w].shape[1])
            rows = (first + half * (count // 2), count // 2)
        if rows is None:
            return ref.at[slot] if slot else ref
        return ref.at[(*slot, pl.ds(*rows))]

    def begin(self, step, n_steps, ins, outs, sems):
        @pl.when(step == 0)
        def _():
            self.start(ins, outs, sems)

        @pl.when(step == min(n_steps - 1, (5 * n_steps) // 8))
        def _():
            self.relay(ins, outs, sems)

        @pl.when(step == n_steps - 1)
        def _():
            self.mid(ins, outs, sems)

    def end(self, step, n_steps, ins, outs, sems):
        @pl.when(step == n_steps - 1)
        def _():
            self.finish(ins, outs, sems)

    def relay(self, ins, outs, sems):
        pass

    def mid(self, ins, outs, sems):
        pass


class _Riders:
    def __init__(self, riders):
        self.riders = list(riders)
        self.ins = [a for r in self.riders for a in r.ins]
        self.out_shape = [o for r in self.riders for o in r.out_shape]
        self.scratch = [c for r in self.riders for c in r.scratch]
        self.aliases, n_in, n_out = {}, 0, 0
        for r in self.riders:
            self.aliases.update({n_in + p: n_out + o for p, o in r.aliases.items()})
            n_in, n_out = n_in + len(r.ins), n_out + len(r.out_shape)

    def _each(self, ins, outs, sems):
        i = o = c = 0
        for r in self.riders:
            yield r, ins[i:i + len(r.ins)], outs[o:o + len(r.out_shape)], sems[c:c + len(r.scratch)]
            i, o, c = i + len(r.ins), o + len(r.out_shape), c + len(r.scratch)

    def begin(self, step, n_steps, ins, outs, sems):
        for r, i, o, c in self._each(ins, outs, sems):
            r.begin(step, n_steps, i, o, c)

    def end(self, step, n_steps, ins, outs, sems):
        for r, i, o, c in self._each(ins, outs, sems):
            r.end(step, n_steps, i, o, c)

    @property
    def results(self):
        return [x for r in self.riders for x in r.results]

    @results.setter
    def results(self, res):
        for r, _, o, _ in self._each([], list(res), []):
            r.results = o


class _GatherRider(_Rider):
    def __init__(self, blks, rows=None, fill=None):
        super().__init__(blks, [jax.ShapeDtypeStruct((N_DEV,) + b.shape, b.dtype) for b in blks], rows, fill)
        counts = [(r[1] if r is not None else b.shape[0]) for r, b in zip(self.rows, blks)]
        self.halves = [n % 32 == 0 for n in counts]

    def _copies(self, x_refs, out_refs, sems):
        nb = len(self.out_shape)
        send_sems, recv_sems, local_sems = sems
        x, y, c = _coords()
        me, sib = (x, y, c), (x, y, 1 - c)
        xn, yn, dg = _flip(4, x, y, c), _flip(2, x, y, c), _flip(6, x, y, c)
        plans = []
        for w in range(nb):
            own = self.span(w, x_refs[w])

            def copy(k, block, to, src=None, half=None, w=w):
                slot = self.span(w, out_refs[w], _lin(block), half=half)
                return pltpu.make_async_remote_copy(
                    src_ref=slot if src is None else src, dst_ref=slot, send_sem=send_sems.at[k * nb + w],
                    recv_sem=recv_sems.at[k * nb + w], device_id=to, device_id_type=MESH)

            mine = pltpu.make_async_copy(own, self.span(w, out_refs[w], _lin(me)), local_sems.at[w])
            first = [copy(0, me, sib, src=own), copy(1, me, xn, src=own), copy(2, me, yn, src=own)]
            if self.halves[w]:
                relay = [(copy(1, xn, me), [copy(3, xn, yn, half=0), copy(5, xn, sib)]),
                         (copy(2, yn, me), [copy(4, yn, xn, half=1), copy(6, yn, sib)])]
                last = [(copy(3, dg, me, half=0), []), (copy(4, dg, me, half=1), [copy(7, dg, sib)])]
            else:
                first.append(copy(3, me, dg, src=own))
                relay = [(copy(1, xn, me), [copy(5, xn, sib)]), (copy(2, yn, me), [copy(6, yn, sib)])]
                last = [(copy(3, dg, me), [copy(7, dg, sib)])]
            other = lambda p: (p[0], p[1], 1 - c)
            from_sib = [copy(0, sib, me), copy(5, other(xn), me), copy(6, other(yn), me), copy(7, other(dg), me)]
            plans.append((mine, first, relay, last, from_sib))
        return plans

    def start(self, ins, outs, sems):
        for mine, first, _, _, _ in self._copies(ins, outs, sems):
            for cp in [mine] + first:
                cp.start()

    def relay(self, ins, outs, sems):
        for _, _, relay, _, _ in self._copies(ins, outs, sems):
            for arrival, released in relay:
                arrival.wait_recv()
                for cp in released:
                    cp.start()

    def mid(self, ins, outs, sems):
        for _, _, _, last, _ in self._copies(ins, outs, sems):
            for arrival, released in last:
                arrival.wait_recv()
                for cp in released:
                    cp.start()

    def finish(self, ins, outs, sems):
        for mine, first, relay, last, from_sib in self._copies(ins, outs, sems):
            for cp in from_sib:
                cp.wait_recv()
            for cp in first + [cp for _, released in relay + last for cp in released]:
                cp.wait_send()
            mine.wait()


class _ExchangeRider(_Rider):
    def __init__(self, gs, rows=None, fill=None):
        super().__init__(gs, [jax.ShapeDtypeStruct(g.shape, g.dtype) for g in gs], rows, fill)

    def _copies(self, g_refs, out_refs, sems):
        nb = len(self.out_shape)
        send_sems, recv_sems, local_sems = sems
        x, y, c = _coords()
        me = _lin((x, y, c))

        def copy(k, w, landing):
            peer = _flip(k, x, y, c)
            src, dst = (me, _lin(peer)) if landing else (_lin(peer), me)
            return pltpu.make_async_remote_copy(
                src_ref=self.span(w, g_refs[w], src), dst_ref=self.span(w, out_refs[w], dst),
                send_sem=send_sems.at[(k - 1) * nb + w], recv_sem=recv_sems.at[(k - 1) * nb + w],
                device_id=peer, device_id_type=MESH)

        mines = [pltpu.make_async_copy(self.span(w, g_refs[w], me), self.span(w, out_refs[w], me), local_sems.at[w])
                 for w in range(nb)]
        sends = [copy(k, w, False) for w in range(nb) for k in range(1, N_DEV)]
        landings = [copy(k, w, True) for w in range(nb) for k in range(1, N_DEV)]
        return mines, sends, landings

    def start(self, ins, outs, sems):
        mines, sends, _ = self._copies(ins, outs, sems)
        for cp in mines + sends:
            cp.start()

    def finish(self, ins, outs, sems):
        mines, sends, landings = self._copies(ins, outs, sems)
        for cp in landings:
            cp.wait_recv()
        for cp in sends:
            cp.wait_send()
        for mine in mines:
            mine.wait()


class _SiblingRider(_Rider):
    def __init__(self, gs):
        super().__init__(gs, [jax.ShapeDtypeStruct((4,) + g.shape[1:], g.dtype) for g in gs])

    def _copies(self, g_refs, out_refs, sems, landing):
        nb = len(self.out_shape)
        send_sems, recv_sems, _ = sems
        x, y, c = _coords()
        whose = c if landing else 1 - c
        return [pltpu.make_async_remote_copy(
            src_ref=g_refs[w].at[2 * q + whose], dst_ref=out_refs[w].at[q], send_sem=send_sems.at[q * nb + w],
            recv_sem=recv_sems.at[q * nb + w], device_id=(x, y, 1 - c), device_id_type=MESH)
            for w in range(nb) for q in range(4)]

    def start(self, ins, outs, sems):
        for cp in self._copies(ins, outs, sems, False):
            cp.start()

    def finish(self, ins, outs, sems):
        for cp in self._copies(ins, outs, sems, True):
            cp.wait_recv()
        for cp in self._copies(ins, outs, sems, False):
            cp.wait_send()


class _ChipRider(_Rider):
    FLIPS = (4, 2, 6)

    def __init__(self, ps, rows=None, fill=None):
        super().__init__(ps, [jax.ShapeDtypeStruct(p.shape, p.dtype) for p in ps], rows, fill)

    def _copies(self, p_refs, out_refs, sems):
        nb = len(self.out_shape)
        send_sems, recv_sems, local_sems = sems
        x, y, c = _coords()
        my_chip = 2 * x + y

        def copy(j, w, landing):
            peer = _flip(self.FLIPS[j], x, y, c)
            peer_chip = 2 * peer[0] + peer[1]
            src, dst = (my_chip, peer_chip) if landing else (peer_chip, my_chip)
            return pltpu.make_async_remote_copy(
                src_ref=self.span(w, p_refs[w], src), dst_ref=self.span(w, out_refs[w], dst),
                send_sem=send_sems.at[j * nb + w], recv_sem=recv_sems.at[j * nb + w],
                device_id=peer, device_id_type=MESH)

        mines = [pltpu.make_async_copy(self.span(w, p_refs[w], my_chip), self.span(w, out_refs[w], my_chip),
                                       local_sems.at[w]) for w in range(nb)]
        sends = [copy(j, w, False) for w in range(nb) for j in range(3)]
        landings = [copy(j, w, True) for w in range(nb) for j in range(3)]
        return mines, sends, landings

    def start(self, ins, outs, sems):
        mines, sends, _ = self._copies(ins, outs, sems)
        for cp in mines + sends:
            cp.start()

    def finish(self, ins, outs, sems):
        mines, sends, landings = self._copies(ins, outs, sems)
        for cp in landings:
            cp.wait_recv()
        for cp in sends:
            cp.wait_send()
        for mine in mines:
            mine.wait()


def _pairsum(name, g, land):
    _, r, cols = g.shape
    tb = r if r % 16 else _row_tile(r, 16, max(16, (1024 * 1024) // cols))
    core = lax.axis_index("c").astype(jnp.int32).reshape(1)

    def body(core_ref, g_ref, l_ref, o_ref):
        o_ref[...] = (g_ref[...].astype(F32) + l_ref[...].astype(F32)).astype(o_ref.dtype)

    return pl.pallas_call(
        body, name=name, out_shape=jax.ShapeDtypeStruct((4, r, cols), g.dtype),
        grid_spec=pltpu.PrefetchScalarGridSpec(
            num_scalar_prefetch=1, grid=(4, r // tb),
            in_specs=[pl.BlockSpec((None, None, tb, cols), lambda q, i, core_ref: (q, core_ref[0], i, 0)),
                      pl.BlockSpec((None, tb, cols), lambda q, i, core_ref: (q, i, 0))],
            out_specs=pl.BlockSpec((None, tb, cols), lambda q, i, core_ref: (q, i, 0))),
        compiler_params=_params(("arbitrary", "arbitrary")),
    )(core, g.reshape(4, 2, r, cols), land)


def _run_rider(name, rider):
    n_in, n_out = len(rider.ins), len(rider.out_shape)

    def body(*refs):
        ride = (refs[:n_in], refs[n_in:n_in + n_out], refs[n_in + n_out:])
        rider.start(*ride)
        rider.relay(*ride)
        rider.mid(*ride)
        rider.finish(*ride)

    rider.results = pl.pallas_call(
        body, name=name, out_shape=rider.out_shape, in_specs=[ANY] * n_in, out_specs=[ANY] * n_out,
        scratch_shapes=rider.scratch, input_output_aliases=dict(rider.aliases),
        compiler_params=pltpu.CompilerParams(has_side_effects=True),
    )(*rider.ins)
    return rider.results


def _as_matrix(name, g):
    if name == 'w_in':
        return g
    if name in COL_SHARDED:
        return jnp.transpose(g, (1, 0, 2)).reshape(g.shape[1], N_DEV * g.shape[2])
    return g.reshape(N_DEV * g.shape[1], g.shape[2])


def _by_owner(name, g):
    if name == 'w_in':
        return g
    if name in COL_SHARDED:
        return jnp.transpose(g.reshape(g.shape[0], N_DEV, g.shape[1] // N_DEV), (1, 0, 2))
    return g.reshape(N_DEV, g.shape[0] // N_DEV, g.shape[1])


class _Comm:
    def __init__(self, shards):
        self.shards, self.W, self.parts, self.partial, self.sent = shards, {}, {}, {}, {}
        self.from_sibling, self.pairs = {}, {}

    def _ride(self, cls, names, srcs, part, sink):
        part = part or {}
        rider = cls(srcs, rows=[part[n][:2] if n in part else None for n in names],
                    fill=[self.partial.pop((sink, n), None) for n in names])
        rider.names, rider.sink = names, sink
        rider.unfinished = {n for n in names if n in part and not part[n][2]}
        return rider

    def gather(self, names, part=None):
        return self._ride(_GatherRider, names, [self.shards[n] for n in names], part, 'W')

    def exchange(self, names, G, part=None):
        for n in names:
            if n not in self.sent:
                self.sent[n] = _by_owner(n, G[n])
        return self._ride(_ExchangeRider, names, [self.sent[n] for n in names], part, 'parts')

    def to_sibling(self, names, G):
        for n in names:
            self.sent[n] = _by_owner(n, G[n])
        rider = _SiblingRider([self.sent[n] for n in names])
        rider.names, rider.sink, rider.unfinished = names, 'sibling', set()
        return rider

    def pairsum(self, names):
        for n in names:
            self.pairs[n] = _pairsum("pairsum_" + n, self.sent[n], self.from_sibling.pop(n))

    def to_owner(self, names, part=None):
        return self._ride(_ChipRider, names, [self.pairs[n] for n in names], part, 'parts')

    def collect(self, rider):
        for n, res in zip(rider.names, rider.results):
            if n in rider.unfinished:
                self.partial[(rider.sink, n)] = res
            elif rider.sink == 'W':
                self.W[n] = _as_matrix(n, res)
            elif rider.sink == 'sibling':
                self.from_sibling[n] = res
            else:
                self.parts[n] = res


def _adamw_math(p_ref, w_ref, m_ref, v_ref, g_ref, d_ref, nm_ref, nv_ref):
    c1 = 1.0 - ADAM_B1 ** ADAM_STEP
    c2 = 1.0 - ADAM_B2 ** ADAM_STEP
    g = p_ref[0].astype(F32)
    for s in range(1, p_ref.shape[0]):
        g = g + p_ref[s].astype(F32)
    nm = ADAM_B1 * m_ref[...] + (1.0 - ADAM_B1) * g
    nv = ADAM_B2 * v_ref[...] + (1.0 - ADAM_B2) * (g * g)
    g_ref[...] = g
    nm_ref[...] = nm
    nv_ref[...] = nv
    d_ref[...] = -ADAM_LR * ((nm / c1) / (jnp.sqrt(nv / c2) + ADAM_EPS) + ADAM_WD * w_ref[...])


def _adamw_vectors(parts, ws, ms, vs, loss_parts):
    k = len(ws)

    def body(*refs):
        for i in range(k):
            _adamw_math(refs[i], refs[k + i], refs[2 * k + i], refs[3 * k + i], *refs[4 * k + 1 + 4 * i:4 * k + 5 + 4 * i])
        lp, lo = refs[4 * k], refs[8 * k + 1]
        lo[...] = functools.reduce(jnp.add, [lp[s] for s in range(lp.shape[0])])

    return pl.pallas_call(
        body, name="adamw_vectors",
        out_shape=[jax.ShapeDtypeStruct(w.shape, F32) for w in ws for _ in range(4)] + [jax.ShapeDtypeStruct((1, 128), F32)],
        compiler_params=_params(),
    )(*parts, *ws, *ms, *vs, loss_parts)


def _adamw(name, parts, w, m, v, tb):
    n, R, Wd = parts.shape
    assert R % tb == 0
    body = functools.partial(_adamw_math)

    row = pl.BlockSpec((tb, Wd), lambda i: (i, 0))
    return pl.pallas_call(
        body, name=name, grid=(R // tb,),
        in_specs=[pl.BlockSpec((n, tb, Wd), lambda i: (0, i, 0)), row, row, row],
        out_specs=[row, row, row, row], out_shape=[jax.ShapeDtypeStruct((R, Wd), F32)] * 4,
        compiler_params=_params(("arbitrary",)),
    )(parts, w, m, v)


def _row_tile(R, unit, cap):
    best = unit
    for t in range(unit, cap + 1, unit):
        if R % t == 0:
            best = t
    return best


def kernel(x, ffn1_w_gate, ffn1_w_up, ffn1_w_down, ln1_g, ln1_b, w_in, b_in, ret_gn_g, conv_k, conv_b, conv_ln_g, conv_ln_b, w_ret_o, w_conv_o, w_out, ln2_g, ln2_b, ffn2_w_gate, ffn2_w_up, ffn2_w_down, ln3_g, ln3_b, loss_target, m_ffn1_w_gate, m_ffn1_w_up, m_ffn1_w_down, m_ln1_g, m_ln1_b, m_w_in, m_b_in, m_ret_gn_g, m_conv_k, m_conv_b, m_conv_ln_g, m_conv_ln_b, m_w_ret_o, m_w_conv_o, m_w_out, m_ln2_g, m_ln2_b, m_ffn2_w_gate, m_ffn2_w_up, m_ffn2_w_down, m_ln3_g, m_ln3_b, v_ffn1_w_gate, v_ffn1_w_up, v_ffn1_w_down, v_ln1_g, v_ln1_b, v_w_in, v_b_in, v_ret_gn_g, v_conv_k, v_conv_b, v_conv_ln_g, v_conv_ln_b, v_w_ret_o, v_w_conv_o, v_w_out, v_ln2_g, v_ln2_b, v_ffn2_w_gate, v_ffn2_w_up, v_ffn2_w_down, v_ln3_g, v_ln3_b):
    given = dict(locals())
    wts = {n: given[n] for n in WEIGHTS}
    mom = {n: given['m_' + n] for n in WEIGHTS}
    var = {n: given['v_' + n] for n in WEIGHTS}

    def shard2d(a):
        return a.reshape(a.shape[-3] * a.shape[-2] if a.ndim == 4 else a.shape[-2], a.shape[-1])

    comm = _Comm({n: shard2d(wts[n]).astype(BF16) for n in BIG})
    P = {n: wts[n].reshape(1, -1) for n in SMALL}
    loss, grad_x, _, S = _local_step(x[0], loss_target[0], None, P, comm=comm)

    parts = comm.parts
    res = {}
    for n in BIG:
        rows, cols = parts[n].shape[1:]
        tb = rows if rows % 16 else _row_tile(rows, 16, max(16, (256 * 1024) // cols))
        res[n] = _adamw("adamw_" + n, parts[n], shard2d(wts[n]), shard2d(mom[n]), shard2d(var[n]), tb)

    vec_parts = _run_rider("gather_vector_grads", _GatherRider([S[n] for n in SMALL] + [loss]))
    vec = _adamw_vectors(vec_parts[:-1], [P[n] for n in SMALL], [mom[n].reshape(1, -1) for n in SMALL],
                         [var[n].reshape(1, -1) for n in SMALL], vec_parts[-1])
    for i, n in enumerate(SMALL):
        res[n] = vec[4 * i:4 * i + 4]

    outs = [vec[-1][0, 0], grad_x[None]]
    for k in range(4):
        for n in WEIGHTS:
            outs.append(res[n][k].reshape(wts[n].shape))
    return tuple(outs)
```

```python
import functools
import math

import jax
import jax.numpy as jnp
from jax import lax
from jax.experimental import pallas as pl
from jax.experimental.pallas import tpu as pltpu

F32 = jnp.float32
BF16 = jnp.bfloat16

N_DEV = 8
LN_EPS = 1e-5
ALPHA = 2.0 ** 0.25
RET_DK = 128
RET_DV = 256
RET_CHUNK = 256
ROPE_BASE = 10000.0
CONV_WIDTH = 31
HALO = 32
ADAM_LR, ADAM_B1, ADAM_B2, ADAM_EPS, ADAM_WD, ADAM_STEP = 0.001, 0.9, 0.999, 1e-08, 0.01, 10
VMEM_LIMIT = 52 * 1024 * 1024
MESH = pl.DeviceIdType.MESH
ANY = pl.BlockSpec(memory_space=pl.ANY)

BIG = ['ffn1_w_gate', 'ffn1_w_up', 'ffn1_w_down', 'w_in', 'w_ret_o', 'w_conv_o', 'w_out',
       'ffn2_w_gate', 'ffn2_w_up', 'ffn2_w_down', 'conv_k']
COL_SHARDED = {'ffn1_w_gate', 'ffn1_w_up', 'w_in', 'ffn2_w_gate', 'ffn2_w_up', 'conv_k'}
SMALL = ['ln1_g', 'ln1_b', 'b_in', 'ret_gn_g', 'conv_b', 'conv_ln_g', 'conv_ln_b', 'ln2_g', 'ln2_b', 'ln3_g', 'ln3_b']
WEIGHTS = ['ffn1_w_gate', 'ffn1_w_up', 'ffn1_w_down', 'ln1_g', 'ln1_b', 'w_in', 'b_in', 'ret_gn_g', 'conv_k', 'conv_b',
           'conv_ln_g', 'conv_ln_b', 'w_ret_o', 'w_conv_o', 'w_out', 'ln2_g', 'ln2_b', 'ffn2_w_gate', 'ffn2_w_up',
           'ffn2_w_down', 'ln3_g', 'ln3_b']


def _params(sem=None):
    return pltpu.CompilerParams(dimension_semantics=sem, vmem_limit_bytes=VMEM_LIMIT)


def _sigmoid(x):
    return jax.nn.sigmoid(x)


def _dsilu(x, sg):
    return sg * (1.0 + x * (1.0 - sg))


def _fit(dim, want):
    if dim <= want:
        return dim
    return max(t for t in range(128, want + 1, 128) if dim % t == 0)


def _dot(a, b, ta=False, tb=False):
    dn = (((0,) if ta else (1,), (1,) if tb else (0,)), ((), ()))
    return lax.dot_general(a, b, dn, preferred_element_type=F32)


def _mm(name, As, Bs, prods, epi, out_dtypes, *, ta=False, tb=False, tm, tn, tk, extras=(), i_outer=True,
        b3=False, o3=False, rider=None, bsum=False, epi_rows=0):
    a0, b0 = As[0], Bs[0]
    M, K = (a0.shape[1], a0.shape[0]) if ta else a0.shape
    if b3:
        S, rows, cs = b0.shape
        N = rows if tb else S * cs
        assert K == (S * cs if tb else rows)
        tn, tk = (tn, cs) if tb else (cs, tk)
    else:
        N = b0.shape[0] if tb else b0.shape[1]
    tm, tn, tk = _fit(M, tm), _fit(N, tn), _fit(K, tk)
    assert M % tm == 0 and N % tn == 0 and K % tk == 0, (name, M, N, K, tm, tn, tk)
    gi, gj, gk = M // tm, N // tn, K // tk
    grid = (gi, gj, gk) if i_outer else (gj, gi, gk)

    def ij(g0, g1):
        return (g0, g1) if i_outer else (g1, g0)

    def amap(g0, g1, k):
        i, _ = ij(g0, g1)
        return (k, i) if ta else (i, k)

    def bmap(g0, g1, k):
        _, j = ij(g0, g1)
        return (j, k) if tb else (k, j)

    def bmap3(g0, g1, k):
        _, j = ij(g0, g1)
        return (k, j, 0) if tb else (j, k, 0)

    in_specs = [pl.BlockSpec((tk, tm) if ta else (tm, tk), amap) for _ in As]
    if b3:
        in_specs += [pl.BlockSpec((None, tn, tk) if tb else (None, tk, tn), bmap3) for _ in Bs]
    else:
        in_specs += [pl.BlockSpec((tn, tk) if tb else (tk, tn), bmap) for _ in Bs]
    args = list(As) + list(Bs)
    for arr, kind, coloff in extras:
        assert coloff % tn == 0
        off = coloff // tn
        if kind == 'mn':
            in_specs.append(pl.BlockSpec((tm, tn), lambda g0, g1, k, off=off: (ij(g0, g1)[0], ij(g0, g1)[1] + off)))
        else:
            in_specs.append(pl.BlockSpec((1, tn), lambda g0, g1, k, off=off: (0, ij(g0, g1)[1] + off)))
        args.append(arr)
    if o3:
        out_shape = [jax.ShapeDtypeStruct((gj, M, tn), dt) for dt in out_dtypes]
        out_specs = [pl.BlockSpec((None, tm, tn), lambda g0, g1, k: (ij(g0, g1)[1], ij(g0, g1)[0], 0))
                     for _ in out_dtypes]
    else:
        cols = [isinstance(dt, tuple) for dt in out_dtypes]
        assert gj == 1 or not any(cols)
        out_shape = [jax.ShapeDtypeStruct((M, 1), dt[1]) if c else jax.ShapeDtypeStruct((M, N), dt)
                     for c, dt in zip(cols, out_dtypes)]
        out_specs = [pl.BlockSpec((tm, 1), lambda g0, g1, k: (ij(g0, g1)[0], 0)) if c
                     else pl.BlockSpec((tm, tn), lambda g0, g1, k: ij(g0, g1)) for c in cols]
    if bsum:
        assert gi == 1 and not tb and not b3
        out_shape.append(jax.ShapeDtypeStruct((1, N), F32))
        out_specs.append(pl.BlockSpec((1, tn), lambda g0, g1, k: (0, ij(g0, g1)[1])))
    n_a, n_b, n_e, n_o = len(As), len(Bs), len(extras), len(out_shape)
    n_p = len(prods) if gk > 1 else 0
    scratch = [pltpu.VMEM((tm, tn), F32) for _ in range(n_p)]
    if rider is not None:
        in_specs, out_specs = in_specs + [ANY] * len(rider.ins), out_specs + [ANY] * len(rider.out_shape)
        args, out_shape, scratch = args + rider.ins, out_shape + rider.out_shape, scratch + rider.scratch
    n_in, n_out = len(args), len(out_shape)

    def body(*refs):
        a_refs = refs[:n_a]
        b_refs = refs[n_a:n_a + n_b]
        e_refs = refs[n_a + n_b:n_a + n_b + n_e]
        o_refs = refs[n_in:n_in + n_o]
        acc_refs = refs[n_in + n_out:n_in + n_out + n_p]
        k = pl.program_id(2)
        if rider is not None:
            step = (pl.program_id(0) * grid[1] + pl.program_id(1)) * gk + k
            ride = (step, grid[0] * grid[1] * gk, refs[n_a + n_b + n_e:n_in], refs[n_in + n_o:n_in + n_out],
                    refs[n_in + n_out + n_p:])
            rider.begin(*ride)

        def finish(accs, rows=slice(None)):
            ex = [(e[rows, :] if kind == 'mn' else e[...]).astype(F32) for e, (_, kind, _) in zip(e_refs, extras)]
            for o, r in zip(o_refs, epi(accs, *ex)):
                o[rows, :] = r.astype(o.dtype)

        if bsum:
            @pl.when(k == 0)
            def _():
                o_refs[-1][...] = jnp.zeros_like(o_refs[-1])

            o_refs[-1][...] += _colsum(b_refs[0][...].astype(F32))

        if gk == 1:
            sub = tm if ta or not epi_rows else _fit(tm, epi_rows)
            for r0 in range(0, tm, sub):
                rows = slice(None) if ta else slice(r0, r0 + sub)
                finish([functools.reduce(jnp.add, [_dot(a_refs[ai][...] if ta else a_refs[ai][rows, :],
                                                        b_refs[bi][...], ta, tb) for ai, bi in terms])
                        for terms in prods], rows)
        else:
            @pl.when(k == 0)
            def _():
                for acc in acc_refs:
                    acc[...] = jnp.zeros_like(acc)

            for p, terms in enumerate(prods):
                for ai, bi in terms:
                    acc_refs[p][...] += _dot(a_refs[ai][...], b_refs[bi][...], ta, tb)

            @pl.when(k == gk - 1)
            def _():
                finish([acc[...] for acc in acc_refs])

        if rider is not None:
            rider.end(*ride)

    aliases = {} if rider is None else {n_a + n_b + n_e + p: n_o + o for p, o in rider.aliases.items()}
    res = pl.pallas_call(
        body, name=name, grid=grid, in_specs=in_specs, out_specs=out_specs, out_shape=out_shape,
        scratch_shapes=scratch, input_output_aliases=aliases,
        compiler_params=_params(("arbitrary", "arbitrary", "arbitrary")),
    )(*args)
    if rider is not None:
        rider.results = res[n_o:]
    return res[:n_o]


def _rows(name, fn, ins, outs, *, T, tb, rider=None):
    tb = min(tb, T)
    assert T % tb == 0
    in_specs, args = [], []
    for arr, kind, width, cb in ins:
        if kind == 'r':
            in_specs.append(pl.BlockSpec((tb, width), lambda i, _, cb=cb: (i, cb)))
        else:
            in_specs.append(pl.BlockSpec((1, width), lambda i, _, cb=cb: (0, cb)))
        args.append(arr)
    out_shape, out_specs = [], []
    for kind, width, dtype in outs:
        if kind == 'r':
            out_shape.append(jax.ShapeDtypeStruct((T, width), dtype))
            out_specs.append(pl.BlockSpec((tb, width), lambda i, _: (i, 0)))
        elif kind == 'c':
            out_shape.append(jax.ShapeDtypeStruct((T, 1), dtype))
            out_specs.append(pl.BlockSpec((tb, 1), lambda i, _: (i, 0)))
        else:
            out_shape.append(jax.ShapeDtypeStruct((1, width), F32))
            out_specs.append(pl.BlockSpec((1, width), lambda i, _: (0, 0)))
    n_in = len(ins)

    def body(*refs):
        i = pl.program_id(0)
        vals = fn(*[r[...] for r in refs[:n_in]])
        for (kind, _, _), o, v in zip(outs, refs[n_in:], vals):
            if kind == 'a':
                @pl.when(i == 0)
                def _(o=o):
                    o[...] = jnp.zeros_like(o)

                o[...] += v
            else:
                o[...] = v.astype(o.dtype)

    return _hosted_call(body, rider, name=name, grid=(T // tb, 1), in_specs=in_specs, out_specs=out_specs,
                        out_shape=out_shape, scratch=[], args=args)


def _colsum(v):
    return jnp.sum(v, axis=0, keepdims=True)


def _ln_stats(z):
    mu = jnp.mean(z, axis=-1, keepdims=True)
    d = z - mu
    var = jnp.mean(d * d, axis=-1, keepdims=True)
    rstd = lax.rsqrt(var + LN_EPS)
    return d * rstd, rstd


def _ln_bwd_math(dy, xhat, rstd, g):
    dxh = dy * g
    m1 = jnp.mean(dxh, axis=-1, keepdims=True)
    m2 = jnp.mean(dxh * xhat, axis=-1, keepdims=True)
    return rstd * (dxh - m1 - xhat * m2)


def _ln_fwd(name, z, g, b, T, D, rider=None):
    def fn(z, g, b):
        xhat, rstd = _ln_stats(z)
        y = xhat * g + b
        return [y, y, xhat, rstd]

    return _rows(name, fn, [(z, 'r', D, 0), (g, 'v', D, 0), (b, 'v', D, 0)],
                 [('r', D, F32), ('r', D, BF16), ('r', D, F32), ('c', 1, F32)], T=T, tb=512, rider=rider)


def _ln_bwd(name, dy, xhat, rstd, g, scale, T, D, rider=None):
    def fn(dy, xhat, rstd, g):
        dz = _ln_bwd_math(dy, xhat, rstd, g)
        return [dz, dz * scale, _colsum(dy * xhat), _colsum(dy)]

    return _rows(name, fn, [(dy, 'r', D, 0), (xhat, 'r', D, 0), (rstd, 'r', 1, 0), (g, 'v', D, 0)],
                 [('r', D, F32), ('r', D, BF16), ('a', D, F32), ('a', D, F32)], T=T, tb=512, rider=rider)


def _ln_loss_bwd(name, z, g, b, tgt, T, D):
    def fn(z, g, b, tgt):
        xhat, rstd = _ln_stats(z)
        err = xhat * g + b - tgt
        row_loss = 0.5 * jnp.mean(err * err, axis=-1, keepdims=True)
        loss = jnp.broadcast_to(jnp.sum(row_loss, axis=0, keepdims=True), (1, 128))
        dy = err * (1.0 / D)
        dz = _ln_bwd_math(dy, xhat, rstd, g)
        return [dz, dz * 0.5, _colsum(dy * xhat), _colsum(dy), loss]

    return _rows(name, fn, [(z, 'r', D, 0), (g, 'v', D, 0), (b, 'v', D, 0), (tgt, 'r', D, 0)],
                 [('r', D, F32), ('r', D, BF16), ('a', D, F32), ('a', D, F32), ('a', 128, F32)], T=T, tb=512)


class _Net:
    def __init__(self, comm, G):
        self.comm, self.G = comm, G

    def gather(self, names, part=None):
        return self.comm.gather(names, part) if self.comm else None

    def exchange(self, names, part=None):
        return self.comm.exchange(names, self.G, part) if self.comm else None

    def to_sibling(self, names):
        return self.comm.to_sibling(names, self.G) if self.comm else None

    def pairsum(self, names):
        if self.comm:
            self.comm.pairsum(names)

    def to_owner(self, names, part=None):
        return self.comm.to_owner(names, part) if self.comm else None

    def done(self, rider):
        if rider is not None:
            for one in getattr(rider, 'riders', [rider]):
                self.comm.collect(one)

    def both(self, *riders):
        return _Riders(riders) if self.comm else None


def _ffn_fwd(tag, xb, x, W, names, net, rider=None, rider_down=None):
    def epi_gu(accs):
        a, b = accs
        return [a, b, a * _sigmoid(a) * b]

    ng, nu, nd = names
    a, b, s = _mm(tag + "_gate_up", [xb], [W[ng], W[nu]], [[(0, 0)], [(0, 1)]], epi_gu, [BF16, BF16, BF16],
                  tm=1024, tn=1408, tk=1024, rider=rider, epi_rows=256)
    net.done(rider)

    def epi_down(accs, xres):
        return [ALPHA * xres + 0.5 * accs[0]]

    rider_down = rider_down() if rider_down else None
    (z,) = _mm(tag + "_down", [s], [W[nd]], [[(0, 0)]], epi_down, [F32], tm=1024, tn=1024, tk=1408,
               extras=[(x, 'mn', 0)], rider=rider_down)
    net.done(rider_down)
    return a, b, s, z


def _ffn_bwd(tag, dzh, dz, xb, a, b, s, W, names, gdt, G, net, ride, pre=(None, None)):
    ng, nu, nd = names

    def epi_ds(accs, a, b):
        ds = accs[0]
        sg = _sigmoid(a)
        return [ds * b * _dsilu(a, sg), ds * a * sg]

    rider = pre[0]() if pre[0] else None
    da, db = _mm(tag + "_ds", [dzh], [W[nd]], [[(0, 0)]], epi_ds, [BF16, BF16], tb=True, tm=1024, tn=1408, tk=1024, epi_rows=256,
                 extras=[(a, 'mn', 0), (b, 'mn', 0)], rider=rider)
    net.done(rider)
    ident = lambda accs: accs
    rider = pre[1]() if pre[1] else None
    (G[nd],) = _mm(tag + "_dwd", [s], [dzh], [[(0, 0)]], ident, [gdt], ta=True, tm=1408, tn=1024, tk=1024,
                   rider=rider)
    net.done(rider)
    if ride == 'all':
        rider = net.to_sibling([nd])
        (G[ng],) = _mm(tag + "_dwg", [xb], [da], [[(0, 0)]], ident, [gdt], ta=True, tm=1024, tn=1408, tk=1024,
                       rider=rider)
        net.done(rider)
        net.pairsum([nd])
        rider = net.to_owner([nd])
        (G[nu],) = _mm(tag + "_dwu", [xb], [db], [[(0, 0)]], ident, [gdt], ta=True, tm=1024, tn=1408, tk=1024,
                       rider=rider)
        net.done(rider)
        rider = net.to_sibling([ng, nu])
        if rider is not None:
            _run_rider(tag + "_to_sibling", rider)
            net.done(rider)
        net.pairsum([ng, nu])
        rider = net.to_owner([ng, nu])
    else:
        rider = net.to_sibling([nd]) if ride else None
        G[ng], G[nu] = _mm(tag + "_dwgu", [xb], [da, db], [[(0, 0)], [(0, 1)]], ident, [gdt, gdt], ta=True,
                           tm=1024, tn=1408, tk=1024, rider=rider)
        net.done(rider)
        if ride:
            net.pairsum([nd])
        rider = net.to_owner([nd]) if ride else None

    def epi_dx(accs, dzres):
        return [ALPHA * dzres + accs[0]]

    (dx,) = _mm(tag + "_dx", [da, db], [W[ng], W[nu]], [[(0, 0), (1, 1)]], epi_dx, [F32], tb=True,
                tm=1024, tn=1024, tk=1408, extras=[(dz, 'mn', 0)], rider=rider)
    net.done(rider)
    return dx


def _ret_tables(H, T):
    C = RET_CHUNK
    log_g = jnp.log(1.0 - jnp.exp2(-5.0 - jnp.arange(H, dtype=F32)))
    idx = jnp.arange(C, dtype=F32)
    diff = idx[:, None] - idx[None, :]
    dm = jnp.where(diff[None] >= 0, jnp.exp(jnp.maximum(diff, 0.0)[None] * log_g[:, None, None]), 0.0)
    xi = jnp.broadcast_to(jnp.exp((idx[None, :] + 1.0) * log_g[:, None])[:, :, None], (H, C, RET_DV))
    zeta = jnp.broadcast_to(jnp.exp((C - 1.0 - idx)[None, :] * log_g[:, None])[:, :, None], (H, C, RET_DK))
    gc = jnp.broadcast_to(jnp.exp(C * log_g)[:, None, None], (H, 1, RET_DV))
    half = RET_DK // 2
    freqs = ROPE_BASE ** (-jnp.arange(half, dtype=F32) / half)
    ang = jnp.arange(T, dtype=F32)[:, None] * freqs[None, :]
    cos, sin = jnp.cos(ang), jnp.sin(ang)
    cosf = jnp.concatenate([cos, cos], axis=1)
    sins = jnp.concatenate([-sin, sin], axis=1)
    return dm, xi, zeta, gc, cosf, sins


def _rot(x, cosf, sins):
    return x * cosf + pltpu.roll(x, RET_DK // 2, 1) * sins


def _rot_bwd(dy, cosf, sins):
    return dy * cosf + pltpu.roll(dy * sins, RET_DK // 2, 1)


RET_HB = 8


def _ret_specs(H, HB, rev, NC):
    C, G = RET_CHUNK, H // HB
    nn = (lambda n: NC - 1 - n) if rev else (lambda n: n)
    return [
        pl.BlockSpec((C, HB * RET_DK), lambda h, n: (nn(n), h)),
        pl.BlockSpec((C, HB * RET_DK), lambda h, n: (nn(n), G + h)),
        pl.BlockSpec((C, HB * RET_DV), lambda h, n: (nn(n), G + h)),
        pl.BlockSpec((C, HB * RET_DV), lambda h, n: (nn(n), 2 * G + h)),
        pl.BlockSpec((C, RET_DK), lambda h, n: (nn(n), 0)),
        pl.BlockSpec((C, RET_DK), lambda h, n: (nn(n), 0)),
        pl.BlockSpec((1, HB * RET_DV), lambda h, n: (0, h)),
        pl.BlockSpec((HB, C, C), lambda h, n: (h, 0, 0)),
        pl.BlockSpec((HB, C, RET_DV), lambda h, n: (h, 0, 0)),
        pl.BlockSpec((HB, C, RET_DK), lambda h, n: (h, 0, 0)),
        pl.BlockSpec((HB, 1, RET_DV), lambda h, n: (h, 0, 0)),
    ]


def _ret_fwd(proj, gn_g, tabs, H, T, rider=None):
    C, NC = RET_CHUNK, T // RET_CHUNK
    HB = min(RET_HB, H)
    dm, xi, zeta, gc, cosf, sins = tabs
    scale = RET_DK ** -0.5

    def body(q_ref, k_ref, v_ref, g_ref, cos_ref, sin_ref, gn_ref, dm_ref, xi_ref, zt_ref, gc_ref,
             r_ref, ri_ref, st_ref, state):
        @pl.when(pl.program_id(1) == 0)
        def _():
            state[...] = jnp.zeros_like(state)

        cs, sn = cos_ref[...], sin_ref[...]
        hs = range(HB)
        qk = [slice(h * RET_DK, (h + 1) * RET_DK) for h in hs]
        vv = [slice(h * RET_DV, (h + 1) * RET_DV) for h in hs]
        kr = [_rot(k_ref[:, qk[h]].astype(F32), cs, sn) for h in hs]
        qb = [(_rot(q_ref[:, qk[h]].astype(F32), cs, sn) * scale).astype(BF16) for h in hs]
        kb = [kr[h].astype(BF16) for h in hs]
        kzb = [(kr[h] * zt_ref[h]).astype(BF16) for h in hs]
        vb = [v_ref[:, vv[h]].astype(BF16) for h in hs]
        st = [state[h] for h in hs]
        stb = [st[h].astype(BF16) for h in hs]
        sb = [(_dot(qb[h], kb[h], tb=True) * dm_ref[h]).astype(BF16) for h in hs]
        cross = [_dot(qb[h], stb[h]) for h in hs]
        kv = [_dot(kzb[h], vb[h], ta=True) for h in hs]
        intra = [_dot(sb[h], vb[h]) for h in hs]
        for h in hs:
            st_ref[h] = stb[h]
            state[h] = gc_ref[h] * st[h] + kv[h]
        for h in hs:
            r = intra[h] + cross[h] * xi_ref[h]
            rhat, _ = _ln_stats(r)
            g = g_ref[:, vv[h]].astype(F32)
            r_ref[:, vv[h]] = r
            ri_ref[:, vv[h]] = (g * _sigmoid(g) * (rhat * gn_ref[:, vv[h]])).astype(BF16)

    VW = H * RET_DV
    return _hosted_call(
        body, rider, name="ret_fwd", grid=(H // HB, NC), in_specs=_ret_specs(H, HB, False, NC),
        out_specs=[pl.BlockSpec((C, HB * RET_DV), lambda h, n: (n, h)),
                   pl.BlockSpec((C, HB * RET_DV), lambda h, n: (n, h)),
                   pl.BlockSpec((HB, None, RET_DK, RET_DV), lambda h, n: (h, n, 0, 0))],
        out_shape=[jax.ShapeDtypeStruct((T, VW), F32), jax.ShapeDtypeStruct((T, VW), BF16),
                   jax.ShapeDtypeStruct((H, NC, RET_DK, RET_DV), BF16)],
        scratch=[pltpu.VMEM((HB, RET_DK, RET_DV), F32)],
        args=[proj, proj, proj, proj, cosf, sins, gn_g, dm, xi, zeta, gc])


def _hosted_call(body, rider, *, name, grid, in_specs, out_specs, out_shape, scratch, args):
    n_in, n_out, n_scr = len(args), len(out_shape), len(scratch)
    if rider is None:
        hosted = body
    else:
        n_ri, n_ro = len(rider.ins), len(rider.out_shape)
        in_specs, out_specs = in_specs + [ANY] * n_ri, out_specs + [ANY] * n_ro
        args, out_shape, scratch = args + rider.ins, out_shape + rider.out_shape, scratch + rider.scratch

        def hosted(*refs):
            o0, s0 = n_in + n_ri, n_in + n_ri + n_out + n_ro
            step = pl.program_id(0) * grid[1] + pl.program_id(1)
            ride = (step, grid[0] * grid[1], refs[n_in:o0], refs[o0 + n_out:s0], refs[s0 + n_scr:])
            rider.begin(*ride)
            body(*refs[:n_in], *refs[o0:o0 + n_out], *refs[s0:s0 + n_scr])
            rider.end(*ride)

    aliases = {} if rider is None else {n_in + p: n_out + o for p, o in rider.aliases.items()}
    res = pl.pallas_call(
        hosted, name=name, grid=grid, in_specs=in_specs, out_specs=out_specs, out_shape=out_shape,
        scratch_shapes=scratch, input_output_aliases=aliases, compiler_params=_params(("arbitrary", "arbitrary")),
    )(*args)
    if rider is not None:
        rider.results = res[n_out:]
    return res[:n_out]


def _ret_bwd(dri, r, states, proj, gn_g, tabs, H, T, in_w, rider=None):
    C, NC = RET_CHUNK, T // RET_CHUNK
    HB = min(RET_HB, H)
    dm, xi, zeta, gc, cosf, sins = tabs
    scale = RET_DK ** -0.5

    def body(q_ref, k_ref, v_ref, g_ref, cos_ref, sin_ref, gn_ref, dm_ref, xi_ref, zt_ref, gc_ref,
             dri_ref, r_ref, st_ref, dp_ref, dgn_ref, dstate):
        @pl.when(pl.program_id(1) == 0)
        def _():
            dstate[...] = jnp.zeros_like(dstate)
            dgn_ref[...] = jnp.zeros_like(dgn_ref)

        cs, sn = cos_ref[...], sin_ref[...]
        hs = range(HB)
        qk = [slice(h * RET_DK, (h + 1) * RET_DK) for h in hs]
        vv = [slice(h * RET_DV, (h + 1) * RET_DV) for h in hs]
        qr = [_rot(q_ref[:, qk[h]].astype(F32), cs, sn) * scale for h in hs]
        kr = [_rot(k_ref[:, qk[h]].astype(F32), cs, sn) for h in hs]
        qb = [qr[h].astype(BF16) for h in hs]
        kb = [kr[h].astype(BF16) for h in hs]
        vb = [v_ref[:, vv[h]].astype(BF16) for h in hs]
        qxb = [(qr[h] * xi_ref[h, :, :RET_DK]).astype(BF16) for h in hs]
        kzb = [(kr[h] * zt_ref[h]).astype(BF16) for h in hs]
        drb = []
        for h in hs:
            rhat, rstd = _ln_stats(r_ref[:, vv[h]])
            g, gn, dpre = g_ref[:, vv[h]].astype(F32), gn_ref[:, vv[h]], dri_ref[:, vv[h]]
            sg = _sigmoid(g)
            dp_ref[:, 2 * QW + VW + h * RET_DV:2 * QW + VW + (h + 1) * RET_DV] = (
                dpre * (rhat * gn) * _dsilu(g, sg)).astype(BF16)
            drn = dpre * (g * sg)
            dgn_ref[:, vv[h]] += _colsum(drn * rhat)
            drb.append(_ln_bwd_math(drn, rhat, rstd, gn).astype(BF16))
        ds1 = [dstate[h] for h in hs]
        ds1b = [ds1[h].astype(BF16) for h in hs]
        sb = [(_dot(qb[h], kb[h], tb=True) * dm_ref[h]).astype(BF16) for h in hs]
        dsb = [(_dot(drb[h], vb[h], tb=True) * dm_ref[h]).astype(BF16) for h in hs]
        dq_x = [_dot(drb[h], st_ref[h], tb=True) for h in hs]
        dk_x = [_dot(vb[h], ds1b[h], tb=True) for h in hs]
        dv_x = [_dot(kzb[h], ds1b[h]) for h in hs]
        dst = [_dot(qxb[h], drb[h], ta=True) for h in hs]
        for h in hs:
            dstate[h] = gc_ref[h] * ds1[h] + dst[h]
        dv_i = [_dot(sb[h], drb[h], ta=True) for h in hs]
        dq_i = [_dot(dsb[h], kb[h]) for h in hs]
        dk_i = [_dot(dsb[h], qb[h], ta=True) for h in hs]
        for h in hs:
            dp_ref[:, 2 * QW + h * RET_DV:2 * QW + (h + 1) * RET_DV] = (dv_i[h] + dv_x[h]).astype(BF16)
            dq = dq_i[h] + dq_x[h] * xi_ref[h, :, :RET_DK]
            dk = dk_i[h] + dk_x[h] * zt_ref[h]
            dp_ref[:, qk[h]] = _rot_bwd(dq * scale, cs, sn).astype(BF16)
            dp_ref[:, QW + h * RET_DK:QW + (h + 1) * RET_DK] = _rot_bwd(dk, cs, sn).astype(BF16)

    VW, QW = H * RET_DV, H * RET_DK
    rv = lambda n: NC - 1 - n
    in_specs = _ret_specs(H, HB, True, NC) + [
        pl.BlockSpec((C, HB * RET_DV), lambda h, n: (rv(n), h)),
        pl.BlockSpec((C, HB * RET_DV), lambda h, n: (rv(n), h)),
        pl.BlockSpec((HB, None, RET_DK, RET_DV), lambda h, n: (h, rv(n), 0, 0)),
    ]
    assert HB == H
    return _hosted_call(
        body, rider, name="ret_bwd", grid=(1, NC), in_specs=in_specs,
        out_specs=[pl.BlockSpec((C, 2 * QW + 2 * VW), lambda h, n: (rv(n), 0)),
                   pl.BlockSpec((1, VW), lambda h, n: (0, 0))],
        out_shape=[jax.ShapeDtypeStruct((T, in_w), BF16), jax.ShapeDtypeStruct((1, VW), F32)],
        scratch=[pltpu.VMEM((HB, RET_DK, RET_DV), F32)],
        args=[proj, proj, proj, proj, cosf, sins, gn_g, dm, xi, zeta, gc, dri, r, states])


CONV_CW = 128
CONV_TB = 512


def _conv_fwd(proj, kpad, bias, off_a, CC, T, rider=None):
    tb, cw = min(CONV_TB, T), CONV_CW
    hb = tb // HALO
    ca, cb = off_a // cw, (off_a + CC) // cw

    def body(a_ref, b_ref, ap_ref, bp_ref, k_ref, bias_ref, u1_ref, win):
        i = pl.program_id(0)
        keep = (i > 0).astype(F32)
        win[0:HALO, :] = ap_ref[...].astype(F32) * _sigmoid(bp_ref[...].astype(F32)) * keep
        win[HALO:, :] = a_ref[...].astype(F32) * _sigmoid(b_ref[...].astype(F32))
        acc = jnp.broadcast_to(bias_ref[...], (tb, cw))
        for w in range(CONV_WIDTH):
            acc = acc + k_ref[w:w + 1, :] * win[pl.ds(HALO - (CONV_WIDTH - 1) + w, tb), :]
        u1_ref[...] = acc

    prev = lambda i: jnp.maximum(i * hb - 1, 0)
    (u1,) = _hosted_call(
        body, rider, name="conv_fwd", grid=(T // tb, CC // cw),
        in_specs=[pl.BlockSpec((tb, cw), lambda i, c: (i, ca + c)),
                  pl.BlockSpec((tb, cw), lambda i, c: (i, cb + c)),
                  pl.BlockSpec((HALO, cw), lambda i, c: (prev(i), ca + c)),
                  pl.BlockSpec((HALO, cw), lambda i, c: (prev(i), cb + c)),
                  pl.BlockSpec((HALO, cw), lambda i, c: (0, c)),
                  pl.BlockSpec((1, cw), lambda i, c: (0, c))],
        out_specs=[pl.BlockSpec((tb, cw), lambda i, c: (i, c))],
        out_shape=[jax.ShapeDtypeStruct((T, CC), F32)],
        scratch=[pltpu.VMEM((tb + HALO, cw), F32)],
        args=[proj, proj, proj, proj, kpad, bias])
    return u1


def _conv_bwd(du1, proj, kpad, off_a, CC, T, rider=None):
    tb, cw = min(CONV_TB, T), CONV_CW
    hb = tb // HALO
    nt = T // tb
    ca, cb = off_a // cw, (off_a + CC) // cw

    def body(d_ref, dn_ref, a_ref, b_ref, ap_ref, bp_ref, k_ref, da_ref, db_ref, dk_ref, winu, wind):
        i = pl.program_id(1)
        a, b = a_ref[...].astype(F32), b_ref[...].astype(F32)
        sgb = _sigmoid(b)
        winu[0:HALO, :] = ap_ref[...].astype(F32) * _sigmoid(bp_ref[...].astype(F32)) * (i > 0).astype(F32)
        winu[HALO:, :] = a * sgb
        d = d_ref[...]
        wind[0:tb, :] = d
        wind[tb:, :] = dn_ref[...] * (i < nt - 1).astype(F32)

        @pl.when(i == 0)
        def _():
            dk_ref[...] = jnp.zeros_like(dk_ref)

        du0 = jnp.zeros((tb, cw), F32)
        for w in range(CONV_WIDTH):
            du0 = du0 + k_ref[w:w + 1, :] * wind[pl.ds(CONV_WIDTH - 1 - w, tb), :]
            dk_ref[w:w + 1, :] += _colsum(winu[pl.ds(HALO - (CONV_WIDTH - 1) + w, tb), :] * d)
        da_ref[...] = (du0 * sgb).astype(BF16)
        db_ref[...] = (du0 * a * sgb * (1.0 - sgb)).astype(BF16)

    prev = lambda i: jnp.maximum(i * hb - 1, 0)
    nxt = lambda i: jnp.minimum((i + 1) * hb, T // HALO - 1)
    return _hosted_call(
        body, rider, name="conv_bwd", grid=(CC // cw, nt),
        in_specs=[pl.BlockSpec((tb, cw), lambda c, i: (i, c)),
                  pl.BlockSpec((HALO, cw), lambda c, i: (nxt(i), c)),
                  pl.BlockSpec((tb, cw), lambda c, i: (i, ca + c)),
                  pl.BlockSpec((tb, cw), lambda c, i: (i, cb + c)),
                  pl.BlockSpec((HALO, cw), lambda c, i: (prev(i), ca + c)),
                  pl.BlockSpec((HALO, cw), lambda c, i: (prev(i), cb + c)),
                  pl.BlockSpec((HALO, cw), lambda c, i: (0, c))],
        out_specs=[pl.BlockSpec((tb, cw), lambda c, i: (i, c)),
                   pl.BlockSpec((tb, cw), lambda c, i: (i, c)),
                   pl.BlockSpec((HALO, cw), lambda c, i: (0, c))],
        out_shape=[jax.ShapeDtypeStruct((T, CC), BF16), jax.ShapeDtypeStruct((T, CC), BF16),
                   jax.ShapeDtypeStruct((HALO, CC), F32)],
        scratch=[pltpu.VMEM((tb + HALO, cw), F32), pltpu.VMEM((tb + HALO, cw), F32)],
        args=[du1, du1, proj, proj, proj, proj, kpad])


FFN1 = ('ffn1_w_gate', 'ffn1_w_up', 'ffn1_w_down')
FFN2 = ('ffn2_w_gate', 'ffn2_w_up', 'ffn2_w_down')


def _local_step(x, tgt, W, P, gdt=BF16, comm=None):
    T, D = x.shape
    G = {}
    net = _Net(comm, G)
    if comm is not None:
        W = comm.W
        first = net.gather(['ffn1_w_gate', 'ffn1_w_up'])
    (xb,) = _rows("x_to_bf16", lambda v: [v], [(x, 'r', D, 0)], [('r', D, BF16)], T=T, tb=512,
                  rider=first if comm is not None else None)
    if comm is not None:
        net.done(first)
    VW = P['ret_gn_g'].shape[1]
    H = VW // RET_DV
    QW = H * RET_DK
    CC = P['conv_b'].shape[1]
    off_glu = 2 * QW + 2 * VW
    off_gate = off_glu + 2 * CC
    ident = lambda accs: accs

    a1, b1, s1, z1 = _ffn_fwd("ffn1", xb, x, W, FFN1, net,
                              rider=net.gather(['ffn1_w_down', 'w_in'], {'w_in': (0, (3 * D) // 8, False)}),
                              rider_down=lambda: net.gather(['w_in'], {'w_in': ((3 * D) // 8, (3 * D) // 8, False)}))
    rider = net.gather(['w_in'], {'w_in': ((3 * D) // 4, D // 4, True)})
    x1, x1b, xh1, rs1 = _ln_fwd("ln1", z1, P['ln1_g'], P['ln1_b'], T, D, rider=rider)
    net.done(rider)

    rest = net.gather(['conv_k', 'w_ret_o', 'w_conv_o', 'w_out', 'ffn2_w_down'])
    (proj,) = _mm("w_in", [x1b], [W['w_in']], [[(0, 0)]], lambda accs, bias: [accs[0] + bias], [F32],
                  tm=2048, tn=0, tk=1024, extras=[(P['b_in'], 'n', 0)], i_outer=False, b3=True, rider=rest)
    net.done(rest)
    tabs = _ret_tables(H, T)
    rider = net.gather(['ffn2_w_gate', 'ffn2_w_up'])
    r, ret_in, states = _ret_fwd(proj, P['ret_gn_g'], tabs, H, T, rider=rider)
    net.done(rider)
    kpad = jnp.pad(W['conv_k'].astype(F32), ((0, HALO - CONV_WIDTH), (0, 0)))
    u1 = _conv_fwd(proj, kpad, P['conv_b'], off_glu, CC, T)

    def conv_ln(u1, g, b):
        xhat, rstd = _ln_stats(u1)
        u2 = xhat * g + b
        return [xhat, rstd, u2 * _sigmoid(u2)]

    xhc, rsc, u3 = _rows("conv_ln", conv_ln, [(u1, 'r', CC, 0), (P['conv_ln_g'], 'v', CC, 0), (P['conv_ln_b'], 'v', CC, 0)],
                         [('r', CC, F32), ('c', 1, F32), ('r', CC, BF16)], T=T, tb=512)
    (ret_out,) = _mm("ret_o", [ret_in], [W['w_ret_o']], [[(0, 0)]], ident, [F32], tm=1024, tn=1024, tk=2048)

    def epi_merge(accs, ret_out, gr, gc):
        conv_out = accs[0]
        return [conv_out, _sigmoid(gr) * ret_out + _sigmoid(gc) * conv_out]

    conv_out, merged = _mm("conv_o_merge", [u3], [W['w_conv_o']], [[(0, 0)]], epi_merge, [F32, BF16],
                           tm=512, tn=D, tk=1024, epi_rows=256,
                           extras=[(ret_out, 'mn', 0), (proj, 'mn', off_gate), (proj, 'mn', off_gate + D)])
    def epi_out_ln(accs, xr, g, b):
        xhat, rstd = _ln_stats(ALPHA * xr + accs[0])
        y = xhat * g + b
        return [y, y, xhat, rstd]

    x2, x2b, xh2, rs2 = _mm("w_out_ln2", [merged], [W['w_out']], [[(0, 0)]], epi_out_ln, [F32, BF16, F32, ('c', F32)],
                            tm=1024, tn=D, tk=1024, epi_rows=256,
                            extras=[(x1, 'mn', 0), (P['ln2_g'], 'n', 0), (P['ln2_b'], 'n', 0)])
    a2, b2, s2, z3 = _ffn_fwd("ffn2", x2b, x2, W, FFN2, net)
    dz3, dz3h, g_ln3_g, g_ln3_b, loss = _ln_loss_bwd("ln3_loss", z3, P['ln3_g'], P['ln3_b'], tgt, T, D)

    S = {'ln3_g': g_ln3_g, 'ln3_b': g_ln3_b}
    dy2 = _ffn_bwd("ffn2b", dz3h, dz3, x2b, a2, b2, s2, W, FFN2, gdt, G, net, 'down')
    rider = net.to_sibling(['ffn2_w_gate', 'ffn2_w_up'])
    dz2, dz2b, S['ln2_g'], S['ln2_b'] = _ln_bwd("ln2b", dy2, xh2, rs2, P['ln2_g'], 1.0, T, D, rider=rider)
    net.done(rider)
    net.pairsum(['ffn2_w_gate', 'ffn2_w_up'])

    (G['w_out'],) = _mm("d_w_out", [merged], [dz2b], [[(0, 0)]], ident, [gdt], ta=True, tm=1024, tn=1024, tk=1024)

    def epi_dmerge(accs, ret_out, conv_out, gr, gc):
        dm_ = accs[0]
        sr, sc = _sigmoid(gr), _sigmoid(gc)
        return [dm_ * sr, dm_ * sc, dm_ * ret_out * sr * (1.0 - sr), dm_ * conv_out * sc * (1.0 - sc)]

    rider = net.to_sibling(['w_out'])
    dret_out, dconv_out, dgate_r, dgate_c = _mm(
        "d_merge", [dz2b], [W['w_out']], [[(0, 0)]], epi_dmerge, [BF16, BF16, BF16, BF16], tb=True,
        tm=512, tn=D, tk=1024, epi_rows=256, rider=rider,
        extras=[(ret_out, 'mn', 0), (conv_out, 'mn', 0), (proj, 'mn', off_gate), (proj, 'mn', off_gate + D)])
    net.done(rider)
    (G['w_ret_o'],) = _mm("d_w_ret_o", [ret_in], [dret_out], [[(0, 0)]], ident, [gdt], ta=True, tm=1024, tn=1024, tk=1024)
    (G['w_conv_o'],) = _mm("d_w_conv_o", [u3], [dconv_out], [[(0, 0)]], ident, [gdt], ta=True, tm=1024, tn=1024, tk=1024)
    rider = net.to_sibling(['w_ret_o', 'w_conv_o'])
    (dri,) = _mm("d_ret_in", [dret_out], [W['w_ret_o']], [[(0, 0)]], ident, [F32], tb=True, tm=1024, tn=1024, tk=1024,
                 rider=rider)
    net.done(rider)
    net.pairsum(['w_out', 'w_ret_o', 'w_conv_o'])
    rider = net.to_owner(['ffn2_w_gate', 'ffn2_w_up'])
    dproj, S['ret_gn_g'] = _ret_bwd(dri, r, states, proj, P['ret_gn_g'], tabs, H, T, proj.shape[1], rider=rider)
    net.done(rider)

    def epi_du2(accs, xhat, g, b):
        u2 = xhat * g + b
        return [accs[0] * _dsilu(u2, _sigmoid(u2))]

    (du2,) = _mm("d_u3", [dconv_out], [W['w_conv_o']], [[(0, 0)]], epi_du2, [F32], tb=True, tm=512, tn=CC, tk=1024, epi_rows=256,
                 extras=[(xhc, 'mn', 0), (P['conv_ln_g'], 'n', 0), (P['conv_ln_b'], 'n', 0)])

    def conv_ln_bwd(du2, xhat, rstd, g):
        du1 = _ln_bwd_math(du2, xhat, rstd, g)
        return [du1, _colsum(du2 * xhat), _colsum(du2), _colsum(du1)]

    du1, S['conv_ln_g'], S['conv_ln_b'], S['conv_b'] = _rows(
        "conv_ln_bwd", conv_ln_bwd, [(du2, 'r', CC, 0), (xhc, 'r', CC, 0), (rsc, 'r', 1, 0), (P['conv_ln_g'], 'v', CC, 0)],
        [('r', CC, F32), ('a', CC, F32), ('a', CC, F32), ('a', CC, F32)], T=T, tb=512)
    dglu_a, dglu_b, dkpad = _conv_bwd(du1, proj, kpad, off_glu, CC, T)
    G['conv_k'] = dkpad[:CONV_WIDTH].astype(gdt)

    dproj = lax.dynamic_update_slice(dproj, jnp.concatenate([dglu_a, dglu_b, dgate_r, dgate_c], axis=1), (0, off_glu))
    IN_W = dproj.shape[1]
    rider = net.both(net.exchange(['conv_k']),
                     net.to_owner(['w_out', 'w_ret_o', 'w_conv_o']))
    G['w_in'], S['b_in'] = _mm("d_w_in", [x1b], [dproj], [[(0, 0)]], ident, [gdt], ta=True, o3=True, bsum=True,
                               tm=1024, tn=W['w_in'].shape[2], tk=1024, rider=rider)
    net.done(rider)
    cuts = [0, (3 * D) // 16, (43 * D) // 64, D]
    w_in_rows = [{'w_in': (cuts[i], cuts[i + 1] - cuts[i], i == 2)} for i in range(3)]
    rider = net.to_sibling(['w_in'])
    (dy1,) = _mm("d_x1", [dproj], [W['w_in']], [[(0, 0)]], lambda accs, dzr: [ALPHA * dzr + accs[0]], [F32], tb=True,
                 b3=True, tm=1024, tn=1024, tk=0, extras=[(dz2, 'mn', 0)], rider=rider)
    net.done(rider)
    net.pairsum(['w_in'])
    rider = net.to_owner(['w_in'], w_in_rows[0])
    dz1, dz1h, S['ln1_g'], S['ln1_b'] = _ln_bwd("ln1b", dy1, xh1, rs1, P['ln1_g'], 0.5, T, D, rider=rider)
    net.done(rider)
    grad_x = _ffn_bwd("ffn1b", dz1h, dz1, xb, a1, b1, s1, W, FFN1, gdt, G, net, 'all',
                      pre=(lambda: net.to_owner(['w_in'], w_in_rows[1]), lambda: net.to_owner(['w_in'], w_in_rows[2])))
    return loss, grad_x, G, S


def _coords():
    return lax.axis_index("x"), lax.axis_index("y"), lax.axis_index("c")


def _flip(k, x, y, c):
    return (1 - x if k & 4 else x, 1 - y if k & 2 else y, 1 - c if k & 1 else c)


def _lin(p):
    return 4 * p[0] + 2 * p[1] + p[2]


class _Rider:
    def __init__(self, ins, out_shape, rows=None, fill=None):
        nb = len(ins)
        self.rows = rows or [None] * nb
        fill = fill or [None] * nb
        self.aliases = {nb + i: w for i, w in enumerate(w for w in range(nb) if fill[w] is not None)}
        self.ins = list(ins) + [f for f in fill if f is not None]
        self.out_shape, self.results = list(out_shape), None
        self.scratch = [pltpu.SemaphoreType.DMA((8 * nb,)), pltpu.SemaphoreType.DMA((8 * nb,)),
                        pltpu.SemaphoreType.DMA((nb,))]

    def span(self, w, ref, *slot, half=None):
        rows = self.rows[w]
        if half is not None:
            first, count = rows if rows is not None else (0, self.out_shape[w].shape[1])
            rows = (first + half * (count // 2), count // 2)
        if rows is None:
            return ref.at[slot] if slot else ref
        return ref.at[(*slot, pl.ds(*rows))]

    def begin(self, step, n_steps, ins, outs, sems):
        @pl.when(step == 0)
        def _():
            self.start(ins, outs, sems)

        @pl.when(step == min(n_steps - 1, (5 * n_steps) // 8))
        def _():
            self.relay(ins, outs, sems)

        @pl.when(step == n_steps - 1)
        def _():
            self.mid(ins, outs, sems)

    def end(self, step, n_steps, ins, outs, sems):
        @pl.when(step == n_steps - 1)
        def _():
            self.finish(ins, outs, sems)

    def relay(self, ins, outs, sems):
        pass

    def mid(self, ins, outs, sems):
        pass


class _Riders:
    def __init__(self, riders):
        self.riders = list(riders)
        self.ins = [a for r in self.riders for a in r.ins]
        self.out_shape = [o for r in self.riders for o in r.out_shape]
        self.scratch = [c for r in self.riders for c in r.scratch]
        self.aliases, n_in, n_out = {}, 0, 0
        for r in self.riders:
            self.aliases.update({n_in + p: n_out + o for p, o in r.aliases.items()})
            n_in, n_out = n_in + len(r.ins), n_out + len(r.out_shape)

    def _each(self, ins, outs, sems):
        i = o = c = 0
        for r in self.riders:
            yield r, ins[i:i + len(r.ins)], outs[o:o + len(r.out_shape)], sems[c:c + len(r.scratch)]
            i, o, c = i + len(r.ins), o + len(r.out_shape), c + len(r.scratch)

    def begin(self, step, n_steps, ins, outs, sems):
        for r, i, o, c in self._each(ins, outs, sems):
            r.begin(step, n_steps, i, o, c)

    def end(self, step, n_steps, ins, outs, sems):
        for r, i, o, c in self._each(ins, outs, sems):
            r.end(step, n_steps, i, o, c)

    @property
    def results(self):
        return [x for r in self.riders for x in r.results]

    @results.setter
    def results(self, res):
        for r, _, o, _ in self._each([], list(res), []):
            r.results = o


class _GatherRider(_Rider):
    def __init__(self, blks, rows=None, fill=None):
        super().__init__(blks, [jax.ShapeDtypeStruct((N_DEV,) + b.shape, b.dtype) for b in blks], rows, fill)
        counts = [(r[1] if r is not None else b.shape[0]) for r, b in zip(self.rows, blks)]
        self.halves = [n % 32 == 0 for n in counts]

    def _copies(self, x_refs, out_refs, sems):
        nb = len(self.out_shape)
        send_sems, recv_sems, local_sems = sems
        x, y, c = _coords()
        me, sib = (x, y, c), (x, y, 1 - c)
        xn, yn, dg = _flip(4, x, y, c), _flip(2, x, y, c), _flip(6, x, y, c)
        plans = []
        for w in range(nb):
            own = self.span(w, x_refs[w])

            def copy(k, block, to, src=None, half=None, w=w):
                slot = self.span(w, out_refs[w], _lin(block), half=half)
                return pltpu.make_async_remote_copy(
                    src_ref=slot if src is None else src, dst_ref=slot, send_sem=send_sems.at[k * nb + w],
                    recv_sem=recv_sems.at[k * nb + w], device_id=to, device_id_type=MESH)

            mine = pltpu.make_async_copy(own, self.span(w, out_refs[w], _lin(me)), local_sems.at[w])
            first = [copy(0, me, sib, src=own), copy(1, me, xn, src=own), copy(2, me, yn, src=own)]
            if self.halves[w]:
                relay = [(copy(1, xn, me), [copy(3, xn, yn, half=0), copy(5, xn, sib)]),
                         (copy(2, yn, me), [copy(4, yn, xn, half=1), copy(6, yn, sib)])]
                last = [(copy(3, dg, me, half=0), []), (copy(4, dg, me, half=1), [copy(7, dg, sib)])]
            else:
                first.append(copy(3, me, dg, src=own))
                relay = [(copy(1, xn, me), [copy(5, xn, sib)]), (copy(2, yn, me), [copy(6, yn, sib)])]
                last = [(copy(3, dg, me), [copy(7, dg, sib)])]
            other = lambda p: (p[0], p[1], 1 - c)
            from_sib = [copy(0, sib, me), copy(5, other(xn), me), copy(6, other(yn), me), copy(7, other(dg), me)]
            plans.append((mine, first, relay, last, from_sib))
        return plans

    def start(self, ins, outs, sems):
        for mine, first, _, _, _ in self._copies(ins, outs, sems):
            for cp in [mine] + first:
                cp.start()

    def relay(self, ins, outs, sems):
        for _, _, relay, _, _ in self._copies(ins, outs, sems):
            for arrival, released in relay:
                arrival.wait_recv()
                for cp in released:
                    cp.start()

    def mid(self, ins, outs, sems):
        for _, _, _, last, _ in self._copies(ins, outs, sems):
            for arrival, released in last:
                arrival.wait_recv()
                for cp in released:
                    cp.start()

    def finish(self, ins, outs, sems):
        for mine, first, relay, last, from_sib in self._copies(ins, outs, sems):
            for cp in from_sib:
                cp.wait_recv()
            for cp in first + [cp for _, released in relay + last for cp in released]:
                cp.wait_send()
            mine.wait()


class _ExchangeRider(_Rider):
    def __init__(self, gs, rows=None, fill=None):
        super().__init__(gs, [jax.ShapeDtypeStruct(g.shape, g.dtype) for g in gs], rows, fill)

    def _copies(self, g_refs, out_refs, sems):
        nb = len(self.out_shape)
        send_sems, recv_sems, local_sems = sems
        x, y, c = _coords()
        me = _lin((x, y, c))

        def copy(k, w, landing):
            peer = _flip(k, x, y, c)
            src, dst = (me, _lin(peer)) if landing else (_lin(peer), me)
            return pltpu.make_async_remote_copy(
                src_ref=self.span(w, g_refs[w], src), dst_ref=self.span(w, out_refs[w], dst),
                send_sem=send_sems.at[(k - 1) * nb + w], recv_sem=recv_sems.at[(k - 1) * nb + w],
                device_id=peer, device_id_type=MESH)

        mines = [pltpu.make_async_copy(self.span(w, g_refs[w], me), self.span(w, out_refs[w], me), local_sems.at[w])
                 for w in range(nb)]
        sends = [copy(k, w, False) for w in range(nb) for k in range(1, N_DEV)]
        landings = [copy(k, w, True) for w in range(nb) for k in range(1, N_DEV)]
        return mines, sends, landings

    def start(self, ins, outs, sems):
        mines, sends, _ = self._copies(ins, outs, sems)
        for cp in mines + sends:
            cp.start()

    def finish(self, ins, outs, sems):
        mines, sends, landings = self._copies(ins, outs, sems)
        for cp in landings:
            cp.wait_recv()
        for cp in sends:
            cp.wait_send()
        for mine in mines:
            mine.wait()


class _SiblingRider(_Rider):
    def __init__(self, gs):
        super().__init__(gs, [jax.ShapeDtypeStruct((4,) + g.shape[1:], g.dtype) for g in gs])

    def _copies(self, g_refs, out_refs, sems, landing):
        nb = len(self.out_shape)
        send_sems, recv_sems, _ = sems
        x, y, c = _coords()
        whose = c if landing else 1 - c
        return [pltpu.make_async_remote_copy(
            src_ref=g_refs[w].at[2 * q + whose], dst_ref=out_refs[w].at[q], send_sem=send_sems.at[q * nb + w],
            recv_sem=recv_sems.at[q * nb + w], device_id=(x, y, 1 - c), device_id_type=MESH)
            for w in range(nb) for q in range(4)]

    def start(self, ins, outs, sems):
        for cp in self._copies(ins, outs, sems, False):
            cp.start()

    def finish(self, ins, outs, sems):
        for cp in self._copies(ins, outs, sems, True):
            cp.wait_recv()
        for cp in self._copies(ins, outs, sems, False):
            cp.wait_send()


class _ChipRider(_Rider):
    FLIPS = (4, 2, 6)

    def __init__(self, ps, rows=None, fill=None):
        super().__init__(ps, [jax.ShapeDtypeStruct(p.shape, p.dtype) for p in ps], rows, fill)

    def _copies(self, p_refs, out_refs, sems):
        nb = len(self.out_shape)
        send_sems, recv_sems, local_sems = sems
        x, y, c = _coords()
        my_chip = 2 * x + y

        def copy(j, w, landing):
            peer = _flip(self.FLIPS[j], x, y, c)
            peer_chip = 2 * peer[0] + peer[1]
            src, dst = (my_chip, peer_chip) if landing else (peer_chip, my_chip)
            return pltpu.make_async_remote_copy(
                src_ref=self.span(w, p_refs[w], src), dst_ref=self.span(w, out_refs[w], dst),
                send_sem=send_sems.at[j * nb + w], recv_sem=recv_sems.at[j * nb + w],
                device_id=peer, device_id_type=MESH)

        mines = [pltpu.make_async_copy(self.span(w, p_refs[w], my_chip), self.span(w, out_refs[w], my_chip),
                                       local_sems.at[w]) for w in range(nb)]
        sends = [copy(j, w, False) for w in range(nb) for j in range(3)]
        landings = [copy(j, w, True) for w in range(nb) for j in range(3)]
        return mines, sends, landings

    def start(self, ins, outs, sems):
        mines, sends, _ = self._copies(ins, outs, sems)
        for cp in mines + sends:
            cp.start()

    def finish(self, ins, outs, sems):
        mines, sends, landings = self._copies(ins, outs, sems)
        for cp in landings:
            cp.wait_recv()
        for cp in sends:
            cp.wait_send()
        for mine in mines:
            mine.wait()


def _pairsum(name, g, land):
    _, r, cols = g.shape
    tb = r if r % 16 else _row_tile(r, 16, max(16, (1024 * 1024) // cols))
    core = lax.axis_index("c").astype(jnp.int32).reshape(1)

    def body(core_ref, g_ref, l_ref, o_ref):
        o_ref[...] = (g_ref[...].astype(F32) + l_ref[...].astype(F32)).astype(o_ref.dtype)

    return pl.pallas_call(
        body, name=name, out_shape=jax.ShapeDtypeStruct((4, r, cols), g.dtype),
        grid_spec=pltpu.PrefetchScalarGridSpec(
            num_scalar_prefetch=1, grid=(4, r // tb),
            in_specs=[pl.BlockSpec((None, None, tb, cols), lambda q, i, core_ref: (q, core_ref[0], i, 0)),
                      pl.BlockSpec((None, tb, cols), lambda q, i, core_ref: (q, i, 0))],
            out_specs=pl.BlockSpec((None, tb, cols), lambda q, i, core_ref: (q, i, 0))),
        compiler_params=_params(("arbitrary", "arbitrary")),
    )(core, g.reshape(4, 2, r, cols), land)


def _run_rider(name, rider):
    n_in, n_out = len(rider.ins), len(rider.out_shape)

    def body(*refs):
        ride = (refs[:n_in], refs[n_in:n_in + n_out], refs[n_in + n_out:])
        rider.start(*ride)
        rider.relay(*ride)
        rider.mid(*ride)
        rider.finish(*ride)

    rider.results = pl.pallas_call(
        body, name=name, out_shape=rider.out_shape, in_specs=[ANY] * n_in, out_specs=[ANY] * n_out,
        scratch_shapes=rider.scratch, input_output_aliases=dict(rider.aliases),
        compiler_params=pltpu.CompilerParams(has_side_effects=True),
    )(*rider.ins)
    return rider.results


def _as_matrix(name, g):
    if name == 'w_in':
        return g
    if name in COL_SHARDED:
        return jnp.transpose(g, (1, 0, 2)).reshape(g.shape[1], N_DEV * g.shape[2])
    return g.reshape(N_DEV * g.shape[1], g.shape[2])


def _by_owner(name, g):
    if name == 'w_in':
        return g
    if name in COL_SHARDED:
        return jnp.transpose(g.reshape(g.shape[0], N_DEV, g.shape[1] // N_DEV), (1, 0, 2))
    return g.reshape(N_DEV, g.shape[0] // N_DEV, g.shape[1])


class _Comm:
    def __init__(self, shards):
        self.shards, self.W, self.parts, self.partial, self.sent = shards, {}, {}, {}, {}
        self.from_sibling, self.pairs = {}, {}

    def _ride(self, cls, names, srcs, part, sink):
        part = part or {}
        rider = cls(srcs, rows=[part[n][:2] if n in part else None for n in names],
                    fill=[self.partial.pop((sink, n), None) for n in names])
        rider.names, rider.sink = names, sink
        rider.unfinished = {n for n in names if n in part and not part[n][2]}
        return rider

    def gather(self, names, part=None):
        return self._ride(_GatherRider, names, [self.shards[n] for n in names], part, 'W')

    def exchange(self, names, G, part=None):
        for n in names:
            if n not in self.sent:
                self.sent[n] = _by_owner(n, G[n])
        return self._ride(_ExchangeRider, names, [self.sent[n] for n in names], part, 'parts')

    def to_sibling(self, names, G):
        for n in names:
            self.sent[n] = _by_owner(n, G[n])
        rider = _SiblingRider([self.sent[n] for n in names])
        rider.names, rider.sink, rider.unfinished = names, 'sibling', set()
        return rider

    def pairsum(self, names):
        for n in names:
            self.pairs[n] = _pairsum("pairsum_" + n, self.sent[n], self.from_sibling.pop(n))

    def to_owner(self, names, part=None):
        return self._ride(_ChipRider, names, [self.pairs[n] for n in names], part, 'parts')

    def collect(self, rider):
        for n, res in zip(rider.names, rider.results):
            if n in rider.unfinished:
                self.partial[(rider.sink, n)] = res
            elif rider.sink == 'W':
                self.W[n] = _as_matrix(n, res)
            elif rider.sink == 'sibling':
                self.from_sibling[n] = res
            else:
                self.parts[n] = res


def _adamw_math(p_ref, w_ref, m_ref, v_ref, g_ref, d_ref, nm_ref, nv_ref):
    c1 = 1.0 - ADAM_B1 ** ADAM_STEP
    c2 = 1.0 - ADAM_B2 ** ADAM_STEP
    g = p_ref[0].astype(F32)
    for s in range(1, p_ref.shape[0]):
        g = g + p_ref[s].astype(F32)
    nm = ADAM_B1 * m_ref[...] + (1.0 - ADAM_B1) * g
    nv = ADAM_B2 * v_ref[...] + (1.0 - ADAM_B2) * (g * g)
    g_ref[...] = g
    nm_ref[...] = nm
    nv_ref[...] = nv
    d_ref[...] = -ADAM_LR * ((nm / c1) / (jnp.sqrt(nv / c2) + ADAM_EPS) + ADAM_WD * w_ref[...])


def _adamw_vectors(parts, ws, ms, vs, loss_parts):
    k = len(ws)

    def body(*refs):
        for i in range(k):
            _adamw_math(refs[i], refs[k + i], refs[2 * k + i], refs[3 * k + i], *refs[4 * k + 1 + 4 * i:4 * k + 5 + 4 * i])
        lp, lo = refs[4 * k], refs[8 * k + 1]
        lo[...] = functools.reduce(jnp.add, [lp[s] for s in range(lp.shape[0])])

    return pl.pallas_call(
        body, name="adamw_vectors",
        out_shape=[jax.ShapeDtypeStruct(w.shape, F32) for w in ws for _ in range(4)] + [jax.ShapeDtypeStruct((1, 128), F32)],
        compiler_params=_params(),
    )(*parts, *ws, *ms, *vs, loss_parts)


def _adamw(name, parts, w, m, v, tb):
    n, R, Wd = parts.shape
    assert R % tb == 0
    body = functools.partial(_adamw_math)

    row = pl.BlockSpec((tb, Wd), lambda i: (i, 0))
    return pl.pallas_call(
        body, name=name, grid=(R // tb,),
        in_specs=[pl.BlockSpec((n, tb, Wd), lambda i: (0, i, 0)), row, row, row],
        out_specs=[row, row, row, row], out_shape=[jax.ShapeDtypeStruct((R, Wd), F32)] * 4,
        compiler_params=_params(("arbitrary",)),
    )(parts, w, m, v)


def _row_tile(R, unit, cap):
    best = unit
    for t in range(unit, cap + 1, unit):
        if R % t == 0:
            best = t
    return best


def kernel(x, ffn1_w_gate, ffn1_w_up, ffn1_w_down, ln1_g, ln1_b, w_in, b_in, ret_gn_g, conv_k, conv_b, conv_ln_g, conv_ln_b, w_ret_o, w_conv_o, w_out, ln2_g, ln2_b, ffn2_w_gate, ffn2_w_up, ffn2_w_down, ln3_g, ln3_b, loss_target, m_ffn1_w_gate, m_ffn1_w_up, m_ffn1_w_down, m_ln1_g, m_ln1_b, m_w_in, m_b_in, m_ret_gn_g, m_conv_k, m_conv_b, m_conv_ln_g, m_conv_ln_b, m_w_ret_o, m_w_conv_o, m_w_out, m_ln2_g, m_ln2_b, m_ffn2_w_gate, m_ffn2_w_up, m_ffn2_w_down, m_ln3_g, m_ln3_b, v_ffn1_w_gate, v_ffn1_w_up, v_ffn1_w_down, v_ln1_g, v_ln1_b, v_w_in, v_b_in, v_ret_gn_g, v_conv_k, v_conv_b, v_conv_ln_g, v_conv_ln_b, v_w_ret_o, v_w_conv_o, v_w_out, v_ln2_g, v_ln2_b, v_ffn2_w_gate, v_ffn2_w_up, v_ffn2_w_down, v_ln3_g, v_ln3_b):
    given = dict(locals())
    wts = {n: given[n] for n in WEIGHTS}
    mom = {n: given['m_' + n] for n in WEIGHTS}
    var = {n: given['v_' + n] for n in WEIGHTS}

    def shard2d(a):
        return a.reshape(a.shape[-3] * a.shape[-2] if a.ndim == 4 else a.shape[-2], a.shape[-1])

    comm = _Comm({n: shard2d(wts[n]).astype(BF16) for n in BIG})
    P = {n: wts[n].reshape(1, -1) for n in SMALL}
    loss, grad_x, _, S = _local_step(x[0], loss_target[0], None, P, comm=comm)

    parts = comm.parts
    res = {}
    for n in BIG:
        rows, cols = parts[n].shape[1:]
        tb = rows if rows % 16 else _row_tile(rows, 16, max(16, (256 * 1024) // cols))
        res[n] = _adamw("adamw_" + n, parts[n], shard2d(wts[n]), shard2d(mom[n]), shard2d(var[n]), tb)

    vec_parts = _run_rider("gather_vector_grads", _GatherRider([S[n] for n in SMALL] + [loss]))
    vec = _adamw_vectors(vec_parts[:-1], [P[n] for n in SMALL], [mom[n].reshape(1, -1) for n in SMALL],
                         [var[n].reshape(1, -1) for n in SMALL], vec_parts[-1])
    for i, n in enumerate(SMALL):
        res[n] = vec[4 * i:4 * i + 4]

    outs = [vec[-1][0, 0], grad_x[None]]
    for k in range(4):
        for n in WEIGHTS:
            outs.append(res[n][k].reshape(wts[n].shape))
    return tuple(outs)
```
